```python
import jax, jax.numpy as jnp
from jax import lax
import numpy as np

D_MODEL = 1024
BATCH = 8
SEQ = 8192
DEPTH = 2

N_MIXERS = 2
N_A_LAYERS = (DEPTH + 1) // 2
N_B_LAYERS = DEPTH // 2
CHUNK = 128
GM_WIDTH = 2 * D_MODEL
GM_GROUPS = 8
GM_GROUP_DIM = GM_WIDTH // GM_GROUPS
N_HEADS = 16
HEAD_DIM = D_MODEL // N_HEADS
Q_BLOCK = 128
FFN_DIM = 2 * D_MODEL
CONV_WIDTH = 3
RMS_EPS = 1e-6
LN_EPS = 1e-5
FORGET_BIAS_INIT = 4.0

kernel_name = "hybrid_gmlp_fox_convffn"


def rmsnorm(x, g):
    xf = x.astype(jnp.float32)
    y = xf * lax.rsqrt(jnp.mean(xf * xf, axis=-1, keepdims=True) + RMS_EPS)
    return (y * g.astype(jnp.float32)).astype(x.dtype)


def layernorm(x, g, b):
    xf = x.astype(jnp.float32)
    mu = jnp.mean(xf, axis=-1, keepdims=True)
    xc = xf - mu
    y = xc * lax.rsqrt(jnp.mean(xc * xc, axis=-1, keepdims=True) + LN_EPS)
    return (y * g.astype(jnp.float32) + b.astype(jnp.float32)).astype(x.dtype)


def chunked_spatial_gating_mixer(h, w_in, ln_g, ln_b, w_s, b_s, w_out):
    bsz, seq, _ = h.shape
    z = jax.nn.gelu(h @ w_in)
    u, v = jnp.split(z, 2, axis=-1)
    v = layernorm(v, ln_g, ln_b)
    n_chunks = seq // CHUNK
    v = v.reshape(bsz, n_chunks, CHUNK, GM_GROUPS, GM_GROUP_DIM)
    causal = jnp.tril(jnp.ones((CHUNK, CHUNK), dtype=bool))
    w_causal = jnp.where(causal[None], w_s, jnp.zeros((), w_s.dtype))
    s = jnp.einsum("gts,bnsgc->bntgc", w_causal, v)
    s = s + b_s.T[None, None, :, :, None]
    s = s.reshape(bsz, seq, GM_WIDTH)
    return (u * s) @ w_out


def forgetting_attention_mixer(h, w_qkvf, b_f, w_o):
    bsz, seq, _ = h.shape
    proj = h @ w_qkvf
    q, k, v, f_logit = jnp.split(proj, [D_MODEL, 2 * D_MODEL, 3 * D_MODEL], axis=-1)
    q = q.reshape(bsz, seq, N_HEADS, HEAD_DIM)
    k = k.reshape(bsz, seq, N_HEADS, HEAD_DIM)
    v = v.reshape(bsz, seq, N_HEADS, HEAD_DIM)
    log_f = jax.nn.log_sigmoid((f_logit + b_f).astype(jnp.float32))
    cum = jnp.cumsum(log_f, axis=1).transpose(0, 2, 1)
    scale = HEAD_DIM ** -0.5
    neg = jnp.finfo(jnp.float32).min
    outs = []
    for blk in range(seq // Q_BLOCK):
        q0 = blk * Q_BLOCK
        q1 = q0 + Q_BLOCK
        qb = q[:, q0:q1]
        kb = k[:, :q1]
        vb = v[:, :q1]
        logits = jnp.einsum("bqhd,bkhd->bhqk", qb, kb,
                            preferred_element_type=jnp.float32) * scale
        logits = logits + cum[:, :, q0:q1, None] - cum[:, :, None, :q1]
        mask = jnp.arange(q0, q1)[:, None] >= jnp.arange(q1)[None, :]
        logits = jnp.where(mask, logits, neg)
        p = jax.nn.softmax(logits, axis=-1)
        outs.append(jnp.einsum("bhqk,bkhd->bqhd", p.astype(vb.dtype), vb))
    o = jnp.concatenate(outs, axis=1).reshape(bsz, seq, D_MODEL)
    return o @ w_o


def conv_gated_ffn(h, w_gate, w_up, conv_w, conv_b, w_down):
    seq = h.shape[1]
    a = h @ w_gate
    a_pad = jnp.pad(a, ((0, 0), (CONV_WIDTH - 1, 0), (0, 0)))
    a = conv_b + a_pad[:, 0:seq] * conv_w[0]
    for i in range(1, CONV_WIDTH):
        a = a + a_pad[:, i:i + seq] * conv_w[i]
    return (jax.nn.silu(a) * (h @ w_up)) @ w_down


def _fwd_setup_inputs(seed: int = 0) -> dict:
    key = jax.random.key(seed)
    ks = jax.random.split(key, 20)

    def normal(k, shape, scale):
        return jax.random.normal(k, shape, jnp.float32) * scale

    D, E, G, C, H, F = D_MODEL, GM_WIDTH, GM_GROUPS, CHUNK, N_HEADS, FFN_DIM
    return {
        "x": normal(ks[0], (BATCH, SEQ, D), 1.0),
        "mix_norm_g": 1.0 + normal(ks[1], (DEPTH, D), 0.02),
        "ffn_norm_g": 1.0 + normal(ks[2], (DEPTH, D), 0.02),
        "gm_w_in": normal(ks[3], (N_A_LAYERS, D, 2 * E), D ** -0.5),
        "gm_ln_g": 1.0 + normal(ks[4], (N_A_LAYERS, E), 0.02),
        "gm_ln_b": normal(ks[5], (N_A_LAYERS, E), 0.02),
        "gm_w_s": normal(ks[6], (N_A_LAYERS, G, C, C), C ** -0.5),
        "gm_b_s": 1.0 + normal(ks[7], (N_A_LAYERS, G, C), 0.02),
        "gm_w_out": normal(ks[8], (N_A_LAYERS, E, D), E ** -0.5),
        "fox_w_qkvf": normal(ks[9], (N_B_LAYERS, D, 3 * D + H), D ** -0.5),
        "fox_b_f": FORGET_BIAS_INIT + normal(ks[10], (N_B_LAYERS, H), 0.5),
        "fox_w_o": normal(ks[11], (N_B_LAYERS, D, D), D ** -0.5),
        "ffn_w_gate": normal(ks[12], (DEPTH, D, F), D ** -0.5),
        "ffn_w_up": normal(ks[13], (DEPTH, D, F), D ** -0.5),
        "ffn_conv_w": normal(ks[14], (DEPTH, CONV_WIDTH, F), CONV_WIDTH ** -0.5),
        "ffn_conv_b": normal(ks[15], (DEPTH, F), 0.01),
        "ffn_w_down": normal(ks[16], (DEPTH, F, D), F ** -0.5),
        "final_norm_g": 1.0 + normal(ks[17], (D,), 0.02),
    }


def _fwd_reference(x, mix_norm_g, ffn_norm_g, gm_w_in, gm_ln_g, gm_ln_b, gm_w_s, gm_b_s,
              gm_w_out, fox_w_qkvf, fox_b_f, fox_w_o, ffn_w_gate, ffn_w_up,
              ffn_conv_w, ffn_conv_b, ffn_w_down, final_norm_g):
    h = x
    for i in range(DEPTH):
        hn = rmsnorm(h, mix_norm_g[i])
        j = i // N_MIXERS
        if i % N_MIXERS == 0:
            mix = chunked_spatial_gating_mixer(hn, gm_w_in[j], gm_ln_g[j], gm_ln_b[j],
                                               gm_w_s[j], gm_b_s[j], gm_w_out[j])
        else:
            mix = forgetting_attention_mixer(hn, fox_w_qkvf[j], fox_b_f[j], fox_w_o[j])
        h = h + mix
        hn = rmsnorm(h, ffn_norm_g[i])
        h = h + conv_gated_ffn(hn, ffn_w_gate[i], ffn_w_up[i], ffn_conv_w[i],
                               ffn_conv_b[i], ffn_w_down[i])
    return rmsnorm(h, final_norm_g)


import jax as _jax
import jax.numpy as _jnp

TWIN_FORMAT = 'train_step'
FWD_PARAMS = ['x', 'mix_norm_g', 'ffn_norm_g', 'gm_w_in', 'gm_ln_g', 'gm_ln_b', 'gm_w_s', 'gm_b_s', 'gm_w_out', 'fox_w_qkvf', 'fox_b_f', 'fox_w_o', 'ffn_w_gate', 'ffn_w_up', 'ffn_conv_w', 'ffn_conv_b', 'ffn_w_down', 'final_norm_g']
TWIN_WEIGHTS = ['mix_norm_g', 'ffn_norm_g', 'gm_w_in', 'gm_ln_g', 'gm_ln_b', 'gm_w_s', 'gm_b_s', 'gm_w_out', 'fox_w_qkvf', 'fox_b_f', 'fox_w_o', 'ffn_w_gate', 'ffn_w_up', 'ffn_conv_w', 'ffn_conv_b', 'ffn_w_down', 'final_norm_g']
TWIN_DIFF_INPUT = 'x'
TWIN_INPUTS = ['x', 'mix_norm_g', 'ffn_norm_g', 'gm_w_in', 'gm_ln_g', 'gm_ln_b', 'gm_w_s', 'gm_b_s', 'gm_w_out', 'fox_w_qkvf', 'fox_b_f', 'fox_w_o', 'ffn_w_gate', 'ffn_w_up', 'ffn_conv_w', 'ffn_conv_b', 'ffn_w_down', 'final_norm_g', 'loss_target', 'm_mix_norm_g', 'm_ffn_norm_g', 'm_gm_w_in', 'm_gm_ln_g', 'm_gm_ln_b', 'm_gm_w_s', 'm_gm_b_s', 'm_gm_w_out', 'm_fox_w_qkvf', 'm_fox_b_f', 'm_fox_w_o', 'm_ffn_w_gate', 'm_ffn_w_up', 'm_ffn_conv_w', 'm_ffn_conv_b', 'm_ffn_w_down', 'm_final_norm_g', 'v_mix_norm_g', 'v_ffn_norm_g', 'v_gm_w_in', 'v_gm_ln_g', 'v_gm_ln_b', 'v_gm_w_s', 'v_gm_b_s', 'v_gm_w_out', 'v_fox_w_qkvf', 'v_fox_b_f', 'v_fox_w_o', 'v_ffn_w_gate', 'v_ffn_w_up', 'v_ffn_conv_w', 'v_ffn_conv_b', 'v_ffn_w_down', 'v_final_norm_g']
TWIN_OUTPUTS = ['loss', 'grad_x', 'grad_mix_norm_g', 'grad_ffn_norm_g', 'grad_gm_w_in', 'grad_gm_ln_g', 'grad_gm_ln_b', 'grad_gm_w_s', 'grad_gm_b_s', 'grad_gm_w_out', 'grad_fox_w_qkvf', 'grad_fox_b_f', 'grad_fox_w_o', 'grad_ffn_w_gate', 'grad_ffn_w_up', 'grad_ffn_conv_w', 'grad_ffn_conv_b', 'grad_ffn_w_down', 'grad_final_norm_g', 'delta_mix_norm_g', 'delta_ffn_norm_g', 'delta_gm_w_in', 'delta_gm_ln_g', 'delta_gm_ln_b', 'delta_gm_w_s', 'delta_gm_b_s', 'delta_gm_w_out', 'delta_fox_w_qkvf', 'delta_fox_b_f', 'delta_fox_w_o', 'delta_ffn_w_gate', 'delta_ffn_w_up', 'delta_ffn_conv_w', 'delta_ffn_conv_b', 'delta_ffn_w_down', 'delta_final_norm_g', 'new_m_mix_norm_g', 'new_m_ffn_norm_g', 'new_m_gm_w_in', 'new_m_gm_ln_g', 'new_m_gm_ln_b', 'new_m_gm_w_s', 'new_m_gm_b_s', 'new_m_gm_w_out', 'new_m_fox_w_qkvf', 'new_m_fox_b_f', 'new_m_fox_w_o', 'new_m_ffn_w_gate', 'new_m_ffn_w_up', 'new_m_ffn_conv_w', 'new_m_ffn_conv_b', 'new_m_ffn_w_down', 'new_m_final_norm_g', 'new_v_mix_norm_g', 'new_v_ffn_norm_g', 'new_v_gm_w_in', 'new_v_gm_ln_g', 'new_v_gm_ln_b', 'new_v_gm_w_s', 'new_v_gm_b_s', 'new_v_gm_w_out', 'new_v_fox_w_qkvf', 'new_v_fox_b_f', 'new_v_fox_w_o', 'new_v_ffn_w_gate', 'new_v_ffn_w_up', 'new_v_ffn_conv_w', 'new_v_ffn_conv_b', 'new_v_ffn_w_down', 'new_v_final_norm_g']
TWIN_LEAF_KINDS = {'loss': 'loss', 'grad_x': 'grad_x', 'grad_mix_norm_g': 'grad_w', 'grad_ffn_norm_g': 'grad_w', 'grad_gm_w_in': 'grad_w', 'grad_gm_ln_g': 'grad_w', 'grad_gm_ln_b': 'grad_w', 'grad_gm_w_s': 'grad_w', 'grad_gm_b_s': 'grad_w', 'grad_gm_w_out': 'grad_w', 'grad_fox_w_qkvf': 'grad_w', 'grad_fox_b_f': 'grad_w', 'grad_fox_w_o': 'grad_w', 'grad_ffn_w_gate': 'grad_w', 'grad_ffn_w_up': 'grad_w', 'grad_ffn_conv_w': 'grad_w', 'grad_ffn_conv_b': 'grad_w', 'grad_ffn_w_down': 'grad_w', 'grad_final_norm_g': 'grad_w', 'delta_mix_norm_g': 'delta_w', 'delta_ffn_norm_g': 'delta_w', 'delta_gm_w_in': 'delta_w', 'delta_gm_ln_g': 'delta_w', 'delta_gm_ln_b': 'delta_w', 'delta_gm_w_s': 'delta_w', 'delta_gm_b_s': 'delta_w', 'delta_gm_w_out': 'delta_w', 'delta_fox_w_qkvf': 'delta_w', 'delta_fox_b_f': 'delta_w', 'delta_fox_w_o': 'delta_w', 'delta_ffn_w_gate': 'delta_w', 'delta_ffn_w_up': 'delta_w', 'delta_ffn_conv_w': 'delta_w', 'delta_ffn_conv_b': 'delta_w', 'delta_ffn_w_down': 'delta_w', 'delta_final_norm_g': 'delta_w', 'new_m_mix_norm_g': 'new_m', 'new_m_ffn_norm_g': 'new_m', 'new_m_gm_w_in': 'new_m', 'new_m_gm_ln_g': 'new_m', 'new_m_gm_ln_b': 'new_m', 'new_m_gm_w_s': 'new_m', 'new_m_gm_b_s': 'new_m', 'new_m_gm_w_out': 'new_m', 'new_m_fox_w_qkvf': 'new_m', 'new_m_fox_b_f': 'new_m', 'new_m_fox_w_o': 'new_m', 'new_m_ffn_w_gate': 'new_m', 'new_m_ffn_w_up': 'new_m', 'new_m_ffn_conv_w': 'new_m', 'new_m_ffn_conv_b': 'new_m', 'new_m_ffn_w_down': 'new_m', 'new_m_final_norm_g': 'new_m', 'new_v_mix_norm_g': 'new_v', 'new_v_ffn_norm_g': 'new_v', 'new_v_gm_w_in': 'new_v', 'new_v_gm_ln_g': 'new_v', 'new_v_gm_ln_b': 'new_v', 'new_v_gm_w_s': 'new_v', 'new_v_gm_b_s': 'new_v', 'new_v_gm_w_out': 'new_v', 'new_v_fox_w_qkvf': 'new_v', 'new_v_fox_b_f': 'new_v', 'new_v_fox_w_o': 'new_v', 'new_v_ffn_w_gate': 'new_v', 'new_v_ffn_w_up': 'new_v', 'new_v_ffn_conv_w': 'new_v', 'new_v_ffn_conv_b': 'new_v', 'new_v_ffn_w_down': 'new_v', 'new_v_final_norm_g': 'new_v'}


def _forward(args):
    return _fwd_reference(*[args[k] for k in FWD_PARAMS])


def _output_shape():
    out = _jax.eval_shape(lambda: _forward(_fwd_setup_inputs(0)))
    return out.shape, out.dtype

N_MICROBATCH = 1
ADAM_LR = 0.001
ADAM_B1 = 0.9
ADAM_B2 = 0.999
ADAM_EPS = 1e-08
ADAM_WD = 0.01
ADAM_STEP = 10
PER_EXAMPLE_BATCH_AXIS = {'x': 0, 'loss_target': 0}
SHARED_INPUTS = []
_WEIGHT_DTYPES = {'mix_norm_g': _jnp.float32, 'ffn_norm_g': _jnp.float32, 'gm_w_in': _jnp.float32, 'gm_ln_g': _jnp.float32, 'gm_ln_b': _jnp.float32, 'gm_w_s': _jnp.float32, 'gm_b_s': _jnp.float32, 'gm_w_out': _jnp.float32, 'fox_w_qkvf': _jnp.float32, 'fox_b_f': _jnp.float32, 'fox_w_o': _jnp.float32, 'ffn_w_gate': _jnp.float32, 'ffn_w_up': _jnp.float32, 'ffn_conv_w': _jnp.float32, 'ffn_conv_b': _jnp.float32, 'ffn_w_down': _jnp.float32, 'final_norm_g': _jnp.float32}
MOMENT_SCALE = {'mix_norm_g': 1.702987e-01, 'ffn_norm_g': 1.649044e-01, 'gm_w_in': 1.131012e-01, 'gm_ln_g': 7.502497e-02, 'gm_ln_b': 7.422417e-02, 'gm_w_s': 1.078378e-01, 'gm_b_s': 1.584339e-01, 'gm_w_out': 1.963876e-01, 'fox_w_qkvf': 4.687307e-02, 'fox_b_f': 3.315149e-01, 'fox_w_o': 6.110269e-02, 'ffn_w_gate': 7.949941e-02, 'ffn_w_up': 7.652148e-02, 'ffn_conv_w': 8.056003e-02, 'ffn_conv_b': 8.181914e-02, 'ffn_w_down': 1.091467e-01, 'final_norm_g': 6.423394e+01}


def _to_microbatches(a, axis):
    t = _jnp.moveaxis(a, axis, 0)
    t = t.reshape((N_MICROBATCH, t.shape[0] // N_MICROBATCH) + t.shape[1:])
    return _jnp.moveaxis(t, 1, axis + 1)


def setup_inputs(seed: int = 0) -> dict:
    inp = _fwd_setup_inputs(seed)
    key = _jax.random.fold_in(_jax.random.key(seed), 7919)
    shape, _ = _output_shape()
    out = dict(inp)
    out["loss_target"] = _jax.random.normal(_jax.random.fold_in(key, 0), shape, _jnp.float32)
    for i, name in enumerate(TWIN_WEIGHTS):
        w = inp[name].astype(_jnp.float32)
        if MOMENT_SCALE is None:
            s = _jnp.sqrt(_jnp.mean(_jnp.square(w)) + 1e-30)
        else:
            s = MOMENT_SCALE[name]
        km, kv = _jax.random.split(_jax.random.fold_in(key, i + 1))
        out[name] = w
        out["m_" + name] = s * _jax.random.normal(km, w.shape, _jnp.float32)
        out["v_" + name] = (s * s) * _jax.random.uniform(kv, w.shape, _jnp.float32, 0.5, 1.5)
    if N_MICROBATCH > 1:
        for name, axis in PER_EXAMPLE_BATCH_AXIS.items():
            out[name] = _to_microbatches(out[name], axis)
    return {'x': out['x'], 'mix_norm_g': out['mix_norm_g'], 'ffn_norm_g': out['ffn_norm_g'], 'gm_w_in': out['gm_w_in'], 'gm_ln_g': out['gm_ln_g'], 'gm_ln_b': out['gm_ln_b'], 'gm_w_s': out['gm_w_s'], 'gm_b_s': out['gm_b_s'], 'gm_w_out': out['gm_w_out'], 'fox_w_qkvf': out['fox_w_qkvf'], 'fox_b_f': out['fox_b_f'], 'fox_w_o': out['fox_w_o'], 'ffn_w_gate': out['ffn_w_gate'], 'ffn_w_up': out['ffn_w_up'], 'ffn_conv_w': out['ffn_conv_w'], 'ffn_conv_b': out['ffn_conv_b'], 'ffn_w_down': out['ffn_w_down'], 'final_norm_g': out['final_norm_g'], 'loss_target': out['loss_target'], 'm_mix_norm_g': out['m_mix_norm_g'], 'm_ffn_norm_g': out['m_ffn_norm_g'], 'm_gm_w_in': out['m_gm_w_in'], 'm_gm_ln_g': out['m_gm_ln_g'], 'm_gm_ln_b': out['m_gm_ln_b'], 'm_gm_w_s': out['m_gm_w_s'], 'm_gm_b_s': out['m_gm_b_s'], 'm_gm_w_out': out['m_gm_w_out'], 'm_fox_w_qkvf': out['m_fox_w_qkvf'], 'm_fox_b_f': out['m_fox_b_f'], 'm_fox_w_o': out['m_fox_w_o'], 'm_ffn_w_gate': out['m_ffn_w_gate'], 'm_ffn_w_up': out['m_ffn_w_up'], 'm_ffn_conv_w': out['m_ffn_conv_w'], 'm_ffn_conv_b': out['m_ffn_conv_b'], 'm_ffn_w_down': out['m_ffn_w_down'], 'm_final_norm_g': out['m_final_norm_g'], 'v_mix_norm_g': out['v_mix_norm_g'], 'v_ffn_norm_g': out['v_ffn_norm_g'], 'v_gm_w_in': out['v_gm_w_in'], 'v_gm_ln_g': out['v_gm_ln_g'], 'v_gm_ln_b': out['v_gm_ln_b'], 'v_gm_w_s': out['v_gm_w_s'], 'v_gm_b_s': out['v_gm_b_s'], 'v_gm_w_out': out['v_gm_w_out'], 'v_fox_w_qkvf': out['v_fox_w_qkvf'], 'v_fox_b_f': out['v_fox_b_f'], 'v_fox_w_o': out['v_fox_w_o'], 'v_ffn_w_gate': out['v_ffn_w_gate'], 'v_ffn_w_up': out['v_ffn_w_up'], 'v_ffn_conv_w': out['v_ffn_conv_w'], 'v_ffn_conv_b': out['v_ffn_conv_b'], 'v_ffn_w_down': out['v_ffn_w_down'], 'v_final_norm_g': out['v_final_norm_g']}


def _loss(weights, diff, rest, loss_target):
    with _jax.named_scope("forward"):
        args = {**rest, TWIN_DIFF_INPUT: diff, **{k: w.astype(_WEIGHT_DTYPES[k]) for k, w in weights.items()}}
        y = _forward(args)
    with _jax.named_scope("loss_head"):
        err = _jnp.square(y.astype(_jnp.float32) - loss_target)
        return 0.5 * _jnp.sum(_jnp.mean(err, axis=-1)) if err.ndim else 0.5 * err


def _adamw(w, g, m, v):
    m = ADAM_B1 * m + (1.0 - ADAM_B1) * g
    v = ADAM_B2 * v + (1.0 - ADAM_B2) * _jnp.square(g)
    m_hat = m / (1.0 - ADAM_B1 ** ADAM_STEP)
    v_hat = v / (1.0 - ADAM_B2 ** ADAM_STEP)
    delta = -ADAM_LR * (m_hat / (_jnp.sqrt(v_hat) + ADAM_EPS) + ADAM_WD * w)
    return delta, m, v


def reference(x, mix_norm_g, ffn_norm_g, gm_w_in, gm_ln_g, gm_ln_b, gm_w_s, gm_b_s, gm_w_out, fox_w_qkvf, fox_b_f, fox_w_o, ffn_w_gate, ffn_w_up, ffn_conv_w, ffn_conv_b, ffn_w_down, final_norm_g, loss_target, m_mix_norm_g, m_ffn_norm_g, m_gm_w_in, m_gm_ln_g, m_gm_ln_b, m_gm_w_s, m_gm_b_s, m_gm_w_out, m_fox_w_qkvf, m_fox_b_f, m_fox_w_o, m_ffn_w_gate, m_ffn_w_up, m_ffn_conv_w, m_ffn_conv_b, m_ffn_w_down, m_final_norm_g, v_mix_norm_g, v_ffn_norm_g, v_gm_w_in, v_gm_ln_g, v_gm_ln_b, v_gm_w_s, v_gm_b_s, v_gm_w_out, v_fox_w_qkvf, v_fox_b_f, v_fox_w_o, v_ffn_w_gate, v_ffn_w_up, v_ffn_conv_w, v_ffn_conv_b, v_ffn_w_down, v_final_norm_g):
    given = dict(x=x, mix_norm_g=mix_norm_g, ffn_norm_g=ffn_norm_g, gm_w_in=gm_w_in, gm_ln_g=gm_ln_g, gm_ln_b=gm_ln_b, gm_w_s=gm_w_s, gm_b_s=gm_b_s, gm_w_out=gm_w_out, fox_w_qkvf=fox_w_qkvf, fox_b_f=fox_b_f, fox_w_o=fox_w_o, ffn_w_gate=ffn_w_gate, ffn_w_up=ffn_w_up, ffn_conv_w=ffn_conv_w, ffn_conv_b=ffn_conv_b, ffn_w_down=ffn_w_down, final_norm_g=final_norm_g, loss_target=loss_target, m_mix_norm_g=m_mix_norm_g, m_ffn_norm_g=m_ffn_norm_g, m_gm_w_in=m_gm_w_in, m_gm_ln_g=m_gm_ln_g, m_gm_ln_b=m_gm_ln_b, m_gm_w_s=m_gm_w_s, m_gm_b_s=m_gm_b_s, m_gm_w_out=m_gm_w_out, m_fox_w_qkvf=m_fox_w_qkvf, m_fox_b_f=m_fox_b_f, m_fox_w_o=m_fox_w_o, m_ffn_w_gate=m_ffn_w_gate, m_ffn_w_up=m_ffn_w_up, m_ffn_conv_w=m_ffn_conv_w, m_ffn_conv_b=m_ffn_conv_b, m_ffn_w_down=m_ffn_w_down, m_final_norm_g=m_final_norm_g, v_mix_norm_g=v_mix_norm_g, v_ffn_norm_g=v_ffn_norm_g, v_gm_w_in=v_gm_w_in, v_gm_ln_g=v_gm_ln_g, v_gm_ln_b=v_gm_ln_b, v_gm_w_s=v_gm_w_s, v_gm_b_s=v_gm_b_s, v_gm_w_out=v_gm_w_out, v_fox_w_qkvf=v_fox_w_qkvf, v_fox_b_f=v_fox_b_f, v_fox_w_o=v_fox_w_o, v_ffn_w_gate=v_ffn_w_gate, v_ffn_w_up=v_ffn_w_up, v_ffn_conv_w=v_ffn_conv_w, v_ffn_conv_b=v_ffn_conv_b, v_ffn_w_down=v_ffn_w_down, v_final_norm_g=v_final_norm_g)
    weights = {n: given[n] for n in TWIN_WEIGHTS}
    shared = {n: given[n] for n in SHARED_INPUTS}
    per_example = {n: given[n] for n in ['x']}
    grad_fn = _jax.value_and_grad(_loss, argnums=(0, 1))

    def one_microbatch(ex, loss_target):
        ex = dict(ex)
        diff = ex.pop(TWIN_DIFF_INPUT)
        return grad_fn(weights, diff, {**shared, **ex}, loss_target)

    if N_MICROBATCH == 1:
        loss, (grad_w, grad_x) = one_microbatch(per_example, given["loss_target"])
    else:
        def body(carry, xs):
            loss_sum, grad_sum = carry
            l_k, (gw_k, gx_k) = one_microbatch(xs[0], xs[1])
            with _jax.named_scope("update"):
                return (loss_sum + l_k, _jax.tree.map(_jnp.add, grad_sum, gw_k)), gx_k

        init = (_jnp.zeros((), _jnp.float32), _jax.tree.map(_jnp.zeros_like, weights))
        (loss, grad_w), grad_x = _jax.lax.scan(body, init, (per_example, given["loss_target"]))
    with _jax.named_scope("update"):
        delta_w, new_m, new_v = {}, {}, {}
        for n in TWIN_WEIGHTS:
            delta_w[n], new_m[n], new_v[n] = _adamw(weights[n], grad_w[n], given["m_" + n], given["v_" + n])
    return (loss, grad_x, *[grad_w[n] for n in TWIN_WEIGHTS], *[delta_w[n] for n in TWIN_WEIGHTS],
            *[new_m[n] for n in TWIN_WEIGHTS], *[new_v[n] for n in TWIN_WEIGHTS])
```

```python
import functools
import math

import jax
import jax.numpy as jnp
from jax import lax
from jax.experimental import pallas as pl
from jax.experimental.pallas import tpu as pltpu

F32 = jnp.float32
BF16 = jnp.bfloat16

RMS_EPS = 1e-6
LN_EPS = 1e-5
CHUNK = 128
GM_GROUPS = 8
HEAD_DIM = 64
LANES = 128
ATT_BLOCK = 256
VMEM_LIMIT_V7X = 56 * 1024 * 1024

ADAM_LR = 0.001
ADAM_B1 = 0.9
ADAM_B2 = 0.999
ADAM_EPS = 1e-08
ADAM_WD = 0.01
ADAM_STEP = 10

MESH = pl.DeviceIdType.MESH
ANY = pl.BlockSpec(memory_space=pl.ANY)
NEG_BIG = -1e30


def _params(n_grid):
    return pltpu.CompilerParams(dimension_semantics=("arbitrary",) * n_grid, vmem_limit_bytes=VMEM_LIMIT_V7X)


def _dot(a, b):
    return jnp.dot(a, b, preferred_element_type=F32)


def _dot_nt(a, b):
    return lax.dot_general(a, b, (((1,), (1,)), ((), ())), preferred_element_type=F32)


def _dot_tn(a, b):
    return lax.dot_general(a, b, (((0,), (0,)), ((), ())), preferred_element_type=F32)


def _split3(x):
    hi = x.astype(BF16)
    r = x - hi.astype(F32)
    mid = r.astype(BF16)
    lo = (r - mid.astype(F32)).astype(BF16)
    return hi, mid, lo


def _dot3_lhs(x, m):
    hi, mid, lo = _split3(x)
    return _dot(hi, m) + _dot(mid, m) + _dot(lo, m)


def _dot3_rhs(m, x):
    hi, mid, lo = _split3(x)
    return _dot(m, hi) + _dot(m, mid) + _dot(m, lo)


def _load_once(pairs, sem):
    @pl.when(pl.program_id(0) == 0)
    def _():
        copies = [pltpu.make_async_copy(src, dst, sem.at[k]) for k, (src, dst) in enumerate(pairs)]
        for cp in copies:
            cp.start()
        for cp in copies:
            cp.wait()


def _rms_fwd(x, g):
    r = lax.rsqrt(jnp.mean(x * x, axis=-1, keepdims=True) + RMS_EPS)
    xhat = x * r
    return xhat, r, xhat * g


def _rms_bwd(dy, xhat, r, g):
    w = dy * g
    dx = r * (w - xhat * jnp.mean(w * xhat, axis=-1, keepdims=True))
    return dx, dy * xhat


def _gelu_parts(a):
    c = math.sqrt(2.0 / math.pi)
    a2 = a * a
    t = jnp.tanh(c * (a + 0.044715 * a * a2))
    z = 0.5 * a * (1.0 + t)
    dz = 0.5 * (1.0 + t) + 0.5 * a * (1.0 - t * t) * (c * (1.0 + 3.0 * 0.044715 * a2))
    return z, dz


def _sigmoid(x):
    return 1.0 / (1.0 + jnp.exp(-x))


def _gmlp_core(a, lng, lnb, wc_ref, bias, n_chunk, gd):
    e = a.shape[1] // 2
    z, dz = _gelu_parts(a)
    u = z[:, :e]
    v = z[:, e:]
    mu = jnp.mean(v, axis=-1, keepdims=True)
    vc = v - mu
    rstd = lax.rsqrt(jnp.mean(vc * vc, axis=-1, keepdims=True) + LN_EPS)
    vhat = vc * rstd
    vln = vhat * lng + lnb
    vlb = vln.astype(BF16)
    rows = []
    for ci in range(n_chunk):
        cols = []
        for g in range(GM_GROUPS):
            blk = vlb[ci * CHUNK:(ci + 1) * CHUNK, g * gd:(g + 1) * gd]
            cols.append(_dot(wc_ref[g], blk))
        rows.append(jnp.concatenate(cols, axis=1) + bias)
    s = rows[0] if n_chunk == 1 else jnp.concatenate(rows, axis=0)
    return dz, u, vhat, rstd, vlb, s


def _gmlp_fwd(h, g_mix, w_in, lng, lnb, wc, bias, w_out, tm):
    s_len, d = h.shape
    n_p, _, w = w_in.shape
    e = w_out.shape[0]
    gd = e // GM_GROUPS
    n_chunk = tm // CHUNK

    def body(h_ref, g_ref, win_hbm, lng_ref, lnb_ref, wc_ref, bias_ref, wout_hbm,
             hout_ref, a_ref, hn_ref, gated_ref, win_v, wout_v, sem):
        _load_once([(win_hbm, win_v), (wout_hbm, wout_v)], sem)
        x = h_ref[...]
        _, _, y = _rms_fwd(x, g_ref[...])
        hn = y.astype(BF16)
        hn_ref[...] = hn
        for p in range(n_p):
            a_ref[:, p * w:(p + 1) * w] = _dot(hn, win_v[p])
        _, u, _, _, _, s = _gmlp_core(a_ref[...], lng_ref[...], lnb_ref[...], wc_ref, bias_ref[...], n_chunk, gd)
        gated = (u * s).astype(BF16)
        gated_ref[...] = gated
        hout_ref[...] = x + _dot(gated, wout_v[...])

    row = lambda i: (i, 0)
    const2 = lambda i: (0, 0)
    return pl.pallas_call(
        body, name="gmlp_fwd", grid=(s_len // tm,),
        in_specs=[pl.BlockSpec((tm, d), row), pl.BlockSpec((1, d), const2), ANY,
                  pl.BlockSpec((1, e), const2), pl.BlockSpec((1, e), const2),
                  pl.BlockSpec(wc.shape, lambda i: (0, 0, 0)), pl.BlockSpec((CHUNK, e), const2), ANY],
        out_specs=[pl.BlockSpec((tm, d), row), pl.BlockSpec((tm, 2 * e), row),
                   pl.BlockSpec((tm, d), row), pl.BlockSpec((tm, e), row)],
        out_shape=[jax.ShapeDtypeStruct((s_len, d), F32), jax.ShapeDtypeStruct((s_len, 2 * e), F32),
                   jax.ShapeDtypeStruct((s_len, d), BF16), jax.ShapeDtypeStruct((s_len, e), BF16)],
        scratch_shapes=[pltpu.VMEM(w_in.shape, BF16), pltpu.VMEM(w_out.shape, BF16), pltpu.SemaphoreType.DMA((2,))],
        compiler_params=_params(1),
    )(h, g_mix, w_in, lng, lnb, wc, bias, w_out)


def _gmlp_bwd(h, dh, a, g_mix, w_in, lng, lnb, wc, wct, bias, w_out, seg, tm):
    s_len, d = h.shape
    n_p, _, w = w_in.shape
    e = w_out.shape[0]
    gd = e // GM_GROUPS
    n_chunk = tm // CHUNK
    n_blk = s_len // tm

    def body(h_ref, dh_ref, a_ref, g_ref, win_hbm, lng_ref, lnb_ref, wc_ref, wct_ref, bias_ref, wout_hbm, seg_ref,
             dhin_ref, da_ref, gws_ref, gbs_ref, glng_ref, glnb_ref, gmix_ref, win_v, wout_v, dsum, sem):
        i = pl.program_id(0)
        _load_once([(win_hbm, win_v), (wout_hbm, wout_v)], sem)

        @pl.when(i == 0)
        def _():
            gws_ref[...] = jnp.zeros_like(gws_ref)
            glng_ref[...] = jnp.zeros_like(glng_ref)
            glnb_ref[...] = jnp.zeros_like(glnb_ref)
            gmix_ref[...] = jnp.zeros_like(gmix_ref)
            dsum[...] = jnp.zeros_like(dsum)

        x = h_ref[...]
        dh_v = dh_ref[...]
        g = g_ref[...]
        lng_v = lng_ref[...]
        xhat, r, _ = _rms_fwd(x, g)
        dz_da, u, vhat, rstd, vlb, s = _gmlp_core(a_ref[...], lng_v, lnb_ref[...], wc_ref, bias_ref[...], n_chunk, gd)
        dg = _dot_nt(dh_v.astype(BF16), wout_v[...])
        du = dg * s
        ds = dg * u
        dsb = ds.astype(BF16)
        rows = []
        ds_acc = None
        for ci in range(n_chunk):
            lo, hi = ci * CHUNK, (ci + 1) * CHUNK
            cols = []
            for gi in range(GM_GROUPS):
                d_blk = dsb[lo:hi, gi * gd:(gi + 1) * gd]
                gws_ref[gi] += _dot_nt(d_blk, vlb[lo:hi, gi * gd:(gi + 1) * gd])
                cols.append(_dot(wct_ref[gi], d_blk))
            rows.append(jnp.concatenate(cols, axis=1))
            ds_acc = ds[lo:hi] if ds_acc is None else ds_acc + ds[lo:hi]
        dsum[...] += ds_acc
        dvln = rows[0] if n_chunk == 1 else jnp.concatenate(rows, axis=0)
        glng_ref[...] += jnp.sum(dvln * vhat, axis=0, keepdims=True)
        glnb_ref[...] += jnp.sum(dvln, axis=0, keepdims=True)
        dvhat = dvln * lng_v
        dv = rstd * (dvhat - jnp.mean(dvhat, axis=-1, keepdims=True)
                     - vhat * jnp.mean(dvhat * vhat, axis=-1, keepdims=True))
        da = jnp.concatenate([du, dv], axis=1) * dz_da
        dab = da.astype(BF16)
        da_ref[...] = dab
        dhn = _dot_nt(dab[:, :w], win_v[0])
        for p in range(1, n_p):
            dhn += _dot_nt(dab[:, p * w:(p + 1) * w], win_v[p])
        dx, gg = _rms_bwd(dhn, xhat, r, g)
        gmix_ref[...] += jnp.sum(gg, axis=0, keepdims=True)
        dhin_ref[...] = dh_v + dx

        @pl.when(i == n_blk - 1)
        def _():
            tril = lax.broadcasted_iota(jnp.int32, (CHUNK, CHUNK), 0) >= lax.broadcasted_iota(jnp.int32, (CHUNK, CHUNK), 1)
            for gi in range(GM_GROUPS):
                gws_ref[gi] = jnp.where(tril, gws_ref[gi], 0.0)
            gbs_ref[...] = _dot3_lhs(dsum[...], seg_ref[...])

    row = lambda i: (i, 0)
    const2 = lambda i: (0, 0)
    const3 = lambda i: (0, 0, 0)
    return pl.pallas_call(
        body, name="gmlp_bwd", grid=(n_blk,),
        in_specs=[pl.BlockSpec((tm, d), row), pl.BlockSpec((tm, d), row), pl.BlockSpec((tm, 2 * e), row),
                  pl.BlockSpec((1, d), const2), ANY, pl.BlockSpec((1, e), const2), pl.BlockSpec((1, e), const2),
                  pl.BlockSpec(wc.shape, const3), pl.BlockSpec(wct.shape, const3), pl.BlockSpec((CHUNK, e), const2),
                  ANY, pl.BlockSpec((e, LANES), const2)],
        out_specs=[pl.BlockSpec((tm, d), row), pl.BlockSpec((tm, 2 * e), row), pl.BlockSpec(wc.shape, const3),
                   pl.BlockSpec((CHUNK, LANES), const2), pl.BlockSpec((1, e), const2), pl.BlockSpec((1, e), const2),
                   pl.BlockSpec((1, d), const2)],
        out_shape=[jax.ShapeDtypeStruct((s_len, d), F32), jax.ShapeDtypeStruct((s_len, 2 * e), BF16),
                   jax.ShapeDtypeStruct(wc.shape, F32), jax.ShapeDtypeStruct((CHUNK, LANES), F32),
                   jax.ShapeDtypeStruct((1, e), F32), jax.ShapeDtypeStruct((1, e), F32), jax.ShapeDtypeStruct((1, d), F32)],
        scratch_shapes=[pltpu.VMEM(w_in.shape, BF16), pltpu.VMEM(w_out.shape, BF16), pltpu.VMEM((CHUNK, e), F32),
                        pltpu.SemaphoreType.DMA((2,))],
        compiler_params=_params(1),
    )(h, dh, a, g_mix, w_in, lng, lnb, wc, wct, bias, w_out, seg)


def _shift_down(a, k, fill):
    tm = a.shape[0]
    out = pltpu.roll(a, k, 0)
    rid = lax.broadcasted_iota(jnp.int32, a.shape, 0)
    for j in range(k):
        out = jnp.where(rid == j, fill[8 - k + j:8 - k + j + 1, :], out)
    return out


def _shift_up(a, k, fill):
    tm = a.shape[0]
    out = pltpu.roll(a, tm - k, 0)
    rid = lax.broadcasted_iota(jnp.int32, a.shape, 0)
    for j in range(k):
        out = jnp.where(rid == tm - k + j, fill[j:j + 1, :], out)
    return out


def _ffn_fwd(h, g_norm, wg_all, wu_all, wd_all, layer, conv_w, conv_b, tm):
    s_len, d = h.shape
    n_p = wg_all.shape[0]
    fq = wg_all.shape[3]
    f = n_p * fq

    def body(h_ref, g_ref, wg_hbm, wu_hbm, wd_hbm, cw_ref, cb_ref,
             hout_ref, a_ref, up_ref, hn_ref, hid_ref, wg_v, wu_v, wd_v, carry, sem):
        i = pl.program_id(0)
        _load_once([(wg_hbm.at[:, layer], wg_v), (wu_hbm.at[:, layer], wu_v), (wd_hbm.at[:, layer], wd_v)], sem)

        @pl.when(i == 0)
        def _():
            carry[...] = jnp.zeros_like(carry)

        x = h_ref[...]
        _, _, y = _rms_fwd(x, g_ref[...])
        hn = y.astype(BF16)
        hn_ref[...] = hn
        for p in range(n_p):
            a_ref[:, p * fq:(p + 1) * fq] = _dot(hn, wg_v[p])
            up_ref[:, p * fq:(p + 1) * fq] = _dot(hn, wu_v[p])
        a = a_ref[...]
        prev = carry[...]
        am1 = _shift_down(a, 1, prev)
        am2 = _shift_down(a, 2, prev)
        carry[...] = a[tm - 8:tm, :]
        cw = cw_ref[...]
        ac = cb_ref[...] + am2 * cw[0:1, :]
        ac = ac + am1 * cw[1:2, :]
        ac = ac + a * cw[2:3, :]
        hid = (ac * _sigmoid(ac) * up_ref[...]).astype(BF16)
        hid_ref[...] = hid
        y2 = _dot(hid[:, :fq], wd_v[0])
        for p in range(1, n_p):
            y2 += _dot(hid[:, p * fq:(p + 1) * fq], wd_v[p])
        hout_ref[...] = x + y2

    row = lambda i: (i, 0)
    const2 = lambda i: (0, 0)
    return pl.pallas_call(
        body, name=f"ffn_fwd_{layer}", grid=(s_len // tm,),
        in_specs=[pl.BlockSpec((tm, d), row), pl.BlockSpec((1, d), const2), ANY, ANY, ANY,
                  pl.BlockSpec((8, f), const2), pl.BlockSpec((1, f), const2)],
        out_specs=[pl.BlockSpec((tm, d), row), pl.BlockSpec((tm, f), row), pl.BlockSpec((tm, f), row),
                   pl.BlockSpec((tm, d), row), pl.BlockSpec((tm, f), row)],
        out_shape=[jax.ShapeDtypeStruct((s_len, d), F32), jax.ShapeDtypeStruct((s_len, f), F32),
                   jax.ShapeDtypeStruct((s_len, f), F32), jax.ShapeDtypeStruct((s_len, d), BF16),
                   jax.ShapeDtypeStruct((s_len, f), BF16)],
        scratch_shapes=[pltpu.VMEM((n_p, d, fq), BF16), pltpu.VMEM((n_p, d, fq), BF16), pltpu.VMEM((n_p, fq, d), BF16),
                        pltpu.VMEM((8, f), F32), pltpu.SemaphoreType.DMA((3,))],
        compiler_params=_params(1),
    )(h, g_norm, wg_all, wu_all, wd_all, conv_w, conv_b)


def _ffn_bwd(h, dh, a, up, g_norm, wg_all, wu_all, wd_all, layer, conv_w, conv_b, tm):
    s_len, d = h.shape
    n_p = wg_all.shape[0]
    fq = wg_all.shape[3]
    f = n_p * fq
    n_blk = s_len // tm
    t8 = tm // 8

    def body(h_ref, dh_ref, a_ref, ahalo_ref, up_ref, g_ref, wg_hbm, wu_hbm, wd_hbm, cw_ref, cb_ref,
             dhin_ref, da_ref, dup_ref, gcw_ref, gcb_ref, gn_ref, wg_v, wu_v, wd_v, carry, sem):
        i = pl.program_id(0)
        _load_once([(wg_hbm.at[:, layer], wg_v), (wu_hbm.at[:, layer], wu_v), (wd_hbm.at[:, layer], wd_v)], sem)

        @pl.when(i == 0)
        def _():
            carry[...] = jnp.zeros_like(carry)
            gcw_ref[...] = jnp.zeros_like(gcw_ref)
            gcb_ref[...] = jnp.zeros_like(gcb_ref)
            gn_ref[...] = jnp.zeros_like(gn_ref)

        x = h_ref[...]
        dh_v = dh_ref[...]
        g = g_ref[...]
        xhat, r, _ = _rms_fwd(x, g)
        a = a_ref[...]
        up_v = up_ref[...]
        prev = jnp.where(i == n_blk - 1, 0.0, ahalo_ref[...])
        am1 = _shift_down(a, 1, prev)
        am2 = _shift_down(a, 2, prev)
        cw = cw_ref[...]
        ac = cb_ref[...] + am2 * cw[0:1, :]
        ac = ac + am1 * cw[1:2, :]
        ac = ac + a * cw[2:3, :]
        sg = _sigmoid(ac)
        sil = ac * sg
        dhb = dh_v.astype(BF16)
        dhid = jnp.concatenate([_dot_nt(dhb, wd_v[p]) for p in range(n_p)], axis=1)
        dup = dhid * sil
        dac = dhid * up_v * (sg * (1.0 + ac * (1.0 - sg)))
        gcb_ref[...] += jnp.sum(dac, axis=0, keepdims=True)
        gcw_ref[0:1, :] += jnp.sum(dac * am2, axis=0, keepdims=True)
        gcw_ref[1:2, :] += jnp.sum(dac * am1, axis=0, keepdims=True)
        gcw_ref[2:3, :] += jnp.sum(dac * a, axis=0, keepdims=True)
        nxt = carry[...]
        dp1 = _shift_up(dac, 1, nxt)
        dp2 = _shift_up(dac, 2, nxt)
        carry[...] = dac[0:8, :]
        da = dac * cw[2:3, :] + dp1 * cw[1:2, :] + dp2 * cw[0:1, :]
        dab = da.astype(BF16)
        dupb = dup.astype(BF16)
        da_ref[...] = dab
        dup_ref[...] = dupb
        dhn = _dot_nt(dab[:, :fq], wg_v[0]) + _dot_nt(dupb[:, :fq], wu_v[0])
        for p in range(1, n_p):
            dhn += _dot_nt(dab[:, p * fq:(p + 1) * fq], wg_v[p]) + _dot_nt(dupb[:, p * fq:(p + 1) * fq], wu_v[p])
        dx, gg = _rms_bwd(dhn, xhat, r, g)
        gn_ref[...] += jnp.sum(gg, axis=0, keepdims=True)
        dhin_ref[...] = dh_v + dx

    rev = lambda i: (n_blk - 1 - i, 0)
    halo = lambda i: (jnp.maximum((n_blk - 1 - i) * t8 - 1, 0), 0)
    const2 = lambda i: (0, 0)
    return pl.pallas_call(
        body, name=f"ffn_bwd_{layer}", grid=(n_blk,),
        in_specs=[pl.BlockSpec((tm, d), rev), pl.BlockSpec((tm, d), rev), pl.BlockSpec((tm, f), rev),
                  pl.BlockSpec((8, f), halo), pl.BlockSpec((tm, f), rev), pl.BlockSpec((1, d), const2), ANY, ANY, ANY,
                  pl.BlockSpec((8, f), const2), pl.BlockSpec((1, f), const2)],
        out_specs=[pl.BlockSpec((tm, d), rev), pl.BlockSpec((tm, f), rev), pl.BlockSpec((tm, f), rev),
                   pl.BlockSpec((8, f), const2), pl.BlockSpec((1, f), const2), pl.BlockSpec((1, d), const2)],
        out_shape=[jax.ShapeDtypeStruct((s_len, d), F32), jax.ShapeDtypeStruct((s_len, f), BF16),
                   jax.ShapeDtypeStruct((s_len, f), BF16), jax.ShapeDtypeStruct((8, f), F32),
                   jax.ShapeDtypeStruct((1, f), F32), jax.ShapeDtypeStruct((1, d), F32)],
        scratch_shapes=[pltpu.VMEM((n_p, d, fq), BF16), pltpu.VMEM((n_p, d, fq), BF16), pltpu.VMEM((n_p, fq, d), BF16),
                        pltpu.VMEM((8, f), F32), pltpu.SemaphoreType.DMA((3,))],
        compiler_params=_params(1),
    )(h, dh, a, a, up, g_norm, wg_all, wu_all, wd_all, conv_w, conv_b)


def _even_head_lanes(shape, axis):
    return (lax.broadcasted_iota(jnp.int32, shape, axis) & HEAD_DIM) == 0


def _fox_proj_fwd(h, g_norm, wq, wk, wv, wf, bf, expand, tm):
    s_len, d = h.shape

    def body(h_ref, g_ref, wq_hbm, wk_hbm, wv_hbm, wf_ref, bf_ref, ex_ref,
             hn_ref, q_ref, qt_ref, ka_ref, kb_ref, kat_ref, kbt_ref, va_ref, vb_ref, vat_ref, vbt_ref,
             z_ref, cumx_ref, cum_ref, wq_v, wk_v, wv_v, total, sem):
        i = pl.program_id(0)
        _load_once([(wq_hbm, wq_v), (wk_hbm, wk_v), (wv_hbm, wv_v)], sem)

        @pl.when(i == 0)
        def _():
            total[...] = jnp.zeros_like(total)

        x = h_ref[...]
        _, _, y = _rms_fwd(x, g_ref[...])
        hn = y.astype(BF16)
        hn_ref[...] = hn
        q = _dot(hn, wq_v[...]) * (HEAD_DIM ** -0.5)
        k = _dot(hn, wk_v[...])
        v = _dot(hn, wv_v[...])
        q_ref[...] = q.astype(BF16)
        qt_ref[...] = q.T.astype(BF16)
        even = _even_head_lanes((tm, d), 1)
        even_t = _even_head_lanes((d, tm), 0)
        kt = k.T
        vt = v.T
        ka_ref[...] = jnp.where(even, k, 0.0).astype(BF16)
        kb_ref[...] = jnp.where(even, 0.0, k).astype(BF16)
        kat_ref[...] = jnp.where(even_t, kt, 0.0).astype(BF16)
        kbt_ref[...] = jnp.where(even_t, 0.0, kt).astype(BF16)
        va_ref[...] = jnp.where(even, v, 0.0).astype(BF16)
        vb_ref[...] = jnp.where(even, 0.0, v).astype(BF16)
        vat_ref[...] = jnp.where(even_t, vt, 0.0).astype(BF16)
        vbt_ref[...] = jnp.where(even_t, 0.0, vt).astype(BF16)
        z = _dot(hn, wf_ref[...]) + bf_ref[...]
        z_ref[...] = z
        logf = jnp.minimum(z, 0.0) - jnp.log(1.0 + jnp.exp(-jnp.abs(z)))
        tri = (lax.broadcasted_iota(jnp.int32, (tm, tm), 0) >= lax.broadcasted_iota(jnp.int32, (tm, tm), 1))
        cum = _dot3_rhs(jnp.where(tri, 1.0, 0.0).astype(BF16), logf) + total[0:1, :]
        total[...] = jnp.broadcast_to(cum[tm - 1:tm, :], total.shape)
        cum_ref[...] = cum
        cumx_ref[...] = _dot3_lhs(cum, ex_ref[...])

    row = lambda i: (i, 0)
    col = lambda i: (0, i)
    const2 = lambda i: (0, 0)
    sd = jax.ShapeDtypeStruct((s_len, d), BF16)
    ds_ = jax.ShapeDtypeStruct((d, s_len), BF16)
    rs, cs = pl.BlockSpec((tm, d), row), pl.BlockSpec((d, tm), col)
    return pl.pallas_call(
        body, name="fox_proj_fwd", grid=(s_len // tm,),
        in_specs=[rs, pl.BlockSpec((1, d), const2), ANY, ANY, ANY, pl.BlockSpec((d, LANES), const2),
                  pl.BlockSpec((1, LANES), const2), pl.BlockSpec((LANES, d), const2)],
        out_specs=[rs, rs, cs, rs, rs, cs, cs, rs, rs, cs, cs,
                   pl.BlockSpec((tm, LANES), row), rs, pl.BlockSpec((tm, LANES), row)],
        out_shape=[sd, sd, ds_, sd, sd, ds_, ds_, sd, sd, ds_, ds_,
                   jax.ShapeDtypeStruct((s_len, LANES), F32), jax.ShapeDtypeStruct((s_len, d), F32),
                   jax.ShapeDtypeStruct((s_len, LANES), F32)],
        scratch_shapes=[pltpu.VMEM((d, d), BF16), pltpu.VMEM((d, d), BF16), pltpu.VMEM((d, d), BF16),
                        pltpu.VMEM((8, LANES), F32), pltpu.SemaphoreType.DMA((3,))],
        compiler_params=_params(1),
    )(h, g_norm, wq, wk, wv, wf, bf, expand)


def _pair_select(lo, hi, shape):
    return jnp.where(lax.broadcasted_iota(jnp.int32, shape, 1) < HEAD_DIM, lo, hi)


def _causal(t):
    return lax.broadcasted_iota(jnp.int32, (t, t), 0) >= lax.broadcasted_iota(jnp.int32, (t, t), 1)


def _flash_fwd(q, kat, kbt, va, vb, cumx, cum_rows):
    s_len, d = q.shape
    t = ATT_BLOCK
    n_pair = d // LANES
    n_q = s_len // t

    def body(q_ref, kat_ref, kbt_ref, va_ref, vb_ref, cqx_ref, ck_ref, o_ref, rbx_ref):
        i = pl.program_id(1)
        qv = q_ref[...]
        cq = (cqx_ref[:, 0:1], cqx_ref[:, HEAD_DIM:HEAD_DIM + 1])
        kts = (kat_ref, kbt_ref)
        vs = (va_ref, vb_ref)

        def step(kb, carry, masked):
            m, l, acc = carry
            off = pl.multiple_of(kb * t, t)
            new_m, new_l, alphas, pv = [], [], [], None
            for hh in range(2):
                s = _dot(qv, kts[hh][:, pl.ds(off, t)]) + cq[hh] - ck_ref[0, hh:hh + 1, pl.ds(off, t)]
                if masked:
                    s = jnp.where(_causal(t), s, NEG_BIG)
                m_new = jnp.maximum(m[hh], jnp.max(s, axis=1, keepdims=True))
                alpha = jnp.exp(m[hh] - m_new)
                p = jnp.exp(s - m_new)
                new_l.append(alpha * l[hh] + jnp.sum(p, axis=1, keepdims=True))
                new_m.append(m_new)
                alphas.append(alpha)
                contrib = _dot(p.astype(BF16), vs[hh][pl.ds(off, t), :])
                pv = contrib if pv is None else pv + contrib
            acc = acc * _pair_select(alphas[0], alphas[1], (t, LANES)) + pv
            return tuple(new_m), tuple(new_l), acc

        init = ((jnp.full((t, 1), NEG_BIG, F32),) * 2, (jnp.zeros((t, 1), F32),) * 2, jnp.zeros((t, LANES), F32))
        carry = lax.fori_loop(0, i, lambda kb, c: step(kb, c, False), init)
        m, l, acc = step(i, carry, True)
        o_ref[...] = acc / _pair_select(l[0], l[1], (t, LANES))
        rb = [cq[hh] - (m[hh] + jnp.log(l[hh])) for hh in range(2)]
        rbx_ref[...] = _pair_select(rb[0], rb[1], (t, LANES))

    qblk = pl.BlockSpec((t, LANES), lambda j, i: (i, j))
    whole_t = pl.BlockSpec((LANES, s_len), lambda j, i: (j, 0))
    whole = pl.BlockSpec((s_len, LANES), lambda j, i: (0, j))
    return pl.pallas_call(
        body, name="flash_fwd", grid=(n_pair, n_q),
        in_specs=[qblk, whole_t, whole_t, whole, whole, qblk, pl.BlockSpec((1, 8, s_len), lambda j, i: (j, 0, 0))],
        out_specs=[qblk, qblk],
        out_shape=[jax.ShapeDtypeStruct((s_len, d), F32), jax.ShapeDtypeStruct((s_len, d), F32)],
        compiler_params=_params(2),
    )(q, kat, kbt, va, vb, cumx, cum_rows)


def _flash_bwd_dq(q, kat, kbt, ka, kb_, vat, vbt, do, rbx, deltax, cum_rows):
    s_len, d = q.shape
    t = ATT_BLOCK
    n_pair = d // LANES
    n_q = s_len // t

    def body(q_ref, kat_ref, kbt_ref, ka_ref, kb_ref, vat_ref, vbt_ref, do_ref, rbx_ref, dlx_ref, ck_ref,
             dq_ref, dck_ref, dcqx_ref):
        i = pl.program_id(1)

        @pl.when(i == 0)
        def _():
            dck_ref[...] = jnp.zeros_like(dck_ref)

        qv = q_ref[...]
        dov = do_ref[...]
        rb = (rbx_ref[:, 0:1], rbx_ref[:, HEAD_DIM:HEAD_DIM + 1])
        dl = (dlx_ref[:, 0:1], dlx_ref[:, HEAD_DIM:HEAD_DIM + 1])
        kts, ks, vts = (kat_ref, kbt_ref), (ka_ref, kb_ref), (vat_ref, vbt_ref)

        def step(kb, carry, masked):
            acc, rsum = carry
            off = pl.multiple_of(kb * t, t)
            new_rsum = []
            for hh in range(2):
                s = _dot(qv, kts[hh][:, pl.ds(off, t)]) + rb[hh] - ck_ref[0, hh:hh + 1, pl.ds(off, t)]
                p = jnp.exp(s)
                if masked:
                    p = jnp.where(_causal(t), p, 0.0)
                dp = _dot(dov, vts[hh][:, pl.ds(off, t)])
                ds = p * (dp - dl[hh])
                acc = acc + _dot(ds.astype(BF16), ks[hh][pl.ds(off, t), :])
                dck_ref[0, hh:hh + 1, pl.ds(off, t)] -= jnp.sum(ds, axis=0, keepdims=True)
                new_rsum.append(rsum[hh] + jnp.sum(ds, axis=1, keepdims=True))
            return acc, tuple(new_rsum)

        init = (jnp.zeros((t, LANES), F32), (jnp.zeros((t, 1), F32),) * 2)
        carry = lax.fori_loop(0, i, lambda kb, c: step(kb, c, False), init)
        acc, rsum = step(i, carry, True)
        dq_ref[...] = (acc * (HEAD_DIM ** -0.5)).astype(BF16)
        dcqx_ref[...] = _pair_select(rsum[0], rsum[1], (t, LANES))

    qblk = pl.BlockSpec((t, LANES), lambda j, i: (i, j))
    whole_t = pl.BlockSpec((LANES, s_len), lambda j, i: (j, 0))
    whole = pl.BlockSpec((s_len, LANES), lambda j, i: (0, j))
    rows = pl.BlockSpec((1, 8, s_len), lambda j, i: (j, 0, 0))
    return pl.pallas_call(
        body, name="flash_bwd_dq", grid=(n_pair, n_q),
        in_specs=[qblk, whole_t, whole_t, whole, whole, whole_t, whole_t, qblk, qblk, qblk, rows],
        out_specs=[qblk, rows, qblk],
        out_shape=[jax.ShapeDtypeStruct((s_len, d), BF16), jax.ShapeDtypeStruct((n_pair, 8, s_len), F32),
                   jax.ShapeDtypeStruct((s_len, d), F32)],
        compiler_params=_params(2),
    )(q, kat, kbt, ka, kb_, vat, vbt, do, rbx, deltax, cum_rows)


def _flash_bwd_dkv(ka, kb_, va, vb, q, qt, do, dot_, cumx, rb_rows, delta_rows):
    s_len, d = q.shape
    t = ATT_BLOCK
    n_pair = d // LANES
    n_k = s_len // t

    def body(ka_ref, kb_ref, va_ref, vb_ref, q_ref, qt_ref, do_ref, dot_ref, ckx_ref, rb_ref, dl_ref, dk_ref, dv_ref):
        kblk = pl.program_id(1)
        ck = (ckx_ref[:, 0:1], ckx_ref[:, HEAD_DIM:HEAD_DIM + 1])
        ks = (ka_ref[...], kb_ref[...])
        vs = (va_ref[...], vb_ref[...])

        def step(qb, carry, masked):
            off = pl.multiple_of(qb * t, t)
            qtv = qt_ref[:, pl.ds(off, t)]
            dotv = dot_ref[:, pl.ds(off, t)]
            qv = q_ref[pl.ds(off, t), :]
            dov = do_ref[pl.ds(off, t), :]
            out = []
            for hh in range(2):
                dk_acc, dv_acc = carry[hh]
                st = _dot(ks[hh], qtv) + rb_ref[0, hh:hh + 1, pl.ds(off, t)] - ck[hh]
                pt = jnp.exp(st)
                if masked:
                    keys = lax.broadcasted_iota(jnp.int32, (t, t), 0)
                    pt = jnp.where(lax.broadcasted_iota(jnp.int32, (t, t), 1) >= keys, pt, 0.0)
                dpt = _dot(vs[hh], dotv)
                dst = pt * (dpt - dl_ref[0, hh:hh + 1, pl.ds(off, t)])
                out.append((dk_acc + _dot(dst.astype(BF16), qv), dv_acc + _dot(pt.astype(BF16), dov)))
            return tuple(out)

        zero = jnp.zeros((t, LANES), F32)
        carry = step(kblk, ((zero, zero), (zero, zero)), True)
        carry = lax.fori_loop(kblk + 1, n_k, lambda qb, c: step(qb, c, False), carry)
        dk_ref[...] = _pair_select(carry[0][0], carry[1][0], (t, LANES)).astype(BF16)
        dv_ref[...] = _pair_select(carry[0][1], carry[1][1], (t, LANES)).astype(BF16)

    kblk_spec = pl.BlockSpec((t, LANES), lambda j, i: (i, j))
    whole_t = pl.BlockSpec((LANES, s_len), lambda j, i: (j, 0))
    whole = pl.BlockSpec((s_len, LANES), lambda j, i: (0, j))
    rows = pl.BlockSpec((1, 8, s_len), lambda j, i: (j, 0, 0))
    return pl.pallas_call(
        body, name="flash_bwd_dkv", grid=(n_pair, n_k),
        in_specs=[kblk_spec, kblk_spec, kblk_spec, kblk_spec, whole, whole_t, whole, whole_t, kblk_spec, rows, rows],
        out_specs=[kblk_spec, kblk_spec],
        out_shape=[jax.ShapeDtypeStruct((s_len, d), BF16), jax.ShapeDtypeStruct((s_len, d), BF16)],
        compiler_params=_params(2),
    )(ka, kb_, va, vb, q, qt, do, dot_, cumx, rb_rows, delta_rows)


def _oproj_fwd(h, o, wo, tm):
    s_len, d = h.shape

    def body(h_ref, o_ref, wo_hbm, hout_ref, wo_v, sem):
        _load_once([(wo_hbm, wo_v)], sem)
        hout_ref[...] = h_ref[...] + _dot(o_ref[...].astype(BF16), wo_v[...])

    row = lambda i: (i, 0)
    return pl.pallas_call(
        body, name="oproj_fwd", grid=(s_len // tm,),
        in_specs=[pl.BlockSpec((tm, d), row), pl.BlockSpec((tm, d), row), ANY],
        out_specs=pl.BlockSpec((tm, d), row),
        out_shape=jax.ShapeDtypeStruct((s_len, d), F32),
        scratch_shapes=[pltpu.VMEM((d, d), BF16), pltpu.SemaphoreType.DMA((1,))],
        compiler_params=_params(1),
    )(h, o, wo)


def _oproj_bwd(dh, o, wo, seg, expand, tm):
    s_len, d = dh.shape

    def body(dh_ref, o_ref, wo_hbm, seg_ref, ex_ref, do_ref, dot_ref, dlx_ref, wo_v, sem):
        _load_once([(wo_hbm, wo_v)], sem)
        do = _dot_nt(dh_ref[...].astype(BF16), wo_v[...])
        do_ref[...] = do.astype(BF16)
        dot_ref[...] = do.T.astype(BF16)
        dlx_ref[...] = _dot3_lhs(_dot3_lhs(do * o_ref[...], seg_ref[...]), ex_ref[...])

    row = lambda i: (i, 0)
    const2 = lambda i: (0, 0)
    return pl.pallas_call(
        body, name="oproj_bwd", grid=(s_len // tm,),
        in_specs=[pl.BlockSpec((tm, d), row), pl.BlockSpec((tm, d), row), ANY,
                  pl.BlockSpec((d, LANES), const2), pl.BlockSpec((LANES, d), const2)],
        out_specs=[pl.BlockSpec((tm, d), row), pl.BlockSpec((d, tm), lambda i: (0, i)), pl.BlockSpec((tm, d), row)],
        out_shape=[jax.ShapeDtypeStruct((s_len, d), BF16), jax.ShapeDtypeStruct((d, s_len), BF16),
                   jax.ShapeDtypeStruct((s_len, d), F32)],
        scratch_shapes=[pltpu.VMEM((d, d), BF16), pltpu.SemaphoreType.DMA((1,))],
        compiler_params=_params(1),
    )(dh, o, wo, seg, expand)


def _forget_bwd(dcum, z, tm):
    s_len = dcum.shape[0]
    n_blk = s_len // tm

    def body(dc_ref, z_ref, dfl_ref, gb_ref, total):
        i = pl.program_id(0)

        @pl.when(i == 0)
        def _():
            total[...] = jnp.zeros_like(total)
            gb_ref[...] = jnp.zeros_like(gb_ref)

        upper = (lax.broadcasted_iota(jnp.int32, (tm, tm), 0) <= lax.broadcasted_iota(jnp.int32, (tm, tm), 1))
        suffix = _dot3_rhs(jnp.where(upper, 1.0, 0.0).astype(BF16), dc_ref[...]) + total[0:1, :]
        total[...] = jnp.broadcast_to(suffix[0:1, :], total.shape)
        dfl = suffix * _sigmoid(-z_ref[...])
        dfl_ref[...] = dfl
        gb_ref[...] += jnp.sum(dfl, axis=0, keepdims=True)

    rev = lambda i: (n_blk - 1 - i, 0)
    return pl.pallas_call(
        body, name="forget_bwd", grid=(n_blk,),
        in_specs=[pl.BlockSpec((tm, LANES), rev), pl.BlockSpec((tm, LANES), rev)],
        out_specs=[pl.BlockSpec((tm, LANES), rev), pl.BlockSpec((1, LANES), lambda i: (0, 0))],
        out_shape=[jax.ShapeDtypeStruct((s_len, LANES), F32), jax.ShapeDtypeStruct((1, LANES), F32)],
        scratch_shapes=[pltpu.VMEM((8, LANES), F32)],
        compiler_params=_params(1),
    )(dcum, z)


def _fox_proj_bwd(h, dh, dq, dk, dv, dfl, g_norm, wq, wk, wv, wf, tm):
    s_len, d = h.shape

    def body(h_ref, dh_ref, dq_ref, dk_ref, dv_ref, dfl_ref, g_ref, wq_hbm, wk_hbm, wv_hbm, wf_ref,
             dhin_ref, dflb_ref, gn_ref, wq_v, wk_v, wv_v, sem):
        _load_once([(wq_hbm, wq_v), (wk_hbm, wk_v), (wv_hbm, wv_v)], sem)

        @pl.when(pl.program_id(0) == 0)
        def _():
            gn_ref[...] = jnp.zeros_like(gn_ref)

        g = g_ref[...]
        xhat, r, _ = _rms_fwd(h_ref[...], g)
        dflb = dfl_ref[...].astype(BF16)
        dflb_ref[...] = dflb
        dhn = (_dot_nt(dq_ref[...], wq_v[...]) + _dot_nt(dk_ref[...], wk_v[...]) + _dot_nt(dv_ref[...], wv_v[...])
               + _dot_nt(dflb, wf_ref[...]))
        dx, gg = _rms_bwd(dhn, xhat, r, g)
        gn_ref[...] += jnp.sum(gg, axis=0, keepdims=True)
        dhin_ref[...] = dh_ref[...] + dx

    row = lambda i: (i, 0)
    const2 = lambda i: (0, 0)
    rs = pl.BlockSpec((tm, d), row)
    return pl.pallas_call(
        body, name="fox_proj_bwd", grid=(s_len // tm,),
        in_specs=[rs, rs, rs, rs, rs, pl.BlockSpec((tm, LANES), row), pl.BlockSpec((1, d), const2), ANY, ANY, ANY,
                  pl.BlockSpec((d, LANES), const2)],
        out_specs=[rs, pl.BlockSpec((tm, LANES), row), pl.BlockSpec((1, d), const2)],
        out_shape=[jax.ShapeDtypeStruct((s_len, d), F32), jax.ShapeDtypeStruct((s_len, LANES), BF16),
                   jax.ShapeDtypeStruct((1, d), F32)],
        scratch_shapes=[pltpu.VMEM((d, d), BF16), pltpu.VMEM((d, d), BF16), pltpu.VMEM((d, d), BF16),
                        pltpu.SemaphoreType.DMA((3,))],
        compiler_params=_params(1),
    )(h, dh, dq, dk, dv, dfl, g_norm, wq, wk, wv, wf)


def _loss_head(h, target, g_final, tm):
    s_len, d = h.shape
    n_blk = s_len // tm

    def body(h_ref, t_ref, g_ref, dh_ref, loss_ref, gg_ref, sq):
        i = pl.program_id(0)

        @pl.when(i == 0)
        def _():
            sq[...] = jnp.zeros_like(sq)
            gg_ref[...] = jnp.zeros_like(gg_ref)

        g = g_ref[...]
        xhat, r, y = _rms_fwd(h_ref[...], g)
        err = y - t_ref[...]
        sq[...] += jnp.sum(err * err, axis=0, keepdims=True)
        dx, gg = _rms_bwd(err * (1.0 / d), xhat, r, g)
        gg_ref[...] += jnp.sum(gg, axis=0, keepdims=True)
        dh_ref[...] = dx

        @pl.when(i == n_blk - 1)
        def _():
            loss_ref[...] = jnp.broadcast_to(jnp.sum(sq[...], axis=1, keepdims=True) * (0.5 / d), loss_ref.shape)

    row = lambda i: (i, 0)
    const2 = lambda i: (0, 0)
    return pl.pallas_call(
        body, name="loss_head", grid=(n_blk,),
        in_specs=[pl.BlockSpec((tm, d), row), pl.BlockSpec((tm, d), row), pl.BlockSpec((1, d), const2)],
        out_specs=[pl.BlockSpec((tm, d), row), pl.BlockSpec((1, LANES), const2), pl.BlockSpec((1, d), const2)],
        out_shape=[jax.ShapeDtypeStruct((s_len, d), F32), jax.ShapeDtypeStruct((1, LANES), F32),
                   jax.ShapeDtypeStruct((1, d), F32)],
        scratch_shapes=[pltpu.VMEM((1, d), F32)],
        compiler_params=_params(1),
    )(h, target, g_final)


def _wgrad(x, dy, n_piece, name):
    s_len, k = x.shape
    n = dy.shape[1]
    tn = min(n // n_piece, 1024)
    tk = min(k, 1024)
    ts = 512
    per_piece = (n // n_piece) // tn

    def body(x_ref, dy_ref, o_ref):
        @pl.when(pl.program_id(2) == 0)
        def _():
            o_ref[...] = jnp.zeros_like(o_ref)
        o_ref[0] += _dot_tn(x_ref[...].astype(BF16), dy_ref[...].astype(BF16))

    return pl.pallas_call(
        body, name=name, grid=(k // tk, n // tn, s_len // ts),
        in_specs=[pl.BlockSpec((ts, tk), lambda a, b, c: (c, a)), pl.BlockSpec((ts, tn), lambda a, b, c: (c, b))],
        out_specs=pl.BlockSpec((1, tk, tn), lambda a, b, c: (b // per_piece, a, b % per_piece)),
        out_shape=jax.ShapeDtypeStruct((n_piece, k, n // n_piece), F32),
        compiler_params=_params(3),
    )(x, dy)


def _sum_slots(b, name):
    n, rows, c = b.shape
    tr = min(rows, 512)

    def body(b_ref, o_ref):
        acc = b_ref[0]
        for k in range(1, n):
            acc = acc + b_ref[k]
        o_ref[...] = acc

    return pl.pallas_call(
        body, name=name, grid=(rows // tr,),
        in_specs=[pl.BlockSpec((n, tr, c), lambda i: (0, i, 0))],
        out_specs=pl.BlockSpec((tr, c), lambda i: (i, 0)),
        out_shape=jax.ShapeDtypeStruct((rows, c), F32),
        compiler_params=_params(1),
    )(b)


def _adamw(w, m, v, g, name):
    rows, c = w.shape
    tr = min(rows, 256)

    def body(w_ref, m_ref, v_ref, g_ref, d_ref, mo_ref, vo_ref):
        gv = g_ref[...]
        m_new = ADAM_B1 * m_ref[...] + (1.0 - ADAM_B1) * gv
        v_new = ADAM_B2 * v_ref[...] + (1.0 - ADAM_B2) * (gv * gv)
        m_hat = m_new / (1.0 - ADAM_B1 ** ADAM_STEP)
        v_hat = v_new / (1.0 - ADAM_B2 ** ADAM_STEP)
        d_ref[...] = -ADAM_LR * (m_hat / (jnp.sqrt(v_hat) + ADAM_EPS) + ADAM_WD * w_ref[...])
        mo_ref[...] = m_new
        vo_ref[...] = v_new

    spec = pl.BlockSpec((tr, c), lambda i: (i, 0))
    shape = jax.ShapeDtypeStruct((rows, c), F32)
    return pl.pallas_call(
        body, name=name, grid=(rows // tr,),
        in_specs=[spec] * 4, out_specs=[spec] * 3, out_shape=[shape] * 3,
        compiler_params=_params(1),
    )(w, m, v, g)


def _place():
    x, y, c = lax.axis_index("x"), lax.axis_index("y"), lax.axis_index("c")
    chips = [(1 - x, y), (x, 1 - y), (1 - x, 1 - y)]
    return x, y, c, chips


def _all_gather_chips(shards):
    n = len(shards)

    def body(*refs):
        ins, outs = refs[:n], refs[n:2 * n]
        send_sems, recv_sems, local_sems = refs[2 * n:]
        x, y, c, chips = _place()
        mine = 2 * x + y
        local = [pltpu.make_async_copy(ins[k], outs[k].at[mine], local_sems.at[k]) for k in range(n)]
        for cp in local:
            cp.start()
        sends = []
        for k in range(n):
            for j, (tx, ty) in enumerate(chips):
                sends.append(pltpu.make_async_remote_copy(
                    src_ref=ins[k], dst_ref=outs[k].at[mine], send_sem=send_sems.at[k, j], recv_sem=recv_sems.at[k, j],
                    device_id=(tx, ty, c), device_id_type=MESH))
        for cp in sends:
            cp.start()
        for k in range(n):
            for j, (tx, ty) in enumerate(chips):
                pltpu.make_async_remote_copy(
                    src_ref=ins[k], dst_ref=outs[k].at[2 * tx + ty], send_sem=send_sems.at[k, j],
                    recv_sem=recv_sems.at[k, j], device_id=(tx, ty, c), device_id_type=MESH).wait()
        for cp in local:
            cp.wait()

    return pl.pallas_call(
        body, name="weights_all_gather",
        in_specs=[ANY] * n, out_specs=[ANY] * n,
        out_shape=[jax.ShapeDtypeStruct((4,) + s.shape, s.dtype) for s in shards],
        scratch_shapes=[pltpu.SemaphoreType.DMA((n, 3)), pltpu.SemaphoreType.DMA((n, 3)), pltpu.SemaphoreType.DMA((n,))],
    )(*shards)


def _pair_exchange(grads):
    n = len(grads)

    def body(*refs):
        ins, outs = refs[:n], refs[n:2 * n]
        send_sems, recv_sems, local_sems = refs[2 * n:]
        x, y, c, _ = _place()
        local, remote = [], []
        for k in range(n):
            half = grads[k].shape[1] // 2
            own = ins[k].at[:, pl.ds(pl.multiple_of(c * half, 8), half), :]
            other = ins[k].at[:, pl.ds(pl.multiple_of((1 - c) * half, 8), half), :]
            local.append(pltpu.make_async_copy(own, outs[k].at[c], local_sems.at[k]))
            remote.append(pltpu.make_async_remote_copy(
                src_ref=other, dst_ref=outs[k].at[c], send_sem=send_sems.at[k], recv_sem=recv_sems.at[k],
                device_id=(x, y, 1 - c), device_id_type=MESH))
        for cp in local + remote:
            cp.start()
        for k in range(n):
            half = grads[k].shape[1] // 2
            pltpu.make_async_remote_copy(
                src_ref=ins[k].at[:, pl.ds(0, half), :], dst_ref=outs[k].at[1 - c], send_sem=send_sems.at[k],
                recv_sem=recv_sems.at[k], device_id=(x, y, 1 - c), device_id_type=MESH).wait()
        for cp in local:
            cp.wait()

    return pl.pallas_call(
        body, name="grads_pair_exchange",
        in_specs=[ANY] * n, out_specs=[ANY] * n,
        out_shape=[jax.ShapeDtypeStruct((2, 4, g.shape[1] // 2, g.shape[2]), F32) for g in grads],
        scratch_shapes=[pltpu.SemaphoreType.DMA((n,)), pltpu.SemaphoreType.DMA((n,)), pltpu.SemaphoreType.DMA((n,))],
    )(*grads)


def _chip_scatter(halves):
    n = len(halves)

    def body(*refs):
        ins, outs = refs[:n], refs[n:2 * n]
        send_sems, recv_sems, local_sems = refs[2 * n:]
        x, y, c, chips = _place()
        mine = 2 * x + y
        local = [pltpu.make_async_copy(ins[k].at[mine], outs[k].at[0], local_sems.at[k]) for k in range(n)]
        for cp in local:
            cp.start()
        sends = []
        for k in range(n):
            for j, (tx, ty) in enumerate(chips):
                sends.append(pltpu.make_async_remote_copy(
                    src_ref=ins[k].at[2 * tx + ty], dst_ref=outs[k].at[1 + j], send_sem=send_sems.at[k, j],
                    recv_sem=recv_sems.at[k, j], device_id=(tx, ty, c), device_id_type=MESH))
        for cp in sends:
            cp.start()
        for cp in sends:
            cp.wait()
        for cp in local:
            cp.wait()

    return pl.pallas_call(
        body, name="grads_chip_scatter",
        in_specs=[ANY] * n, out_specs=[ANY] * n,
        out_shape=[jax.ShapeDtypeStruct(hv.shape, F32) for hv in halves],
        scratch_shapes=[pltpu.SemaphoreType.DMA((n, 3)), pltpu.SemaphoreType.DMA((n, 3)), pltpu.SemaphoreType.DMA((n,))],
    )(*halves)


def _pair_share(finals):
    n = len(finals)

    def body(*refs):
        ins, outs = refs[:n], refs[n:2 * n]
        send_sems, recv_sems, local_sems = refs[2 * n:]
        x, y, c, _ = _place()
        local, remote = [], []
        for k in range(n):
            half = finals[k].shape[0]
            rows = outs[k].at[pl.ds(pl.multiple_of(c * half, 8), half), :]
            local.append(pltpu.make_async_copy(ins[k], rows, local_sems.at[k]))
            remote.append(pltpu.make_async_remote_copy(
                src_ref=ins[k], dst_ref=rows, send_sem=send_sems.at[k], recv_sem=recv_sems.at[k],
                device_id=(x, y, 1 - c), device_id_type=MESH))
        for cp in local + remote:
            cp.start()
        for cp in remote + local:
            cp.wait()

    return pl.pallas_call(
        body, name="grads_pair_share",
        in_specs=[ANY] * n, out_specs=[ANY] * n,
        out_shape=[jax.ShapeDtypeStruct((2 * fv.shape[0], fv.shape[1]), F32) for fv in finals],
        scratch_shapes=[pltpu.SemaphoreType.DMA((n,)), pltpu.SemaphoreType.DMA((n,)), pltpu.SemaphoreType.DMA((n,))],
    )(*finals)


def _small_all_reduce(buf):
    rows, c_ = buf.shape

    def body(in_ref, out_ref, pair_buf, slots, send_sems, recv_sems):
        x, y, c, chips = _place()
        mine = 2 * x + y
        pair = pltpu.make_async_remote_copy(
            src_ref=in_ref, dst_ref=pair_buf, send_sem=send_sems.at[0], recv_sem=recv_sems.at[0],
            device_id=(x, y, 1 - c), device_id_type=MESH)
        pair.start()
        pair.wait()
        slots[mine] = in_ref[...] + pair_buf[...]
        sends = [pltpu.make_async_remote_copy(
            src_ref=slots.at[mine], dst_ref=slots.at[mine], send_sem=send_sems.at[1 + j], recv_sem=recv_sems.at[1 + j],
            device_id=(tx, ty, c), device_id_type=MESH) for j, (tx, ty) in enumerate(chips)]
        for cp in sends:
            cp.start()
        for j, (tx, ty) in enumerate(chips):
            pltpu.make_async_remote_copy(
                src_ref=slots.at[mine], dst_ref=slots.at[2 * tx + ty], send_sem=send_sems.at[1 + j],
                recv_sem=recv_sems.at[1 + j], device_id=(tx, ty, c), device_id_type=MESH).wait()
        out_ref[...] = ((slots[0] + slots[1]) + slots[2]) + slots[3]

    vm = pl.BlockSpec(memory_space=pltpu.VMEM)
    return pl.pallas_call(
        body, name="small_all_reduce", in_specs=[vm], out_specs=vm,
        out_shape=jax.ShapeDtypeStruct((rows, c_), F32),
        scratch_shapes=[pltpu.VMEM((rows, c_), F32), pltpu.VMEM((4, rows, c_), F32),
                        pltpu.SemaphoreType.DMA((4,)), pltpu.SemaphoreType.DMA((4,))],
        compiler_params=pltpu.CompilerParams(vmem_limit_bytes=VMEM_LIMIT_V7X),
    )(buf)


def _reduce_scatter(grads):
    pairs = _pair_exchange(grads)
    halves = [_sum_slots(p.reshape(2, -1, p.shape[-1]), f"pair_sum_{k}").reshape(p.shape[1:]) for k, p in enumerate(pairs)]
    slots = _chip_scatter(halves)
    finals = [_sum_slots(s, f"chip_sum_{k}") for k, s in enumerate(slots)]
    return _pair_share(finals)


PACK_COLS = 1024


def _pack(arrays):
    flat = jnp.concatenate([a.reshape(-1).astype(F32) for a in arrays])
    rows = -(-flat.shape[0] // PACK_COLS)
    rows = -(-rows // 8) * 8
    return jnp.pad(flat, (0, rows * PACK_COLS - flat.shape[0])).reshape(rows, PACK_COLS)


def _unpack(buf, shapes):
    flat = buf.reshape(-1)
    out, at = [], 0
    for shp in shapes:
        size = math.prod(shp)
        out.append(flat[at:at + size].reshape(shp))
        at += size
    return out


def kernel(x, mix_norm_g, ffn_norm_g, gm_w_in, gm_ln_g, gm_ln_b, gm_w_s, gm_b_s, gm_w_out, fox_w_qkvf, fox_b_f, fox_w_o, ffn_w_gate, ffn_w_up, ffn_conv_w, ffn_conv_b, ffn_w_down, final_norm_g, loss_target, m_mix_norm_g, m_ffn_norm_g, m_gm_w_in, m_gm_ln_g, m_gm_ln_b, m_gm_w_s, m_gm_b_s, m_gm_w_out, m_fox_w_qkvf, m_fox_b_f, m_fox_w_o, m_ffn_w_gate, m_ffn_w_up, m_ffn_conv_w, m_ffn_conv_b, m_ffn_w_down, m_final_norm_g, v_mix_norm_g, v_ffn_norm_g, v_gm_w_in, v_gm_ln_g, v_gm_ln_b, v_gm_w_s, v_gm_b_s, v_gm_w_out, v_fox_w_qkvf, v_fox_b_f, v_fox_w_o, v_ffn_w_gate, v_ffn_w_up, v_ffn_conv_w, v_ffn_conv_b, v_ffn_w_down, v_final_norm_g):
    _, s_len, d = x.shape
    e = gm_ln_g.shape[1]
    f = ffn_conv_b.shape[1]
    n_head = fox_b_f.shape[1]
    n_pair = n_head // 2
    gd = e // GM_GROUPS
    qkvf_cols = fox_w_qkvf.shape[2]
    assert d == n_head * HEAD_DIM and d % (2 * LANES) == 0 and s_len % 512 == 0 and gd % LANES == 0
    assert gm_w_s.shape[2] == CHUNK and 4 * qkvf_cols == 3 * d + n_head
    tm = 256
    h0 = x[0]
    target = loss_target[0]

    gathered = _all_gather_chips([
        gm_w_in[0].astype(BF16), gm_w_out[0].astype(BF16), fox_w_qkvf[0].astype(BF16), fox_w_o[0].astype(BF16),
        ffn_w_gate.astype(BF16), ffn_w_up.astype(BF16), ffn_w_down.astype(BF16), ffn_conv_w])
    w_in, w_out4, qkvf4, wo4, wg_all, wu_all, wd_all, cw4 = gathered
    w_out = w_out4.reshape(e, d)
    qkvf = jnp.transpose(qkvf4, (1, 0, 2)).reshape(d, 4 * qkvf_cols)
    wq, wk, wv = qkvf[:, :d], qkvf[:, d:2 * d], qkvf[:, 2 * d:3 * d]
    wf = jnp.pad(qkvf[:, 3 * d:], ((0, 0), (0, LANES - n_head)))
    wo = wo4.reshape(d, d)
    conv_w_full = jnp.transpose(cw4, (1, 2, 0, 3)).reshape(2, 3, f)
    conv_w8 = jnp.pad(conv_w_full, ((0, 0), (0, 5), (0, 0)))
    bf_pad = jnp.pad(fox_b_f, ((0, 0), (0, LANES - n_head)))

    tril = jnp.tril(jnp.ones((CHUNK, CHUNK), bool))
    wc = jnp.where(tril[None], gm_w_s[0], 0.0).astype(BF16)
    wct = jnp.transpose(wc, (0, 2, 1))
    bias = jnp.repeat(gm_b_s[0].T, gd, axis=1)
    seg_groups = (jnp.arange(e)[:, None] // gd == jnp.arange(LANES)[None, :]).astype(BF16)
    seg_heads = (jnp.arange(d)[:, None] // HEAD_DIM == jnp.arange(LANES)[None, :]).astype(BF16)
    expand_heads = seg_heads.T

    h1, a0, hn0, gated0 = _gmlp_fwd(h0, mix_norm_g[0:1], w_in, gm_ln_g, gm_ln_b, wc, bias, w_out, tm)
    h2, fa0, fup0, fhn0, fhid0 = _ffn_fwd(h1, ffn_norm_g[0:1], wg_all, wu_all, wd_all, 0, conv_w8[0], ffn_conv_b[0:1], tm)
    (hn1, q, qt, ka, kb_, kat, kbt, va, vb, vat, vbt, z_f, cumx, cum) = _fox_proj_fwd(
        h2, mix_norm_g[1:2], wq, wk, wv, wf, bf_pad, expand_heads, tm)

    def pair_rows(cols):
        r = cols[:, :n_head].T.reshape(n_pair, 2, s_len)
        return jnp.pad(r, ((0, 0), (0, 6), (0, 0)))

    cum_rows = pair_rows(cum)
    o, rbx = _flash_fwd(q, kat, kbt, va, vb, cumx, cum_rows)
    h3 = _oproj_fwd(h2, o, wo, tm)
    h4, fa1, fup1, fhn1, fhid1 = _ffn_fwd(h3, ffn_norm_g[1:2], wg_all, wu_all, wd_all, 1, conv_w8[1], ffn_conv_b[1:2], tm)

    dh4, loss_part, g_final = _loss_head(h4, target, final_norm_g.reshape(1, d), tm)
    dh3, da1, dup1, gcw1, gcb1, gfn1 = _ffn_bwd(h3, dh4, fa1, fup1, ffn_norm_g[1:2], wg_all, wu_all, wd_all, 1,
                                                conv_w8[1], ffn_conv_b[1:2], tm)
    g_gate1 = _wgrad(fhn1, da1, 4, "wgrad_gate_1")
    g_up1 = _wgrad(fhn1, dup1, 4, "wgrad_up_1")
    g_down1 = _wgrad(fhid1, dh4, 1, "wgrad_down_1").reshape(4, f // 4, d)

    do, dot_, deltax = _oproj_bwd(dh3, o, wo, seg_heads, expand_heads, tm)
    g_wo = _wgrad(o, dh3, 1, "wgrad_wo").reshape(4, d // 4, d)
    dq, dck_rows, dcqx = _flash_bwd_dq(q, kat, kbt, ka, kb_, vat, vbt, do, rbx, deltax, cum_rows)
    rb_rows = pair_rows(rbx[:, ::HEAD_DIM])
    delta_rows = pair_rows(deltax[:, ::HEAD_DIM])
    dk, dv = _flash_bwd_dkv(ka, kb_, va, vb, q, qt, do, dot_, cumx, rb_rows, delta_rows)
    dcum = jnp.pad(dck_rows[:, :2, :].reshape(n_head, s_len).T + dcqx[:, ::HEAD_DIM], ((0, 0), (0, LANES - n_head)))
    dfl, g_bf = _forget_bwd(dcum, z_f, tm)
    dh2, dflb, gmn1 = _fox_proj_bwd(h2, dh3, dq, dk, dv, dfl, mix_norm_g[1:2], wq, wk, wv, wf, tm)
    g_q = _wgrad(hn1, dq, 1, "wgrad_q")[0]
    g_k = _wgrad(hn1, dk, 1, "wgrad_k")[0]
    g_v = _wgrad(hn1, dv, 1, "wgrad_v")[0]
    g_f = _wgrad(hn1, dflb, 1, "wgrad_f")[0][:, :n_head]
    g_qkvf = jnp.concatenate([g_q, g_k, g_v, g_f], axis=1).reshape(d, 4, qkvf_cols).transpose(1, 0, 2)

    dh1, da0f, dup0, gcw0, gcb0, gfn0 = _ffn_bwd(h1, dh2, fa0, fup0, ffn_norm_g[0:1], wg_all, wu_all, wd_all, 0,
                                                 conv_w8[0], ffn_conv_b[0:1], tm)
    g_gate0 = _wgrad(fhn0, da0f, 4, "wgrad_gate_0")
    g_up0 = _wgrad(fhn0, dup0, 4, "wgrad_up_0")
    g_down0 = _wgrad(fhid0, dh2, 1, "wgrad_down_0").reshape(4, f // 4, d)

    dh0, da0, g_ws, g_bs_t, g_lng, g_lnb, gmn0 = _gmlp_bwd(
        h0, dh1, a0, mix_norm_g[0:1], w_in, gm_ln_g, gm_ln_b, wc, wct, bias, w_out, seg_groups, CHUNK)
    g_win = _wgrad(hn0, da0, 4, "wgrad_gm_in")
    g_wout = _wgrad(gated0, dh1, 1, "wgrad_gm_out").reshape(4, e // 4, d)

    big = _reduce_scatter([g_win, g_wout, g_qkvf, g_wo, g_gate0, g_gate1, g_up0, g_up1, g_down0, g_down1])
    r_win, r_wout, r_qkvf, r_wo, r_gate0, r_gate1, r_up0, r_up1, r_down0, r_down1 = big

    small = [jnp.concatenate([gmn0, gmn1]), jnp.concatenate([gfn0, gfn1]), g_lng, g_lnb, g_ws[None],
             g_bs_t[:, :GM_GROUPS].T[None], g_bf[:, :n_head], jnp.stack([gcw0[:3], gcw1[:3]]),
             jnp.concatenate([gcb0, gcb1]), g_final.reshape(d), loss_part[0, :1]]
    small_shapes = [a.shape for a in small]
    reduced = _unpack(_small_all_reduce(_pack(small)), small_shapes)
    (r_mix, r_ffn, r_lng, r_lnb, r_ws, r_bs, r_bf, r_cw_full, r_cb, r_final, r_loss) = reduced
    chip = 2 * lax.axis_index("x") + lax.axis_index("y")
    r_cw = lax.dynamic_slice_in_dim(r_cw_full, chip * (f // 4), f // 4, axis=2)

    def update_big(name, w, m, v, g):
        shp = w.shape
        two = lambda t: t.reshape(-1, shp[-1])
        dlt, mn, vn = _adamw(two(w), two(m), two(v), two(g), f"adamw_{name}")
        return g.reshape(shp), dlt.reshape(shp), mn.reshape(shp), vn.reshape(shp)

    res = {}
    res["gm_w_in"] = update_big("gm_w_in", gm_w_in, m_gm_w_in, v_gm_w_in, r_win)
    res["gm_w_out"] = update_big("gm_w_out", gm_w_out, m_gm_w_out, v_gm_w_out, r_wout)
    res["fox_w_qkvf"] = update_big("fox_w_qkvf", fox_w_qkvf, m_fox_w_qkvf, v_fox_w_qkvf, r_qkvf)
    res["fox_w_o"] = update_big("fox_w_o", fox_w_o, m_fox_w_o, v_fox_w_o, r_wo)
    res["ffn_w_gate"] = update_big("ffn_w_gate", ffn_w_gate, m_ffn_w_gate, v_ffn_w_gate, jnp.stack([r_gate0, r_gate1]))
    res["ffn_w_up"] = update_big("ffn_w_up", ffn_w_up, m_ffn_w_up, v_ffn_w_up, jnp.stack([r_up0, r_up1]))
    res["ffn_w_down"] = update_big("ffn_w_down", ffn_w_down, m_ffn_w_down, v_ffn_w_down, jnp.stack([r_down0, r_down1]))

    small_names = ["mix_norm_g", "ffn_norm_g", "gm_ln_g", "gm_ln_b", "gm_w_s", "gm_b_s", "fox_b_f", "ffn_conv_w",
                   "ffn_conv_b", "final_norm_g"]
    small_w = [mix_norm_g, ffn_norm_g, gm_ln_g, gm_ln_b, gm_w_s, gm_b_s, fox_b_f, ffn_conv_w, ffn_conv_b, final_norm_g]
    small_m = [m_mix_norm_g, m_ffn_norm_g, m_gm_ln_g, m_gm_ln_b, m_gm_w_s, m_gm_b_s, m_fox_b_f, m_ffn_conv_w,
               m_ffn_conv_b, m_final_norm_g]
    small_v = [v_mix_norm_g, v_ffn_norm_g, v_gm_ln_g, v_gm_ln_b, v_gm_w_s, v_gm_b_s, v_fox_b_f, v_ffn_conv_w,
               v_ffn_conv_b, v_final_norm_g]
    small_g = [r_mix, r_ffn, r_lng, r_lnb, r_ws, r_bs, r_bf, r_cw, r_cb, r_final]
    shapes = [w.shape for w in small_w]
    small_g = [g.reshape(s) for g, s in zip(small_g, shapes)]
    dlt, mn, vn = _adamw(_pack(small_w), _pack(small_m), _pack(small_v), _pack(small_g), "adamw_small")
    for name, g, dl_, m_, v_ in zip(small_names, small_g, _unpack(dlt, shapes), _unpack(mn, shapes), _unpack(vn, shapes)):
        res[name] = (g, dl_, m_, v_)

    order = ["mix_norm_g", "ffn_norm_g", "gm_w_in", "gm_ln_g", "gm_ln_b", "gm_w_s", "gm_b_s", "gm_w_out", "fox_w_qkvf",
             "fox_b_f", "fox_w_o", "ffn_w_gate", "ffn_w_up", "ffn_conv_w", "ffn_conv_b", "ffn_w_down", "final_norm_g"]
    outs = [r_loss.reshape(()), dh0[None]]
    for part in range(4):
        outs += [res[name][part] for name in order]
    return tuple(outs)
```

```python
import functools
import math

import jax
import jax.numpy as jnp
from jax import lax
from jax.experimental import pallas as pl
from jax.experimental.pallas import tpu as pltpu

F32 = jnp.float32
BF16 = jnp.bfloat16

RMS_EPS = 1e-6
LN_EPS = 1e-5
CHUNK = 128
GM_GROUPS = 8
HEAD_DIM = 64
LANES = 128
ATT_BLOCK = 256
ATT_CHUNK = 1024
VMEM_LIMIT_V7X = 56 * 1024 * 1024

ADAM_LR = 0.001
ADAM_B1 = 0.9
ADAM_B2 = 0.999
ADAM_EPS = 1e-08
ADAM_WD = 0.01
ADAM_STEP = 10

MESH = pl.DeviceIdType.MESH
ANY = pl.BlockSpec(memory_space=pl.ANY)
NEG_BIG = -1e30


def _params(n_grid):
    return pltpu.CompilerParams(dimension_semantics=("arbitrary",) * n_grid, vmem_limit_bytes=VMEM_LIMIT_V7X)


def _dot(a, b):
    return jnp.dot(a, b, preferred_element_type=F32)


def _dot_nt(a, b):
    return lax.dot_general(a, b, (((1,), (1,)), ((), ())), preferred_element_type=F32)


def _dot_tn(a, b):
    return lax.dot_general(a, b, (((0,), (0,)), ((), ())), preferred_element_type=F32)


def _split3(x):
    hi = x.astype(BF16)
    r = x - hi.astype(F32)
    mid = r.astype(BF16)
    lo = (r - mid.astype(F32)).astype(BF16)
    return hi, mid, lo


def _dot3_lhs(x, m):
    hi, mid, lo = _split3(x)
    return _dot(hi, m) + _dot(mid, m) + _dot(lo, m)


def _dot3_rhs(m, x):
    hi, mid, lo = _split3(x)
    return _dot(m, hi) + _dot(m, mid) + _dot(m, lo)


def _load_once(pairs, sem):
    @pl.when(pl.program_id(0) == 0)
    def _():
        copies = [pltpu.make_async_copy(src, dst, sem.at[k]) for k, (src, dst) in enumerate(pairs)]
        for cp in copies:
            cp.start()
        for cp in copies:
            cp.wait()


def _rms_fwd(x, g):
    r = lax.rsqrt(jnp.mean(x * x, axis=-1, keepdims=True) + RMS_EPS)
    xhat = x * r
    return xhat, r, xhat * g


def _rms_bwd(dy, xhat, r, g):
    w = dy * g
    dx = r * (w - xhat * jnp.mean(w * xhat, axis=-1, keepdims=True))
    return dx, dy * xhat


def _gelu_parts(a):
    c = math.sqrt(2.0 / math.pi)
    a2 = a * a
    t = jnp.tanh(c * (a + 0.044715 * a * a2))
    z = 0.5 * a * (1.0 + t)
    dz = 0.5 * (1.0 + t) + 0.5 * a * (1.0 - t * t) * (c * (1.0 + 3.0 * 0.044715 * a2))
    return z, dz


def _sigmoid(x):
    return 1.0 / (1.0 + jnp.exp(-x))


def _gmlp_core(a, lng, lnb, wc_ref, bias, n_chunk, gd):
    e = a.shape[1] // 2
    z, dz = _gelu_parts(a)
    u = z[:, :e]
    v = z[:, e:]
    mu = jnp.mean(v, axis=-1, keepdims=True)
    vc = v - mu
    rstd = lax.rsqrt(jnp.mean(vc * vc, axis=-1, keepdims=True) + LN_EPS)
    vhat = vc * rstd
    vln = vhat * lng + lnb
    vlb = vln.astype(BF16)
    rows = []
    for ci in range(n_chunk):
        cols = []
        for g in range(GM_GROUPS):
            blk = vlb[ci * CHUNK:(ci + 1) * CHUNK, g * gd:(g + 1) * gd]
            cols.append(_dot(wc_ref[g], blk))
        rows.append(jnp.concatenate(cols, axis=1) + bias)
    s = rows[0] if n_chunk == 1 else jnp.concatenate(rows, axis=0)
    return dz, u, vhat, rstd, vlb, s


def _gmlp_fwd(h, g_mix, w_in, lng, lnb, wc, bias, w_out, tm):
    s_len, d = h.shape
    n_p, _, w = w_in.shape
    e = w_out.shape[0]
    gd = e // GM_GROUPS
    n_chunk = tm // CHUNK

    def body(h_ref, g_ref, win_hbm, lng_ref, lnb_ref, wc_ref, bias_ref, wout_hbm,
             hout_ref, a_ref, hn_ref, gated_ref, win_v, wout_v, sem):
        _load_once([(win_hbm, win_v), (wout_hbm, wout_v)], sem)
        x = h_ref[...]
        _, _, y = _rms_fwd(x, g_ref[...])
        hn = y.astype(BF16)
        hn_ref[...] = hn
        for p in range(n_p):
            a_ref[:, p * w:(p + 1) * w] = _dot(hn, win_v[p])
        _, u, _, _, _, s = _gmlp_core(a_ref[...], lng_ref[...], lnb_ref[...], wc_ref, bias_ref[...], n_chunk, gd)
        gated = (u * s).astype(BF16)
        gated_ref[...] = gated
        hout_ref[...] = x + _dot(gated, wout_v[...])

    row = lambda i: (i, 0)
    const2 = lambda i: (0, 0)
    return pl.pallas_call(
        body, name="gmlp_fwd", grid=(s_len // tm,),
        in_specs=[pl.BlockSpec((tm, d), row), pl.BlockSpec((1, d), const2), ANY,
                  pl.BlockSpec((1, e), const2), pl.BlockSpec((1, e), const2),
                  pl.BlockSpec(wc.shape, lambda i: (0, 0, 0)), pl.BlockSpec((CHUNK, e), const2), ANY],
        out_specs=[pl.BlockSpec((tm, d), row), pl.BlockSpec((tm, 2 * e), row),
                   pl.BlockSpec((tm, d), row), pl.BlockSpec((tm, e), row)],
        out_shape=[jax.ShapeDtypeStruct((s_len, d), F32), jax.ShapeDtypeStruct((s_len, 2 * e), F32),
                   jax.ShapeDtypeStruct((s_len, d), BF16), jax.ShapeDtypeStruct((s_len, e), BF16)],
        scratch_shapes=[pltpu.VMEM(w_in.shape, BF16), pltpu.VMEM(w_out.shape, BF16), pltpu.SemaphoreType.DMA((2,))],
        compiler_params=_params(1),
    )(h, g_mix, w_in, lng, lnb, wc, bias, w_out)


def _gmlp_bwd(h, dh, a, g_mix, w_in, lng, lnb, wc, wct, bias, w_out, seg, tm):
    s_len, d = h.shape
    n_p, _, w = w_in.shape
    e = w_out.shape[0]
    gd = e // GM_GROUPS
    n_chunk = tm // CHUNK
    n_blk = s_len // tm

    def body(h_ref, dh_ref, a_ref, g_ref, win_hbm, lng_ref, lnb_ref, wc_ref, wct_ref, bias_ref, wout_hbm, seg_ref,
             dhin_ref, da_ref, gws_ref, gbs_ref, glng_ref, glnb_ref, gmix_ref, win_v, wout_v, dsum, sem):
        i = pl.program_id(0)
        _load_once([(win_hbm, win_v), (wout_hbm, wout_v)], sem)

        @pl.when(i == 0)
        def _():
            gws_ref[...] = jnp.zeros_like(gws_ref)
            glng_ref[...] = jnp.zeros_like(glng_ref)
            glnb_ref[...] = jnp.zeros_like(glnb_ref)
            gmix_ref[...] = jnp.zeros_like(gmix_ref)
            dsum[...] = jnp.zeros_like(dsum)

        x = h_ref[...]
        dh_v = dh_ref[...]
        g = g_ref[...]
        lng_v = lng_ref[...]
        xhat, r, _ = _rms_fwd(x, g)
        dz_da, u, vhat, rstd, vlb, s = _gmlp_core(a_ref[...], lng_v, lnb_ref[...], wc_ref, bias_ref[...], n_chunk, gd)
        dg = _dot_nt(dh_v.astype(BF16), wout_v[...])
        du = dg * s
        ds = dg * u
        dsb = ds.astype(BF16)
        rows = []
        ds_acc = None
        for ci in range(n_chunk):
            lo, hi = ci * CHUNK, (ci + 1) * CHUNK
            cols = []
            for gi in range(GM_GROUPS):
                d_blk = dsb[lo:hi, gi * gd:(gi + 1) * gd]
                gws_ref[gi] += _dot_nt(d_blk, vlb[lo:hi, gi * gd:(gi + 1) * gd])
                cols.append(_dot(wct_ref[gi], d_blk))
            rows.append(jnp.concatenate(cols, axis=1))
            ds_acc = ds[lo:hi] if ds_acc is None else ds_acc + ds[lo:hi]
        dsum[...] += ds_acc
        dvln = rows[0] if n_chunk == 1 else jnp.concatenate(rows, axis=0)
        glng_ref[...] += jnp.sum(dvln * vhat, axis=0, keepdims=True)
        glnb_ref[...] += jnp.sum(dvln, axis=0, keepdims=True)
        dvhat = dvln * lng_v
        dv = rstd * (dvhat - jnp.mean(dvhat, axis=-1, keepdims=True)
                     - vhat * jnp.mean(dvhat * vhat, axis=-1, keepdims=True))
        da = jnp.concatenate([du, dv], axis=1) * dz_da
        dab = da.astype(BF16)
        da_ref[...] = dab
        dhn = _dot_nt(dab[:, :w], win_v[0])
        for p in range(1, n_p):
            dhn += _dot_nt(dab[:, p * w:(p + 1) * w], win_v[p])
        dx, gg = _rms_bwd(dhn, xhat, r, g)
        gmix_ref[...] += jnp.sum(gg, axis=0, keepdims=True)
        dhin_ref[...] = dh_v + dx

        @pl.when(i == n_blk - 1)
        def _():
            tril = lax.broadcasted_iota(jnp.int32, (CHUNK, CHUNK), 0) >= lax.broadcasted_iota(jnp.int32, (CHUNK, CHUNK), 1)
            for gi in range(GM_GROUPS):
                gws_ref[gi] = jnp.where(tril, gws_ref[gi], 0.0)
            gbs_ref[...] = _dot3_lhs(dsum[...], seg_ref[...])

    row = lambda i: (i, 0)
    const2 = lambda i: (0, 0)
    const3 = lambda i: (0, 0, 0)
    return pl.pallas_call(
        body, name="gmlp_bwd", grid=(n_blk,),
        in_specs=[pl.BlockSpec((tm, d), row), pl.BlockSpec((tm, d), row), pl.BlockSpec((tm, 2 * e), row),
                  pl.BlockSpec((1, d), const2), ANY, pl.BlockSpec((1, e), const2), pl.BlockSpec((1, e), const2),
                  pl.BlockSpec(wc.shape, const3), pl.BlockSpec(wct.shape, const3), pl.BlockSpec((CHUNK, e), const2),
                  ANY, pl.BlockSpec((e, LANES), const2)],
        out_specs=[pl.BlockSpec((tm, d), row), pl.BlockSpec((tm, 2 * e), row), pl.BlockSpec(wc.shape, const3),
                   pl.BlockSpec((CHUNK, LANES), const2), pl.BlockSpec((1, e), const2), pl.BlockSpec((1, e), const2),
                   pl.BlockSpec((1, d), const2)],
        out_shape=[jax.ShapeDtypeStruct((s_len, d), F32), jax.ShapeDtypeStruct((s_len, 2 * e), BF16),
                   jax.ShapeDtypeStruct(wc.shape, F32), jax.ShapeDtypeStruct((CHUNK, LANES), F32),
                   jax.ShapeDtypeStruct((1, e), F32), jax.ShapeDtypeStruct((1, e), F32), jax.ShapeDtypeStruct((1, d), F32)],
        scratch_shapes=[pltpu.VMEM(w_in.shape, BF16), pltpu.VMEM(w_out.shape, BF16), pltpu.VMEM((CHUNK, e), F32),
                        pltpu.SemaphoreType.DMA((2,))],
        compiler_params=_params(1),
    )(h, dh, a, g_mix, w_in, lng, lnb, wc, wct, bias, w_out, seg)


def _shift_down(a, k, fill):
    tm = a.shape[0]
    out = pltpu.roll(a, k, 0)
    rid = lax.broadcasted_iota(jnp.int32, a.shape, 0)
    for j in range(k):
        out = jnp.where(rid == j, fill[8 - k + j:8 - k + j + 1, :], out)
    return out


def _shift_up(a, k, fill):
    tm = a.shape[0]
    out = pltpu.roll(a, tm - k, 0)
    rid = lax.broadcasted_iota(jnp.int32, a.shape, 0)
    for j in range(k):
        out = jnp.where(rid == tm - k + j, fill[j:j + 1, :], out)
    return out


def _ffn_fwd(h, g_norm, wg_all, wu_all, wd_all, layer, conv_w, conv_b, tm):
    s_len, d = h.shape
    n_p = wg_all.shape[0]
    fq = wg_all.shape[3]
    f = n_p * fq

    def body(h_ref, g_ref, wg_hbm, wu_hbm, wd_hbm, cw_ref, cb_ref,
             hout_ref, a_ref, up_ref, hn_ref, hid_ref, wg_v, wu_v, wd_v, carry, sem):
        i = pl.program_id(0)
        _load_once([(wg_hbm.at[:, layer], wg_v), (wu_hbm.at[:, layer], wu_v), (wd_hbm.at[:, layer], wd_v)], sem)

        @pl.when(i == 0)
        def _():
            carry[...] = jnp.zeros_like(carry)

        x = h_ref[...]
        _, _, y = _rms_fwd(x, g_ref[...])
        hn = y.astype(BF16)
        hn_ref[...] = hn
        for p in range(n_p):
            a_ref[:, p * fq:(p + 1) * fq] = _dot(hn, wg_v[p])
            up_ref[:, p * fq:(p + 1) * fq] = _dot(hn, wu_v[p])
        a = a_ref[...]
        prev = carry[...]
        am1 = _shift_down(a, 1, prev)
        am2 = _shift_down(a, 2, prev)
        carry[...] = a[tm - 8:tm, :]
        cw = cw_ref[...]
        ac = cb_ref[...] + am2 * cw[0:1, :]
        ac = ac + am1 * cw[1:2, :]
        ac = ac + a * cw[2:3, :]
        hid = (ac * _sigmoid(ac) * up_ref[...]).astype(BF16)
        hid_ref[...] = hid
        y2 = _dot(hid[:, :fq], wd_v[0])
        for p in range(1, n_p):
            y2 += _dot(hid[:, p * fq:(p + 1) * fq], wd_v[p])
        hout_ref[...] = x + y2

    row = lambda i: (i, 0)
    const2 = lambda i: (0, 0)
    return pl.pallas_call(
        body, name=f"ffn_fwd_{layer}", grid=(s_len // tm,),
        in_specs=[pl.BlockSpec((tm, d), row), pl.BlockSpec((1, d), const2), ANY, ANY, ANY,
                  pl.BlockSpec((8, f), const2), pl.BlockSpec((1, f), const2)],
        out_specs=[pl.BlockSpec((tm, d), row), pl.BlockSpec((tm, f), row), pl.BlockSpec((tm, f), row),
                   pl.BlockSpec((tm, d), row), pl.BlockSpec((tm, f), row)],
        out_shape=[jax.ShapeDtypeStruct((s_len, d), F32), jax.ShapeDtypeStruct((s_len, f), F32),
                   jax.ShapeDtypeStruct((s_len, f), F32), jax.ShapeDtypeStruct((s_len, d), BF16),
                   jax.ShapeDtypeStruct((s_len, f), BF16)],
        scratch_shapes=[pltpu.VMEM((n_p, d, fq), BF16), pltpu.VMEM((n_p, d, fq), BF16), pltpu.VMEM((n_p, fq, d), BF16),
                        pltpu.VMEM((8, f), F32), pltpu.SemaphoreType.DMA((3,))],
        compiler_params=_params(1),
    )(h, g_norm, wg_all, wu_all, wd_all, conv_w, conv_b)


def _ffn_bwd(h, dh, a, up, g_norm, wg_all, wu_all, wd_all, layer, conv_w, conv_b, tm):
    s_len, d = h.shape
    n_p = wg_all.shape[0]
    fq = wg_all.shape[3]
    f = n_p * fq
    n_blk = s_len // tm
    t8 = tm // 8

    def body(h_ref, dh_ref, a_ref, ahalo_ref, up_ref, g_ref, wg_hbm, wu_hbm, wd_hbm, cw_ref, cb_ref,
             dhin_ref, da_ref, dup_ref, gcw_ref, gcb_ref, gn_ref, wg_v, wu_v, wd_v, carry, sem):
        i = pl.program_id(0)
        _load_once([(wg_hbm.at[:, layer], wg_v), (wu_hbm.at[:, layer], wu_v), (wd_hbm.at[:, layer], wd_v)], sem)

        @pl.when(i == 0)
        def _():
            carry[...] = jnp.zeros_like(carry)
            gcw_ref[...] = jnp.zeros_like(gcw_ref)
            gcb_ref[...] = jnp.zeros_like(gcb_ref)
            gn_ref[...] = jnp.zeros_like(gn_ref)

        x = h_ref[...]
        dh_v = dh_ref[...]
        g = g_ref[...]
        xhat, r, _ = _rms_fwd(x, g)
        a = a_ref[...]
        up_v = up_ref[...]
        prev = jnp.where(i == n_blk - 1, 0.0, ahalo_ref[...])
        am1 = _shift_down(a, 1, prev)
        am2 = _shift_down(a, 2, prev)
        cw = cw_ref[...]
        ac = cb_ref[...] + am2 * cw[0:1, :]
        ac = ac + am1 * cw[1:2, :]
        ac = ac + a * cw[2:3, :]
        sg = _sigmoid(ac)
        sil = ac * sg
        dhb = dh_v.astype(BF16)
        dhid = jnp.concatenate([_dot_nt(dhb, wd_v[p]) for p in range(n_p)], axis=1)
        dup = dhid * sil
        dac = dhid * up_v * (sg * (1.0 + ac * (1.0 - sg)))
        gcb_ref[...] += jnp.sum(dac, axis=0, keepdims=True)
        gcw_ref[0:1, :] += jnp.sum(dac * am2, axis=0, keepdims=True)
        gcw_ref[1:2, :] += jnp.sum(dac * am1, axis=0, keepdims=True)
        gcw_ref[2:3, :] += jnp.sum(dac * a, axis=0, keepdims=True)
        nxt = carry[...]
        dp1 = _shift_up(dac, 1, nxt)
        dp2 = _shift_up(dac, 2, nxt)
        carry[...] = dac[0:8, :]
        da = dac * cw[2:3, :] + dp1 * cw[1:2, :] + dp2 * cw[0:1, :]
        dab = da.astype(BF16)
        dupb = dup.astype(BF16)
        da_ref[...] = dab
        dup_ref[...] = dupb
        dhn = _dot_nt(dab[:, :fq], wg_v[0]) + _dot_nt(dupb[:, :fq], wu_v[0])
        for p in range(1, n_p):
            dhn += _dot_nt(dab[:, p * fq:(p + 1) * fq], wg_v[p]) + _dot_nt(dupb[:, p * fq:(p + 1) * fq], wu_v[p])
        dx, gg = _rms_bwd(dhn, xhat, r, g)
        gn_ref[...] += jnp.sum(gg, axis=0, keepdims=True)
        dhin_ref[...] = dh_v + dx

    rev = lambda i: (n_blk - 1 - i, 0)
    halo = lambda i: (jnp.maximum((n_blk - 1 - i) * t8 - 1, 0), 0)
    const2 = lambda i: (0, 0)
    return pl.pallas_call(
        body, name=f"ffn_bwd_{layer}", grid=(n_blk,),
        in_specs=[pl.BlockSpec((tm, d), rev), pl.BlockSpec((tm, d), rev), pl.BlockSpec((tm, f), rev),
                  pl.BlockSpec((8, f), halo), pl.BlockSpec((tm, f), rev), pl.BlockSpec((1, d), const2), ANY, ANY, ANY,
                  pl.BlockSpec((8, f), const2), pl.BlockSpec((1, f), const2)],
        out_specs=[pl.BlockSpec((tm, d), rev), pl.BlockSpec((tm, f), rev), pl.BlockSpec((tm, f), rev),
                   pl.BlockSpec((8, f), const2), pl.BlockSpec((1, f), const2), pl.BlockSpec((1, d), const2)],
        out_shape=[jax.ShapeDtypeStruct((s_len, d), F32), jax.ShapeDtypeStruct((s_len, f), BF16),
                   jax.ShapeDtypeStruct((s_len, f), BF16), jax.ShapeDtypeStruct((8, f), F32),
                   jax.ShapeDtypeStruct((1, f), F32), jax.ShapeDtypeStruct((1, d), F32)],
        scratch_shapes=[pltpu.VMEM((n_p, d, fq), BF16), pltpu.VMEM((n_p, d, fq), BF16), pltpu.VMEM((n_p, fq, d), BF16),
                        pltpu.VMEM((8, f), F32), pltpu.SemaphoreType.DMA((3,))],
        compiler_params=_params(1),
    )(h, dh, a, a, up, g_norm, wg_all, wu_all, wd_all, conv_w, conv_b)


def _even_head_lanes(shape, axis):
    return (lax.broadcasted_iota(jnp.int32, shape, axis) & HEAD_DIM) == 0


def _fox_proj_fwd(h, g_norm, wq, wk, wv, wf, bf, expand, tm):
    s_len, d = h.shape

    def body(h_ref, g_ref, wq_hbm, wk_hbm, wv_hbm, wf_ref, bf_ref, ex_ref,
             hn_ref, q_ref, qt_ref, ka_ref, kb_ref, kat_ref, kbt_ref, va_ref, vb_ref, vat_ref, vbt_ref,
             z_ref, cumx_ref, cum_ref, wq_v, wk_v, wv_v, total, sem):
        i = pl.program_id(0)
        _load_once([(wq_hbm, wq_v), (wk_hbm, wk_v), (wv_hbm, wv_v)], sem)

        @pl.when(i == 0)
        def _():
            total[...] = jnp.zeros_like(total)

        x = h_ref[...]
        _, _, y = _rms_fwd(x, g_ref[...])
        hn = y.astype(BF16)
        hn_ref[...] = hn
        q = _dot(hn, wq_v[...]) * (HEAD_DIM ** -0.5)
        k = _dot(hn, wk_v[...])
        v = _dot(hn, wv_v[...])
        q_ref[...] = q.astype(BF16)
        qt_ref[...] = q.T.astype(BF16)
        even = _even_head_lanes((tm, d), 1)
        even_t = _even_head_lanes((d, tm), 0)
        kt = k.T
        vt = v.T
        ka_ref[...] = jnp.where(even, k, 0.0).astype(BF16)
        kb_ref[...] = jnp.where(even, 0.0, k).astype(BF16)
        kat_ref[...] = jnp.where(even_t, kt, 0.0).astype(BF16)
        kbt_ref[...] = jnp.where(even_t, 0.0, kt).astype(BF16)
        va_ref[...] = jnp.where(even, v, 0.0).astype(BF16)
        vb_ref[...] = jnp.where(even, 0.0, v).astype(BF16)
        vat_ref[...] = jnp.where(even_t, vt, 0.0).astype(BF16)
        vbt_ref[...] = jnp.where(even_t, 0.0, vt).astype(BF16)
        z = _dot(hn, wf_ref[...]) + bf_ref[...]
        z_ref[...] = z
        logf = jnp.minimum(z, 0.0) - jnp.log(1.0 + jnp.exp(-jnp.abs(z)))
        tri = (lax.broadcasted_iota(jnp.int32, (tm, tm), 0) >= lax.broadcasted_iota(jnp.int32, (tm, tm), 1))
        cum = _dot3_rhs(jnp.where(tri, 1.0, 0.0).astype(BF16), logf) + total[0:1, :]
        total[...] = jnp.broadcast_to(cum[tm - 1:tm, :], total.shape)
        cum_ref[...] = cum
        cumx_ref[...] = _dot3_lhs(cum, ex_ref[...])

    row = lambda i: (i, 0)
    col = lambda i: (0, i)
    const2 = lambda i: (0, 0)
    sd = jax.ShapeDtypeStruct((s_len, d), BF16)
    ds_ = jax.ShapeDtypeStruct((d, s_len), BF16)
    rs, cs = pl.BlockSpec((tm, d), row), pl.BlockSpec((d, tm), col)
    return pl.pallas_call(
        body, name="fox_proj_fwd", grid=(s_len // tm,),
        in_specs=[rs, pl.BlockSpec((1, d), const2), ANY, ANY, ANY, pl.BlockSpec((d, LANES), const2),
                  pl.BlockSpec((1, LANES), const2), pl.BlockSpec((LANES, d), const2)],
        out_specs=[rs, rs, cs, rs, rs, cs, cs, rs, rs, cs, cs,
                   pl.BlockSpec((tm, LANES), row), rs, pl.BlockSpec((tm, LANES), row)],
        out_shape=[sd, sd, ds_, sd, sd, ds_, ds_, sd, sd, ds_, ds_,
                   jax.ShapeDtypeStruct((s_len, LANES), F32), jax.ShapeDtypeStruct((s_len, d), F32),
                   jax.ShapeDtypeStruct((s_len, LANES), F32)],
        scratch_shapes=[pltpu.VMEM((d, d), BF16), pltpu.VMEM((d, d), BF16), pltpu.VMEM((d, d), BF16),
                        pltpu.VMEM((8, LANES), F32), pltpu.SemaphoreType.DMA((3,))],
        compiler_params=_params(1),
    )(h, g_norm, wq, wk, wv, wf, bf, expand)


def _pair_select(lo, hi, shape):
    return jnp.where(lax.broadcasted_iota(jnp.int32, shape, 1) < HEAD_DIM, lo, hi)


def _causal(row0, col0, shape):
    return row0 + lax.broadcasted_iota(jnp.int32, shape, 0) >= col0 + lax.broadcasted_iota(jnp.int32, shape, 1)


def _flash_fwd(q, kat, kbt, va, vb, cumx, cum_rows):
    s_len, d = q.shape
    t = ATT_BLOCK
    w = min(ATT_CHUNK, s_len)
    n_pair = d // LANES
    n_q = s_len // t

    def body(q_ref, kat_ref, kbt_ref, va_ref, vb_ref, cqx_ref, ck_ref, o_ref, rbx_ref):
        i = pl.program_id(1)
        qv = q_ref[...]
        cq = (cqx_ref[:, 0:1], cqx_ref[:, HEAD_DIM:HEAD_DIM + 1])
        kts = (kat_ref, kbt_ref)
        vs = (va_ref, vb_ref)

        def step(kb, carry, masked):
            m, l, acc = carry
            off = pl.multiple_of(kb * w, w)
            new_m, new_l, alphas, pv = [], [], [], None
            for hh in range(2):
                s = _dot(qv, kts[hh][:, pl.ds(off, w)]) + cq[hh] - ck_ref[0, hh:hh + 1, pl.ds(off, w)]
                if masked:
                    s = jnp.where(_causal(i * t, off, (t, w)), s, NEG_BIG)
                m_new = jnp.maximum(m[hh], jnp.max(s, axis=1, keepdims=True))
                alpha = jnp.exp(m[hh] - m_new)
                p = jnp.exp(s - m_new)
                new_l.append(alpha * l[hh] + jnp.sum(p, axis=1, keepdims=True))
                new_m.append(m_new)
                alphas.append(alpha)
                contrib = _dot(p.astype(BF16), vs[hh][pl.ds(off, w), :])
                pv = contrib if pv is None else pv + contrib
            acc = acc * _pair_select(alphas[0], alphas[1], (t, LANES)) + pv
            return tuple(new_m), tuple(new_l), acc

        init = ((jnp.full((t, 1), NEG_BIG, F32),) * 2, (jnp.zeros((t, 1), F32),) * 2, jnp.zeros((t, LANES), F32))
        diag = (i * t) // w
        carry = lax.fori_loop(0, diag, lambda kb, c: step(kb, c, False), init)
        m, l, acc = step(diag, carry, True)
        o_ref[...] = acc / _pair_select(l[0], l[1], (t, LANES))
        rb = [cq[hh] - (m[hh] + jnp.log(l[hh])) for hh in range(2)]
        rbx_ref[...] = _pair_select(rb[0], rb[1], (t, LANES))

    qblk = pl.BlockSpec((t, LANES), lambda j, i: (i, j))
    whole_t = pl.BlockSpec((LANES, s_len), lambda j, i: (j, 0))
    whole = pl.BlockSpec((s_len, LANES), lambda j, i: (0, j))
    return pl.pallas_call(
        body, name="flash_fwd", grid=(n_pair, n_q),
        in_specs=[qblk, whole_t, whole_t, whole, whole, qblk, pl.BlockSpec((1, 8, s_len), lambda j, i: (j, 0, 0))],
        out_specs=[qblk, qblk],
        out_shape=[jax.ShapeDtypeStruct((s_len, d), F32), jax.ShapeDtypeStruct((s_len, d), F32)],
        compiler_params=_params(2),
    )(q, kat, kbt, va, vb, cumx, cum_rows)


def _flash_bwd_dq(q, kat, kbt, ka, kb_, vat, vbt, do, rbx, deltax, cum_rows):
    s_len, d = q.shape
    t = ATT_BLOCK
    w = min(ATT_CHUNK, s_len)
    n_pair = d // LANES
    n_q = s_len // t

    def body(q_ref, kat_ref, kbt_ref, ka_ref, kb_ref, vat_ref, vbt_ref, do_ref, rbx_ref, dlx_ref, ck_ref,
             dq_ref, dck_ref, dcqx_ref):
        i = pl.program_id(1)

        @pl.when(i == 0)
        def _():
            dck_ref[...] = jnp.zeros_like(dck_ref)

        qv = q_ref[...]
        dov = do_ref[...]
        rb = (rbx_ref[:, 0:1], rbx_ref[:, HEAD_DIM:HEAD_DIM + 1])
        dl = (dlx_ref[:, 0:1], dlx_ref[:, HEAD_DIM:HEAD_DIM + 1])
        kts, ks, vts = (kat_ref, kbt_ref), (ka_ref, kb_ref), (vat_ref, vbt_ref)

        def step(kb, carry, masked):
            acc, rsum = carry
            off = pl.multiple_of(kb * w, w)
            new_rsum = []
            for hh in range(2):
                s = _dot(qv, kts[hh][:, pl.ds(off, w)]) + rb[hh] - ck_ref[0, hh:hh + 1, pl.ds(off, w)]
                if masked:
                    s = jnp.where(_causal(i * t, off, (t, w)), s, NEG_BIG)
                p = jnp.exp(s)
                dp = _dot(dov, vts[hh][:, pl.ds(off, w)])
                ds = p * (dp - dl[hh])
                acc = acc + _dot(ds.astype(BF16), ks[hh][pl.ds(off, w), :])
                dck_ref[0, hh:hh + 1, pl.ds(off, w)] -= jnp.sum(ds, axis=0, keepdims=True)
                new_rsum.append(rsum[hh] + jnp.sum(ds, axis=1, keepdims=True))
            return acc, tuple(new_rsum)

        init = (jnp.zeros((t, LANES), F32), (jnp.zeros((t, 1), F32),) * 2)
        diag = (i * t) // w
        carry = lax.fori_loop(0, diag, lambda kb, c: step(kb, c, False), init)
        acc, rsum = step(diag, carry, True)
        dq_ref[...] = (acc * (HEAD_DIM ** -0.5)).astype(BF16)
        dcqx_ref[...] = _pair_select(rsum[0], rsum[1], (t, LANES))

    qblk = pl.BlockSpec((t, LANES), lambda j, i: (i, j))
    whole_t = pl.BlockSpec((LANES, s_len), lambda j, i: (j, 0))
    whole = pl.BlockSpec((s_len, LANES), lambda j, i: (0, j))
    rows = pl.BlockSpec((1, 8, s_len), lambda j, i: (j, 0, 0))
    return pl.pallas_call(
        body, name="flash_bwd_dq", grid=(n_pair, n_q),
        in_specs=[qblk, whole_t, whole_t, whole, whole, whole_t, whole_t, qblk, qblk, qblk, rows],
        out_specs=[qblk, rows, qblk],
        out_shape=[jax.ShapeDtypeStruct((s_len, d), BF16), jax.ShapeDtypeStruct((n_pair, 8, s_len), F32),
                   jax.ShapeDtypeStruct((s_len, d), F32)],
        compiler_params=_params(2),
    )(q, kat, kbt, ka, kb_, vat, vbt, do, rbx, deltax, cum_rows)


def _flash_bwd_dkv(ka, kb_, va, vb, q, qt, do, dot_, cumx, rb_rows, delta_rows):
    s_len, d = q.shape
    t = ATT_BLOCK
    w = min(ATT_CHUNK, s_len)
    n_pair = d // LANES
    n_k = s_len // t

    def body(ka_ref, kb_ref, va_ref, vb_ref, q_ref, qt_ref, do_ref, dot_ref, ckx_ref, rb_ref, dl_ref, dk_ref, dv_ref):
        kblk = pl.program_id(1)
        ck = (ckx_ref[:, 0:1], ckx_ref[:, HEAD_DIM:HEAD_DIM + 1])
        ks = (ka_ref[...], kb_ref[...])
        vs = (va_ref[...], vb_ref[...])

        def step(qb, carry, masked):
            off = pl.multiple_of(qb * w, w)
            qtv = qt_ref[:, pl.ds(off, w)]
            dotv = dot_ref[:, pl.ds(off, w)]
            qv = q_ref[pl.ds(off, w), :]
            dov = do_ref[pl.ds(off, w), :]
            out = []
            for hh in range(2):
                dk_acc, dv_acc = carry[hh]
                st = _dot(ks[hh], qtv) + rb_ref[0, hh:hh + 1, pl.ds(off, w)] - ck[hh]
                if masked:
                    keys = kblk * t + lax.broadcasted_iota(jnp.int32, (t, w), 0)
                    st = jnp.where(off + lax.broadcasted_iota(jnp.int32, (t, w), 1) >= keys, st, NEG_BIG)
                pt = jnp.exp(st)
                dpt = _dot(vs[hh], dotv)
                dst = pt * (dpt - dl_ref[0, hh:hh + 1, pl.ds(off, w)])
                out.append((dk_acc + _dot(dst.astype(BF16), qv), dv_acc + _dot(pt.astype(BF16), dov)))
            return tuple(out)

        zero = jnp.zeros((t, LANES), F32)
        diag = (kblk * t) // w
        carry = step(diag, ((zero, zero), (zero, zero)), True)
        carry = lax.fori_loop(diag + 1, s_len // w, lambda qb, c: step(qb, c, False), carry)
        dk_ref[...] = _pair_select(carry[0][0], carry[1][0], (t, LANES)).astype(BF16)
        dv_ref[...] = _pair_select(carry[0][1], carry[1][1], (t, LANES)).astype(BF16)

    kblk_spec = pl.BlockSpec((t, LANES), lambda j, i: (i, j))
    whole_t = pl.BlockSpec((LANES, s_len), lambda j, i: (j, 0))
    whole = pl.BlockSpec((s_len, LANES), lambda j, i: (0, j))
    rows = pl.BlockSpec((1, 8, s_len), lambda j, i: (j, 0, 0))
    return pl.pallas_call(
        body, name="flash_bwd_dkv", grid=(n_pair, n_k),
        in_specs=[kblk_spec, kblk_spec, kblk_spec, kblk_spec, whole, whole_t, whole, whole_t, kblk_spec, rows, rows],
        out_specs=[kblk_spec, kblk_spec],
        out_shape=[jax.ShapeDtypeStruct((s_len, d), BF16), jax.ShapeDtypeStruct((s_len, d), BF16)],
        compiler_params=_params(2),
    )(ka, kb_, va, vb, q, qt, do, dot_, cumx, rb_rows, delta_rows)


def _oproj_fwd(h, o, wo, tm):
    s_len, d = h.shape

    def body(h_ref, o_ref, wo_hbm, hout_ref, wo_v, sem):
        _load_once([(wo_hbm, wo_v)], sem)
        hout_ref[...] = h_ref[...] + _dot(o_ref[...].astype(BF16), wo_v[...])

    row = lambda i: (i, 0)
    return pl.pallas_call(
        body, name="oproj_fwd", grid=(s_len // tm,),
        in_specs=[pl.BlockSpec((tm, d), row), pl.BlockSpec((tm, d), row), ANY],
        out_specs=pl.BlockSpec((tm, d), row),
        out_shape=jax.ShapeDtypeStruct((s_len, d), F32),
        scratch_shapes=[pltpu.VMEM((d, d), BF16), pltpu.SemaphoreType.DMA((1,))],
        compiler_params=_params(1),
    )(h, o, wo)


def _oproj_bwd(dh, o, wo, seg, expand, tm):
    s_len, d = dh.shape

    def body(dh_ref, o_ref, wo_hbm, seg_ref, ex_ref, do_ref, dot_ref, dlx_ref, wo_v, sem):
        _load_once([(wo_hbm, wo_v)], sem)
        do = _dot_nt(dh_ref[...].astype(BF16), wo_v[...])
        do_ref[...] = do.astype(BF16)
        dot_ref[...] = do.T.astype(BF16)
        dlx_ref[...] = _dot3_lhs(_dot3_lhs(do * o_ref[...], seg_ref[...]), ex_ref[...])

    row = lambda i: (i, 0)
    const2 = lambda i: (0, 0)
    return pl.pallas_call(
        body, name="oproj_bwd", grid=(s_len // tm,),
        in_specs=[pl.BlockSpec((tm, d), row), pl.BlockSpec((tm, d), row), ANY,
                  pl.BlockSpec((d, LANES), const2), pl.BlockSpec((LANES, d), const2)],
        out_specs=[pl.BlockSpec((tm, d), row), pl.BlockSpec((d, tm), lambda i: (0, i)), pl.BlockSpec((tm, d), row)],
        out_shape=[jax.ShapeDtypeStruct((s_len, d), BF16), jax.ShapeDtypeStruct((d, s_len), BF16),
                   jax.ShapeDtypeStruct((s_len, d), F32)],
        scratch_shapes=[pltpu.VMEM((d, d), BF16), pltpu.SemaphoreType.DMA((1,))],
        compiler_params=_params(1),
    )(dh, o, wo, seg, expand)


def _forget_bwd(dcum, z, tm):
    s_len = dcum.shape[0]
    n_blk = s_len // tm

    def body(dc_ref, z_ref, dfl_ref, gb_ref, total):
        i = pl.program_id(0)

        @pl.when(i == 0)
        def _():
            total[...] = jnp.zeros_like(total)
            gb_ref[...] = jnp.zeros_like(gb_ref)

        upper = (lax.broadcasted_iota(jnp.int32, (tm, tm), 0) <= lax.broadcasted_iota(jnp.int32, (tm, tm), 1))
        suffix = _dot3_rhs(jnp.where(upper, 1.0, 0.0).astype(BF16), dc_ref[...]) + total[0:1, :]
        total[...] = jnp.broadcast_to(suffix[0:1, :], total.shape)
        dfl = suffix * _sigmoid(-z_ref[...])
        dfl_ref[...] = dfl
        gb_ref[...] += jnp.sum(dfl, axis=0, keepdims=True)

    rev = lambda i: (n_blk - 1 - i, 0)
    return pl.pallas_call(
        body, name="forget_bwd", grid=(n_blk,),
        in_specs=[pl.BlockSpec((tm, LANES), rev), pl.BlockSpec((tm, LANES), rev)],
        out_specs=[pl.BlockSpec((tm, LANES), rev), pl.BlockSpec((1, LANES), lambda i: (0, 0))],
        out_shape=[jax.ShapeDtypeStruct((s_len, LANES), F32), jax.ShapeDtypeStruct((1, LANES), F32)],
        scratch_shapes=[pltpu.VMEM((8, LANES), F32)],
        compiler_params=_params(1),
    )(dcum, z)


def _fox_proj_bwd(h, dh, dq, dk, dv, dfl, g_norm, wq, wk, wv, wf, tm):
    s_len, d = h.shape

    def body(h_ref, dh_ref, dq_ref, dk_ref, dv_ref, dfl_ref, g_ref, wq_hbm, wk_hbm, wv_hbm, wf_ref,
             dhin_ref, dflb_ref, gn_ref, wq_v, wk_v, wv_v, sem):
        _load_once([(wq_hbm, wq_v), (wk_hbm, wk_v), (wv_hbm, wv_v)], sem)

        @pl.when(pl.program_id(0) == 0)
        def _():
            gn_ref[...] = jnp.zeros_like(gn_ref)

        g = g_ref[...]
        xhat, r, _ = _rms_fwd(h_ref[...], g)
        dflb = dfl_ref[...].astype(BF16)
        dflb_ref[...] = dflb
        dhn = (_dot_nt(dq_ref[...], wq_v[...]) + _dot_nt(dk_ref[...], wk_v[...]) + _dot_nt(dv_ref[...], wv_v[...])
               + _dot_nt(dflb, wf_ref[...]))
        dx, gg = _rms_bwd(dhn, xhat, r, g)
        gn_ref[...] += jnp.sum(gg, axis=0, keepdims=True)
        dhin_ref[...] = dh_ref[...] + dx

    row = lambda i: (i, 0)
    const2 = lambda i: (0, 0)
    rs = pl.BlockSpec((tm, d), row)
    return pl.pallas_call(
        body, name="fox_proj_bwd", grid=(s_len // tm,),
        in_specs=[rs, rs, rs, rs, rs, pl.BlockSpec((tm, LANES), row), pl.BlockSpec((1, d), const2), ANY, ANY, ANY,
                  pl.BlockSpec((d, LANES), const2)],
        out_specs=[rs, pl.BlockSpec((tm, LANES), row), pl.BlockSpec((1, d), const2)],
        out_shape=[jax.ShapeDtypeStruct((s_len, d), F32), jax.ShapeDtypeStruct((s_len, LANES), BF16),
                   jax.ShapeDtypeStruct((1, d), F32)],
        scratch_shapes=[pltpu.VMEM((d, d), BF16), pltpu.VMEM((d, d), BF16), pltpu.VMEM((d, d), BF16),
                        pltpu.SemaphoreType.DMA((3,))],
        compiler_params=_params(1),
    )(h, dh, dq, dk, dv, dfl, g_norm, wq, wk, wv, wf)


def _loss_head(h, target, g_final, tm):
    s_len, d = h.shape
    n_blk = s_len // tm

    def body(h_ref, t_ref, g_ref, dh_ref, loss_ref, gg_ref, sq):
        i = pl.program_id(0)

        @pl.when(i == 0)
        def _():
            sq[...] = jnp.zeros_like(sq)
            gg_ref[...] = jnp.zeros_like(gg_ref)

        g = g_ref[...]
        xhat, r, y = _rms_fwd(h_ref[...], g)
        err = y - t_ref[...]
        sq[...] += jnp.sum(err * err, axis=0, keepdims=True)
        dx, gg = _rms_bwd(err * (1.0 / d), xhat, r, g)
        gg_ref[...] += jnp.sum(gg, axis=0, keepdims=True)
        dh_ref[...] = dx

        @pl.when(i == n_blk - 1)
        def _():
            loss_ref[...] = jnp.broadcast_to(jnp.sum(sq[...], axis=1, keepdims=True) * (0.5 / d), loss_ref.shape)

    row = lambda i: (i, 0)
    const2 = lambda i: (0, 0)
    return pl.pallas_call(
        body, name="loss_head", grid=(n_blk,),
        in_specs=[pl.BlockSpec((tm, d), row), pl.BlockSpec((tm, d), row), pl.BlockSpec((1, d), const2)],
        out_specs=[pl.BlockSpec((tm, d), row), pl.BlockSpec((1, LANES), const2), pl.BlockSpec((1, d), const2)],
        out_shape=[jax.ShapeDtypeStruct((s_len, d), F32), jax.ShapeDtypeStruct((1, LANES), F32),
                   jax.ShapeDtypeStruct((1, d), F32)],
        scratch_shapes=[pltpu.VMEM((1, d), F32)],
        compiler_params=_params(1),
    )(h, target, g_final)


def _wgrad(x, dy, n_piece, name):
    s_len, k = x.shape
    n = dy.shape[1]
    tn = min(n // n_piece, 1024)
    tk = min(k, 1024)
    ts = 512
    per_piece = (n // n_piece) // tn

    def body(x_ref, dy_ref, o_ref):
        @pl.when(pl.program_id(2) == 0)
        def _():
            o_ref[...] = jnp.zeros_like(o_ref)
        o_ref[0] += _dot_tn(x_ref[...].astype(BF16), dy_ref[...].astype(BF16))

    return pl.pallas_call(
        body, name=name, grid=(k // tk, n // tn, s_len // ts),
        in_specs=[pl.BlockSpec((ts, tk), lambda a, b, c: (c, a)), pl.BlockSpec((ts, tn), lambda a, b, c: (c, b))],
        out_specs=pl.BlockSpec((1, tk, tn), lambda a, b, c: (b // per_piece, a, b % per_piece)),
        out_shape=jax.ShapeDtypeStruct((n_piece, k, n // n_piece), F32),
        compiler_params=_params(3),
    )(x, dy)


def _pair_sum(g, recv, core, name):
    n_piece, rows, c = g.shape
    half = rows // 2
    tr = min(half, 512)
    nb = half // tr

    def body(core_ref, g_ref, r_ref, o_ref):
        o_ref[...] = g_ref[...] + r_ref[...]

    blk = pl.BlockSpec((1, tr, c), lambda p, i, core_ref: (p, i, 0))
    return pl.pallas_call(
        body, name=name, out_shape=jax.ShapeDtypeStruct((n_piece, half, c), F32),
        grid_spec=pltpu.PrefetchScalarGridSpec(
            num_scalar_prefetch=1, grid=(n_piece, nb),
            in_specs=[pl.BlockSpec((1, tr, c), lambda p, i, core_ref: (p, core_ref[0] * nb + i, 0)), blk],
            out_specs=blk),
        compiler_params=_params(2),
    )(core, g, recv)


def _chip_sum(halves, recv, chip, name):
    _, h, c = halves.shape
    tr = min(h, 512)

    def body(chip_ref, own_ref, r_ref, o_ref):
        o_ref[...] = ((own_ref[0] + r_ref[0]) + r_ref[1]) + r_ref[2]

    return pl.pallas_call(
        body, name=name, out_shape=jax.ShapeDtypeStruct((h, c), F32),
        grid_spec=pltpu.PrefetchScalarGridSpec(
            num_scalar_prefetch=1, grid=(h // tr,),
            in_specs=[pl.BlockSpec((1, tr, c), lambda i, chip_ref: (chip_ref[0], i, 0)),
                      pl.BlockSpec((3, tr, c), lambda i, chip_ref: (0, i, 0))],
            out_specs=pl.BlockSpec((tr, c), lambda i, chip_ref: (i, 0))),
        compiler_params=_params(1),
    )(chip, halves, recv)


def _adamw_math(w, m, v, g):
    m_new = ADAM_B1 * m + (1.0 - ADAM_B1) * g
    v_new = ADAM_B2 * v + (1.0 - ADAM_B2) * (g * g)
    m_hat = m_new / (1.0 - ADAM_B1 ** ADAM_STEP)
    v_hat = v_new / (1.0 - ADAM_B2 ** ADAM_STEP)
    return -ADAM_LR * (m_hat / (jnp.sqrt(v_hat) + ADAM_EPS) + ADAM_WD * w), m_new, v_new


def _adamw(w, m, v, g, name):
    rows, c = w.shape
    tr = min(rows, 256)

    def body(w_ref, m_ref, v_ref, g_ref, d_ref, mo_ref, vo_ref):
        d_ref[...], mo_ref[...], vo_ref[...] = _adamw_math(w_ref[...], m_ref[...], v_ref[...], g_ref[...])

    spec = pl.BlockSpec((tr, c), lambda i: (i, 0))
    shape = jax.ShapeDtypeStruct((rows, c), F32)
    return pl.pallas_call(
        body, name=name, grid=(rows // tr,),
        in_specs=[spec] * 4, out_specs=[spec] * 3, out_shape=[shape] * 3,
        compiler_params=_params(1),
    )(w, m, v, g)


def _adamw_halves(w, m, v, g_own, g_other, core, name):
    rows, c = w.shape
    half = rows // 2
    tr = min(half, 256)
    nb = half // tr

    def body(core_ref, w_ref, m_ref, v_ref, own_ref, other_ref, g_ref, d_ref, mo_ref, vo_ref):
        mine = (pl.program_id(0) // nb) == core_ref[0]
        g = jnp.where(mine, own_ref[...], other_ref[...])
        g_ref[...] = g
        d_ref[...], mo_ref[...], vo_ref[...] = _adamw_math(w_ref[...], m_ref[...], v_ref[...], g)

    spec = pl.BlockSpec((tr, c), lambda i, core_ref: (i, 0))
    own = pl.BlockSpec((tr, c), lambda i, core_ref: (jnp.clip(i - core_ref[0] * nb, 0, nb - 1), 0))
    other = pl.BlockSpec((tr, c), lambda i, core_ref: (jnp.clip(i - (1 - core_ref[0]) * nb, 0, nb - 1), 0))
    shape = jax.ShapeDtypeStruct((rows, c), F32)
    return pl.pallas_call(
        body, name=name, out_shape=[shape] * 4,
        grid_spec=pltpu.PrefetchScalarGridSpec(
            num_scalar_prefetch=1, grid=(rows // tr,),
            in_specs=[spec, spec, spec, own, other], out_specs=[spec] * 4),
        compiler_params=_params(1),
    )(core, w, m, v, g_own, g_other)


def _place():
    x, y, c = lax.axis_index("x"), lax.axis_index("y"), lax.axis_index("c")
    chips = [(1 - x, y), (x, 1 - y), (1 - x, 1 - y)]
    return x, y, c, chips


def _all_gather_chips(shards):
    n = len(shards)

    def body(*refs):
        ins, outs = refs[:n], refs[n:2 * n]
        send_sems, recv_sems, local_sems = refs[2 * n:]
        x, y, c, chips = _place()
        mine = 2 * x + y
        local = [pltpu.make_async_copy(ins[k], outs[k].at[mine], local_sems.at[k]) for k in range(n)]
        for cp in local:
            cp.start()
        sends = []
        for k in range(n):
            for j, (tx, ty) in enumerate(chips):
                sends.append(pltpu.make_async_remote_copy(
                    src_ref=ins[k], dst_ref=outs[k].at[mine], send_sem=send_sems.at[k, j], recv_sem=recv_sems.at[k, j],
                    device_id=(tx, ty, c), device_id_type=MESH))
        for cp in sends:
            cp.start()
        for k in range(n):
            for j, (tx, ty) in enumerate(chips):
                pltpu.make_async_remote_copy(
                    src_ref=ins[k], dst_ref=outs[k].at[2 * tx + ty], send_sem=send_sems.at[k, j],
                    recv_sem=recv_sems.at[k, j], device_id=(tx, ty, c), device_id_type=MESH).wait()
        for cp in local:
            cp.wait()

    return pl.pallas_call(
        body, name="weights_all_gather",
        in_specs=[ANY] * n, out_specs=[ANY] * n,
        out_shape=[jax.ShapeDtypeStruct((4,) + s.shape, s.dtype) for s in shards],
        scratch_shapes=[pltpu.SemaphoreType.DMA((n, 3)), pltpu.SemaphoreType.DMA((n, 3)), pltpu.SemaphoreType.DMA((n,))],
    )(*shards)


def _pair_exchange(grads):
    n = len(grads)

    def body(*refs):
        ins, outs = refs[:n], refs[n:2 * n]
        send_sems, recv_sems = refs[2 * n:]
        x, y, c, _ = _place()
        copies = []
        for k in range(n):
            half = grads[k].shape[1] // 2
            other = ins[k].at[:, pl.ds(pl.multiple_of((1 - c) * half, 8), half), :]
            copies.append(pltpu.make_async_remote_copy(
                src_ref=other, dst_ref=outs[k], send_sem=send_sems.at[k], recv_sem=recv_sems.at[k],
                device_id=(x, y, 1 - c), device_id_type=MESH))
        for cp in copies:
            cp.start()
        for cp in copies:
            cp.wait()

    return pl.pallas_call(
        body, name="grads_pair_exchange",
        in_specs=[ANY] * n, out_specs=[ANY] * n,
        out_shape=[jax.ShapeDtypeStruct((4, g.shape[1] // 2, g.shape[2]), F32) for g in grads],
        scratch_shapes=[pltpu.SemaphoreType.DMA((n,)), pltpu.SemaphoreType.DMA((n,))],
    )(*grads)


def _chip_scatter(halves):
    n = len(halves)

    def body(*refs):
        ins, outs = refs[:n], refs[n:2 * n]
        send_sems, recv_sems = refs[2 * n:]
        x, y, c, chips = _place()
        sends = []
        for k in range(n):
            for j, (tx, ty) in enumerate(chips):
                sends.append(pltpu.make_async_remote_copy(
                    src_ref=ins[k].at[2 * tx + ty], dst_ref=outs[k].at[j], send_sem=send_sems.at[k, j],
                    recv_sem=recv_sems.at[k, j], device_id=(tx, ty, c), device_id_type=MESH))
        for cp in sends:
            cp.start()
        for cp in sends:
            cp.wait()

    return pl.pallas_call(
        body, name="grads_chip_scatter",
        in_specs=[ANY] * n, out_specs=[ANY] * n,
        out_shape=[jax.ShapeDtypeStruct((3,) + hv.shape[1:], F32) for hv in halves],
        scratch_shapes=[pltpu.SemaphoreType.DMA((n, 3)), pltpu.SemaphoreType.DMA((n, 3))],
    )(*halves)


def _pair_share(finals):
    n = len(finals)

    def body(*refs):
        ins, outs = refs[:n], refs[n:2 * n]
        send_sems, recv_sems = refs[2 * n:]
        x, y, c, _ = _place()
        copies = [pltpu.make_async_remote_copy(
            src_ref=ins[k], dst_ref=outs[k], send_sem=send_sems.at[k], recv_sem=recv_sems.at[k],
            device_id=(x, y, 1 - c), device_id_type=MESH) for k in range(n)]
        for cp in copies:
            cp.start()
        for cp in copies:
            cp.wait()

    return pl.pallas_call(
        body, name="grads_pair_share",
        in_specs=[ANY] * n, out_specs=[ANY] * n,
        out_shape=[jax.ShapeDtypeStruct(fv.shape, F32) for fv in finals],
        scratch_shapes=[pltpu.SemaphoreType.DMA((n,)), pltpu.SemaphoreType.DMA((n,))],
    )(*finals)


def _small_all_reduce(buf):
    rows, c_ = buf.shape

    def body(in_ref, out_ref, pair_buf, slots, send_sems, recv_sems):
        x, y, c, chips = _place()
        mine = 2 * x + y
        pair = pltpu.make_async_remote_copy(
            src_ref=in_ref, dst_ref=pair_buf, send_sem=send_sems.at[0], recv_sem=recv_sems.at[0],
            device_id=(x, y, 1 - c), device_id_type=MESH)
        pair.start()
        pair.wait()
        slots[mine] = in_ref[...] + pair_buf[...]
        sends = [pltpu.make_async_remote_copy(
            src_ref=slots.at[mine], dst_ref=slots.at[mine], send_sem=send_sems.at[1 + j], recv_sem=recv_sems.at[1 + j],
            device_id=(tx, ty, c), device_id_type=MESH) for j, (tx, ty) in enumerate(chips)]
        for cp in sends:
            cp.start()
        for j, (tx, ty) in enumerate(chips):
            pltpu.make_async_remote_copy(
                src_ref=slots.at[mine], dst_ref=slots.at[2 * tx + ty], send_sem=send_sems.at[1 + j],
                recv_sem=recv_sems.at[1 + j], device_id=(tx, ty, c), device_id_type=MESH).wait()
        out_ref[...] = ((slots[0] + slots[1]) + slots[2]) + slots[3]

    vm = pl.BlockSpec(memory_space=pltpu.VMEM)
    return pl.pallas_call(
        body, name="small_all_reduce", in_specs=[vm], out_specs=vm,
        out_shape=jax.ShapeDtypeStruct((rows, c_), F32),
        scratch_shapes=[pltpu.VMEM((rows, c_), F32), pltpu.VMEM((4, rows, c_), F32),
                        pltpu.SemaphoreType.DMA((4,)), pltpu.SemaphoreType.DMA((4,))],
        compiler_params=pltpu.CompilerParams(vmem_limit_bytes=VMEM_LIMIT_V7X),
    )(buf)


def _reduce_scatter(grads):
    core = lax.axis_index("c").astype(jnp.int32).reshape(1)
    chip = (2 * lax.axis_index("x") + lax.axis_index("y")).astype(jnp.int32).reshape(1)
    recv = _pair_exchange(grads)
    halves = [_pair_sum(g, r, core, f"pair_sum_{k}") for k, (g, r) in enumerate(zip(grads, recv))]
    recv = _chip_scatter(halves)
    finals = [_chip_sum(hv, r, chip, f"chip_sum_{k}") for k, (hv, r) in enumerate(zip(halves, recv))]
    return list(zip(finals, _pair_share(finals))), core


PACK_COLS = 1024


def _pack(arrays):
    flat = jnp.concatenate([a.reshape(-1).astype(F32) for a in arrays])
    rows = -(-flat.shape[0] // PACK_COLS)
    rows = -(-rows // 8) * 8
    return jnp.pad(flat, (0, rows * PACK_COLS - flat.shape[0])).reshape(rows, PACK_COLS)


def _unpack(buf, shapes):
    flat = buf.reshape(-1)
    out, at = [], 0
    for shp in shapes:
        size = math.prod(shp)
        out.append(flat[at:at + size].reshape(shp))
        at += size
    return out


def kernel(x, mix_norm_g, ffn_norm_g, gm_w_in, gm_ln_g, gm_ln_b, gm_w_s, gm_b_s, gm_w_out, fox_w_qkvf, fox_b_f, fox_w_o, ffn_w_gate, ffn_w_up, ffn_conv_w, ffn_conv_b, ffn_w_down, final_norm_g, loss_target, m_mix_norm_g, m_ffn_norm_g, m_gm_w_in, m_gm_ln_g, m_gm_ln_b, m_gm_w_s, m_gm_b_s, m_gm_w_out, m_fox_w_qkvf, m_fox_b_f, m_fox_w_o, m_ffn_w_gate, m_ffn_w_up, m_ffn_conv_w, m_ffn_conv_b, m_ffn_w_down, m_final_norm_g, v_mix_norm_g, v_ffn_norm_g, v_gm_w_in, v_gm_ln_g, v_gm_ln_b, v_gm_w_s, v_gm_b_s, v_gm_w_out, v_fox_w_qkvf, v_fox_b_f, v_fox_w_o, v_ffn_w_gate, v_ffn_w_up, v_ffn_conv_w, v_ffn_conv_b, v_ffn_w_down, v_final_norm_g):
    _, s_len, d = x.shape
    e = gm_ln_g.shape[1]
    f = ffn_conv_b.shape[1]
    n_head = fox_b_f.shape[1]
    n_pair = n_head // 2
    gd = e // GM_GROUPS
    qkvf_cols = fox_w_qkvf.shape[2]
    assert d == n_head * HEAD_DIM and d % (2 * LANES) == 0 and s_len % 512 == 0 and gd % LANES == 0
    assert gm_w_s.shape[2] == CHUNK and 4 * qkvf_cols == 3 * d + n_head
    tm = 256
    h0 = x[0]
    target = loss_target[0]

    gathered = _all_gather_chips([
        gm_w_in[0].astype(BF16), gm_w_out[0].astype(BF16), fox_w_qkvf[0].astype(BF16), fox_w_o[0].astype(BF16),
        ffn_w_gate.astype(BF16), ffn_w_up.astype(BF16), ffn_w_down.astype(BF16), ffn_conv_w])
    w_in, w_out4, qkvf4, wo4, wg_all, wu_all, wd_all, cw4 = gathered
    w_out = w_out4.reshape(e, d)
    qkvf = jnp.transpose(qkvf4, (1, 0, 2)).reshape(d, 4 * qkvf_cols)
    wq, wk, wv = qkvf[:, :d], qkvf[:, d:2 * d], qkvf[:, 2 * d:3 * d]
    wf = jnp.pad(qkvf[:, 3 * d:], ((0, 0), (0, LANES - n_head)))
    wo = wo4.reshape(d, d)
    conv_w_full = jnp.transpose(cw4, (1, 2, 0, 3)).reshape(2, 3, f)
    conv_w8 = jnp.pad(conv_w_full, ((0, 0), (0, 5), (0, 0)))
    bf_pad = jnp.pad(fox_b_f, ((0, 0), (0, LANES - n_head)))

    tril = jnp.tril(jnp.ones((CHUNK, CHUNK), bool))
    wc = jnp.where(tril[None], gm_w_s[0], 0.0).astype(BF16)
    wct = jnp.transpose(wc, (0, 2, 1))
    bias = jnp.repeat(gm_b_s[0].T, gd, axis=1)
    seg_groups = (jnp.arange(e)[:, None] // gd == jnp.arange(LANES)[None, :]).astype(BF16)
    seg_heads = (jnp.arange(d)[:, None] // HEAD_DIM == jnp.arange(LANES)[None, :]).astype(BF16)
    expand_heads = seg_heads.T

    h1, a0, hn0, gated0 = _gmlp_fwd(h0, mix_norm_g[0:1], w_in, gm_ln_g, gm_ln_b, wc, bias, w_out, tm)
    h2, fa0, fup0, fhn0, fhid0 = _ffn_fwd(h1, ffn_norm_g[0:1], wg_all, wu_all, wd_all, 0, conv_w8[0], ffn_conv_b[0:1], tm)
    (hn1, q, qt, ka, kb_, kat, kbt, va, vb, vat, vbt, z_f, cumx, cum) = _fox_proj_fwd(
        h2, mix_norm_g[1:2], wq, wk, wv, wf, bf_pad, expand_heads, tm)

    def pair_rows(cols):
        r = cols[:, :n_head].T.reshape(n_pair, 2, s_len)
        return jnp.pad(r, ((0, 0), (0, 6), (0, 0)))

    cum_rows = pair_rows(cum)
    o, rbx = _flash_fwd(q, kat, kbt, va, vb, cumx, cum_rows)
    h3 = _oproj_fwd(h2, o, wo, tm)
    h4, fa1, fup1, fhn1, fhid1 = _ffn_fwd(h3, ffn_norm_g[1:2], wg_all, wu_all, wd_all, 1, conv_w8[1], ffn_conv_b[1:2], tm)

    dh4, loss_part, g_final = _loss_head(h4, target, final_norm_g.reshape(1, d), tm)
    dh3, da1, dup1, gcw1, gcb1, gfn1 = _ffn_bwd(h3, dh4, fa1, fup1, ffn_norm_g[1:2], wg_all, wu_all, wd_all, 1,
                                                conv_w8[1], ffn_conv_b[1:2], tm)
    g_gate1 = _wgrad(fhn1, da1, 4, "wgrad_gate_1")
    g_up1 = _wgrad(fhn1, dup1, 4, "wgrad_up_1")
    g_down1 = _wgrad(fhid1, dh4, 1, "wgrad_down_1").reshape(4, f // 4, d)

    do, dot_, deltax = _oproj_bwd(dh3, o, wo, seg_heads, expand_heads, tm)
    g_wo = _wgrad(o, dh3, 1, "wgrad_wo").reshape(4, d // 4, d)
    dq, dck_rows, dcqx = _flash_bwd_dq(q, kat, kbt, ka, kb_, vat, vbt, do, rbx, deltax, cum_rows)
    rb_rows = pair_rows(rbx[:, ::HEAD_DIM])
    delta_rows = pair_rows(deltax[:, ::HEAD_DIM])
    dk, dv = _flash_bwd_dkv(ka, kb_, va, vb, q, qt, do, dot_, cumx, rb_rows, delta_rows)
    dcum = jnp.pad(dck_rows[:, :2, :].reshape(n_head, s_len).T + dcqx[:, ::HEAD_DIM], ((0, 0), (0, LANES - n_head)))
    dfl, g_bf = _forget_bwd(dcum, z_f, tm)
    dh2, dflb, gmn1 = _fox_proj_bwd(h2, dh3, dq, dk, dv, dfl, mix_norm_g[1:2], wq, wk, wv, wf, tm)
    g_q = _wgrad(hn1, dq, 1, "wgrad_q")[0]
    g_k = _wgrad(hn1, dk, 1, "wgrad_k")[0]
    g_v = _wgrad(hn1, dv, 1, "wgrad_v")[0]
    g_f = _wgrad(hn1, dflb, 1, "wgrad_f")[0][:, :n_head]
    g_qkvf = jnp.concatenate([g_q, g_k, g_v, g_f], axis=1).reshape(d, 4, qkvf_cols).transpose(1, 0, 2)

    dh1, da0f, dup0, gcw0, gcb0, gfn0 = _ffn_bwd(h1, dh2, fa0, fup0, ffn_norm_g[0:1], wg_all, wu_all, wd_all, 0,
                                                 conv_w8[0], ffn_conv_b[0:1], tm)
    g_gate0 = _wgrad(fhn0, da0f, 4, "wgrad_gate_0")
    g_up0 = _wgrad(fhn0, dup0, 4, "wgrad_up_0")
    g_down0 = _wgrad(fhid0, dh2, 1, "wgrad_down_0").reshape(4, f // 4, d)

    dh0, da0, g_ws, g_bs_t, g_lng, g_lnb, gmn0 = _gmlp_bwd(
        h0, dh1, a0, mix_norm_g[0:1], w_in, gm_ln_g, gm_ln_b, wc, wct, bias, w_out, seg_groups, CHUNK)
    g_win = _wgrad(hn0, da0, 4, "wgrad_gm_in")
    g_wout = _wgrad(gated0, dh1, 1, "wgrad_gm_out").reshape(4, e // 4, d)

    big, core = _reduce_scatter([g_win, g_wout, g_qkvf, g_wo, g_gate0, g_gate1, g_up0, g_up1, g_down0, g_down1])
    r_win, r_wout, r_qkvf, r_wo, r_gate0, r_gate1, r_up0, r_up1, r_down0, r_down1 = big

    small = [jnp.concatenate([gmn0, gmn1]), jnp.concatenate([gfn0, gfn1]), g_lng, g_lnb, g_ws[None],
             g_bs_t[:, :GM_GROUPS].T[None], g_bf[:, :n_head], jnp.stack([gcw0[:3], gcw1[:3]]),
             jnp.concatenate([gcb0, gcb1]), g_final.reshape(d), loss_part[0, :1]]
    small_shapes = [a.shape for a in small]
    reduced = _unpack(_small_all_reduce(_pack(small)), small_shapes)
    (r_mix, r_ffn, r_lng, r_lnb, r_ws, r_bs, r_bf, r_cw_full, r_cb, r_final, r_loss) = reduced
    chip = 2 * lax.axis_index("x") + lax.axis_index("y")
    r_cw = lax.dynamic_slice_in_dim(r_cw_full, chip * (f // 4), f // 4, axis=2)

    def update_big(name, w, m, v, per_layer):
        parts = [_adamw_halves(w[l], m[l], v[l], own, other, core, f"adamw_{name}_{l}")
                 for l, (own, other) in enumerate(per_layer)]
        return tuple(jnp.stack([p[i] for p in parts]) for i in range(4))

    res = {}
    res["gm_w_in"] = update_big("gm_w_in", gm_w_in, m_gm_w_in, v_gm_w_in, [r_win])
    res["gm_w_out"] = update_big("gm_w_out", gm_w_out, m_gm_w_out, v_gm_w_out, [r_wout])
    res["fox_w_qkvf"] = update_big("fox_w_qkvf", fox_w_qkvf, m_fox_w_qkvf, v_fox_w_qkvf, [r_qkvf])
    res["fox_w_o"] = update_big("fox_w_o", fox_w_o, m_fox_w_o, v_fox_w_o, [r_wo])
    res["ffn_w_gate"] = update_big("ffn_w_gate", ffn_w_gate, m_ffn_w_gate, v_ffn_w_gate, [r_gate0, r_gate1])
    res["ffn_w_up"] = update_big("ffn_w_up", ffn_w_up, m_ffn_w_up, v_ffn_w_up, [r_up0, r_up1])
    res["ffn_w_down"] = update_big("ffn_w_down", ffn_w_down, m_ffn_w_down, v_ffn_w_down, [r_down0, r_down1])

    small_names = ["mix_norm_g", "ffn_norm_g", "gm_ln_g", "gm_ln_b", "gm_w_s", "gm_b_s", "fox_b_f", "ffn_conv_w",
                   "ffn_conv_b", "final_norm_g"]
    small_w = [mix_norm_g, ffn_norm_g, gm_ln_g, gm_ln_b, gm_w_s, gm_b_s, fox_b_f, ffn_conv_w, ffn_conv_b, final_norm_g]
    small_m = [m_mix_norm_g, m_ffn_norm_g, m_gm_ln_g, m_gm_ln_b, m_gm_w_s, m_gm_b_s, m_fox_b_f, m_ffn_conv_w,
               m_ffn_conv_b, m_final_norm_g]
    small_v = [v_mix_norm_g, v_ffn_norm_g, v_gm_ln_g, v_gm_ln_b, v_gm_w_s, v_gm_b_s, v_fox_b_f, v_ffn_conv_w,
               v_ffn_conv_b, v_final_norm_g]
    small_g = [r_mix, r_ffn, r_lng, r_lnb, r_ws, r_bs, r_bf, r_cw, r_cb, r_final]
    shapes = [w.shape for w in small_w]
    small_g = [g.reshape(s) for g, s in zip(small_g, shapes)]
    dlt, mn, vn = _adamw(_pack(small_w), _pack(small_m), _pack(small_v), _pack(small_g), "adamw_small")
    for name, g, dl_, m_, v_ in zip(small_names, small_g, _unpack(dlt, shapes), _unpack(mn, shapes), _unpack(vn, shapes)):
        res[name] = (g, dl_, m_, v_)

    order = ["mix_norm_g", "ffn_norm_g", "gm_w_in", "gm_ln_g", "gm_ln_b", "gm_w_s", "gm_b_s", "gm_w_out", "fox_w_qkvf",
             "fox_b_f", "fox_w_o", "ffn_w_gate", "ffn_w_up", "ffn_conv_w", "ffn_conv_b", "ffn_w_down", "final_norm_g"]
    outs = [r_loss.reshape(()), dh0[None]]
    for part in range(4):
        outs += [res[name][part] for name in order]
    return tuple(outs)
```

```python
import functools
import math

import jax
import jax.numpy as jnp
from jax import lax
from jax.experimental import pallas as pl
from jax.experimental.pallas import tpu as pltpu

F32 = jnp.float32
BF16 = jnp.bfloat16

RMS_EPS = 1e-6
LN_EPS = 1e-5
CHUNK = 128
GM_GROUPS = 8
HEAD_DIM = 64
LANES = 128
ATT_BLOCK = 256
ATT_CHUNK = 1024
VMEM_LIMIT_V7X = 56 * 1024 * 1024

ADAM_LR = 0.001
ADAM_B1 = 0.9
ADAM_B2 = 0.999
ADAM_EPS = 1e-08
ADAM_WD = 0.01
ADAM_STEP = 10

MESH = pl.DeviceIdType.MESH
ANY = pl.BlockSpec(memory_space=pl.ANY)
NEG_BIG = -1e30


def _params(n_grid):
    return pltpu.CompilerParams(dimension_semantics=("arbitrary",) * n_grid, vmem_limit_bytes=VMEM_LIMIT_V7X)


def _dot(a, b):
    return jnp.dot(a, b, preferred_element_type=F32)


def _dot_nt(a, b):
    return lax.dot_general(a, b, (((1,), (1,)), ((), ())), preferred_element_type=F32)


def _dot_tn(a, b):
    return lax.dot_general(a, b, (((0,), (0,)), ((), ())), preferred_element_type=F32)


def _split3(x):
    hi = x.astype(BF16)
    r = x - hi.astype(F32)
    mid = r.astype(BF16)
    lo = (r - mid.astype(F32)).astype(BF16)
    return hi, mid, lo


def _dot3_lhs(x, m):
    hi, mid, lo = _split3(x)
    return _dot(hi, m) + _dot(mid, m) + _dot(lo, m)


def _dot3_rhs(m, x):
    hi, mid, lo = _split3(x)
    return _dot(m, hi) + _dot(m, mid) + _dot(m, lo)


def _load_once(pairs, sem):
    @pl.when(pl.program_id(0) == 0)
    def _():
        copies = [pltpu.make_async_copy(src, dst, sem.at[k]) for k, (src, dst) in enumerate(pairs)]
        for cp in copies:
            cp.start()
        for cp in copies:
            cp.wait()


def _rms_fwd(x, g):
    r = lax.rsqrt(jnp.mean(x * x, axis=-1, keepdims=True) + RMS_EPS)
    xhat = x * r
    return xhat, r, xhat * g


def _rms_bwd(dy, xhat, r, g):
    w = dy * g
    dx = r * (w - xhat * jnp.mean(w * xhat, axis=-1, keepdims=True))
    return dx, dy * xhat


def _gelu_parts(a):
    c = math.sqrt(2.0 / math.pi)
    a2 = a * a
    t = jnp.tanh(c * (a + 0.044715 * a * a2))
    z = 0.5 * a * (1.0 + t)
    dz = 0.5 * (1.0 + t) + 0.5 * a * (1.0 - t * t) * (c * (1.0 + 3.0 * 0.044715 * a2))
    return z, dz


def _sigmoid(x):
    return 1.0 / (1.0 + jnp.exp(-x))


def _gmlp_core(a, lng, lnb, wc_ref, bias, n_chunk, gd):
    e = a.shape[1] // 2
    z, dz = _gelu_parts(a)
    u = z[:, :e]
    v = z[:, e:]
    mu = jnp.mean(v, axis=-1, keepdims=True)
    vc = v - mu
    rstd = lax.rsqrt(jnp.mean(vc * vc, axis=-1, keepdims=True) + LN_EPS)
    vhat = vc * rstd
    vln = vhat * lng + lnb
    vlb = vln.astype(BF16)
    rows = []
    for ci in range(n_chunk):
        cols = []
        for g in range(GM_GROUPS):
            blk = vlb[ci * CHUNK:(ci + 1) * CHUNK, g * gd:(g + 1) * gd]
            cols.append(_dot(wc_ref[g], blk))
        rows.append(jnp.concatenate(cols, axis=1) + bias)
    s = rows[0] if n_chunk == 1 else jnp.concatenate(rows, axis=0)
    return dz, u, vhat, rstd, vlb, s


def _gmlp_fwd(h, g_mix, w_in, lng, lnb, wc, bias, w_out, tm):
    s_len, d = h.shape
    n_p, _, w = w_in.shape
    e = w_out.shape[0]
    gd = e // GM_GROUPS
    n_chunk = tm // CHUNK

    def body(h_ref, g_ref, win_hbm, lng_ref, lnb_ref, wc_ref, bias_ref, wout_hbm,
             hout_ref, a_ref, hn_ref, gated_ref, win_v, wout_v, sem):
        _load_once([(win_hbm, win_v), (wout_hbm, wout_v)], sem)
        x = h_ref[...]
        _, _, y = _rms_fwd(x, g_ref[...])
        hn = y.astype(BF16)
        hn_ref[...] = hn
        for p in range(n_p):
            a_ref[:, p * w:(p + 1) * w] = _dot(hn, win_v[p])
        _, u, _, _, _, s = _gmlp_core(a_ref[...], lng_ref[...], lnb_ref[...], wc_ref, bias_ref[...], n_chunk, gd)
        gated = (u * s).astype(BF16)
        gated_ref[...] = gated
        hout_ref[...] = x + _dot(gated, wout_v[...])

    row = lambda i: (i, 0)
    const2 = lambda i: (0, 0)
    return pl.pallas_call(
        body, name="gmlp_fwd", grid=(s_len // tm,),
        in_specs=[pl.BlockSpec((tm, d), row), pl.BlockSpec((1, d), const2), ANY,
                  pl.BlockSpec((1, e), const2), pl.BlockSpec((1, e), const2),
                  pl.BlockSpec(wc.shape, lambda i: (0, 0, 0)), pl.BlockSpec((CHUNK, e), const2), ANY],
        out_specs=[pl.BlockSpec((tm, d), row), pl.BlockSpec((tm, 2 * e), row),
                   pl.BlockSpec((tm, d), row), pl.BlockSpec((tm, e), row)],
        out_shape=[jax.ShapeDtypeStruct((s_len, d), F32), jax.ShapeDtypeStruct((s_len, 2 * e), F32),
                   jax.ShapeDtypeStruct((s_len, d), BF16), jax.ShapeDtypeStruct((s_len, e), BF16)],
        scratch_shapes=[pltpu.VMEM(w_in.shape, BF16), pltpu.VMEM(w_out.shape, BF16), pltpu.SemaphoreType.DMA((2,))],
        compiler_params=_params(1),
    )(h, g_mix, w_in, lng, lnb, wc, bias, w_out)


def _gmlp_bwd(h, dh, a, g_mix, w_in, lng, lnb, wc, wct, bias, w_out, seg, tm):
    s_len, d = h.shape
    n_p, _, w = w_in.shape
    e = w_out.shape[0]
    gd = e // GM_GROUPS
    n_chunk = tm // CHUNK
    n_blk = s_len // tm

    def body(h_ref, dh_ref, a_ref, g_ref, win_hbm, lng_ref, lnb_ref, wc_ref, wct_ref, bias_ref, wout_hbm, seg_ref,
             dhin_ref, da_ref, gws_ref, gbs_ref, glng_ref, glnb_ref, gmix_ref, win_v, wout_v, dsum, sem):
        i = pl.program_id(0)
        _load_once([(win_hbm, win_v), (wout_hbm, wout_v)], sem)

        @pl.when(i == 0)
        def _():
            gws_ref[...] = jnp.zeros_like(gws_ref)
            glng_ref[...] = jnp.zeros_like(glng_ref)
            glnb_ref[...] = jnp.zeros_like(glnb_ref)
            gmix_ref[...] = jnp.zeros_like(gmix_ref)
            dsum[...] = jnp.zeros_like(dsum)

        x = h_ref[...]
        dh_v = dh_ref[...]
        g = g_ref[...]
        lng_v = lng_ref[...]
        xhat, r, _ = _rms_fwd(x, g)
        dz_da, u, vhat, rstd, vlb, s = _gmlp_core(a_ref[...], lng_v, lnb_ref[...], wc_ref, bias_ref[...], n_chunk, gd)
        dg = _dot_nt(dh_v.astype(BF16), wout_v[...])
        du = dg * s
        ds = dg * u
        dsb = ds.astype(BF16)
        rows = []
        ds_acc = None
        for ci in range(n_chunk):
            lo, hi = ci * CHUNK, (ci + 1) * CHUNK
            cols = []
            for gi in range(GM_GROUPS):
                d_blk = dsb[lo:hi, gi * gd:(gi + 1) * gd]
                gws_ref[gi] += _dot_nt(d_blk, vlb[lo:hi, gi * gd:(gi + 1) * gd])
                cols.append(_dot(wct_ref[gi], d_blk))
            rows.append(jnp.concatenate(cols, axis=1))
            ds_acc = ds[lo:hi] if ds_acc is None else ds_acc + ds[lo:hi]
        dsum[...] += ds_acc
        dvln = rows[0] if n_chunk == 1 else jnp.concatenate(rows, axis=0)
        glng_ref[...] += jnp.sum(dvln * vhat, axis=0, keepdims=True)
        glnb_ref[...] += jnp.sum(dvln, axis=0, keepdims=True)
        dvhat = dvln * lng_v
        dv = rstd * (dvhat - jnp.mean(dvhat, axis=-1, keepdims=True)
                     - vhat * jnp.mean(dvhat * vhat, axis=-1, keepdims=True))
        da = jnp.concatenate([du, dv], axis=1) * dz_da
        dab = da.astype(BF16)
        da_ref[...] = dab
        dhn = _dot_nt(dab[:, :w], win_v[0])
        for p in range(1, n_p):
            dhn += _dot_nt(dab[:, p * w:(p + 1) * w], win_v[p])
        dx, gg = _rms_bwd(dhn, xhat, r, g)
        gmix_ref[...] += jnp.sum(gg, axis=0, keepdims=True)
        dhin_ref[...] = dh_v + dx

        @pl.when(i == n_blk - 1)
        def _():
            tril = lax.broadcasted_iota(jnp.int32, (CHUNK, CHUNK), 0) >= lax.broadcasted_iota(jnp.int32, (CHUNK, CHUNK), 1)
            for gi in range(GM_GROUPS):
                gws_ref[gi] = jnp.where(tril, gws_ref[gi], 0.0)
            gbs_ref[...] = _dot3_lhs(dsum[...], seg_ref[...])

    row = lambda i: (i, 0)
    const2 = lambda i: (0, 0)
    const3 = lambda i: (0, 0, 0)
    return pl.pallas_call(
        body, name="gmlp_bwd", grid=(n_blk,),
        in_specs=[pl.BlockSpec((tm, d), row), pl.BlockSpec((tm, d), row), pl.BlockSpec((tm, 2 * e), row),
                  pl.BlockSpec((1, d), const2), ANY, pl.BlockSpec((1, e), const2), pl.BlockSpec((1, e), const2),
                  pl.BlockSpec(wc.shape, const3), pl.BlockSpec(wct.shape, const3), pl.BlockSpec((CHUNK, e), const2),
                  ANY, pl.BlockSpec((e, LANES), const2)],
        out_specs=[pl.BlockSpec((tm, d), row), pl.BlockSpec((tm, 2 * e), row), pl.BlockSpec(wc.shape, const3),
                   pl.BlockSpec((CHUNK, LANES), const2), pl.BlockSpec((1, e), const2), pl.BlockSpec((1, e), const2),
                   pl.BlockSpec((1, d), const2)],
        out_shape=[jax.ShapeDtypeStruct((s_len, d), F32), jax.ShapeDtypeStruct((s_len, 2 * e), BF16),
                   jax.ShapeDtypeStruct(wc.shape, F32), jax.ShapeDtypeStruct((CHUNK, LANES), F32),
                   jax.ShapeDtypeStruct((1, e), F32), jax.ShapeDtypeStruct((1, e), F32), jax.ShapeDtypeStruct((1, d), F32)],
        scratch_shapes=[pltpu.VMEM(w_in.shape, BF16), pltpu.VMEM(w_out.shape, BF16), pltpu.VMEM((CHUNK, e), F32),
                        pltpu.SemaphoreType.DMA((2,))],
        compiler_params=_params(1),
    )(h, dh, a, g_mix, w_in, lng, lnb, wc, wct, bias, w_out, seg)


def _shift_down(a, k, fill):
    tm = a.shape[0]
    out = pltpu.roll(a, k, 0)
    rid = lax.broadcasted_iota(jnp.int32, a.shape, 0)
    for j in range(k):
        out = jnp.where(rid == j, fill[8 - k + j:8 - k + j + 1, :], out)
    return out


def _shift_up(a, k, fill):
    tm = a.shape[0]
    out = pltpu.roll(a, tm - k, 0)
    rid = lax.broadcasted_iota(jnp.int32, a.shape, 0)
    for j in range(k):
        out = jnp.where(rid == tm - k + j, fill[j:j + 1, :], out)
    return out


def _ffn_fwd(h, g_norm, wg_all, wu_all, wd_all, layer, conv_w, conv_b, tm):
    s_len, d = h.shape
    n_p = wg_all.shape[0]
    fq = wg_all.shape[3]
    f = n_p * fq

    def body(h_ref, g_ref, wg_hbm, wu_hbm, wd_hbm, cw_ref, cb_ref,
             hout_ref, a_ref, up_ref, hn_ref, hid_ref, wg_v, wu_v, wd_v, carry, sem):
        i = pl.program_id(0)
        _load_once([(wg_hbm.at[:, layer], wg_v), (wu_hbm.at[:, layer], wu_v), (wd_hbm.at[:, layer], wd_v)], sem)

        @pl.when(i == 0)
        def _():
            carry[...] = jnp.zeros_like(carry)

        x = h_ref[...]
        _, _, y = _rms_fwd(x, g_ref[...])
        hn = y.astype(BF16)
        hn_ref[...] = hn
        for p in range(n_p):
            a_ref[:, p * fq:(p + 1) * fq] = _dot(hn, wg_v[p])
            up_ref[:, p * fq:(p + 1) * fq] = _dot(hn, wu_v[p])
        a = a_ref[...]
        prev = carry[...]
        am1 = _shift_down(a, 1, prev)
        am2 = _shift_down(a, 2, prev)
        carry[...] = a[tm - 8:tm, :]
        cw = cw_ref[...]
        ac = cb_ref[...] + am2 * cw[0:1, :]
        ac = ac + am1 * cw[1:2, :]
        ac = ac + a * cw[2:3, :]
        hid = (ac * _sigmoid(ac) * up_ref[...]).astype(BF16)
        hid_ref[...] = hid
        y2 = _dot(hid[:, :fq], wd_v[0])
        for p in range(1, n_p):
            y2 += _dot(hid[:, p * fq:(p + 1) * fq], wd_v[p])
        hout_ref[...] = x + y2

    row = lambda i: (i, 0)
    const2 = lambda i: (0, 0)
    return pl.pallas_call(
        body, name=f"ffn_fwd_{layer}", grid=(s_len // tm,),
        in_specs=[pl.BlockSpec((tm, d), row), pl.BlockSpec((1, d), const2), ANY, ANY, ANY,
                  pl.BlockSpec((8, f), const2), pl.BlockSpec((1, f), const2)],
        out_specs=[pl.BlockSpec((tm, d), row), pl.BlockSpec((tm, f), row), pl.BlockSpec((tm, f), row),
                   pl.BlockSpec((tm, d), row), pl.BlockSpec((tm, f), row)],
        out_shape=[jax.ShapeDtypeStruct((s_len, d), F32), jax.ShapeDtypeStruct((s_len, f), F32),
                   jax.ShapeDtypeStruct((s_len, f), F32), jax.ShapeDtypeStruct((s_len, d), BF16),
                   jax.ShapeDtypeStruct((s_len, f), BF16)],
        scratch_shapes=[pltpu.VMEM((n_p, d, fq), BF16), pltpu.VMEM((n_p, d, fq), BF16), pltpu.VMEM((n_p, fq, d), BF16),
                        pltpu.VMEM((8, f), F32), pltpu.SemaphoreType.DMA((3,))],
        compiler_params=_params(1),
    )(h, g_norm, wg_all, wu_all, wd_all, conv_w, conv_b)


def _ffn_bwd(h, dh, a, up, g_norm, wg_all, wu_all, wd_all, layer, conv_w, conv_b, tm):
    s_len, d = h.shape
    n_p = wg_all.shape[0]
    fq = wg_all.shape[3]
    f = n_p * fq
    n_blk = s_len // tm
    t8 = tm // 8

    def body(h_ref, dh_ref, a_ref, ahalo_ref, up_ref, g_ref, wg_hbm, wu_hbm, wd_hbm, cw_ref, cb_ref,
             dhin_ref, da_ref, dup_ref, gcw_ref, gcb_ref, gn_ref, wg_v, wu_v, wd_v, carry, sem):
        i = pl.program_id(0)
        _load_once([(wg_hbm.at[:, layer], wg_v), (wu_hbm.at[:, layer], wu_v), (wd_hbm.at[:, layer], wd_v)], sem)

        @pl.when(i == 0)
        def _():
            carry[...] = jnp.zeros_like(carry)
            gcw_ref[...] = jnp.zeros_like(gcw_ref)
            gcb_ref[...] = jnp.zeros_like(gcb_ref)
            gn_ref[...] = jnp.zeros_like(gn_ref)

        x = h_ref[...]
        dh_v = dh_ref[...]
        g = g_ref[...]
        xhat, r, _ = _rms_fwd(x, g)
        a = a_ref[...]
        up_v = up_ref[...]
        prev = jnp.where(i == n_blk - 1, 0.0, ahalo_ref[...])
        am1 = _shift_down(a, 1, prev)
        am2 = _shift_down(a, 2, prev)
        cw = cw_ref[...]
        ac = cb_ref[...] + am2 * cw[0:1, :]
        ac = ac + am1 * cw[1:2, :]
        ac = ac + a * cw[2:3, :]
        sg = _sigmoid(ac)
        sil = ac * sg
        dhb = dh_v.astype(BF16)
        dhid = jnp.concatenate([_dot_nt(dhb, wd_v[p]) for p in range(n_p)], axis=1)
        dup = dhid * sil
        dac = dhid * up_v * (sg * (1.0 + ac * (1.0 - sg)))
        gcb_ref[...] += jnp.sum(dac, axis=0, keepdims=True)
        gcw_ref[0:1, :] += jnp.sum(dac * am2, axis=0, keepdims=True)
        gcw_ref[1:2, :] += jnp.sum(dac * am1, axis=0, keepdims=True)
        gcw_ref[2:3, :] += jnp.sum(dac * a, axis=0, keepdims=True)
        nxt = carry[...]
        dp1 = _shift_up(dac, 1, nxt)
        dp2 = _shift_up(dac, 2, nxt)
        carry[...] = dac[0:8, :]
        da = dac * cw[2:3, :] + dp1 * cw[1:2, :] + dp2 * cw[0:1, :]
        dab = da.astype(BF16)
        dupb = dup.astype(BF16)
        da_ref[...] = dab
        dup_ref[...] = dupb
        dhn = _dot_nt(dab[:, :fq], wg_v[0]) + _dot_nt(dupb[:, :fq], wu_v[0])
        for p in range(1, n_p):
            dhn += _dot_nt(dab[:, p * fq:(p + 1) * fq], wg_v[p]) + _dot_nt(dupb[:, p * fq:(p + 1) * fq], wu_v[p])
        dx, gg = _rms_bwd(dhn, xhat, r, g)
        gn_ref[...] += jnp.sum(gg, axis=0, keepdims=True)
        dhin_ref[...] = dh_v + dx

    rev = lambda i: (n_blk - 1 - i, 0)
    halo = lambda i: (jnp.maximum((n_blk - 1 - i) * t8 - 1, 0), 0)
    const2 = lambda i: (0, 0)
    return pl.pallas_call(
        body, name=f"ffn_bwd_{layer}", grid=(n_blk,),
        in_specs=[pl.BlockSpec((tm, d), rev), pl.BlockSpec((tm, d), rev), pl.BlockSpec((tm, f), rev),
                  pl.BlockSpec((8, f), halo), pl.BlockSpec((tm, f), rev), pl.BlockSpec((1, d), const2), ANY, ANY, ANY,
                  pl.BlockSpec((8, f), const2), pl.BlockSpec((1, f), const2)],
        out_specs=[pl.BlockSpec((tm, d), rev), pl.BlockSpec((tm, f), rev), pl.BlockSpec((tm, f), rev),
                   pl.BlockSpec((8, f), const2), pl.BlockSpec((1, f), const2), pl.BlockSpec((1, d), const2)],
        out_shape=[jax.ShapeDtypeStruct((s_len, d), F32), jax.ShapeDtypeStruct((s_len, f), BF16),
                   jax.ShapeDtypeStruct((s_len, f), BF16), jax.ShapeDtypeStruct((8, f), F32),
                   jax.ShapeDtypeStruct((1, f), F32), jax.ShapeDtypeStruct((1, d), F32)],
        scratch_shapes=[pltpu.VMEM((n_p, d, fq), BF16), pltpu.VMEM((n_p, d, fq), BF16), pltpu.VMEM((n_p, fq, d), BF16),
                        pltpu.VMEM((8, f), F32), pltpu.SemaphoreType.DMA((3,))],
        compiler_params=_params(1),
    )(h, dh, a, a, up, g_norm, wg_all, wu_all, wd_all, conv_w, conv_b)


def _even_head_lanes(shape, axis):
    return (lax.broadcasted_iota(jnp.int32, shape, axis) & HEAD_DIM) == 0


def _pair_select(lo, hi, shape):
    return jnp.where(lax.broadcasted_iota(jnp.int32, shape, 1) < HEAD_DIM, lo, hi)


def _causal(row0, col0, shape):
    return row0 + lax.broadcasted_iota(jnp.int32, shape, 0) >= col0 + lax.broadcasted_iota(jnp.int32, shape, 1)


N_SPARE = 3


def _spare_selectors(d, key_side):
    lane = jnp.arange(d)[None, :]
    row = jnp.arange(N_SPARE * LANES)[:, None]
    head, part = row % LANES, row // LANES
    off = N_SPARE if key_side else 0
    sel_a = ((head % 2 == 0) & (lane == LANES * (head // 2) + HEAD_DIM + off + part)).astype(F32)
    sel_b = ((head % 2 == 1) & (lane == LANES * (head // 2) + off + part)).astype(F32)
    sign = -1.0 if key_side else 1.0
    ones_off = 0 if key_side else N_SPARE
    in_pair = jnp.arange(d)[None, :] % LANES
    ones_a = ((in_pair >= HEAD_DIM + ones_off) & (in_pair < HEAD_DIM + ones_off + N_SPARE)).astype(F32)
    ones_b = ((in_pair >= ones_off) & (in_pair < ones_off + N_SPARE)).astype(F32)
    return (sign * sel_a).astype(BF16), (sign * sel_b).astype(BF16), ones_a, ones_b


def _parts(x):
    return jnp.concatenate(_split3(x), axis=1)


def _fox_proj_fwd(h, g_norm, wq, wk, wv, wf, bf, sel_q, sel_k, tm):
    s_len, d = h.shape
    sq_a, sq_b, oq_a, oq_b = sel_q
    sk_a, sk_b, ok_a, ok_b = sel_k

    def body(h_ref, g_ref, wq_hbm, wk_hbm, wv_hbm, wf_ref, bf_ref, sqa_ref, sqb_ref, oqa_ref, oqb_ref,
             ska_ref, skb_ref, oka_ref, okb_ref,
             hn_ref, qa_ref, qb_ref, ka_ref, kb_ref, kat_ref, kbt_ref, va_ref, vb_ref, vat_ref, vbt_ref, z_ref,
             wq_v, wk_v, wv_v, total, sem):
        i = pl.program_id(0)
        _load_once([(wq_hbm, wq_v), (wk_hbm, wk_v), (wv_hbm, wv_v)], sem)

        @pl.when(i == 0)
        def _():
            total[...] = jnp.zeros_like(total)

        x = h_ref[...]
        _, _, y = _rms_fwd(x, g_ref[...])
        hn = y.astype(BF16)
        hn_ref[...] = hn
        z = _dot(hn, wf_ref[...]) + bf_ref[...]
        z_ref[...] = z
        logf = jnp.minimum(z, 0.0) - jnp.log(1.0 + jnp.exp(-jnp.abs(z)))
        tri = (lax.broadcasted_iota(jnp.int32, (tm, tm), 0) >= lax.broadcasted_iota(jnp.int32, (tm, tm), 1))
        cum = _dot3_rhs(jnp.where(tri, 1.0, 0.0).astype(BF16), logf) + total[0:1, :]
        total[...] = jnp.broadcast_to(cum[tm - 1:tm, :], total.shape)
        parts = _parts(cum)

        even = _even_head_lanes((tm, d), 1)
        q = _dot(hn, wq_v[...]) * (HEAD_DIM ** -0.5)
        qa_ref[...] = jnp.where(even, q, _dot(parts, sqa_ref[...]) + oqa_ref[...]).astype(BF16)
        qb_ref[...] = jnp.where(even, _dot(parts, sqb_ref[...]) + oqb_ref[...], q).astype(BF16)
        k = _dot(hn, wk_v[...])
        ka = jnp.where(even, k, _dot(parts, ska_ref[...]) + oka_ref[...])
        kb = jnp.where(even, _dot(parts, skb_ref[...]) + okb_ref[...], k)
        ka_ref[...] = ka.astype(BF16)
        kb_ref[...] = kb.astype(BF16)
        kat_ref[...] = ka.T.astype(BF16)
        kbt_ref[...] = kb.T.astype(BF16)
        v = _dot(hn, wv_v[...])
        va = jnp.where(even, v, oka_ref[...])
        vb = jnp.where(even, okb_ref[...], v)
        va_ref[...] = va.astype(BF16)
        vb_ref[...] = vb.astype(BF16)
        vat_ref[...] = va.T.astype(BF16)
        vbt_ref[...] = vb.T.astype(BF16)

    row = lambda i: (i, 0)
    col = lambda i: (0, i)
    const2 = lambda i: (0, 0)
    sd = jax.ShapeDtypeStruct((s_len, d), BF16)
    ds_ = jax.ShapeDtypeStruct((d, s_len), BF16)
    rs, cs = pl.BlockSpec((tm, d), row), pl.BlockSpec((d, tm), col)
    sel = pl.BlockSpec((N_SPARE * LANES, d), const2)
    one = pl.BlockSpec((1, d), const2)
    return pl.pallas_call(
        body, name="fox_proj_fwd", grid=(s_len // tm,),
        in_specs=[rs, one, ANY, ANY, ANY, pl.BlockSpec((d, LANES), const2), pl.BlockSpec((1, LANES), const2),
                  sel, sel, one, one, sel, sel, one, one],
        out_specs=[rs, rs, rs, rs, rs, cs, cs, rs, rs, cs, cs, pl.BlockSpec((tm, LANES), row)],
        out_shape=[sd, sd, sd, sd, sd, ds_, ds_, sd, sd, ds_, ds_, jax.ShapeDtypeStruct((s_len, LANES), F32)],
        scratch_shapes=[pltpu.VMEM((d, d), BF16), pltpu.VMEM((d, d), BF16), pltpu.VMEM((d, d), BF16),
                        pltpu.VMEM((8, LANES), F32), pltpu.SemaphoreType.DMA((3,))],
        compiler_params=_params(1),
    )(h, g_norm, wq, wk, wv, wf, bf, sq_a, sq_b, oq_a, oq_b, sk_a, sk_b, ok_a, ok_b)


def _spare_cols(x, base):
    xf = x[:, base:base + N_SPARE].astype(F32)
    return xf[:, 0:1] + xf[:, 1:2] + xf[:, 2:3]


def _with_query_term(x, term, base):
    lane = lax.broadcasted_iota(jnp.int32, x.shape, 1)
    hi, mid, lo = _split3(term)
    out = jnp.where(lane == base, hi.astype(F32), x)
    out = jnp.where(lane == base + 1, mid.astype(F32), out)
    out = jnp.where(lane == base + 2, lo.astype(F32), out)
    return jnp.where((lane >= base + N_SPARE) & (lane < base + 2 * N_SPARE), 1.0, out)


def _flash_fwd(qa, qb, kat, kbt, va, vb):
    s_len, d = qa.shape
    t = ATT_BLOCK
    w = min(ATT_CHUNK, s_len)
    n_pair = d // LANES
    n_q = s_len // t
    bases = (HEAD_DIM, 0)

    def body(qa_ref, qb_ref, kat_ref, kbt_ref, va_ref, vb_ref, o_ref, qa2_ref, qb2_ref, qat2_ref, qbt2_ref):
        i = pl.program_id(1)
        qs = (qa_ref[...], qb_ref[...])
        kts = (kat_ref, kbt_ref)
        vs = (va_ref, vb_ref)

        def step(kb, carry, masked):
            off = pl.multiple_of(kb * w, w)
            out = []
            for hh in range(2):
                m, acc = carry[hh]
                s = _dot(qs[hh], kts[hh][:, pl.ds(off, w)])
                if masked:
                    s = jnp.where(_causal(i * t, off, (t, w)), s, NEG_BIG)
                m_new = jnp.maximum(m, jnp.max(s, axis=1, keepdims=True))
                p = jnp.exp(s - m_new)
                out.append((m_new, acc * jnp.exp(m - m_new) + _dot(p.astype(BF16), vs[hh][pl.ds(off, w), :])))
            return tuple(out)

        init = ((jnp.full((t, 1), NEG_BIG, F32), jnp.zeros((t, LANES), F32)),) * 2
        diag = (i * t) // w
        carry = lax.fori_loop(0, diag, lambda kb, c: step(kb, c, False), init)
        carry = step(diag, carry, True)
        outs, q2 = [], []
        for hh in range(2):
            m, acc = carry[hh]
            l = acc[:, bases[hh]:bases[hh] + 1]
            outs.append(acc / l)
            term = _spare_cols(qs[hh], bases[hh]) - (m + jnp.log(l))
            q2.append(_with_query_term(qs[hh].astype(F32), term, bases[hh]))
        o_ref[...] = _pair_select(outs[0], outs[1], (t, LANES))
        qa2_ref[...] = q2[0].astype(BF16)
        qb2_ref[...] = q2[1].astype(BF16)
        qat2_ref[...] = q2[0].T.astype(BF16)
        qbt2_ref[...] = q2[1].T.astype(BF16)

    qblk = pl.BlockSpec((t, LANES), lambda j, i: (i, j))
    qblk_t = pl.BlockSpec((LANES, t), lambda j, i: (j, i))
    whole_t = pl.BlockSpec((LANES, s_len), lambda j, i: (j, 0))
    whole = pl.BlockSpec((s_len, LANES), lambda j, i: (0, j))
    sd = jax.ShapeDtypeStruct((s_len, d), BF16)
    ds_ = jax.ShapeDtypeStruct((d, s_len), BF16)
    return pl.pallas_call(
        body, name="flash_fwd", grid=(n_pair, n_q),
        in_specs=[qblk, qblk, whole_t, whole_t, whole, whole],
        out_specs=[qblk, qblk, qblk, qblk_t, qblk_t],
        out_shape=[jax.ShapeDtypeStruct((s_len, d), F32), sd, sd, ds_, ds_],
        compiler_params=_params(2),
    )(qa, qb, kat, kbt, va, vb)


def _flash_bwd_dq(qa, qb, kat, kbt, ka, kb_, vat, vbt, doa, dob):
    s_len, d = qa.shape
    t = ATT_BLOCK
    w = min(ATT_CHUNK, s_len)
    n_pair = d // LANES
    n_q = s_len // t

    def body(qa_ref, qb_ref, kat_ref, kbt_ref, ka_ref, kb_ref, vat_ref, vbt_ref, doa_ref, dob_ref, dq_ref, rs_ref):
        i = pl.program_id(1)
        qs = (qa_ref[...], qb_ref[...])
        dos = (doa_ref[...], dob_ref[...])
        kts, ks, vts = (kat_ref, kbt_ref), (ka_ref, kb_ref), (vat_ref, vbt_ref)

        def step(kb, carry, masked):
            off = pl.multiple_of(kb * w, w)
            out = []
            for hh in range(2):
                s = _dot(qs[hh], kts[hh][:, pl.ds(off, w)])
                if masked:
                    s = jnp.where(_causal(i * t, off, (t, w)), s, NEG_BIG)
                ds = jnp.exp(s) * _dot(dos[hh], vts[hh][:, pl.ds(off, w)])
                out.append(carry[hh] + _dot(ds.astype(BF16), ks[hh][pl.ds(off, w), :]))
            return tuple(out)

        diag = (i * t) // w
        carry = lax.fori_loop(0, diag, lambda kb, c: step(kb, c, False), (jnp.zeros((t, LANES), F32),) * 2)
        acc = step(diag, carry, True)
        dq_ref[...] = (_pair_select(acc[0], acc[1], (t, LANES)) * (HEAD_DIM ** -0.5)).astype(BF16)
        rs_ref[...] = _pair_select(acc[0][:, HEAD_DIM:HEAD_DIM + 1], acc[1][:, 0:1], (t, LANES))

    qblk = pl.BlockSpec((t, LANES), lambda j, i: (i, j))
    whole_t = pl.BlockSpec((LANES, s_len), lambda j, i: (j, 0))
    whole = pl.BlockSpec((s_len, LANES), lambda j, i: (0, j))
    return pl.pallas_call(
        body, name="flash_bwd_dq", grid=(n_pair, n_q),
        in_specs=[qblk, qblk, whole_t, whole_t, whole, whole, whole_t, whole_t, qblk, qblk],
        out_specs=[qblk, qblk],
        out_shape=[jax.ShapeDtypeStruct((s_len, d), BF16), jax.ShapeDtypeStruct((s_len, d), F32)],
        compiler_params=_params(2),
    )(qa, qb, kat, kbt, ka, kb_, vat, vbt, doa, dob)


def _flash_bwd_dkv(ka, kb_, va, vb, qa, qb, qat, qbt, doa, dob, doat, dobt):
    s_len, d = qa.shape
    t = ATT_BLOCK
    w = min(ATT_CHUNK, s_len)
    n_pair = d // LANES
    n_k = s_len // t

    def body(ka_ref, kb_ref, va_ref, vb_ref, qa_ref, qb_ref, qat_ref, qbt_ref, doa_ref, dob_ref, doat_ref, dobt_ref,
             dk_ref, dv_ref, cs_ref):
        kblk = pl.program_id(1)
        ks = (ka_ref[...], kb_ref[...])
        vs = (va_ref[...], vb_ref[...])
        qts, qs, dots, dos = (qat_ref, qbt_ref), (qa_ref, qb_ref), (doat_ref, dobt_ref), (doa_ref, dob_ref)

        def step(qb_, carry, masked):
            off = pl.multiple_of(qb_ * w, w)
            out = []
            for hh in range(2):
                dk_acc, dv_acc = carry[hh]
                st = _dot(ks[hh], qts[hh][:, pl.ds(off, w)])
                if masked:
                    keys = kblk * t + lax.broadcasted_iota(jnp.int32, (t, w), 0)
                    st = jnp.where(off + lax.broadcasted_iota(jnp.int32, (t, w), 1) >= keys, st, NEG_BIG)
                pt = jnp.exp(st)
                dst = pt * _dot(vs[hh], dots[hh][:, pl.ds(off, w)])
                out.append((dk_acc + _dot(dst.astype(BF16), qs[hh][pl.ds(off, w), :]),
                            dv_acc + _dot(pt.astype(BF16), dos[hh][pl.ds(off, w), :])))
            return tuple(out)

        zero = jnp.zeros((t, LANES), F32)
        diag = (kblk * t) // w
        carry = step(diag, ((zero, zero), (zero, zero)), True)
        carry = lax.fori_loop(diag + 1, s_len // w, lambda qb_, c: step(qb_, c, False), carry)
        dk_ref[...] = _pair_select(carry[0][0], carry[1][0], (t, LANES)).astype(BF16)
        dv_ref[...] = _pair_select(carry[0][1], carry[1][1], (t, LANES)).astype(BF16)
        cs_ref[...] = _pair_select(carry[0][0][:, HEAD_DIM + N_SPARE:HEAD_DIM + N_SPARE + 1],
                                   carry[1][0][:, N_SPARE:N_SPARE + 1], (t, LANES))

    kblk_spec = pl.BlockSpec((t, LANES), lambda j, i: (i, j))
    whole_t = pl.BlockSpec((LANES, s_len), lambda j, i: (j, 0))
    whole = pl.BlockSpec((s_len, LANES), lambda j, i: (0, j))
    sd = jax.ShapeDtypeStruct((s_len, d), BF16)
    return pl.pallas_call(
        body, name="flash_bwd_dkv", grid=(n_pair, n_k),
        in_specs=[kblk_spec] * 4 + [whole, whole, whole_t, whole_t, whole, whole, whole_t, whole_t],
        out_specs=[kblk_spec] * 3,
        out_shape=[sd, sd, jax.ShapeDtypeStruct((s_len, d), F32)],
        compiler_params=_params(2),
    )(ka, kb_, va, vb, qa, qb, qat, qbt, doa, dob, doat, dobt)


def _oproj_bwd(dh, o, wo, seg, sel_q, tm):
    s_len, d = dh.shape
    sq_a, sq_b, _, _ = sel_q

    def body(dh_ref, o_ref, wo_hbm, seg_ref, sqa_ref, sqb_ref, doa_ref, dob_ref, doat_ref, dobt_ref, wo_v, sem):
        _load_once([(wo_hbm, wo_v)], sem)
        do = _dot_nt(dh_ref[...].astype(BF16), wo_v[...])
        parts = _parts(-_dot3_lhs(do * o_ref[...], seg_ref[...]))
        even = _even_head_lanes((tm, d), 1)
        doa = jnp.where(even, do, _dot(parts, sqa_ref[...]))
        dob = jnp.where(even, _dot(parts, sqb_ref[...]), do)
        doa_ref[...] = doa.astype(BF16)
        dob_ref[...] = dob.astype(BF16)
        doat_ref[...] = doa.T.astype(BF16)
        dobt_ref[...] = dob.T.astype(BF16)

    row = lambda i: (i, 0)
    const2 = lambda i: (0, 0)
    rs, cs = pl.BlockSpec((tm, d), row), pl.BlockSpec((d, tm), lambda i: (0, i))
    sel = pl.BlockSpec((N_SPARE * LANES, d), const2)
    sd = jax.ShapeDtypeStruct((s_len, d), BF16)
    ds_ = jax.ShapeDtypeStruct((d, s_len), BF16)
    return pl.pallas_call(
        body, name="oproj_bwd", grid=(s_len // tm,),
        in_specs=[rs, rs, ANY, pl.BlockSpec((d, LANES), const2), sel, sel],
        out_specs=[rs, rs, cs, cs], out_shape=[sd, sd, ds_, ds_],
        scratch_shapes=[pltpu.VMEM((d, d), BF16), pltpu.SemaphoreType.DMA((1,))],
        compiler_params=_params(1),
    )(dh, o, wo, seg, sq_a, sq_b)


def _oproj_fwd(h, o, wo, tm):
    s_len, d = h.shape

    def body(h_ref, o_ref, wo_hbm, hout_ref, wo_v, sem):
        _load_once([(wo_hbm, wo_v)], sem)
        hout_ref[...] = h_ref[...] + _dot(o_ref[...].astype(BF16), wo_v[...])

    row = lambda i: (i, 0)
    return pl.pallas_call(
        body, name="oproj_fwd", grid=(s_len // tm,),
        in_specs=[pl.BlockSpec((tm, d), row), pl.BlockSpec((tm, d), row), ANY],
        out_specs=pl.BlockSpec((tm, d), row),
        out_shape=jax.ShapeDtypeStruct((s_len, d), F32),
        scratch_shapes=[pltpu.VMEM((d, d), BF16), pltpu.SemaphoreType.DMA((1,))],
        compiler_params=_params(1),
    )(h, o, wo)


def _forget_bwd(dcum, z, tm):
    s_len = dcum.shape[0]
    n_blk = s_len // tm

    def body(dc_ref, z_ref, dfl_ref, gb_ref, total):
        i = pl.program_id(0)

        @pl.when(i == 0)
        def _():
            total[...] = jnp.zeros_like(total)
            gb_ref[...] = jnp.zeros_like(gb_ref)

        upper = (lax.broadcasted_iota(jnp.int32, (tm, tm), 0) <= lax.broadcasted_iota(jnp.int32, (tm, tm), 1))
        suffix = _dot3_rhs(jnp.where(upper, 1.0, 0.0).astype(BF16), dc_ref[...]) + total[0:1, :]
        total[...] = jnp.broadcast_to(suffix[0:1, :], total.shape)
        dfl = suffix * _sigmoid(-z_ref[...])
        dfl_ref[...] = dfl
        gb_ref[...] += jnp.sum(dfl, axis=0, keepdims=True)

    rev = lambda i: (n_blk - 1 - i, 0)
    return pl.pallas_call(
        body, name="forget_bwd", grid=(n_blk,),
        in_specs=[pl.BlockSpec((tm, LANES), rev), pl.BlockSpec((tm, LANES), rev)],
        out_specs=[pl.BlockSpec((tm, LANES), rev), pl.BlockSpec((1, LANES), lambda i: (0, 0))],
        out_shape=[jax.ShapeDtypeStruct((s_len, LANES), F32), jax.ShapeDtypeStruct((1, LANES), F32)],
        scratch_shapes=[pltpu.VMEM((8, LANES), F32)],
        compiler_params=_params(1),
    )(dcum, z)


def _fox_proj_bwd(h, dh, dq, dk, dv, dfl, g_norm, wq, wk, wv, wf, tm):
    s_len, d = h.shape

    def body(h_ref, dh_ref, dq_ref, dk_ref, dv_ref, dfl_ref, g_ref, wq_hbm, wk_hbm, wv_hbm, wf_ref,
             dhin_ref, dflb_ref, gn_ref, wq_v, wk_v, wv_v, sem):
        _load_once([(wq_hbm, wq_v), (wk_hbm, wk_v), (wv_hbm, wv_v)], sem)

        @pl.when(pl.program_id(0) == 0)
        def _():
            gn_ref[...] = jnp.zeros_like(gn_ref)

        g = g_ref[...]
        xhat, r, _ = _rms_fwd(h_ref[...], g)
        dflb = dfl_ref[...].astype(BF16)
        dflb_ref[...] = dflb
        dhn = (_dot_nt(dq_ref[...], wq_v[...]) + _dot_nt(dk_ref[...], wk_v[...]) + _dot_nt(dv_ref[...], wv_v[...])
               + _dot_nt(dflb, wf_ref[...]))
        dx, gg = _rms_bwd(dhn, xhat, r, g)
        gn_ref[...] += jnp.sum(gg, axis=0, keepdims=True)
        dhin_ref[...] = dh_ref[...] + dx

    row = lambda i: (i, 0)
    const2 = lambda i: (0, 0)
    rs = pl.BlockSpec((tm, d), row)
    return pl.pallas_call(
        body, name="fox_proj_bwd", grid=(s_len // tm,),
        in_specs=[rs, rs, rs, rs, rs, pl.BlockSpec((tm, LANES), row), pl.BlockSpec((1, d), const2), ANY, ANY, ANY,
                  pl.BlockSpec((d, LANES), const2)],
        out_specs=[rs, pl.BlockSpec((tm, LANES), row), pl.BlockSpec((1, d), const2)],
        out_shape=[jax.ShapeDtypeStruct((s_len, d), F32), jax.ShapeDtypeStruct((s_len, LANES), BF16),
                   jax.ShapeDtypeStruct((1, d), F32)],
        scratch_shapes=[pltpu.VMEM((d, d), BF16), pltpu.VMEM((d, d), BF16), pltpu.VMEM((d, d), BF16),
                        pltpu.SemaphoreType.DMA((3,))],
        compiler_params=_params(1),
    )(h, dh, dq, dk, dv, dfl, g_norm, wq, wk, wv, wf)


def _loss_head(h, target, g_final, tm):
    s_len, d = h.shape
    n_blk = s_len // tm

    def body(h_ref, t_ref, g_ref, dh_ref, loss_ref, gg_ref, sq):
        i = pl.program_id(0)

        @pl.when(i == 0)
        def _():
            sq[...] = jnp.zeros_like(sq)
            gg_ref[...] = jnp.zeros_like(gg_ref)

        g = g_ref[...]
        xhat, r, y = _rms_fwd(h_ref[...], g)
        err = y - t_ref[...]
        sq[...] += jnp.sum(err * err, axis=0, keepdims=True)
        dx, gg = _rms_bwd(err * (1.0 / d), xhat, r, g)
        gg_ref[...] += jnp.sum(gg, axis=0, keepdims=True)
        dh_ref[...] = dx

        @pl.when(i == n_blk - 1)
        def _():
            loss_ref[...] = jnp.broadcast_to(jnp.sum(sq[...], axis=1, keepdims=True) * (0.5 / d), loss_ref.shape)

    row = lambda i: (i, 0)
    const2 = lambda i: (0, 0)
    return pl.pallas_call(
        body, name="loss_head", grid=(n_blk,),
        in_specs=[pl.BlockSpec((tm, d), row), pl.BlockSpec((tm, d), row), pl.BlockSpec((1, d), const2)],
        out_specs=[pl.BlockSpec((tm, d), row), pl.BlockSpec((1, LANES), const2), pl.BlockSpec((1, d), const2)],
        out_shape=[jax.ShapeDtypeStruct((s_len, d), F32), jax.ShapeDtypeStruct((1, LANES), F32),
                   jax.ShapeDtypeStruct((1, d), F32)],
        scratch_shapes=[pltpu.VMEM((1, d), F32)],
        compiler_params=_params(1),
    )(h, target, g_final)


def _wgrad(x, dy, n_piece, name):
    s_len, k = x.shape
    n = dy.shape[1]
    tn = min(n // n_piece, 1024)
    tk = min(k, 1024)
    ts = 512
    per_piece = (n // n_piece) // tn

    def body(x_ref, dy_ref, o_ref):
        @pl.when(pl.program_id(2) == 0)
        def _():
            o_ref[...] = jnp.zeros_like(o_ref)
        o_ref[0] += _dot_tn(x_ref[...].astype(BF16), dy_ref[...].astype(BF16))

    return pl.pallas_call(
        body, name=name, grid=(k // tk, n // tn, s_len // ts),
        in_specs=[pl.BlockSpec((ts, tk), lambda a, b, c: (c, a)), pl.BlockSpec((ts, tn), lambda a, b, c: (c, b))],
        out_specs=pl.BlockSpec((1, tk, tn), lambda a, b, c: (b // per_piece, a, b % per_piece)),
        out_shape=jax.ShapeDtypeStruct((n_piece, k, n // n_piece), F32),
        compiler_params=_params(3),
    )(x, dy)


def _pair_sum(g, recv, core, name):
    n_piece, rows, c = g.shape
    half = rows // 2
    tr = min(half, 512)
    nb = half // tr

    def body(core_ref, g_ref, r_ref, o_ref):
        o_ref[...] = g_ref[...] + r_ref[...]

    blk = pl.BlockSpec((1, tr, c), lambda p, i, core_ref: (p, i, 0))
    return pl.pallas_call(
        body, name=name, out_shape=jax.ShapeDtypeStruct((n_piece, half, c), F32),
        grid_spec=pltpu.PrefetchScalarGridSpec(
            num_scalar_prefetch=1, grid=(n_piece, nb),
            in_specs=[pl.BlockSpec((1, tr, c), lambda p, i, core_ref: (p, core_ref[0] * nb + i, 0)), blk],
            out_specs=blk),
        compiler_params=_params(2),
    )(core, g, recv)


def _chip_sum(halves, recv, chip, name):
    _, h, c = halves.shape
    tr = min(h, 512)

    def body(chip_ref, own_ref, r_ref, o_ref):
        o_ref[...] = ((own_ref[0] + r_ref[0]) + r_ref[1]) + r_ref[2]

    return pl.pallas_call(
        body, name=name, out_shape=jax.ShapeDtypeStruct((h, c), F32),
        grid_spec=pltpu.PrefetchScalarGridSpec(
            num_scalar_prefetch=1, grid=(h // tr,),
            in_specs=[pl.BlockSpec((1, tr, c), lambda i, chip_ref: (chip_ref[0], i, 0)),
                      pl.BlockSpec((3, tr, c), lambda i, chip_ref: (0, i, 0))],
            out_specs=pl.BlockSpec((tr, c), lambda i, chip_ref: (i, 0))),
        compiler_params=_params(1),
    )(chip, halves, recv)


def _adamw_math(w, m, v, g):
    m_new = ADAM_B1 * m + (1.0 - ADAM_B1) * g
    v_new = ADAM_B2 * v + (1.0 - ADAM_B2) * (g * g)
    m_hat = m_new / (1.0 - ADAM_B1 ** ADAM_STEP)
    v_hat = v_new / (1.0 - ADAM_B2 ** ADAM_STEP)
    return -ADAM_LR * (m_hat / (jnp.sqrt(v_hat) + ADAM_EPS) + ADAM_WD * w), m_new, v_new


def _adamw(w, m, v, g, name):
    rows, c = w.shape
    tr = min(rows, 256)

    def body(w_ref, m_ref, v_ref, g_ref, d_ref, mo_ref, vo_ref):
        d_ref[...], mo_ref[...], vo_ref[...] = _adamw_math(w_ref[...], m_ref[...], v_ref[...], g_ref[...])

    spec = pl.BlockSpec((tr, c), lambda i: (i, 0))
    shape = jax.ShapeDtypeStruct((rows, c), F32)
    return pl.pallas_call(
        body, name=name, grid=(rows // tr,),
        in_specs=[spec] * 4, out_specs=[spec] * 3, out_shape=[shape] * 3,
        compiler_params=_params(1),
    )(w, m, v, g)


def _adamw_halves(w, m, v, g_own, g_other, core, name):
    rows, c = w.shape
    half = rows // 2
    tr = min(half, 256)
    nb = half // tr

    def body(core_ref, w_ref, m_ref, v_ref, own_ref, other_ref, g_ref, d_ref, mo_ref, vo_ref):
        mine = (pl.program_id(0) // nb) == core_ref[0]
        g = jnp.where(mine, own_ref[...], other_ref[...])
        g_ref[...] = g
        d_ref[...], mo_ref[...], vo_ref[...] = _adamw_math(w_ref[...], m_ref[...], v_ref[...], g)

    spec = pl.BlockSpec((tr, c), lambda i, core_ref: (i, 0))
    own = pl.BlockSpec((tr, c), lambda i, core_ref: (jnp.clip(i - core_ref[0] * nb, 0, nb - 1), 0))
    other = pl.BlockSpec((tr, c), lambda i, core_ref: (jnp.clip(i - (1 - core_ref[0]) * nb, 0, nb - 1), 0))
    shape = jax.ShapeDtypeStruct((rows, c), F32)
    return pl.pallas_call(
        body, name=name, out_shape=[shape] * 4,
        grid_spec=pltpu.PrefetchScalarGridSpec(
            num_scalar_prefetch=1, grid=(rows // tr,),
            in_specs=[spec, spec, spec, own, other], out_specs=[spec] * 4),
        compiler_params=_params(1),
    )(core, w, m, v, g_own, g_other)


def _place():
    x, y, c = lax.axis_index("x"), lax.axis_index("y"), lax.axis_index("c")
    chips = [(1 - x, y), (x, 1 - y), (1 - x, 1 - y)]
    return x, y, c, chips


def _all_gather_chips(shards):
    n = len(shards)

    def body(*refs):
        ins, outs = refs[:n], refs[n:2 * n]
        send_sems, recv_sems, local_sems = refs[2 * n:]
        x, y, c, chips = _place()
        mine = 2 * x + y
        local = [pltpu.make_async_copy(ins[k], outs[k].at[mine], local_sems.at[k]) for k in range(n)]
        for cp in local:
            cp.start()
        sends = []
        for k in range(n):
            for j, (tx, ty) in enumerate(chips):
                sends.append(pltpu.make_async_remote_copy(
                    src_ref=ins[k], dst_ref=outs[k].at[mine], send_sem=send_sems.at[k, j], recv_sem=recv_sems.at[k, j],
                    device_id=(tx, ty, c), device_id_type=MESH))
        for cp in sends:
            cp.start()
        for k in range(n):
            for j, (tx, ty) in enumerate(chips):
                pltpu.make_async_remote_copy(
                    src_ref=ins[k], dst_ref=outs[k].at[2 * tx + ty], send_sem=send_sems.at[k, j],
                    recv_sem=recv_sems.at[k, j], device_id=(tx, ty, c), device_id_type=MESH).wait()
        for cp in local:
            cp.wait()

    return pl.pallas_call(
        body, name="weights_all_gather",
        in_specs=[ANY] * n, out_specs=[ANY] * n,
        out_shape=[jax.ShapeDtypeStruct((4,) + s.shape, s.dtype) for s in shards],
        scratch_shapes=[pltpu.SemaphoreType.DMA((n, 3)), pltpu.SemaphoreType.DMA((n, 3)), pltpu.SemaphoreType.DMA((n,))],
    )(*shards)


def _pair_exchange(grads):
    n = len(grads)

    def body(*refs):
        ins, outs = refs[:n], refs[n:2 * n]
        send_sems, recv_sems = refs[2 * n:]
        x, y, c, _ = _place()
        copies = []
        for k in range(n):
            half = grads[k].shape[1] // 2
            other = ins[k].at[:, pl.ds(pl.multiple_of((1 - c) * half, 8), half), :]
            copies.append(pltpu.make_async_remote_copy(
                src_ref=other, dst_ref=outs[k], send_sem=send_sems.at[k], recv_sem=recv_sems.at[k],
                device_id=(x, y, 1 - c), device_id_type=MESH))
        for cp in copies:
            cp.start()
        for cp in copies:
            cp.wait()

    return pl.pallas_call(
        body, name="grads_pair_exchange",
        in_specs=[ANY] * n, out_specs=[ANY] * n,
        out_shape=[jax.ShapeDtypeStruct((4, g.shape[1] // 2, g.shape[2]), F32) for g in grads],
        scratch_shapes=[pltpu.SemaphoreType.DMA((n,)), pltpu.SemaphoreType.DMA((n,))],
    )(*grads)


def _chip_scatter(halves):
    n = len(halves)

    def body(*refs):
        ins, outs = refs[:n], refs[n:2 * n]
        send_sems, recv_sems = refs[2 * n:]
        x, y, c, chips = _place()
        sends = []
        for k in range(n):
            for j, (tx, ty) in enumerate(chips):
                sends.append(pltpu.make_async_remote_copy(
                    src_ref=ins[k].at[2 * tx + ty], dst_ref=outs[k].at[j], send_sem=send_sems.at[k, j],
                    recv_sem=recv_sems.at[k, j], device_id=(tx, ty, c), device_id_type=MESH))
        for cp in sends:
            cp.start()
        for cp in sends:
            cp.wait()

    return pl.pallas_call(
        body, name="grads_chip_scatter",
        in_specs=[ANY] * n, out_specs=[ANY] * n,
        out_shape=[jax.ShapeDtypeStruct((3,) + hv.shape[1:], F32) for hv in halves],
        scratch_shapes=[pltpu.SemaphoreType.DMA((n, 3)), pltpu.SemaphoreType.DMA((n, 3))],
    )(*halves)


def _pair_share(finals):
    n = len(finals)

    def body(*refs):
        ins, outs = refs[:n], refs[n:2 * n]
        send_sems, recv_sems = refs[2 * n:]
        x, y, c, _ = _place()
        copies = [pltpu.make_async_remote_copy(
            src_ref=ins[k], dst_ref=outs[k], send_sem=send_sems.at[k], recv_sem=recv_sems.at[k],
            device_id=(x, y, 1 - c), device_id_type=MESH) for k in range(n)]
        for cp in copies:
            cp.start()
        for cp in copies:
            cp.wait()

    return pl.pallas_call(
        body, name="grads_pair_share",
        in_specs=[ANY] * n, out_specs=[ANY] * n,
        out_shape=[jax.ShapeDtypeStruct(fv.shape, F32) for fv in finals],
        scratch_shapes=[pltpu.SemaphoreType.DMA((n,)), pltpu.SemaphoreType.DMA((n,))],
    )(*finals)


def _small_all_reduce(buf):
    rows, c_ = buf.shape

    def body(in_ref, out_ref, pair_buf, slots, send_sems, recv_sems):
        x, y, c, chips = _place()
        mine = 2 * x + y
        pair = pltpu.make_async_remote_copy(
            src_ref=in_ref, dst_ref=pair_buf, send_sem=send_sems.at[0], recv_sem=recv_sems.at[0],
            device_id=(x, y, 1 - c), device_id_type=MESH)
        pair.start()
        pair.wait()
        slots[mine] = in_ref[...] + pair_buf[...]
        sends = [pltpu.make_async_remote_copy(
            src_ref=slots.at[mine], dst_ref=slots.at[mine], send_sem=send_sems.at[1 + j], recv_sem=recv_sems.at[1 + j],
            device_id=(tx, ty, c), device_id_type=MESH) for j, (tx, ty) in enumerate(chips)]
        for cp in sends:
            cp.start()
        for j, (tx, ty) in enumerate(chips):
            pltpu.make_async_remote_copy(
                src_ref=slots.at[mine], dst_ref=slots.at[2 * tx + ty], send_sem=send_sems.at[1 + j],
                recv_sem=recv_sems.at[1 + j], device_id=(tx, ty, c), device_id_type=MESH).wait()
        out_ref[...] = ((slots[0] + slots[1]) + slots[2]) + slots[3]

    vm = pl.BlockSpec(memory_space=pltpu.VMEM)
    return pl.pallas_call(
        body, name="small_all_reduce", in_specs=[vm], out_specs=vm,
        out_shape=jax.ShapeDtypeStruct((rows, c_), F32),
        scratch_shapes=[pltpu.VMEM((rows, c_), F32), pltpu.VMEM((4, rows, c_), F32),
                        pltpu.SemaphoreType.DMA((4,)), pltpu.SemaphoreType.DMA((4,))],
        compiler_params=pltpu.CompilerParams(vmem_limit_bytes=VMEM_LIMIT_V7X),
    )(buf)


def _reduce_scatter(grads):
    core = lax.axis_index("c").astype(jnp.int32).reshape(1)
    chip = (2 * lax.axis_index("x") + lax.axis_index("y")).astype(jnp.int32).reshape(1)
    recv = _pair_exchange(grads)
    halves = [_pair_sum(g, r, core, f"pair_sum_{k}") for k, (g, r) in enumerate(zip(grads, recv))]
    recv = _chip_scatter(halves)
    finals = [_chip_sum(hv, r, chip, f"chip_sum_{k}") for k, (hv, r) in enumerate(zip(halves, recv))]
    return list(zip(finals, _pair_share(finals))), core


PACK_COLS = 1024


def _pack(arrays):
    flat = jnp.concatenate([a.reshape(-1).astype(F32) for a in arrays])
    rows = -(-flat.shape[0] // PACK_COLS)
    rows = -(-rows // 8) * 8
    return jnp.pad(flat, (0, rows * PACK_COLS - flat.shape[0])).reshape(rows, PACK_COLS)


def _unpack(buf, shapes):
    flat = buf.reshape(-1)
    out, at = [], 0
    for shp in shapes:
        size = math.prod(shp)
        out.append(flat[at:at + size].reshape(shp))
        at += size
    return out


def kernel(x, mix_norm_g, ffn_norm_g, gm_w_in, gm_ln_g, gm_ln_b, gm_w_s, gm_b_s, gm_w_out, fox_w_qkvf, fox_b_f, fox_w_o, ffn_w_gate, ffn_w_up, ffn_conv_w, ffn_conv_b, ffn_w_down, final_norm_g, loss_target, m_mix_norm_g, m_ffn_norm_g, m_gm_w_in, m_gm_ln_g, m_gm_ln_b, m_gm_w_s, m_gm_b_s, m_gm_w_out, m_fox_w_qkvf, m_fox_b_f, m_fox_w_o, m_ffn_w_gate, m_ffn_w_up, m_ffn_conv_w, m_ffn_conv_b, m_ffn_w_down, m_final_norm_g, v_mix_norm_g, v_ffn_norm_g, v_gm_w_in, v_gm_ln_g, v_gm_ln_b, v_gm_w_s, v_gm_b_s, v_gm_w_out, v_fox_w_qkvf, v_fox_b_f, v_fox_w_o, v_ffn_w_gate, v_ffn_w_up, v_ffn_conv_w, v_ffn_conv_b, v_ffn_w_down, v_final_norm_g):
    _, s_len, d = x.shape
    e = gm_ln_g.shape[1]
    f = ffn_conv_b.shape[1]
    n_head = fox_b_f.shape[1]
    n_pair = n_head // 2
    gd = e // GM_GROUPS
    qkvf_cols = fox_w_qkvf.shape[2]
    assert d == n_head * HEAD_DIM and d % (2 * LANES) == 0 and s_len % 512 == 0 and gd % LANES == 0
    assert gm_w_s.shape[2] == CHUNK and 4 * qkvf_cols == 3 * d + n_head
    tm = 256
    h0 = x[0]
    target = loss_target[0]

    gathered = _all_gather_chips([
        gm_w_in[0].astype(BF16), gm_w_out[0].astype(BF16), fox_w_qkvf[0].astype(BF16), fox_w_o[0].astype(BF16),
        ffn_w_gate.astype(BF16), ffn_w_up.astype(BF16), ffn_w_down.astype(BF16), ffn_conv_w])
    w_in, w_out4, qkvf4, wo4, wg_all, wu_all, wd_all, cw4 = gathered
    w_out = w_out4.reshape(e, d)
    qkvf = jnp.transpose(qkvf4, (1, 0, 2)).reshape(d, 4 * qkvf_cols)
    wq, wk, wv = qkvf[:, :d], qkvf[:, d:2 * d], qkvf[:, 2 * d:3 * d]
    wf = jnp.pad(qkvf[:, 3 * d:], ((0, 0), (0, LANES - n_head)))
    wo = wo4.reshape(d, d)
    conv_w_full = jnp.transpose(cw4, (1, 2, 0, 3)).reshape(2, 3, f)
    conv_w8 = jnp.pad(conv_w_full, ((0, 0), (0, 5), (0, 0)))
    bf_pad = jnp.pad(fox_b_f, ((0, 0), (0, LANES - n_head)))

    tril = jnp.tril(jnp.ones((CHUNK, CHUNK), bool))
    wc = jnp.where(tril[None], gm_w_s[0], 0.0).astype(BF16)
    wct = jnp.transpose(wc, (0, 2, 1))
    bias = jnp.repeat(gm_b_s[0].T, gd, axis=1)
    seg_groups = (jnp.arange(e)[:, None] // gd == jnp.arange(LANES)[None, :]).astype(BF16)
    seg_heads = (jnp.arange(d)[:, None] // HEAD_DIM == jnp.arange(LANES)[None, :]).astype(BF16)
    sel_q = _spare_selectors(d, key_side=False)
    sel_k = _spare_selectors(d, key_side=True)

    h1, a0, hn0, gated0 = _gmlp_fwd(h0, mix_norm_g[0:1], w_in, gm_ln_g, gm_ln_b, wc, bias, w_out, tm)
    h2, fa0, fup0, fhn0, fhid0 = _ffn_fwd(h1, ffn_norm_g[0:1], wg_all, wu_all, wd_all, 0, conv_w8[0], ffn_conv_b[0:1], tm)
    (hn1, qa, qb, ka, kb_, kat, kbt, va, vb, vat, vbt, z_f) = _fox_proj_fwd(
        h2, mix_norm_g[1:2], wq, wk, wv, wf, bf_pad, sel_q, sel_k, tm)
    o, qa2, qb2, qat2, qbt2 = _flash_fwd(qa, qb, kat, kbt, va, vb)
    h3 = _oproj_fwd(h2, o, wo, tm)
    h4, fa1, fup1, fhn1, fhid1 = _ffn_fwd(h3, ffn_norm_g[1:2], wg_all, wu_all, wd_all, 1, conv_w8[1], ffn_conv_b[1:2], tm)

    dh4, loss_part, g_final = _loss_head(h4, target, final_norm_g.reshape(1, d), tm)
    dh3, da1, dup1, gcw1, gcb1, gfn1 = _ffn_bwd(h3, dh4, fa1, fup1, ffn_norm_g[1:2], wg_all, wu_all, wd_all, 1,
                                                conv_w8[1], ffn_conv_b[1:2], tm)
    g_gate1 = _wgrad(fhn1, da1, 4, "wgrad_gate_1")
    g_up1 = _wgrad(fhn1, dup1, 4, "wgrad_up_1")
    g_down1 = _wgrad(fhid1, dh4, 1, "wgrad_down_1").reshape(4, f // 4, d)

    doa, dob, doat, dobt = _oproj_bwd(dh3, o, wo, seg_heads, sel_q, tm)
    g_wo = _wgrad(o, dh3, 1, "wgrad_wo").reshape(4, d // 4, d)
    dq, row_sums = _flash_bwd_dq(qa2, qb2, kat, kbt, ka, kb_, vat, vbt, doa, dob)
    dk, dv, col_sums = _flash_bwd_dkv(ka, kb_, va, vb, qa2, qb2, qat2, qbt2, doa, dob, doat, dobt)
    dcum = jnp.pad((row_sums - col_sums)[:, ::HEAD_DIM], ((0, 0), (0, LANES - n_head)))
    dfl, g_bf = _forget_bwd(dcum, z_f, tm)
    dh2, dflb, gmn1 = _fox_proj_bwd(h2, dh3, dq, dk, dv, dfl, mix_norm_g[1:2], wq, wk, wv, wf, tm)
    g_q = _wgrad(hn1, dq, 1, "wgrad_q")[0]
    g_k = _wgrad(hn1, dk, 1, "wgrad_k")[0]
    g_v = _wgrad(hn1, dv, 1, "wgrad_v")[0]
    g_f = _wgrad(hn1, dflb, 1, "wgrad_f")[0][:, :n_head]
    g_qkvf = jnp.concatenate([g_q, g_k, g_v, g_f], axis=1).reshape(d, 4, qkvf_cols).transpose(1, 0, 2)

    dh1, da0f, dup0, gcw0, gcb0, gfn0 = _ffn_bwd(h1, dh2, fa0, fup0, ffn_norm_g[0:1], wg_all, wu_all, wd_all, 0,
                                                 conv_w8[0], ffn_conv_b[0:1], tm)
    g_gate0 = _wgrad(fhn0, da0f, 4, "wgrad_gate_0")
    g_up0 = _wgrad(fhn0, dup0, 4, "wgrad_up_0")
    g_down0 = _wgrad(fhid0, dh2, 1, "wgrad_down_0").reshape(4, f // 4, d)

    dh0, da0, g_ws, g_bs_t, g_lng, g_lnb, gmn0 = _gmlp_bwd(
        h0, dh1, a0, mix_norm_g[0:1], w_in, gm_ln_g, gm_ln_b, wc, wct, bias, w_out, seg_groups, CHUNK)
    g_win = _wgrad(hn0, da0, 4, "wgrad_gm_in")
    g_wout = _wgrad(gated0, dh1, 1, "wgrad_gm_out").reshape(4, e // 4, d)

    big, core = _reduce_scatter([g_win, g_wout, g_qkvf, g_wo, g_gate0, g_gate1, g_up0, g_up1, g_down0, g_down1])
    r_win, r_wout, r_qkvf, r_wo, r_gate0, r_gate1, r_up0, r_up1, r_down0, r_down1 = big

    small = [jnp.concatenate([gmn0, gmn1]), jnp.concatenate([gfn0, gfn1]), g_lng, g_lnb, g_ws[None],
             g_bs_t[:, :GM_GROUPS].T[None], g_bf[:, :n_head], jnp.stack([gcw0[:3], gcw1[:3]]),
             jnp.concatenate([gcb0, gcb1]), g_final.reshape(d), loss_part[0, :1]]
    small_shapes = [a.shape for a in small]
    reduced = _unpack(_small_all_reduce(_pack(small)), small_shapes)
    (r_mix, r_ffn, r_lng, r_lnb, r_ws, r_bs, r_bf, r_cw_full, r_cb, r_final, r_loss) = reduced
    chip = 2 * lax.axis_index("x") + lax.axis_index("y")
    r_cw = lax.dynamic_slice_in_dim(r_cw_full, chip * (f // 4), f // 4, axis=2)

    def update_big(name, w, m, v, per_layer):
        parts = [_adamw_halves(w[l], m[l], v[l], own, other, core, f"adamw_{name}_{l}")
                 for l, (own, other) in enumerate(per_layer)]
        return tuple(jnp.stack([p[i] for p in parts]) for i in range(4))

    res = {}
    res["gm_w_in"] = update_big("gm_w_in", gm_w_in, m_gm_w_in, v_gm_w_in, [r_win])
    res["gm_w_out"] = update_big("gm_w_out", gm_w_out, m_gm_w_out, v_gm_w_out, [r_wout])
    res["fox_w_qkvf"] = update_big("fox_w_qkvf", fox_w_qkvf, m_fox_w_qkvf, v_fox_w_qkvf, [r_qkvf])
    res["fox_w_o"] = update_big("fox_w_o", fox_w_o, m_fox_w_o, v_fox_w_o, [r_wo])
    res["ffn_w_gate"] = update_big("ffn_w_gate", ffn_w_gate, m_ffn_w_gate, v_ffn_w_gate, [r_gate0, r_gate1])
    res["ffn_w_up"] = update_big("ffn_w_up", ffn_w_up, m_ffn_w_up, v_ffn_w_up, [r_up0, r_up1])
    res["ffn_w_down"] = update_big("ffn_w_down", ffn_w_down, m_ffn_w_down, v_ffn_w_down, [r_down0, r_down1])

    small_names = ["mix_norm_g", "ffn_norm_g", "gm_ln_g", "gm_ln_b", "gm_w_s", "gm_b_s", "fox_b_f", "ffn_conv_w",
                   "ffn_conv_b", "final_norm_g"]
    small_w = [mix_norm_g, ffn_norm_g, gm_ln_g, gm_ln_b, gm_w_s, gm_b_s, fox_b_f, ffn_conv_w, ffn_conv_b, final_norm_g]
    small_m = [m_mix_norm_g, m_ffn_norm_g, m_gm_ln_g, m_gm_ln_b, m_gm_w_s, m_gm_b_s, m_fox_b_f, m_ffn_conv_w,
               m_ffn_conv_b, m_final_norm_g]
    small_v = [v_mix_norm_g, v_ffn_norm_g, v_gm_ln_g, v_gm_ln_b, v_gm_w_s, v_gm_b_s, v_fox_b_f, v_ffn_conv_w,
               v_ffn_conv_b, v_final_norm_g]
    small_g = [r_mix, r_ffn, r_lng, r_lnb, r_ws, r_bs, r_bf, r_cw, r_cb, r_final]
    shapes = [w.shape for w in small_w]
    small_g = [g.reshape(s) for g, s in zip(small_g, shapes)]
    dlt, mn, vn = _adamw(_pack(small_w), _pack(small_m), _pack(small_v), _pack(small_g), "adamw_small")
    for name, g, dl_, m_, v_ in zip(small_names, small_g, _unpack(dlt, shapes), _unpack(mn, shapes), _unpack(vn, shapes)):
        res[name] = (g, dl_, m_, v_)

    order = ["mix_norm_g", "ffn_norm_g", "gm_w_in", "gm_ln_g", "gm_ln_b", "gm_w_s", "gm_b_s", "gm_w_out", "fox_w_qkvf",
             "fox_b_f", "fox_w_o", "ffn_w_gate", "ffn_w_up", "ffn_conv_w", "ffn_conv_b", "ffn_w_down", "final_norm_g"]
    outs = [r_loss.reshape(()), dh0[None]]
    for part in range(4):
        outs += [res[name][part] for name in order]
    return tuple(outs)
```

```python
import functools
import math

import jax
import jax.numpy as jnp
from jax import lax
from jax.experimental import pallas as pl
from jax.experimental.pallas import tpu as pltpu

F32 = jnp.float32
BF16 = jnp.bfloat16

RMS_EPS = 1e-6
LN_EPS = 1e-5
CHUNK = 128
GM_GROUPS = 8
HEAD_DIM = 64
LANES = 128
ATT_BLOCK = 256
ATT_CHUNK = 1024
VMEM_LIMIT_V7X = 56 * 1024 * 1024

ADAM_LR = 0.001
ADAM_B1 = 0.9
ADAM_B2 = 0.999
ADAM_EPS = 1e-08
ADAM_WD = 0.01
ADAM_STEP = 10

MESH = pl.DeviceIdType.MESH
ANY = pl.BlockSpec(memory_space=pl.ANY)
NEG_BIG = -1e30


def _params(n_grid):
    return pltpu.CompilerParams(dimension_semantics=("arbitrary",) * n_grid, vmem_limit_bytes=VMEM_LIMIT_V7X)


def _dot(a, b):
    return jnp.dot(a, b, preferred_element_type=F32)


def _dot_nt(a, b):
    return lax.dot_general(a, b, (((1,), (1,)), ((), ())), preferred_element_type=F32)


def _dot_tn(a, b):
    return lax.dot_general(a, b, (((0,), (0,)), ((), ())), preferred_element_type=F32)


def _split3(x):
    hi = x.astype(BF16)
    r = x - hi.astype(F32)
    mid = r.astype(BF16)
    lo = (r - mid.astype(F32)).astype(BF16)
    return hi, mid, lo


def _dot3_lhs(x, m):
    hi, mid, lo = _split3(x)
    return _dot(hi, m) + _dot(mid, m) + _dot(lo, m)


def _dot3_rhs(m, x):
    hi, mid, lo = _split3(x)
    return _dot(m, hi) + _dot(m, mid) + _dot(m, lo)


def _load_once(pairs, sem):
    @pl.when(pl.program_id(0) == 0)
    def _():
        copies = [pltpu.make_async_copy(src, dst, sem.at[k]) for k, (src, dst) in enumerate(pairs)]
        for cp in copies:
            cp.start()
        for cp in copies:
            cp.wait()


def _rms_fwd(x, g):
    r = lax.rsqrt(jnp.mean(x * x, axis=-1, keepdims=True) + RMS_EPS)
    xhat = x * r
    return xhat, r, xhat * g


def _rms_bwd(dy, xhat, r, g):
    w = dy * g
    dx = r * (w - xhat * jnp.mean(w * xhat, axis=-1, keepdims=True))
    return dx, dy * xhat


def _gelu_parts(a):
    c = math.sqrt(2.0 / math.pi)
    a2 = a * a
    t = jnp.tanh(c * (a + 0.044715 * a * a2))
    z = 0.5 * a * (1.0 + t)
    dz = 0.5 * (1.0 + t) + 0.5 * a * (1.0 - t * t) * (c * (1.0 + 3.0 * 0.044715 * a2))
    return z, dz


def _sigmoid(x):
    return 1.0 / (1.0 + jnp.exp(-x))


def _gmlp_core(a, lng, lnb, wc_ref, bias, n_chunk, gd):
    e = a.shape[1] // 2
    z, dz = _gelu_parts(a)
    u = z[:, :e]
    v = z[:, e:]
    mu = jnp.mean(v, axis=-1, keepdims=True)
    vc = v - mu
    rstd = lax.rsqrt(jnp.mean(vc * vc, axis=-1, keepdims=True) + LN_EPS)
    vhat = vc * rstd
    vln = vhat * lng + lnb
    vlb = vln.astype(BF16)
    rows = []
    for ci in range(n_chunk):
        cols = []
        for g in range(GM_GROUPS):
            blk = vlb[ci * CHUNK:(ci + 1) * CHUNK, g * gd:(g + 1) * gd]
            cols.append(_dot(wc_ref[g], blk))
        rows.append(jnp.concatenate(cols, axis=1) + bias)
    s = rows[0] if n_chunk == 1 else jnp.concatenate(rows, axis=0)
    return dz, u, vhat, rstd, vlb, s


def _gmlp_fwd(h, g_mix, w_in, lng, lnb, wc, bias, w_out, tm):
    s_len, d = h.shape
    n_p, _, w = w_in.shape
    e = w_out.shape[0]
    gd = e // GM_GROUPS
    n_chunk = tm // CHUNK

    def body(h_ref, g_ref, win_hbm, lng_ref, lnb_ref, wc_ref, bias_ref, wout_hbm,
             hout_ref, a_ref, hn_ref, gated_ref, win_v, wout_v, sem):
        _load_once([(win_hbm, win_v), (wout_hbm, wout_v)], sem)
        x = h_ref[...]
        _, _, y = _rms_fwd(x, g_ref[...])
        hn = y.astype(BF16)
        hn_ref[...] = hn
        for p in range(n_p):
            a_ref[:, p * w:(p + 1) * w] = _dot(hn, win_v[p])
        _, u, _, _, _, s = _gmlp_core(a_ref[...], lng_ref[...], lnb_ref[...], wc_ref, bias_ref[...], n_chunk, gd)
        gated = (u * s).astype(BF16)
        gated_ref[...] = gated
        hout_ref[...] = x + _dot(gated, wout_v[...])

    row = lambda i: (i, 0)
    const2 = lambda i: (0, 0)
    return pl.pallas_call(
        body, name="gmlp_fwd", grid=(s_len // tm,),
        in_specs=[pl.BlockSpec((tm, d), row), pl.BlockSpec((1, d), const2), ANY,
                  pl.BlockSpec((1, e), const2), pl.BlockSpec((1, e), const2),
                  pl.BlockSpec(wc.shape, lambda i: (0, 0, 0)), pl.BlockSpec((CHUNK, e), const2), ANY],
        out_specs=[pl.BlockSpec((tm, d), row), pl.BlockSpec((tm, 2 * e), row),
                   pl.BlockSpec((tm, d), row), pl.BlockSpec((tm, e), row)],
        out_shape=[jax.ShapeDtypeStruct((s_len, d), F32), jax.ShapeDtypeStruct((s_len, 2 * e), F32),
                   jax.ShapeDtypeStruct((s_len, d), BF16), jax.ShapeDtypeStruct((s_len, e), BF16)],
        scratch_shapes=[pltpu.VMEM(w_in.shape, BF16), pltpu.VMEM(w_out.shape, BF16), pltpu.SemaphoreType.DMA((2,))],
        compiler_params=_params(1),
    )(h, g_mix, w_in, lng, lnb, wc, bias, w_out)


def _gmlp_bwd(h, dh, a, g_mix, w_in, lng, lnb, wc, wct, bias, w_out, seg, tm):
    s_len, d = h.shape
    n_p, _, w = w_in.shape
    e = w_out.shape[0]
    gd = e // GM_GROUPS
    n_chunk = tm // CHUNK
    n_blk = s_len // tm

    def body(h_ref, dh_ref, a_ref, g_ref, win_hbm, lng_ref, lnb_ref, wc_ref, wct_ref, bias_ref, wout_hbm, seg_ref,
             dhin_ref, da_ref, gws_ref, gbs_ref, glng_ref, glnb_ref, gmix_ref, win_v, wout_v, dsum, sem):
        i = pl.program_id(0)
        _load_once([(win_hbm, win_v), (wout_hbm, wout_v)], sem)

        @pl.when(i == 0)
        def _():
            gws_ref[...] = jnp.zeros_like(gws_ref)
            glng_ref[...] = jnp.zeros_like(glng_ref)
            glnb_ref[...] = jnp.zeros_like(glnb_ref)
            gmix_ref[...] = jnp.zeros_like(gmix_ref)
            dsum[...] = jnp.zeros_like(dsum)

        x = h_ref[...]
        dh_v = dh_ref[...]
        g = g_ref[...]
        lng_v = lng_ref[...]
        xhat, r, _ = _rms_fwd(x, g)
        dz_da, u, vhat, rstd, vlb, s = _gmlp_core(a_ref[...], lng_v, lnb_ref[...], wc_ref, bias_ref[...], n_chunk, gd)
        dg = _dot_nt(dh_v.astype(BF16), wout_v[...])
        du = dg * s
        ds = dg * u
        dsb = ds.astype(BF16)
        rows = []
        ds_acc = None
        for ci in range(n_chunk):
            lo, hi = ci * CHUNK, (ci + 1) * CHUNK
            cols = []
            for gi in range(GM_GROUPS):
                d_blk = dsb[lo:hi, gi * gd:(gi + 1) * gd]
                gws_ref[gi] += _dot_nt(d_blk, vlb[lo:hi, gi * gd:(gi + 1) * gd])
                cols.append(_dot(wct_ref[gi], d_blk))
            rows.append(jnp.concatenate(cols, axis=1))
            ds_acc = ds[lo:hi] if ds_acc is None else ds_acc + ds[lo:hi]
        dsum[...] += ds_acc
        dvln = rows[0] if n_chunk == 1 else jnp.concatenate(rows, axis=0)
        glng_ref[...] += jnp.sum(dvln * vhat, axis=0, keepdims=True)
        glnb_ref[...] += jnp.sum(dvln, axis=0, keepdims=True)
        dvhat = dvln * lng_v
        dv = rstd * (dvhat - jnp.mean(dvhat, axis=-1, keepdims=True)
                     - vhat * jnp.mean(dvhat * vhat, axis=-1, keepdims=True))
        da = jnp.concatenate([du, dv], axis=1) * dz_da
        dab = da.astype(BF16)
        da_ref[...] = dab
        dhn = _dot_nt(dab[:, :w], win_v[0])
        for p in range(1, n_p):
            dhn += _dot_nt(dab[:, p * w:(p + 1) * w], win_v[p])
        dx, gg = _rms_bwd(dhn, xhat, r, g)
        gmix_ref[...] += jnp.sum(gg, axis=0, keepdims=True)
        dhin_ref[...] = dh_v + dx

        @pl.when(i == n_blk - 1)
        def _():
            tril = lax.broadcasted_iota(jnp.int32, (CHUNK, CHUNK), 0) >= lax.broadcasted_iota(jnp.int32, (CHUNK, CHUNK), 1)
            for gi in range(GM_GROUPS):
                gws_ref[gi] = jnp.where(tril, gws_ref[gi], 0.0)
            gbs_ref[...] = _dot3_lhs(dsum[...], seg_ref[...])

    row = lambda i: (i, 0)
    const2 = lambda i: (0, 0)
    const3 = lambda i: (0, 0, 0)
    return pl.pallas_call(
        body, name="gmlp_bwd", grid=(n_blk,),
        in_specs=[pl.BlockSpec((tm, d), row), pl.BlockSpec((tm, d), row), pl.BlockSpec((tm, 2 * e), row),
                  pl.BlockSpec((1, d), const2), ANY, pl.BlockSpec((1, e), const2), pl.BlockSpec((1, e), const2),
                  pl.BlockSpec(wc.shape, const3), pl.BlockSpec(wct.shape, const3), pl.BlockSpec((CHUNK, e), const2),
                  ANY, pl.BlockSpec((e, LANES), const2)],
        out_specs=[pl.BlockSpec((tm, d), row), pl.BlockSpec((tm, 2 * e), row), pl.BlockSpec(wc.shape, const3),
                   pl.BlockSpec((CHUNK, LANES), const2), pl.BlockSpec((1, e), const2), pl.BlockSpec((1, e), const2),
                   pl.BlockSpec((1, d), const2)],
        out_shape=[jax.ShapeDtypeStruct((s_len, d), F32), jax.ShapeDtypeStruct((s_len, 2 * e), BF16),
                   jax.ShapeDtypeStruct(wc.shape, F32), jax.ShapeDtypeStruct((CHUNK, LANES), F32),
                   jax.ShapeDtypeStruct((1, e), F32), jax.ShapeDtypeStruct((1, e), F32), jax.ShapeDtypeStruct((1, d), F32)],
        scratch_shapes=[pltpu.VMEM(w_in.shape, BF16), pltpu.VMEM(w_out.shape, BF16), pltpu.VMEM((CHUNK, e), F32),
                        pltpu.SemaphoreType.DMA((2,))],
        compiler_params=_params(1),
    )(h, dh, a, g_mix, w_in, lng, lnb, wc, wct, bias, w_out, seg)


def _shift_down(a, k, fill):
    tm = a.shape[0]
    out = pltpu.roll(a, k, 0)
    rid = lax.broadcasted_iota(jnp.int32, a.shape, 0)
    for j in range(k):
        out = jnp.where(rid == j, fill[8 - k + j:8 - k + j + 1, :], out)
    return out


def _shift_up(a, k, fill):
    tm = a.shape[0]
    out = pltpu.roll(a, tm - k, 0)
    rid = lax.broadcasted_iota(jnp.int32, a.shape, 0)
    for j in range(k):
        out = jnp.where(rid == tm - k + j, fill[j:j + 1, :], out)
    return out


def _ffn_fwd(h, g_norm, wg_all, wu_all, wd_all, layer, conv_w, conv_b, tm):
    s_len, d = h.shape
    n_p = wg_all.shape[0]
    fq = wg_all.shape[3]
    f = n_p * fq

    def body(h_ref, g_ref, wg_hbm, wu_hbm, wd_hbm, cw_ref, cb_ref,
             hout_ref, a_ref, up_ref, hn_ref, hid_ref, wg_v, wu_v, wd_v, carry, sem):
        i = pl.program_id(0)
        _load_once([(wg_hbm.at[:, layer], wg_v), (wu_hbm.at[:, layer], wu_v), (wd_hbm.at[:, layer], wd_v)], sem)

        @pl.when(i == 0)
        def _():
            carry[...] = jnp.zeros_like(carry)

        x = h_ref[...]
        _, _, y = _rms_fwd(x, g_ref[...])
        hn = y.astype(BF16)
        hn_ref[...] = hn
        for p in range(n_p):
            a_ref[:, p * fq:(p + 1) * fq] = _dot(hn, wg_v[p])
            up_ref[:, p * fq:(p + 1) * fq] = _dot(hn, wu_v[p])
        a = a_ref[...]
        prev = carry[...]
        am1 = _shift_down(a, 1, prev)
        am2 = _shift_down(a, 2, prev)
        carry[...] = a[tm - 8:tm, :]
        cw = cw_ref[...]
        ac = cb_ref[...] + am2 * cw[0:1, :]
        ac = ac + am1 * cw[1:2, :]
        ac = ac + a * cw[2:3, :]
        hid = (ac * _sigmoid(ac) * up_ref[...]).astype(BF16)
        hid_ref[...] = hid
        y2 = _dot(hid[:, :fq], wd_v[0])
        for p in range(1, n_p):
            y2 += _dot(hid[:, p * fq:(p + 1) * fq], wd_v[p])
        hout_ref[...] = x + y2

    row = lambda i: (i, 0)
    const2 = lambda i: (0, 0)
    return pl.pallas_call(
        body, name=f"ffn_fwd_{layer}", grid=(s_len // tm,),
        in_specs=[pl.BlockSpec((tm, d), row), pl.BlockSpec((1, d), const2), ANY, ANY, ANY,
                  pl.BlockSpec((8, f), const2), pl.BlockSpec((1, f), const2)],
        out_specs=[pl.BlockSpec((tm, d), row), pl.BlockSpec((tm, f), row), pl.BlockSpec((tm, f), row),
                   pl.BlockSpec((tm, d), row), pl.BlockSpec((tm, f), row)],
        out_shape=[jax.ShapeDtypeStruct((s_len, d), F32), jax.ShapeDtypeStruct((s_len, f), F32),
                   jax.ShapeDtypeStruct((s_len, f), F32), jax.ShapeDtypeStruct((s_len, d), BF16),
                   jax.ShapeDtypeStruct((s_len, f), BF16)],
        scratch_shapes=[pltpu.VMEM((n_p, d, fq), BF16), pltpu.VMEM((n_p, d, fq), BF16), pltpu.VMEM((n_p, fq, d), BF16),
                        pltpu.VMEM((8, f), F32), pltpu.SemaphoreType.DMA((3,))],
        compiler_params=_params(1),
    )(h, g_norm, wg_all, wu_all, wd_all, conv_w, conv_b)


def _ffn_bwd(h, dh, a, up, g_norm, wg_all, wu_all, wd_all, layer, conv_w, conv_b, tm):
    s_len, d = h.shape
    n_p = wg_all.shape[0]
    fq = wg_all.shape[3]
    f = n_p * fq
    n_blk = s_len // tm
    t8 = tm // 8

    def body(h_ref, dh_ref, a_ref, ahalo_ref, up_ref, g_ref, wg_hbm, wu_hbm, wd_hbm, cw_ref, cb_ref,
             dhin_ref, da_ref, dup_ref, gcw_ref, gcb_ref, gn_ref, wg_v, wu_v, wd_v, carry, sem):
        i = pl.program_id(0)
        _load_once([(wg_hbm.at[:, layer], wg_v), (wu_hbm.at[:, layer], wu_v), (wd_hbm.at[:, layer], wd_v)], sem)

        @pl.when(i == 0)
        def _():
            carry[...] = jnp.zeros_like(carry)
            gcw_ref[...] = jnp.zeros_like(gcw_ref)
            gcb_ref[...] = jnp.zeros_like(gcb_ref)
            gn_ref[...] = jnp.zeros_like(gn_ref)

        x = h_ref[...]
        dh_v = dh_ref[...]
        g = g_ref[...]
        xhat, r, _ = _rms_fwd(x, g)
        a = a_ref[...]
        up_v = up_ref[...]
        prev = jnp.where(i == n_blk - 1, 0.0, ahalo_ref[...])
        am1 = _shift_down(a, 1, prev)
        am2 = _shift_down(a, 2, prev)
        cw = cw_ref[...]
        ac = cb_ref[...] + am2 * cw[0:1, :]
        ac = ac + am1 * cw[1:2, :]
        ac = ac + a * cw[2:3, :]
        sg = _sigmoid(ac)
        sil = ac * sg
        dhb = dh_v.astype(BF16)
        dhid = jnp.concatenate([_dot_nt(dhb, wd_v[p]) for p in range(n_p)], axis=1)
        dup = dhid * sil
        dac = dhid * up_v * (sg * (1.0 + ac * (1.0 - sg)))
        gcb_ref[...] += jnp.sum(dac, axis=0, keepdims=True)
        gcw_ref[0:1, :] += jnp.sum(dac * am2, axis=0, keepdims=True)
        gcw_ref[1:2, :] += jnp.sum(dac * am1, axis=0, keepdims=True)
        gcw_ref[2:3, :] += jnp.sum(dac * a, axis=0, keepdims=True)
        nxt = carry[...]
        dp1 = _shift_up(dac, 1, nxt)
        dp2 = _shift_up(dac, 2, nxt)
        carry[...] = dac[0:8, :]
        da = dac * cw[2:3, :] + dp1 * cw[1:2, :] + dp2 * cw[0:1, :]
        dab = da.astype(BF16)
        dupb = dup.astype(BF16)
        da_ref[...] = dab
        dup_ref[...] = dupb
        dhn = _dot_nt(dab[:, :fq], wg_v[0]) + _dot_nt(dupb[:, :fq], wu_v[0])
        for p in range(1, n_p):
            dhn += _dot_nt(dab[:, p * fq:(p + 1) * fq], wg_v[p]) + _dot_nt(dupb[:, p * fq:(p + 1) * fq], wu_v[p])
        dx, gg = _rms_bwd(dhn, xhat, r, g)
        gn_ref[...] += jnp.sum(gg, axis=0, keepdims=True)
        dhin_ref[...] = dh_v + dx

    rev = lambda i: (n_blk - 1 - i, 0)
    halo = lambda i: (jnp.maximum((n_blk - 1 - i) * t8 - 1, 0), 0)
    const2 = lambda i: (0, 0)
    return pl.pallas_call(
        body, name=f"ffn_bwd_{layer}", grid=(n_blk,),
        in_specs=[pl.BlockSpec((tm, d), rev), pl.BlockSpec((tm, d), rev), pl.BlockSpec((tm, f), rev),
                  pl.BlockSpec((8, f), halo), pl.BlockSpec((tm, f), rev), pl.BlockSpec((1, d), const2), ANY, ANY, ANY,
                  pl.BlockSpec((8, f), const2), pl.BlockSpec((1, f), const2)],
        out_specs=[pl.BlockSpec((tm, d), rev), pl.BlockSpec((tm, f), rev), pl.BlockSpec((tm, f), rev),
                   pl.BlockSpec((8, f), const2), pl.BlockSpec((1, f), const2), pl.BlockSpec((1, d), const2)],
        out_shape=[jax.ShapeDtypeStruct((s_len, d), F32), jax.ShapeDtypeStruct((s_len, f), BF16),
                   jax.ShapeDtypeStruct((s_len, f), BF16), jax.ShapeDtypeStruct((8, f), F32),
                   jax.ShapeDtypeStruct((1, f), F32), jax.ShapeDtypeStruct((1, d), F32)],
        scratch_shapes=[pltpu.VMEM((n_p, d, fq), BF16), pltpu.VMEM((n_p, d, fq), BF16), pltpu.VMEM((n_p, fq, d), BF16),
                        pltpu.VMEM((8, f), F32), pltpu.SemaphoreType.DMA((3,))],
        compiler_params=_params(1),
    )(h, dh, a, a, up, g_norm, wg_all, wu_all, wd_all, conv_w, conv_b)


def _even_head_lanes(shape, axis):
    return (lax.broadcasted_iota(jnp.int32, shape, axis) & HEAD_DIM) == 0


def _pair_select(lo, hi, shape):
    return jnp.where(lax.broadcasted_iota(jnp.int32, shape, 1) < HEAD_DIM, lo, hi)


def _causal(row0, col0, shape):
    return row0 + lax.broadcasted_iota(jnp.int32, shape, 0) >= col0 + lax.broadcasted_iota(jnp.int32, shape, 1)


N_SPARE = 3


def _spare_selectors(d, key_side):
    lane = jnp.arange(d)[None, :]
    row = jnp.arange(N_SPARE * LANES)[:, None]
    head, part = row % LANES, row // LANES
    off = N_SPARE if key_side else 0
    sel_a = ((head % 2 == 0) & (lane == LANES * (head // 2) + HEAD_DIM + off + part)).astype(F32)
    sel_b = ((head % 2 == 1) & (lane == LANES * (head // 2) + off + part)).astype(F32)
    sign = -1.0 if key_side else 1.0
    ones_off = 0 if key_side else N_SPARE
    in_pair = jnp.arange(d)[None, :] % LANES
    ones_a = ((in_pair >= HEAD_DIM + ones_off) & (in_pair < HEAD_DIM + ones_off + N_SPARE)).astype(F32)
    ones_b = ((in_pair >= ones_off) & (in_pair < ones_off + N_SPARE)).astype(F32)
    return (sign * sel_a).astype(BF16), (sign * sel_b).astype(BF16), ones_a, ones_b


def _parts(x):
    return jnp.concatenate(_split3(x), axis=1)


def _fox_proj_fwd(h, g_norm, wq, wk, wv, wf, bf, sel_q, sel_k, tm):
    s_len, d = h.shape
    sq_a, sq_b, oq_a, oq_b = sel_q
    sk_a, sk_b, ok_a, ok_b = sel_k

    def body(h_ref, g_ref, wq_hbm, wk_hbm, wv_hbm, wf_ref, bf_ref, sqa_ref, sqb_ref, oqa_ref, oqb_ref,
             ska_ref, skb_ref, oka_ref, okb_ref,
             hn_ref, qa_ref, qb_ref, ka_ref, kb_ref, kat_ref, kbt_ref, va_ref, vb_ref, vat_ref, vbt_ref, z_ref,
             wq_v, wk_v, wv_v, total, sem):
        i = pl.program_id(0)
        _load_once([(wq_hbm, wq_v), (wk_hbm, wk_v), (wv_hbm, wv_v)], sem)

        @pl.when(i == 0)
        def _():
            total[...] = jnp.zeros_like(total)

        x = h_ref[...]
        _, _, y = _rms_fwd(x, g_ref[...])
        hn = y.astype(BF16)
        hn_ref[...] = hn
        z = _dot(hn, wf_ref[...]) + bf_ref[...]
        z_ref[...] = z
        logf = jnp.minimum(z, 0.0) - jnp.log(1.0 + jnp.exp(-jnp.abs(z)))
        tri = (lax.broadcasted_iota(jnp.int32, (tm, tm), 0) >= lax.broadcasted_iota(jnp.int32, (tm, tm), 1))
        cum = _dot3_rhs(jnp.where(tri, 1.0, 0.0).astype(BF16), logf) + total[0:1, :]
        total[...] = jnp.broadcast_to(cum[tm - 1:tm, :], total.shape)
        parts = _parts(cum)

        even = _even_head_lanes((tm, d), 1)
        q = _dot(hn, wq_v[...]) * (HEAD_DIM ** -0.5)
        qa_ref[...] = jnp.where(even, q, _dot(parts, sqa_ref[...]) + oqa_ref[...]).astype(BF16)
        qb_ref[...] = jnp.where(even, _dot(parts, sqb_ref[...]) + oqb_ref[...], q).astype(BF16)
        k = _dot(hn, wk_v[...])
        ka = jnp.where(even, k, _dot(parts, ska_ref[...]) + oka_ref[...])
        kb = jnp.where(even, _dot(parts, skb_ref[...]) + okb_ref[...], k)
        ka_ref[...] = ka.astype(BF16)
        kb_ref[...] = kb.astype(BF16)
        kat_ref[...] = ka.T.astype(BF16)
        kbt_ref[...] = kb.T.astype(BF16)
        v = _dot(hn, wv_v[...])
        va = jnp.where(even, v, oka_ref[...])
        vb = jnp.where(even, okb_ref[...], v)
        va_ref[...] = va.astype(BF16)
        vb_ref[...] = vb.astype(BF16)
        vat_ref[...] = va.T.astype(BF16)
        vbt_ref[...] = vb.T.astype(BF16)

    row = lambda i: (i, 0)
    col = lambda i: (0, i)
    const2 = lambda i: (0, 0)
    sd = jax.ShapeDtypeStruct((s_len, d), BF16)
    ds_ = jax.ShapeDtypeStruct((d, s_len), BF16)
    rs, cs = pl.BlockSpec((tm, d), row), pl.BlockSpec((d, tm), col)
    sel = pl.BlockSpec((N_SPARE * LANES, d), const2)
    one = pl.BlockSpec((1, d), const2)
    return pl.pallas_call(
        body, name="fox_proj_fwd", grid=(s_len // tm,),
        in_specs=[rs, one, ANY, ANY, ANY, pl.BlockSpec((d, LANES), const2), pl.BlockSpec((1, LANES), const2),
                  sel, sel, one, one, sel, sel, one, one],
        out_specs=[rs, rs, rs, rs, rs, cs, cs, rs, rs, cs, cs, pl.BlockSpec((tm, LANES), row)],
        out_shape=[sd, sd, sd, sd, sd, ds_, ds_, sd, sd, ds_, ds_, jax.ShapeDtypeStruct((s_len, LANES), F32)],
        scratch_shapes=[pltpu.VMEM((d, d), BF16), pltpu.VMEM((d, d), BF16), pltpu.VMEM((d, d), BF16),
                        pltpu.VMEM((8, LANES), F32), pltpu.SemaphoreType.DMA((3,))],
        compiler_params=_params(1),
    )(h, g_norm, wq, wk, wv, wf, bf, sq_a, sq_b, oq_a, oq_b, sk_a, sk_b, ok_a, ok_b)


def _spare_cols(x, base):
    xf = x[:, base:base + N_SPARE].astype(F32)
    return xf[:, 0:1] + xf[:, 1:2] + xf[:, 2:3]


def _with_query_term(x, term, base):
    lane = lax.broadcasted_iota(jnp.int32, x.shape, 1)
    hi, mid, lo = _split3(term)
    out = jnp.where(lane == base, hi.astype(F32), x)
    out = jnp.where(lane == base + 1, mid.astype(F32), out)
    out = jnp.where(lane == base + 2, lo.astype(F32), out)
    return jnp.where((lane >= base + N_SPARE) & (lane < base + 2 * N_SPARE), 1.0, out)


def _flash_fwd(qa, qb, kat, kbt, va, vb):
    s_len, d = qa.shape
    t = ATT_BLOCK
    w = min(ATT_CHUNK, s_len)
    n_pair = d // LANES
    n_q = s_len // t
    bases = (HEAD_DIM, 0)

    def body(qa_ref, qb_ref, kat_ref, kbt_ref, va_ref, vb_ref, o_ref, qa2_ref, qb2_ref, qat2_ref, qbt2_ref):
        i = pl.program_id(1)
        qs = (qa_ref[...], qb_ref[...])
        kts = (kat_ref, kbt_ref)
        vs = (va_ref, vb_ref)

        def step(kb, carry, masked):
            off = pl.multiple_of(kb * w, w)
            out = []
            for hh in range(2):
                m, acc = carry[hh]
                s = _dot(qs[hh], kts[hh][:, pl.ds(off, w)])
                if masked:
                    s = jnp.where(_causal(i * t, off, (t, w)), s, NEG_BIG)
                m_new = jnp.maximum(m, jnp.max(s, axis=1, keepdims=True))
                p = jnp.exp(s - m_new)
                out.append((m_new, acc * jnp.exp(m - m_new) + _dot(p.astype(BF16), vs[hh][pl.ds(off, w), :])))
            return tuple(out)

        init = ((jnp.full((t, 1), NEG_BIG, F32), jnp.zeros((t, LANES), F32)),) * 2
        diag = (i * t) // w
        carry = lax.fori_loop(0, diag, lambda kb, c: step(kb, c, False), init)
        carry = step(diag, carry, True)
        outs, q2 = [], []
        for hh in range(2):
            m, acc = carry[hh]
            l = acc[:, bases[hh]:bases[hh] + 1]
            outs.append(acc / l)
            term = _spare_cols(qs[hh], bases[hh]) - (m + jnp.log(l))
            q2.append(_with_query_term(qs[hh].astype(F32), term, bases[hh]))
        o_ref[...] = _pair_select(outs[0], outs[1], (t, LANES))
        qa2_ref[...] = q2[0].astype(BF16)
        qb2_ref[...] = q2[1].astype(BF16)
        qat2_ref[...] = q2[0].T.astype(BF16)
        qbt2_ref[...] = q2[1].T.astype(BF16)

    qblk = pl.BlockSpec((t, LANES), lambda j, i: (i, j))
    qblk_t = pl.BlockSpec((LANES, t), lambda j, i: (j, i))
    whole_t = pl.BlockSpec((LANES, s_len), lambda j, i: (j, 0))
    whole = pl.BlockSpec((s_len, LANES), lambda j, i: (0, j))
    sd = jax.ShapeDtypeStruct((s_len, d), BF16)
    ds_ = jax.ShapeDtypeStruct((d, s_len), BF16)
    return pl.pallas_call(
        body, name="flash_fwd", grid=(n_pair, n_q),
        in_specs=[qblk, qblk, whole_t, whole_t, whole, whole],
        out_specs=[qblk, qblk, qblk, qblk_t, qblk_t],
        out_shape=[jax.ShapeDtypeStruct((s_len, d), F32), sd, sd, ds_, ds_],
        compiler_params=_params(2),
    )(qa, qb, kat, kbt, va, vb)


def _flash_bwd(qa, qb, qat, qbt, kat, kbt, ka, kb_, vat, vbt, doa, dob, doat, dobt):
    s_len, d = qa.shape
    t = ATT_BLOCK
    w = min(ATT_CHUNK, s_len)
    n_pair = d // LANES
    n_q = s_len // t

    def body(qa_ref, qb_ref, qat_ref, qbt_ref, kat_hbm, kbt_hbm, ka_hbm, kb_hbm, vat_hbm, vbt_hbm,
             doa_ref, dob_ref, doat_ref, dobt_ref,
             dq_ref, dkt_ref, dvt_ref, rs_ref, cs_ref,
             kat_v, kbt_v, ka_v, kb_v, vat_v, vbt_v, dkt_acc, dvt_acc, cs_acc, sem):
        j = pl.program_id(0)
        i = pl.program_id(1)

        @pl.when(i == 0)
        def _():
            rows = pl.ds(pl.multiple_of(j * LANES, LANES), LANES)
            copies = [pltpu.make_async_copy(src, dst, sem.at[n]) for n, (src, dst) in enumerate([
                (kat_hbm.at[rows, :], kat_v), (kbt_hbm.at[rows, :], kbt_v), (ka_hbm.at[:, rows], ka_v),
                (kb_hbm.at[:, rows], kb_v), (vat_hbm.at[rows, :], vat_v), (vbt_hbm.at[rows, :], vbt_v)])]
            for cp in copies:
                cp.start()
            dkt_acc[...] = jnp.zeros_like(dkt_acc)
            dvt_acc[...] = jnp.zeros_like(dvt_acc)
            cs_acc[...] = jnp.zeros_like(cs_acc)
            for cp in copies:
                cp.wait()

        qs = (qa_ref[...], qb_ref[...])
        dos = (doa_ref[...], dob_ref[...])
        first = lax.broadcasted_iota(jnp.int32, (LANES, t), 0) < HEAD_DIM
        zero = jnp.zeros((LANES, t), BF16)
        qts = (jnp.where(first, qat_ref[...], zero), jnp.where(first, zero, qbt_ref[...]))
        dots = (jnp.where(first, doat_ref[...], zero), jnp.where(first, zero, dobt_ref[...]))
        ones = jnp.ones((8, t), BF16)
        kts, ks, vts = (kat_v, kbt_v), (ka_v, kb_v), (vat_v, vbt_v)

        def step(kb, carry, masked):
            off = pl.multiple_of(kb * w, w)
            cols = pl.ds(off, w)
            out = []
            for hh in range(2):
                s = _dot(qs[hh], kts[hh][:, cols])
                if masked:
                    s = jnp.where(_causal(i * t, off, (t, w)), s, NEG_BIG)
                p = jnp.exp(s)
                ds = (p * _dot(dos[hh], vts[hh][:, cols])).astype(BF16)
                out.append(carry[hh] + _dot(ds, ks[hh][cols, :]))
                dvt_acc[:, cols] += _dot(dots[hh], p.astype(BF16))
                dkt_acc[:, cols] += _dot(qts[hh], ds)
                cs_acc[8 * hh:8 * hh + 8, cols] += _dot(ones, ds)
            return tuple(out)

        diag = (i * t) // w
        carry = lax.fori_loop(0, diag, lambda kb, c: step(kb, c, False), (jnp.zeros((t, LANES), F32),) * 2)
        acc = step(diag, carry, True)
        dq_ref[...] = (_pair_select(acc[0], acc[1], (t, LANES)) * (HEAD_DIM ** -0.5)).astype(BF16)
        rs_ref[...] = _pair_select(acc[0][:, HEAD_DIM:HEAD_DIM + 1], acc[1][:, 0:1], (t, LANES))

        @pl.when(i == n_q - 1)
        def _():
            dkt_ref[...] = dkt_acc[...].astype(BF16)
            dvt_ref[...] = dvt_acc[...].astype(BF16)
            cs_ref[0] = cs_acc[...]

    qblk = pl.BlockSpec((t, LANES), lambda j, i: (i, j))
    qblk_t = pl.BlockSpec((LANES, t), lambda j, i: (j, i))
    whole_t = pl.BlockSpec((LANES, s_len), lambda j, i: (j, 0))
    ds_ = jax.ShapeDtypeStruct((d, s_len), BF16)
    return pl.pallas_call(
        body, name="flash_bwd", grid=(n_pair, n_q),
        in_specs=[qblk, qblk, qblk_t, qblk_t, ANY, ANY, ANY, ANY, ANY, ANY, qblk, qblk, qblk_t, qblk_t],
        out_specs=[qblk, whole_t, whole_t, qblk, pl.BlockSpec((1, 16, s_len), lambda j, i: (j, 0, 0))],
        out_shape=[jax.ShapeDtypeStruct((s_len, d), BF16), ds_, ds_, jax.ShapeDtypeStruct((s_len, d), F32),
                   jax.ShapeDtypeStruct((n_pair, 16, s_len), F32)],
        scratch_shapes=[pltpu.VMEM((LANES, s_len), BF16), pltpu.VMEM((LANES, s_len), BF16),
                        pltpu.VMEM((s_len, LANES), BF16), pltpu.VMEM((s_len, LANES), BF16),
                        pltpu.VMEM((LANES, s_len), BF16), pltpu.VMEM((LANES, s_len), BF16),
                        pltpu.VMEM((LANES, s_len), F32), pltpu.VMEM((LANES, s_len), F32),
                        pltpu.VMEM((16, s_len), F32), pltpu.SemaphoreType.DMA((6,))],
        compiler_params=_params(2),
    )(qa, qb, qat, qbt, kat, kbt, ka, kb_, vat, vbt, doa, dob, doat, dobt)


def _wgrad_t(at, b, name):
    k, s_len = at.shape
    n = b.shape[1]
    tn, tk, ts = min(n, 1024), min(k, 1024), 512

    def body(a_ref, b_ref, o_ref):
        @pl.when(pl.program_id(2) == 0)
        def _():
            o_ref[...] = jnp.zeros_like(o_ref)
        o_ref[...] += _dot(a_ref[...].astype(BF16), b_ref[...].astype(BF16))

    return pl.pallas_call(
        body, name=name, grid=(k // tk, n // tn, s_len // ts),
        in_specs=[pl.BlockSpec((tk, ts), lambda a, b_, c: (a, c)), pl.BlockSpec((ts, tn), lambda a, b_, c: (c, b_))],
        out_specs=pl.BlockSpec((tk, tn), lambda a, b_, c: (a, b_)),
        out_shape=jax.ShapeDtypeStruct((k, n), F32),
        compiler_params=_params(3),
    )(at, b)


def _oproj_bwd(dh, o, wo, seg, sel_q, tm):
    s_len, d = dh.shape
    sq_a, sq_b, _, _ = sel_q

    def body(dh_ref, o_ref, wo_hbm, seg_ref, sqa_ref, sqb_ref, doa_ref, dob_ref, doat_ref, dobt_ref, wo_v, sem):
        _load_once([(wo_hbm, wo_v)], sem)
        do = _dot_nt(dh_ref[...].astype(BF16), wo_v[...])
        parts = _parts(-_dot3_lhs(do * o_ref[...], seg_ref[...]))
        even = _even_head_lanes((tm, d), 1)
        doa = jnp.where(even, do, _dot(parts, sqa_ref[...]))
        dob = jnp.where(even, _dot(parts, sqb_ref[...]), do)
        doa_ref[...] = doa.astype(BF16)
        dob_ref[...] = dob.astype(BF16)
        doat_ref[...] = doa.T.astype(BF16)
        dobt_ref[...] = dob.T.astype(BF16)

    row = lambda i: (i, 0)
    const2 = lambda i: (0, 0)
    rs, cs = pl.BlockSpec((tm, d), row), pl.BlockSpec((d, tm), lambda i: (0, i))
    sel = pl.BlockSpec((N_SPARE * LANES, d), const2)
    sd = jax.ShapeDtypeStruct((s_len, d), BF16)
    ds_ = jax.ShapeDtypeStruct((d, s_len), BF16)
    return pl.pallas_call(
        body, name="oproj_bwd", grid=(s_len // tm,),
        in_specs=[rs, rs, ANY, pl.BlockSpec((d, LANES), const2), sel, sel],
        out_specs=[rs, rs, cs, cs], out_shape=[sd, sd, ds_, ds_],
        scratch_shapes=[pltpu.VMEM((d, d), BF16), pltpu.SemaphoreType.DMA((1,))],
        compiler_params=_params(1),
    )(dh, o, wo, seg, sq_a, sq_b)


def _oproj_fwd(h, o, wo, tm):
    s_len, d = h.shape

    def body(h_ref, o_ref, wo_hbm, hout_ref, wo_v, sem):
        _load_once([(wo_hbm, wo_v)], sem)
        hout_ref[...] = h_ref[...] + _dot(o_ref[...].astype(BF16), wo_v[...])

    row = lambda i: (i, 0)
    return pl.pallas_call(
        body, name="oproj_fwd", grid=(s_len // tm,),
        in_specs=[pl.BlockSpec((tm, d), row), pl.BlockSpec((tm, d), row), ANY],
        out_specs=pl.BlockSpec((tm, d), row),
        out_shape=jax.ShapeDtypeStruct((s_len, d), F32),
        scratch_shapes=[pltpu.VMEM((d, d), BF16), pltpu.SemaphoreType.DMA((1,))],
        compiler_params=_params(1),
    )(h, o, wo)


def _forget_bwd(dcum, z, tm):
    s_len = dcum.shape[0]
    n_blk = s_len // tm

    def body(dc_ref, z_ref, dfl_ref, gb_ref, total):
        i = pl.program_id(0)

        @pl.when(i == 0)
        def _():
            total[...] = jnp.zeros_like(total)
            gb_ref[...] = jnp.zeros_like(gb_ref)

        upper = (lax.broadcasted_iota(jnp.int32, (tm, tm), 0) <= lax.broadcasted_iota(jnp.int32, (tm, tm), 1))
        suffix = _dot3_rhs(jnp.where(upper, 1.0, 0.0).astype(BF16), dc_ref[...]) + total[0:1, :]
        total[...] = jnp.broadcast_to(suffix[0:1, :], total.shape)
        dfl = suffix * _sigmoid(-z_ref[...])
        dfl_ref[...] = dfl
        gb_ref[...] += jnp.sum(dfl, axis=0, keepdims=True)

    rev = lambda i: (n_blk - 1 - i, 0)
    return pl.pallas_call(
        body, name="forget_bwd", grid=(n_blk,),
        in_specs=[pl.BlockSpec((tm, LANES), rev), pl.BlockSpec((tm, LANES), rev)],
        out_specs=[pl.BlockSpec((tm, LANES), rev), pl.BlockSpec((1, LANES), lambda i: (0, 0))],
        out_shape=[jax.ShapeDtypeStruct((s_len, LANES), F32), jax.ShapeDtypeStruct((1, LANES), F32)],
        scratch_shapes=[pltpu.VMEM((8, LANES), F32)],
        compiler_params=_params(1),
    )(dcum, z)


def _fox_proj_bwd(h, dh, dq, dkt, dvt, dfl, g_norm, wq, wk, wv, wf, tm):
    s_len, d = h.shape

    def body(h_ref, dh_ref, dq_ref, dkt_ref, dvt_ref, dfl_ref, g_ref, wq_hbm, wk_hbm, wv_hbm, wf_ref,
             dhin_ref, dflb_ref, gn_ref, wq_v, wk_v, wv_v, sem):
        _load_once([(wq_hbm, wq_v), (wk_hbm, wk_v), (wv_hbm, wv_v)], sem)

        @pl.when(pl.program_id(0) == 0)
        def _():
            gn_ref[...] = jnp.zeros_like(gn_ref)

        g = g_ref[...]
        xhat, r, _ = _rms_fwd(h_ref[...], g)
        dflb = dfl_ref[...].astype(BF16)
        dflb_ref[...] = dflb
        from_kv = _dot(wk_v[...], dkt_ref[...]) + _dot(wv_v[...], dvt_ref[...])
        dhn = _dot_nt(dq_ref[...], wq_v[...]) + _dot_nt(dflb, wf_ref[...]) + from_kv.T
        dx, gg = _rms_bwd(dhn, xhat, r, g)
        gn_ref[...] += jnp.sum(gg, axis=0, keepdims=True)
        dhin_ref[...] = dh_ref[...] + dx

    row = lambda i: (i, 0)
    const2 = lambda i: (0, 0)
    rs = pl.BlockSpec((tm, d), row)
    cs = pl.BlockSpec((d, tm), lambda i: (0, i))
    return pl.pallas_call(
        body, name="fox_proj_bwd", grid=(s_len // tm,),
        in_specs=[rs, rs, rs, cs, cs, pl.BlockSpec((tm, LANES), row), pl.BlockSpec((1, d), const2), ANY, ANY, ANY,
                  pl.BlockSpec((d, LANES), const2)],
        out_specs=[rs, pl.BlockSpec((tm, LANES), row), pl.BlockSpec((1, d), const2)],
        out_shape=[jax.ShapeDtypeStruct((s_len, d), F32), jax.ShapeDtypeStruct((s_len, LANES), BF16),
                   jax.ShapeDtypeStruct((1, d), F32)],
        scratch_shapes=[pltpu.VMEM((d, d), BF16), pltpu.VMEM((d, d), BF16), pltpu.VMEM((d, d), BF16),
                        pltpu.SemaphoreType.DMA((3,))],
        compiler_params=_params(1),
    )(h, dh, dq, dkt, dvt, dfl, g_norm, wq, wk, wv, wf)


def _loss_head(h, target, g_final, tm):
    s_len, d = h.shape
    n_blk = s_len // tm

    def body(h_ref, t_ref, g_ref, dh_ref, loss_ref, gg_ref, sq):
        i = pl.program_id(0)

        @pl.when(i == 0)
        def _():
            sq[...] = jnp.zeros_like(sq)
            gg_ref[...] = jnp.zeros_like(gg_ref)

        g = g_ref[...]
        xhat, r, y = _rms_fwd(h_ref[...], g)
        err = y - t_ref[...]
        sq[...] += jnp.sum(err * err, axis=0, keepdims=True)
        dx, gg = _rms_bwd(err * (1.0 / d), xhat, r, g)
        gg_ref[...] += jnp.sum(gg, axis=0, keepdims=True)
        dh_ref[...] = dx

        @pl.when(i == n_blk - 1)
        def _():
            loss_ref[...] = jnp.broadcast_to(jnp.sum(sq[...], axis=1, keepdims=True) * (0.5 / d), loss_ref.shape)

    row = lambda i: (i, 0)
    const2 = lambda i: (0, 0)
    return pl.pallas_call(
        body, name="loss_head", grid=(n_blk,),
        in_specs=[pl.BlockSpec((tm, d), row), pl.BlockSpec((tm, d), row), pl.BlockSpec((1, d), const2)],
        out_specs=[pl.BlockSpec((tm, d), row), pl.BlockSpec((1, LANES), const2), pl.BlockSpec((1, d), const2)],
        out_shape=[jax.ShapeDtypeStruct((s_len, d), F32), jax.ShapeDtypeStruct((1, LANES), F32),
                   jax.ShapeDtypeStruct((1, d), F32)],
        scratch_shapes=[pltpu.VMEM((1, d), F32)],
        compiler_params=_params(1),
    )(h, target, g_final)


def _wgrad(x, dy, n_piece, name):
    s_len, k = x.shape
    n = dy.shape[1]
    tn = min(n // n_piece, 1024)
    tk = min(k, 1024)
    ts = 512
    per_piece = (n // n_piece) // tn

    def body(x_ref, dy_ref, o_ref):
        @pl.when(pl.program_id(2) == 0)
        def _():
            o_ref[...] = jnp.zeros_like(o_ref)
        o_ref[0] += _dot_tn(x_ref[...].astype(BF16), dy_ref[...].astype(BF16))

    return pl.pallas_call(
        body, name=name, grid=(k // tk, n // tn, s_len // ts),
        in_specs=[pl.BlockSpec((ts, tk), lambda a, b, c: (c, a)), pl.BlockSpec((ts, tn), lambda a, b, c: (c, b))],
        out_specs=pl.BlockSpec((1, tk, tn), lambda a, b, c: (b // per_piece, a, b % per_piece)),
        out_shape=jax.ShapeDtypeStruct((n_piece, k, n // n_piece), F32),
        compiler_params=_params(3),
    )(x, dy)


def _pair_sum(g, recv, core, name):
    n_piece, rows, c = g.shape
    half = rows // 2
    tr = min(half, 512)
    nb = half // tr

    def body(core_ref, g_ref, r_ref, o_ref):
        o_ref[...] = g_ref[...] + r_ref[...]

    blk = pl.BlockSpec((1, tr, c), lambda p, i, core_ref: (p, i, 0))
    return pl.pallas_call(
        body, name=name, out_shape=jax.ShapeDtypeStruct((n_piece, half, c), F32),
        grid_spec=pltpu.PrefetchScalarGridSpec(
            num_scalar_prefetch=1, grid=(n_piece, nb),
            in_specs=[pl.BlockSpec((1, tr, c), lambda p, i, core_ref: (p, core_ref[0] * nb + i, 0)), blk],
            out_specs=blk),
        compiler_params=_params(2),
    )(core, g, recv)


def _chip_sum(halves, recv, chip, name):
    _, h, c = halves.shape
    tr = min(h, 512)

    def body(chip_ref, own_ref, r_ref, o_ref):
        o_ref[...] = ((own_ref[0] + r_ref[0]) + r_ref[1]) + r_ref[2]

    return pl.pallas_call(
        body, name=name, out_shape=jax.ShapeDtypeStruct((h, c), F32),
        grid_spec=pltpu.PrefetchScalarGridSpec(
            num_scalar_prefetch=1, grid=(h // tr,),
            in_specs=[pl.BlockSpec((1, tr, c), lambda i, chip_ref: (chip_ref[0], i, 0)),
                      pl.BlockSpec((3, tr, c), lambda i, chip_ref: (0, i, 0))],
            out_specs=pl.BlockSpec((tr, c), lambda i, chip_ref: (i, 0))),
        compiler_params=_params(1),
    )(chip, halves, recv)


def _adamw_math(w, m, v, g):
    m_new = ADAM_B1 * m + (1.0 - ADAM_B1) * g
    v_new = ADAM_B2 * v + (1.0 - ADAM_B2) * (g * g)
    m_hat = m_new / (1.0 - ADAM_B1 ** ADAM_STEP)
    v_hat = v_new / (1.0 - ADAM_B2 ** ADAM_STEP)
    return -ADAM_LR * (m_hat / (jnp.sqrt(v_hat) + ADAM_EPS) + ADAM_WD * w), m_new, v_new


def _adamw(w, m, v, g, name):
    rows, c = w.shape
    tr = min(rows, 256)

    def body(w_ref, m_ref, v_ref, g_ref, d_ref, mo_ref, vo_ref):
        d_ref[...], mo_ref[...], vo_ref[...] = _adamw_math(w_ref[...], m_ref[...], v_ref[...], g_ref[...])

    spec = pl.BlockSpec((tr, c), lambda i: (i, 0))
    shape = jax.ShapeDtypeStruct((rows, c), F32)
    return pl.pallas_call(
        body, name=name, grid=(rows // tr,),
        in_specs=[spec] * 4, out_specs=[spec] * 3, out_shape=[shape] * 3,
        compiler_params=_params(1),
    )(w, m, v, g)


def _adamw_halves(w, m, v, g_own, g_other, core, name):
    rows, c = w.shape
    half = rows // 2
    tr = min(half, 256)
    nb = half // tr

    def body(core_ref, w_ref, m_ref, v_ref, own_ref, other_ref, g_ref, d_ref, mo_ref, vo_ref):
        mine = (pl.program_id(0) // nb) == core_ref[0]
        g = jnp.where(mine, own_ref[...], other_ref[...])
        g_ref[...] = g
        d_ref[...], mo_ref[...], vo_ref[...] = _adamw_math(w_ref[...], m_ref[...], v_ref[...], g)

    spec = pl.BlockSpec((tr, c), lambda i, core_ref: (i, 0))
    own = pl.BlockSpec((tr, c), lambda i, core_ref: (jnp.clip(i - core_ref[0] * nb, 0, nb - 1), 0))
    other = pl.BlockSpec((tr, c), lambda i, core_ref: (jnp.clip(i - (1 - core_ref[0]) * nb, 0, nb - 1), 0))
    shape = jax.ShapeDtypeStruct((rows, c), F32)
    return pl.pallas_call(
        body, name=name, out_shape=[shape] * 4,
        grid_spec=pltpu.PrefetchScalarGridSpec(
            num_scalar_prefetch=1, grid=(rows // tr,),
            in_specs=[spec, spec, spec, own, other], out_specs=[spec] * 4),
        compiler_params=_params(1),
    )(core, w, m, v, g_own, g_other)


def _place():
    x, y, c = lax.axis_index("x"), lax.axis_index("y"), lax.axis_index("c")
    chips = [(1 - x, y), (x, 1 - y), (1 - x, 1 - y)]
    return x, y, c, chips


def _all_gather_chips(shards):
    n = len(shards)

    def body(*refs):
        ins, outs = refs[:n], refs[n:2 * n]
        send_sems, recv_sems, local_sems = refs[2 * n:]
        x, y, c, chips = _place()
        mine = 2 * x + y
        local = [pltpu.make_async_copy(ins[k], outs[k].at[mine], local_sems.at[k]) for k in range(n)]
        for cp in local:
            cp.start()
        sends = []
        for k in range(n):
            for j, (tx, ty) in enumerate(chips):
                sends.append(pltpu.make_async_remote_copy(
                    src_ref=ins[k], dst_ref=outs[k].at[mine], send_sem=send_sems.at[k, j], recv_sem=recv_sems.at[k, j],
                    device_id=(tx, ty, c), device_id_type=MESH))
        for cp in sends:
            cp.start()
        for k in range(n):
            for j, (tx, ty) in enumerate(chips):
                pltpu.make_async_remote_copy(
                    src_ref=ins[k], dst_ref=outs[k].at[2 * tx + ty], send_sem=send_sems.at[k, j],
                    recv_sem=recv_sems.at[k, j], device_id=(tx, ty, c), device_id_type=MESH).wait()
        for cp in local:
            cp.wait()

    return pl.pallas_call(
        body, name="weights_all_gather",
        in_specs=[ANY] * n, out_specs=[ANY] * n,
        out_shape=[jax.ShapeDtypeStruct((4,) + s.shape, s.dtype) for s in shards],
        scratch_shapes=[pltpu.SemaphoreType.DMA((n, 3)), pltpu.SemaphoreType.DMA((n, 3)), pltpu.SemaphoreType.DMA((n,))],
    )(*shards)


def _pair_exchange(grads):
    n = len(grads)

    def body(*refs):
        ins, outs = refs[:n], refs[n:2 * n]
        send_sems, recv_sems = refs[2 * n:]
        x, y, c, _ = _place()
        copies = []
        for k in range(n):
            half = grads[k].shape[1] // 2
            other = ins[k].at[:, pl.ds(pl.multiple_of((1 - c) * half, 8), half), :]
            copies.append(pltpu.make_async_remote_copy(
                src_ref=other, dst_ref=outs[k], send_sem=send_sems.at[k], recv_sem=recv_sems.at[k],
                device_id=(x, y, 1 - c), device_id_type=MESH))
        for cp in copies:
            cp.start()
        for cp in copies:
            cp.wait()

    return pl.pallas_call(
        body, name="grads_pair_exchange",
        in_specs=[ANY] * n, out_specs=[ANY] * n,
        out_shape=[jax.ShapeDtypeStruct((4, g.shape[1] // 2, g.shape[2]), F32) for g in grads],
        scratch_shapes=[pltpu.SemaphoreType.DMA((n,)), pltpu.SemaphoreType.DMA((n,))],
    )(*grads)


def _chip_scatter(halves):
    n = len(halves)

    def body(*refs):
        ins, outs = refs[:n], refs[n:2 * n]
        send_sems, recv_sems = refs[2 * n:]
        x, y, c, chips = _place()
        sends = []
        for k in range(n):
            for j, (tx, ty) in enumerate(chips):
                sends.append(pltpu.make_async_remote_copy(
                    src_ref=ins[k].at[2 * tx + ty], dst_ref=outs[k].at[j], send_sem=send_sems.at[k, j],
                    recv_sem=recv_sems.at[k, j], device_id=(tx, ty, c), device_id_type=MESH))
        for cp in sends:
            cp.start()
        for cp in sends:
            cp.wait()

    return pl.pallas_call(
        body, name="grads_chip_scatter",
        in_specs=[ANY] * n, out_specs=[ANY] * n,
        out_shape=[jax.ShapeDtypeStruct((3,) + hv.shape[1:], F32) for hv in halves],
        scratch_shapes=[pltpu.SemaphoreType.DMA((n, 3)), pltpu.SemaphoreType.DMA((n, 3))],
    )(*halves)


def _pair_share(finals):
    n = len(finals)

    def body(*refs):
        ins, outs = refs[:n], refs[n:2 * n]
        send_sems, recv_sems = refs[2 * n:]
        x, y, c, _ = _place()
        copies = [pltpu.make_async_remote_copy(
            src_ref=ins[k], dst_ref=outs[k], send_sem=send_sems.at[k], recv_sem=recv_sems.at[k],
            device_id=(x, y, 1 - c), device_id_type=MESH) for k in range(n)]
        for cp in copies:
            cp.start()
        for cp in copies:
            cp.wait()

    return pl.pallas_call(
        body, name="grads_pair_share",
        in_specs=[ANY] * n, out_specs=[ANY] * n,
        out_shape=[jax.ShapeDtypeStruct(fv.shape, F32) for fv in finals],
        scratch_shapes=[pltpu.SemaphoreType.DMA((n,)), pltpu.SemaphoreType.DMA((n,))],
    )(*finals)


def _small_all_reduce(buf):
    rows, c_ = buf.shape

    def body(in_ref, out_ref, pair_buf, slots, send_sems, recv_sems):
        x, y, c, chips = _place()
        mine = 2 * x + y
        pair = pltpu.make_async_remote_copy(
            src_ref=in_ref, dst_ref=pair_buf, send_sem=send_sems.at[0], recv_sem=recv_sems.at[0],
            device_id=(x, y, 1 - c), device_id_type=MESH)
        pair.start()
        pair.wait()
        slots[mine] = in_ref[...] + pair_buf[...]
        sends = [pltpu.make_async_remote_copy(
            src_ref=slots.at[mine], dst_ref=slots.at[mine], send_sem=send_sems.at[1 + j], recv_sem=recv_sems.at[1 + j],
            device_id=(tx, ty, c), device_id_type=MESH) for j, (tx, ty) in enumerate(chips)]
        for cp in sends:
            cp.start()
        for j, (tx, ty) in enumerate(chips):
            pltpu.make_async_remote_copy(
                src_ref=slots.at[mine], dst_ref=slots.at[2 * tx + ty], send_sem=send_sems.at[1 + j],
                recv_sem=recv_sems.at[1 + j], device_id=(tx, ty, c), device_id_type=MESH).wait()
        out_ref[...] = ((slots[0] + slots[1]) + slots[2]) + slots[3]

    vm = pl.BlockSpec(memory_space=pltpu.VMEM)
    return pl.pallas_call(
        body, name="small_all_reduce", in_specs=[vm], out_specs=vm,
        out_shape=jax.ShapeDtypeStruct((rows, c_), F32),
        scratch_shapes=[pltpu.VMEM((rows, c_), F32), pltpu.VMEM((4, rows, c_), F32),
                        pltpu.SemaphoreType.DMA((4,)), pltpu.SemaphoreType.DMA((4,))],
        compiler_params=pltpu.CompilerParams(vmem_limit_bytes=VMEM_LIMIT_V7X),
    )(buf)


def _reduce_scatter(grads):
    core = lax.axis_index("c").astype(jnp.int32).reshape(1)
    chip = (2 * lax.axis_index("x") + lax.axis_index("y")).astype(jnp.int32).reshape(1)
    recv = _pair_exchange(grads)
    halves = [_pair_sum(g, r, core, f"pair_sum_{k}") for k, (g, r) in enumerate(zip(grads, recv))]
    recv = _chip_scatter(halves)
    finals = [_chip_sum(hv, r, chip, f"chip_sum_{k}") for k, (hv, r) in enumerate(zip(halves, recv))]
    return list(zip(finals, _pair_share(finals))), core


PACK_COLS = 1024


def _pack(arrays):
    flat = jnp.concatenate([a.reshape(-1).astype(F32) for a in arrays])
    rows = -(-flat.shape[0] // PACK_COLS)
    rows = -(-rows // 8) * 8
    return jnp.pad(flat, (0, rows * PACK_COLS - flat.shape[0])).reshape(rows, PACK_COLS)


def _unpack(buf, shapes):
    flat = buf.reshape(-1)
    out, at = [], 0
    for shp in shapes:
        size = math.prod(shp)
        out.append(flat[at:at + size].reshape(shp))
        at += size
    return out


def kernel(x, mix_norm_g, ffn_norm_g, gm_w_in, gm_ln_g, gm_ln_b, gm_w_s, gm_b_s, gm_w_out, fox_w_qkvf, fox_b_f, fox_w_o, ffn_w_gate, ffn_w_up, ffn_conv_w, ffn_conv_b, ffn_w_down, final_norm_g, loss_target, m_mix_norm_g, m_ffn_norm_g, m_gm_w_in, m_gm_ln_g, m_gm_ln_b, m_gm_w_s, m_gm_b_s, m_gm_w_out, m_fox_w_qkvf, m_fox_b_f, m_fox_w_o, m_ffn_w_gate, m_ffn_w_up, m_ffn_conv_w, m_ffn_conv_b, m_ffn_w_down, m_final_norm_g, v_mix_norm_g, v_ffn_norm_g, v_gm_w_in, v_gm_ln_g, v_gm_ln_b, v_gm_w_s, v_gm_b_s, v_gm_w_out, v_fox_w_qkvf, v_fox_b_f, v_fox_w_o, v_ffn_w_gate, v_ffn_w_up, v_ffn_conv_w, v_ffn_conv_b, v_ffn_w_down, v_final_norm_g):
    _, s_len, d = x.shape
    e = gm_ln_g.shape[1]
    f = ffn_conv_b.shape[1]
    n_head = fox_b_f.shape[1]
    n_pair = n_head // 2
    gd = e // GM_GROUPS
    qkvf_cols = fox_w_qkvf.shape[2]
    assert d == n_head * HEAD_DIM and d % (2 * LANES) == 0 and s_len % 512 == 0 and gd % LANES == 0
    assert gm_w_s.shape[2] == CHUNK and 4 * qkvf_cols == 3 * d + n_head
    tm = 256
    h0 = x[0]
    target = loss_target[0]

    gathered = _all_gather_chips([
        gm_w_in[0].astype(BF16), gm_w_out[0].astype(BF16), fox_w_qkvf[0].astype(BF16), fox_w_o[0].astype(BF16),
        ffn_w_gate.astype(BF16), ffn_w_up.astype(BF16), ffn_w_down.astype(BF16), ffn_conv_w])
    w_in, w_out4, qkvf4, wo4, wg_all, wu_all, wd_all, cw4 = gathered
    w_out = w_out4.reshape(e, d)
    qkvf = jnp.transpose(qkvf4, (1, 0, 2)).reshape(d, 4 * qkvf_cols)
    wq, wk, wv = qkvf[:, :d], qkvf[:, d:2 * d], qkvf[:, 2 * d:3 * d]
    wf = jnp.pad(qkvf[:, 3 * d:], ((0, 0), (0, LANES - n_head)))
    wo = wo4.reshape(d, d)
    conv_w_full = jnp.transpose(cw4, (1, 2, 0, 3)).reshape(2, 3, f)
    conv_w8 = jnp.pad(conv_w_full, ((0, 0), (0, 5), (0, 0)))
    bf_pad = jnp.pad(fox_b_f, ((0, 0), (0, LANES - n_head)))

    tril = jnp.tril(jnp.ones((CHUNK, CHUNK), bool))
    wc = jnp.where(tril[None], gm_w_s[0], 0.0).astype(BF16)
    wct = jnp.transpose(wc, (0, 2, 1))
    bias = jnp.repeat(gm_b_s[0].T, gd, axis=1)
    seg_groups = (jnp.arange(e)[:, None] // gd == jnp.arange(LANES)[None, :]).astype(BF16)
    seg_heads = (jnp.arange(d)[:, None] // HEAD_DIM == jnp.arange(LANES)[None, :]).astype(BF16)
    sel_q = _spare_selectors(d, key_side=False)
    sel_k = _spare_selectors(d, key_side=True)

    h1, a0, hn0, gated0 = _gmlp_fwd(h0, mix_norm_g[0:1], w_in, gm_ln_g, gm_ln_b, wc, bias, w_out, tm)
    h2, fa0, fup0, fhn0, fhid0 = _ffn_fwd(h1, ffn_norm_g[0:1], wg_all, wu_all, wd_all, 0, conv_w8[0], ffn_conv_b[0:1], tm)
    (hn1, qa, qb, ka, kb_, kat, kbt, va, vb, vat, vbt, z_f) = _fox_proj_fwd(
        h2, mix_norm_g[1:2], wq, wk, wv, wf, bf_pad, sel_q, sel_k, tm)
    o, qa2, qb2, qat2, qbt2 = _flash_fwd(qa, qb, kat, kbt, va, vb)
    h3 = _oproj_fwd(h2, o, wo, tm)
    h4, fa1, fup1, fhn1, fhid1 = _ffn_fwd(h3, ffn_norm_g[1:2], wg_all, wu_all, wd_all, 1, conv_w8[1], ffn_conv_b[1:2], tm)

    dh4, loss_part, g_final = _loss_head(h4, target, final_norm_g.reshape(1, d), tm)
    dh3, da1, dup1, gcw1, gcb1, gfn1 = _ffn_bwd(h3, dh4, fa1, fup1, ffn_norm_g[1:2], wg_all, wu_all, wd_all, 1,
                                                conv_w8[1], ffn_conv_b[1:2], tm)
    g_gate1 = _wgrad(fhn1, da1, 4, "wgrad_gate_1")
    g_up1 = _wgrad(fhn1, dup1, 4, "wgrad_up_1")
    g_down1 = _wgrad(fhid1, dh4, 1, "wgrad_down_1").reshape(4, f // 4, d)

    doa, dob, doat, dobt = _oproj_bwd(dh3, o, wo, seg_heads, sel_q, tm)
    g_wo = _wgrad(o, dh3, 1, "wgrad_wo").reshape(4, d // 4, d)
    dq, dkt, dvt, row_sums, col_sums = _flash_bwd(qa2, qb2, qat2, qbt2, kat, kbt, ka, kb_, vat, vbt,
                                                  doa, dob, doat, dobt)
    col_sums = col_sums[:, ::8, :].reshape(n_head, s_len).T
    dcum = jnp.pad(row_sums[:, ::HEAD_DIM] - col_sums, ((0, 0), (0, LANES - n_head)))
    dfl, g_bf = _forget_bwd(dcum, z_f, tm)
    dh2, dflb, gmn1 = _fox_proj_bwd(h2, dh3, dq, dkt, dvt, dfl, mix_norm_g[1:2], wq, wk, wv, wf, tm)
    g_q = _wgrad(hn1, dq, 1, "wgrad_q")[0]
    g_k = _wgrad_t(dkt, hn1, "wgrad_k").T
    g_v = _wgrad_t(dvt, hn1, "wgrad_v").T
    g_f = _wgrad(hn1, dflb, 1, "wgrad_f")[0][:, :n_head]
    g_qkvf = jnp.concatenate([g_q, g_k, g_v, g_f], axis=1).reshape(d, 4, qkvf_cols).transpose(1, 0, 2)

    dh1, da0f, dup0, gcw0, gcb0, gfn0 = _ffn_bwd(h1, dh2, fa0, fup0, ffn_norm_g[0:1], wg_all, wu_all, wd_all, 0,
                                                 conv_w8[0], ffn_conv_b[0:1], tm)
    g_gate0 = _wgrad(fhn0, da0f, 4, "wgrad_gate_0")
    g_up0 = _wgrad(fhn0, dup0, 4, "wgrad_up_0")
    g_down0 = _wgrad(fhid0, dh2, 1, "wgrad_down_0").reshape(4, f // 4, d)

    dh0, da0, g_ws, g_bs_t, g_lng, g_lnb, gmn0 = _gmlp_bwd(
        h0, dh1, a0, mix_norm_g[0:1], w_in, gm_ln_g, gm_ln_b, wc, wct, bias, w_out, seg_groups, CHUNK)
    g_win = _wgrad(hn0, da0, 4, "wgrad_gm_in")
    g_wout = _wgrad(gated0, dh1, 1, "wgrad_gm_out").reshape(4, e // 4, d)

    big, core = _reduce_scatter([g_win, g_wout, g_qkvf, g_wo, g_gate0, g_gate1, g_up0, g_up1, g_down0, g_down1])
    r_win, r_wout, r_qkvf, r_wo, r_gate0, r_gate1, r_up0, r_up1, r_down0, r_down1 = big

    small = [jnp.concatenate([gmn0, gmn1]), jnp.concatenate([gfn0, gfn1]), g_lng, g_lnb, g_ws[None],
             g_bs_t[:, :GM_GROUPS].T[None], g_bf[:, :n_head], jnp.stack([gcw0[:3], gcw1[:3]]),
             jnp.concatenate([gcb0, gcb1]), g_final.reshape(d), loss_part[0, :1]]
    small_shapes = [a.shape for a in small]
    reduced = _unpack(_small_all_reduce(_pack(small)), small_shapes)
    (r_mix, r_ffn, r_lng, r_lnb, r_ws, r_bs, r_bf, r_cw_full, r_cb, r_final, r_loss) = reduced
    chip = 2 * lax.axis_index("x") + lax.axis_index("y")
    r_cw = lax.dynamic_slice_in_dim(r_cw_full, chip * (f // 4), f // 4, axis=2)

    def update_big(name, w, m, v, per_layer):
        parts = [_adamw_halves(w[l], m[l], v[l], own, other, core, f"adamw_{name}_{l}")
                 for l, (own, other) in enumerate(per_layer)]
        return tuple(jnp.stack([p[i] for p in parts]) for i in range(4))

    res = {}
    res["gm_w_in"] = update_big("gm_w_in", gm_w_in, m_gm_w_in, v_gm_w_in, [r_win])
    res["gm_w_out"] = update_big("gm_w_out", gm_w_out, m_gm_w_out, v_gm_w_out, [r_wout])
    res["fox_w_qkvf"] = update_big("fox_w_qkvf", fox_w_qkvf, m_fox_w_qkvf, v_fox_w_qkvf, [r_qkvf])
    res["fox_w_o"] = update_big("fox_w_o", fox_w_o, m_fox_w_o, v_fox_w_o, [r_wo])
    res["ffn_w_gate"] = update_big("ffn_w_gate", ffn_w_gate, m_ffn_w_gate, v_ffn_w_gate, [r_gate0, r_gate1])
    res["ffn_w_up"] = update_big("ffn_w_up", ffn_w_up, m_ffn_w_up, v_ffn_w_up, [r_up0, r_up1])
    res["ffn_w_down"] = update_big("ffn_w_down", ffn_w_down, m_ffn_w_down, v_ffn_w_down, [r_down0, r_down1])

    small_names = ["mix_norm_g", "ffn_norm_g", "gm_ln_g", "gm_ln_b", "gm_w_s", "gm_b_s", "fox_b_f", "ffn_conv_w",
                   "ffn_conv_b", "final_norm_g"]
    small_w = [mix_norm_g, ffn_norm_g, gm_ln_g, gm_ln_b, gm_w_s, gm_b_s, fox_b_f, ffn_conv_w, ffn_conv_b, final_norm_g]
    small_m = [m_mix_norm_g, m_ffn_norm_g, m_gm_ln_g, m_gm_ln_b, m_gm_w_s, m_gm_b_s, m_fox_b_f, m_ffn_conv_w,
               m_ffn_conv_b, m_final_norm_g]
    small_v = [v_mix_norm_g, v_ffn_norm_g, v_gm_ln_g, v_gm_ln_b, v_gm_w_s, v_gm_b_s, v_fox_b_f, v_ffn_conv_w,
               v_ffn_conv_b, v_final_norm_g]
    small_g = [r_mix, r_ffn, r_lng, r_lnb, r_ws, r_bs, r_bf, r_cw, r_cb, r_final]
    shapes = [w.shape for w in small_w]
    small_g = [g.reshape(s) for g, s in zip(small_g, shapes)]
    dlt, mn, vn = _adamw(_pack(small_w), _pack(small_m), _pack(small_v), _pack(small_g), "adamw_small")
    for name, g, dl_, m_, v_ in zip(small_names, small_g, _unpack(dlt, shapes), _unpack(mn, shapes), _unpack(vn, shapes)):
        res[name] = (g, dl_, m_, v_)

    order = ["mix_norm_g", "ffn_norm_g", "gm_w_in", "gm_ln_g", "gm_ln_b", "gm_w_s", "gm_b_s", "gm_w_out", "fox_w_qkvf",
             "fox_b_f", "fox_w_o", "ffn_w_gate", "ffn_w_up", "ffn_conv_w", "ffn_conv_b", "ffn_w_down", "final_norm_g"]
    outs = [r_loss.reshape(()), dh0[None]]
    for part in range(4):
        outs += [res[name][part] for name in order]
    return tuple(outs)
```

```python
import functools
import math

import jax
import jax.numpy as jnp
from jax import lax
from jax.experimental import pallas as pl
from jax.experimental.pallas import tpu as pltpu

F32 = jnp.float32
BF16 = jnp.bfloat16

RMS_EPS = 1e-6
LN_EPS = 1e-5
CHUNK = 128
GM_GROUPS = 8
HEAD_DIM = 64
LANES = 128
ATT_BLOCK = 256
ATT_CHUNK = 1024
VMEM_LIMIT_V7X = 56 * 1024 * 1024

ADAM_LR = 0.001
ADAM_B1 = 0.9
ADAM_B2 = 0.999
ADAM_EPS = 1e-08
ADAM_WD = 0.01
ADAM_STEP = 10

MESH = pl.DeviceIdType.MESH
ANY = pl.BlockSpec(memory_space=pl.ANY)
NEG_BIG = -1e30


def _params(n_grid):
    return pltpu.CompilerParams(dimension_semantics=("arbitrary",) * n_grid, vmem_limit_bytes=VMEM_LIMIT_V7X)


def _dot(a, b):
    return jnp.dot(a, b, preferred_element_type=F32)


def _dot_nt(a, b):
    return lax.dot_general(a, b, (((1,), (1,)), ((), ())), preferred_element_type=F32)


def _dot_tn(a, b):
    return lax.dot_general(a, b, (((0,), (0,)), ((), ())), preferred_element_type=F32)


def _split3(x):
    hi = x.astype(BF16)
    r = x - hi.astype(F32)
    mid = r.astype(BF16)
    lo = (r - mid.astype(F32)).astype(BF16)
    return hi, mid, lo


def _dot3_lhs(x, m):
    hi, mid, lo = _split3(x)
    return _dot(hi, m) + _dot(mid, m) + _dot(lo, m)


def _dot3_rhs(m, x):
    hi, mid, lo = _split3(x)
    return _dot(m, hi) + _dot(m, mid) + _dot(m, lo)


def _load_once(pairs, sem):
    @pl.when(pl.program_id(0) == 0)
    def _():
        copies = [pltpu.make_async_copy(src, dst, sem.at[k]) for k, (src, dst) in enumerate(pairs)]
        for cp in copies:
            cp.start()
        for cp in copies:
            cp.wait()


def _rms_fwd(x, g):
    r = lax.rsqrt(jnp.mean(x * x, axis=-1, keepdims=True) + RMS_EPS)
    xhat = x * r
    return xhat, r, xhat * g


def _rms_bwd(dy, xhat, r, g):
    w = dy * g
    dx = r * (w - xhat * jnp.mean(w * xhat, axis=-1, keepdims=True))
    return dx, dy * xhat


def _gelu_parts(a):
    c = math.sqrt(2.0 / math.pi)
    a2 = a * a
    t = jnp.tanh(c * (a + 0.044715 * a * a2))
    z = 0.5 * a * (1.0 + t)
    dz = 0.5 * (1.0 + t) + 0.5 * a * (1.0 - t * t) * (c * (1.0 + 3.0 * 0.044715 * a2))
    return z, dz


def _sigmoid(x):
    return 1.0 / (1.0 + jnp.exp(-x))


def _gmlp_core(a, lng, lnb, wc_ref, bias, n_chunk, gd):
    e = a.shape[1] // 2
    z, dz = _gelu_parts(a)
    u = z[:, :e]
    v = z[:, e:]
    mu = jnp.mean(v, axis=-1, keepdims=True)
    vc = v - mu
    rstd = lax.rsqrt(jnp.mean(vc * vc, axis=-1, keepdims=True) + LN_EPS)
    vhat = vc * rstd
    vln = vhat * lng + lnb
    vlb = vln.astype(BF16)
    rows = []
    for ci in range(n_chunk):
        cols = []
        for g in range(GM_GROUPS):
            blk = vlb[ci * CHUNK:(ci + 1) * CHUNK, g * gd:(g + 1) * gd]
            cols.append(_dot(wc_ref[g], blk))
        rows.append(jnp.concatenate(cols, axis=1) + bias)
    s = rows[0] if n_chunk == 1 else jnp.concatenate(rows, axis=0)
    return dz, u, vhat, rstd, vlb, s


def _gmlp_fwd(h, g_mix, w_in, lng, lnb, wc, bias, w_out, tm):
    s_len, d = h.shape
    n_p, _, w = w_in.shape
    e = w_out.shape[0]
    gd = e // GM_GROUPS
    n_chunk = tm // CHUNK

    def body(h_ref, g_ref, win_hbm, lng_ref, lnb_ref, wc_ref, bias_ref, wout_hbm,
             hout_ref, a_ref, hn_ref, gated_ref, win_v, wout_v, sem):
        _load_once([(win_hbm, win_v), (wout_hbm, wout_v)], sem)
        x = h_ref[...]
        _, _, y = _rms_fwd(x, g_ref[...])
        hn = y.astype(BF16)
        hn_ref[...] = hn
        for p in range(n_p):
            a_ref[:, p * w:(p + 1) * w] = _dot(hn, win_v[p])
        _, u, _, _, _, s = _gmlp_core(a_ref[...], lng_ref[...], lnb_ref[...], wc_ref, bias_ref[...], n_chunk, gd)
        gated = (u * s).astype(BF16)
        gated_ref[...] = gated
        hout_ref[...] = x + _dot(gated, wout_v[...])

    row = lambda i: (i, 0)
    const2 = lambda i: (0, 0)
    return pl.pallas_call(
        body, name="gmlp_fwd", grid=(s_len // tm,),
        in_specs=[pl.BlockSpec((tm, d), row), pl.BlockSpec((1, d), const2), ANY,
                  pl.BlockSpec((1, e), const2), pl.BlockSpec((1, e), const2),
                  pl.BlockSpec(wc.shape, lambda i: (0, 0, 0)), pl.BlockSpec((CHUNK, e), const2), ANY],
        out_specs=[pl.BlockSpec((tm, d), row), pl.BlockSpec((tm, 2 * e), row),
                   pl.BlockSpec((tm, d), row), pl.BlockSpec((tm, e), row)],
        out_shape=[jax.ShapeDtypeStruct((s_len, d), F32), jax.ShapeDtypeStruct((s_len, 2 * e), F32),
                   jax.ShapeDtypeStruct((s_len, d), BF16), jax.ShapeDtypeStruct((s_len, e), BF16)],
        scratch_shapes=[pltpu.VMEM(w_in.shape, BF16), pltpu.VMEM(w_out.shape, BF16), pltpu.SemaphoreType.DMA((2,))],
        compiler_params=_params(1),
    )(h, g_mix, w_in, lng, lnb, wc, bias, w_out)


def _gmlp_bwd(h, dh, a, g_mix, w_in, lng, lnb, wc, wct, bias, w_out, seg, tm):
    s_len, d = h.shape
    n_p, _, w = w_in.shape
    e = w_out.shape[0]
    gd = e // GM_GROUPS
    n_chunk = tm // CHUNK
    n_blk = s_len // tm

    def body(h_ref, dh_ref, a_ref, g_ref, win_hbm, lng_ref, lnb_ref, wc_ref, wct_ref, bias_ref, wout_hbm, seg_ref,
             dhin_ref, da_ref, gws_ref, gbs_ref, glng_ref, glnb_ref, gmix_ref, win_v, wout_v, dsum, sem):
        i = pl.program_id(0)
        _load_once([(win_hbm, win_v), (wout_hbm, wout_v)], sem)

        @pl.when(i == 0)
        def _():
            gws_ref[...] = jnp.zeros_like(gws_ref)
            glng_ref[...] = jnp.zeros_like(glng_ref)
            glnb_ref[...] = jnp.zeros_like(glnb_ref)
            gmix_ref[...] = jnp.zeros_like(gmix_ref)
            dsum[...] = jnp.zeros_like(dsum)

        x = h_ref[...]
        dh_v = dh_ref[...]
        g = g_ref[...]
        lng_v = lng_ref[...]
        xhat, r, _ = _rms_fwd(x, g)
        dz_da, u, vhat, rstd, vlb, s = _gmlp_core(a_ref[...], lng_v, lnb_ref[...], wc_ref, bias_ref[...], n_chunk, gd)
        dg = _dot_nt(dh_v.astype(BF16), wout_v[...])
        du = dg * s
        ds = dg * u
        dsb = ds.astype(BF16)
        rows = []
        ds_acc = None
        for ci in range(n_chunk):
            lo, hi = ci * CHUNK, (ci + 1) * CHUNK
            cols = []
            for gi in range(GM_GROUPS):
                d_blk = dsb[lo:hi, gi * gd:(gi + 1) * gd]
                gws_ref[gi] += _dot_nt(d_blk, vlb[lo:hi, gi * gd:(gi + 1) * gd])
                cols.append(_dot(wct_ref[gi], d_blk))
            rows.append(jnp.concatenate(cols, axis=1))
            ds_acc = ds[lo:hi] if ds_acc is None else ds_acc + ds[lo:hi]
        dsum[...] += ds_acc
        dvln = rows[0] if n_chunk == 1 else jnp.concatenate(rows, axis=0)
        glng_ref[...] += jnp.sum(dvln * vhat, axis=0, keepdims=True)
        glnb_ref[...] += jnp.sum(dvln, axis=0, keepdims=True)
        dvhat = dvln * lng_v
        dv = rstd * (dvhat - jnp.mean(dvhat, axis=-1, keepdims=True)
                     - vhat * jnp.mean(dvhat * vhat, axis=-1, keepdims=True))
        da = jnp.concatenate([du, dv], axis=1) * dz_da
        dab = da.astype(BF16)
        da_ref[...] = dab
        dhn = _dot_nt(dab[:, :w], win_v[0])
        for p in range(1, n_p):
            dhn += _dot_nt(dab[:, p * w:(p + 1) * w], win_v[p])
        dx, gg = _rms_bwd(dhn, xhat, r, g)
        gmix_ref[...] += jnp.sum(gg, axis=0, keepdims=True)
        dhin_ref[...] = dh_v + dx

        @pl.when(i == n_blk - 1)
        def _():
            tril = lax.broadcasted_iota(jnp.int32, (CHUNK, CHUNK), 0) >= lax.broadcasted_iota(jnp.int32, (CHUNK, CHUNK), 1)
            for gi in range(GM_GROUPS):
                gws_ref[gi] = jnp.where(tril, gws_ref[gi], 0.0)
            gbs_ref[...] = _dot3_lhs(dsum[...], seg_ref[...])

    row = lambda i: (i, 0)
    const2 = lambda i: (0, 0)
    const3 = lambda i: (0, 0, 0)
    return pl.pallas_call(
        body, name="gmlp_bwd", grid=(n_blk,),
        in_specs=[pl.BlockSpec((tm, d), row), pl.BlockSpec((tm, d), row), pl.BlockSpec((tm, 2 * e), row),
                  pl.BlockSpec((1, d), const2), ANY, pl.BlockSpec((1, e), const2), pl.BlockSpec((1, e), const2),
                  pl.BlockSpec(wc.shape, const3), pl.BlockSpec(wct.shape, const3), pl.BlockSpec((CHUNK, e), const2),
                  ANY, pl.BlockSpec((e, LANES), const2)],
        out_specs=[pl.BlockSpec((tm, d), row), pl.BlockSpec((tm, 2 * e), row), pl.BlockSpec(wc.shape, const3),
                   pl.BlockSpec((CHUNK, LANES), const2), pl.BlockSpec((1, e), const2), pl.BlockSpec((1, e), const2),
                   pl.BlockSpec((1, d), const2)],
        out_shape=[jax.ShapeDtypeStruct((s_len, d), F32), jax.ShapeDtypeStruct((s_len, 2 * e), BF16),
                   jax.ShapeDtypeStruct(wc.shape, F32), jax.ShapeDtypeStruct((CHUNK, LANES), F32),
                   jax.ShapeDtypeStruct((1, e), F32), jax.ShapeDtypeStruct((1, e), F32), jax.ShapeDtypeStruct((1, d), F32)],
        scratch_shapes=[pltpu.VMEM(w_in.shape, BF16), pltpu.VMEM(w_out.shape, BF16), pltpu.VMEM((CHUNK, e), F32),
                        pltpu.SemaphoreType.DMA((2,))],
        compiler_params=_params(1),
    )(h, dh, a, g_mix, w_in, lng, lnb, wc, wct, bias, w_out, seg)


def _shift_down(a, k, fill):
    tm = a.shape[0]
    out = pltpu.roll(a, k, 0)
    rid = lax.broadcasted_iota(jnp.int32, a.shape, 0)
    for j in range(k):
        out = jnp.where(rid == j, fill[8 - k + j:8 - k + j + 1, :], out)
    return out


def _shift_up(a, k, fill):
    tm = a.shape[0]
    out = pltpu.roll(a, tm - k, 0)
    rid = lax.broadcasted_iota(jnp.int32, a.shape, 0)
    for j in range(k):
        out = jnp.where(rid == tm - k + j, fill[j:j + 1, :], out)
    return out


def _ffn_fwd(h, g_norm, wg_all, wu_all, wd_all, layer, conv_w, conv_b, tm):
    s_len, d = h.shape
    n_p = wg_all.shape[0]
    fq = wg_all.shape[3]
    f = n_p * fq

    def body(h_ref, g_ref, wg_hbm, wu_hbm, wd_hbm, cw_ref, cb_ref,
             hout_ref, a_ref, up_ref, hn_ref, hid_ref, wg_v, wu_v, wd_v, carry, sem):
        i = pl.program_id(0)
        _load_once([(wg_hbm.at[:, layer], wg_v), (wu_hbm.at[:, layer], wu_v), (wd_hbm.at[:, layer], wd_v)], sem)

        @pl.when(i == 0)
        def _():
            carry[...] = jnp.zeros_like(carry)

        x = h_ref[...]
        _, _, y = _rms_fwd(x, g_ref[...])
        hn = y.astype(BF16)
        hn_ref[...] = hn
        for p in range(n_p):
            a_ref[:, p * fq:(p + 1) * fq] = _dot(hn, wg_v[p])
            up_ref[:, p * fq:(p + 1) * fq] = _dot(hn, wu_v[p])
        a = a_ref[...]
        prev = carry[...]
        am1 = _shift_down(a, 1, prev)
        am2 = _shift_down(a, 2, prev)
        carry[...] = a[tm - 8:tm, :]
        cw = cw_ref[...]
        ac = cb_ref[...] + am2 * cw[0:1, :]
        ac = ac + am1 * cw[1:2, :]
        ac = ac + a * cw[2:3, :]
        hid = (ac * _sigmoid(ac) * up_ref[...]).astype(BF16)
        hid_ref[...] = hid
        y2 = _dot(hid[:, :fq], wd_v[0])
        for p in range(1, n_p):
            y2 += _dot(hid[:, p * fq:(p + 1) * fq], wd_v[p])
        hout_ref[...] = x + y2

    row = lambda i: (i, 0)
    const2 = lambda i: (0, 0)
    return pl.pallas_call(
        body, name=f"ffn_fwd_{layer}", grid=(s_len // tm,),
        in_specs=[pl.BlockSpec((tm, d), row), pl.BlockSpec((1, d), const2), ANY, ANY, ANY,
                  pl.BlockSpec((8, f), const2), pl.BlockSpec((1, f), const2)],
        out_specs=[pl.BlockSpec((tm, d), row), pl.BlockSpec((tm, f), row), pl.BlockSpec((tm, f), row),
                   pl.BlockSpec((tm, d), row), pl.BlockSpec((tm, f), row)],
        out_shape=[jax.ShapeDtypeStruct((s_len, d), F32), jax.ShapeDtypeStruct((s_len, f), F32),
                   jax.ShapeDtypeStruct((s_len, f), F32), jax.ShapeDtypeStruct((s_len, d), BF16),
                   jax.ShapeDtypeStruct((s_len, f), BF16)],
        scratch_shapes=[pltpu.VMEM((n_p, d, fq), BF16), pltpu.VMEM((n_p, d, fq), BF16), pltpu.VMEM((n_p, fq, d), BF16),
                        pltpu.VMEM((8, f), F32), pltpu.SemaphoreType.DMA((3,))],
        compiler_params=_params(1),
    )(h, g_norm, wg_all, wu_all, wd_all, conv_w, conv_b)


def _ffn_bwd(h, dh, a, up, g_norm, wg_all, wu_all, wd_all, layer, conv_w, conv_b, tm):
    s_len, d = h.shape
    n_p = wg_all.shape[0]
    fq = wg_all.shape[3]
    f = n_p * fq
    n_blk = s_len // tm
    t8 = tm // 8

    def body(h_ref, dh_ref, a_ref, ahalo_ref, up_ref, g_ref, wg_hbm, wu_hbm, wd_hbm, cw_ref, cb_ref,
             dhin_ref, da_ref, dup_ref, gcw_ref, gcb_ref, gn_ref, wg_v, wu_v, wd_v, carry, sem):
        i = pl.program_id(0)
        _load_once([(wg_hbm.at[:, layer], wg_v), (wu_hbm.at[:, layer], wu_v), (wd_hbm.at[:, layer], wd_v)], sem)

        @pl.when(i == 0)
        def _():
            carry[...] = jnp.zeros_like(carry)
            gcw_ref[...] = jnp.zeros_like(gcw_ref)
            gcb_ref[...] = jnp.zeros_like(gcb_ref)
            gn_ref[...] = jnp.zeros_like(gn_ref)

        x = h_ref[...]
        dh_v = dh_ref[...]
        g = g_ref[...]
        xhat, r, _ = _rms_fwd(x, g)
        a = a_ref[...]
        up_v = up_ref[...]
        prev = jnp.where(i == n_blk - 1, 0.0, ahalo_ref[...])
        am1 = _shift_down(a, 1, prev)
        am2 = _shift_down(a, 2, prev)
        cw = cw_ref[...]
        ac = cb_ref[...] + am2 * cw[0:1, :]
        ac = ac + am1 * cw[1:2, :]
        ac = ac + a * cw[2:3, :]
        sg = _sigmoid(ac)
        sil = ac * sg
        dhb = dh_v.astype(BF16)
        dhid = jnp.concatenate([_dot_nt(dhb, wd_v[p]) for p in range(n_p)], axis=1)
        dup = dhid * sil
        dac = dhid * up_v * (sg * (1.0 + ac * (1.0 - sg)))
        gcb_ref[...] += jnp.sum(dac, axis=0, keepdims=True)
        gcw_ref[0:1, :] += jnp.sum(dac * am2, axis=0, keepdims=True)
        gcw_ref[1:2, :] += jnp.sum(dac * am1, axis=0, keepdims=True)
        gcw_ref[2:3, :] += jnp.sum(dac * a, axis=0, keepdims=True)
        nxt = carry[...]
        dp1 = _shift_up(dac, 1, nxt)
        dp2 = _shift_up(dac, 2, nxt)
        carry[...] = dac[0:8, :]
        da = dac * cw[2:3, :] + dp1 * cw[1:2, :] + dp2 * cw[0:1, :]
        dab = da.astype(BF16)
        dupb = dup.astype(BF16)
        da_ref[...] = dab
        dup_ref[...] = dupb
        dhn = _dot_nt(dab[:, :fq], wg_v[0]) + _dot_nt(dupb[:, :fq], wu_v[0])
        for p in range(1, n_p):
            dhn += _dot_nt(dab[:, p * fq:(p + 1) * fq], wg_v[p]) + _dot_nt(dupb[:, p * fq:(p + 1) * fq], wu_v[p])
        dx, gg = _rms_bwd(dhn, xhat, r, g)
        gn_ref[...] += jnp.sum(gg, axis=0, keepdims=True)
        dhin_ref[...] = dh_v + dx

    rev = lambda i: (n_blk - 1 - i, 0)
    halo = lambda i: (jnp.maximum((n_blk - 1 - i) * t8 - 1, 0), 0)
    const2 = lambda i: (0, 0)
    return pl.pallas_call(
        body, name=f"ffn_bwd_{layer}", grid=(n_blk,),
        in_specs=[pl.BlockSpec((tm, d), rev), pl.BlockSpec((tm, d), rev), pl.BlockSpec((tm, f), rev),
                  pl.BlockSpec((8, f), halo), pl.BlockSpec((tm, f), rev), pl.BlockSpec((1, d), const2), ANY, ANY, ANY,
                  pl.BlockSpec((8, f), const2), pl.BlockSpec((1, f), const2)],
        out_specs=[pl.BlockSpec((tm, d), rev), pl.BlockSpec((tm, f), rev), pl.BlockSpec((tm, f), rev),
                   pl.BlockSpec((8, f), const2), pl.BlockSpec((1, f), const2), pl.BlockSpec((1, d), const2)],
        out_shape=[jax.ShapeDtypeStruct((s_len, d), F32), jax.ShapeDtypeStruct((s_len, f), BF16),
                   jax.ShapeDtypeStruct((s_len, f), BF16), jax.ShapeDtypeStruct((8, f), F32),
                   jax.ShapeDtypeStruct((1, f), F32), jax.ShapeDtypeStruct((1, d), F32)],
        scratch_shapes=[pltpu.VMEM((n_p, d, fq), BF16), pltpu.VMEM((n_p, d, fq), BF16), pltpu.VMEM((n_p, fq, d), BF16),
                        pltpu.VMEM((8, f), F32), pltpu.SemaphoreType.DMA((3,))],
        compiler_params=_params(1),
    )(h, dh, a, a, up, g_norm, wg_all, wu_all, wd_all, conv_w, conv_b)


def _even_head_lanes(shape, axis):
    return (lax.broadcasted_iota(jnp.int32, shape, axis) & HEAD_DIM) == 0


def _pair_select(lo, hi, shape):
    return jnp.where(lax.broadcasted_iota(jnp.int32, shape, 1) < HEAD_DIM, lo, hi)


def _causal(row0, col0, shape):
    return row0 + lax.broadcasted_iota(jnp.int32, shape, 0) >= col0 + lax.broadcasted_iota(jnp.int32, shape, 1)


N_SPARE = 3


def _spare_selectors(d, key_side):
    lane = jnp.arange(d)[None, :]
    row = jnp.arange(N_SPARE * LANES)[:, None]
    head, part = row % LANES, row // LANES
    off = N_SPARE if key_side else 0
    sel_a = ((head % 2 == 0) & (lane == LANES * (head // 2) + HEAD_DIM + off + part)).astype(F32)
    sel_b = ((head % 2 == 1) & (lane == LANES * (head // 2) + off + part)).astype(F32)
    sign = -1.0 if key_side else 1.0
    ones_off = 0 if key_side else N_SPARE
    in_pair = jnp.arange(d)[None, :] % LANES
    ones_a = ((in_pair >= HEAD_DIM + ones_off) & (in_pair < HEAD_DIM + ones_off + N_SPARE)).astype(F32)
    ones_b = ((in_pair >= ones_off) & (in_pair < ones_off + N_SPARE)).astype(F32)
    return (sign * sel_a).astype(BF16), (sign * sel_b).astype(BF16), ones_a, ones_b


def _parts(x):
    return jnp.concatenate(_split3(x), axis=1)


def _fox_proj_fwd(h, g_norm, wq, wk, wv, wf, bf, sel_q, sel_k, tm):
    s_len, d = h.shape
    sq_a, sq_b, oq_a, oq_b = sel_q
    sk_a, sk_b, ok_a, ok_b = sel_k

    def body(h_ref, g_ref, wq_hbm, wk_hbm, wv_hbm, wf_ref, bf_ref, sqa_ref, sqb_ref, oqa_ref, oqb_ref,
             ska_ref, skb_ref, oka_ref, okb_ref,
             hn_ref, qa_ref, qb_ref, ka_ref, kb_ref, kat_ref, kbt_ref, va_ref, vb_ref, vat_ref, vbt_ref, z_ref,
             wq_v, wk_v, wv_v, total, sem):
        i = pl.program_id(0)
        _load_once([(wq_hbm, wq_v), (wk_hbm, wk_v), (wv_hbm, wv_v)], sem)

        @pl.when(i == 0)
        def _():
            total[...] = jnp.zeros_like(total)

        x = h_ref[...]
        _, _, y = _rms_fwd(x, g_ref[...])
        hn = y.astype(BF16)
        hn_ref[...] = hn
        z = _dot(hn, wf_ref[...]) + bf_ref[...]
        z_ref[...] = z
        logf = jnp.minimum(z, 0.0) - jnp.log(1.0 + jnp.exp(-jnp.abs(z)))
        tri = (lax.broadcasted_iota(jnp.int32, (tm, tm), 0) >= lax.broadcasted_iota(jnp.int32, (tm, tm), 1))
        cum = _dot3_rhs(jnp.where(tri, 1.0, 0.0).astype(BF16), logf) + total[0:1, :]
        total[...] = jnp.broadcast_to(cum[tm - 1:tm, :], total.shape)
        parts = _parts(cum)

        even = _even_head_lanes((tm, d), 1)
        q = _dot(hn, wq_v[...]) * (HEAD_DIM ** -0.5)
        qa_ref[...] = jnp.where(even, q, _dot(parts, sqa_ref[...]) + oqa_ref[...]).astype(BF16)
        qb_ref[...] = jnp.where(even, _dot(parts, sqb_ref[...]) + oqb_ref[...], q).astype(BF16)
        k = _dot(hn, wk_v[...])
        ka = jnp.where(even, k, _dot(parts, ska_ref[...]) + oka_ref[...])
        kb = jnp.where(even, _dot(parts, skb_ref[...]) + okb_ref[...], k)
        ka_ref[...] = ka.astype(BF16)
        kb_ref[...] = kb.astype(BF16)
        kat_ref[...] = ka.T.astype(BF16)
        kbt_ref[...] = kb.T.astype(BF16)
        v = _dot(hn, wv_v[...])
        va = jnp.where(even, v, oka_ref[...])
        vb = jnp.where(even, okb_ref[...], v)
        va_ref[...] = va.astype(BF16)
        vb_ref[...] = vb.astype(BF16)
        vat_ref[...] = va.T.astype(BF16)
        vbt_ref[...] = vb.T.astype(BF16)

    row = lambda i: (i, 0)
    col = lambda i: (0, i)
    const2 = lambda i: (0, 0)
    sd = jax.ShapeDtypeStruct((s_len, d), BF16)
    ds_ = jax.ShapeDtypeStruct((d, s_len), BF16)
    rs, cs = pl.BlockSpec((tm, d), row), pl.BlockSpec((d, tm), col)
    sel = pl.BlockSpec((N_SPARE * LANES, d), const2)
    one = pl.BlockSpec((1, d), const2)
    return pl.pallas_call(
        body, name="fox_proj_fwd", grid=(s_len // tm,),
        in_specs=[rs, one, ANY, ANY, ANY, pl.BlockSpec((d, LANES), const2), pl.BlockSpec((1, LANES), const2),
                  sel, sel, one, one, sel, sel, one, one],
        out_specs=[rs, rs, rs, rs, rs, cs, cs, rs, rs, cs, cs, pl.BlockSpec((tm, LANES), row)],
        out_shape=[sd, sd, sd, sd, sd, ds_, ds_, sd, sd, ds_, ds_, jax.ShapeDtypeStruct((s_len, LANES), F32)],
        scratch_shapes=[pltpu.VMEM((d, d), BF16), pltpu.VMEM((d, d), BF16), pltpu.VMEM((d, d), BF16),
                        pltpu.VMEM((8, LANES), F32), pltpu.SemaphoreType.DMA((3,))],
        compiler_params=_params(1),
    )(h, g_norm, wq, wk, wv, wf, bf, sq_a, sq_b, oq_a, oq_b, sk_a, sk_b, ok_a, ok_b)


def _spare_cols(x, base):
    xf = x[:, base:base + N_SPARE].astype(F32)
    return xf[:, 0:1] + xf[:, 1:2] + xf[:, 2:3]


def _with_query_term(x, term, base):
    lane = lax.broadcasted_iota(jnp.int32, x.shape, 1)
    hi, mid, lo = _split3(term)
    out = jnp.where(lane == base, hi.astype(F32), x)
    out = jnp.where(lane == base + 1, mid.astype(F32), out)
    out = jnp.where(lane == base + 2, lo.astype(F32), out)
    return jnp.where((lane >= base + N_SPARE) & (lane < base + 2 * N_SPARE), 1.0, out)


def _flash_fwd(qa, qb, kat, kbt, va, vb):
    s_len, d = qa.shape
    sub = ATT_BLOCK
    n_sub = 2 if s_len % (2 * sub) == 0 else 1
    t = n_sub * sub
    w = min(ATT_CHUNK, s_len)
    n_pair = d // LANES
    n_q = s_len // t
    bases = (HEAD_DIM, 0)
    chains = [(r, hh) for r in range(n_sub) for hh in range(2)]

    def body(qa_ref, qb_ref, kat_ref, kbt_ref, va_ref, vb_ref, o_ref, qa2_ref, qb2_ref, qat2_ref, qbt2_ref):
        i = pl.program_id(1)
        q_refs = (qa_ref, qb_ref)
        qs = [q_refs[hh][r * sub:(r + 1) * sub, :] for r, hh in chains]
        kts = (kat_ref, kbt_ref)
        vs = (va_ref, vb_ref)

        def step(kb, carry, masked):
            off = pl.multiple_of(kb * w, w)
            out = []
            for c, (r, hh) in enumerate(chains):
                m, acc = carry[c]
                s = _dot(qs[c], kts[hh][:, pl.ds(off, w)])
                if masked:
                    s = jnp.where(_causal(i * t + r * sub, off, (sub, w)), s, NEG_BIG)
                m_new = jnp.maximum(m, jnp.max(s, axis=1, keepdims=True))
                p = jnp.exp(s - m_new)
                out.append((m_new, acc * jnp.exp(m - m_new) + _dot(p.astype(BF16), vs[hh][pl.ds(off, w), :])))
            return tuple(out)

        init = ((jnp.full((sub, 1), NEG_BIG, F32), jnp.zeros((sub, LANES), F32)),) * len(chains)
        diag = (i * t) // w
        carry = lax.fori_loop(0, diag, lambda kb, c: step(kb, c, False), init)
        carry = step(diag, carry, True)
        for r in range(n_sub):
            outs, q2 = [], []
            for hh in range(2):
                m, acc = carry[2 * r + hh]
                l = acc[:, bases[hh]:bases[hh] + 1]
                outs.append(acc / l)
                term = _spare_cols(qs[2 * r + hh], bases[hh]) - (m + jnp.log(l))
                q2.append(_with_query_term(qs[2 * r + hh].astype(F32), term, bases[hh]))
            rows = slice(r * sub, (r + 1) * sub)
            o_ref[rows, :] = _pair_select(outs[0], outs[1], (sub, LANES))
            qa2_ref[rows, :] = q2[0].astype(BF16)
            qb2_ref[rows, :] = q2[1].astype(BF16)
            qat2_ref[:, rows] = q2[0].T.astype(BF16)
            qbt2_ref[:, rows] = q2[1].T.astype(BF16)

    qblk = pl.BlockSpec((t, LANES), lambda j, i: (i, j))
    qblk_t = pl.BlockSpec((LANES, t), lambda j, i: (j, i))
    whole_t = pl.BlockSpec((LANES, s_len), lambda j, i: (j, 0))
    whole = pl.BlockSpec((s_len, LANES), lambda j, i: (0, j))
    sd = jax.ShapeDtypeStruct((s_len, d), BF16)
    ds_ = jax.ShapeDtypeStruct((d, s_len), BF16)
    return pl.pallas_call(
        body, name="flash_fwd", grid=(n_pair, n_q),
        in_specs=[qblk, qblk, whole_t, whole_t, whole, whole],
        out_specs=[qblk, qblk, qblk, qblk_t, qblk_t],
        out_shape=[jax.ShapeDtypeStruct((s_len, d), F32), sd, sd, ds_, ds_],
        compiler_params=_params(2),
    )(qa, qb, kat, kbt, va, vb)


def _flash_bwd(qa, qb, qat, qbt, kat, kbt, ka, kb_, vat, vbt, doa, dob, doat, dobt):
    s_len, d = qa.shape
    t = ATT_BLOCK
    w = min(ATT_CHUNK, s_len)
    n_pair = d // LANES
    n_q = s_len // t

    def body(qa_ref, qb_ref, qat_ref, qbt_ref, kat_hbm, kbt_hbm, ka_hbm, kb_hbm, vat_hbm, vbt_hbm,
             doa_ref, dob_ref, doat_ref, dobt_ref,
             dq_ref, dkt_ref, dvt_ref, rs_ref, cs_ref,
             kat_v, kbt_v, ka_v, kb_v, vat_v, vbt_v, dkt_acc, dvt_acc, cs_acc, sem):
        j = pl.program_id(0)
        i = pl.program_id(1)

        @pl.when(i == 0)
        def _():
            rows = pl.ds(pl.multiple_of(j * LANES, LANES), LANES)
            copies = [pltpu.make_async_copy(src, dst, sem.at[n]) for n, (src, dst) in enumerate([
                (kat_hbm.at[rows, :], kat_v), (kbt_hbm.at[rows, :], kbt_v), (ka_hbm.at[:, rows], ka_v),
                (kb_hbm.at[:, rows], kb_v), (vat_hbm.at[rows, :], vat_v), (vbt_hbm.at[rows, :], vbt_v)])]
            for cp in copies:
                cp.start()
            dkt_acc[...] = jnp.zeros_like(dkt_acc)
            dvt_acc[...] = jnp.zeros_like(dvt_acc)
            cs_acc[...] = jnp.zeros_like(cs_acc)
            for cp in copies:
                cp.wait()

        qs = (qa_ref[...], qb_ref[...])
        dos = (doa_ref[...], dob_ref[...])
        first = lax.broadcasted_iota(jnp.int32, (LANES, t), 0) < HEAD_DIM
        zero = jnp.zeros((LANES, t), BF16)
        qts = (jnp.where(first, qat_ref[...], zero), jnp.where(first, zero, qbt_ref[...]))
        dots = (jnp.where(first, doat_ref[...], zero), jnp.where(first, zero, dobt_ref[...]))
        ones = jnp.ones((8, t), BF16)
        kts, ks, vts = (kat_v, kbt_v), (ka_v, kb_v), (vat_v, vbt_v)

        def step(kb, carry, masked):
            off = pl.multiple_of(kb * w, w)
            cols = pl.ds(off, w)
            out = []
            for hh in range(2):
                s = _dot(qs[hh], kts[hh][:, cols])
                if masked:
                    s = jnp.where(_causal(i * t, off, (t, w)), s, NEG_BIG)
                p = jnp.exp(s)
                ds = (p * _dot(dos[hh], vts[hh][:, cols])).astype(BF16)
                out.append(carry[hh] + _dot(ds, ks[hh][cols, :]))
                dvt_acc[:, cols] += _dot(dots[hh], p.astype(BF16))
                dkt_acc[:, cols] += _dot(qts[hh], ds)
                cs_acc[8 * hh:8 * hh + 8, cols] += _dot(ones, ds)
            return tuple(out)

        diag = (i * t) // w
        carry = lax.fori_loop(0, diag, lambda kb, c: step(kb, c, False), (jnp.zeros((t, LANES), F32),) * 2)
        acc = step(diag, carry, True)
        dq_ref[...] = (_pair_select(acc[0], acc[1], (t, LANES)) * (HEAD_DIM ** -0.5)).astype(BF16)
        rs_ref[...] = _pair_select(acc[0][:, HEAD_DIM:HEAD_DIM + 1], acc[1][:, 0:1], (t, LANES))

        @pl.when(i == n_q - 1)
        def _():
            dkt_ref[...] = dkt_acc[...].astype(BF16)
            dvt_ref[...] = dvt_acc[...].astype(BF16)
            cs_ref[0] = cs_acc[...]

    qblk = pl.BlockSpec((t, LANES), lambda j, i: (i, j))
    qblk_t = pl.BlockSpec((LANES, t), lambda j, i: (j, i))
    whole_t = pl.BlockSpec((LANES, s_len), lambda j, i: (j, 0))
    ds_ = jax.ShapeDtypeStruct((d, s_len), BF16)
    return pl.pallas_call(
        body, name="flash_bwd", grid=(n_pair, n_q),
        in_specs=[qblk, qblk, qblk_t, qblk_t, ANY, ANY, ANY, ANY, ANY, ANY, qblk, qblk, qblk_t, qblk_t],
        out_specs=[qblk, whole_t, whole_t, qblk, pl.BlockSpec((1, 16, s_len), lambda j, i: (j, 0, 0))],
        out_shape=[jax.ShapeDtypeStruct((s_len, d), BF16), ds_, ds_, jax.ShapeDtypeStruct((s_len, d), F32),
                   jax.ShapeDtypeStruct((n_pair, 16, s_len), F32)],
        scratch_shapes=[pltpu.VMEM((LANES, s_len), BF16), pltpu.VMEM((LANES, s_len), BF16),
                        pltpu.VMEM((s_len, LANES), BF16), pltpu.VMEM((s_len, LANES), BF16),
                        pltpu.VMEM((LANES, s_len), BF16), pltpu.VMEM((LANES, s_len), BF16),
                        pltpu.VMEM((LANES, s_len), F32), pltpu.VMEM((LANES, s_len), F32),
                        pltpu.VMEM((16, s_len), F32), pltpu.SemaphoreType.DMA((6,))],
        compiler_params=_params(2),
    )(qa, qb, qat, qbt, kat, kbt, ka, kb_, vat, vbt, doa, dob, doat, dobt)


def _wgrad_t(at, b, name):
    k, s_len = at.shape
    n = b.shape[1]
    tn, tk, ts = min(n, 1024), min(k, 1024), 512

    def body(a_ref, b_ref, o_ref):
        @pl.when(pl.program_id(2) == 0)
        def _():
            o_ref[...] = jnp.zeros_like(o_ref)
        o_ref[...] += _dot(a_ref[...].astype(BF16), b_ref[...].astype(BF16))

    return pl.pallas_call(
        body, name=name, grid=(k // tk, n // tn, s_len // ts),
        in_specs=[pl.BlockSpec((tk, ts), lambda a, b_, c: (a, c)), pl.BlockSpec((ts, tn), lambda a, b_, c: (c, b_))],
        out_specs=pl.BlockSpec((tk, tn), lambda a, b_, c: (a, b_)),
        out_shape=jax.ShapeDtypeStruct((k, n), F32),
        compiler_params=_params(3),
    )(at, b)


def _oproj_bwd(dh, o, wo, seg, sel_q, tm):
    s_len, d = dh.shape
    sq_a, sq_b, _, _ = sel_q

    def body(dh_ref, o_ref, wo_hbm, seg_ref, sqa_ref, sqb_ref, doa_ref, dob_ref, doat_ref, dobt_ref, wo_v, sem):
        _load_once([(wo_hbm, wo_v)], sem)
        do = _dot_nt(dh_ref[...].astype(BF16), wo_v[...])
        parts = _parts(-_dot3_lhs(do * o_ref[...], seg_ref[...]))
        even = _even_head_lanes((tm, d), 1)
        doa = jnp.where(even, do, _dot(parts, sqa_ref[...]))
        dob = jnp.where(even, _dot(parts, sqb_ref[...]), do)
        doa_ref[...] = doa.astype(BF16)
        dob_ref[...] = dob.astype(BF16)
        doat_ref[...] = doa.T.astype(BF16)
        dobt_ref[...] = dob.T.astype(BF16)

    row = lambda i: (i, 0)
    const2 = lambda i: (0, 0)
    rs, cs = pl.BlockSpec((tm, d), row), pl.BlockSpec((d, tm), lambda i: (0, i))
    sel = pl.BlockSpec((N_SPARE * LANES, d), const2)
    sd = jax.ShapeDtypeStruct((s_len, d), BF16)
    ds_ = jax.ShapeDtypeStruct((d, s_len), BF16)
    return pl.pallas_call(
        body, name="oproj_bwd", grid=(s_len // tm,),
        in_specs=[rs, rs, ANY, pl.BlockSpec((d, LANES), const2), sel, sel],
        out_specs=[rs, rs, cs, cs], out_shape=[sd, sd, ds_, ds_],
        scratch_shapes=[pltpu.VMEM((d, d), BF16), pltpu.SemaphoreType.DMA((1,))],
        compiler_params=_params(1),
    )(dh, o, wo, seg, sq_a, sq_b)


def _oproj_fwd(h, o, wo, tm):
    s_len, d = h.shape

    def body(h_ref, o_ref, wo_hbm, hout_ref, wo_v, sem):
        _load_once([(wo_hbm, wo_v)], sem)
        hout_ref[...] = h_ref[...] + _dot(o_ref[...].astype(BF16), wo_v[...])

    row = lambda i: (i, 0)
    return pl.pallas_call(
        body, name="oproj_fwd", grid=(s_len // tm,),
        in_specs=[pl.BlockSpec((tm, d), row), pl.BlockSpec((tm, d), row), ANY],
        out_specs=pl.BlockSpec((tm, d), row),
        out_shape=jax.ShapeDtypeStruct((s_len, d), F32),
        scratch_shapes=[pltpu.VMEM((d, d), BF16), pltpu.SemaphoreType.DMA((1,))],
        compiler_params=_params(1),
    )(h, o, wo)


def _forget_bwd(dcum, z, tm):
    s_len = dcum.shape[0]
    n_blk = s_len // tm

    def body(dc_ref, z_ref, dfl_ref, gb_ref, total):
        i = pl.program_id(0)

        @pl.when(i == 0)
        def _():
            total[...] = jnp.zeros_like(total)
            gb_ref[...] = jnp.zeros_like(gb_ref)

        upper = (lax.broadcasted_iota(jnp.int32, (tm, tm), 0) <= lax.broadcasted_iota(jnp.int32, (tm, tm), 1))
        suffix = _dot3_rhs(jnp.where(upper, 1.0, 0.0).astype(BF16), dc_ref[...]) + total[0:1, :]
        total[...] = jnp.broadcast_to(suffix[0:1, :], total.shape)
        dfl = suffix * _sigmoid(-z_ref[...])
        dfl_ref[...] = dfl
        gb_ref[...] += jnp.sum(dfl, axis=0, keepdims=True)

    rev = lambda i: (n_blk - 1 - i, 0)
    return pl.pallas_call(
        body, name="forget_bwd", grid=(n_blk,),
        in_specs=[pl.BlockSpec((tm, LANES), rev), pl.BlockSpec((tm, LANES), rev)],
        out_specs=[pl.BlockSpec((tm, LANES), rev), pl.BlockSpec((1, LANES), lambda i: (0, 0))],
        out_shape=[jax.ShapeDtypeStruct((s_len, LANES), F32), jax.ShapeDtypeStruct((1, LANES), F32)],
        scratch_shapes=[pltpu.VMEM((8, LANES), F32)],
        compiler_params=_params(1),
    )(dcum, z)


def _fox_proj_bwd(h, dh, dq, dkt, dvt, dfl, g_norm, wq, wk, wv, wf, tm):
    s_len, d = h.shape

    def body(h_ref, dh_ref, dq_ref, dkt_ref, dvt_ref, dfl_ref, g_ref, wq_hbm, wk_hbm, wv_hbm, wf_ref,
             dhin_ref, dflb_ref, gn_ref, wq_v, wk_v, wv_v, sem):
        _load_once([(wq_hbm, wq_v), (wk_hbm, wk_v), (wv_hbm, wv_v)], sem)

        @pl.when(pl.program_id(0) == 0)
        def _():
            gn_ref[...] = jnp.zeros_like(gn_ref)

        g = g_ref[...]
        xhat, r, _ = _rms_fwd(h_ref[...], g)
        dflb = dfl_ref[...].astype(BF16)
        dflb_ref[...] = dflb
        from_kv = _dot(wk_v[...], dkt_ref[...]) + _dot(wv_v[...], dvt_ref[...])
        dhn = _dot_nt(dq_ref[...], wq_v[...]) + _dot_nt(dflb, wf_ref[...]) + from_kv.T
        dx, gg = _rms_bwd(dhn, xhat, r, g)
        gn_ref[...] += jnp.sum(gg, axis=0, keepdims=True)
        dhin_ref[...] = dh_ref[...] + dx

    row = lambda i: (i, 0)
    const2 = lambda i: (0, 0)
    rs = pl.BlockSpec((tm, d), row)
    cs = pl.BlockSpec((d, tm), lambda i: (0, i))
    return pl.pallas_call(
        body, name="fox_proj_bwd", grid=(s_len // tm,),
        in_specs=[rs, rs, rs, cs, cs, pl.BlockSpec((tm, LANES), row), pl.BlockSpec((1, d), const2), ANY, ANY, ANY,
                  pl.BlockSpec((d, LANES), const2)],
        out_specs=[rs, pl.BlockSpec((tm, LANES), row), pl.BlockSpec((1, d), const2)],
        out_shape=[jax.ShapeDtypeStruct((s_len, d), F32), jax.ShapeDtypeStruct((s_len, LANES), BF16),
                   jax.ShapeDtypeStruct((1, d), F32)],
        scratch_shapes=[pltpu.VMEM((d, d), BF16), pltpu.VMEM((d, d), BF16), pltpu.VMEM((d, d), BF16),
                        pltpu.SemaphoreType.DMA((3,))],
        compiler_params=_params(1),
    )(h, dh, dq, dkt, dvt, dfl, g_norm, wq, wk, wv, wf)


def _loss_head(h, target, g_final, tm):
    s_len, d = h.shape
    n_blk = s_len // tm

    def body(h_ref, t_ref, g_ref, dh_ref, loss_ref, gg_ref, sq):
        i = pl.program_id(0)

        @pl.when(i == 0)
        def _():
            sq[...] = jnp.zeros_like(sq)
            gg_ref[...] = jnp.zeros_like(gg_ref)

        g = g_ref[...]
        xhat, r, y = _rms_fwd(h_ref[...], g)
        err = y - t_ref[...]
        sq[...] += jnp.sum(err * err, axis=0, keepdims=True)
        dx, gg = _rms_bwd(err * (1.0 / d), xhat, r, g)
        gg_ref[...] += jnp.sum(gg, axis=0, keepdims=True)
        dh_ref[...] = dx

        @pl.when(i == n_blk - 1)
        def _():
            loss_ref[...] = jnp.broadcast_to(jnp.sum(sq[...], axis=1, keepdims=True) * (0.5 / d), loss_ref.shape)

    row = lambda i: (i, 0)
    const2 = lambda i: (0, 0)
    return pl.pallas_call(
        body, name="loss_head", grid=(n_blk,),
        in_specs=[pl.BlockSpec((tm, d), row), pl.BlockSpec((tm, d), row), pl.BlockSpec((1, d), const2)],
        out_specs=[pl.BlockSpec((tm, d), row), pl.BlockSpec((1, LANES), const2), pl.BlockSpec((1, d), const2)],
        out_shape=[jax.ShapeDtypeStruct((s_len, d), F32), jax.ShapeDtypeStruct((1, LANES), F32),
                   jax.ShapeDtypeStruct((1, d), F32)],
        scratch_shapes=[pltpu.VMEM((1, d), F32)],
        compiler_params=_params(1),
    )(h, target, g_final)


def _wgrad(x, dy, n_piece, name):
    s_len, k = x.shape
    n = dy.shape[1]
    tn = min(n // n_piece, 1024)
    tk = min(k, 1024)
    ts = 512
    per_piece = (n // n_piece) // tn

    def body(x_ref, dy_ref, o_ref):
        @pl.when(pl.program_id(2) == 0)
        def _():
            o_ref[...] = jnp.zeros_like(o_ref)
        o_ref[0] += _dot_tn(x_ref[...].astype(BF16), dy_ref[...].astype(BF16))

    return pl.pallas_call(
        body, name=name, grid=(k // tk, n // tn, s_len // ts),
        in_specs=[pl.BlockSpec((ts, tk), lambda a, b, c: (c, a)), pl.BlockSpec((ts, tn), lambda a, b, c: (c, b))],
        out_specs=pl.BlockSpec((1, tk, tn), lambda a, b, c: (b // per_piece, a, b % per_piece)),
        out_shape=jax.ShapeDtypeStruct((n_piece, k, n // n_piece), F32),
        compiler_params=_params(3),
    )(x, dy)


def _pair_sum(g, recv, core, name):
    n_piece, rows, c = g.shape
    half = rows // 2
    tr = min(half, 512)
    nb = half // tr

    def body(core_ref, g_ref, r_ref, o_ref):
        o_ref[...] = g_ref[...] + r_ref[...]

    blk = pl.BlockSpec((1, tr, c), lambda p, i, core_ref: (p, i, 0))
    return pl.pallas_call(
        body, name=name, out_shape=jax.ShapeDtypeStruct((n_piece, half, c), F32),
        grid_spec=pltpu.PrefetchScalarGridSpec(
            num_scalar_prefetch=1, grid=(n_piece, nb),
            in_specs=[pl.BlockSpec((1, tr, c), lambda p, i, core_ref: (p, core_ref[0] * nb + i, 0)), blk],
            out_specs=blk),
        compiler_params=_params(2),
    )(core, g, recv)


def _chip_sum(halves, recv, chip, name):
    _, h, c = halves.shape
    tr = min(h, 512)

    def body(chip_ref, own_ref, r_ref, o_ref):
        o_ref[...] = ((own_ref[0] + r_ref[0]) + r_ref[1]) + r_ref[2]

    return pl.pallas_call(
        body, name=name, out_shape=jax.ShapeDtypeStruct((h, c), F32),
        grid_spec=pltpu.PrefetchScalarGridSpec(
            num_scalar_prefetch=1, grid=(h // tr,),
            in_specs=[pl.BlockSpec((1, tr, c), lambda i, chip_ref: (chip_ref[0], i, 0)),
                      pl.BlockSpec((3, tr, c), lambda i, chip_ref: (0, i, 0))],
            out_specs=pl.BlockSpec((tr, c), lambda i, chip_ref: (i, 0))),
        compiler_params=_params(1),
    )(chip, halves, recv)


def _adamw_math(w, m, v, g):
    m_new = ADAM_B1 * m + (1.0 - ADAM_B1) * g
    v_new = ADAM_B2 * v + (1.0 - ADAM_B2) * (g * g)
    m_hat = m_new / (1.0 - ADAM_B1 ** ADAM_STEP)
    v_hat = v_new / (1.0 - ADAM_B2 ** ADAM_STEP)
    return -ADAM_LR * (m_hat / (jnp.sqrt(v_hat) + ADAM_EPS) + ADAM_WD * w), m_new, v_new


def _adamw(w, m, v, g, name):
    rows, c = w.shape
    tr = min(rows, 256)

    def body(w_ref, m_ref, v_ref, g_ref, d_ref, mo_ref, vo_ref):
        d_ref[...], mo_ref[...], vo_ref[...] = _adamw_math(w_ref[...], m_ref[...], v_ref[...], g_ref[...])

    spec = pl.BlockSpec((tr, c), lambda i: (i, 0))
    shape = jax.ShapeDtypeStruct((rows, c), F32)
    return pl.pallas_call(
        body, name=name, grid=(rows // tr,),
        in_specs=[spec] * 4, out_specs=[spec] * 3, out_shape=[shape] * 3,
        compiler_params=_params(1),
    )(w, m, v, g)


def _adamw_halves(w, m, v, g_own, g_other, core, name):
    rows, c = w.shape
    half = rows // 2
    tr = min(half, 256)
    nb = half // tr

    def body(core_ref, w_ref, m_ref, v_ref, own_ref, other_ref, g_ref, d_ref, mo_ref, vo_ref):
        mine = (pl.program_id(0) // nb) == core_ref[0]
        g = jnp.where(mine, own_ref[...], other_ref[...])
        g_ref[...] = g
        d_ref[...], mo_ref[...], vo_ref[...] = _adamw_math(w_ref[...], m_ref[...], v_ref[...], g)

    spec = pl.BlockSpec((tr, c), lambda i, core_ref: (i, 0))
    own = pl.BlockSpec((tr, c), lambda i, core_ref: (jnp.clip(i - core_ref[0] * nb, 0, nb - 1), 0))
    other = pl.BlockSpec((tr, c), lambda i, core_ref: (jnp.clip(i - (1 - core_ref[0]) * nb, 0, nb - 1), 0))
    shape = jax.ShapeDtypeStruct((rows, c), F32)
    return pl.pallas_call(
        body, name=name, out_shape=[shape] * 4,
        grid_spec=pltpu.PrefetchScalarGridSpec(
            num_scalar_prefetch=1, grid=(rows // tr,),
            in_specs=[spec, spec, spec, own, other], out_specs=[spec] * 4),
        compiler_params=_params(1),
    )(core, w, m, v, g_own, g_other)


def _place():
    x, y, c = lax.axis_index("x"), lax.axis_index("y"), lax.axis_index("c")
    chips = [(1 - x, y), (x, 1 - y), (1 - x, 1 - y)]
    return x, y, c, chips


def _all_gather_chips(shards):
    n = len(shards)

    def body(*refs):
        ins, outs = refs[:n], refs[n:2 * n]
        send_sems, recv_sems, local_sems = refs[2 * n:]
        x, y, c, chips = _place()
        mine = 2 * x + y
        local = [pltpu.make_async_copy(ins[k], outs[k].at[mine], local_sems.at[k]) for k in range(n)]
        for cp in local:
            cp.start()
        sends = []
        for k in range(n):
            for j, (tx, ty) in enumerate(chips):
                sends.append(pltpu.make_async_remote_copy(
                    src_ref=ins[k], dst_ref=outs[k].at[mine], send_sem=send_sems.at[k, j], recv_sem=recv_sems.at[k, j],
                    device_id=(tx, ty, c), device_id_type=MESH))
        for cp in sends:
            cp.start()
        for k in range(n):
            for j, (tx, ty) in enumerate(chips):
                pltpu.make_async_remote_copy(
                    src_ref=ins[k], dst_ref=outs[k].at[2 * tx + ty], send_sem=send_sems.at[k, j],
                    recv_sem=recv_sems.at[k, j], device_id=(tx, ty, c), device_id_type=MESH).wait()
        for cp in local:
            cp.wait()

    return pl.pallas_call(
        body, name="weights_all_gather",
        in_specs=[ANY] * n, out_specs=[ANY] * n,
        out_shape=[jax.ShapeDtypeStruct((4,) + s.shape, s.dtype) for s in shards],
        scratch_shapes=[pltpu.SemaphoreType.DMA((n, 3)), pltpu.SemaphoreType.DMA((n, 3)), pltpu.SemaphoreType.DMA((n,))],
    )(*shards)


def _pair_exchange(grads):
    n = len(grads)

    def body(*refs):
        ins, outs = refs[:n], refs[n:2 * n]
        send_sems, recv_sems = refs[2 * n:]
        x, y, c, _ = _place()
        copies = []
        for k in range(n):
            half = grads[k].shape[1] // 2
            other = ins[k].at[:, pl.ds(pl.multiple_of((1 - c) * half, 8), half), :]
            copies.append(pltpu.make_async_remote_copy(
                src_ref=other, dst_ref=outs[k], send_sem=send_sems.at[k], recv_sem=recv_sems.at[k],
                device_id=(x, y, 1 - c), device_id_type=MESH))
        for cp in copies:
            cp.start()
        for cp in copies:
            cp.wait()

    return pl.pallas_call(
        body, name="grads_pair_exchange",
        in_specs=[ANY] * n, out_specs=[ANY] * n,
        out_shape=[jax.ShapeDtypeStruct((4, g.shape[1] // 2, g.shape[2]), F32) for g in grads],
        scratch_shapes=[pltpu.SemaphoreType.DMA((n,)), pltpu.SemaphoreType.DMA((n,))],
    )(*grads)


def _chip_scatter(halves):
    n = len(halves)

    def body(*refs):
        ins, outs = refs[:n], refs[n:2 * n]
        send_sems, recv_sems = refs[2 * n:]
        x, y, c, chips = _place()
        sends = []
        for k in range(n):
            for j, (tx, ty) in enumerate(chips):
                sends.append(pltpu.make_async_remote_copy(
                    src_ref=ins[k].at[2 * tx + ty], dst_ref=outs[k].at[j], send_sem=send_sems.at[k, j],
                    recv_sem=recv_sems.at[k, j], device_id=(tx, ty, c), device_id_type=MESH))
        for cp in sends:
            cp.start()
        for cp in sends:
            cp.wait()

    return pl.pallas_call(
        body, name="grads_chip_scatter",
        in_specs=[ANY] * n, out_specs=[ANY] * n,
        out_shape=[jax.ShapeDtypeStruct((3,) + hv.shape[1:], F32) for hv in halves],
        scratch_shapes=[pltpu.SemaphoreType.DMA((n, 3)), pltpu.SemaphoreType.DMA((n, 3))],
    )(*halves)


def _pair_share(finals):
    n = len(finals)

    def body(*refs):
        ins, outs = refs[:n], refs[n:2 * n]
        send_sems, recv_sems = refs[2 * n:]
        x, y, c, _ = _place()
        copies = [pltpu.make_async_remote_copy(
            src_ref=ins[k], dst_ref=outs[k], send_sem=send_sems.at[k], recv_sem=recv_sems.at[k],
            device_id=(x, y, 1 - c), device_id_type=MESH) for k in range(n)]
        for cp in copies:
            cp.start()
        for cp in copies:
            cp.wait()

    return pl.pallas_call(
        body, name="grads_pair_share",
        in_specs=[ANY] * n, out_specs=[ANY] * n,
        out_shape=[jax.ShapeDtypeStruct(fv.shape, F32) for fv in finals],
        scratch_shapes=[pltpu.SemaphoreType.DMA((n,)), pltpu.SemaphoreType.DMA((n,))],
    )(*finals)


def _small_all_reduce(buf):
    rows, c_ = buf.shape

    def body(in_ref, out_ref, pair_buf, slots, send_sems, recv_sems):
        x, y, c, chips = _place()
        mine = 2 * x + y
        pair = pltpu.make_async_remote_copy(
            src_ref=in_ref, dst_ref=pair_buf, send_sem=send_sems.at[0], recv_sem=recv_sems.at[0],
            device_id=(x, y, 1 - c), device_id_type=MESH)
        pair.start()
        pair.wait()
        slots[mine] = in_ref[...] + pair_buf[...]
        sends = [pltpu.make_async_remote_copy(
            src_ref=slots.at[mine], dst_ref=slots.at[mine], send_sem=send_sems.at[1 + j], recv_sem=recv_sems.at[1 + j],
            device_id=(tx, ty, c), device_id_type=MESH) for j, (tx, ty) in enumerate(chips)]
        for cp in sends:
            cp.start()
        for j, (tx, ty) in enumerate(chips):
            pltpu.make_async_remote_copy(
                src_ref=slots.at[mine], dst_ref=slots.at[2 * tx + ty], send_sem=send_sems.at[1 + j],
                recv_sem=recv_sems.at[1 + j], device_id=(tx, ty, c), device_id_type=MESH).wait()
        out_ref[...] = ((slots[0] + slots[1]) + slots[2]) + slots[3]

    vm = pl.BlockSpec(memory_space=pltpu.VMEM)
    return pl.pallas_call(
        body, name="small_all_reduce", in_specs=[vm], out_specs=vm,
        out_shape=jax.ShapeDtypeStruct((rows, c_), F32),
        scratch_shapes=[pltpu.VMEM((rows, c_), F32), pltpu.VMEM((4, rows, c_), F32),
                        pltpu.SemaphoreType.DMA((4,)), pltpu.SemaphoreType.DMA((4,))],
        compiler_params=pltpu.CompilerParams(vmem_limit_bytes=VMEM_LIMIT_V7X),
    )(buf)


def _reduce_scatter(grads):
    core = lax.axis_index("c").astype(jnp.int32).reshape(1)
    chip = (2 * lax.axis_index("x") + lax.axis_index("y")).astype(jnp.int32).reshape(1)
    recv = _pair_exchange(grads)
    halves = [_pair_sum(g, r, core, f"pair_sum_{k}") for k, (g, r) in enumerate(zip(grads, recv))]
    recv = _chip_scatter(halves)
    finals = [_chip_sum(hv, r, chip, f"chip_sum_{k}") for k, (hv, r) in enumerate(zip(halves, recv))]
    return list(zip(finals, _pair_share(finals))), core


PACK_COLS = 1024


def _pack(arrays):
    flat = jnp.concatenate([a.reshape(-1).astype(F32) for a in arrays])
    rows = -(-flat.shape[0] // PACK_COLS)
    rows = -(-rows // 8) * 8
    return jnp.pad(flat, (0, rows * PACK_COLS - flat.shape[0])).reshape(rows, PACK_COLS)


def _unpack(buf, shapes):
    flat = buf.reshape(-1)
    out, at = [], 0
    for shp in shapes:
        size = math.prod(shp)
        out.append(flat[at:at + size].reshape(shp))
        at += size
    return out


def kernel(x, mix_norm_g, ffn_norm_g, gm_w_in, gm_ln_g, gm_ln_b, gm_w_s, gm_b_s, gm_w_out, fox_w_qkvf, fox_b_f, fox_w_o, ffn_w_gate, ffn_w_up, ffn_conv_w, ffn_conv_b, ffn_w_down, final_norm_g, loss_target, m_mix_norm_g, m_ffn_norm_g, m_gm_w_in, m_gm_ln_g, m_gm_ln_b, m_gm_w_s, m_gm_b_s, m_gm_w_out, m_fox_w_qkvf, m_fox_b_f, m_fox_w_o, m_ffn_w_gate, m_ffn_w_up, m_ffn_conv_w, m_ffn_conv_b, m_ffn_w_down, m_final_norm_g, v_mix_norm_g, v_ffn_norm_g, v_gm_w_in, v_gm_ln_g, v_gm_ln_b, v_gm_w_s, v_gm_b_s, v_gm_w_out, v_fox_w_qkvf, v_fox_b_f, v_fox_w_o, v_ffn_w_gate, v_ffn_w_up, v_ffn_conv_w, v_ffn_conv_b, v_ffn_w_down, v_final_norm_g):
    _, s_len, d = x.shape
    e = gm_ln_g.shape[1]
    f = ffn_conv_b.shape[1]
    n_head = fox_b_f.shape[1]
    n_pair = n_head // 2
    gd = e // GM_GROUPS
    qkvf_cols = fox_w_qkvf.shape[2]
    assert d == n_head * HEAD_DIM and d % (2 * LANES) == 0 and s_len % 512 == 0 and gd % LANES == 0
    assert gm_w_s.shape[2] == CHUNK and 4 * qkvf_cols == 3 * d + n_head
    tm = 256
    h0 = x[0]
    target = loss_target[0]

    gathered = _all_gather_chips([
        gm_w_in[0].astype(BF16), gm_w_out[0].astype(BF16), fox_w_qkvf[0].astype(BF16), fox_w_o[0].astype(BF16),
        ffn_w_gate.astype(BF16), ffn_w_up.astype(BF16), ffn_w_down.astype(BF16), ffn_conv_w])
    w_in, w_out4, qkvf4, wo4, wg_all, wu_all, wd_all, cw4 = gathered
    w_out = w_out4.reshape(e, d)
    qkvf = jnp.transpose(qkvf4, (1, 0, 2)).reshape(d, 4 * qkvf_cols)
    wq, wk, wv = qkvf[:, :d], qkvf[:, d:2 * d], qkvf[:, 2 * d:3 * d]
    wf = jnp.pad(qkvf[:, 3 * d:], ((0, 0), (0, LANES - n_head)))
    wo = wo4.reshape(d, d)
    conv_w_full = jnp.transpose(cw4, (1, 2, 0, 3)).reshape(2, 3, f)
    conv_w8 = jnp.pad(conv_w_full, ((0, 0), (0, 5), (0, 0)))
    bf_pad = jnp.pad(fox_b_f, ((0, 0), (0, LANES - n_head)))

    tril = jnp.tril(jnp.ones((CHUNK, CHUNK), bool))
    wc = jnp.where(tril[None], gm_w_s[0], 0.0).astype(BF16)
    wct = jnp.transpose(wc, (0, 2, 1))
    bias = jnp.repeat(gm_b_s[0].T, gd, axis=1)
    seg_groups = (jnp.arange(e)[:, None] // gd == jnp.arange(LANES)[None, :]).astype(BF16)
    seg_heads = (jnp.arange(d)[:, None] // HEAD_DIM == jnp.arange(LANES)[None, :]).astype(BF16)
    sel_q = _spare_selectors(d, key_side=False)
    sel_k = _spare_selectors(d, key_side=True)

    h1, a0, hn0, gated0 = _gmlp_fwd(h0, mix_norm_g[0:1], w_in, gm_ln_g, gm_ln_b, wc, bias, w_out, tm)
    h2, fa0, fup0, fhn0, fhid0 = _ffn_fwd(h1, ffn_norm_g[0:1], wg_all, wu_all, wd_all, 0, conv_w8[0], ffn_conv_b[0:1], tm)
    (hn1, qa, qb, ka, kb_, kat, kbt, va, vb, vat, vbt, z_f) = _fox_proj_fwd(
        h2, mix_norm_g[1:2], wq, wk, wv, wf, bf_pad, sel_q, sel_k, tm)
    o, qa2, qb2, qat2, qbt2 = _flash_fwd(qa, qb, kat, kbt, va, vb)
    h3 = _oproj_fwd(h2, o, wo, tm)
    h4, fa1, fup1, fhn1, fhid1 = _ffn_fwd(h3, ffn_norm_g[1:2], wg_all, wu_all, wd_all, 1, conv_w8[1], ffn_conv_b[1:2], tm)

    dh4, loss_part, g_final = _loss_head(h4, target, final_norm_g.reshape(1, d), tm)
    dh3, da1, dup1, gcw1, gcb1, gfn1 = _ffn_bwd(h3, dh4, fa1, fup1, ffn_norm_g[1:2], wg_all, wu_all, wd_all, 1,
                                                conv_w8[1], ffn_conv_b[1:2], tm)
    g_gate1 = _wgrad(fhn1, da1, 4, "wgrad_gate_1")
    g_up1 = _wgrad(fhn1, dup1, 4, "wgrad_up_1")
    g_down1 = _wgrad(fhid1, dh4, 1, "wgrad_down_1").reshape(4, f // 4, d)

    doa, dob, doat, dobt = _oproj_bwd(dh3, o, wo, seg_heads, sel_q, tm)
    g_wo = _wgrad(o, dh3, 1, "wgrad_wo").reshape(4, d // 4, d)
    dq, dkt, dvt, row_sums, col_sums = _flash_bwd(qa2, qb2, qat2, qbt2, kat, kbt, ka, kb_, vat, vbt,
                                                  doa, dob, doat, dobt)
    col_sums = col_sums[:, ::8, :].reshape(n_head, s_len).T
    dcum = jnp.pad(row_sums[:, ::HEAD_DIM] - col_sums, ((0, 0), (0, LANES - n_head)))
    dfl, g_bf = _forget_bwd(dcum, z_f, tm)
    dh2, dflb, gmn1 = _fox_proj_bwd(h2, dh3, dq, dkt, dvt, dfl, mix_norm_g[1:2], wq, wk, wv, wf, tm)
    g_q = _wgrad(hn1, dq, 1, "wgrad_q")[0]
    g_k = _wgrad_t(dkt, hn1, "wgrad_k").T
    g_v = _wgrad_t(dvt, hn1, "wgrad_v").T
    g_f = _wgrad(hn1, dflb, 1, "wgrad_f")[0][:, :n_head]
    g_qkvf = jnp.concatenate([g_q, g_k, g_v, g_f], axis=1).reshape(d, 4, qkvf_cols).transpose(1, 0, 2)

    dh1, da0f, dup0, gcw0, gcb0, gfn0 = _ffn_bwd(h1, dh2, fa0, fup0, ffn_norm_g[0:1], wg_all, wu_all, wd_all, 0,
                                                 conv_w8[0], ffn_conv_b[0:1], tm)
    g_gate0 = _wgrad(fhn0, da0f, 4, "wgrad_gate_0")
    g_up0 = _wgrad(fhn0, dup0, 4, "wgrad_up_0")
    g_down0 = _wgrad(fhid0, dh2, 1, "wgrad_down_0").reshape(4, f // 4, d)

    dh0, da0, g_ws, g_bs_t, g_lng, g_lnb, gmn0 = _gmlp_bwd(
        h0, dh1, a0, mix_norm_g[0:1], w_in, gm_ln_g, gm_ln_b, wc, wct, bias, w_out, seg_groups, tm)
    g_win = _wgrad(hn0, da0, 4, "wgrad_gm_in")
    g_wout = _wgrad(gated0, dh1, 1, "wgrad_gm_out").reshape(4, e // 4, d)

    big, core = _reduce_scatter([g_win, g_wout, g_qkvf, g_wo, g_gate0, g_gate1, g_up0, g_up1, g_down0, g_down1])
    r_win, r_wout, r_qkvf, r_wo, r_gate0, r_gate1, r_up0, r_up1, r_down0, r_down1 = big

    small = [jnp.concatenate([gmn0, gmn1]), jnp.concatenate([gfn0, gfn1]), g_lng, g_lnb, g_ws[None],
             g_bs_t[:, :GM_GROUPS].T[None], g_bf[:, :n_head], jnp.stack([gcw0[:3], gcw1[:3]]),
             jnp.concatenate([gcb0, gcb1]), g_final.reshape(d), loss_part[0, :1]]
    small_shapes = [a.shape for a in small]
    reduced = _unpack(_small_all_reduce(_pack(small)), small_shapes)
    (r_mix, r_ffn, r_lng, r_lnb, r_ws, r_bs, r_bf, r_cw_full, r_cb, r_final, r_loss) = reduced
    chip = 2 * lax.axis_index("x") + lax.axis_index("y")
    r_cw = lax.dynamic_slice_in_dim(r_cw_full, chip * (f // 4), f // 4, axis=2)

    def update_big(name, w, m, v, per_layer):
        parts = [_adamw_halves(w[l], m[l], v[l], own, other, core, f"adamw_{name}_{l}")
                 for l, (own, other) in enumerate(per_layer)]
        return tuple(jnp.stack([p[i] for p in parts]) for i in range(4))

    res = {}
    res["gm_w_in"] = update_big("gm_w_in", gm_w_in, m_gm_w_in, v_gm_w_in, [r_win])
    res["gm_w_out"] = update_big("gm_w_out", gm_w_out, m_gm_w_out, v_gm_w_out, [r_wout])
    res["fox_w_qkvf"] = update_big("fox_w_qkvf", fox_w_qkvf, m_fox_w_qkvf, v_fox_w_qkvf, [r_qkvf])
    res["fox_w_o"] = update_big("fox_w_o", fox_w_o, m_fox_w_o, v_fox_w_o, [r_wo])
    res["ffn_w_gate"] = update_big("ffn_w_gate", ffn_w_gate, m_ffn_w_gate, v_ffn_w_gate, [r_gate0, r_gate1])
    res["ffn_w_up"] = update_big("ffn_w_up", ffn_w_up, m_ffn_w_up, v_ffn_w_up, [r_up0, r_up1])
    res["ffn_w_down"] = update_big("ffn_w_down", ffn_w_down, m_ffn_w_down, v_ffn_w_down, [r_down0, r_down1])

    small_names = ["mix_norm_g", "ffn_norm_g", "gm_ln_g", "gm_ln_b", "gm_w_s", "gm_b_s", "fox_b_f", "ffn_conv_w",
                   "ffn_conv_b", "final_norm_g"]
    small_w = [mix_norm_g, ffn_norm_g, gm_ln_g, gm_ln_b, gm_w_s, gm_b_s, fox_b_f, ffn_conv_w, ffn_conv_b, final_norm_g]
    small_m = [m_mix_norm_g, m_ffn_norm_g, m_gm_ln_g, m_gm_ln_b, m_gm_w_s, m_gm_b_s, m_fox_b_f, m_ffn_conv_w,
               m_ffn_conv_b, m_final_norm_g]
    small_v = [v_mix_norm_g, v_ffn_norm_g, v_gm_ln_g, v_gm_ln_b, v_gm_w_s, v_gm_b_s, v_fox_b_f, v_ffn_conv_w,
               v_ffn_conv_b, v_final_norm_g]
    small_g = [r_mix, r_ffn, r_lng, r_lnb, r_ws, r_bs, r_bf, r_cw, r_cb, r_final]
    shapes = [w.shape for w in small_w]
    small_g = [g.reshape(s) for g, s in zip(small_g, shapes)]
    dlt, mn, vn = _adamw(_pack(small_w), _pack(small_m), _pack(small_v), _pack(small_g), "adamw_small")
    for name, g, dl_, m_, v_ in zip(small_names, small_g, _unpack(dlt, shapes), _unpack(mn, shapes), _unpack(vn, shapes)):
        res[name] = (g, dl_, m_, v_)

    order = ["mix_norm_g", "ffn_norm_g", "gm_w_in", "gm_ln_g", "gm_ln_b", "gm_w_s", "gm_b_s", "gm_w_out", "fox_w_qkvf",
             "fox_b_f", "fox_w_o", "ffn_w_gate", "ffn_w_up", "ffn_conv_w", "ffn_conv_b", "ffn_w_down", "final_norm_g"]
    outs = [r_loss.reshape(()), dh0[None]]
    for part in range(4):
        outs += [res[name][part] for name in order]
    return tuple(outs)
```

```python
import functools
import math

import jax
import jax.numpy as jnp
from jax import lax
from jax.experimental import pallas as pl
from jax.experimental.pallas import tpu as pltpu

F32 = jnp.float32
BF16 = jnp.bfloat16

RMS_EPS = 1e-6
LN_EPS = 1e-5
CHUNK = 128
GM_GROUPS = 8
HEAD_DIM = 64
LANES = 128
ATT_BLOCK = 256
ATT_CHUNK = 1024
VMEM_LIMIT_V7X = 56 * 1024 * 1024

ADAM_LR = 0.001
ADAM_B1 = 0.9
ADAM_B2 = 0.999
ADAM_EPS = 1e-08
ADAM_WD = 0.01
ADAM_STEP = 10

MESH = pl.DeviceIdType.MESH
ANY = pl.BlockSpec(memory_space=pl.ANY)
NEG_BIG = -1e30


def _params(n_grid):
    return pltpu.CompilerParams(dimension_semantics=("arbitrary",) * n_grid, vmem_limit_bytes=VMEM_LIMIT_V7X)


def _dot(a, b):
    return jnp.dot(a, b, preferred_element_type=F32)


def _dot_nt(a, b):
    return lax.dot_general(a, b, (((1,), (1,)), ((), ())), preferred_element_type=F32)


def _dot_tn(a, b):
    return lax.dot_general(a, b, (((0,), (0,)), ((), ())), preferred_element_type=F32)


def _split3(x):
    hi = x.astype(BF16)
    r = x - hi.astype(F32)
    mid = r.astype(BF16)
    lo = (r - mid.astype(F32)).astype(BF16)
    return hi, mid, lo


def _dot3_lhs(x, m):
    hi, mid, lo = _split3(x)
    return _dot(hi, m) + _dot(mid, m) + _dot(lo, m)


def _dot3_rhs(m, x):
    hi, mid, lo = _split3(x)
    return _dot(m, hi) + _dot(m, mid) + _dot(m, lo)


def _load_once(pairs, sem):
    @pl.when(pl.program_id(0) == 0)
    def _():
        copies = [pltpu.make_async_copy(src, dst, sem.at[k]) for k, (src, dst) in enumerate(pairs)]
        for cp in copies:
            cp.start()
        for cp in copies:
            cp.wait()


def _rms_fwd(x, g):
    r = lax.rsqrt(jnp.mean(x * x, axis=-1, keepdims=True) + RMS_EPS)
    xhat = x * r
    return xhat, r, xhat * g


def _rms_bwd(dy, xhat, r, g):
    w = dy * g
    dx = r * (w - xhat * jnp.mean(w * xhat, axis=-1, keepdims=True))
    return dx, dy * xhat


def _gelu_parts(a):
    c = math.sqrt(2.0 / math.pi)
    a2 = a * a
    t = jnp.tanh(c * (a + 0.044715 * a * a2))
    z = 0.5 * a * (1.0 + t)
    dz = 0.5 * (1.0 + t) + 0.5 * a * (1.0 - t * t) * (c * (1.0 + 3.0 * 0.044715 * a2))
    return z, dz


def _sigmoid(x):
    return 1.0 / (1.0 + jnp.exp(-x))


def _gmlp_core(a, lng, lnb, wc_ref, bias, n_chunk, gd):
    e = a.shape[1] // 2
    z, dz = _gelu_parts(a)
    u = z[:, :e]
    v = z[:, e:]
    mu = jnp.mean(v, axis=-1, keepdims=True)
    vc = v - mu
    rstd = lax.rsqrt(jnp.mean(vc * vc, axis=-1, keepdims=True) + LN_EPS)
    vhat = vc * rstd
    vln = vhat * lng + lnb
    vlb = vln.astype(BF16)
    rows = []
    for ci in range(n_chunk):
        cols = []
        for g in range(GM_GROUPS):
            blk = vlb[ci * CHUNK:(ci + 1) * CHUNK, g * gd:(g + 1) * gd]
            cols.append(_dot(wc_ref[g], blk))
        rows.append(jnp.concatenate(cols, axis=1) + bias)
    s = rows[0] if n_chunk == 1 else jnp.concatenate(rows, axis=0)
    return dz, u, vhat, rstd, vlb, s


def _gmlp_fwd(h, g_mix, w_in, lng, lnb, wc, bias, w_out, tm):
    s_len, d = h.shape
    n_p, _, w = w_in.shape
    e = w_out.shape[0]
    gd = e // GM_GROUPS
    n_chunk = tm // CHUNK

    def body(h_ref, g_ref, win_hbm, lng_ref, lnb_ref, wc_ref, bias_ref, wout_hbm,
             hout_ref, a_ref, hn_ref, gated_ref, win_v, wout_v, sem):
        _load_once([(win_hbm, win_v), (wout_hbm, wout_v)], sem)
        x = h_ref[...]
        _, _, y = _rms_fwd(x, g_ref[...])
        hn = y.astype(BF16)
        hn_ref[...] = hn
        for p in range(n_p):
            a_ref[:, p * w:(p + 1) * w] = _dot(hn, win_v[p])
        _, u, _, _, _, s = _gmlp_core(a_ref[...], lng_ref[...], lnb_ref[...], wc_ref, bias_ref[...], n_chunk, gd)
        gated = (u * s).astype(BF16)
        gated_ref[...] = gated
        hout_ref[...] = x + _dot(gated, wout_v[...])

    row = lambda i: (i, 0)
    const2 = lambda i: (0, 0)
    return pl.pallas_call(
        body, name="gmlp_fwd", grid=(s_len // tm,),
        in_specs=[pl.BlockSpec((tm, d), row), pl.BlockSpec((1, d), const2), ANY,
                  pl.BlockSpec((1, e), const2), pl.BlockSpec((1, e), const2),
                  pl.BlockSpec(wc.shape, lambda i: (0, 0, 0)), pl.BlockSpec((CHUNK, e), const2), ANY],
        out_specs=[pl.BlockSpec((tm, d), row), pl.BlockSpec((tm, 2 * e), row),
                   pl.BlockSpec((tm, d), row), pl.BlockSpec((tm, e), row)],
        out_shape=[jax.ShapeDtypeStruct((s_len, d), F32), jax.ShapeDtypeStruct((s_len, 2 * e), F32),
                   jax.ShapeDtypeStruct((s_len, d), BF16), jax.ShapeDtypeStruct((s_len, e), BF16)],
        scratch_shapes=[pltpu.VMEM(w_in.shape, BF16), pltpu.VMEM(w_out.shape, BF16), pltpu.SemaphoreType.DMA((2,))],
        compiler_params=_params(1),
    )(h, g_mix, w_in, lng, lnb, wc, bias, w_out)


def _gmlp_bwd(h, dh, a, g_mix, w_in, lng, lnb, wc, wct, bias, w_out, seg, tm):
    s_len, d = h.shape
    n_p, _, w = w_in.shape
    e = w_out.shape[0]
    gd = e // GM_GROUPS
    n_chunk = tm // CHUNK
    n_blk = s_len // tm

    def body(h_ref, dh_ref, a_ref, g_ref, win_hbm, lng_ref, lnb_ref, wc_ref, wct_ref, bias_ref, wout_hbm, seg_ref,
             dhin_ref, da_ref, gws_ref, gbs_ref, glng_ref, glnb_ref, gmix_ref, win_v, wout_v, dsum, sem):
        i = pl.program_id(0)
        _load_once([(win_hbm, win_v), (wout_hbm, wout_v)], sem)

        @pl.when(i == 0)
        def _():
            gws_ref[...] = jnp.zeros_like(gws_ref)
            glng_ref[...] = jnp.zeros_like(glng_ref)
            glnb_ref[...] = jnp.zeros_like(glnb_ref)
            gmix_ref[...] = jnp.zeros_like(gmix_ref)
            dsum[...] = jnp.zeros_like(dsum)

        x = h_ref[...]
        dh_v = dh_ref[...]
        g = g_ref[...]
        lng_v = lng_ref[...]
        xhat, r, _ = _rms_fwd(x, g)
        dz_da, u, vhat, rstd, vlb, s = _gmlp_core(a_ref[...], lng_v, lnb_ref[...], wc_ref, bias_ref[...], n_chunk, gd)
        dg = _dot_nt(dh_v.astype(BF16), wout_v[...])
        du = dg * s
        ds = dg * u
        dsb = ds.astype(BF16)
        rows = []
        ds_acc = None
        for ci in range(n_chunk):
            lo, hi = ci * CHUNK, (ci + 1) * CHUNK
            cols = []
            for gi in range(GM_GROUPS):
                d_blk = dsb[lo:hi, gi * gd:(gi + 1) * gd]
                gws_ref[gi] += _dot_nt(d_blk, vlb[lo:hi, gi * gd:(gi + 1) * gd])
                cols.append(_dot(wct_ref[gi], d_blk))
            rows.append(jnp.concatenate(cols, axis=1))
            ds_acc = ds[lo:hi] if ds_acc is None else ds_acc + ds[lo:hi]
        dsum[...] += ds_acc
        dvln = rows[0] if n_chunk == 1 else jnp.concatenate(rows, axis=0)
        glng_ref[...] += jnp.sum(dvln * vhat, axis=0, keepdims=True)
        glnb_ref[...] += jnp.sum(dvln, axis=0, keepdims=True)
        dvhat = dvln * lng_v
        dv = rstd * (dvhat - jnp.mean(dvhat, axis=-1, keepdims=True)
                     - vhat * jnp.mean(dvhat * vhat, axis=-1, keepdims=True))
        da = jnp.concatenate([du, dv], axis=1) * dz_da
        dab = da.astype(BF16)
        da_ref[...] = dab
        dhn = _dot_nt(dab[:, :w], win_v[0])
        for p in range(1, n_p):
            dhn += _dot_nt(dab[:, p * w:(p + 1) * w], win_v[p])
        dx, gg = _rms_bwd(dhn, xhat, r, g)
        gmix_ref[...] += jnp.sum(gg, axis=0, keepdims=True)
        dhin_ref[...] = dh_v + dx

        @pl.when(i == n_blk - 1)
        def _():
            tril = lax.broadcasted_iota(jnp.int32, (CHUNK, CHUNK), 0) >= lax.broadcasted_iota(jnp.int32, (CHUNK, CHUNK), 1)
            for gi in range(GM_GROUPS):
                gws_ref[gi] = jnp.where(tril, gws_ref[gi], 0.0)
            gbs_ref[...] = _dot3_lhs(dsum[...], seg_ref[...])

    row = lambda i: (i, 0)
    const2 = lambda i: (0, 0)
    const3 = lambda i: (0, 0, 0)
    return pl.pallas_call(
        body, name="gmlp_bwd", grid=(n_blk,),
        in_specs=[pl.BlockSpec((tm, d), row), pl.BlockSpec((tm, d), row), pl.BlockSpec((tm, 2 * e), row),
                  pl.BlockSpec((1, d), const2), ANY, pl.BlockSpec((1, e), const2), pl.BlockSpec((1, e), const2),
                  pl.BlockSpec(wc.shape, const3), pl.BlockSpec(wct.shape, const3), pl.BlockSpec((CHUNK, e), const2),
                  ANY, pl.BlockSpec((e, LANES), const2)],
        out_specs=[pl.BlockSpec((tm, d), row), pl.BlockSpec((tm, 2 * e), row), pl.BlockSpec(wc.shape, const3),
                   pl.BlockSpec((CHUNK, LANES), const2), pl.BlockSpec((1, e), const2), pl.BlockSpec((1, e), const2),
                   pl.BlockSpec((1, d), const2)],
        out_shape=[jax.ShapeDtypeStruct((s_len, d), F32), jax.ShapeDtypeStruct((s_len, 2 * e), BF16),
                   jax.ShapeDtypeStruct(wc.shape, F32), jax.ShapeDtypeStruct((CHUNK, LANES), F32),
                   jax.ShapeDtypeStruct((1, e), F32), jax.ShapeDtypeStruct((1, e), F32), jax.ShapeDtypeStruct((1, d), F32)],
        scratch_shapes=[pltpu.VMEM(w_in.shape, BF16), pltpu.VMEM(w_out.shape, BF16), pltpu.VMEM((CHUNK, e), F32),
                        pltpu.SemaphoreType.DMA((2,))],
        compiler_params=_params(1),
    )(h, dh, a, g_mix, w_in, lng, lnb, wc, wct, bias, w_out, seg)


def _shift_down(a, k, fill):
    tm = a.shape[0]
    out = pltpu.roll(a, k, 0)
    rid = lax.broadcasted_iota(jnp.int32, a.shape, 0)
    for j in range(k):
        out = jnp.where(rid == j, fill[8 - k + j:8 - k + j + 1, :], out)
    return out


def _shift_up(a, k, fill):
    tm = a.shape[0]
    out = pltpu.roll(a, tm - k, 0)
    rid = lax.broadcasted_iota(jnp.int32, a.shape, 0)
    for j in range(k):
        out = jnp.where(rid == tm - k + j, fill[j:j + 1, :], out)
    return out


def _ffn_fwd(h, g_norm, wg_all, wu_all, wd_all, layer, conv_w, conv_b, tm):
    s_len, d = h.shape
    n_p = wg_all.shape[0]
    fq = wg_all.shape[3]
    f = n_p * fq

    def body(h_ref, g_ref, wg_hbm, wu_hbm, wd_hbm, cw_ref, cb_ref,
             hout_ref, a_ref, up_ref, hn_ref, hid_ref, wg_v, wu_v, wd_v, carry, sem):
        i = pl.program_id(0)
        _load_once([(wg_hbm.at[:, layer], wg_v), (wu_hbm.at[:, layer], wu_v), (wd_hbm.at[:, layer], wd_v)], sem)

        @pl.when(i == 0)
        def _():
            carry[...] = jnp.zeros_like(carry)

        x = h_ref[...]
        _, _, y = _rms_fwd(x, g_ref[...])
        hn = y.astype(BF16)
        hn_ref[...] = hn
        for p in range(n_p):
            a_ref[:, p * fq:(p + 1) * fq] = _dot(hn, wg_v[p])
            up_ref[:, p * fq:(p + 1) * fq] = _dot(hn, wu_v[p])
        a = a_ref[...]
        prev = carry[...]
        am1 = _shift_down(a, 1, prev)
        am2 = _shift_down(a, 2, prev)
        carry[...] = a[tm - 8:tm, :]
        cw = cw_ref[...]
        ac = cb_ref[...] + am2 * cw[0:1, :]
        ac = ac + am1 * cw[1:2, :]
        ac = ac + a * cw[2:3, :]
        hid = (ac * _sigmoid(ac) * up_ref[...]).astype(BF16)
        hid_ref[...] = hid
        y2 = _dot(hid[:, :fq], wd_v[0])
        for p in range(1, n_p):
            y2 += _dot(hid[:, p * fq:(p + 1) * fq], wd_v[p])
        hout_ref[...] = x + y2

    row = lambda i: (i, 0)
    const2 = lambda i: (0, 0)
    return pl.pallas_call(
        body, name=f"ffn_fwd_{layer}", grid=(s_len // tm,),
        in_specs=[pl.BlockSpec((tm, d), row), pl.BlockSpec((1, d), const2), ANY, ANY, ANY,
                  pl.BlockSpec((8, f), const2), pl.BlockSpec((1, f), const2)],
        out_specs=[pl.BlockSpec((tm, d), row), pl.BlockSpec((tm, f), row), pl.BlockSpec((tm, f), row),
                   pl.BlockSpec((tm, d), row), pl.BlockSpec((tm, f), row)],
        out_shape=[jax.ShapeDtypeStruct((s_len, d), F32), jax.ShapeDtypeStruct((s_len, f), F32),
                   jax.ShapeDtypeStruct((s_len, f), F32), jax.ShapeDtypeStruct((s_len, d), BF16),
                   jax.ShapeDtypeStruct((s_len, f), BF16)],
        scratch_shapes=[pltpu.VMEM((n_p, d, fq), BF16), pltpu.VMEM((n_p, d, fq), BF16), pltpu.VMEM((n_p, fq, d), BF16),
                        pltpu.VMEM((8, f), F32), pltpu.SemaphoreType.DMA((3,))],
        compiler_params=_params(1),
    )(h, g_norm, wg_all, wu_all, wd_all, conv_w, conv_b)


def _ffn_bwd(h, dh, a, up, g_norm, wg_all, wu_all, wd_all, layer, conv_w, conv_b, tm):
    s_len, d = h.shape
    n_p = wg_all.shape[0]
    fq = wg_all.shape[3]
    f = n_p * fq
    n_blk = s_len // tm
    t8 = tm // 8

    def body(h_ref, dh_ref, a_ref, ahalo_ref, up_ref, g_ref, wg_hbm, wu_hbm, wd_hbm, cw_ref, cb_ref,
             dhin_ref, da_ref, dup_ref, gcw_ref, gcb_ref, gn_ref, wg_v, wu_v, wd_v, carry, sem):
        i = pl.program_id(0)
        _load_once([(wg_hbm.at[:, layer], wg_v), (wu_hbm.at[:, layer], wu_v), (wd_hbm.at[:, layer], wd_v)], sem)

        @pl.when(i == 0)
        def _():
            carry[...] = jnp.zeros_like(carry)
            gcw_ref[...] = jnp.zeros_like(gcw_ref)
            gcb_ref[...] = jnp.zeros_like(gcb_ref)
            gn_ref[...] = jnp.zeros_like(gn_ref)

        x = h_ref[...]
        dh_v = dh_ref[...]
        g = g_ref[...]
        xhat, r, _ = _rms_fwd(x, g)
        a = a_ref[...]
        up_v = up_ref[...]
        prev = jnp.where(i == n_blk - 1, 0.0, ahalo_ref[...])
        am1 = _shift_down(a, 1, prev)
        am2 = _shift_down(a, 2, prev)
        cw = cw_ref[...]
        ac = cb_ref[...] + am2 * cw[0:1, :]
        ac = ac + am1 * cw[1:2, :]
        ac = ac + a * cw[2:3, :]
        sg = _sigmoid(ac)
        sil = ac * sg
        dhb = dh_v.astype(BF16)
        dhid = jnp.concatenate([_dot_nt(dhb, wd_v[p]) for p in range(n_p)], axis=1)
        dup = dhid * sil
        dac = dhid * up_v * (sg * (1.0 + ac * (1.0 - sg)))
        gcb_ref[...] += jnp.sum(dac, axis=0, keepdims=True)
        gcw_ref[0:1, :] += jnp.sum(dac * am2, axis=0, keepdims=True)
        gcw_ref[1:2, :] += jnp.sum(dac * am1, axis=0, keepdims=True)
        gcw_ref[2:3, :] += jnp.sum(dac * a, axis=0, keepdims=True)
        nxt = carry[...]
        dp1 = _shift_up(dac, 1, nxt)
        dp2 = _shift_up(dac, 2, nxt)
        carry[...] = dac[0:8, :]
        da = dac * cw[2:3, :] + dp1 * cw[1:2, :] + dp2 * cw[0:1, :]
        dab = da.astype(BF16)
        dupb = dup.astype(BF16)
        da_ref[...] = dab
        dup_ref[...] = dupb
        dhn = _dot_nt(dab[:, :fq], wg_v[0]) + _dot_nt(dupb[:, :fq], wu_v[0])
        for p in range(1, n_p):
            dhn += _dot_nt(dab[:, p * fq:(p + 1) * fq], wg_v[p]) + _dot_nt(dupb[:, p * fq:(p + 1) * fq], wu_v[p])
        dx, gg = _rms_bwd(dhn, xhat, r, g)
        gn_ref[...] += jnp.sum(gg, axis=0, keepdims=True)
        dhin_ref[...] = dh_v + dx

    rev = lambda i: (n_blk - 1 - i, 0)
    halo = lambda i: (jnp.maximum((n_blk - 1 - i) * t8 - 1, 0), 0)
    const2 = lambda i: (0, 0)
    return pl.pallas_call(
        body, name=f"ffn_bwd_{layer}", grid=(n_blk,),
        in_specs=[pl.BlockSpec((tm, d), rev), pl.BlockSpec((tm, d), rev), pl.BlockSpec((tm, f), rev),
                  pl.BlockSpec((8, f), halo), pl.BlockSpec((tm, f), rev), pl.BlockSpec((1, d), const2), ANY, ANY, ANY,
                  pl.BlockSpec((8, f), const2), pl.BlockSpec((1, f), const2)],
        out_specs=[pl.BlockSpec((tm, d), rev), pl.BlockSpec((tm, f), rev), pl.BlockSpec((tm, f), rev),
                   pl.BlockSpec((8, f), const2), pl.BlockSpec((1, f), const2), pl.BlockSpec((1, d), const2)],
        out_shape=[jax.ShapeDtypeStruct((s_len, d), F32), jax.ShapeDtypeStruct((s_len, f), BF16),
                   jax.ShapeDtypeStruct((s_len, f), BF16), jax.ShapeDtypeStruct((8, f), F32),
                   jax.ShapeDtypeStruct((1, f), F32), jax.ShapeDtypeStruct((1, d), F32)],
        scratch_shapes=[pltpu.VMEM((n_p, d, fq), BF16), pltpu.VMEM((n_p, d, fq), BF16), pltpu.VMEM((n_p, fq, d), BF16),
                        pltpu.VMEM((8, f), F32), pltpu.SemaphoreType.DMA((3,))],
        compiler_params=_params(1),
    )(h, dh, a, a, up, g_norm, wg_all, wu_all, wd_all, conv_w, conv_b)


def _even_head_lanes(shape, axis):
    return (lax.broadcasted_iota(jnp.int32, shape, axis) & HEAD_DIM) == 0


def _pair_select(lo, hi, shape):
    return jnp.where(lax.broadcasted_iota(jnp.int32, shape, 1) < HEAD_DIM, lo, hi)


def _causal(row0, col0, shape):
    return row0 + lax.broadcasted_iota(jnp.int32, shape, 0) >= col0 + lax.broadcasted_iota(jnp.int32, shape, 1)


N_SPARE = 3


def _spare_selectors(d, key_side):
    lane = jnp.arange(d)[None, :]
    row = jnp.arange(N_SPARE * LANES)[:, None]
    head, part = row % LANES, row // LANES
    off = N_SPARE if key_side else 0
    sel_a = ((head % 2 == 0) & (lane == LANES * (head // 2) + HEAD_DIM + off + part)).astype(F32)
    sel_b = ((head % 2 == 1) & (lane == LANES * (head // 2) + off + part)).astype(F32)
    sign = -1.0 if key_side else 1.0
    ones_off = 0 if key_side else N_SPARE
    in_pair = jnp.arange(d)[None, :] % LANES
    ones_a = ((in_pair >= HEAD_DIM + ones_off) & (in_pair < HEAD_DIM + ones_off + N_SPARE)).astype(F32)
    ones_b = ((in_pair >= ones_off) & (in_pair < ones_off + N_SPARE)).astype(F32)
    return (sign * sel_a).astype(BF16), (sign * sel_b).astype(BF16), ones_a, ones_b


def _parts(x):
    return jnp.concatenate(_split3(x), axis=1)


def _fox_proj_fwd(h, g_norm, wq, wk, wv, wf, bf, sel_q, sel_k, tm):
    s_len, d = h.shape
    sq_a, sq_b, oq_a, oq_b = sel_q
    sk_a, sk_b, ok_a, ok_b = sel_k

    def body(h_ref, g_ref, wq_hbm, wk_hbm, wv_hbm, wf_ref, bf_ref, sqa_ref, sqb_ref, oqa_ref, oqb_ref,
             ska_ref, skb_ref, oka_ref, okb_ref,
             hn_ref, qa_ref, qb_ref, ka_ref, kb_ref, kat_ref, kbt_ref, va_ref, vb_ref, vat_ref, vbt_ref, z_ref,
             wq_v, wk_v, wv_v, total, sem):
        i = pl.program_id(0)
        _load_once([(wq_hbm, wq_v), (wk_hbm, wk_v), (wv_hbm, wv_v)], sem)

        @pl.when(i == 0)
        def _():
            total[...] = jnp.zeros_like(total)

        x = h_ref[...]
        _, _, y = _rms_fwd(x, g_ref[...])
        hn = y.astype(BF16)
        hn_ref[...] = hn
        z = _dot(hn, wf_ref[...]) + bf_ref[...]
        z_ref[...] = z
        logf = jnp.minimum(z, 0.0) - jnp.log(1.0 + jnp.exp(-jnp.abs(z)))
        tri = (lax.broadcasted_iota(jnp.int32, (tm, tm), 0) >= lax.broadcasted_iota(jnp.int32, (tm, tm), 1))
        cum = _dot3_rhs(jnp.where(tri, 1.0, 0.0).astype(BF16), logf) + total[0:1, :]
        total[...] = jnp.broadcast_to(cum[tm - 1:tm, :], total.shape)
        parts = _parts(cum)

        even = _even_head_lanes((tm, d), 1)
        q = _dot(hn, wq_v[...]) * (HEAD_DIM ** -0.5)
        qa_ref[...] = jnp.where(even, q, _dot(parts, sqa_ref[...]) + oqa_ref[...]).astype(BF16)
        qb_ref[...] = jnp.where(even, _dot(parts, sqb_ref[...]) + oqb_ref[...], q).astype(BF16)
        k = _dot(hn, wk_v[...])
        ka = jnp.where(even, k, _dot(parts, ska_ref[...]) + oka_ref[...])
        kb = jnp.where(even, _dot(parts, skb_ref[...]) + okb_ref[...], k)
        ka_ref[...] = ka.astype(BF16)
        kb_ref[...] = kb.astype(BF16)
        kat_ref[...] = ka.T.astype(BF16)
        kbt_ref[...] = kb.T.astype(BF16)
        v = _dot(hn, wv_v[...])
        va = jnp.where(even, v, oka_ref[...])
        vb = jnp.where(even, okb_ref[...], v)
        va_ref[...] = va.astype(BF16)
        vb_ref[...] = vb.astype(BF16)
        vat_ref[...] = va.T.astype(BF16)
        vbt_ref[...] = vb.T.astype(BF16)

    row = lambda i: (i, 0)
    col = lambda i: (0, i)
    const2 = lambda i: (0, 0)
    sd = jax.ShapeDtypeStruct((s_len, d), BF16)
    ds_ = jax.ShapeDtypeStruct((d, s_len), BF16)
    rs, cs = pl.BlockSpec((tm, d), row), pl.BlockSpec((d, tm), col)
    sel = pl.BlockSpec((N_SPARE * LANES, d), const2)
    one = pl.BlockSpec((1, d), const2)
    return pl.pallas_call(
        body, name="fox_proj_fwd", grid=(s_len // tm,),
        in_specs=[rs, one, ANY, ANY, ANY, pl.BlockSpec((d, LANES), const2), pl.BlockSpec((1, LANES), const2),
                  sel, sel, one, one, sel, sel, one, one],
        out_specs=[rs, rs, rs, rs, rs, cs, cs, rs, rs, cs, cs, pl.BlockSpec((tm, LANES), row)],
        out_shape=[sd, sd, sd, sd, sd, ds_, ds_, sd, sd, ds_, ds_, jax.ShapeDtypeStruct((s_len, LANES), F32)],
        scratch_shapes=[pltpu.VMEM((d, d), BF16), pltpu.VMEM((d, d), BF16), pltpu.VMEM((d, d), BF16),
                        pltpu.VMEM((8, LANES), F32), pltpu.SemaphoreType.DMA((3,))],
        compiler_params=_params(1),
    )(h, g_norm, wq, wk, wv, wf, bf, sq_a, sq_b, oq_a, oq_b, sk_a, sk_b, ok_a, ok_b)


def _spare_cols(x, base):
    xf = x[:, base:base + N_SPARE].astype(F32)
    return xf[:, 0:1] + xf[:, 1:2] + xf[:, 2:3]


def _with_query_term(x, term, base):
    lane = lax.broadcasted_iota(jnp.int32, x.shape, 1)
    hi, mid, lo = _split3(term)
    out = jnp.where(lane == base, hi.astype(F32), x)
    out = jnp.where(lane == base + 1, mid.astype(F32), out)
    out = jnp.where(lane == base + 2, lo.astype(F32), out)
    return jnp.where((lane >= base + N_SPARE) & (lane < base + 2 * N_SPARE), 1.0, out)


def _flash_fwd(qa, qb, kat, kbt, va, vb):
    s_len, d = qa.shape
    sub = ATT_BLOCK
    n_sub = 2 if s_len % (2 * sub) == 0 else 1
    t = n_sub * sub
    w = min(ATT_CHUNK, s_len)
    n_pair = d // LANES
    n_q = s_len // t
    bases = (HEAD_DIM, 0)
    chains = [(r, hh) for r in range(n_sub) for hh in range(2)]

    def body(qa_ref, qb_ref, kat_ref, kbt_ref, va_ref, vb_ref, o_ref, qa2_ref, qb2_ref, qat2_ref, qbt2_ref):
        i = pl.program_id(1)
        q_refs = (qa_ref, qb_ref)
        qs = [q_refs[hh][r * sub:(r + 1) * sub, :] for r, hh in chains]
        kts = (kat_ref, kbt_ref)
        vs = (va_ref, vb_ref)

        def step(kb, carry, masked):
            off = pl.multiple_of(kb * w, w)
            scores = [_dot(qs[c], kts[hh][:, pl.ds(off, w)]) for c, (r, hh) in enumerate(chains)]
            probs, stats = [], []
            for c, (r, hh) in enumerate(chains):
                m, _ = carry[c]
                s = scores[c]
                if masked:
                    s = jnp.where(_causal(i * t + r * sub, off, (sub, w)), s, NEG_BIG)
                m_new = jnp.maximum(m, jnp.max(s, axis=1, keepdims=True))
                probs.append(jnp.exp(s - m_new).astype(BF16))
                stats.append((m_new, jnp.exp(m - m_new)))
            return tuple((stats[c][0], carry[c][1] * stats[c][1] + _dot(probs[c], vs[hh][pl.ds(off, w), :]))
                         for c, (r, hh) in enumerate(chains))

        init = ((jnp.full((sub, 1), NEG_BIG, F32), jnp.zeros((sub, LANES), F32)),) * len(chains)
        diag = (i * t) // w
        carry = lax.fori_loop(0, diag, lambda kb, c: step(kb, c, False), init)
        carry = step(diag, carry, True)
        for r in range(n_sub):
            outs, q2 = [], []
            for hh in range(2):
                m, acc = carry[2 * r + hh]
                l = acc[:, bases[hh]:bases[hh] + 1]
                outs.append(acc / l)
                term = _spare_cols(qs[2 * r + hh], bases[hh]) - (m + jnp.log(l))
                q2.append(_with_query_term(qs[2 * r + hh].astype(F32), term, bases[hh]))
            rows = slice(r * sub, (r + 1) * sub)
            o_ref[rows, :] = _pair_select(outs[0], outs[1], (sub, LANES))
            qa2_ref[rows, :] = q2[0].astype(BF16)
            qb2_ref[rows, :] = q2[1].astype(BF16)
            qat2_ref[:, rows] = q2[0].T.astype(BF16)
            qbt2_ref[:, rows] = q2[1].T.astype(BF16)

    qblk = pl.BlockSpec((t, LANES), lambda j, i: (i, j))
    qblk_t = pl.BlockSpec((LANES, t), lambda j, i: (j, i))
    whole_t = pl.BlockSpec((LANES, s_len), lambda j, i: (j, 0))
    whole = pl.BlockSpec((s_len, LANES), lambda j, i: (0, j))
    sd = jax.ShapeDtypeStruct((s_len, d), BF16)
    ds_ = jax.ShapeDtypeStruct((d, s_len), BF16)
    return pl.pallas_call(
        body, name="flash_fwd", grid=(n_pair, n_q),
        in_specs=[qblk, qblk, whole_t, whole_t, whole, whole],
        out_specs=[qblk, qblk, qblk, qblk_t, qblk_t],
        out_shape=[jax.ShapeDtypeStruct((s_len, d), F32), sd, sd, ds_, ds_],
        compiler_params=_params(2),
    )(qa, qb, kat, kbt, va, vb)


def _flash_bwd(qa, qb, qat, qbt, kat, kbt, ka, kb_, vat, vbt, doa, dob, doat, dobt):
    s_len, d = qa.shape
    t = ATT_BLOCK
    w = min(ATT_CHUNK, s_len)
    n_pair = d // LANES
    n_q = s_len // t

    def body(qa_ref, qb_ref, qat_ref, qbt_ref, kat_hbm, kbt_hbm, ka_hbm, kb_hbm, vat_hbm, vbt_hbm,
             doa_ref, dob_ref, doat_ref, dobt_ref,
             dq_ref, dkt_ref, dvt_ref, rs_ref, cs_ref,
             kat_v, kbt_v, ka_v, kb_v, vat_v, vbt_v, dkt_acc, dvt_acc, cs_acc, sem):
        j = pl.program_id(0)
        i = pl.program_id(1)

        @pl.when(i == 0)
        def _():
            rows = pl.ds(pl.multiple_of(j * LANES, LANES), LANES)
            copies = [pltpu.make_async_copy(src, dst, sem.at[n]) for n, (src, dst) in enumerate([
                (kat_hbm.at[rows, :], kat_v), (kbt_hbm.at[rows, :], kbt_v), (ka_hbm.at[:, rows], ka_v),
                (kb_hbm.at[:, rows], kb_v), (vat_hbm.at[rows, :], vat_v), (vbt_hbm.at[rows, :], vbt_v)])]
            for cp in copies:
                cp.start()
            dkt_acc[...] = jnp.zeros_like(dkt_acc)
            dvt_acc[...] = jnp.zeros_like(dvt_acc)
            cs_acc[...] = jnp.zeros_like(cs_acc)
            for cp in copies:
                cp.wait()

        qs = (qa_ref[...], qb_ref[...])
        dos = (doa_ref[...], dob_ref[...])
        first = lax.broadcasted_iota(jnp.int32, (LANES, t), 0) < HEAD_DIM
        zero = jnp.zeros((LANES, t), BF16)
        qts = (jnp.where(first, qat_ref[...], zero), jnp.where(first, zero, qbt_ref[...]))
        dots = (jnp.where(first, doat_ref[...], zero), jnp.where(first, zero, dobt_ref[...]))
        ones = jnp.ones((8, t), BF16)
        kts, ks, vts = (kat_v, kbt_v), (ka_v, kb_v), (vat_v, vbt_v)

        def step(kb, carry, masked):
            off = pl.multiple_of(kb * w, w)
            cols = pl.ds(off, w)
            scores = [_dot(qs[hh], kts[hh][:, cols]) for hh in range(2)]
            dps = [_dot(dos[hh], vts[hh][:, cols]) for hh in range(2)]
            ps, dss = [], []
            for hh in range(2):
                s = scores[hh]
                if masked:
                    s = jnp.where(_causal(i * t, off, (t, w)), s, NEG_BIG)
                p = jnp.exp(s)
                dss.append((p * dps[hh]).astype(BF16))
                ps.append(p.astype(BF16))
            out = []
            for hh in range(2):
                out.append(carry[hh] + _dot(dss[hh], ks[hh][cols, :]))
                dvt_acc[:, cols] += _dot(dots[hh], ps[hh])
                dkt_acc[:, cols] += _dot(qts[hh], dss[hh])
                cs_acc[8 * hh:8 * hh + 8, cols] += _dot(ones, dss[hh])
            return tuple(out)

        diag = (i * t) // w
        carry = lax.fori_loop(0, diag, lambda kb, c: step(kb, c, False), (jnp.zeros((t, LANES), F32),) * 2)
        acc = step(diag, carry, True)
        dq_ref[...] = (_pair_select(acc[0], acc[1], (t, LANES)) * (HEAD_DIM ** -0.5)).astype(BF16)
        rs_ref[...] = _pair_select(acc[0][:, HEAD_DIM:HEAD_DIM + 1], acc[1][:, 0:1], (t, LANES))

        @pl.when(i == n_q - 1)
        def _():
            dkt_ref[...] = dkt_acc[...].astype(BF16)
            dvt_ref[...] = dvt_acc[...].astype(BF16)
            cs_ref[0] = cs_acc[...]

    qblk = pl.BlockSpec((t, LANES), lambda j, i: (i, j))
    qblk_t = pl.BlockSpec((LANES, t), lambda j, i: (j, i))
    whole_t = pl.BlockSpec((LANES, s_len), lambda j, i: (j, 0))
    ds_ = jax.ShapeDtypeStruct((d, s_len), BF16)
    return pl.pallas_call(
        body, name="flash_bwd", grid=(n_pair, n_q),
        in_specs=[qblk, qblk, qblk_t, qblk_t, ANY, ANY, ANY, ANY, ANY, ANY, qblk, qblk, qblk_t, qblk_t],
        out_specs=[qblk, whole_t, whole_t, qblk, pl.BlockSpec((1, 16, s_len), lambda j, i: (j, 0, 0))],
        out_shape=[jax.ShapeDtypeStruct((s_len, d), BF16), ds_, ds_, jax.ShapeDtypeStruct((s_len, d), F32),
                   jax.ShapeDtypeStruct((n_pair, 16, s_len), F32)],
        scratch_shapes=[pltpu.VMEM((LANES, s_len), BF16), pltpu.VMEM((LANES, s_len), BF16),
                        pltpu.VMEM((s_len, LANES), BF16), pltpu.VMEM((s_len, LANES), BF16),
                        pltpu.VMEM((LANES, s_len), BF16), pltpu.VMEM((LANES, s_len), BF16),
                        pltpu.VMEM((LANES, s_len), F32), pltpu.VMEM((LANES, s_len), F32),
                        pltpu.VMEM((16, s_len), F32), pltpu.SemaphoreType.DMA((6,))],
        compiler_params=_params(2),
    )(qa, qb, qat, qbt, kat, kbt, ka, kb_, vat, vbt, doa, dob, doat, dobt)


def _wgrad_t(at, b, name):
    k, s_len = at.shape
    n = b.shape[1]
    tn, tk, ts = min(n, 1024), min(k, 1024), 512

    def body(a_ref, b_ref, o_ref):
        @pl.when(pl.program_id(2) == 0)
        def _():
            o_ref[...] = jnp.zeros_like(o_ref)
        o_ref[...] += _dot(a_ref[...].astype(BF16), b_ref[...].astype(BF16))

    return pl.pallas_call(
        body, name=name, grid=(k // tk, n // tn, s_len // ts),
        in_specs=[pl.BlockSpec((tk, ts), lambda a, b_, c: (a, c)), pl.BlockSpec((ts, tn), lambda a, b_, c: (c, b_))],
        out_specs=pl.BlockSpec((tk, tn), lambda a, b_, c: (a, b_)),
        out_shape=jax.ShapeDtypeStruct((k, n), F32),
        compiler_params=_params(3),
    )(at, b)


def _oproj_bwd(dh, o, wo, seg, sel_q, tm):
    s_len, d = dh.shape
    sq_a, sq_b, _, _ = sel_q

    def body(dh_ref, o_ref, wo_hbm, seg_ref, sqa_ref, sqb_ref, doa_ref, dob_ref, doat_ref, dobt_ref, wo_v, sem):
        _load_once([(wo_hbm, wo_v)], sem)
        do = _dot_nt(dh_ref[...].astype(BF16), wo_v[...])
        parts = _parts(-_dot3_lhs(do * o_ref[...], seg_ref[...]))
        even = _even_head_lanes((tm, d), 1)
        doa = jnp.where(even, do, _dot(parts, sqa_ref[...]))
        dob = jnp.where(even, _dot(parts, sqb_ref[...]), do)
        doa_ref[...] = doa.astype(BF16)
        dob_ref[...] = dob.astype(BF16)
        doat_ref[...] = doa.T.astype(BF16)
        dobt_ref[...] = dob.T.astype(BF16)

    row = lambda i: (i, 0)
    const2 = lambda i: (0, 0)
    rs, cs = pl.BlockSpec((tm, d), row), pl.BlockSpec((d, tm), lambda i: (0, i))
    sel = pl.BlockSpec((N_SPARE * LANES, d), const2)
    sd = jax.ShapeDtypeStruct((s_len, d), BF16)
    ds_ = jax.ShapeDtypeStruct((d, s_len), BF16)
    return pl.pallas_call(
        body, name="oproj_bwd", grid=(s_len // tm,),
        in_specs=[rs, rs, ANY, pl.BlockSpec((d, LANES), const2), sel, sel],
        out_specs=[rs, rs, cs, cs], out_shape=[sd, sd, ds_, ds_],
        scratch_shapes=[pltpu.VMEM((d, d), BF16), pltpu.SemaphoreType.DMA((1,))],
        compiler_params=_params(1),
    )(dh, o, wo, seg, sq_a, sq_b)


def _oproj_fwd(h, o, wo, tm):
    s_len, d = h.shape

    def body(h_ref, o_ref, wo_hbm, hout_ref, wo_v, sem):
        _load_once([(wo_hbm, wo_v)], sem)
        hout_ref[...] = h_ref[...] + _dot(o_ref[...].astype(BF16), wo_v[...])

    row = lambda i: (i, 0)
    return pl.pallas_call(
        body, name="oproj_fwd", grid=(s_len // tm,),
        in_specs=[pl.BlockSpec((tm, d), row), pl.BlockSpec((tm, d), row), ANY],
        out_specs=pl.BlockSpec((tm, d), row),
        out_shape=jax.ShapeDtypeStruct((s_len, d), F32),
        scratch_shapes=[pltpu.VMEM((d, d), BF16), pltpu.SemaphoreType.DMA((1,))],
        compiler_params=_params(1),
    )(h, o, wo)


def _forget_bwd(dcum, z, tm):
    s_len = dcum.shape[0]
    n_blk = s_len // tm

    def body(dc_ref, z_ref, dfl_ref, gb_ref, total):
        i = pl.program_id(0)

        @pl.when(i == 0)
        def _():
            total[...] = jnp.zeros_like(total)
            gb_ref[...] = jnp.zeros_like(gb_ref)

        upper = (lax.broadcasted_iota(jnp.int32, (tm, tm), 0) <= lax.broadcasted_iota(jnp.int32, (tm, tm), 1))
        suffix = _dot3_rhs(jnp.where(upper, 1.0, 0.0).astype(BF16), dc_ref[...]) + total[0:1, :]
        total[...] = jnp.broadcast_to(suffix[0:1, :], total.shape)
        dfl = suffix * _sigmoid(-z_ref[...])
        dfl_ref[...] = dfl
        gb_ref[...] += jnp.sum(dfl, axis=0, keepdims=True)

    rev = lambda i: (n_blk - 1 - i, 0)
    return pl.pallas_call(
        body, name="forget_bwd", grid=(n_blk,),
        in_specs=[pl.BlockSpec((tm, LANES), rev), pl.BlockSpec((tm, LANES), rev)],
        out_specs=[pl.BlockSpec((tm, LANES), rev), pl.BlockSpec((1, LANES), lambda i: (0, 0))],
        out_shape=[jax.ShapeDtypeStruct((s_len, LANES), F32), jax.ShapeDtypeStruct((1, LANES), F32)],
        scratch_shapes=[pltpu.VMEM((8, LANES), F32)],
        compiler_params=_params(1),
    )(dcum, z)


def _fox_proj_bwd(h, dh, dq, dkt, dvt, dfl, g_norm, wq, wk, wv, wf, tm):
    s_len, d = h.shape

    def body(h_ref, dh_ref, dq_ref, dkt_ref, dvt_ref, dfl_ref, g_ref, wq_hbm, wk_hbm, wv_hbm, wf_ref,
             dhin_ref, dflb_ref, gn_ref, wq_v, wk_v, wv_v, sem):
        _load_once([(wq_hbm, wq_v), (wk_hbm, wk_v), (wv_hbm, wv_v)], sem)

        @pl.when(pl.program_id(0) == 0)
        def _():
            gn_ref[...] = jnp.zeros_like(gn_ref)

        g = g_ref[...]
        xhat, r, _ = _rms_fwd(h_ref[...], g)
        dflb = dfl_ref[...].astype(BF16)
        dflb_ref[...] = dflb
        from_kv = _dot(wk_v[...], dkt_ref[...]) + _dot(wv_v[...], dvt_ref[...])
        dhn = _dot_nt(dq_ref[...], wq_v[...]) + _dot_nt(dflb, wf_ref[...]) + from_kv.T
        dx, gg = _rms_bwd(dhn, xhat, r, g)
        gn_ref[...] += jnp.sum(gg, axis=0, keepdims=True)
        dhin_ref[...] = dh_ref[...] + dx

    row = lambda i: (i, 0)
    const2 = lambda i: (0, 0)
    rs = pl.BlockSpec((tm, d), row)
    cs = pl.BlockSpec((d, tm), lambda i: (0, i))
    return pl.pallas_call(
        body, name="fox_proj_bwd", grid=(s_len // tm,),
        in_specs=[rs, rs, rs, cs, cs, pl.BlockSpec((tm, LANES), row), pl.BlockSpec((1, d), const2), ANY, ANY, ANY,
                  pl.BlockSpec((d, LANES), const2)],
        out_specs=[rs, pl.BlockSpec((tm, LANES), row), pl.BlockSpec((1, d), const2)],
        out_shape=[jax.ShapeDtypeStruct((s_len, d), F32), jax.ShapeDtypeStruct((s_len, LANES), BF16),
                   jax.ShapeDtypeStruct((1, d), F32)],
        scratch_shapes=[pltpu.VMEM((d, d), BF16), pltpu.VMEM((d, d), BF16), pltpu.VMEM((d, d), BF16),
                        pltpu.SemaphoreType.DMA((3,))],
        compiler_params=_params(1),
    )(h, dh, dq, dkt, dvt, dfl, g_norm, wq, wk, wv, wf)


def _loss_head(h, target, g_final, tm):
    s_len, d = h.shape
    n_blk = s_len // tm

    def body(h_ref, t_ref, g_ref, dh_ref, loss_ref, gg_ref, sq):
        i = pl.program_id(0)

        @pl.when(i == 0)
        def _():
            sq[...] = jnp.zeros_like(sq)
            gg_ref[...] = jnp.zeros_like(gg_ref)

        g = g_ref[...]
        xhat, r, y = _rms_fwd(h_ref[...], g)
        err = y - t_ref[...]
        sq[...] += jnp.sum(err * err, axis=0, keepdims=True)
        dx, gg = _rms_bwd(err * (1.0 / d), xhat, r, g)
        gg_ref[...] += jnp.sum(gg, axis=0, keepdims=True)
        dh_ref[...] = dx

        @pl.when(i == n_blk - 1)
        def _():
            loss_ref[...] = jnp.broadcast_to(jnp.sum(sq[...], axis=1, keepdims=True) * (0.5 / d), loss_ref.shape)

    row = lambda i: (i, 0)
    const2 = lambda i: (0, 0)
    return pl.pallas_call(
        body, name="loss_head", grid=(n_blk,),
        in_specs=[pl.BlockSpec((tm, d), row), pl.BlockSpec((tm, d), row), pl.BlockSpec((1, d), const2)],
        out_specs=[pl.BlockSpec((tm, d), row), pl.BlockSpec((1, LANES), const2), pl.BlockSpec((1, d), const2)],
        out_shape=[jax.ShapeDtypeStruct((s_len, d), F32), jax.ShapeDtypeStruct((1, LANES), F32),
                   jax.ShapeDtypeStruct((1, d), F32)],
        scratch_shapes=[pltpu.VMEM((1, d), F32)],
        compiler_params=_params(1),
    )(h, target, g_final)


def _wgrad(x, dy, n_piece, name):
    s_len, k = x.shape
    n = dy.shape[1]
    tn = min(n // n_piece, 1024)
    tk = min(k, 1024)
    ts = 512
    per_piece = (n // n_piece) // tn

    def body(x_ref, dy_ref, o_ref):
        @pl.when(pl.program_id(2) == 0)
        def _():
            o_ref[...] = jnp.zeros_like(o_ref)
        o_ref[0] += _dot_tn(x_ref[...].astype(BF16), dy_ref[...].astype(BF16))

    return pl.pallas_call(
        body, name=name, grid=(k // tk, n // tn, s_len // ts),
        in_specs=[pl.BlockSpec((ts, tk), lambda a, b, c: (c, a)), pl.BlockSpec((ts, tn), lambda a, b, c: (c, b))],
        out_specs=pl.BlockSpec((1, tk, tn), lambda a, b, c: (b // per_piece, a, b % per_piece)),
        out_shape=jax.ShapeDtypeStruct((n_piece, k, n // n_piece), F32),
        compiler_params=_params(3),
    )(x, dy)


def _pair_sum(g, recv, core, name):
    n_piece, rows, c = g.shape
    half = rows // 2
    tr = min(half, 512)
    nb = half // tr

    def body(core_ref, g_ref, r_ref, o_ref):
        o_ref[...] = g_ref[...] + r_ref[...]

    blk = pl.BlockSpec((1, tr, c), lambda p, i, core_ref: (p, i, 0))
    return pl.pallas_call(
        body, name=name, out_shape=jax.ShapeDtypeStruct((n_piece, half, c), F32),
        grid_spec=pltpu.PrefetchScalarGridSpec(
            num_scalar_prefetch=1, grid=(n_piece, nb),
            in_specs=[pl.BlockSpec((1, tr, c), lambda p, i, core_ref: (p, core_ref[0] * nb + i, 0)), blk],
            out_specs=blk),
        compiler_params=_params(2),
    )(core, g, recv)


def _chip_sum(halves, recv, chip, name):
    _, h, c = halves.shape
    tr = min(h, 512)

    def body(chip_ref, own_ref, r_ref, o_ref):
        o_ref[...] = ((own_ref[0] + r_ref[0]) + r_ref[1]) + r_ref[2]

    return pl.pallas_call(
        body, name=name, out_shape=jax.ShapeDtypeStruct((h, c), F32),
        grid_spec=pltpu.PrefetchScalarGridSpec(
            num_scalar_prefetch=1, grid=(h // tr,),
            in_specs=[pl.BlockSpec((1, tr, c), lambda i, chip_ref: (chip_ref[0], i, 0)),
                      pl.BlockSpec((3, tr, c), lambda i, chip_ref: (0, i, 0))],
            out_specs=pl.BlockSpec((tr, c), lambda i, chip_ref: (i, 0))),
        compiler_params=_params(1),
    )(chip, halves, recv)


def _adamw_math(w, m, v, g):
    m_new = ADAM_B1 * m + (1.0 - ADAM_B1) * g
    v_new = ADAM_B2 * v + (1.0 - ADAM_B2) * (g * g)
    m_hat = m_new / (1.0 - ADAM_B1 ** ADAM_STEP)
    v_hat = v_new / (1.0 - ADAM_B2 ** ADAM_STEP)
    return -ADAM_LR * (m_hat / (jnp.sqrt(v_hat) + ADAM_EPS) + ADAM_WD * w), m_new, v_new


def _adamw(w, m, v, g, name):
    rows, c = w.shape
    tr = min(rows, 256)

    def body(w_ref, m_ref, v_ref, g_ref, d_ref, mo_ref, vo_ref):
        d_ref[...], mo_ref[...], vo_ref[...] = _adamw_math(w_ref[...], m_ref[...], v_ref[...], g_ref[...])

    spec = pl.BlockSpec((tr, c), lambda i: (i, 0))
    shape = jax.ShapeDtypeStruct((rows, c), F32)
    return pl.pallas_call(
        body, name=name, grid=(rows // tr,),
        in_specs=[spec] * 4, out_specs=[spec] * 3, out_shape=[shape] * 3,
        compiler_params=_params(1),
    )(w, m, v, g)


def _adamw_halves(w, m, v, g_own, g_other, core, name):
    rows, c = w.shape
    half = rows // 2
    tr = min(half, 256)
    nb = half // tr

    def body(core_ref, w_ref, m_ref, v_ref, own_ref, other_ref, g_ref, d_ref, mo_ref, vo_ref):
        mine = (pl.program_id(0) // nb) == core_ref[0]
        g = jnp.where(mine, own_ref[...], other_ref[...])
        g_ref[...] = g
        d_ref[...], mo_ref[...], vo_ref[...] = _adamw_math(w_ref[...], m_ref[...], v_ref[...], g)

    spec = pl.BlockSpec((tr, c), lambda i, core_ref: (i, 0))
    own = pl.BlockSpec((tr, c), lambda i, core_ref: (jnp.clip(i - core_ref[0] * nb, 0, nb - 1), 0))
    other = pl.BlockSpec((tr, c), lambda i, core_ref: (jnp.clip(i - (1 - core_ref[0]) * nb, 0, nb - 1), 0))
    shape = jax.ShapeDtypeStruct((rows, c), F32)
    return pl.pallas_call(
        body, name=name, out_shape=[shape] * 4,
        grid_spec=pltpu.PrefetchScalarGridSpec(
            num_scalar_prefetch=1, grid=(rows // tr,),
            in_specs=[spec, spec, spec, own, other], out_specs=[spec] * 4),
        compiler_params=_params(1),
    )(core, w, m, v, g_own, g_other)


def _place():
    x, y, c = lax.axis_index("x"), lax.axis_index("y"), lax.axis_index("c")
    chips = [(1 - x, y), (x, 1 - y), (1 - x, 1 - y)]
    return x, y, c, chips


def _all_gather_chips(shards):
    n = len(shards)
    halves = [s.shape[0] // 2 for s in shards]

    def half_of(ref, k, which):
        start = which * halves[k]
        if halves[k] % 8 == 0:
            start = pl.multiple_of(start, 8)
        return ref.at[pl.ds(start, halves[k])]

    def over_ici(*refs):
        ins, outs = refs[:n], refs[n:2 * n]
        send_sems, recv_sems = refs[2 * n:]
        x, y, c, chips = _place()
        mine = 2 * x + y
        sends = []
        for k in range(n):
            sends.append(pltpu.make_async_remote_copy(
                src_ref=ins[k], dst_ref=outs[k].at[mine], send_sem=send_sems.at[k, 3], recv_sem=recv_sems.at[k, 3],
                device_id=(x, y, 1 - c), device_id_type=MESH))
            for j, (tx, ty) in enumerate(chips):
                sends.append(pltpu.make_async_remote_copy(
                    src_ref=half_of(ins[k], k, c), dst_ref=half_of(outs[k].at[mine], k, c),
                    send_sem=send_sems.at[k, j], recv_sem=recv_sems.at[k, j],
                    device_id=(tx, ty, c), device_id_type=MESH))
        for cp in sends:
            cp.start()
        for cp in sends:
            cp.wait()

    gathered = pl.pallas_call(
        over_ici, name="weights_gather_ici",
        in_specs=[ANY] * n, out_specs=[ANY] * n,
        out_shape=[jax.ShapeDtypeStruct((4,) + s.shape, s.dtype) for s in shards],
        scratch_shapes=[pltpu.SemaphoreType.DMA((n, 4)), pltpu.SemaphoreType.DMA((n, 4))],
    )(*shards)

    def over_d2d(*refs):
        ins, outs = refs[:n], refs[n:2 * n]
        send_sems, recv_sems = refs[2 * n:]
        x, y, c, chips = _place()
        sends = []
        for k in range(n):
            for j, (tx, ty) in enumerate(chips):
                piece = half_of(outs[k].at[2 * tx + ty], k, c)
                sends.append(pltpu.make_async_remote_copy(
                    src_ref=piece, dst_ref=piece, send_sem=send_sems.at[k, j], recv_sem=recv_sems.at[k, j],
                    device_id=(x, y, 1 - c), device_id_type=MESH))
        for cp in sends:
            cp.start()
        for cp in sends:
            cp.wait()

    return pl.pallas_call(
        over_d2d, name="weights_gather_pair",
        in_specs=[ANY] * n, out_specs=[ANY] * n,
        out_shape=[jax.ShapeDtypeStruct(g.shape, g.dtype) for g in gathered],
        input_output_aliases={k: k for k in range(n)},
        scratch_shapes=[pltpu.SemaphoreType.DMA((n, 3)), pltpu.SemaphoreType.DMA((n, 3))],
    )(*gathered)


def _pair_exchange(grads):
    n = len(grads)

    def body(*refs):
        ins, outs = refs[:n], refs[n:2 * n]
        send_sems, recv_sems = refs[2 * n:]
        x, y, c, _ = _place()
        copies = []
        for k in range(n):
            half = grads[k].shape[1] // 2
            other = ins[k].at[:, pl.ds(pl.multiple_of((1 - c) * half, 8), half), :]
            copies.append(pltpu.make_async_remote_copy(
                src_ref=other, dst_ref=outs[k], send_sem=send_sems.at[k], recv_sem=recv_sems.at[k],
                device_id=(x, y, 1 - c), device_id_type=MESH))
        for cp in copies:
            cp.start()
        for cp in copies:
            cp.wait()

    return pl.pallas_call(
        body, name="grads_pair_exchange",
        in_specs=[ANY] * n, out_specs=[ANY] * n,
        out_shape=[jax.ShapeDtypeStruct((4, g.shape[1] // 2, g.shape[2]), F32) for g in grads],
        scratch_shapes=[pltpu.SemaphoreType.DMA((n,)), pltpu.SemaphoreType.DMA((n,))],
    )(*grads)


def _chip_scatter(halves):
    n = len(halves)

    def body(*refs):
        ins, outs = refs[:n], refs[n:2 * n]
        send_sems, recv_sems = refs[2 * n:]
        x, y, c, chips = _place()
        sends = []
        for k in range(n):
            for j, (tx, ty) in enumerate(chips):
                sends.append(pltpu.make_async_remote_copy(
                    src_ref=ins[k].at[2 * tx + ty], dst_ref=outs[k].at[j], send_sem=send_sems.at[k, j],
                    recv_sem=recv_sems.at[k, j], device_id=(tx, ty, c), device_id_type=MESH))
        for cp in sends:
            cp.start()
        for cp in sends:
            cp.wait()

    return pl.pallas_call(
        body, name="grads_chip_scatter",
        in_specs=[ANY] * n, out_specs=[ANY] * n,
        out_shape=[jax.ShapeDtypeStruct((3,) + hv.shape[1:], F32) for hv in halves],
        scratch_shapes=[pltpu.SemaphoreType.DMA((n, 3)), pltpu.SemaphoreType.DMA((n, 3))],
    )(*halves)


def _pair_share(finals):
    n = len(finals)

    def body(*refs):
        ins, outs = refs[:n], refs[n:2 * n]
        send_sems, recv_sems = refs[2 * n:]
        x, y, c, _ = _place()
        copies = [pltpu.make_async_remote_copy(
            src_ref=ins[k], dst_ref=outs[k], send_sem=send_sems.at[k], recv_sem=recv_sems.at[k],
            device_id=(x, y, 1 - c), device_id_type=MESH) for k in range(n)]
        for cp in copies:
            cp.start()
        for cp in copies:
            cp.wait()

    return pl.pallas_call(
        body, name="grads_pair_share",
        in_specs=[ANY] * n, out_specs=[ANY] * n,
        out_shape=[jax.ShapeDtypeStruct(fv.shape, F32) for fv in finals],
        scratch_shapes=[pltpu.SemaphoreType.DMA((n,)), pltpu.SemaphoreType.DMA((n,))],
    )(*finals)


def _small_all_reduce(buf):
    rows, c_ = buf.shape

    def body(in_ref, out_ref, pair_buf, slots, send_sems, recv_sems):
        x, y, c, chips = _place()
        mine = 2 * x + y
        pair = pltpu.make_async_remote_copy(
            src_ref=in_ref, dst_ref=pair_buf, send_sem=send_sems.at[0], recv_sem=recv_sems.at[0],
            device_id=(x, y, 1 - c), device_id_type=MESH)
        pair.start()
        pair.wait()
        slots[mine] = in_ref[...] + pair_buf[...]
        sends = [pltpu.make_async_remote_copy(
            src_ref=slots.at[mine], dst_ref=slots.at[mine], send_sem=send_sems.at[1 + j], recv_sem=recv_sems.at[1 + j],
            device_id=(tx, ty, c), device_id_type=MESH) for j, (tx, ty) in enumerate(chips)]
        for cp in sends:
            cp.start()
        for j, (tx, ty) in enumerate(chips):
            pltpu.make_async_remote_copy(
                src_ref=slots.at[mine], dst_ref=slots.at[2 * tx + ty], send_sem=send_sems.at[1 + j],
                recv_sem=recv_sems.at[1 + j], device_id=(tx, ty, c), device_id_type=MESH).wait()
        out_ref[...] = ((slots[0] + slots[1]) + slots[2]) + slots[3]

    vm = pl.BlockSpec(memory_space=pltpu.VMEM)
    return pl.pallas_call(
        body, name="small_all_reduce", in_specs=[vm], out_specs=vm,
        out_shape=jax.ShapeDtypeStruct((rows, c_), F32),
        scratch_shapes=[pltpu.VMEM((rows, c_), F32), pltpu.VMEM((4, rows, c_), F32),
                        pltpu.SemaphoreType.DMA((4,)), pltpu.SemaphoreType.DMA((4,))],
        compiler_params=pltpu.CompilerParams(vmem_limit_bytes=VMEM_LIMIT_V7X),
    )(buf)


def _reduce_scatter(grads):
    core = lax.axis_index("c").astype(jnp.int32).reshape(1)
    chip = (2 * lax.axis_index("x") + lax.axis_index("y")).astype(jnp.int32).reshape(1)
    recv = _pair_exchange(grads)
    halves = [_pair_sum(g, r, core, f"pair_sum_{k}") for k, (g, r) in enumerate(zip(grads, recv))]
    recv = _chip_scatter(halves)
    finals = [_chip_sum(hv, r, chip, f"chip_sum_{k}") for k, (hv, r) in enumerate(zip(halves, recv))]
    return list(zip(finals, _pair_share(finals))), core


PACK_COLS = 1024


def _pack(arrays):
    flat = jnp.concatenate([a.reshape(-1).astype(F32) for a in arrays])
    rows = -(-flat.shape[0] // PACK_COLS)
    rows = -(-rows // 8) * 8
    return jnp.pad(flat, (0, rows * PACK_COLS - flat.shape[0])).reshape(rows, PACK_COLS)


def _unpack(buf, shapes):
    flat = buf.reshape(-1)
    out, at = [], 0
    for shp in shapes:
        size = math.prod(shp)
        out.append(flat[at:at + size].reshape(shp))
        at += size
    return out


def kernel(x, mix_norm_g, ffn_norm_g, gm_w_in, gm_ln_g, gm_ln_b, gm_w_s, gm_b_s, gm_w_out, fox_w_qkvf, fox_b_f, fox_w_o, ffn_w_gate, ffn_w_up, ffn_conv_w, ffn_conv_b, ffn_w_down, final_norm_g, loss_target, m_mix_norm_g, m_ffn_norm_g, m_gm_w_in, m_gm_ln_g, m_gm_ln_b, m_gm_w_s, m_gm_b_s, m_gm_w_out, m_fox_w_qkvf, m_fox_b_f, m_fox_w_o, m_ffn_w_gate, m_ffn_w_up, m_ffn_conv_w, m_ffn_conv_b, m_ffn_w_down, m_final_norm_g, v_mix_norm_g, v_ffn_norm_g, v_gm_w_in, v_gm_ln_g, v_gm_ln_b, v_gm_w_s, v_gm_b_s, v_gm_w_out, v_fox_w_qkvf, v_fox_b_f, v_fox_w_o, v_ffn_w_gate, v_ffn_w_up, v_ffn_conv_w, v_ffn_conv_b, v_ffn_w_down, v_final_norm_g):
    _, s_len, d = x.shape
    e = gm_ln_g.shape[1]
    f = ffn_conv_b.shape[1]
    n_head = fox_b_f.shape[1]
    n_pair = n_head // 2
    gd = e // GM_GROUPS
    qkvf_cols = fox_w_qkvf.shape[2]
    assert d == n_head * HEAD_DIM and d % (2 * LANES) == 0 and s_len % 512 == 0 and gd % LANES == 0
    assert gm_w_s.shape[2] == CHUNK and 4 * qkvf_cols == 3 * d + n_head
    tm = 256
    h0 = x[0]
    target = loss_target[0]

    gathered = _all_gather_chips([
        gm_w_in[0].astype(BF16), gm_w_out[0].astype(BF16), fox_w_qkvf[0].astype(BF16), fox_w_o[0].astype(BF16),
        ffn_w_gate.astype(BF16), ffn_w_up.astype(BF16), ffn_w_down.astype(BF16), ffn_conv_w])
    w_in, w_out4, qkvf4, wo4, wg_all, wu_all, wd_all, cw4 = gathered
    w_out = w_out4.reshape(e, d)
    qkvf = jnp.transpose(qkvf4, (1, 0, 2)).reshape(d, 4 * qkvf_cols)
    wq, wk, wv = qkvf[:, :d], qkvf[:, d:2 * d], qkvf[:, 2 * d:3 * d]
    wf = jnp.pad(qkvf[:, 3 * d:], ((0, 0), (0, LANES - n_head)))
    wo = wo4.reshape(d, d)
    conv_w_full = jnp.transpose(cw4, (1, 2, 0, 3)).reshape(2, 3, f)
    conv_w8 = jnp.pad(conv_w_full, ((0, 0), (0, 5), (0, 0)))
    bf_pad = jnp.pad(fox_b_f, ((0, 0), (0, LANES - n_head)))

    tril = jnp.tril(jnp.ones((CHUNK, CHUNK), bool))
    wc = jnp.where(tril[None], gm_w_s[0], 0.0).astype(BF16)
    wct = jnp.transpose(wc, (0, 2, 1))
    bias = jnp.repeat(gm_b_s[0].T, gd, axis=1)
    seg_groups = (jnp.arange(e)[:, None] // gd == jnp.arange(LANES)[None, :]).astype(BF16)
    seg_heads = (jnp.arange(d)[:, None] // HEAD_DIM == jnp.arange(LANES)[None, :]).astype(BF16)
    sel_q = _spare_selectors(d, key_side=False)
    sel_k = _spare_selectors(d, key_side=True)

    h1, a0, hn0, gated0 = _gmlp_fwd(h0, mix_norm_g[0:1], w_in, gm_ln_g, gm_ln_b, wc, bias, w_out, tm)
    h2, fa0, fup0, fhn0, fhid0 = _ffn_fwd(h1, ffn_norm_g[0:1], wg_all, wu_all, wd_all, 0, conv_w8[0], ffn_conv_b[0:1], tm)
    (hn1, qa, qb, ka, kb_, kat, kbt, va, vb, vat, vbt, z_f) = _fox_proj_fwd(
        h2, mix_norm_g[1:2], wq, wk, wv, wf, bf_pad, sel_q, sel_k, tm)
    o, qa2, qb2, qat2, qbt2 = _flash_fwd(qa, qb, kat, kbt, va, vb)
    h3 = _oproj_fwd(h2, o, wo, tm)
    h4, fa1, fup1, fhn1, fhid1 = _ffn_fwd(h3, ffn_norm_g[1:2], wg_all, wu_all, wd_all, 1, conv_w8[1], ffn_conv_b[1:2], tm)

    dh4, loss_part, g_final = _loss_head(h4, target, final_norm_g.reshape(1, d), tm)
    dh3, da1, dup1, gcw1, gcb1, gfn1 = _ffn_bwd(h3, dh4, fa1, fup1, ffn_norm_g[1:2], wg_all, wu_all, wd_all, 1,
                                                conv_w8[1], ffn_conv_b[1:2], tm)
    g_gate1 = _wgrad(fhn1, da1, 4, "wgrad_gate_1")
    g_up1 = _wgrad(fhn1, dup1, 4, "wgrad_up_1")
    g_down1 = _wgrad(fhid1, dh4, 1, "wgrad_down_1").reshape(4, f // 4, d)

    doa, dob, doat, dobt = _oproj_bwd(dh3, o, wo, seg_heads, sel_q, tm)
    g_wo = _wgrad(o, dh3, 1, "wgrad_wo").reshape(4, d // 4, d)
    dq, dkt, dvt, row_sums, col_sums = _flash_bwd(qa2, qb2, qat2, qbt2, kat, kbt, ka, kb_, vat, vbt,
                                                  doa, dob, doat, dobt)
    col_sums = col_sums[:, ::8, :].reshape(n_head, s_len).T
    dcum = jnp.pad(row_sums[:, ::HEAD_DIM] - col_sums, ((0, 0), (0, LANES - n_head)))
    dfl, g_bf = _forget_bwd(dcum, z_f, tm)
    dh2, dflb, gmn1 = _fox_proj_bwd(h2, dh3, dq, dkt, dvt, dfl, mix_norm_g[1:2], wq, wk, wv, wf, tm)
    g_q = _wgrad(hn1, dq, 1, "wgrad_q")[0]
    g_k = _wgrad_t(dkt, hn1, "wgrad_k").T
    g_v = _wgrad_t(dvt, hn1, "wgrad_v").T
    g_f = _wgrad(hn1, dflb, 1, "wgrad_f")[0][:, :n_head]
    g_qkvf = jnp.concatenate([g_q, g_k, g_v, g_f], axis=1).reshape(d, 4, qkvf_cols).transpose(1, 0, 2)

    dh1, da0f, dup0, gcw0, gcb0, gfn0 = _ffn_bwd(h1, dh2, fa0, fup0, ffn_norm_g[0:1], wg_all, wu_all, wd_all, 0,
                                                 conv_w8[0], ffn_conv_b[0:1], tm)
    g_gate0 = _wgrad(fhn0, da0f, 4, "wgrad_gate_0")
    g_up0 = _wgrad(fhn0, dup0, 4, "wgrad_up_0")
    g_down0 = _wgrad(fhid0, dh2, 1, "wgrad_down_0").reshape(4, f // 4, d)

    dh0, da0, g_ws, g_bs_t, g_lng, g_lnb, gmn0 = _gmlp_bwd(
        h0, dh1, a0, mix_norm_g[0:1], w_in, gm_ln_g, gm_ln_b, wc, wct, bias, w_out, seg_groups, tm)
    g_win = _wgrad(hn0, da0, 4, "wgrad_gm_in")
    g_wout = _wgrad(gated0, dh1, 1, "wgrad_gm_out").reshape(4, e // 4, d)

    big, core = _reduce_scatter([g_win, g_wout, g_qkvf, g_wo, g_gate0, g_gate1, g_up0, g_up1, g_down0, g_down1])
    r_win, r_wout, r_qkvf, r_wo, r_gate0, r_gate1, r_up0, r_up1, r_down0, r_down1 = big

    small = [jnp.concatenate([gmn0, gmn1]), jnp.concatenate([gfn0, gfn1]), g_lng, g_lnb, g_ws[None],
             g_bs_t[:, :GM_GROUPS].T[None], g_bf[:, :n_head], jnp.stack([gcw0[:3], gcw1[:3]]),
             jnp.concatenate([gcb0, gcb1]), g_final.reshape(d), loss_part[0, :1]]
    small_shapes = [a.shape for a in small]
    reduced = _unpack(_small_all_reduce(_pack(small)), small_shapes)
    (r_mix, r_ffn, r_lng, r_lnb, r_ws, r_bs, r_bf, r_cw_full, r_cb, r_final, r_loss) = reduced
    chip = 2 * lax.axis_index("x") + lax.axis_index("y")
    r_cw = lax.dynamic_slice_in_dim(r_cw_full, chip * (f // 4), f // 4, axis=2)

    def update_big(name, w, m, v, per_layer):
        parts = [_adamw_halves(w[l], m[l], v[l], own, other, core, f"adamw_{name}_{l}")
                 for l, (own, other) in enumerate(per_layer)]
        return tuple(jnp.stack([p[i] for p in parts]) for i in range(4))

    res = {}
    res["gm_w_in"] = update_big("gm_w_in", gm_w_in, m_gm_w_in, v_gm_w_in, [r_win])
    res["gm_w_out"] = update_big("gm_w_out", gm_w_out, m_gm_w_out, v_gm_w_out, [r_wout])
    res["fox_w_qkvf"] = update_big("fox_w_qkvf", fox_w_qkvf, m_fox_w_qkvf, v_fox_w_qkvf, [r_qkvf])
    res["fox_w_o"] = update_big("fox_w_o", fox_w_o, m_fox_w_o, v_fox_w_o, [r_wo])
    res["ffn_w_gate"] = update_big("ffn_w_gate", ffn_w_gate, m_ffn_w_gate, v_ffn_w_gate, [r_gate0, r_gate1])
    res["ffn_w_up"] = update_big("ffn_w_up", ffn_w_up, m_ffn_w_up, v_ffn_w_up, [r_up0, r_up1])
    res["ffn_w_down"] = update_big("ffn_w_down", ffn_w_down, m_ffn_w_down, v_ffn_w_down, [r_down0, r_down1])

    small_names = ["mix_norm_g", "ffn_norm_g", "gm_ln_g", "gm_ln_b", "gm_w_s", "gm_b_s", "fox_b_f", "ffn_conv_w",
                   "ffn_conv_b", "final_norm_g"]
    small_w = [mix_norm_g, ffn_norm_g, gm_ln_g, gm_ln_b, gm_w_s, gm_b_s, fox_b_f, ffn_conv_w, ffn_conv_b, final_norm_g]
    small_m = [m_mix_norm_g, m_ffn_norm_g, m_gm_ln_g, m_gm_ln_b, m_gm_w_s, m_gm_b_s, m_fox_b_f, m_ffn_conv_w,
               m_ffn_conv_b, m_final_norm_g]
    small_v = [v_mix_norm_g, v_ffn_norm_g, v_gm_ln_g, v_gm_ln_b, v_gm_w_s, v_gm_b_s, v_fox_b_f, v_ffn_conv_w,
               v_ffn_conv_b, v_final_norm_g]
    small_g = [r_mix, r_ffn, r_lng, r_lnb, r_ws, r_bs, r_bf, r_cw, r_cb, r_final]
    shapes = [w.shape for w in small_w]
    small_g = [g.reshape(s) for g, s in zip(small_g, shapes)]
    dlt, mn, vn = _adamw(_pack(small_w), _pack(small_m), _pack(small_v), _pack(small_g), "adamw_small")
    for name, g, dl_, m_, v_ in zip(small_names, small_g, _unpack(dlt, shapes), _unpack(mn, shapes), _unpack(vn, shapes)):
        res[name] = (g, dl_, m_, v_)

    order = ["mix_norm_g", "ffn_norm_g", "gm_w_in", "gm_ln_g", "gm_ln_b", "gm_w_s", "gm_b_s", "gm_w_out", "fox_w_qkvf",
             "fox_b_f", "fox_w_o", "ffn_w_gate", "ffn_w_up", "ffn_conv_w", "ffn_conv_b", "ffn_w_down", "final_norm_g"]
    outs = [r_loss.reshape(()), dh0[None]]
    for part in range(4):
        outs += [res[name][part] for name in order]
    return tuple(outs)
```

```python
import functools
import math

import jax
import jax.numpy as jnp
from jax import lax
from jax.experimental import pallas as pl
from jax.experimental.pallas import tpu as pltpu

F32 = jnp.float32
BF16 = jnp.bfloat16

RMS_EPS = 1e-6
LN_EPS = 1e-5
CHUNK = 128
GM_GROUPS = 8
HEAD_DIM = 64
LANES = 128
ATT_BLOCK = 256
ATT_CHUNK = 1024
VMEM_LIMIT_V7X = 56 * 1024 * 1024

ADAM_LR = 0.001
ADAM_B1 = 0.9
ADAM_B2 = 0.999
ADAM_EPS = 1e-08
ADAM_WD = 0.01
ADAM_STEP = 10

MESH = pl.DeviceIdType.MESH
ANY = pl.BlockSpec(memory_space=pl.ANY)
NEG_BIG = -1e30


def _params(n_grid):
    return pltpu.CompilerParams(dimension_semantics=("arbitrary",) * n_grid, vmem_limit_bytes=VMEM_LIMIT_V7X)


def _dot(a, b):
    return jnp.dot(a, b, preferred_element_type=F32)


def _dot_nt(a, b):
    return lax.dot_general(a, b, (((1,), (1,)), ((), ())), preferred_element_type=F32)


def _dot_tn(a, b):
    return lax.dot_general(a, b, (((0,), (0,)), ((), ())), preferred_element_type=F32)


def _split3(x):
    hi = x.astype(BF16)
    r = x - hi.astype(F32)
    mid = r.astype(BF16)
    lo = (r - mid.astype(F32)).astype(BF16)
    return hi, mid, lo


def _dot3_lhs(x, m):
    hi, mid, lo = _split3(x)
    return _dot(hi, m) + _dot(mid, m) + _dot(lo, m)


def _dot3_rhs(m, x):
    hi, mid, lo = _split3(x)
    return _dot(m, hi) + _dot(m, mid) + _dot(m, lo)


def _load_once(pairs, sem):
    @pl.when(pl.program_id(0) == 0)
    def _():
        copies = [pltpu.make_async_copy(src, dst, sem.at[k]) for k, (src, dst) in enumerate(pairs)]
        for cp in copies:
            cp.start()
        for cp in copies:
            cp.wait()


def _rms_fwd(x, g):
    r = lax.rsqrt(jnp.mean(x * x, axis=-1, keepdims=True) + RMS_EPS)
    xhat = x * r
    return xhat, r, xhat * g


def _rms_bwd(dy, xhat, r, g):
    w = dy * g
    dx = r * (w - xhat * jnp.mean(w * xhat, axis=-1, keepdims=True))
    return dx, dy * xhat


def _gelu_parts(a):
    c = math.sqrt(2.0 / math.pi)
    a2 = a * a
    t = jnp.tanh(c * (a + 0.044715 * a * a2))
    z = 0.5 * a * (1.0 + t)
    dz = 0.5 * (1.0 + t) + 0.5 * a * (1.0 - t * t) * (c * (1.0 + 3.0 * 0.044715 * a2))
    return z, dz


def _sigmoid(x):
    return 1.0 / (1.0 + jnp.exp(-x))


def _gmlp_core(a, lng, lnb, wc_ref, bias, n_chunk, gd):
    e = a.shape[1] // 2
    z, dz = _gelu_parts(a)
    u = z[:, :e]
    v = z[:, e:]
    mu = jnp.mean(v, axis=-1, keepdims=True)
    vc = v - mu
    rstd = lax.rsqrt(jnp.mean(vc * vc, axis=-1, keepdims=True) + LN_EPS)
    vhat = vc * rstd
    vln = vhat * lng + lnb
    vlb = vln.astype(BF16)
    rows = []
    for ci in range(n_chunk):
        cols = []
        for g in range(GM_GROUPS):
            blk = vlb[ci * CHUNK:(ci + 1) * CHUNK, g * gd:(g + 1) * gd]
            cols.append(_dot(wc_ref[g], blk))
        rows.append(jnp.concatenate(cols, axis=1) + bias)
    s = rows[0] if n_chunk == 1 else jnp.concatenate(rows, axis=0)
    return dz, u, vhat, rstd, vlb, s


def _gmlp_fwd(h, g_mix, w_in, lng, lnb, wc, bias, w_out, tm):
    s_len, d = h.shape
    n_p, _, w = w_in.shape
    e = w_out.shape[0]
    gd = e // GM_GROUPS
    n_chunk = tm // CHUNK

    def body(h_ref, g_ref, win_hbm, lng_ref, lnb_ref, wc_ref, bias_ref, wout_hbm,
             hout_ref, a_ref, hn_ref, gated_ref, win_v, wout_v, sem):
        _load_once([(win_hbm, win_v), (wout_hbm, wout_v)], sem)
        x = h_ref[...]
        _, _, y = _rms_fwd(x, g_ref[...])
        hn = y.astype(BF16)
        hn_ref[...] = hn
        for p in range(n_p):
            a_ref[:, p * w:(p + 1) * w] = _dot(hn, win_v[p])
        _, u, _, _, _, s = _gmlp_core(a_ref[...], lng_ref[...], lnb_ref[...], wc_ref, bias_ref[...], n_chunk, gd)
        gated = (u * s).astype(BF16)
        gated_ref[...] = gated
        hout_ref[...] = x + _dot(gated, wout_v[...])

    row = lambda i: (i, 0)
    const2 = lambda i: (0, 0)
    return pl.pallas_call(
        body, name="gmlp_fwd", grid=(s_len // tm,),
        in_specs=[pl.BlockSpec((tm, d), row), pl.BlockSpec((1, d), const2), ANY,
                  pl.BlockSpec((1, e), const2), pl.BlockSpec((1, e), const2),
                  pl.BlockSpec(wc.shape, lambda i: (0, 0, 0)), pl.BlockSpec((CHUNK, e), const2), ANY],
        out_specs=[pl.BlockSpec((tm, d), row), pl.BlockSpec((tm, 2 * e), row),
                   pl.BlockSpec((tm, d), row), pl.BlockSpec((tm, e), row)],
        out_shape=[jax.ShapeDtypeStruct((s_len, d), F32), jax.ShapeDtypeStruct((s_len, 2 * e), F32),
                   jax.ShapeDtypeStruct((s_len, d), BF16), jax.ShapeDtypeStruct((s_len, e), BF16)],
        scratch_shapes=[pltpu.VMEM(w_in.shape, BF16), pltpu.VMEM(w_out.shape, BF16), pltpu.SemaphoreType.DMA((2,))],
        compiler_params=_params(1),
    )(h, g_mix, w_in, lng, lnb, wc, bias, w_out)


def _gmlp_bwd(h, dh, a, g_mix, w_in, lng, lnb, wc, wct, bias, w_out, seg, tm):
    s_len, d = h.shape
    n_p, _, w = w_in.shape
    e = w_out.shape[0]
    gd = e // GM_GROUPS
    n_chunk = tm // CHUNK
    n_blk = s_len // tm

    def body(h_ref, dh_ref, a_ref, g_ref, win_hbm, lng_ref, lnb_ref, wc_ref, wct_ref, bias_ref, wout_hbm, seg_ref,
             dhin_ref, da_ref, gws_ref, gbs_ref, glng_ref, glnb_ref, gmix_ref, win_v, wout_v, dsum, sem):
        i = pl.program_id(0)
        _load_once([(win_hbm, win_v), (wout_hbm, wout_v)], sem)

        @pl.when(i == 0)
        def _():
            gws_ref[...] = jnp.zeros_like(gws_ref)
            glng_ref[...] = jnp.zeros_like(glng_ref)
            glnb_ref[...] = jnp.zeros_like(glnb_ref)
            gmix_ref[...] = jnp.zeros_like(gmix_ref)
            dsum[...] = jnp.zeros_like(dsum)

        x = h_ref[...]
        dh_v = dh_ref[...]
        g = g_ref[...]
        lng_v = lng_ref[...]
        xhat, r, _ = _rms_fwd(x, g)
        dz_da, u, vhat, rstd, vlb, s = _gmlp_core(a_ref[...], lng_v, lnb_ref[...], wc_ref, bias_ref[...], n_chunk, gd)
        dg = _dot_nt(dh_v.astype(BF16), wout_v[...])
        du = dg * s
        ds = dg * u
        dsb = ds.astype(BF16)
        rows = []
        ds_acc = None
        for ci in range(n_chunk):
            lo, hi = ci * CHUNK, (ci + 1) * CHUNK
            cols = []
            for gi in range(GM_GROUPS):
                d_blk = dsb[lo:hi, gi * gd:(gi + 1) * gd]
                gws_ref[gi] += _dot_nt(d_blk, vlb[lo:hi, gi * gd:(gi + 1) * gd])
                cols.append(_dot(wct_ref[gi], d_blk))
            rows.append(jnp.concatenate(cols, axis=1))
            ds_acc = ds[lo:hi] if ds_acc is None else ds_acc + ds[lo:hi]
        dsum[...] += ds_acc
        dvln = rows[0] if n_chunk == 1 else jnp.concatenate(rows, axis=0)
        glng_ref[...] += jnp.sum(dvln * vhat, axis=0, keepdims=True)
        glnb_ref[...] += jnp.sum(dvln, axis=0, keepdims=True)
        dvhat = dvln * lng_v
        dv = rstd * (dvhat - jnp.mean(dvhat, axis=-1, keepdims=True)
                     - vhat * jnp.mean(dvhat * vhat, axis=-1, keepdims=True))
        da = jnp.concatenate([du, dv], axis=1) * dz_da
        dab = da.astype(BF16)
        da_ref[...] = dab
        dhn = _dot_nt(dab[:, :w], win_v[0])
        for p in range(1, n_p):
            dhn += _dot_nt(dab[:, p * w:(p + 1) * w], win_v[p])
        dx, gg = _rms_bwd(dhn, xhat, r, g)
        gmix_ref[...] += jnp.sum(gg, axis=0, keepdims=True)
        dhin_ref[...] = dh_v + dx

        @pl.when(i == n_blk - 1)
        def _():
            tril = lax.broadcasted_iota(jnp.int32, (CHUNK, CHUNK), 0) >= lax.broadcasted_iota(jnp.int32, (CHUNK, CHUNK), 1)
            for gi in range(GM_GROUPS):
                gws_ref[gi] = jnp.where(tril, gws_ref[gi], 0.0)
            gbs_ref[...] = _dot3_lhs(dsum[...], seg_ref[...])

    row = lambda i: (i, 0)
    const2 = lambda i: (0, 0)
    const3 = lambda i: (0, 0, 0)
    return pl.pallas_call(
        body, name="gmlp_bwd", grid=(n_blk,),
        in_specs=[pl.BlockSpec((tm, d), row), pl.BlockSpec((tm, d), row), pl.BlockSpec((tm, 2 * e), row),
                  pl.BlockSpec((1, d), const2), ANY, pl.BlockSpec((1, e), const2), pl.BlockSpec((1, e), const2),
                  pl.BlockSpec(wc.shape, const3), pl.BlockSpec(wct.shape, const3), pl.BlockSpec((CHUNK, e), const2),
                  ANY, pl.BlockSpec((e, LANES), const2)],
        out_specs=[pl.BlockSpec((tm, d), row), pl.BlockSpec((tm, 2 * e), row), pl.BlockSpec(wc.shape, const3),
                   pl.BlockSpec((CHUNK, LANES), const2), pl.BlockSpec((1, e), const2), pl.BlockSpec((1, e), const2),
                   pl.BlockSpec((1, d), const2)],
        out_shape=[jax.ShapeDtypeStruct((s_len, d), F32), jax.ShapeDtypeStruct((s_len, 2 * e), BF16),
                   jax.ShapeDtypeStruct(wc.shape, F32), jax.ShapeDtypeStruct((CHUNK, LANES), F32),
                   jax.ShapeDtypeStruct((1, e), F32), jax.ShapeDtypeStruct((1, e), F32), jax.ShapeDtypeStruct((1, d), F32)],
        scratch_shapes=[pltpu.VMEM(w_in.shape, BF16), pltpu.VMEM(w_out.shape, BF16), pltpu.VMEM((CHUNK, e), F32),
                        pltpu.SemaphoreType.DMA((2,))],
        compiler_params=_params(1),
    )(h, dh, a, g_mix, w_in, lng, lnb, wc, wct, bias, w_out, seg)


def _shift_down(a, k, fill):
    tm = a.shape[0]
    out = pltpu.roll(a, k, 0)
    rid = lax.broadcasted_iota(jnp.int32, a.shape, 0)
    for j in range(k):
        out = jnp.where(rid == j, fill[8 - k + j:8 - k + j + 1, :], out)
    return out


def _shift_up(a, k, fill):
    tm = a.shape[0]
    out = pltpu.roll(a, tm - k, 0)
    rid = lax.broadcasted_iota(jnp.int32, a.shape, 0)
    for j in range(k):
        out = jnp.where(rid == tm - k + j, fill[j:j + 1, :], out)
    return out


def _ffn_fwd(h, g_norm, wg_all, wu_all, wd_all, layer, conv_w, conv_b, tm):
    s_len, d = h.shape
    n_p = wg_all.shape[0]
    fq = wg_all.shape[3]
    f = n_p * fq

    def body(h_ref, g_ref, wg_hbm, wu_hbm, wd_hbm, cw_ref, cb_ref,
             hout_ref, a_ref, up_ref, hn_ref, hid_ref, wg_v, wu_v, wd_v, carry, sem):
        i = pl.program_id(0)
        _load_once([(wg_hbm.at[:, layer], wg_v), (wu_hbm.at[:, layer], wu_v), (wd_hbm.at[:, layer], wd_v)], sem)

        @pl.when(i == 0)
        def _():
            carry[...] = jnp.zeros_like(carry)

        x = h_ref[...]
        _, _, y = _rms_fwd(x, g_ref[...])
        hn = y.astype(BF16)
        hn_ref[...] = hn
        for p in range(n_p):
            a_ref[:, p * fq:(p + 1) * fq] = _dot(hn, wg_v[p])
            up_ref[:, p * fq:(p + 1) * fq] = _dot(hn, wu_v[p])
        a = a_ref[...]
        prev = carry[...]
        am1 = _shift_down(a, 1, prev)
        am2 = _shift_down(a, 2, prev)
        carry[...] = a[tm - 8:tm, :]
        cw = cw_ref[...]
        ac = cb_ref[...] + am2 * cw[0:1, :]
        ac = ac + am1 * cw[1:2, :]
        ac = ac + a * cw[2:3, :]
        hid = (ac * _sigmoid(ac) * up_ref[...]).astype(BF16)
        hid_ref[...] = hid
        y2 = _dot(hid[:, :fq], wd_v[0])
        for p in range(1, n_p):
            y2 += _dot(hid[:, p * fq:(p + 1) * fq], wd_v[p])
        hout_ref[...] = x + y2

    row = lambda i: (i, 0)
    const2 = lambda i: (0, 0)
    return pl.pallas_call(
        body, name=f"ffn_fwd_{layer}", grid=(s_len // tm,),
        in_specs=[pl.BlockSpec((tm, d), row), pl.BlockSpec((1, d), const2), ANY, ANY, ANY,
                  pl.BlockSpec((8, f), const2), pl.BlockSpec((1, f), const2)],
        out_specs=[pl.BlockSpec((tm, d), row), pl.BlockSpec((tm, f), row), pl.BlockSpec((tm, f), row),
                   pl.BlockSpec((tm, d), row), pl.BlockSpec((tm, f), row)],
        out_shape=[jax.ShapeDtypeStruct((s_len, d), F32), jax.ShapeDtypeStruct((s_len, f), F32),
                   jax.ShapeDtypeStruct((s_len, f), F32), jax.ShapeDtypeStruct((s_len, d), BF16),
                   jax.ShapeDtypeStruct((s_len, f), BF16)],
        scratch_shapes=[pltpu.VMEM((n_p, d, fq), BF16), pltpu.VMEM((n_p, d, fq), BF16), pltpu.VMEM((n_p, fq, d), BF16),
                        pltpu.VMEM((8, f), F32), pltpu.SemaphoreType.DMA((3,))],
        compiler_params=_params(1),
    )(h, g_norm, wg_all, wu_all, wd_all, conv_w, conv_b)


def _ffn_bwd(h, dh, a, up, g_norm, wg_all, wu_all, wd_all, layer, conv_w, conv_b, tm):
    s_len, d = h.shape
    n_p = wg_all.shape[0]
    fq = wg_all.shape[3]
    f = n_p * fq
    n_blk = s_len // tm
    t8 = tm // 8

    def body(h_ref, dh_ref, a_ref, ahalo_ref, up_ref, g_ref, wg_hbm, wu_hbm, wd_hbm, cw_ref, cb_ref,
             dhin_ref, da_ref, dup_ref, gcw_ref, gcb_ref, gn_ref, wg_v, wu_v, wd_v, carry, sem):
        i = pl.program_id(0)
        _load_once([(wg_hbm.at[:, layer], wg_v), (wu_hbm.at[:, layer], wu_v), (wd_hbm.at[:, layer], wd_v)], sem)

        @pl.when(i == 0)
        def _():
            carry[...] = jnp.zeros_like(carry)
            gcw_ref[...] = jnp.zeros_like(gcw_ref)
            gcb_ref[...] = jnp.zeros_like(gcb_ref)
            gn_ref[...] = jnp.zeros_like(gn_ref)

        x = h_ref[...]
        dh_v = dh_ref[...]
        g = g_ref[...]
        xhat, r, _ = _rms_fwd(x, g)
        a = a_ref[...]
        up_v = up_ref[...]
        prev = jnp.where(i == n_blk - 1, 0.0, ahalo_ref[...])
        am1 = _shift_down(a, 1, prev)
        am2 = _shift_down(a, 2, prev)
        cw = cw_ref[...]
        ac = cb_ref[...] + am2 * cw[0:1, :]
        ac = ac + am1 * cw[1:2, :]
        ac = ac + a * cw[2:3, :]
        sg = _sigmoid(ac)
        sil = ac * sg
        dhb = dh_v.astype(BF16)
        dhid = jnp.concatenate([_dot_nt(dhb, wd_v[p]) for p in range(n_p)], axis=1)
        dup = dhid * sil
        dac = dhid * up_v * (sg * (1.0 + ac * (1.0 - sg)))
        gcb_ref[...] += jnp.sum(dac, axis=0, keepdims=True)
        gcw_ref[0:1, :] += jnp.sum(dac * am2, axis=0, keepdims=True)
        gcw_ref[1:2, :] += jnp.sum(dac * am1, axis=0, keepdims=True)
        gcw_ref[2:3, :] += jnp.sum(dac * a, axis=0, keepdims=True)
        nxt = carry[...]
        dp1 = _shift_up(dac, 1, nxt)
        dp2 = _shift_up(dac, 2, nxt)
        carry[...] = dac[0:8, :]
        da = dac * cw[2:3, :] + dp1 * cw[1:2, :] + dp2 * cw[0:1, :]
        dab = da.astype(BF16)
        dupb = dup.astype(BF16)
        da_ref[...] = dab
        dup_ref[...] = dupb
        dhn = _dot_nt(dab[:, :fq], wg_v[0]) + _dot_nt(dupb[:, :fq], wu_v[0])
        for p in range(1, n_p):
            dhn += _dot_nt(dab[:, p * fq:(p + 1) * fq], wg_v[p]) + _dot_nt(dupb[:, p * fq:(p + 1) * fq], wu_v[p])
        dx, gg = _rms_bwd(dhn, xhat, r, g)
        gn_ref[...] += jnp.sum(gg, axis=0, keepdims=True)
        dhin_ref[...] = dh_v + dx

    rev = lambda i: (n_blk - 1 - i, 0)
    halo = lambda i: (jnp.maximum((n_blk - 1 - i) * t8 - 1, 0), 0)
    const2 = lambda i: (0, 0)
    return pl.pallas_call(
        body, name=f"ffn_bwd_{layer}", grid=(n_blk,),
        in_specs=[pl.BlockSpec((tm, d), rev), pl.BlockSpec((tm, d), rev), pl.BlockSpec((tm, f), rev),
                  pl.BlockSpec((8, f), halo), pl.BlockSpec((tm, f), rev), pl.BlockSpec((1, d), const2), ANY, ANY, ANY,
                  pl.BlockSpec((8, f), const2), pl.BlockSpec((1, f), const2)],
        out_specs=[pl.BlockSpec((tm, d), rev), pl.BlockSpec((tm, f), rev), pl.BlockSpec((tm, f), rev),
                   pl.BlockSpec((8, f), const2), pl.BlockSpec((1, f), const2), pl.BlockSpec((1, d), const2)],
        out_shape=[jax.ShapeDtypeStruct((s_len, d), F32), jax.ShapeDtypeStruct((s_len, f), BF16),
                   jax.ShapeDtypeStruct((s_len, f), BF16), jax.ShapeDtypeStruct((8, f), F32),
                   jax.ShapeDtypeStruct((1, f), F32), jax.ShapeDtypeStruct((1, d), F32)],
        scratch_shapes=[pltpu.VMEM((n_p, d, fq), BF16), pltpu.VMEM((n_p, d, fq), BF16), pltpu.VMEM((n_p, fq, d), BF16),
                        pltpu.VMEM((8, f), F32), pltpu.SemaphoreType.DMA((3,))],
        compiler_params=_params(1),
    )(h, dh, a, a, up, g_norm, wg_all, wu_all, wd_all, conv_w, conv_b)


def _even_head_lanes(shape, axis):
    return (lax.broadcasted_iota(jnp.int32, shape, axis) & HEAD_DIM) == 0


def _pair_select(lo, hi, shape):
    return jnp.where(lax.broadcasted_iota(jnp.int32, shape, 1) < HEAD_DIM, lo, hi)


def _causal(row0, col0, shape):
    return row0 + lax.broadcasted_iota(jnp.int32, shape, 0) >= col0 + lax.broadcasted_iota(jnp.int32, shape, 1)


N_SPARE = 3


def _spare_selectors(d, key_side):
    lane = jnp.arange(d)[None, :]
    row = jnp.arange(N_SPARE * LANES)[:, None]
    head, part = row % LANES, row // LANES
    off = N_SPARE if key_side else 0
    sel_a = ((head % 2 == 0) & (lane == LANES * (head // 2) + HEAD_DIM + off + part)).astype(F32)
    sel_b = ((head % 2 == 1) & (lane == LANES * (head // 2) + off + part)).astype(F32)
    sign = -1.0 if key_side else 1.0
    ones_off = 0 if key_side else N_SPARE
    in_pair = jnp.arange(d)[None, :] % LANES
    ones_a = ((in_pair >= HEAD_DIM + ones_off) & (in_pair < HEAD_DIM + ones_off + N_SPARE)).astype(F32)
    ones_b = ((in_pair >= ones_off) & (in_pair < ones_off + N_SPARE)).astype(F32)
    return (sign * sel_a).astype(BF16), (sign * sel_b).astype(BF16), ones_a, ones_b


def _parts(x):
    return jnp.concatenate(_split3(x), axis=1)


def _fox_proj_fwd(h, g_norm, wq, wk, wv, wf, bf, sel_q, sel_k, tm):
    s_len, d = h.shape
    sq_a, sq_b, oq_a, oq_b = sel_q
    sk_a, sk_b, ok_a, ok_b = sel_k

    def body(h_ref, g_ref, wq_hbm, wk_hbm, wv_hbm, wf_ref, bf_ref, sqa_ref, sqb_ref, oqa_ref, oqb_ref,
             ska_ref, skb_ref, oka_ref, okb_ref,
             hn_ref, qa_ref, qb_ref, ka_ref, kb_ref, kat_ref, kbt_ref, va_ref, vb_ref, vat_ref, vbt_ref, z_ref,
             wq_v, wk_v, wv_v, total, sem):
        i = pl.program_id(0)
        _load_once([(wq_hbm, wq_v), (wk_hbm, wk_v), (wv_hbm, wv_v)], sem)

        @pl.when(i == 0)
        def _():
            total[...] = jnp.zeros_like(total)

        x = h_ref[...]
        _, _, y = _rms_fwd(x, g_ref[...])
        hn = y.astype(BF16)
        hn_ref[...] = hn
        z = _dot(hn, wf_ref[...]) + bf_ref[...]
        z_ref[...] = z
        logf = jnp.minimum(z, 0.0) - jnp.log(1.0 + jnp.exp(-jnp.abs(z)))
        tri = (lax.broadcasted_iota(jnp.int32, (tm, tm), 0) >= lax.broadcasted_iota(jnp.int32, (tm, tm), 1))
        cum = _dot3_rhs(jnp.where(tri, 1.0, 0.0).astype(BF16), logf) + total[0:1, :]
        total[...] = jnp.broadcast_to(cum[tm - 1:tm, :], total.shape)
        parts = _parts(cum)

        even = _even_head_lanes((tm, d), 1)
        q = _dot(hn, wq_v[...]) * (HEAD_DIM ** -0.5)
        qa_ref[...] = jnp.where(even, q, _dot(parts, sqa_ref[...]) + oqa_ref[...]).astype(BF16)
        qb_ref[...] = jnp.where(even, _dot(parts, sqb_ref[...]) + oqb_ref[...], q).astype(BF16)
        k = _dot(hn, wk_v[...])
        ka = jnp.where(even, k, _dot(parts, ska_ref[...]) + oka_ref[...])
        kb = jnp.where(even, _dot(parts, skb_ref[...]) + okb_ref[...], k)
        ka_ref[...] = ka.astype(BF16)
        kb_ref[...] = kb.astype(BF16)
        kat_ref[...] = ka.T.astype(BF16)
        kbt_ref[...] = kb.T.astype(BF16)
        v = _dot(hn, wv_v[...])
        va = jnp.where(even, v, oka_ref[...])
        vb = jnp.where(even, okb_ref[...], v)
        va_ref[...] = va.astype(BF16)
        vb_ref[...] = vb.astype(BF16)
        vat_ref[...] = va.T.astype(BF16)
        vbt_ref[...] = vb.T.astype(BF16)

    row = lambda i: (i, 0)
    col = lambda i: (0, i)
    const2 = lambda i: (0, 0)
    sd = jax.ShapeDtypeStruct((s_len, d), BF16)
    ds_ = jax.ShapeDtypeStruct((d, s_len), BF16)
    rs, cs = pl.BlockSpec((tm, d), row), pl.BlockSpec((d, tm), col)
    sel = pl.BlockSpec((N_SPARE * LANES, d), const2)
    one = pl.BlockSpec((1, d), const2)
    return pl.pallas_call(
        body, name="fox_proj_fwd", grid=(s_len // tm,),
        in_specs=[rs, one, ANY, ANY, ANY, pl.BlockSpec((d, LANES), const2), pl.BlockSpec((1, LANES), const2),
                  sel, sel, one, one, sel, sel, one, one],
        out_specs=[rs, rs, rs, rs, rs, cs, cs, rs, rs, cs, cs, pl.BlockSpec((tm, LANES), row)],
        out_shape=[sd, sd, sd, sd, sd, ds_, ds_, sd, sd, ds_, ds_, jax.ShapeDtypeStruct((s_len, LANES), F32)],
        scratch_shapes=[pltpu.VMEM((d, d), BF16), pltpu.VMEM((d, d), BF16), pltpu.VMEM((d, d), BF16),
                        pltpu.VMEM((8, LANES), F32), pltpu.SemaphoreType.DMA((3,))],
        compiler_params=_params(1),
    )(h, g_norm, wq, wk, wv, wf, bf, sq_a, sq_b, oq_a, oq_b, sk_a, sk_b, ok_a, ok_b)


def _spare_cols(x, base):
    xf = x[:, base:base + N_SPARE].astype(F32)
    return xf[:, 0:1] + xf[:, 1:2] + xf[:, 2:3]


def _with_query_term(x, term, base):
    lane = lax.broadcasted_iota(jnp.int32, x.shape, 1)
    hi, mid, lo = _split3(term)
    out = jnp.where(lane == base, hi.astype(F32), x)
    out = jnp.where(lane == base + 1, mid.astype(F32), out)
    out = jnp.where(lane == base + 2, lo.astype(F32), out)
    return jnp.where((lane >= base + N_SPARE) & (lane < base + 2 * N_SPARE), 1.0, out)


def _flash_fwd(qa, qb, kat, kbt, va, vb):
    s_len, d = qa.shape
    sub = ATT_BLOCK
    n_sub = 2 if s_len % (2 * sub) == 0 else 1
    t = n_sub * sub
    w = min(ATT_CHUNK, s_len)
    n_pair = d // LANES
    n_q = s_len // t
    bases = (HEAD_DIM, 0)
    chains = [(r, hh) for r in range(n_sub) for hh in range(2)]

    def body(qa_ref, qb_ref, kat_ref, kbt_ref, va_ref, vb_ref, o_ref, qa2_ref, qb2_ref, qat2_ref, qbt2_ref):
        i = pl.program_id(1)
        q_refs = (qa_ref, qb_ref)
        qs = [q_refs[hh][r * sub:(r + 1) * sub, :] for r, hh in chains]
        kts = (kat_ref, kbt_ref)
        vs = (va_ref, vb_ref)

        def step(kb, carry, masked):
            off = pl.multiple_of(kb * w, w)
            scores = [_dot(qs[c], kts[hh][:, pl.ds(off, w)]) for c, (r, hh) in enumerate(chains)]
            probs, stats = [], []
            for c, (r, hh) in enumerate(chains):
                m, _ = carry[c]
                s = scores[c]
                if masked:
                    s = jnp.where(_causal(i * t + r * sub, off, (sub, w)), s, NEG_BIG)
                m_new = jnp.maximum(m, jnp.max(s, axis=1, keepdims=True))
                probs.append(jnp.exp(s - m_new).astype(BF16))
                stats.append((m_new, jnp.exp(m - m_new)))
            return tuple((stats[c][0], carry[c][1] * stats[c][1] + _dot(probs[c], vs[hh][pl.ds(off, w), :]))
                         for c, (r, hh) in enumerate(chains))

        init = ((jnp.full((sub, 1), NEG_BIG, F32), jnp.zeros((sub, LANES), F32)),) * len(chains)
        diag = (i * t) // w
        carry = lax.fori_loop(0, diag, lambda kb, c: step(kb, c, False), init)
        carry = step(diag, carry, True)
        for r in range(n_sub):
            outs, q2 = [], []
            for hh in range(2):
                m, acc = carry[2 * r + hh]
                l = acc[:, bases[hh]:bases[hh] + 1]
                outs.append(acc / l)
                term = _spare_cols(qs[2 * r + hh], bases[hh]) - (m + jnp.log(l))
                q2.append(_with_query_term(qs[2 * r + hh].astype(F32), term, bases[hh]))
            rows = slice(r * sub, (r + 1) * sub)
            o_ref[rows, :] = _pair_select(outs[0], outs[1], (sub, LANES))
            qa2_ref[rows, :] = q2[0].astype(BF16)
            qb2_ref[rows, :] = q2[1].astype(BF16)
            qat2_ref[:, rows] = q2[0].T.astype(BF16)
            qbt2_ref[:, rows] = q2[1].T.astype(BF16)

    qblk = pl.BlockSpec((t, LANES), lambda j, i: (i, j))
    qblk_t = pl.BlockSpec((LANES, t), lambda j, i: (j, i))
    whole_t = pl.BlockSpec((LANES, s_len), lambda j, i: (j, 0))
    whole = pl.BlockSpec((s_len, LANES), lambda j, i: (0, j))
    sd = jax.ShapeDtypeStruct((s_len, d), BF16)
    ds_ = jax.ShapeDtypeStruct((d, s_len), BF16)
    return pl.pallas_call(
        body, name="flash_fwd", grid=(n_pair, n_q),
        in_specs=[qblk, qblk, whole_t, whole_t, whole, whole],
        out_specs=[qblk, qblk, qblk, qblk_t, qblk_t],
        out_shape=[jax.ShapeDtypeStruct((s_len, d), F32), sd, sd, ds_, ds_],
        compiler_params=_params(2),
    )(qa, qb, kat, kbt, va, vb)


def _flash_bwd(qa, qb, qat, qbt, kat, kbt, ka, kb_, vat, vbt, doa, dob, doat, dobt):
    s_len, d = qa.shape
    t = ATT_BLOCK
    w = min(ATT_CHUNK, s_len)
    n_pair = d // LANES
    n_q = s_len // t

    def body(qa_ref, qb_ref, qat_ref, qbt_ref, kat_hbm, kbt_hbm, ka_hbm, kb_hbm, vat_hbm, vbt_hbm,
             doa_ref, dob_ref, doat_ref, dobt_ref,
             dq_ref, dkt_ref, dvt_ref, rs_ref, cs_ref,
             kat_v, kbt_v, ka_v, kb_v, vat_v, vbt_v, dkt_acc, dvt_acc, cs_acc, sem):
        j = pl.program_id(0)
        i = pl.program_id(1)

        @pl.when(i == 0)
        def _():
            rows = pl.ds(pl.multiple_of(j * LANES, LANES), LANES)
            copies = [pltpu.make_async_copy(src, dst, sem.at[n]) for n, (src, dst) in enumerate([
                (kat_hbm.at[rows, :], kat_v), (kbt_hbm.at[rows, :], kbt_v), (ka_hbm.at[:, rows], ka_v),
                (kb_hbm.at[:, rows], kb_v), (vat_hbm.at[rows, :], vat_v), (vbt_hbm.at[rows, :], vbt_v)])]
            for cp in copies:
                cp.start()
            dkt_acc[...] = jnp.zeros_like(dkt_acc)
            dvt_acc[...] = jnp.zeros_like(dvt_acc)
            cs_acc[...] = jnp.zeros_like(cs_acc)
            for cp in copies:
                cp.wait()

        qs = (qa_ref[...], qb_ref[...])
        dos = (doa_ref[...], dob_ref[...])
        first = lax.broadcasted_iota(jnp.int32, (LANES, t), 0) < HEAD_DIM
        zero = jnp.zeros((LANES, t), BF16)
        qts = (jnp.where(first, qat_ref[...], zero), jnp.where(first, zero, qbt_ref[...]))
        dots = (jnp.where(first, doat_ref[...], zero), jnp.where(first, zero, dobt_ref[...]))
        ones = jnp.ones((8, t), BF16)
        kts, ks, vts = (kat_v, kbt_v), (ka_v, kb_v), (vat_v, vbt_v)

        def step(kb, carry, masked):
            off = pl.multiple_of(kb * w, w)
            cols = pl.ds(off, w)
            scores = [_dot(qs[hh], kts[hh][:, cols]) for hh in range(2)]
            dps = [_dot(dos[hh], vts[hh][:, cols]) for hh in range(2)]
            ps, dss = [], []
            for hh in range(2):
                s = scores[hh]
                if masked:
                    s = jnp.where(_causal(i * t, off, (t, w)), s, NEG_BIG)
                p = jnp.exp(s)
                dss.append((p * dps[hh]).astype(BF16))
                ps.append(p.astype(BF16))
            out = []
            for hh in range(2):
                out.append(carry[hh] + _dot(dss[hh], ks[hh][cols, :]))
                dvt_acc[:, cols] += _dot(dots[hh], ps[hh])
                dkt_acc[:, cols] += _dot(qts[hh], dss[hh])
                cs_acc[8 * hh:8 * hh + 8, cols] += _dot(ones, dss[hh])
            return tuple(out)

        diag = (i * t) // w
        carry = lax.fori_loop(0, diag, lambda kb, c: step(kb, c, False), (jnp.zeros((t, LANES), F32),) * 2)
        acc = step(diag, carry, True)
        dq_ref[...] = (_pair_select(acc[0], acc[1], (t, LANES)) * (HEAD_DIM ** -0.5)).astype(BF16)
        rs_ref[...] = _pair_select(acc[0][:, HEAD_DIM:HEAD_DIM + 1], acc[1][:, 0:1], (t, LANES))

        @pl.when(i == n_q - 1)
        def _():
            dkt_ref[...] = dkt_acc[...].astype(BF16)
            dvt_ref[...] = dvt_acc[...].astype(BF16)
            cs_ref[0] = cs_acc[...]

    qblk = pl.BlockSpec((t, LANES), lambda j, i: (i, j))
    qblk_t = pl.BlockSpec((LANES, t), lambda j, i: (j, i))
    whole_t = pl.BlockSpec((LANES, s_len), lambda j, i: (j, 0))
    ds_ = jax.ShapeDtypeStruct((d, s_len), BF16)
    return pl.pallas_call(
        body, name="flash_bwd", grid=(n_pair, n_q),
        in_specs=[qblk, qblk, qblk_t, qblk_t, ANY, ANY, ANY, ANY, ANY, ANY, qblk, qblk, qblk_t, qblk_t],
        out_specs=[qblk, whole_t, whole_t, qblk, pl.BlockSpec((1, 16, s_len), lambda j, i: (j, 0, 0))],
        out_shape=[jax.ShapeDtypeStruct((s_len, d), BF16), ds_, ds_, jax.ShapeDtypeStruct((s_len, d), F32),
                   jax.ShapeDtypeStruct((n_pair, 16, s_len), F32)],
        scratch_shapes=[pltpu.VMEM((LANES, s_len), BF16), pltpu.VMEM((LANES, s_len), BF16),
                        pltpu.VMEM((s_len, LANES), BF16), pltpu.VMEM((s_len, LANES), BF16),
                        pltpu.VMEM((LANES, s_len), BF16), pltpu.VMEM((LANES, s_len), BF16),
                        pltpu.VMEM((LANES, s_len), F32), pltpu.VMEM((LANES, s_len), F32),
                        pltpu.VMEM((16, s_len), F32), pltpu.SemaphoreType.DMA((6,))],
        compiler_params=_params(2),
    )(qa, qb, qat, qbt, kat, kbt, ka, kb_, vat, vbt, doa, dob, doat, dobt)


def _wgrad_t(at, b, name):
    k, s_len = at.shape
    n = b.shape[1]
    tn, tk, ts = min(n, 1024), min(k, 1024), 512

    def body(a_ref, b_ref, o_ref):
        @pl.when(pl.program_id(2) == 0)
        def _():
            o_ref[...] = jnp.zeros_like(o_ref)
        o_ref[...] += _dot(a_ref[...].astype(BF16), b_ref[...].astype(BF16))

    return pl.pallas_call(
        body, name=name, grid=(k // tk, n // tn, s_len // ts),
        in_specs=[pl.BlockSpec((tk, ts), lambda a, b_, c: (a, c)), pl.BlockSpec((ts, tn), lambda a, b_, c: (c, b_))],
        out_specs=pl.BlockSpec((tk, tn), lambda a, b_, c: (a, b_)),
        out_shape=jax.ShapeDtypeStruct((k, n), F32),
        compiler_params=_params(3),
    )(at, b)


def _oproj_bwd(dh, o, wo, seg, sel_q, tm):
    s_len, d = dh.shape
    sq_a, sq_b, _, _ = sel_q

    def body(dh_ref, o_ref, wo_hbm, seg_ref, sqa_ref, sqb_ref, doa_ref, dob_ref, doat_ref, dobt_ref, wo_v, sem):
        _load_once([(wo_hbm, wo_v)], sem)
        do = _dot_nt(dh_ref[...].astype(BF16), wo_v[...])
        parts = _parts(-_dot3_lhs(do * o_ref[...], seg_ref[...]))
        even = _even_head_lanes((tm, d), 1)
        doa = jnp.where(even, do, _dot(parts, sqa_ref[...]))
        dob = jnp.where(even, _dot(parts, sqb_ref[...]), do)
        doa_ref[...] = doa.astype(BF16)
        dob_ref[...] = dob.astype(BF16)
        doat_ref[...] = doa.T.astype(BF16)
        dobt_ref[...] = dob.T.astype(BF16)

    row = lambda i: (i, 0)
    const2 = lambda i: (0, 0)
    rs, cs = pl.BlockSpec((tm, d), row), pl.BlockSpec((d, tm), lambda i: (0, i))
    sel = pl.BlockSpec((N_SPARE * LANES, d), const2)
    sd = jax.ShapeDtypeStruct((s_len, d), BF16)
    ds_ = jax.ShapeDtypeStruct((d, s_len), BF16)
    return pl.pallas_call(
        body, name="oproj_bwd", grid=(s_len // tm,),
        in_specs=[rs, rs, ANY, pl.BlockSpec((d, LANES), const2), sel, sel],
        out_specs=[rs, rs, cs, cs], out_shape=[sd, sd, ds_, ds_],
        scratch_shapes=[pltpu.VMEM((d, d), BF16), pltpu.SemaphoreType.DMA((1,))],
        compiler_params=_params(1),
    )(dh, o, wo, seg, sq_a, sq_b)


def _oproj_fwd(h, o, wo, tm):
    s_len, d = h.shape

    def body(h_ref, o_ref, wo_hbm, hout_ref, wo_v, sem):
        _load_once([(wo_hbm, wo_v)], sem)
        hout_ref[...] = h_ref[...] + _dot(o_ref[...].astype(BF16), wo_v[...])

    row = lambda i: (i, 0)
    return pl.pallas_call(
        body, name="oproj_fwd", grid=(s_len // tm,),
        in_specs=[pl.BlockSpec((tm, d), row), pl.BlockSpec((tm, d), row), ANY],
        out_specs=pl.BlockSpec((tm, d), row),
        out_shape=jax.ShapeDtypeStruct((s_len, d), F32),
        scratch_shapes=[pltpu.VMEM((d, d), BF16), pltpu.SemaphoreType.DMA((1,))],
        compiler_params=_params(1),
    )(h, o, wo)


def _forget_bwd(dcum, z, tm):
    s_len = dcum.shape[0]
    n_blk = s_len // tm

    def body(dc_ref, z_ref, dfl_ref, gb_ref, total):
        i = pl.program_id(0)

        @pl.when(i == 0)
        def _():
            total[...] = jnp.zeros_like(total)
            gb_ref[...] = jnp.zeros_like(gb_ref)

        upper = (lax.broadcasted_iota(jnp.int32, (tm, tm), 0) <= lax.broadcasted_iota(jnp.int32, (tm, tm), 1))
        suffix = _dot3_rhs(jnp.where(upper, 1.0, 0.0).astype(BF16), dc_ref[...]) + total[0:1, :]
        total[...] = jnp.broadcast_to(suffix[0:1, :], total.shape)
        dfl = suffix * _sigmoid(-z_ref[...])
        dfl_ref[...] = dfl
        gb_ref[...] += jnp.sum(dfl, axis=0, keepdims=True)

    rev = lambda i: (n_blk - 1 - i, 0)
    return pl.pallas_call(
        body, name="forget_bwd", grid=(n_blk,),
        in_specs=[pl.BlockSpec((tm, LANES), rev), pl.BlockSpec((tm, LANES), rev)],
        out_specs=[pl.BlockSpec((tm, LANES), rev), pl.BlockSpec((1, LANES), lambda i: (0, 0))],
        out_shape=[jax.ShapeDtypeStruct((s_len, LANES), F32), jax.ShapeDtypeStruct((1, LANES), F32)],
        scratch_shapes=[pltpu.VMEM((8, LANES), F32)],
        compiler_params=_params(1),
    )(dcum, z)


def _fox_proj_bwd(h, dh, dq, dkt, dvt, dfl, g_norm, wq, wk, wv, wf, tm):
    s_len, d = h.shape

    def body(h_ref, dh_ref, dq_ref, dkt_ref, dvt_ref, dfl_ref, g_ref, wq_hbm, wk_hbm, wv_hbm, wf_ref,
             dhin_ref, dflb_ref, gn_ref, wq_v, wk_v, wv_v, sem):
        _load_once([(wq_hbm, wq_v), (wk_hbm, wk_v), (wv_hbm, wv_v)], sem)

        @pl.when(pl.program_id(0) == 0)
        def _():
            gn_ref[...] = jnp.zeros_like(gn_ref)

        g = g_ref[...]
        xhat, r, _ = _rms_fwd(h_ref[...], g)
        dflb = dfl_ref[...].astype(BF16)
        dflb_ref[...] = dflb
        from_kv = _dot(wk_v[...], dkt_ref[...]) + _dot(wv_v[...], dvt_ref[...])
        dhn = _dot_nt(dq_ref[...], wq_v[...]) + _dot_nt(dflb, wf_ref[...]) + from_kv.T
        dx, gg = _rms_bwd(dhn, xhat, r, g)
        gn_ref[...] += jnp.sum(gg, axis=0, keepdims=True)
        dhin_ref[...] = dh_ref[...] + dx

    row = lambda i: (i, 0)
    const2 = lambda i: (0, 0)
    rs = pl.BlockSpec((tm, d), row)
    cs = pl.BlockSpec((d, tm), lambda i: (0, i))
    return pl.pallas_call(
        body, name="fox_proj_bwd", grid=(s_len // tm,),
        in_specs=[rs, rs, rs, cs, cs, pl.BlockSpec((tm, LANES), row), pl.BlockSpec((1, d), const2), ANY, ANY, ANY,
                  pl.BlockSpec((d, LANES), const2)],
        out_specs=[rs, pl.BlockSpec((tm, LANES), row), pl.BlockSpec((1, d), const2)],
        out_shape=[jax.ShapeDtypeStruct((s_len, d), F32), jax.ShapeDtypeStruct((s_len, LANES), BF16),
                   jax.ShapeDtypeStruct((1, d), F32)],
        scratch_shapes=[pltpu.VMEM((d, d), BF16), pltpu.VMEM((d, d), BF16), pltpu.VMEM((d, d), BF16),
                        pltpu.SemaphoreType.DMA((3,))],
        compiler_params=_params(1),
    )(h, dh, dq, dkt, dvt, dfl, g_norm, wq, wk, wv, wf)


def _loss_head(h, target, g_final, tm):
    s_len, d = h.shape
    n_blk = s_len // tm

    def body(h_ref, t_ref, g_ref, dh_ref, loss_ref, gg_ref, sq):
        i = pl.program_id(0)

        @pl.when(i == 0)
        def _():
            sq[...] = jnp.zeros_like(sq)
            gg_ref[...] = jnp.zeros_like(gg_ref)

        g = g_ref[...]
        xhat, r, y = _rms_fwd(h_ref[...], g)
        err = y - t_ref[...]
        sq[...] += jnp.sum(err * err, axis=0, keepdims=True)
        dx, gg = _rms_bwd(err * (1.0 / d), xhat, r, g)
        gg_ref[...] += jnp.sum(gg, axis=0, keepdims=True)
        dh_ref[...] = dx

        @pl.when(i == n_blk - 1)
        def _():
            loss_ref[...] = jnp.broadcast_to(jnp.sum(sq[...], axis=1, keepdims=True) * (0.5 / d), loss_ref.shape)

    row = lambda i: (i, 0)
    const2 = lambda i: (0, 0)
    return pl.pallas_call(
        body, name="loss_head", grid=(n_blk,),
        in_specs=[pl.BlockSpec((tm, d), row), pl.BlockSpec((tm, d), row), pl.BlockSpec((1, d), const2)],
        out_specs=[pl.BlockSpec((tm, d), row), pl.BlockSpec((1, LANES), const2), pl.BlockSpec((1, d), const2)],
        out_shape=[jax.ShapeDtypeStruct((s_len, d), F32), jax.ShapeDtypeStruct((1, LANES), F32),
                   jax.ShapeDtypeStruct((1, d), F32)],
        scratch_shapes=[pltpu.VMEM((1, d), F32)],
        compiler_params=_params(1),
    )(h, target, g_final)


def _wgrad(x, dy, n_piece, name):
    s_len, k = x.shape
    n = dy.shape[1]
    tn = min(n // n_piece, 1024)
    tk = min(k, 1024)
    ts = 512
    per_piece = (n // n_piece) // tn

    def body(x_ref, dy_ref, o_ref):
        @pl.when(pl.program_id(2) == 0)
        def _():
            o_ref[...] = jnp.zeros_like(o_ref)
        o_ref[0] += _dot_tn(x_ref[...].astype(BF16), dy_ref[...].astype(BF16))

    return pl.pallas_call(
        body, name=name, grid=(k // tk, n // tn, s_len // ts),
        in_specs=[pl.BlockSpec((ts, tk), lambda a, b, c: (c, a)), pl.BlockSpec((ts, tn), lambda a, b, c: (c, b))],
        out_specs=pl.BlockSpec((1, tk, tn), lambda a, b, c: (b // per_piece, a, b % per_piece)),
        out_shape=jax.ShapeDtypeStruct((n_piece, k, n // n_piece), F32),
        compiler_params=_params(3),
    )(x, dy)


def _pair_sum(g, recv, core, name):
    n_piece, rows, c = g.shape
    half = rows // 2
    tr = min(half, 512)
    nb = half // tr

    def body(core_ref, g_ref, r_ref, o_ref, ob_ref):
        total = g_ref[...] + r_ref[...]
        o_ref[...] = total
        ob_ref[...] = total.astype(BF16)

    blk = pl.BlockSpec((1, tr, c), lambda p, i, core_ref: (p, i, 0))
    return pl.pallas_call(
        body, name=name,
        out_shape=[jax.ShapeDtypeStruct((n_piece, half, c), F32), jax.ShapeDtypeStruct((n_piece, half, c), BF16)],
        grid_spec=pltpu.PrefetchScalarGridSpec(
            num_scalar_prefetch=1, grid=(n_piece, nb),
            in_specs=[pl.BlockSpec((1, tr, c), lambda p, i, core_ref: (p, core_ref[0] * nb + i, 0)), blk],
            out_specs=[blk, blk]),
        compiler_params=_params(2),
    )(core, g, recv)


def _chip_sum(halves, recv, chip, name):
    _, h, c = halves.shape
    tr = min(h, 512)

    def body(chip_ref, own_ref, r_ref, o_ref):
        o_ref[...] = ((own_ref[0] + r_ref[0].astype(F32)) + r_ref[1].astype(F32)) + r_ref[2].astype(F32)

    return pl.pallas_call(
        body, name=name, out_shape=jax.ShapeDtypeStruct((h, c), F32),
        grid_spec=pltpu.PrefetchScalarGridSpec(
            num_scalar_prefetch=1, grid=(h // tr,),
            in_specs=[pl.BlockSpec((1, tr, c), lambda i, chip_ref: (chip_ref[0], i, 0)),
                      pl.BlockSpec((3, tr, c), lambda i, chip_ref: (0, i, 0))],
            out_specs=pl.BlockSpec((tr, c), lambda i, chip_ref: (i, 0))),
        compiler_params=_params(1),
    )(chip, halves, recv)


def _adamw_math(w, m, v, g):
    m_new = ADAM_B1 * m + (1.0 - ADAM_B1) * g
    v_new = ADAM_B2 * v + (1.0 - ADAM_B2) * (g * g)
    m_hat = m_new / (1.0 - ADAM_B1 ** ADAM_STEP)
    v_hat = v_new / (1.0 - ADAM_B2 ** ADAM_STEP)
    return -ADAM_LR * (m_hat / (jnp.sqrt(v_hat) + ADAM_EPS) + ADAM_WD * w), m_new, v_new


def _adamw(w, m, v, g, name):
    rows, c = w.shape
    tr = min(rows, 256)

    def body(w_ref, m_ref, v_ref, g_ref, d_ref, mo_ref, vo_ref):
        d_ref[...], mo_ref[...], vo_ref[...] = _adamw_math(w_ref[...], m_ref[...], v_ref[...], g_ref[...])

    spec = pl.BlockSpec((tr, c), lambda i: (i, 0))
    shape = jax.ShapeDtypeStruct((rows, c), F32)
    return pl.pallas_call(
        body, name=name, grid=(rows // tr,),
        in_specs=[spec] * 4, out_specs=[spec] * 3, out_shape=[shape] * 3,
        compiler_params=_params(1),
    )(w, m, v, g)


def _adamw_halves(w, m, v, g_own, g_other, core, name):
    rows, c = w.shape
    half = rows // 2
    tr = min(half, 256)
    nb = half // tr

    def body(core_ref, w_ref, m_ref, v_ref, own_ref, other_ref, g_ref, d_ref, mo_ref, vo_ref):
        mine = (pl.program_id(0) // nb) == core_ref[0]
        g = jnp.where(mine, own_ref[...], other_ref[...])
        g_ref[...] = g
        d_ref[...], mo_ref[...], vo_ref[...] = _adamw_math(w_ref[...], m_ref[...], v_ref[...], g)

    spec = pl.BlockSpec((tr, c), lambda i, core_ref: (i, 0))
    own = pl.BlockSpec((tr, c), lambda i, core_ref: (jnp.clip(i - core_ref[0] * nb, 0, nb - 1), 0))
    other = pl.BlockSpec((tr, c), lambda i, core_ref: (jnp.clip(i - (1 - core_ref[0]) * nb, 0, nb - 1), 0))
    shape = jax.ShapeDtypeStruct((rows, c), F32)
    return pl.pallas_call(
        body, name=name, out_shape=[shape] * 4,
        grid_spec=pltpu.PrefetchScalarGridSpec(
            num_scalar_prefetch=1, grid=(rows // tr,),
            in_specs=[spec, spec, spec, own, other], out_specs=[spec] * 4),
        compiler_params=_params(1),
    )(core, w, m, v, g_own, g_other)


def _place():
    x, y, c = lax.axis_index("x"), lax.axis_index("y"), lax.axis_index("c")
    chips = [(1 - x, y), (x, 1 - y), (1 - x, 1 - y)]
    return x, y, c, chips


def _all_gather_chips(shards):
    n = len(shards)
    halves = [s.shape[0] // 2 for s in shards]

    def half_of(ref, k, which):
        start = which * halves[k]
        if halves[k] % 8 == 0:
            start = pl.multiple_of(start, 8)
        return ref.at[pl.ds(start, halves[k])]

    def over_ici(*refs):
        ins, outs = refs[:n], refs[n:2 * n]
        send_sems, recv_sems = refs[2 * n:]
        x, y, c, chips = _place()
        mine = 2 * x + y
        sends = []
        for k in range(n):
            sends.append(pltpu.make_async_remote_copy(
                src_ref=ins[k], dst_ref=outs[k].at[mine], send_sem=send_sems.at[k, 3], recv_sem=recv_sems.at[k, 3],
                device_id=(x, y, 1 - c), device_id_type=MESH))
            for j, (tx, ty) in enumerate(chips):
                sends.append(pltpu.make_async_remote_copy(
                    src_ref=half_of(ins[k], k, c), dst_ref=half_of(outs[k].at[mine], k, c),
                    send_sem=send_sems.at[k, j], recv_sem=recv_sems.at[k, j],
                    device_id=(tx, ty, c), device_id_type=MESH))
        for cp in sends:
            cp.start()
        for cp in sends:
            cp.wait()

    gathered = pl.pallas_call(
        over_ici, name="weights_gather_ici",
        in_specs=[ANY] * n, out_specs=[ANY] * n,
        out_shape=[jax.ShapeDtypeStruct((4,) + s.shape, s.dtype) for s in shards],
        scratch_shapes=[pltpu.SemaphoreType.DMA((n, 4)), pltpu.SemaphoreType.DMA((n, 4))],
    )(*shards)

    def over_d2d(*refs):
        ins, outs = refs[:n], refs[n:2 * n]
        send_sems, recv_sems = refs[2 * n:]
        x, y, c, chips = _place()
        sends = []
        for k in range(n):
            for j, (tx, ty) in enumerate(chips):
                piece = half_of(outs[k].at[2 * tx + ty], k, c)
                sends.append(pltpu.make_async_remote_copy(
                    src_ref=piece, dst_ref=piece, send_sem=send_sems.at[k, j], recv_sem=recv_sems.at[k, j],
                    device_id=(x, y, 1 - c), device_id_type=MESH))
        for cp in sends:
            cp.start()
        for cp in sends:
            cp.wait()

    return pl.pallas_call(
        over_d2d, name="weights_gather_pair",
        in_specs=[ANY] * n, out_specs=[ANY] * n,
        out_shape=[jax.ShapeDtypeStruct(g.shape, g.dtype) for g in gathered],
        input_output_aliases={k: k for k in range(n)},
        scratch_shapes=[pltpu.SemaphoreType.DMA((n, 3)), pltpu.SemaphoreType.DMA((n, 3))],
    )(*gathered)


def _pair_exchange(grads):
    n = len(grads)

    def body(*refs):
        ins, outs = refs[:n], refs[n:2 * n]
        send_sems, recv_sems = refs[2 * n:]
        x, y, c, _ = _place()
        copies = []
        for k in range(n):
            half = grads[k].shape[1] // 2
            other = ins[k].at[:, pl.ds(pl.multiple_of((1 - c) * half, 8), half), :]
            copies.append(pltpu.make_async_remote_copy(
                src_ref=other, dst_ref=outs[k], send_sem=send_sems.at[k], recv_sem=recv_sems.at[k],
                device_id=(x, y, 1 - c), device_id_type=MESH))
        for cp in copies:
            cp.start()
        for cp in copies:
            cp.wait()

    return pl.pallas_call(
        body, name="grads_pair_exchange",
        in_specs=[ANY] * n, out_specs=[ANY] * n,
        out_shape=[jax.ShapeDtypeStruct((4, g.shape[1] // 2, g.shape[2]), F32) for g in grads],
        scratch_shapes=[pltpu.SemaphoreType.DMA((n,)), pltpu.SemaphoreType.DMA((n,))],
    )(*grads)


def _chip_scatter(halves):
    n = len(halves)

    def body(*refs):
        ins, outs = refs[:n], refs[n:2 * n]
        send_sems, recv_sems = refs[2 * n:]
        x, y, c, chips = _place()
        sends = []
        for k in range(n):
            for j, (tx, ty) in enumerate(chips):
                sends.append(pltpu.make_async_remote_copy(
                    src_ref=ins[k].at[2 * tx + ty], dst_ref=outs[k].at[j], send_sem=send_sems.at[k, j],
                    recv_sem=recv_sems.at[k, j], device_id=(tx, ty, c), device_id_type=MESH))
        for cp in sends:
            cp.start()
        for cp in sends:
            cp.wait()

    return pl.pallas_call(
        body, name="grads_chip_scatter",
        in_specs=[ANY] * n, out_specs=[ANY] * n,
        out_shape=[jax.ShapeDtypeStruct((3,) + hv.shape[1:], hv.dtype) for hv in halves],
        scratch_shapes=[pltpu.SemaphoreType.DMA((n, 3)), pltpu.SemaphoreType.DMA((n, 3))],
    )(*halves)


def _pair_share(finals):
    n = len(finals)

    def body(*refs):
        ins, outs = refs[:n], refs[n:2 * n]
        send_sems, recv_sems = refs[2 * n:]
        x, y, c, _ = _place()
        copies = [pltpu.make_async_remote_copy(
            src_ref=ins[k], dst_ref=outs[k], send_sem=send_sems.at[k], recv_sem=recv_sems.at[k],
            device_id=(x, y, 1 - c), device_id_type=MESH) for k in range(n)]
        for cp in copies:
            cp.start()
        for cp in copies:
            cp.wait()

    return pl.pallas_call(
        body, name="grads_pair_share",
        in_specs=[ANY] * n, out_specs=[ANY] * n,
        out_shape=[jax.ShapeDtypeStruct(fv.shape, F32) for fv in finals],
        scratch_shapes=[pltpu.SemaphoreType.DMA((n,)), pltpu.SemaphoreType.DMA((n,))],
    )(*finals)


def _small_all_reduce(buf):
    rows, c_ = buf.shape

    def body(in_ref, out_ref, pair_buf, slots, send_sems, recv_sems):
        x, y, c, chips = _place()
        mine = 2 * x + y
        pair = pltpu.make_async_remote_copy(
            src_ref=in_ref, dst_ref=pair_buf, send_sem=send_sems.at[0], recv_sem=recv_sems.at[0],
            device_id=(x, y, 1 - c), device_id_type=MESH)
        pair.start()
        pair.wait()
        slots[mine] = in_ref[...] + pair_buf[...]
        sends = [pltpu.make_async_remote_copy(
            src_ref=slots.at[mine], dst_ref=slots.at[mine], send_sem=send_sems.at[1 + j], recv_sem=recv_sems.at[1 + j],
            device_id=(tx, ty, c), device_id_type=MESH) for j, (tx, ty) in enumerate(chips)]
        for cp in sends:
            cp.start()
        for j, (tx, ty) in enumerate(chips):
            pltpu.make_async_remote_copy(
                src_ref=slots.at[mine], dst_ref=slots.at[2 * tx + ty], send_sem=send_sems.at[1 + j],
                recv_sem=recv_sems.at[1 + j], device_id=(tx, ty, c), device_id_type=MESH).wait()
        out_ref[...] = ((slots[0] + slots[1]) + slots[2]) + slots[3]

    vm = pl.BlockSpec(memory_space=pltpu.VMEM)
    return pl.pallas_call(
        body, name="small_all_reduce", in_specs=[vm], out_specs=vm,
        out_shape=jax.ShapeDtypeStruct((rows, c_), F32),
        scratch_shapes=[pltpu.VMEM((rows, c_), F32), pltpu.VMEM((4, rows, c_), F32),
                        pltpu.SemaphoreType.DMA((4,)), pltpu.SemaphoreType.DMA((4,))],
        compiler_params=pltpu.CompilerParams(vmem_limit_bytes=VMEM_LIMIT_V7X),
    )(buf)


def _reduce_scatter(grads):
    core = lax.axis_index("c").astype(jnp.int32).reshape(1)
    chip = (2 * lax.axis_index("x") + lax.axis_index("y")).astype(jnp.int32).reshape(1)
    recv = _pair_exchange(grads)
    halves = [_pair_sum(g, r, core, f"pair_sum_{k}") for k, (g, r) in enumerate(zip(grads, recv))]
    recv = _chip_scatter([hb for _, hb in halves])
    finals = [_chip_sum(hv, r, chip, f"chip_sum_{k}") for k, ((hv, _), r) in enumerate(zip(halves, recv))]
    return list(zip(finals, _pair_share(finals))), core


PACK_COLS = 1024


def _pack(arrays):
    flat = jnp.concatenate([a.reshape(-1).astype(F32) for a in arrays])
    rows = -(-flat.shape[0] // PACK_COLS)
    rows = -(-rows // 8) * 8
    return jnp.pad(flat, (0, rows * PACK_COLS - flat.shape[0])).reshape(rows, PACK_COLS)


def _unpack(buf, shapes):
    flat = buf.reshape(-1)
    out, at = [], 0
    for shp in shapes:
        size = math.prod(shp)
        out.append(flat[at:at + size].reshape(shp))
        at += size
    return out


def kernel(x, mix_norm_g, ffn_norm_g, gm_w_in, gm_ln_g, gm_ln_b, gm_w_s, gm_b_s, gm_w_out, fox_w_qkvf, fox_b_f, fox_w_o, ffn_w_gate, ffn_w_up, ffn_conv_w, ffn_conv_b, ffn_w_down, final_norm_g, loss_target, m_mix_norm_g, m_ffn_norm_g, m_gm_w_in, m_gm_ln_g, m_gm_ln_b, m_gm_w_s, m_gm_b_s, m_gm_w_out, m_fox_w_qkvf, m_fox_b_f, m_fox_w_o, m_ffn_w_gate, m_ffn_w_up, m_ffn_conv_w, m_ffn_conv_b, m_ffn_w_down, m_final_norm_g, v_mix_norm_g, v_ffn_norm_g, v_gm_w_in, v_gm_ln_g, v_gm_ln_b, v_gm_w_s, v_gm_b_s, v_gm_w_out, v_fox_w_qkvf, v_fox_b_f, v_fox_w_o, v_ffn_w_gate, v_ffn_w_up, v_ffn_conv_w, v_ffn_conv_b, v_ffn_w_down, v_final_norm_g):
    _, s_len, d = x.shape
    e = gm_ln_g.shape[1]
    f = ffn_conv_b.shape[1]
    n_head = fox_b_f.shape[1]
    n_pair = n_head // 2
    gd = e // GM_GROUPS
    qkvf_cols = fox_w_qkvf.shape[2]
    assert d == n_head * HEAD_DIM and d % (2 * LANES) == 0 and s_len % 512 == 0 and gd % LANES == 0
    assert gm_w_s.shape[2] == CHUNK and 4 * qkvf_cols == 3 * d + n_head
    tm = 256
    h0 = x[0]
    target = loss_target[0]

    gathered = _all_gather_chips([
        gm_w_in[0].astype(BF16), gm_w_out[0].astype(BF16), fox_w_qkvf[0].astype(BF16), fox_w_o[0].astype(BF16),
        ffn_w_gate.astype(BF16), ffn_w_up.astype(BF16), ffn_w_down.astype(BF16), ffn_conv_w])
    w_in, w_out4, qkvf4, wo4, wg_all, wu_all, wd_all, cw4 = gathered
    w_out = w_out4.reshape(e, d)
    qkvf = jnp.transpose(qkvf4, (1, 0, 2)).reshape(d, 4 * qkvf_cols)
    wq, wk, wv = qkvf[:, :d], qkvf[:, d:2 * d], qkvf[:, 2 * d:3 * d]
    wf = jnp.pad(qkvf[:, 3 * d:], ((0, 0), (0, LANES - n_head)))
    wo = wo4.reshape(d, d)
    conv_w_full = jnp.transpose(cw4, (1, 2, 0, 3)).reshape(2, 3, f)
    conv_w8 = jnp.pad(conv_w_full, ((0, 0), (0, 5), (0, 0)))
    bf_pad = jnp.pad(fox_b_f, ((0, 0), (0, LANES - n_head)))

    tril = jnp.tril(jnp.ones((CHUNK, CHUNK), bool))
    wc = jnp.where(tril[None], gm_w_s[0], 0.0).astype(BF16)
    wct = jnp.transpose(wc, (0, 2, 1))
    bias = jnp.repeat(gm_b_s[0].T, gd, axis=1)
    seg_groups = (jnp.arange(e)[:, None] // gd == jnp.arange(LANES)[None, :]).astype(BF16)
    seg_heads = (jnp.arange(d)[:, None] // HEAD_DIM == jnp.arange(LANES)[None, :]).astype(BF16)
    sel_q = _spare_selectors(d, key_side=False)
    sel_k = _spare_selectors(d, key_side=True)

    h1, a0, hn0, gated0 = _gmlp_fwd(h0, mix_norm_g[0:1], w_in, gm_ln_g, gm_ln_b, wc, bias, w_out, tm)
    h2, fa0, fup0, fhn0, fhid0 = _ffn_fwd(h1, ffn_norm_g[0:1], wg_all, wu_all, wd_all, 0, conv_w8[0], ffn_conv_b[0:1], tm)
    (hn1, qa, qb, ka, kb_, kat, kbt, va, vb, vat, vbt, z_f) = _fox_proj_fwd(
        h2, mix_norm_g[1:2], wq, wk, wv, wf, bf_pad, sel_q, sel_k, tm)
    o, qa2, qb2, qat2, qbt2 = _flash_fwd(qa, qb, kat, kbt, va, vb)
    h3 = _oproj_fwd(h2, o, wo, tm)
    h4, fa1, fup1, fhn1, fhid1 = _ffn_fwd(h3, ffn_norm_g[1:2], wg_all, wu_all, wd_all, 1, conv_w8[1], ffn_conv_b[1:2], tm)

    dh4, loss_part, g_final = _loss_head(h4, target, final_norm_g.reshape(1, d), tm)
    dh3, da1, dup1, gcw1, gcb1, gfn1 = _ffn_bwd(h3, dh4, fa1, fup1, ffn_norm_g[1:2], wg_all, wu_all, wd_all, 1,
                                                conv_w8[1], ffn_conv_b[1:2], tm)
    g_gate1 = _wgrad(fhn1, da1, 4, "wgrad_gate_1")
    g_up1 = _wgrad(fhn1, dup1, 4, "wgrad_up_1")
    g_down1 = _wgrad(fhid1, dh4, 1, "wgrad_down_1").reshape(4, f // 4, d)

    doa, dob, doat, dobt = _oproj_bwd(dh3, o, wo, seg_heads, sel_q, tm)
    g_wo = _wgrad(o, dh3, 1, "wgrad_wo").reshape(4, d // 4, d)
    dq, dkt, dvt, row_sums, col_sums = _flash_bwd(qa2, qb2, qat2, qbt2, kat, kbt, ka, kb_, vat, vbt,
                                                  doa, dob, doat, dobt)
    col_sums = col_sums[:, ::8, :].reshape(n_head, s_len).T
    dcum = jnp.pad(row_sums[:, ::HEAD_DIM] - col_sums, ((0, 0), (0, LANES - n_head)))
    dfl, g_bf = _forget_bwd(dcum, z_f, tm)
    dh2, dflb, gmn1 = _fox_proj_bwd(h2, dh3, dq, dkt, dvt, dfl, mix_norm_g[1:2], wq, wk, wv, wf, tm)
    g_q = _wgrad(hn1, dq, 1, "wgrad_q")[0]
    g_k = _wgrad_t(dkt, hn1, "wgrad_k").T
    g_v = _wgrad_t(dvt, hn1, "wgrad_v").T
    g_f = _wgrad(hn1, dflb, 1, "wgrad_f")[0][:, :n_head]
    g_qkvf = jnp.concatenate([g_q, g_k, g_v, g_f], axis=1).reshape(d, 4, qkvf_cols).transpose(1, 0, 2)

    dh1, da0f, dup0, gcw0, gcb0, gfn0 = _ffn_bwd(h1, dh2, fa0, fup0, ffn_norm_g[0:1], wg_all, wu_all, wd_all, 0,
                                                 conv_w8[0], ffn_conv_b[0:1], tm)
    g_gate0 = _wgrad(fhn0, da0f, 4, "wgrad_gate_0")
    g_up0 = _wgrad(fhn0, dup0, 4, "wgrad_up_0")
    g_down0 = _wgrad(fhid0, dh2, 1, "wgrad_down_0").reshape(4, f // 4, d)

    dh0, da0, g_ws, g_bs_t, g_lng, g_lnb, gmn0 = _gmlp_bwd(
        h0, dh1, a0, mix_norm_g[0:1], w_in, gm_ln_g, gm_ln_b, wc, wct, bias, w_out, seg_groups, tm)
    g_win = _wgrad(hn0, da0, 4, "wgrad_gm_in")
    g_wout = _wgrad(gated0, dh1, 1, "wgrad_gm_out").reshape(4, e // 4, d)

    big, core = _reduce_scatter([g_win, g_wout, g_qkvf, g_wo, g_gate0, g_gate1, g_up0, g_up1, g_down0, g_down1])
    r_win, r_wout, r_qkvf, r_wo, r_gate0, r_gate1, r_up0, r_up1, r_down0, r_down1 = big

    small = [jnp.concatenate([gmn0, gmn1]), jnp.concatenate([gfn0, gfn1]), g_lng, g_lnb, g_ws[None],
             g_bs_t[:, :GM_GROUPS].T[None], g_bf[:, :n_head], jnp.stack([gcw0[:3], gcw1[:3]]),
             jnp.concatenate([gcb0, gcb1]), g_final.reshape(d), loss_part[0, :1]]
    small_shapes = [a.shape for a in small]
    reduced = _unpack(_small_all_reduce(_pack(small)), small_shapes)
    (r_mix, r_ffn, r_lng, r_lnb, r_ws, r_bs, r_bf, r_cw_full, r_cb, r_final, r_loss) = reduced
    chip = 2 * lax.axis_index("x") + lax.axis_index("y")
    r_cw = lax.dynamic_slice_in_dim(r_cw_full, chip * (f // 4), f // 4, axis=2)

    def update_big(name, w, m, v, per_layer):
        parts = [_adamw_halves(w[l], m[l], v[l], own, other, core, f"adamw_{name}_{l}")
                 for l, (own, other) in enumerate(per_layer)]
        return tuple(jnp.stack([p[i] for p in parts]) for i in range(4))

    res = {}
    res["gm_w_in"] = update_big("gm_w_in", gm_w_in, m_gm_w_in, v_gm_w_in, [r_win])
    res["gm_w_out"] = update_big("gm_w_out", gm_w_out, m_gm_w_out, v_gm_w_out, [r_wout])
    res["fox_w_qkvf"] = update_big("fox_w_qkvf", fox_w_qkvf, m_fox_w_qkvf, v_fox_w_qkvf, [r_qkvf])
    res["fox_w_o"] = update_big("fox_w_o", fox_w_o, m_fox_w_o, v_fox_w_o, [r_wo])
    res["ffn_w_gate"] = update_big("ffn_w_gate", ffn_w_gate, m_ffn_w_gate, v_ffn_w_gate, [r_gate0, r_gate1])
    res["ffn_w_up"] = update_big("ffn_w_up", ffn_w_up, m_ffn_w_up, v_ffn_w_up, [r_up0, r_up1])
    res["ffn_w_down"] = update_big("ffn_w_down", ffn_w_down, m_ffn_w_down, v_ffn_w_down, [r_down0, r_down1])

    small_names = ["mix_norm_g", "ffn_norm_g", "gm_ln_g", "gm_ln_b", "gm_w_s", "gm_b_s", "fox_b_f", "ffn_conv_w",
                   "ffn_conv_b", "final_norm_g"]
    small_w = [mix_norm_g, ffn_norm_g, gm_ln_g, gm_ln_b, gm_w_s, gm_b_s, fox_b_f, ffn_conv_w, ffn_conv_b, final_norm_g]
    small_m = [m_mix_norm_g, m_ffn_norm_g, m_gm_ln_g, m_gm_ln_b, m_gm_w_s, m_gm_b_s, m_fox_b_f, m_ffn_conv_w,
               m_ffn_conv_b, m_final_norm_g]
    small_v = [v_mix_norm_g, v_ffn_norm_g, v_gm_ln_g, v_gm_ln_b, v_gm_w_s, v_gm_b_s, v_fox_b_f, v_ffn_conv_w,
               v_ffn_conv_b, v_final_norm_g]
    small_g = [r_mix, r_ffn, r_lng, r_lnb, r_ws, r_bs, r_bf, r_cw, r_cb, r_final]
    shapes = [w.shape for w in small_w]
    small_g = [g.reshape(s) for g, s in zip(small_g, shapes)]
    dlt, mn, vn = _adamw(_pack(small_w), _pack(small_m), _pack(small_v), _pack(small_g), "adamw_small")
    for name, g, dl_, m_, v_ in zip(small_names, small_g, _unpack(dlt, shapes), _unpack(mn, shapes), _unpack(vn, shapes)):
        res[name] = (g, dl_, m_, v_)

    order = ["mix_norm_g", "ffn_norm_g", "gm_w_in", "gm_ln_g", "gm_ln_b", "gm_w_s", "gm_b_s", "gm_w_out", "fox_w_qkvf",
             "fox_b_f", "fox_w_o", "ffn_w_gate", "ffn_w_up", "ffn_conv_w", "ffn_conv_b", "ffn_w_down", "final_norm_g"]
    outs = [r_loss.reshape(()), dh0[None]]
    for part in range(4):
        outs += [res[name][part] for name in order]
    return tuple(outs)
```

```python
import functools
import math

import jax
import jax.numpy as jnp
from jax import lax
from jax.experimental import pallas as pl
from jax.experimental.pallas import tpu as pltpu

F32 = jnp.float32
BF16 = jnp.bfloat16

RMS_EPS = 1e-6
LN_EPS = 1e-5
CHUNK = 128
GM_GROUPS = 8
HEAD_DIM = 64
LANES = 128
ATT_BLOCK = 256
ATT_CHUNK = 1024
VMEM_LIMIT_V7X = 56 * 1024 * 1024

ADAM_LR = 0.001
ADAM_B1 = 0.9
ADAM_B2 = 0.999
ADAM_EPS = 1e-08
ADAM_WD = 0.01
ADAM_STEP = 10

MESH = pl.DeviceIdType.MESH
ANY = pl.BlockSpec(memory_space=pl.ANY)
NEG_BIG = -1e30


def _params(n_grid):
    return pltpu.CompilerParams(dimension_semantics=("arbitrary",) * n_grid, vmem_limit_bytes=VMEM_LIMIT_V7X)


def _dot(a, b):
    return jnp.dot(a, b, preferred_element_type=F32)


def _dot_nt(a, b):
    return lax.dot_general(a, b, (((1,), (1,)), ((), ())), preferred_element_type=F32)


def _dot_tn(a, b):
    return lax.dot_general(a, b, (((0,), (0,)), ((), ())), preferred_element_type=F32)


def _split3(x):
    hi = x.astype(BF16)
    r = x - hi.astype(F32)
    mid = r.astype(BF16)
    lo = (r - mid.astype(F32)).astype(BF16)
    return hi, mid, lo


def _dot3_lhs(x, m):
    hi, mid, lo = _split3(x)
    return _dot(hi, m) + _dot(mid, m) + _dot(lo, m)


def _dot3_rhs(m, x):
    hi, mid, lo = _split3(x)
    return _dot(m, hi) + _dot(m, mid) + _dot(m, lo)


def _load_once(pairs, sem):
    @pl.when(pl.program_id(0) == 0)
    def _():
        copies = [pltpu.make_async_copy(src, dst, sem.at[k]) for k, (src, dst) in enumerate(pairs)]
        for cp in copies:
            cp.start()
        for cp in copies:
            cp.wait()


def _rms_fwd(x, g):
    r = lax.rsqrt(jnp.mean(x * x, axis=-1, keepdims=True) + RMS_EPS)
    xhat = x * r
    return xhat, r, xhat * g


def _rms_bwd(dy, xhat, r, g):
    w = dy * g
    dx = r * (w - xhat * jnp.mean(w * xhat, axis=-1, keepdims=True))
    return dx, dy * xhat


def _gelu_parts(a):
    c = math.sqrt(2.0 / math.pi)
    a2 = a * a
    t = jnp.tanh(c * (a + 0.044715 * a * a2))
    z = 0.5 * a * (1.0 + t)
    dz = 0.5 * (1.0 + t) + 0.5 * a * (1.0 - t * t) * (c * (1.0 + 3.0 * 0.044715 * a2))
    return z, dz


def _sigmoid(x):
    return 1.0 / (1.0 + jnp.exp(-x))


def _gmlp_core(a, lng, lnb, wc_ref, bias, n_chunk, gd):
    e = a.shape[1] // 2
    z, dz = _gelu_parts(a)
    u = z[:, :e]
    v = z[:, e:]
    mu = jnp.mean(v, axis=-1, keepdims=True)
    vc = v - mu
    rstd = lax.rsqrt(jnp.mean(vc * vc, axis=-1, keepdims=True) + LN_EPS)
    vhat = vc * rstd
    vln = vhat * lng + lnb
    vlb = vln.astype(BF16)
    rows = []
    for ci in range(n_chunk):
        cols = []
        for g in range(GM_GROUPS):
            blk = vlb[ci * CHUNK:(ci + 1) * CHUNK, g * gd:(g + 1) * gd]
            cols.append(_dot(wc_ref[g], blk))
        rows.append(jnp.concatenate(cols, axis=1) + bias)
    s = rows[0] if n_chunk == 1 else jnp.concatenate(rows, axis=0)
    return dz, u, vhat, rstd, vlb, s


def _gmlp_fwd(h, g_mix, w_in, lng, lnb, wc, bias, w_out, tm):
    s_len, d = h.shape
    n_p, _, w = w_in.shape
    e = w_out.shape[0]
    gd = e // GM_GROUPS
    n_chunk = tm // CHUNK

    def body(h_ref, g_ref, win_hbm, lng_ref, lnb_ref, wc_ref, bias_ref, wout_hbm,
             hout_ref, a_ref, hn_ref, gated_ref, win_v, wout_v, sem):
        _load_once([(win_hbm, win_v), (wout_hbm, wout_v)], sem)
        x = h_ref[...]
        _, _, y = _rms_fwd(x, g_ref[...])
        hn = y.astype(BF16)
        hn_ref[...] = hn
        for p in range(n_p):
            a_ref[:, p * w:(p + 1) * w] = _dot(hn, win_v[p])
        _, u, _, _, _, s = _gmlp_core(a_ref[...], lng_ref[...], lnb_ref[...], wc_ref, bias_ref[...], n_chunk, gd)
        gated = (u * s).astype(BF16)
        gated_ref[...] = gated
        hout_ref[...] = x + _dot(gated, wout_v[...])

    row = lambda i: (i, 0)
    const2 = lambda i: (0, 0)
    return pl.pallas_call(
        body, name="gmlp_fwd", grid=(s_len // tm,),
        in_specs=[pl.BlockSpec((tm, d), row), pl.BlockSpec((1, d), const2), ANY,
                  pl.BlockSpec((1, e), const2), pl.BlockSpec((1, e), const2),
                  pl.BlockSpec(wc.shape, lambda i: (0, 0, 0)), pl.BlockSpec((CHUNK, e), const2), ANY],
        out_specs=[pl.BlockSpec((tm, d), row), pl.BlockSpec((tm, 2 * e), row),
                   pl.BlockSpec((tm, d), row), pl.BlockSpec((tm, e), row)],
        out_shape=[jax.ShapeDtypeStruct((s_len, d), F32), jax.ShapeDtypeStruct((s_len, 2 * e), F32),
                   jax.ShapeDtypeStruct((s_len, d), BF16), jax.ShapeDtypeStruct((s_len, e), BF16)],
        scratch_shapes=[pltpu.VMEM(w_in.shape, BF16), pltpu.VMEM(w_out.shape, BF16), pltpu.SemaphoreType.DMA((2,))],
        compiler_params=_params(1),
    )(h, g_mix, w_in, lng, lnb, wc, bias, w_out)


def _gmlp_bwd(h, dh, a, g_mix, w_in, lng, lnb, wc, wct, bias, w_out, seg, tm):
    s_len, d = h.shape
    n_p, _, w = w_in.shape
    e = w_out.shape[0]
    gd = e // GM_GROUPS
    n_chunk = tm // CHUNK
    n_blk = s_len // tm

    def body(h_ref, dh_ref, a_ref, g_ref, win_hbm, lng_ref, lnb_ref, wc_ref, wct_ref, bias_ref, wout_hbm, seg_ref,
             dhin_ref, da_ref, gws_ref, gbs_ref, glng_ref, glnb_ref, gmix_ref, win_v, wout_v, dsum, sem):
        i = pl.program_id(0)
        _load_once([(win_hbm, win_v), (wout_hbm, wout_v)], sem)

        @pl.when(i == 0)
        def _():
            gws_ref[...] = jnp.zeros_like(gws_ref)
            glng_ref[...] = jnp.zeros_like(glng_ref)
            glnb_ref[...] = jnp.zeros_like(glnb_ref)
            gmix_ref[...] = jnp.zeros_like(gmix_ref)
            dsum[...] = jnp.zeros_like(dsum)

        x = h_ref[...]
        dh_v = dh_ref[...]
        g = g_ref[...]
        lng_v = lng_ref[...]
        xhat, r, _ = _rms_fwd(x, g)
        dz_da, u, vhat, rstd, vlb, s = _gmlp_core(a_ref[...], lng_v, lnb_ref[...], wc_ref, bias_ref[...], n_chunk, gd)
        dg = _dot_nt(dh_v.astype(BF16), wout_v[...])
        du = dg * s
        ds = dg * u
        dsb = ds.astype(BF16)
        rows = []
        ds_acc = None
        for ci in range(n_chunk):
            lo, hi = ci * CHUNK, (ci + 1) * CHUNK
            cols = []
            for gi in range(GM_GROUPS):
                d_blk = dsb[lo:hi, gi * gd:(gi + 1) * gd]
                gws_ref[gi] += _dot_nt(d_blk, vlb[lo:hi, gi * gd:(gi + 1) * gd])
                cols.append(_dot(wct_ref[gi], d_blk))
            rows.append(jnp.concatenate(cols, axis=1))
            ds_acc = ds[lo:hi] if ds_acc is None else ds_acc + ds[lo:hi]
        dsum[...] += ds_acc
        dvln = rows[0] if n_chunk == 1 else jnp.concatenate(rows, axis=0)
        glng_ref[...] += jnp.sum(dvln * vhat, axis=0, keepdims=True)
        glnb_ref[...] += jnp.sum(dvln, axis=0, keepdims=True)
        dvhat = dvln * lng_v
        dv = rstd * (dvhat - jnp.mean(dvhat, axis=-1, keepdims=True)
                     - vhat * jnp.mean(dvhat * vhat, axis=-1, keepdims=True))
        da = jnp.concatenate([du, dv], axis=1) * dz_da
        dab = da.astype(BF16)
        da_ref[...] = dab
        dhn = _dot_nt(dab[:, :w], win_v[0])
        for p in range(1, n_p):
            dhn += _dot_nt(dab[:, p * w:(p + 1) * w], win_v[p])
        dx, gg = _rms_bwd(dhn, xhat, r, g)
        gmix_ref[...] += jnp.sum(gg, axis=0, keepdims=True)
        dhin_ref[...] = dh_v + dx

        @pl.when(i == n_blk - 1)
        def _():
            tril = lax.broadcasted_iota(jnp.int32, (CHUNK, CHUNK), 0) >= lax.broadcasted_iota(jnp.int32, (CHUNK, CHUNK), 1)
            for gi in range(GM_GROUPS):
                gws_ref[gi] = jnp.where(tril, gws_ref[gi], 0.0)
            gbs_ref[...] = _dot3_lhs(dsum[...], seg_ref[...])

    row = lambda i: (i, 0)
    const2 = lambda i: (0, 0)
    const3 = lambda i: (0, 0, 0)
    return pl.pallas_call(
        body, name="gmlp_bwd", grid=(n_blk,),
        in_specs=[pl.BlockSpec((tm, d), row), pl.BlockSpec((tm, d), row), pl.BlockSpec((tm, 2 * e), row),
                  pl.BlockSpec((1, d), const2), ANY, pl.BlockSpec((1, e), const2), pl.BlockSpec((1, e), const2),
                  pl.BlockSpec(wc.shape, const3), pl.BlockSpec(wct.shape, const3), pl.BlockSpec((CHUNK, e), const2),
                  ANY, pl.BlockSpec((e, LANES), const2)],
        out_specs=[pl.BlockSpec((tm, d), row), pl.BlockSpec((tm, 2 * e), row), pl.BlockSpec(wc.shape, const3),
                   pl.BlockSpec((CHUNK, LANES), const2), pl.BlockSpec((1, e), const2), pl.BlockSpec((1, e), const2),
                   pl.BlockSpec((1, d), const2)],
        out_shape=[jax.ShapeDtypeStruct((s_len, d), F32), jax.ShapeDtypeStruct((s_len, 2 * e), BF16),
                   jax.ShapeDtypeStruct(wc.shape, F32), jax.ShapeDtypeStruct((CHUNK, LANES), F32),
                   jax.ShapeDtypeStruct((1, e), F32), jax.ShapeDtypeStruct((1, e), F32), jax.ShapeDtypeStruct((1, d), F32)],
        scratch_shapes=[pltpu.VMEM(w_in.shape, BF16), pltpu.VMEM(w_out.shape, BF16), pltpu.VMEM((CHUNK, e), F32),
                        pltpu.SemaphoreType.DMA((2,))],
        compiler_params=_params(1),
    )(h, dh, a, g_mix, w_in, lng, lnb, wc, wct, bias, w_out, seg)


def _shift_down(a, k, fill):
    tm = a.shape[0]
    out = pltpu.roll(a, k, 0)
    rid = lax.broadcasted_iota(jnp.int32, a.shape, 0)
    for j in range(k):
        out = jnp.where(rid == j, fill[8 - k + j:8 - k + j + 1, :], out)
    return out


def _shift_up(a, k, fill):
    tm = a.shape[0]
    out = pltpu.roll(a, tm - k, 0)
    rid = lax.broadcasted_iota(jnp.int32, a.shape, 0)
    for j in range(k):
        out = jnp.where(rid == tm - k + j, fill[j:j + 1, :], out)
    return out


def _ffn_fwd(h, g_norm, wg_all, wu_all, wd_all, layer, conv_w, conv_b, tm):
    s_len, d = h.shape
    n_p = wg_all.shape[0]
    fq = wg_all.shape[3]
    f = n_p * fq

    def body(h_ref, g_ref, wg_hbm, wu_hbm, wd_hbm, cw_ref, cb_ref,
             hout_ref, a_ref, up_ref, hn_ref, hid_ref, wg_v, wu_v, wd_v, carry, sem):
        i = pl.program_id(0)
        _load_once([(wg_hbm.at[:, layer], wg_v), (wu_hbm.at[:, layer], wu_v), (wd_hbm.at[:, layer], wd_v)], sem)

        @pl.when(i == 0)
        def _():
            carry[...] = jnp.zeros_like(carry)

        x = h_ref[...]
        _, _, y = _rms_fwd(x, g_ref[...])
        hn = y.astype(BF16)
        hn_ref[...] = hn
        for p in range(n_p):
            a_ref[:, p * fq:(p + 1) * fq] = _dot(hn, wg_v[p])
            up_ref[:, p * fq:(p + 1) * fq] = _dot(hn, wu_v[p])
        a = a_ref[...]
        prev = carry[...]
        am1 = _shift_down(a, 1, prev)
        am2 = _shift_down(a, 2, prev)
        carry[...] = a[tm - 8:tm, :]
        cw = cw_ref[...]
        ac = cb_ref[...] + am2 * cw[0:1, :]
        ac = ac + am1 * cw[1:2, :]
        ac = ac + a * cw[2:3, :]
        hid = (ac * _sigmoid(ac) * up_ref[...]).astype(BF16)
        hid_ref[...] = hid
        y2 = _dot(hid[:, :fq], wd_v[0])
        for p in range(1, n_p):
            y2 += _dot(hid[:, p * fq:(p + 1) * fq], wd_v[p])
        hout_ref[...] = x + y2

    row = lambda i: (i, 0)
    const2 = lambda i: (0, 0)
    return pl.pallas_call(
        body, name=f"ffn_fwd_{layer}", grid=(s_len // tm,),
        in_specs=[pl.BlockSpec((tm, d), row), pl.BlockSpec((1, d), const2), ANY, ANY, ANY,
                  pl.BlockSpec((8, f), const2), pl.BlockSpec((1, f), const2)],
        out_specs=[pl.BlockSpec((tm, d), row), pl.BlockSpec((tm, f), row), pl.BlockSpec((tm, f), row),
                   pl.BlockSpec((tm, d), row), pl.BlockSpec((tm, f), row)],
        out_shape=[jax.ShapeDtypeStruct((s_len, d), F32), jax.ShapeDtypeStruct((s_len, f), F32),
                   jax.ShapeDtypeStruct((s_len, f), F32), jax.ShapeDtypeStruct((s_len, d), BF16),
                   jax.ShapeDtypeStruct((s_len, f), BF16)],
        scratch_shapes=[pltpu.VMEM((n_p, d, fq), BF16), pltpu.VMEM((n_p, d, fq), BF16), pltpu.VMEM((n_p, fq, d), BF16),
                        pltpu.VMEM((8, f), F32), pltpu.SemaphoreType.DMA((3,))],
        compiler_params=_params(1),
    )(h, g_norm, wg_all, wu_all, wd_all, conv_w, conv_b)


def _ffn_bwd(h, dh, a, up, g_norm, wg_all, wu_all, wd_all, layer, conv_w, conv_b, tm):
    s_len, d = h.shape
    n_p = wg_all.shape[0]
    fq = wg_all.shape[3]
    f = n_p * fq
    n_blk = s_len // tm
    t8 = tm // 8

    def body(h_ref, dh_ref, a_ref, ahalo_ref, up_ref, g_ref, wg_hbm, wu_hbm, wd_hbm, cw_ref, cb_ref,
             dhin_ref, da_ref, dup_ref, gcw_ref, gcb_ref, gn_ref, wg_v, wu_v, wd_v, carry, sem):
        i = pl.program_id(0)
        _load_once([(wg_hbm.at[:, layer], wg_v), (wu_hbm.at[:, layer], wu_v), (wd_hbm.at[:, layer], wd_v)], sem)

        @pl.when(i == 0)
        def _():
            carry[...] = jnp.zeros_like(carry)
            gcw_ref[...] = jnp.zeros_like(gcw_ref)
            gcb_ref[...] = jnp.zeros_like(gcb_ref)
            gn_ref[...] = jnp.zeros_like(gn_ref)

        x = h_ref[...]
        dh_v = dh_ref[...]
        g = g_ref[...]
        xhat, r, _ = _rms_fwd(x, g)
        a = a_ref[...]
        up_v = up_ref[...]
        prev = jnp.where(i == n_blk - 1, 0.0, ahalo_ref[...])
        am1 = _shift_down(a, 1, prev)
        am2 = _shift_down(a, 2, prev)
        cw = cw_ref[...]
        ac = cb_ref[...] + am2 * cw[0:1, :]
        ac = ac + am1 * cw[1:2, :]
        ac = ac + a * cw[2:3, :]
        sg = _sigmoid(ac)
        sil = ac * sg
        dhb = dh_v.astype(BF16)
        dhid = jnp.concatenate([_dot_nt(dhb, wd_v[p]) for p in range(n_p)], axis=1)
        dup = dhid * sil
        dac = dhid * up_v * (sg * (1.0 + ac * (1.0 - sg)))
        gcb_ref[...] += jnp.sum(dac, axis=0, keepdims=True)
        gcw_ref[0:1, :] += jnp.sum(dac * am2, axis=0, keepdims=True)
        gcw_ref[1:2, :] += jnp.sum(dac * am1, axis=0, keepdims=True)
        gcw_ref[2:3, :] += jnp.sum(dac * a, axis=0, keepdims=True)
        nxt = carry[...]
        dp1 = _shift_up(dac, 1, nxt)
        dp2 = _shift_up(dac, 2, nxt)
        carry[...] = dac[0:8, :]
        da = dac * cw[2:3, :] + dp1 * cw[1:2, :] + dp2 * cw[0:1, :]
        dab = da.astype(BF16)
        dupb = dup.astype(BF16)
        da_ref[...] = dab
        dup_ref[...] = dupb
        dhn = _dot_nt(dab[:, :fq], wg_v[0]) + _dot_nt(dupb[:, :fq], wu_v[0])
        for p in range(1, n_p):
            dhn += _dot_nt(dab[:, p * fq:(p + 1) * fq], wg_v[p]) + _dot_nt(dupb[:, p * fq:(p + 1) * fq], wu_v[p])
        dx, gg = _rms_bwd(dhn, xhat, r, g)
        gn_ref[...] += jnp.sum(gg, axis=0, keepdims=True)
        dhin_ref[...] = dh_v + dx

    rev = lambda i: (n_blk - 1 - i, 0)
    halo = lambda i: (jnp.maximum((n_blk - 1 - i) * t8 - 1, 0), 0)
    const2 = lambda i: (0, 0)
    return pl.pallas_call(
        body, name=f"ffn_bwd_{layer}", grid=(n_blk,),
        in_specs=[pl.BlockSpec((tm, d), rev), pl.BlockSpec((tm, d), rev), pl.BlockSpec((tm, f), rev),
                  pl.BlockSpec((8, f), halo), pl.BlockSpec((tm, f), rev), pl.BlockSpec((1, d), const2), ANY, ANY, ANY,
                  pl.BlockSpec((8, f), const2), pl.BlockSpec((1, f), const2)],
        out_specs=[pl.BlockSpec((tm, d), rev), pl.BlockSpec((tm, f), rev), pl.BlockSpec((tm, f), rev),
                   pl.BlockSpec((8, f), const2), pl.BlockSpec((1, f), const2), pl.BlockSpec((1, d), const2)],
        out_shape=[jax.ShapeDtypeStruct((s_len, d), F32), jax.ShapeDtypeStruct((s_len, f), BF16),
                   jax.ShapeDtypeStruct((s_len, f), BF16), jax.ShapeDtypeStruct((8, f), F32),
                   jax.ShapeDtypeStruct((1, f), F32), jax.ShapeDtypeStruct((1, d), F32)],
        scratch_shapes=[pltpu.VMEM((n_p, d, fq), BF16), pltpu.VMEM((n_p, d, fq), BF16), pltpu.VMEM((n_p, fq, d), BF16),
                        pltpu.VMEM((8, f), F32), pltpu.SemaphoreType.DMA((3,))],
        compiler_params=_params(1),
    )(h, dh, a, a, up, g_norm, wg_all, wu_all, wd_all, conv_w, conv_b)


def _even_head_lanes(shape, axis):
    return (lax.broadcasted_iota(jnp.int32, shape, axis) & HEAD_DIM) == 0


def _pair_select(lo, hi, shape):
    return jnp.where(lax.broadcasted_iota(jnp.int32, shape, 1) < HEAD_DIM, lo, hi)


def _causal(row0, col0, shape):
    return row0 + lax.broadcasted_iota(jnp.int32, shape, 0) >= col0 + lax.broadcasted_iota(jnp.int32, shape, 1)


N_SPARE = 3


def _spare_selectors(d, key_side):
    lane = jnp.arange(d)[None, :]
    row = jnp.arange(N_SPARE * LANES)[:, None]
    head, part = row % LANES, row // LANES
    off = N_SPARE if key_side else 0
    sel_a = ((head % 2 == 0) & (lane == LANES * (head // 2) + HEAD_DIM + off + part)).astype(F32)
    sel_b = ((head % 2 == 1) & (lane == LANES * (head // 2) + off + part)).astype(F32)
    sign = -1.0 if key_side else 1.0
    ones_off = 0 if key_side else N_SPARE
    in_pair = jnp.arange(d)[None, :] % LANES
    ones_a = ((in_pair >= HEAD_DIM + ones_off) & (in_pair < HEAD_DIM + ones_off + N_SPARE)).astype(F32)
    ones_b = ((in_pair >= ones_off) & (in_pair < ones_off + N_SPARE)).astype(F32)
    return (sign * sel_a).astype(BF16), (sign * sel_b).astype(BF16), ones_a, ones_b


def _parts(x):
    return jnp.concatenate(_split3(x), axis=1)


def _fox_proj_fwd(h, g_norm, wq, wk, wv, wf, bf, sel_q, sel_k, tm):
    s_len, d = h.shape
    sq_a, sq_b, oq_a, oq_b = sel_q
    sk_a, sk_b, ok_a, ok_b = sel_k

    def body(h_ref, g_ref, wq_hbm, wk_hbm, wv_hbm, wf_ref, bf_ref, sqa_ref, sqb_ref, oqa_ref, oqb_ref,
             ska_ref, skb_ref, oka_ref, okb_ref,
             hn_ref, qa_ref, qb_ref, ka_ref, kb_ref, kat_ref, kbt_ref, va_ref, vb_ref, vat_ref, vbt_ref, z_ref,
             wq_v, wk_v, wv_v, total, sem):
        i = pl.program_id(0)
        _load_once([(wq_hbm, wq_v), (wk_hbm, wk_v), (wv_hbm, wv_v)], sem)

        @pl.when(i == 0)
        def _():
            total[...] = jnp.zeros_like(total)

        x = h_ref[...]
        _, _, y = _rms_fwd(x, g_ref[...])
        hn = y.astype(BF16)
        hn_ref[...] = hn
        z = _dot(hn, wf_ref[...]) + bf_ref[...]
        z_ref[...] = z
        logf = jnp.minimum(z, 0.0) - jnp.log(1.0 + jnp.exp(-jnp.abs(z)))
        tri = (lax.broadcasted_iota(jnp.int32, (tm, tm), 0) >= lax.broadcasted_iota(jnp.int32, (tm, tm), 1))
        cum = _dot3_rhs(jnp.where(tri, 1.0, 0.0).astype(BF16), logf) + total[0:1, :]
        total[...] = jnp.broadcast_to(cum[tm - 1:tm, :], total.shape)
        parts = _parts(cum)

        even = _even_head_lanes((tm, d), 1)
        q = _dot(hn, wq_v[...]) * (HEAD_DIM ** -0.5)
        qa_ref[...] = jnp.where(even, q, _dot(parts, sqa_ref[...]) + oqa_ref[...]).astype(BF16)
        qb_ref[...] = jnp.where(even, _dot(parts, sqb_ref[...]) + oqb_ref[...], q).astype(BF16)
        k = _dot(hn, wk_v[...])
        ka = jnp.where(even, k, _dot(parts, ska_ref[...]) + oka_ref[...])
        kb = jnp.where(even, _dot(parts, skb_ref[...]) + okb_ref[...], k)
        ka_ref[...] = ka.astype(BF16)
        kb_ref[...] = kb.astype(BF16)
        kat_ref[...] = ka.T.astype(BF16)
        kbt_ref[...] = kb.T.astype(BF16)
        v = _dot(hn, wv_v[...])
        va = jnp.where(even, v, oka_ref[...])
        vb = jnp.where(even, okb_ref[...], v)
        va_ref[...] = va.astype(BF16)
        vb_ref[...] = vb.astype(BF16)
        vat_ref[...] = va.T.astype(BF16)
        vbt_ref[...] = vb.T.astype(BF16)

    row = lambda i: (i, 0)
    col = lambda i: (0, i)
    const2 = lambda i: (0, 0)
    sd = jax.ShapeDtypeStruct((s_len, d), BF16)
    ds_ = jax.ShapeDtypeStruct((d, s_len), BF16)
    rs, cs = pl.BlockSpec((tm, d), row), pl.BlockSpec((d, tm), col)
    sel = pl.BlockSpec((N_SPARE * LANES, d), const2)
    one = pl.BlockSpec((1, d), const2)
    return pl.pallas_call(
        body, name="fox_proj_fwd", grid=(s_len // tm,),
        in_specs=[rs, one, ANY, ANY, ANY, pl.BlockSpec((d, LANES), const2), pl.BlockSpec((1, LANES), const2),
                  sel, sel, one, one, sel, sel, one, one],
        out_specs=[rs, rs, rs, rs, rs, cs, cs, rs, rs, cs, cs, pl.BlockSpec((tm, LANES), row)],
        out_shape=[sd, sd, sd, sd, sd, ds_, ds_, sd, sd, ds_, ds_, jax.ShapeDtypeStruct((s_len, LANES), F32)],
        scratch_shapes=[pltpu.VMEM((d, d), BF16), pltpu.VMEM((d, d), BF16), pltpu.VMEM((d, d), BF16),
                        pltpu.VMEM((8, LANES), F32), pltpu.SemaphoreType.DMA((3,))],
        compiler_params=_params(1),
    )(h, g_norm, wq, wk, wv, wf, bf, sq_a, sq_b, oq_a, oq_b, sk_a, sk_b, ok_a, ok_b)


def _spare_cols(x, base):
    xf = x[:, base:base + N_SPARE].astype(F32)
    return xf[:, 0:1] + xf[:, 1:2] + xf[:, 2:3]


def _with_query_term(x, term, base):
    lane = lax.broadcasted_iota(jnp.int32, x.shape, 1)
    hi, mid, lo = _split3(term)
    out = jnp.where(lane == base, hi.astype(F32), x)
    out = jnp.where(lane == base + 1, mid.astype(F32), out)
    out = jnp.where(lane == base + 2, lo.astype(F32), out)
    return jnp.where((lane >= base + N_SPARE) & (lane < base + 2 * N_SPARE), 1.0, out)


def _flash_fwd(qa, qb, kat, kbt, va, vb):
    s_len, d = qa.shape
    sub = ATT_BLOCK
    n_sub = 2 if s_len % (2 * sub) == 0 else 1
    t = n_sub * sub
    w = min(ATT_CHUNK, s_len)
    n_pair = d // LANES
    n_q = s_len // t
    bases = (HEAD_DIM, 0)
    chains = [(r, hh) for r in range(n_sub) for hh in range(2)]

    def body(qa_ref, qb_ref, kat_ref, kbt_ref, va_ref, vb_ref, o_ref, qa2_ref, qb2_ref, qat2_ref, qbt2_ref):
        i = pl.program_id(1)
        q_refs = (qa_ref, qb_ref)
        qs = [q_refs[hh][r * sub:(r + 1) * sub, :] for r, hh in chains]
        kts = (kat_ref, kbt_ref)
        vs = (va_ref, vb_ref)

        def step(kb, carry, masked):
            off = pl.multiple_of(kb * w, w)
            scores = [_dot(qs[c], kts[hh][:, pl.ds(off, w)]) for c, (r, hh) in enumerate(chains)]
            probs, stats = [], []
            for c, (r, hh) in enumerate(chains):
                m, _ = carry[c]
                s = scores[c]
                if masked:
                    s = jnp.where(_causal(i * t + r * sub, off, (sub, w)), s, NEG_BIG)
                m_new = jnp.maximum(m, jnp.max(s, axis=1, keepdims=True))
                probs.append(jnp.exp(s - m_new).astype(BF16))
                stats.append((m_new, jnp.exp(m - m_new)))
            return tuple((stats[c][0], carry[c][1] * stats[c][1] + _dot(probs[c], vs[hh][pl.ds(off, w), :]))
                         for c, (r, hh) in enumerate(chains))

        init = ((jnp.full((sub, 1), NEG_BIG, F32), jnp.zeros((sub, LANES), F32)),) * len(chains)
        diag = (i * t) // w
        carry = lax.fori_loop(0, diag, lambda kb, c: step(kb, c, False), init)
        carry = step(diag, carry, True)
        for r in range(n_sub):
            outs, q2 = [], []
            for hh in range(2):
                m, acc = carry[2 * r + hh]
                l = acc[:, bases[hh]:bases[hh] + 1]
                outs.append(acc / l)
                term = _spare_cols(qs[2 * r + hh], bases[hh]) - (m + jnp.log(l))
                q2.append(_with_query_term(qs[2 * r + hh].astype(F32), term, bases[hh]))
            rows = slice(r * sub, (r + 1) * sub)
            o_ref[rows, :] = _pair_select(outs[0], outs[1], (sub, LANES))
            qa2_ref[rows, :] = q2[0].astype(BF16)
            qb2_ref[rows, :] = q2[1].astype(BF16)
            qat2_ref[:, rows] = q2[0].T.astype(BF16)
            qbt2_ref[:, rows] = q2[1].T.astype(BF16)

    qblk = pl.BlockSpec((t, LANES), lambda j, i: (i, j))
    qblk_t = pl.BlockSpec((LANES, t), lambda j, i: (j, i))
    whole_t = pl.BlockSpec((LANES, s_len), lambda j, i: (j, 0))
    whole = pl.BlockSpec((s_len, LANES), lambda j, i: (0, j))
    sd = jax.ShapeDtypeStruct((s_len, d), BF16)
    ds_ = jax.ShapeDtypeStruct((d, s_len), BF16)
    return pl.pallas_call(
        body, name="flash_fwd", grid=(n_pair, n_q),
        in_specs=[qblk, qblk, whole_t, whole_t, whole, whole],
        out_specs=[qblk, qblk, qblk, qblk_t, qblk_t],
        out_shape=[jax.ShapeDtypeStruct((s_len, d), F32), sd, sd, ds_, ds_],
        compiler_params=_params(2),
    )(qa, qb, kat, kbt, va, vb)


def _flash_bwd(qa, qb, qat, qbt, kat, kbt, vat, vbt, doa, dob, doat, dobt):
    s_len, d = qa.shape
    t = ATT_BLOCK
    w = min(ATT_CHUNK, s_len)
    n_pair = d // LANES
    n_q = s_len // t
    hd = HEAD_DIM

    def body(qa_ref, qb_ref, qat_ref, qbt_ref, kat_hbm, kbt_hbm, vat_hbm, vbt_hbm, doa_ref, dob_ref, doat_ref, dobt_ref,
             dqt_ref, dkt_ref, dvt_ref, rs_ref, cs_ref,
             kat_v, kbt_v, vat_v, vbt_v, dkt_acc, dvt_acc, cs_acc, sem):
        j = pl.program_id(0)
        i = pl.program_id(1)

        @pl.when(i == 0)
        def _():
            rows = pl.ds(pl.multiple_of(j * LANES, LANES), LANES)
            copies = [pltpu.make_async_copy(src.at[rows, :], dst, sem.at[n]) for n, (src, dst) in enumerate([
                (kat_hbm, kat_v), (kbt_hbm, kbt_v), (vat_hbm, vat_v), (vbt_hbm, vbt_v)])]
            for cp in copies:
                cp.start()
            dkt_acc[...] = jnp.zeros_like(dkt_acc)
            dvt_acc[...] = jnp.zeros_like(dvt_acc)
            cs_acc[...] = jnp.zeros_like(cs_acc)
            for cp in copies:
                cp.wait()

        qs = (qa_ref[...], qb_ref[...])
        dos = (doa_ref[...], dob_ref[...])
        own = (slice(0, hd), slice(hd, 2 * hd))
        spare = (slice(hd, hd + 8), slice(0, 8))
        used = (slice(0, hd + 16), slice(0, 2 * hd))
        qts = (qat_ref[used[0], :], qbt_ref[used[1], :])
        dots = (doat_ref[own[0], :], dobt_ref[own[1], :])
        kts, vts = (kat_v, kbt_v), (vat_v, vbt_v)

        def step(kb, carry, masked):
            off = pl.multiple_of(kb * w, w)
            cols = pl.ds(off, w)
            scores = [_dot(qs[hh], kts[hh][:, cols]) for hh in range(2)]
            dps = [_dot(dos[hh], vts[hh][:, cols]) for hh in range(2)]
            ps, dss = [], []
            for hh in range(2):
                s = scores[hh]
                if masked:
                    s = jnp.where(_causal(i * t, off, (t, w)), s, NEG_BIG)
                p = jnp.exp(s)
                dss.append((p * dps[hh]).astype(BF16))
                ps.append(p.astype(BF16))
            out = []
            for hh in range(2):
                out.append(carry[hh] + _dot_nt(kts[hh][used[hh], cols], dss[hh]))
                dvt_acc[own[hh], cols] += _dot(dots[hh], ps[hh])
                with_sums = _dot(qts[hh], dss[hh])
                dkt_acc[own[hh], cols] += with_sums[own[hh], :]
                cs_acc[8 * hh:8 * hh + 8, cols] += with_sums[spare[hh], :]
            return tuple(out)

        diag = (i * t) // w
        init = (jnp.zeros((hd + 16, t), F32), jnp.zeros((2 * hd, t), F32))
        carry = lax.fori_loop(0, diag, lambda kb, c: step(kb, c, False), init)
        carry = step(diag, carry, True)
        for hh in range(2):
            dqt_ref[own[hh], :] = (carry[hh][own[hh], :] * (hd ** -0.5)).astype(BF16)
            rs_ref[0, 8 * hh:8 * hh + 8, :] = carry[hh][spare[hh], :]

        @pl.when(i == n_q - 1)
        def _():
            dkt_ref[...] = dkt_acc[...].astype(BF16)
            dvt_ref[...] = dvt_acc[...].astype(BF16)
            cs_ref[0] = cs_acc[...]

    qblk = pl.BlockSpec((t, LANES), lambda j, i: (i, j))
    qblk_t = pl.BlockSpec((LANES, t), lambda j, i: (j, i))
    whole_t = pl.BlockSpec((LANES, s_len), lambda j, i: (j, 0))
    ds_ = jax.ShapeDtypeStruct((d, s_len), BF16)
    sums = jax.ShapeDtypeStruct((n_pair, 16, s_len), F32)
    return pl.pallas_call(
        body, name="flash_bwd", grid=(n_pair, n_q),
        in_specs=[qblk, qblk, qblk_t, qblk_t, ANY, ANY, ANY, ANY, qblk, qblk, qblk_t, qblk_t],
        out_specs=[qblk_t, whole_t, whole_t, pl.BlockSpec((1, 16, t), lambda j, i: (j, 0, i)),
                   pl.BlockSpec((1, 16, s_len), lambda j, i: (j, 0, 0))],
        out_shape=[ds_, ds_, ds_, sums, sums],
        scratch_shapes=[pltpu.VMEM((LANES, s_len), BF16), pltpu.VMEM((LANES, s_len), BF16),
                        pltpu.VMEM((LANES, s_len), BF16), pltpu.VMEM((LANES, s_len), BF16),
                        pltpu.VMEM((LANES, s_len), F32), pltpu.VMEM((LANES, s_len), F32),
                        pltpu.VMEM((16, s_len), F32), pltpu.SemaphoreType.DMA((4,))],
        compiler_params=_params(2),
    )(qa, qb, qat, qbt, kat, kbt, vat, vbt, doa, dob, doat, dobt)


def _wgrad_t(at, b, name):
    k, s_len = at.shape
    n = b.shape[1]
    tn, tk, ts = min(n, 1024), min(k, 1024), 512

    def body(a_ref, b_ref, o_ref):
        @pl.when(pl.program_id(2) == 0)
        def _():
            o_ref[...] = jnp.zeros_like(o_ref)
        o_ref[...] += _dot(a_ref[...].astype(BF16), b_ref[...].astype(BF16))

    return pl.pallas_call(
        body, name=name, grid=(k // tk, n // tn, s_len // ts),
        in_specs=[pl.BlockSpec((tk, ts), lambda a, b_, c: (a, c)), pl.BlockSpec((ts, tn), lambda a, b_, c: (c, b_))],
        out_specs=pl.BlockSpec((tk, tn), lambda a, b_, c: (a, b_)),
        out_shape=jax.ShapeDtypeStruct((k, n), F32),
        compiler_params=_params(3),
    )(at, b)


def _oproj_bwd(dh, o, wo, seg, sel_q, tm):
    s_len, d = dh.shape
    sq_a, sq_b, _, _ = sel_q

    def body(dh_ref, o_ref, wo_hbm, seg_ref, sqa_ref, sqb_ref, doa_ref, dob_ref, doat_ref, dobt_ref, wo_v, sem):
        _load_once([(wo_hbm, wo_v)], sem)
        do = _dot_nt(dh_ref[...].astype(BF16), wo_v[...])
        parts = _parts(-_dot3_lhs(do * o_ref[...], seg_ref[...]))
        even = _even_head_lanes((tm, d), 1)
        doa = jnp.where(even, do, _dot(parts, sqa_ref[...]))
        dob = jnp.where(even, _dot(parts, sqb_ref[...]), do)
        doa_ref[...] = doa.astype(BF16)
        dob_ref[...] = dob.astype(BF16)
        doat_ref[...] = doa.T.astype(BF16)
        dobt_ref[...] = dob.T.astype(BF16)

    row = lambda i: (i, 0)
    const2 = lambda i: (0, 0)
    rs, cs = pl.BlockSpec((tm, d), row), pl.BlockSpec((d, tm), lambda i: (0, i))
    sel = pl.BlockSpec((N_SPARE * LANES, d), const2)
    sd = jax.ShapeDtypeStruct((s_len, d), BF16)
    ds_ = jax.ShapeDtypeStruct((d, s_len), BF16)
    return pl.pallas_call(
        body, name="oproj_bwd", grid=(s_len // tm,),
        in_specs=[rs, rs, ANY, pl.BlockSpec((d, LANES), const2), sel, sel],
        out_specs=[rs, rs, cs, cs], out_shape=[sd, sd, ds_, ds_],
        scratch_shapes=[pltpu.VMEM((d, d), BF16), pltpu.SemaphoreType.DMA((1,))],
        compiler_params=_params(1),
    )(dh, o, wo, seg, sq_a, sq_b)


def _oproj_fwd(h, o, wo, tm):
    s_len, d = h.shape

    def body(h_ref, o_ref, wo_hbm, hout_ref, wo_v, sem):
        _load_once([(wo_hbm, wo_v)], sem)
        hout_ref[...] = h_ref[...] + _dot(o_ref[...].astype(BF16), wo_v[...])

    row = lambda i: (i, 0)
    return pl.pallas_call(
        body, name="oproj_fwd", grid=(s_len // tm,),
        in_specs=[pl.BlockSpec((tm, d), row), pl.BlockSpec((tm, d), row), ANY],
        out_specs=pl.BlockSpec((tm, d), row),
        out_shape=jax.ShapeDtypeStruct((s_len, d), F32),
        scratch_shapes=[pltpu.VMEM((d, d), BF16), pltpu.SemaphoreType.DMA((1,))],
        compiler_params=_params(1),
    )(h, o, wo)


def _forget_bwd(dcum, z, tm):
    s_len = dcum.shape[0]
    n_blk = s_len // tm

    def body(dc_ref, z_ref, dfl_ref, gb_ref, total):
        i = pl.program_id(0)

        @pl.when(i == 0)
        def _():
            total[...] = jnp.zeros_like(total)
            gb_ref[...] = jnp.zeros_like(gb_ref)

        upper = (lax.broadcasted_iota(jnp.int32, (tm, tm), 0) <= lax.broadcasted_iota(jnp.int32, (tm, tm), 1))
        suffix = _dot3_rhs(jnp.where(upper, 1.0, 0.0).astype(BF16), dc_ref[...]) + total[0:1, :]
        total[...] = jnp.broadcast_to(suffix[0:1, :], total.shape)
        dfl = suffix * _sigmoid(-z_ref[...])
        dfl_ref[...] = dfl
        gb_ref[...] += jnp.sum(dfl, axis=0, keepdims=True)

    rev = lambda i: (n_blk - 1 - i, 0)
    return pl.pallas_call(
        body, name="forget_bwd", grid=(n_blk,),
        in_specs=[pl.BlockSpec((tm, LANES), rev), pl.BlockSpec((tm, LANES), rev)],
        out_specs=[pl.BlockSpec((tm, LANES), rev), pl.BlockSpec((1, LANES), lambda i: (0, 0))],
        out_shape=[jax.ShapeDtypeStruct((s_len, LANES), F32), jax.ShapeDtypeStruct((1, LANES), F32)],
        scratch_shapes=[pltpu.VMEM((8, LANES), F32)],
        compiler_params=_params(1),
    )(dcum, z)


def _fox_proj_bwd(h, dh, dqt, dkt, dvt, dfl, g_norm, wq, wk, wv, wf, tm):
    s_len, d = h.shape

    def body(h_ref, dh_ref, dqt_ref, dkt_ref, dvt_ref, dfl_ref, g_ref, wq_hbm, wk_hbm, wv_hbm, wf_ref,
             dhin_ref, dflb_ref, gn_ref, wq_v, wk_v, wv_v, sem):
        _load_once([(wq_hbm, wq_v), (wk_hbm, wk_v), (wv_hbm, wv_v)], sem)

        @pl.when(pl.program_id(0) == 0)
        def _():
            gn_ref[...] = jnp.zeros_like(gn_ref)

        g = g_ref[...]
        xhat, r, _ = _rms_fwd(h_ref[...], g)
        dflb = dfl_ref[...].astype(BF16)
        dflb_ref[...] = dflb
        from_qkv = (_dot(wq_v[...], dqt_ref[...]) + _dot(wk_v[...], dkt_ref[...])
                    + _dot(wv_v[...], dvt_ref[...]))
        dhn = _dot_nt(dflb, wf_ref[...]) + from_qkv.T
        dx, gg = _rms_bwd(dhn, xhat, r, g)
        gn_ref[...] += jnp.sum(gg, axis=0, keepdims=True)
        dhin_ref[...] = dh_ref[...] + dx

    row = lambda i: (i, 0)
    const2 = lambda i: (0, 0)
    rs = pl.BlockSpec((tm, d), row)
    cs = pl.BlockSpec((d, tm), lambda i: (0, i))
    return pl.pallas_call(
        body, name="fox_proj_bwd", grid=(s_len // tm,),
        in_specs=[rs, rs, cs, cs, cs, pl.BlockSpec((tm, LANES), row), pl.BlockSpec((1, d), const2), ANY, ANY, ANY,
                  pl.BlockSpec((d, LANES), const2)],
        out_specs=[rs, pl.BlockSpec((tm, LANES), row), pl.BlockSpec((1, d), const2)],
        out_shape=[jax.ShapeDtypeStruct((s_len, d), F32), jax.ShapeDtypeStruct((s_len, LANES), BF16),
                   jax.ShapeDtypeStruct((1, d), F32)],
        scratch_shapes=[pltpu.VMEM((d, d), BF16), pltpu.VMEM((d, d), BF16), pltpu.VMEM((d, d), BF16),
                        pltpu.SemaphoreType.DMA((3,))],
        compiler_params=_params(1),
    )(h, dh, dqt, dkt, dvt, dfl, g_norm, wq, wk, wv, wf)


def _loss_head(h, target, g_final, tm):
    s_len, d = h.shape
    n_blk = s_len // tm

    def body(h_ref, t_ref, g_ref, dh_ref, loss_ref, gg_ref, sq):
        i = pl.program_id(0)

        @pl.when(i == 0)
        def _():
            sq[...] = jnp.zeros_like(sq)
            gg_ref[...] = jnp.zeros_like(gg_ref)

        g = g_ref[...]
        xhat, r, y = _rms_fwd(h_ref[...], g)
        err = y - t_ref[...]
        sq[...] += jnp.sum(err * err, axis=0, keepdims=True)
        dx, gg = _rms_bwd(err * (1.0 / d), xhat, r, g)
        gg_ref[...] += jnp.sum(gg, axis=0, keepdims=True)
        dh_ref[...] = dx

        @pl.when(i == n_blk - 1)
        def _():
            loss_ref[...] = jnp.broadcast_to(jnp.sum(sq[...], axis=1, keepdims=True) * (0.5 / d), loss_ref.shape)

    row = lambda i: (i, 0)
    const2 = lambda i: (0, 0)
    return pl.pallas_call(
        body, name="loss_head", grid=(n_blk,),
        in_specs=[pl.BlockSpec((tm, d), row), pl.BlockSpec((tm, d), row), pl.BlockSpec((1, d), const2)],
        out_specs=[pl.BlockSpec((tm, d), row), pl.BlockSpec((1, LANES), const2), pl.BlockSpec((1, d), const2)],
        out_shape=[jax.ShapeDtypeStruct((s_len, d), F32), jax.ShapeDtypeStruct((1, LANES), F32),
                   jax.ShapeDtypeStruct((1, d), F32)],
        scratch_shapes=[pltpu.VMEM((1, d), F32)],
        compiler_params=_params(1),
    )(h, target, g_final)


def _wgrad(x, dy, n_piece, name):
    s_len, k = x.shape
    n = dy.shape[1]
    tn = min(n // n_piece, 1024)
    tk = min(k, 1024)
    ts = 512
    per_piece = (n // n_piece) // tn

    def body(x_ref, dy_ref, o_ref):
        @pl.when(pl.program_id(2) == 0)
        def _():
            o_ref[...] = jnp.zeros_like(o_ref)
        o_ref[0] += _dot_tn(x_ref[...].astype(BF16), dy_ref[...].astype(BF16))

    return pl.pallas_call(
        body, name=name, grid=(k // tk, n // tn, s_len // ts),
        in_specs=[pl.BlockSpec((ts, tk), lambda a, b, c: (c, a)), pl.BlockSpec((ts, tn), lambda a, b, c: (c, b))],
        out_specs=pl.BlockSpec((1, tk, tn), lambda a, b, c: (b // per_piece, a, b % per_piece)),
        out_shape=jax.ShapeDtypeStruct((n_piece, k, n // n_piece), F32),
        compiler_params=_params(3),
    )(x, dy)


def _pair_sum(g, recv, core, name):
    n_piece, rows, c = g.shape
    half = rows // 2
    tr = min(half, 512)
    nb = half // tr

    def body(core_ref, g_ref, r_ref, o_ref, ob_ref):
        total = g_ref[...] + r_ref[...]
        o_ref[...] = total
        ob_ref[...] = total.astype(BF16)

    blk = pl.BlockSpec((1, tr, c), lambda p, i, core_ref: (p, i, 0))
    return pl.pallas_call(
        body, name=name,
        out_shape=[jax.ShapeDtypeStruct((n_piece, half, c), F32), jax.ShapeDtypeStruct((n_piece, half, c), BF16)],
        grid_spec=pltpu.PrefetchScalarGridSpec(
            num_scalar_prefetch=1, grid=(n_piece, nb),
            in_specs=[pl.BlockSpec((1, tr, c), lambda p, i, core_ref: (p, core_ref[0] * nb + i, 0)), blk],
            out_specs=[blk, blk]),
        compiler_params=_params(2),
    )(core, g, recv)


def _chip_sum(halves, recv, chip, name):
    _, h, c = halves.shape
    tr = min(h, 512)

    def body(chip_ref, own_ref, r_ref, o_ref):
        o_ref[...] = ((own_ref[0] + r_ref[0].astype(F32)) + r_ref[1].astype(F32)) + r_ref[2].astype(F32)

    return pl.pallas_call(
        body, name=name, out_shape=jax.ShapeDtypeStruct((h, c), F32),
        grid_spec=pltpu.PrefetchScalarGridSpec(
            num_scalar_prefetch=1, grid=(h // tr,),
            in_specs=[pl.BlockSpec((1, tr, c), lambda i, chip_ref: (chip_ref[0], i, 0)),
                      pl.BlockSpec((3, tr, c), lambda i, chip_ref: (0, i, 0))],
            out_specs=pl.BlockSpec((tr, c), lambda i, chip_ref: (i, 0))),
        compiler_params=_params(1),
    )(chip, halves, recv)


def _adamw_math(w, m, v, g):
    m_new = ADAM_B1 * m + (1.0 - ADAM_B1) * g
    v_new = ADAM_B2 * v + (1.0 - ADAM_B2) * (g * g)
    m_hat = m_new / (1.0 - ADAM_B1 ** ADAM_STEP)
    v_hat = v_new / (1.0 - ADAM_B2 ** ADAM_STEP)
    return -ADAM_LR * (m_hat / (jnp.sqrt(v_hat) + ADAM_EPS) + ADAM_WD * w), m_new, v_new


def _adamw(w, m, v, g, name):
    rows, c = w.shape
    tr = min(rows, 256)

    def body(w_ref, m_ref, v_ref, g_ref, d_ref, mo_ref, vo_ref):
        d_ref[...], mo_ref[...], vo_ref[...] = _adamw_math(w_ref[...], m_ref[...], v_ref[...], g_ref[...])

    spec = pl.BlockSpec((tr, c), lambda i: (i, 0))
    shape = jax.ShapeDtypeStruct((rows, c), F32)
    return pl.pallas_call(
        body, name=name, grid=(rows // tr,),
        in_specs=[spec] * 4, out_specs=[spec] * 3, out_shape=[shape] * 3,
        compiler_params=_params(1),
    )(w, m, v, g)


def _adamw_halves(w, m, v, g_own, g_other, core, name):
    rows, c = w.shape
    half = rows // 2
    tr = min(half, 256)
    nb = half // tr

    def body(core_ref, w_ref, m_ref, v_ref, own_ref, other_ref, g_ref, d_ref, mo_ref, vo_ref):
        mine = (pl.program_id(0) // nb) == core_ref[0]
        g = jnp.where(mine, own_ref[...], other_ref[...])
        g_ref[...] = g
        d_ref[...], mo_ref[...], vo_ref[...] = _adamw_math(w_ref[...], m_ref[...], v_ref[...], g)

    spec = pl.BlockSpec((tr, c), lambda i, core_ref: (i, 0))
    own = pl.BlockSpec((tr, c), lambda i, core_ref: (jnp.clip(i - core_ref[0] * nb, 0, nb - 1), 0))
    other = pl.BlockSpec((tr, c), lambda i, core_ref: (jnp.clip(i - (1 - core_ref[0]) * nb, 0, nb - 1), 0))
    shape = jax.ShapeDtypeStruct((rows, c), F32)
    return pl.pallas_call(
        body, name=name, out_shape=[shape] * 4,
        grid_spec=pltpu.PrefetchScalarGridSpec(
            num_scalar_prefetch=1, grid=(rows // tr,),
            in_specs=[spec, spec, spec, own, other], out_specs=[spec] * 4),
        compiler_params=_params(1),
    )(core, w, m, v, g_own, g_other)


def _place():
    x, y, c = lax.axis_index("x"), lax.axis_index("y"), lax.axis_index("c")
    chips = [(1 - x, y), (x, 1 - y), (1 - x, 1 - y)]
    return x, y, c, chips


def _all_gather_chips(shards):
    n = len(shards)
    halves = [s.shape[0] // 2 for s in shards]

    def half_of(ref, k, which):
        start = which * halves[k]
        if halves[k] % 8 == 0:
            start = pl.multiple_of(start, 8)
        return ref.at[pl.ds(start, halves[k])]

    def over_ici(*refs):
        ins, outs = refs[:n], refs[n:2 * n]
        send_sems, recv_sems = refs[2 * n:]
        x, y, c, chips = _place()
        mine = 2 * x + y
        sends = []
        for k in range(n):
            sends.append(pltpu.make_async_remote_copy(
                src_ref=ins[k], dst_ref=outs[k].at[mine], send_sem=send_sems.at[k, 3], recv_sem=recv_sems.at[k, 3],
                device_id=(x, y, 1 - c), device_id_type=MESH))
            for j, (tx, ty) in enumerate(chips):
                sends.append(pltpu.make_async_remote_copy(
                    src_ref=half_of(ins[k], k, c), dst_ref=half_of(outs[k].at[mine], k, c),
                    send_sem=send_sems.at[k, j], recv_sem=recv_sems.at[k, j],
                    device_id=(tx, ty, c), device_id_type=MESH))
        for cp in sends:
            cp.start()
        for cp in sends:
            cp.wait()

    gathered = pl.pallas_call(
        over_ici, name="weights_gather_ici",
        in_specs=[ANY] * n, out_specs=[ANY] * n,
        out_shape=[jax.ShapeDtypeStruct((4,) + s.shape, s.dtype) for s in shards],
        scratch_shapes=[pltpu.SemaphoreType.DMA((n, 4)), pltpu.SemaphoreType.DMA((n, 4))],
    )(*shards)

    def over_d2d(*refs):
        ins, outs = refs[:n], refs[n:2 * n]
        send_sems, recv_sems = refs[2 * n:]
        x, y, c, chips = _place()
        sends = []
        for k in range(n):
            for j, (tx, ty) in enumerate(chips):
                piece = half_of(outs[k].at[2 * tx + ty], k, c)
                sends.append(pltpu.make_async_remote_copy(
                    src_ref=piece, dst_ref=piece, send_sem=send_sems.at[k, j], recv_sem=recv_sems.at[k, j],
                    device_id=(x, y, 1 - c), device_id_type=MESH))
        for cp in sends:
            cp.start()
        for cp in sends:
            cp.wait()

    return pl.pallas_call(
        over_d2d, name="weights_gather_pair",
        in_specs=[ANY] * n, out_specs=[ANY] * n,
        out_shape=[jax.ShapeDtypeStruct(g.shape, g.dtype) for g in gathered],
        input_output_aliases={k: k for k in range(n)},
        scratch_shapes=[pltpu.SemaphoreType.DMA((n, 3)), pltpu.SemaphoreType.DMA((n, 3))],
    )(*gathered)


def _pair_exchange(grads):
    n = len(grads)

    def body(*refs):
        ins, outs = refs[:n], refs[n:2 * n]
        send_sems, recv_sems = refs[2 * n:]
        x, y, c, _ = _place()
        copies = []
        for k in range(n):
            half = grads[k].shape[1] // 2
            other = ins[k].at[:, pl.ds(pl.multiple_of((1 - c) * half, 8), half), :]
            copies.append(pltpu.make_async_remote_copy(
                src_ref=other, dst_ref=outs[k], send_sem=send_sems.at[k], recv_sem=recv_sems.at[k],
                device_id=(x, y, 1 - c), device_id_type=MESH))
        for cp in copies:
            cp.start()
        for cp in copies:
            cp.wait()

    return pl.pallas_call(
        body, name="grads_pair_exchange",
        in_specs=[ANY] * n, out_specs=[ANY] * n,
        out_shape=[jax.ShapeDtypeStruct((4, g.shape[1] // 2, g.shape[2]), F32) for g in grads],
        scratch_shapes=[pltpu.SemaphoreType.DMA((n,)), pltpu.SemaphoreType.DMA((n,))],
    )(*grads)


def _chip_scatter(halves):
    n = len(halves)

    def body(*refs):
        ins, outs = refs[:n], refs[n:2 * n]
        send_sems, recv_sems = refs[2 * n:]
        x, y, c, chips = _place()
        sends = []
        for k in range(n):
            for j, (tx, ty) in enumerate(chips):
                sends.append(pltpu.make_async_remote_copy(
                    src_ref=ins[k].at[2 * tx + ty], dst_ref=outs[k].at[j], send_sem=send_sems.at[k, j],
                    recv_sem=recv_sems.at[k, j], device_id=(tx, ty, c), device_id_type=MESH))
        for cp in sends:
            cp.start()
        for cp in sends:
            cp.wait()

    return pl.pallas_call(
        body, name="grads_chip_scatter",
        in_specs=[ANY] * n, out_specs=[ANY] * n,
        out_shape=[jax.ShapeDtypeStruct((3,) + hv.shape[1:], hv.dtype) for hv in halves],
        scratch_shapes=[pltpu.SemaphoreType.DMA((n, 3)), pltpu.SemaphoreType.DMA((n, 3))],
    )(*halves)


def _pair_share(finals):
    n = len(finals)

    def body(*refs):
        ins, outs = refs[:n], refs[n:2 * n]
        send_sems, recv_sems = refs[2 * n:]
        x, y, c, _ = _place()
        copies = [pltpu.make_async_remote_copy(
            src_ref=ins[k], dst_ref=outs[k], send_sem=send_sems.at[k], recv_sem=recv_sems.at[k],
            device_id=(x, y, 1 - c), device_id_type=MESH) for k in range(n)]
        for cp in copies:
            cp.start()
        for cp in copies:
            cp.wait()

    return pl.pallas_call(
        body, name="grads_pair_share",
        in_specs=[ANY] * n, out_specs=[ANY] * n,
        out_shape=[jax.ShapeDtypeStruct(fv.shape, F32) for fv in finals],
        scratch_shapes=[pltpu.SemaphoreType.DMA((n,)), pltpu.SemaphoreType.DMA((n,))],
    )(*finals)


def _small_all_reduce(buf):
    rows, c_ = buf.shape

    def body(in_ref, out_ref, pair_buf, slots, send_sems, recv_sems):
        x, y, c, chips = _place()
        mine = 2 * x + y
        pair = pltpu.make_async_remote_copy(
            src_ref=in_ref, dst_ref=pair_buf, send_sem=send_sems.at[0], recv_sem=recv_sems.at[0],
            device_id=(x, y, 1 - c), device_id_type=MESH)
        pair.start()
        pair.wait()
        slots[mine] = in_ref[...] + pair_buf[...]
        sends = [pltpu.make_async_remote_copy(
            src_ref=slots.at[mine], dst_ref=slots.at[mine], send_sem=send_sems.at[1 + j], recv_sem=recv_sems.at[1 + j],
            device_id=(tx, ty, c), device_id_type=MESH) for j, (tx, ty) in enumerate(chips)]
        for cp in sends:
            cp.start()
        for j, (tx, ty) in enumerate(chips):
            pltpu.make_async_remote_copy(
                src_ref=slots.at[mine], dst_ref=slots.at[2 * tx + ty], send_sem=send_sems.at[1 + j],
                recv_sem=recv_sems.at[1 + j], device_id=(tx, ty, c), device_id_type=MESH).wait()
        out_ref[...] = ((slots[0] + slots[1]) + slots[2]) + slots[3]

    vm = pl.BlockSpec(memory_space=pltpu.VMEM)
    return pl.pallas_call(
        body, name="small_all_reduce", in_specs=[vm], out_specs=vm,
        out_shape=jax.ShapeDtypeStruct((rows, c_), F32),
        scratch_shapes=[pltpu.VMEM((rows, c_), F32), pltpu.VMEM((4, rows, c_), F32),
                        pltpu.SemaphoreType.DMA((4,)), pltpu.SemaphoreType.DMA((4,))],
        compiler_params=pltpu.CompilerParams(vmem_limit_bytes=VMEM_LIMIT_V7X),
    )(buf)


def _reduce_scatter(grads):
    core = lax.axis_index("c").astype(jnp.int32).reshape(1)
    chip = (2 * lax.axis_index("x") + lax.axis_index("y")).astype(jnp.int32).reshape(1)
    recv = _pair_exchange(grads)
    halves = [_pair_sum(g, r, core, f"pair_sum_{k}") for k, (g, r) in enumerate(zip(grads, recv))]
    recv = _chip_scatter([hb for _, hb in halves])
    finals = [_chip_sum(hv, r, chip, f"chip_sum_{k}") for k, ((hv, _), r) in enumerate(zip(halves, recv))]
    return list(zip(finals, _pair_share(finals))), core


PACK_COLS = 1024


def _pack(arrays):
    flat = jnp.concatenate([a.reshape(-1).astype(F32) for a in arrays])
    rows = -(-flat.shape[0] // PACK_COLS)
    rows = -(-rows // 8) * 8
    return jnp.pad(flat, (0, rows * PACK_COLS - flat.shape[0])).reshape(rows, PACK_COLS)


def _unpack(buf, shapes):
    flat = buf.reshape(-1)
    out, at = [], 0
    for shp in shapes:
        size = math.prod(shp)
        out.append(flat[at:at + size].reshape(shp))
        at += size
    return out


def kernel(x, mix_norm_g, ffn_norm_g, gm_w_in, gm_ln_g, gm_ln_b, gm_w_s, gm_b_s, gm_w_out, fox_w_qkvf, fox_b_f, fox_w_o, ffn_w_gate, ffn_w_up, ffn_conv_w, ffn_conv_b, ffn_w_down, final_norm_g, loss_target, m_mix_norm_g, m_ffn_norm_g, m_gm_w_in, m_gm_ln_g, m_gm_ln_b, m_gm_w_s, m_gm_b_s, m_gm_w_out, m_fox_w_qkvf, m_fox_b_f, m_fox_w_o, m_ffn_w_gate, m_ffn_w_up, m_ffn_conv_w, m_ffn_conv_b, m_ffn_w_down, m_final_norm_g, v_mix_norm_g, v_ffn_norm_g, v_gm_w_in, v_gm_ln_g, v_gm_ln_b, v_gm_w_s, v_gm_b_s, v_gm_w_out, v_fox_w_qkvf, v_fox_b_f, v_fox_w_o, v_ffn_w_gate, v_ffn_w_up, v_ffn_conv_w, v_ffn_conv_b, v_ffn_w_down, v_final_norm_g):
    _, s_len, d = x.shape
    e = gm_ln_g.shape[1]
    f = ffn_conv_b.shape[1]
    n_head = fox_b_f.shape[1]
    n_pair = n_head // 2
    gd = e // GM_GROUPS
    qkvf_cols = fox_w_qkvf.shape[2]
    assert d == n_head * HEAD_DIM and d % (2 * LANES) == 0 and s_len % 512 == 0 and gd % LANES == 0
    assert gm_w_s.shape[2] == CHUNK and 4 * qkvf_cols == 3 * d + n_head
    tm = 256
    h0 = x[0]
    target = loss_target[0]

    gathered = _all_gather_chips([
        gm_w_in[0].astype(BF16), gm_w_out[0].astype(BF16), fox_w_qkvf[0].astype(BF16), fox_w_o[0].astype(BF16),
        ffn_w_gate.astype(BF16), ffn_w_up.astype(BF16), ffn_w_down.astype(BF16), ffn_conv_w])
    w_in, w_out4, qkvf4, wo4, wg_all, wu_all, wd_all, cw4 = gathered
    w_out = w_out4.reshape(e, d)
    qkvf = jnp.transpose(qkvf4, (1, 0, 2)).reshape(d, 4 * qkvf_cols)
    wq, wk, wv = qkvf[:, :d], qkvf[:, d:2 * d], qkvf[:, 2 * d:3 * d]
    wf = jnp.pad(qkvf[:, 3 * d:], ((0, 0), (0, LANES - n_head)))
    wo = wo4.reshape(d, d)
    conv_w_full = jnp.transpose(cw4, (1, 2, 0, 3)).reshape(2, 3, f)
    conv_w8 = jnp.pad(conv_w_full, ((0, 0), (0, 5), (0, 0)))
    bf_pad = jnp.pad(fox_b_f, ((0, 0), (0, LANES - n_head)))

    tril = jnp.tril(jnp.ones((CHUNK, CHUNK), bool))
    wc = jnp.where(tril[None], gm_w_s[0], 0.0).astype(BF16)
    wct = jnp.transpose(wc, (0, 2, 1))
    bias = jnp.repeat(gm_b_s[0].T, gd, axis=1)
    seg_groups = (jnp.arange(e)[:, None] // gd == jnp.arange(LANES)[None, :]).astype(BF16)
    seg_heads = (jnp.arange(d)[:, None] // HEAD_DIM == jnp.arange(LANES)[None, :]).astype(BF16)
    sel_q = _spare_selectors(d, key_side=False)
    sel_k = _spare_selectors(d, key_side=True)

    h1, a0, hn0, gated0 = _gmlp_fwd(h0, mix_norm_g[0:1], w_in, gm_ln_g, gm_ln_b, wc, bias, w_out, tm)
    h2, fa0, fup0, fhn0, fhid0 = _ffn_fwd(h1, ffn_norm_g[0:1], wg_all, wu_all, wd_all, 0, conv_w8[0], ffn_conv_b[0:1], tm)
    (hn1, qa, qb, ka, kb_, kat, kbt, va, vb, vat, vbt, z_f) = _fox_proj_fwd(
        h2, mix_norm_g[1:2], wq, wk, wv, wf, bf_pad, sel_q, sel_k, tm)
    o, qa2, qb2, qat2, qbt2 = _flash_fwd(qa, qb, kat, kbt, va, vb)
    h3 = _oproj_fwd(h2, o, wo, tm)
    h4, fa1, fup1, fhn1, fhid1 = _ffn_fwd(h3, ffn_norm_g[1:2], wg_all, wu_all, wd_all, 1, conv_w8[1], ffn_conv_b[1:2], tm)

    dh4, loss_part, g_final = _loss_head(h4, target, final_norm_g.reshape(1, d), tm)
    dh3, da1, dup1, gcw1, gcb1, gfn1 = _ffn_bwd(h3, dh4, fa1, fup1, ffn_norm_g[1:2], wg_all, wu_all, wd_all, 1,
                                                conv_w8[1], ffn_conv_b[1:2], tm)
    g_gate1 = _wgrad(fhn1, da1, 4, "wgrad_gate_1")
    g_up1 = _wgrad(fhn1, dup1, 4, "wgrad_up_1")
    g_down1 = _wgrad(fhid1, dh4, 1, "wgrad_down_1").reshape(4, f // 4, d)

    doa, dob, doat, dobt = _oproj_bwd(dh3, o, wo, seg_heads, sel_q, tm)
    g_wo = _wgrad(o, dh3, 1, "wgrad_wo").reshape(4, d // 4, d)
    dqt, dkt, dvt, row_sums, col_sums = _flash_bwd(qa2, qb2, qat2, qbt2, kat, kbt, vat, vbt, doa, dob, doat, dobt)
    sums = row_sums[:, 0::8, :] - col_sums[:, N_SPARE::8, :]
    dcum = jnp.pad(sums.reshape(n_head, s_len).T, ((0, 0), (0, LANES - n_head)))
    dfl, g_bf = _forget_bwd(dcum, z_f, tm)
    dh2, dflb, gmn1 = _fox_proj_bwd(h2, dh3, dqt, dkt, dvt, dfl, mix_norm_g[1:2], wq, wk, wv, wf, tm)
    g_q = _wgrad_t(dqt, hn1, "wgrad_q").T
    g_k = _wgrad_t(dkt, hn1, "wgrad_k").T
    g_v = _wgrad_t(dvt, hn1, "wgrad_v").T
    g_f = _wgrad(hn1, dflb, 1, "wgrad_f")[0][:, :n_head]
    g_qkvf = jnp.concatenate([g_q, g_k, g_v, g_f], axis=1).reshape(d, 4, qkvf_cols).transpose(1, 0, 2)

    dh1, da0f, dup0, gcw0, gcb0, gfn0 = _ffn_bwd(h1, dh2, fa0, fup0, ffn_norm_g[0:1], wg_all, wu_all, wd_all, 0,
                                                 conv_w8[0], ffn_conv_b[0:1], tm)
    g_gate0 = _wgrad(fhn0, da0f, 4, "wgrad_gate_0")
    g_up0 = _wgrad(fhn0, dup0, 4, "wgrad_up_0")
    g_down0 = _wgrad(fhid0, dh2, 1, "wgrad_down_0").reshape(4, f // 4, d)

    dh0, da0, g_ws, g_bs_t, g_lng, g_lnb, gmn0 = _gmlp_bwd(
        h0, dh1, a0, mix_norm_g[0:1], w_in, gm_ln_g, gm_ln_b, wc, wct, bias, w_out, seg_groups, tm)
    g_win = _wgrad(hn0, da0, 4, "wgrad_gm_in")
    g_wout = _wgrad(gated0, dh1, 1, "wgrad_gm_out").reshape(4, e // 4, d)

    big, core = _reduce_scatter([g_win, g_wout, g_qkvf, g_wo, g_gate0, g_gate1, g_up0, g_up1, g_down0, g_down1])
    r_win, r_wout, r_qkvf, r_wo, r_gate0, r_gate1, r_up0, r_up1, r_down0, r_down1 = big

    small = [jnp.concatenate([gmn0, gmn1]), jnp.concatenate([gfn0, gfn1]), g_lng, g_lnb, g_ws[None],
             g_bs_t[:, :GM_GROUPS].T[None], g_bf[:, :n_head], jnp.stack([gcw0[:3], gcw1[:3]]),
             jnp.concatenate([gcb0, gcb1]), g_final.reshape(d), loss_part[0, :1]]
    small_shapes = [a.shape for a in small]
    reduced = _unpack(_small_all_reduce(_pack(small)), small_shapes)
    (r_mix, r_ffn, r_lng, r_lnb, r_ws, r_bs, r_bf, r_cw_full, r_cb, r_final, r_loss) = reduced
    chip = 2 * lax.axis_index("x") + lax.axis_index("y")
    r_cw = lax.dynamic_slice_in_dim(r_cw_full, chip * (f // 4), f // 4, axis=2)

    def update_big(name, w, m, v, per_layer):
        parts = [_adamw_halves(w[l], m[l], v[l], own, other, core, f"adamw_{name}_{l}")
                 for l, (own, other) in enumerate(per_layer)]
        return tuple(jnp.stack([p[i] for p in parts]) for i in range(4))

    res = {}
    res["gm_w_in"] = update_big("gm_w_in", gm_w_in, m_gm_w_in, v_gm_w_in, [r_win])
    res["gm_w_out"] = update_big("gm_w_out", gm_w_out, m_gm_w_out, v_gm_w_out, [r_wout])
    res["fox_w_qkvf"] = update_big("fox_w_qkvf", fox_w_qkvf, m_fox_w_qkvf, v_fox_w_qkvf, [r_qkvf])
    res["fox_w_o"] = update_big("fox_w_o", fox_w_o, m_fox_w_o, v_fox_w_o, [r_wo])
    res["ffn_w_gate"] = update_big("ffn_w_gate", ffn_w_gate, m_ffn_w_gate, v_ffn_w_gate, [r_gate0, r_gate1])
    res["ffn_w_up"] = update_big("ffn_w_up", ffn_w_up, m_ffn_w_up, v_ffn_w_up, [r_up0, r_up1])
    res["ffn_w_down"] = update_big("ffn_w_down", ffn_w_down, m_ffn_w_down, v_ffn_w_down, [r_down0, r_down1])

    small_names = ["mix_norm_g", "ffn_norm_g", "gm_ln_g", "gm_ln_b", "gm_w_s", "gm_b_s", "fox_b_f", "ffn_conv_w",
                   "ffn_conv_b", "final_norm_g"]
    small_w = [mix_norm_g, ffn_norm_g, gm_ln_g, gm_ln_b, gm_w_s, gm_b_s, fox_b_f, ffn_conv_w, ffn_conv_b, final_norm_g]
    small_m = [m_mix_norm_g, m_ffn_norm_g, m_gm_ln_g, m_gm_ln_b, m_gm_w_s, m_gm_b_s, m_fox_b_f, m_ffn_conv_w,
               m_ffn_conv_b, m_final_norm_g]
    small_v = [v_mix_norm_g, v_ffn_norm_g, v_gm_ln_g, v_gm_ln_b, v_gm_w_s, v_gm_b_s, v_fox_b_f, v_ffn_conv_w,
               v_ffn_conv_b, v_final_norm_g]
    small_g = [r_mix, r_ffn, r_lng, r_lnb, r_ws, r_bs, r_bf, r_cw, r_cb, r_final]
    shapes = [w.shape for w in small_w]
    small_g = [g.reshape(s) for g, s in zip(small_g, shapes)]
    dlt, mn, vn = _adamw(_pack(small_w), _pack(small_m), _pack(small_v), _pack(small_g), "adamw_small")
    for name, g, dl_, m_, v_ in zip(small_names, small_g, _unpack(dlt, shapes), _unpack(mn, shapes), _unpack(vn, shapes)):
        res[name] = (g, dl_, m_, v_)

    order = ["mix_norm_g", "ffn_norm_g", "gm_w_in", "gm_ln_g", "gm_ln_b", "gm_w_s", "gm_b_s", "gm_w_out", "fox_w_qkvf",
             "fox_b_f", "fox_w_o", "ffn_w_gate", "ffn_w_up", "ffn_conv_w", "ffn_conv_b", "ffn_w_down", "final_norm_g"]
    outs = [r_loss.reshape(()), dh0[None]]
    for part in range(4):
        outs += [res[name][part] for name in order]
    return tuple(outs)
```

```python
import functools
import math

import jax
import jax.numpy as jnp
from jax import lax
from jax.experimental import pallas as pl
from jax.experimental.pallas import tpu as pltpu

F32 = jnp.float32
BF16 = jnp.bfloat16

RMS_EPS = 1e-6
LN_EPS = 1e-5
CHUNK = 128
GM_GROUPS = 8
HEAD_DIM = 64
LANES = 128
ATT_BLOCK = 256
ATT_CHUNK = 1024
VMEM_LIMIT_V7X = 56 * 1024 * 1024

ADAM_LR = 0.001
ADAM_B1 = 0.9
ADAM_B2 = 0.999
ADAM_EPS = 1e-08
ADAM_WD = 0.01
ADAM_STEP = 10

MESH = pl.DeviceIdType.MESH
ANY = pl.BlockSpec(memory_space=pl.ANY)
NEG_BIG = -1e30


def _params(n_grid):
    return pltpu.CompilerParams(dimension_semantics=("arbitrary",) * n_grid, vmem_limit_bytes=VMEM_LIMIT_V7X)


def _dot(a, b):
    return jnp.dot(a, b, preferred_element_type=F32)


def _dot_nt(a, b):
    return lax.dot_general(a, b, (((1,), (1,)), ((), ())), preferred_element_type=F32)


def _dot_tn(a, b):
    return lax.dot_general(a, b, (((0,), (0,)), ((), ())), preferred_element_type=F32)


def _split3(x):
    hi = x.astype(BF16)
    r = x - hi.astype(F32)
    mid = r.astype(BF16)
    lo = (r - mid.astype(F32)).astype(BF16)
    return hi, mid, lo


def _dot3_lhs(x, m):
    hi, mid, lo = _split3(x)
    return _dot(hi, m) + _dot(mid, m) + _dot(lo, m)


def _dot3_rhs(m, x):
    hi, mid, lo = _split3(x)
    return _dot(m, hi) + _dot(m, mid) + _dot(m, lo)


def _load_once(pairs, sem):
    @pl.when(pl.program_id(0) == 0)
    def _():
        copies = [pltpu.make_async_copy(src, dst, sem.at[k]) for k, (src, dst) in enumerate(pairs)]
        for cp in copies:
            cp.start()
        for cp in copies:
            cp.wait()


def _rms_fwd(x, g):
    r = lax.rsqrt(jnp.mean(x * x, axis=-1, keepdims=True) + RMS_EPS)
    xhat = x * r
    return xhat, r, xhat * g


def _rms_bwd(dy, xhat, r, g):
    w = dy * g
    dx = r * (w - xhat * jnp.mean(w * xhat, axis=-1, keepdims=True))
    return dx, dy * xhat


def _gelu_parts(a):
    c = math.sqrt(2.0 / math.pi)
    a2 = a * a
    t = jnp.tanh(c * (a + 0.044715 * a * a2))
    z = 0.5 * a * (1.0 + t)
    dz = 0.5 * (1.0 + t) + 0.5 * a * (1.0 - t * t) * (c * (1.0 + 3.0 * 0.044715 * a2))
    return z, dz


def _sigmoid(x):
    return 1.0 / (1.0 + jnp.exp(-x))


def _gmlp_core(a, lng, lnb, wc_ref, bias, n_chunk, gd):
    e = a.shape[1] // 2
    z, dz = _gelu_parts(a)
    u = z[:, :e]
    v = z[:, e:]
    mu = jnp.mean(v, axis=-1, keepdims=True)
    vc = v - mu
    rstd = lax.rsqrt(jnp.mean(vc * vc, axis=-1, keepdims=True) + LN_EPS)
    vhat = vc * rstd
    vln = vhat * lng + lnb
    vlb = vln.astype(BF16)
    rows = []
    for ci in range(n_chunk):
        cols = []
        for g in range(GM_GROUPS):
            blk = vlb[ci * CHUNK:(ci + 1) * CHUNK, g * gd:(g + 1) * gd]
            cols.append(_dot(wc_ref[g], blk))
        rows.append(jnp.concatenate(cols, axis=1) + bias)
    s = rows[0] if n_chunk == 1 else jnp.concatenate(rows, axis=0)
    return dz, u, vhat, rstd, vlb, s


def _gmlp_fwd(h, g_mix, w_in, lng, lnb, wc, bias, w_out, tm):
    s_len, d = h.shape
    n_p, _, w = w_in.shape
    e = w_out.shape[0]
    gd = e // GM_GROUPS
    n_chunk = tm // CHUNK

    def body(h_ref, g_ref, win_hbm, lng_ref, lnb_ref, wc_ref, bias_ref, wout_hbm,
             hout_ref, a_ref, hn_ref, gated_ref, win_v, wout_v, sem):
        _load_once([(win_hbm, win_v), (wout_hbm, wout_v)], sem)
        x = h_ref[...]
        _, _, y = _rms_fwd(x, g_ref[...])
        hn = y.astype(BF16)
        hn_ref[...] = hn
        for p in range(n_p):
            a_ref[:, p * w:(p + 1) * w] = _dot(hn, win_v[p])
        _, u, _, _, _, s = _gmlp_core(a_ref[...], lng_ref[...], lnb_ref[...], wc_ref, bias_ref[...], n_chunk, gd)
        gated = (u * s).astype(BF16)
        gated_ref[...] = gated
        hout_ref[...] = x + _dot(gated, wout_v[...])

    row = lambda i: (i, 0)
    const2 = lambda i: (0, 0)
    return pl.pallas_call(
        body, name="gmlp_fwd", grid=(s_len // tm,),
        in_specs=[pl.BlockSpec((tm, d), row), pl.BlockSpec((1, d), const2), ANY,
                  pl.BlockSpec((1, e), const2), pl.BlockSpec((1, e), const2),
                  pl.BlockSpec(wc.shape, lambda i: (0, 0, 0)), pl.BlockSpec((CHUNK, e), const2), ANY],
        out_specs=[pl.BlockSpec((tm, d), row), pl.BlockSpec((tm, 2 * e), row),
                   pl.BlockSpec((tm, d), row), pl.BlockSpec((tm, e), row)],
        out_shape=[jax.ShapeDtypeStruct((s_len, d), F32), jax.ShapeDtypeStruct((s_len, 2 * e), F32),
                   jax.ShapeDtypeStruct((s_len, d), BF16), jax.ShapeDtypeStruct((s_len, e), BF16)],
        scratch_shapes=[pltpu.VMEM(w_in.shape, BF16), pltpu.VMEM(w_out.shape, BF16), pltpu.SemaphoreType.DMA((2,))],
        compiler_params=_params(1),
    )(h, g_mix, w_in, lng, lnb, wc, bias, w_out)


def _gmlp_bwd(h, dh, a, g_mix, w_in, lng, lnb, wc, wct, bias, w_out, seg, tm):
    s_len, d = h.shape
    n_p, _, w = w_in.shape
    e = w_out.shape[0]
    gd = e // GM_GROUPS
    n_chunk = tm // CHUNK
    n_blk = s_len // tm

    def body(h_ref, dh_ref, a_ref, g_ref, win_hbm, lng_ref, lnb_ref, wc_ref, wct_ref, bias_ref, wout_hbm, seg_ref,
             dhin_ref, da_ref, gws_ref, gbs_ref, glng_ref, glnb_ref, gmix_ref, win_v, wout_v, dsum, sem):
        i = pl.program_id(0)
        _load_once([(win_hbm, win_v), (wout_hbm, wout_v)], sem)

        @pl.when(i == 0)
        def _():
            gws_ref[...] = jnp.zeros_like(gws_ref)
            glng_ref[...] = jnp.zeros_like(glng_ref)
            glnb_ref[...] = jnp.zeros_like(glnb_ref)
            gmix_ref[...] = jnp.zeros_like(gmix_ref)
            dsum[...] = jnp.zeros_like(dsum)

        x = h_ref[...]
        dh_v = dh_ref[...]
        g = g_ref[...]
        lng_v = lng_ref[...]
        xhat, r, _ = _rms_fwd(x, g)
        dz_da, u, vhat, rstd, vlb, s = _gmlp_core(a_ref[...], lng_v, lnb_ref[...], wc_ref, bias_ref[...], n_chunk, gd)
        dg = _dot_nt(dh_v.astype(BF16), wout_v[...])
        du = dg * s
        ds = dg * u
        dsb = ds.astype(BF16)
        rows = []
        ds_acc = None
        for ci in range(n_chunk):
            lo, hi = ci * CHUNK, (ci + 1) * CHUNK
            cols = []
            for gi in range(GM_GROUPS):
                d_blk = dsb[lo:hi, gi * gd:(gi + 1) * gd]
                gws_ref[gi] += _dot_nt(d_blk, vlb[lo:hi, gi * gd:(gi + 1) * gd])
                cols.append(_dot(wct_ref[gi], d_blk))
            rows.append(jnp.concatenate(cols, axis=1))
            ds_acc = ds[lo:hi] if ds_acc is None else ds_acc + ds[lo:hi]
        dsum[...] += ds_acc
        dvln = rows[0] if n_chunk == 1 else jnp.concatenate(rows, axis=0)
        glng_ref[...] += jnp.sum(dvln * vhat, axis=0, keepdims=True)
        glnb_ref[...] += jnp.sum(dvln, axis=0, keepdims=True)
        dvhat = dvln * lng_v
        dv = rstd * (dvhat - jnp.mean(dvhat, axis=-1, keepdims=True)
                     - vhat * jnp.mean(dvhat * vhat, axis=-1, keepdims=True))
        da = jnp.concatenate([du, dv], axis=1) * dz_da
        dab = da.astype(BF16)
        da_ref[...] = dab
        dhn = _dot_nt(dab[:, :w], win_v[0])
        for p in range(1, n_p):
            dhn += _dot_nt(dab[:, p * w:(p + 1) * w], win_v[p])
        dx, gg = _rms_bwd(dhn, xhat, r, g)
        gmix_ref[...] += jnp.sum(gg, axis=0, keepdims=True)
        dhin_ref[...] = dh_v + dx

        @pl.when(i == n_blk - 1)
        def _():
            tril = lax.broadcasted_iota(jnp.int32, (CHUNK, CHUNK), 0) >= lax.broadcasted_iota(jnp.int32, (CHUNK, CHUNK), 1)
            for gi in range(GM_GROUPS):
                gws_ref[gi] = jnp.where(tril, gws_ref[gi], 0.0)
            gbs_ref[...] = _dot3_lhs(dsum[...], seg_ref[...])

    row = lambda i: (i, 0)
    const2 = lambda i: (0, 0)
    const3 = lambda i: (0, 0, 0)
    return pl.pallas_call(
        body, name="gmlp_bwd", grid=(n_blk,),
        in_specs=[pl.BlockSpec((tm, d), row), pl.BlockSpec((tm, d), row), pl.BlockSpec((tm, 2 * e), row),
                  pl.BlockSpec((1, d), const2), ANY, pl.BlockSpec((1, e), const2), pl.BlockSpec((1, e), const2),
                  pl.BlockSpec(wc.shape, const3), pl.BlockSpec(wct.shape, const3), pl.BlockSpec((CHUNK, e), const2),
                  ANY, pl.BlockSpec((e, LANES), const2)],
        out_specs=[pl.BlockSpec((tm, d), row), pl.BlockSpec((tm, 2 * e), row), pl.BlockSpec(wc.shape, const3),
                   pl.BlockSpec((CHUNK, LANES), const2), pl.BlockSpec((1, e), const2), pl.BlockSpec((1, e), const2),
                   pl.BlockSpec((1, d), const2)],
        out_shape=[jax.ShapeDtypeStruct((s_len, d), F32), jax.ShapeDtypeStruct((s_len, 2 * e), BF16),
                   jax.ShapeDtypeStruct(wc.shape, F32), jax.ShapeDtypeStruct((CHUNK, LANES), F32),
                   jax.ShapeDtypeStruct((1, e), F32), jax.ShapeDtypeStruct((1, e), F32), jax.ShapeDtypeStruct((1, d), F32)],
        scratch_shapes=[pltpu.VMEM(w_in.shape, BF16), pltpu.VMEM(w_out.shape, BF16), pltpu.VMEM((CHUNK, e), F32),
                        pltpu.SemaphoreType.DMA((2,))],
        compiler_params=_params(1),
    )(h, dh, a, g_mix, w_in, lng, lnb, wc, wct, bias, w_out, seg)


def _shift_down(a, k, fill):
    tm = a.shape[0]
    out = pltpu.roll(a, k, 0)
    rid = lax.broadcasted_iota(jnp.int32, a.shape, 0)
    for j in range(k):
        out = jnp.where(rid == j, fill[8 - k + j:8 - k + j + 1, :], out)
    return out


def _shift_up(a, k, fill):
    tm = a.shape[0]
    out = pltpu.roll(a, tm - k, 0)
    rid = lax.broadcasted_iota(jnp.int32, a.shape, 0)
    for j in range(k):
        out = jnp.where(rid == tm - k + j, fill[j:j + 1, :], out)
    return out


def _ffn_fwd(h, g_norm, wg_all, wu_all, wd_all, layer, conv_w, conv_b, tm):
    s_len, d = h.shape
    n_p = wg_all.shape[0]
    fq = wg_all.shape[3]
    f = n_p * fq

    def body(h_ref, g_ref, wg_hbm, wu_hbm, wd_hbm, cw_ref, cb_ref,
             hout_ref, a_ref, up_ref, hn_ref, hid_ref, wg_v, wu_v, wd_v, carry, sem):
        i = pl.program_id(0)
        _load_once([(wg_hbm.at[:, layer], wg_v), (wu_hbm.at[:, layer], wu_v), (wd_hbm.at[:, layer], wd_v)], sem)

        @pl.when(i == 0)
        def _():
            carry[...] = jnp.zeros_like(carry)

        x = h_ref[...]
        _, _, y = _rms_fwd(x, g_ref[...])
        hn = y.astype(BF16)
        hn_ref[...] = hn
        for p in range(n_p):
            a_ref[:, p * fq:(p + 1) * fq] = _dot(hn, wg_v[p])
            up_ref[:, p * fq:(p + 1) * fq] = _dot(hn, wu_v[p])
        a = a_ref[...]
        prev = carry[...]
        am1 = _shift_down(a, 1, prev)
        am2 = _shift_down(a, 2, prev)
        carry[...] = a[tm - 8:tm, :]
        cw = cw_ref[...]
        ac = cb_ref[...] + am2 * cw[0:1, :]
        ac = ac + am1 * cw[1:2, :]
        ac = ac + a * cw[2:3, :]
        hid = (ac * _sigmoid(ac) * up_ref[...]).astype(BF16)
        hid_ref[...] = hid
        y2 = _dot(hid[:, :fq], wd_v[0])
        for p in range(1, n_p):
            y2 += _dot(hid[:, p * fq:(p + 1) * fq], wd_v[p])
        hout_ref[...] = x + y2

    row = lambda i: (i, 0)
    const2 = lambda i: (0, 0)
    return pl.pallas_call(
        body, name=f"ffn_fwd_{layer}", grid=(s_len // tm,),
        in_specs=[pl.BlockSpec((tm, d), row), pl.BlockSpec((1, d), const2), ANY, ANY, ANY,
                  pl.BlockSpec((8, f), const2), pl.BlockSpec((1, f), const2)],
        out_specs=[pl.BlockSpec((tm, d), row), pl.BlockSpec((tm, f), row), pl.BlockSpec((tm, f), row),
                   pl.BlockSpec((tm, d), row), pl.BlockSpec((tm, f), row)],
        out_shape=[jax.ShapeDtypeStruct((s_len, d), F32), jax.ShapeDtypeStruct((s_len, f), F32),
                   jax.ShapeDtypeStruct((s_len, f), F32), jax.ShapeDtypeStruct((s_len, d), BF16),
                   jax.ShapeDtypeStruct((s_len, f), BF16)],
        scratch_shapes=[pltpu.VMEM((n_p, d, fq), BF16), pltpu.VMEM((n_p, d, fq), BF16), pltpu.VMEM((n_p, fq, d), BF16),
                        pltpu.VMEM((8, f), F32), pltpu.SemaphoreType.DMA((3,))],
        compiler_params=_params(1),
    )(h, g_norm, wg_all, wu_all, wd_all, conv_w, conv_b)


def _ffn_bwd(h, dh, a, up, g_norm, wg_all, wu_all, wd_all, layer, conv_w, conv_b, tm):
    s_len, d = h.shape
    n_p = wg_all.shape[0]
    fq = wg_all.shape[3]
    f = n_p * fq
    n_blk = s_len // tm
    t8 = tm // 8

    def body(h_ref, dh_ref, a_ref, ahalo_ref, up_ref, g_ref, wg_hbm, wu_hbm, wd_hbm, cw_ref, cb_ref,
             dhin_ref, da_ref, dup_ref, gcw_ref, gcb_ref, gn_ref, wg_v, wu_v, wd_v, carry, sem):
        i = pl.program_id(0)
        _load_once([(wg_hbm.at[:, layer], wg_v), (wu_hbm.at[:, layer], wu_v), (wd_hbm.at[:, layer], wd_v)], sem)

        @pl.when(i == 0)
        def _():
            carry[...] = jnp.zeros_like(carry)
            gcw_ref[...] = jnp.zeros_like(gcw_ref)
            gcb_ref[...] = jnp.zeros_like(gcb_ref)
            gn_ref[...] = jnp.zeros_like(gn_ref)

        x = h_ref[...]
        dh_v = dh_ref[...]
        g = g_ref[...]
        xhat, r, _ = _rms_fwd(x, g)
        a = a_ref[...]
        up_v = up_ref[...]
        prev = jnp.where(i == n_blk - 1, 0.0, ahalo_ref[...])
        am1 = _shift_down(a, 1, prev)
        am2 = _shift_down(a, 2, prev)
        cw = cw_ref[...]
        ac = cb_ref[...] + am2 * cw[0:1, :]
        ac = ac + am1 * cw[1:2, :]
        ac = ac + a * cw[2:3, :]
        sg = _sigmoid(ac)
        sil = ac * sg
        dhb = dh_v.astype(BF16)
        dhid = jnp.concatenate([_dot_nt(dhb, wd_v[p]) for p in range(n_p)], axis=1)
        dup = dhid * sil
        dac = dhid * up_v * (sg * (1.0 + ac * (1.0 - sg)))
        gcb_ref[...] += jnp.sum(dac, axis=0, keepdims=True)
        gcw_ref[0:1, :] += jnp.sum(dac * am2, axis=0, keepdims=True)
        gcw_ref[1:2, :] += jnp.sum(dac * am1, axis=0, keepdims=True)
        gcw_ref[2:3, :] += jnp.sum(dac * a, axis=0, keepdims=True)
        nxt = carry[...]
        dp1 = _shift_up(dac, 1, nxt)
        dp2 = _shift_up(dac, 2, nxt)
        carry[...] = dac[0:8, :]
        da = dac * cw[2:3, :] + dp1 * cw[1:2, :] + dp2 * cw[0:1, :]
        dab = da.astype(BF16)
        dupb = dup.astype(BF16)
        da_ref[...] = dab
        dup_ref[...] = dupb
        dhn = _dot_nt(dab[:, :fq], wg_v[0]) + _dot_nt(dupb[:, :fq], wu_v[0])
        for p in range(1, n_p):
            dhn += _dot_nt(dab[:, p * fq:(p + 1) * fq], wg_v[p]) + _dot_nt(dupb[:, p * fq:(p + 1) * fq], wu_v[p])
        dx, gg = _rms_bwd(dhn, xhat, r, g)
        gn_ref[...] += jnp.sum(gg, axis=0, keepdims=True)
        dhin_ref[...] = dh_v + dx

    rev = lambda i: (n_blk - 1 - i, 0)
    halo = lambda i: (jnp.maximum((n_blk - 1 - i) * t8 - 1, 0), 0)
    const2 = lambda i: (0, 0)
    return pl.pallas_call(
        body, name=f"ffn_bwd_{layer}", grid=(n_blk,),
        in_specs=[pl.BlockSpec((tm, d), rev), pl.BlockSpec((tm, d), rev), pl.BlockSpec((tm, f), rev),
                  pl.BlockSpec((8, f), halo), pl.BlockSpec((tm, f), rev), pl.BlockSpec((1, d), const2), ANY, ANY, ANY,
                  pl.BlockSpec((8, f), const2), pl.BlockSpec((1, f), const2)],
        out_specs=[pl.BlockSpec((tm, d), rev), pl.BlockSpec((tm, f), rev), pl.BlockSpec((tm, f), rev),
                   pl.BlockSpec((8, f), const2), pl.BlockSpec((1, f), const2), pl.BlockSpec((1, d), const2)],
        out_shape=[jax.ShapeDtypeStruct((s_len, d), F32), jax.ShapeDtypeStruct((s_len, f), BF16),
                   jax.ShapeDtypeStruct((s_len, f), BF16), jax.ShapeDtypeStruct((8, f), F32),
                   jax.ShapeDtypeStruct((1, f), F32), jax.ShapeDtypeStruct((1, d), F32)],
        scratch_shapes=[pltpu.VMEM((n_p, d, fq), BF16), pltpu.VMEM((n_p, d, fq), BF16), pltpu.VMEM((n_p, fq, d), BF16),
                        pltpu.VMEM((8, f), F32), pltpu.SemaphoreType.DMA((3,))],
        compiler_params=_params(1),
    )(h, dh, a, a, up, g_norm, wg_all, wu_all, wd_all, conv_w, conv_b)


def _even_head_lanes(shape, axis):
    return (lax.broadcasted_iota(jnp.int32, shape, axis) & HEAD_DIM) == 0


def _pair_select(lo, hi, shape):
    return jnp.where(lax.broadcasted_iota(jnp.int32, shape, 1) < HEAD_DIM, lo, hi)


def _causal(row0, col0, shape):
    return row0 + lax.broadcasted_iota(jnp.int32, shape, 0) >= col0 + lax.broadcasted_iota(jnp.int32, shape, 1)


N_SPARE = 3


def _spare_selectors(d, key_side):
    lane = jnp.arange(d)[None, :]
    row = jnp.arange(N_SPARE * LANES)[:, None]
    head, part = row % LANES, row // LANES
    off = N_SPARE if key_side else 0
    sel_a = ((head % 2 == 0) & (lane == LANES * (head // 2) + HEAD_DIM + off + part)).astype(F32)
    sel_b = ((head % 2 == 1) & (lane == LANES * (head // 2) + off + part)).astype(F32)
    sign = -1.0 if key_side else 1.0
    ones_off = 0 if key_side else N_SPARE
    in_pair = jnp.arange(d)[None, :] % LANES
    ones_a = ((in_pair >= HEAD_DIM + ones_off) & (in_pair < HEAD_DIM + ones_off + N_SPARE)).astype(F32)
    ones_b = ((in_pair >= ones_off) & (in_pair < ones_off + N_SPARE)).astype(F32)
    return (sign * sel_a).astype(BF16), (sign * sel_b).astype(BF16), ones_a, ones_b


def _parts(x):
    return jnp.concatenate(_split3(x), axis=1)


def _fox_proj_fwd(h, g_norm, wq, wk, wv, wf, bf, sel_q, sel_k, tm):
    s_len, d = h.shape
    sq_a, sq_b, oq_a, oq_b = sel_q
    sk_a, sk_b, ok_a, ok_b = sel_k

    def body(h_ref, g_ref, wq_hbm, wk_hbm, wv_hbm, wf_ref, bf_ref, sqa_ref, sqb_ref, oqa_ref, oqb_ref,
             ska_ref, skb_ref, oka_ref, okb_ref,
             hn_ref, qa_ref, qb_ref, kat_ref, kbt_ref, va_ref, vb_ref, vat_ref, vbt_ref, z_ref,
             wq_v, wk_v, wv_v, total, sem):
        i = pl.program_id(0)
        _load_once([(wq_hbm, wq_v), (wk_hbm, wk_v), (wv_hbm, wv_v)], sem)

        @pl.when(i == 0)
        def _():
            total[...] = jnp.zeros_like(total)

        x = h_ref[...]
        _, _, y = _rms_fwd(x, g_ref[...])
        hn = y.astype(BF16)
        hn_ref[...] = hn
        z = _dot(hn, wf_ref[...]) + bf_ref[...]
        z_ref[...] = z
        logf = jnp.minimum(z, 0.0) - jnp.log(1.0 + jnp.exp(-jnp.abs(z)))
        tri = (lax.broadcasted_iota(jnp.int32, (tm, tm), 0) >= lax.broadcasted_iota(jnp.int32, (tm, tm), 1))
        cum = _dot3_rhs(jnp.where(tri, 1.0, 0.0).astype(BF16), logf) + total[0:1, :]
        total[...] = jnp.broadcast_to(cum[tm - 1:tm, :], total.shape)
        parts = _parts(cum)

        even = _even_head_lanes((tm, d), 1)
        q = _dot(hn, wq_v[...]) * (HEAD_DIM ** -0.5)
        qa_ref[...] = jnp.where(even, q, _dot(parts, sqa_ref[...]) + oqa_ref[...]).astype(BF16)
        qb_ref[...] = jnp.where(even, _dot(parts, sqb_ref[...]) + oqb_ref[...], q).astype(BF16)
        k = _dot(hn, wk_v[...])
        ka = jnp.where(even, k, _dot(parts, ska_ref[...]) + oka_ref[...])
        kb = jnp.where(even, _dot(parts, skb_ref[...]) + okb_ref[...], k)
        kat_ref[...] = ka.T.astype(BF16)
        kbt_ref[...] = kb.T.astype(BF16)
        v = _dot(hn, wv_v[...])
        va = jnp.where(even, v, oka_ref[...])
        vb = jnp.where(even, okb_ref[...], v)
        va_ref[...] = va.astype(BF16)
        vb_ref[...] = vb.astype(BF16)
        vat_ref[...] = va.T.astype(BF16)
        vbt_ref[...] = vb.T.astype(BF16)

    row = lambda i: (i, 0)
    col = lambda i: (0, i)
    const2 = lambda i: (0, 0)
    sd = jax.ShapeDtypeStruct((s_len, d), BF16)
    ds_ = jax.ShapeDtypeStruct((d, s_len), BF16)
    rs, cs = pl.BlockSpec((tm, d), row), pl.BlockSpec((d, tm), col)
    sel = pl.BlockSpec((N_SPARE * LANES, d), const2)
    one = pl.BlockSpec((1, d), const2)
    return pl.pallas_call(
        body, name="fox_proj_fwd", grid=(s_len // tm,),
        in_specs=[rs, one, ANY, ANY, ANY, pl.BlockSpec((d, LANES), const2), pl.BlockSpec((1, LANES), const2),
                  sel, sel, one, one, sel, sel, one, one],
        out_specs=[rs, rs, rs, cs, cs, rs, rs, cs, cs, pl.BlockSpec((tm, LANES), row)],
        out_shape=[sd, sd, sd, ds_, ds_, sd, sd, ds_, ds_, jax.ShapeDtypeStruct((s_len, LANES), F32)],
        scratch_shapes=[pltpu.VMEM((d, d), BF16), pltpu.VMEM((d, d), BF16), pltpu.VMEM((d, d), BF16),
                        pltpu.VMEM((8, LANES), F32), pltpu.SemaphoreType.DMA((3,))],
        compiler_params=_params(1),
    )(h, g_norm, wq, wk, wv, wf, bf, sq_a, sq_b, oq_a, oq_b, sk_a, sk_b, ok_a, ok_b)


def _spare_cols(x, base):
    xf = x[:, base:base + N_SPARE].astype(F32)
    return xf[:, 0:1] + xf[:, 1:2] + xf[:, 2:3]


def _with_query_term(x, term, base):
    lane = lax.broadcasted_iota(jnp.int32, x.shape, 1)
    hi, mid, lo = _split3(term)
    out = jnp.where(lane == base, hi.astype(F32), x)
    out = jnp.where(lane == base + 1, mid.astype(F32), out)
    out = jnp.where(lane == base + 2, lo.astype(F32), out)
    return jnp.where((lane >= base + N_SPARE) & (lane < base + 2 * N_SPARE), 1.0, out)


def _flash_fwd(qa, qb, kat, kbt, va, vb):
    s_len, d = qa.shape
    sub = ATT_BLOCK
    n_sub = 2 if s_len % (2 * sub) == 0 else 1
    t = n_sub * sub
    w = min(ATT_CHUNK, s_len)
    n_pair = d // LANES
    n_q = s_len // t
    bases = (HEAD_DIM, 0)
    chains = [(r, hh) for r in range(n_sub) for hh in range(2)]

    def body(qa_ref, qb_ref, kat_ref, kbt_ref, va_ref, vb_ref, o_ref, qa2_ref, qb2_ref, qat2_ref, qbt2_ref):
        i = pl.program_id(1)
        q_refs = (qa_ref, qb_ref)
        qs = [q_refs[hh][r * sub:(r + 1) * sub, :] for r, hh in chains]
        kts = (kat_ref, kbt_ref)
        vs = (va_ref, vb_ref)

        def step(kb, carry, masked, width=w):
            off = pl.multiple_of(kb * w, w)
            cols = pl.ds(off, width)
            scores = [_dot(qs[c], kts[hh][:, cols]) for c, (r, hh) in enumerate(chains)]
            probs, stats = [], []
            for c, (r, hh) in enumerate(chains):
                m, _ = carry[c]
                s = scores[c]
                if masked:
                    s = jnp.where(_causal(i * t + r * sub, off, (sub, width)), s, NEG_BIG)
                m_new = jnp.maximum(m, jnp.max(s, axis=1, keepdims=True))
                probs.append(jnp.exp(s - m_new).astype(BF16))
                stats.append((m_new, jnp.exp(m - m_new)))
            return tuple((stats[c][0], carry[c][1] * stats[c][1] + _dot(probs[c], vs[hh][cols, :]))
                         for c, (r, hh) in enumerate(chains))

        init = ((jnp.full((sub, 1), NEG_BIG, F32), jnp.zeros((sub, LANES), F32)),) * len(chains)
        diag = (i * t) // w
        carry = lax.fori_loop(0, diag, lambda kb, c: step(kb, c, False), init)
        carry = lax.cond((i * t) % w + t <= w // 2,
                         lambda c: step(diag, c, True, w // 2), lambda c: step(diag, c, True), carry)
        for r in range(n_sub):
            outs, q2 = [], []
            for hh in range(2):
                m, acc = carry[2 * r + hh]
                l = acc[:, bases[hh]:bases[hh] + 1]
                outs.append(acc / l)
                term = _spare_cols(qs[2 * r + hh], bases[hh]) - (m + jnp.log(l))
                q2.append(_with_query_term(qs[2 * r + hh].astype(F32), term, bases[hh]))
            rows = slice(r * sub, (r + 1) * sub)
            o_ref[rows, :] = _pair_select(outs[0], outs[1], (sub, LANES))
            qa2_ref[rows, :] = q2[0].astype(BF16)
            qb2_ref[rows, :] = q2[1].astype(BF16)
            qat2_ref[:, rows] = q2[0].T.astype(BF16)
            qbt2_ref[:, rows] = q2[1].T.astype(BF16)

    qblk = pl.BlockSpec((t, LANES), lambda j, i: (i, j))
    qblk_t = pl.BlockSpec((LANES, t), lambda j, i: (j, i))
    whole_t = pl.BlockSpec((LANES, s_len), lambda j, i: (j, 0))
    whole = pl.BlockSpec((s_len, LANES), lambda j, i: (0, j))
    sd = jax.ShapeDtypeStruct((s_len, d), BF16)
    ds_ = jax.ShapeDtypeStruct((d, s_len), BF16)
    return pl.pallas_call(
        body, name="flash_fwd", grid=(n_pair, n_q),
        in_specs=[qblk, qblk, whole_t, whole_t, whole, whole],
        out_specs=[qblk, qblk, qblk, qblk_t, qblk_t],
        out_shape=[jax.ShapeDtypeStruct((s_len, d), F32), sd, sd, ds_, ds_],
        compiler_params=_params(2),
    )(qa, qb, kat, kbt, va, vb)


def _flash_bwd(qa, qb, qat, qbt, kat, kbt, vat, vbt, doa, dob, doat, dobt):
    s_len, d = qa.shape
    t = ATT_BLOCK
    w = min(ATT_CHUNK, s_len)
    n_pair = d // LANES
    n_q = s_len // t
    hd = HEAD_DIM

    def body(qa_ref, qb_ref, qat_ref, qbt_ref, kat_hbm, kbt_hbm, vat_hbm, vbt_hbm, doa_ref, dob_ref, doat_ref, dobt_ref,
             dqt_ref, dkt_ref, dvt_ref, rs_ref, cs_ref,
             kat_v, kbt_v, vat_v, vbt_v, dkt_acc, dvt_acc, cs_acc, sem):
        j = pl.program_id(0)
        i = pl.program_id(1)

        @pl.when(i == 0)
        def _():
            rows = pl.ds(pl.multiple_of(j * LANES, LANES), LANES)
            copies = [pltpu.make_async_copy(src.at[rows, :], dst, sem.at[n]) for n, (src, dst) in enumerate([
                (kat_hbm, kat_v), (kbt_hbm, kbt_v), (vat_hbm, vat_v), (vbt_hbm, vbt_v)])]
            for cp in copies:
                cp.start()
            dkt_acc[...] = jnp.zeros_like(dkt_acc)
            dvt_acc[...] = jnp.zeros_like(dvt_acc)
            cs_acc[...] = jnp.zeros_like(cs_acc)
            for cp in copies:
                cp.wait()

        qs = (qa_ref[...], qb_ref[...])
        dos = (doa_ref[...], dob_ref[...])
        own = (slice(0, hd), slice(hd, 2 * hd))
        spare = (slice(hd, hd + 8), slice(0, 8))
        used = (slice(0, hd + 16), slice(0, 2 * hd))
        qts = (qat_ref[used[0], :], qbt_ref[used[1], :])
        dots = (doat_ref[own[0], :], dobt_ref[own[1], :])
        kts, vts = (kat_v, kbt_v), (vat_v, vbt_v)

        def step(kb, carry, masked, width=w):
            off = pl.multiple_of(kb * w, w)
            cols = pl.ds(off, width)
            scores = [_dot(qs[hh], kts[hh][:, cols]) for hh in range(2)]
            dps = [_dot(dos[hh], vts[hh][:, cols]) for hh in range(2)]
            ps, dss = [], []
            for hh in range(2):
                s = scores[hh]
                if masked:
                    s = jnp.where(_causal(i * t, off, (t, width)), s, NEG_BIG)
                p = jnp.exp(s)
                dss.append((p * dps[hh]).astype(BF16))
                ps.append(p.astype(BF16))
            out = []
            for hh in range(2):
                out.append(carry[hh] + _dot_nt(kts[hh][used[hh], cols], dss[hh]))
                dvt_acc[own[hh], cols] += _dot(dots[hh], ps[hh])
                with_sums = _dot(qts[hh], dss[hh])
                dkt_acc[own[hh], cols] += with_sums[own[hh], :]
                cs_acc[8 * hh:8 * hh + 8, cols] += with_sums[spare[hh], :]
            return tuple(out)

        diag = (i * t) // w
        init = (jnp.zeros((hd + 16, t), F32), jnp.zeros((2 * hd, t), F32))
        carry = lax.fori_loop(0, diag, lambda kb, c: step(kb, c, False), init)
        carry = lax.cond((i * t) % w + t <= w // 2,
                         lambda c: step(diag, c, True, w // 2), lambda c: step(diag, c, True), carry)
        for hh in range(2):
            dqt_ref[own[hh], :] = (carry[hh][own[hh], :] * (hd ** -0.5)).astype(BF16)
            rs_ref[0, 8 * hh:8 * hh + 8, :] = carry[hh][spare[hh], :]

        @pl.when(i == n_q - 1)
        def _():
            dkt_ref[...] = dkt_acc[...].astype(BF16)
            dvt_ref[...] = dvt_acc[...].astype(BF16)
            cs_ref[0] = cs_acc[...]

    qblk = pl.BlockSpec((t, LANES), lambda j, i: (i, j))
    qblk_t = pl.BlockSpec((LANES, t), lambda j, i: (j, i))
    whole_t = pl.BlockSpec((LANES, s_len), lambda j, i: (j, 0))
    ds_ = jax.ShapeDtypeStruct((d, s_len), BF16)
    sums = jax.ShapeDtypeStruct((n_pair, 16, s_len), F32)
    return pl.pallas_call(
        body, name="flash_bwd", grid=(n_pair, n_q),
        in_specs=[qblk, qblk, qblk_t, qblk_t, ANY, ANY, ANY, ANY, qblk, qblk, qblk_t, qblk_t],
        out_specs=[qblk_t, whole_t, whole_t, pl.BlockSpec((1, 16, t), lambda j, i: (j, 0, i)),
                   pl.BlockSpec((1, 16, s_len), lambda j, i: (j, 0, 0))],
        out_shape=[ds_, ds_, ds_, sums, sums],
        scratch_shapes=[pltpu.VMEM((LANES, s_len), BF16), pltpu.VMEM((LANES, s_len), BF16),
                        pltpu.VMEM((LANES, s_len), BF16), pltpu.VMEM((LANES, s_len), BF16),
                        pltpu.VMEM((LANES, s_len), F32), pltpu.VMEM((LANES, s_len), F32),
                        pltpu.VMEM((16, s_len), F32), pltpu.SemaphoreType.DMA((4,))],
        compiler_params=_params(2),
    )(qa, qb, qat, qbt, kat, kbt, vat, vbt, doa, dob, doat, dobt)


def _wgrad_t(at, b, name):
    k, s_len = at.shape
    n = b.shape[1]
    tn, tk, ts = min(n, 1024), min(k, 1024), min(s_len, 1024)

    def body(a_ref, b_ref, o_ref):
        @pl.when(pl.program_id(2) == 0)
        def _():
            o_ref[...] = jnp.zeros_like(o_ref)
        o_ref[...] += _dot(a_ref[...].astype(BF16), b_ref[...].astype(BF16))

    return pl.pallas_call(
        body, name=name, grid=(k // tk, n // tn, s_len // ts),
        in_specs=[pl.BlockSpec((tk, ts), lambda a, b_, c: (a, c)), pl.BlockSpec((ts, tn), lambda a, b_, c: (c, b_))],
        out_specs=pl.BlockSpec((tk, tn), lambda a, b_, c: (a, b_)),
        out_shape=jax.ShapeDtypeStruct((k, n), F32),
        compiler_params=_params(3),
    )(at, b)


def _oproj_bwd(dh, o, wo, seg, sel_q, tm):
    s_len, d = dh.shape
    sq_a, sq_b, _, _ = sel_q

    def body(dh_ref, o_ref, wo_hbm, seg_ref, sqa_ref, sqb_ref, doa_ref, dob_ref, doat_ref, dobt_ref, wo_v, sem):
        _load_once([(wo_hbm, wo_v)], sem)
        do = _dot_nt(dh_ref[...].astype(BF16), wo_v[...])
        parts = _parts(-_dot3_lhs(do * o_ref[...], seg_ref[...]))
        even = _even_head_lanes((tm, d), 1)
        doa = jnp.where(even, do, _dot(parts, sqa_ref[...]))
        dob = jnp.where(even, _dot(parts, sqb_ref[...]), do)
        doa_ref[...] = doa.astype(BF16)
        dob_ref[...] = dob.astype(BF16)
        doat_ref[...] = doa.T.astype(BF16)
        dobt_ref[...] = dob.T.astype(BF16)

    row = lambda i: (i, 0)
    const2 = lambda i: (0, 0)
    rs, cs = pl.BlockSpec((tm, d), row), pl.BlockSpec((d, tm), lambda i: (0, i))
    sel = pl.BlockSpec((N_SPARE * LANES, d), const2)
    sd = jax.ShapeDtypeStruct((s_len, d), BF16)
    ds_ = jax.ShapeDtypeStruct((d, s_len), BF16)
    return pl.pallas_call(
        body, name="oproj_bwd", grid=(s_len // tm,),
        in_specs=[rs, rs, ANY, pl.BlockSpec((d, LANES), const2), sel, sel],
        out_specs=[rs, rs, cs, cs], out_shape=[sd, sd, ds_, ds_],
        scratch_shapes=[pltpu.VMEM((d, d), BF16), pltpu.SemaphoreType.DMA((1,))],
        compiler_params=_params(1),
    )(dh, o, wo, seg, sq_a, sq_b)


def _oproj_fwd(h, o, wo, tm):
    s_len, d = h.shape

    def body(h_ref, o_ref, wo_hbm, hout_ref, wo_v, sem):
        _load_once([(wo_hbm, wo_v)], sem)
        hout_ref[...] = h_ref[...] + _dot(o_ref[...].astype(BF16), wo_v[...])

    row = lambda i: (i, 0)
    return pl.pallas_call(
        body, name="oproj_fwd", grid=(s_len // tm,),
        in_specs=[pl.BlockSpec((tm, d), row), pl.BlockSpec((tm, d), row), ANY],
        out_specs=pl.BlockSpec((tm, d), row),
        out_shape=jax.ShapeDtypeStruct((s_len, d), F32),
        scratch_shapes=[pltpu.VMEM((d, d), BF16), pltpu.SemaphoreType.DMA((1,))],
        compiler_params=_params(1),
    )(h, o, wo)


def _forget_bwd(dcum, z, tm):
    s_len = dcum.shape[0]
    n_blk = s_len // tm

    def body(dc_ref, z_ref, dfl_ref, gb_ref, total):
        i = pl.program_id(0)

        @pl.when(i == 0)
        def _():
            total[...] = jnp.zeros_like(total)
            gb_ref[...] = jnp.zeros_like(gb_ref)

        upper = (lax.broadcasted_iota(jnp.int32, (tm, tm), 0) <= lax.broadcasted_iota(jnp.int32, (tm, tm), 1))
        suffix = _dot3_rhs(jnp.where(upper, 1.0, 0.0).astype(BF16), dc_ref[...]) + total[0:1, :]
        total[...] = jnp.broadcast_to(suffix[0:1, :], total.shape)
        dfl = suffix * _sigmoid(-z_ref[...])
        dfl_ref[...] = dfl
        gb_ref[...] += jnp.sum(dfl, axis=0, keepdims=True)

    rev = lambda i: (n_blk - 1 - i, 0)
    return pl.pallas_call(
        body, name="forget_bwd", grid=(n_blk,),
        in_specs=[pl.BlockSpec((tm, LANES), rev), pl.BlockSpec((tm, LANES), rev)],
        out_specs=[pl.BlockSpec((tm, LANES), rev), pl.BlockSpec((1, LANES), lambda i: (0, 0))],
        out_shape=[jax.ShapeDtypeStruct((s_len, LANES), F32), jax.ShapeDtypeStruct((1, LANES), F32)],
        scratch_shapes=[pltpu.VMEM((8, LANES), F32)],
        compiler_params=_params(1),
    )(dcum, z)


def _fox_proj_bwd(h, dh, dqt, dkt, dvt, dfl, g_norm, wq, wk, wv, wf, tm):
    s_len, d = h.shape

    def body(h_ref, dh_ref, dqt_ref, dkt_ref, dvt_ref, dfl_ref, g_ref, wq_hbm, wk_hbm, wv_hbm, wf_ref,
             dhin_ref, dflb_ref, gn_ref, wq_v, wk_v, wv_v, sem):
        _load_once([(wq_hbm, wq_v), (wk_hbm, wk_v), (wv_hbm, wv_v)], sem)

        @pl.when(pl.program_id(0) == 0)
        def _():
            gn_ref[...] = jnp.zeros_like(gn_ref)

        g = g_ref[...]
        xhat, r, _ = _rms_fwd(h_ref[...], g)
        dflb = dfl_ref[...].astype(BF16)
        dflb_ref[...] = dflb
        from_qkv = (_dot(wq_v[...], dqt_ref[...]) + _dot(wk_v[...], dkt_ref[...])
                    + _dot(wv_v[...], dvt_ref[...]))
        dhn = _dot_nt(dflb, wf_ref[...]) + from_qkv.T
        dx, gg = _rms_bwd(dhn, xhat, r, g)
        gn_ref[...] += jnp.sum(gg, axis=0, keepdims=True)
        dhin_ref[...] = dh_ref[...] + dx

    row = lambda i: (i, 0)
    const2 = lambda i: (0, 0)
    rs = pl.BlockSpec((tm, d), row)
    cs = pl.BlockSpec((d, tm), lambda i: (0, i))
    return pl.pallas_call(
        body, name="fox_proj_bwd", grid=(s_len // tm,),
        in_specs=[rs, rs, cs, cs, cs, pl.BlockSpec((tm, LANES), row), pl.BlockSpec((1, d), const2), ANY, ANY, ANY,
                  pl.BlockSpec((d, LANES), const2)],
        out_specs=[rs, pl.BlockSpec((tm, LANES), row), pl.BlockSpec((1, d), const2)],
        out_shape=[jax.ShapeDtypeStruct((s_len, d), F32), jax.ShapeDtypeStruct((s_len, LANES), BF16),
                   jax.ShapeDtypeStruct((1, d), F32)],
        scratch_shapes=[pltpu.VMEM((d, d), BF16), pltpu.VMEM((d, d), BF16), pltpu.VMEM((d, d), BF16),
                        pltpu.SemaphoreType.DMA((3,))],
        compiler_params=_params(1),
    )(h, dh, dqt, dkt, dvt, dfl, g_norm, wq, wk, wv, wf)


def _loss_head(h, target, g_final, tm):
    s_len, d = h.shape
    n_blk = s_len // tm

    def body(h_ref, t_ref, g_ref, dh_ref, loss_ref, gg_ref, sq):
        i = pl.program_id(0)

        @pl.when(i == 0)
        def _():
            sq[...] = jnp.zeros_like(sq)
            gg_ref[...] = jnp.zeros_like(gg_ref)

        g = g_ref[...]
        xhat, r, y = _rms_fwd(h_ref[...], g)
        err = y - t_ref[...]
        sq[...] += jnp.sum(err * err, axis=0, keepdims=True)
        dx, gg = _rms_bwd(err * (1.0 / d), xhat, r, g)
        gg_ref[...] += jnp.sum(gg, axis=0, keepdims=True)
        dh_ref[...] = dx

        @pl.when(i == n_blk - 1)
        def _():
            loss_ref[...] = jnp.broadcast_to(jnp.sum(sq[...], axis=1, keepdims=True) * (0.5 / d), loss_ref.shape)

    row = lambda i: (i, 0)
    const2 = lambda i: (0, 0)
    return pl.pallas_call(
        body, name="loss_head", grid=(n_blk,),
        in_specs=[pl.BlockSpec((tm, d), row), pl.BlockSpec((tm, d), row), pl.BlockSpec((1, d), const2)],
        out_specs=[pl.BlockSpec((tm, d), row), pl.BlockSpec((1, LANES), const2), pl.BlockSpec((1, d), const2)],
        out_shape=[jax.ShapeDtypeStruct((s_len, d), F32), jax.ShapeDtypeStruct((1, LANES), F32),
                   jax.ShapeDtypeStruct((1, d), F32)],
        scratch_shapes=[pltpu.VMEM((1, d), F32)],
        compiler_params=_params(1),
    )(h, target, g_final)


def _wgrad(x, dy, n_piece, name):
    s_len, k = x.shape
    n = dy.shape[1]
    tn = min(n // n_piece, 1024)
    tk = min(k, 1024)
    ts = min(s_len, 1024)
    per_piece = (n // n_piece) // tn

    def body(x_ref, dy_ref, o_ref):
        @pl.when(pl.program_id(2) == 0)
        def _():
            o_ref[...] = jnp.zeros_like(o_ref)
        o_ref[0] += _dot_tn(x_ref[...].astype(BF16), dy_ref[...].astype(BF16))

    return pl.pallas_call(
        body, name=name, grid=(k // tk, n // tn, s_len // ts),
        in_specs=[pl.BlockSpec((ts, tk), lambda a, b, c: (c, a)), pl.BlockSpec((ts, tn), lambda a, b, c: (c, b))],
        out_specs=pl.BlockSpec((1, tk, tn), lambda a, b, c: (b // per_piece, a, b % per_piece)),
        out_shape=jax.ShapeDtypeStruct((n_piece, k, n // n_piece), F32),
        compiler_params=_params(3),
    )(x, dy)


def _pair_sum(g, recv, core, name):
    n_piece, rows, c = g.shape
    half = rows // 2
    tr = min(half, 512)
    nb = half // tr

    def body(core_ref, g_ref, r_ref, o_ref, ob_ref):
        total = g_ref[...] + r_ref[...]
        o_ref[...] = total
        ob_ref[...] = total.astype(BF16)

    blk = pl.BlockSpec((1, tr, c), lambda p, i, core_ref: (p, i, 0))
    return pl.pallas_call(
        body, name=name,
        out_shape=[jax.ShapeDtypeStruct((n_piece, half, c), F32), jax.ShapeDtypeStruct((n_piece, half, c), BF16)],
        grid_spec=pltpu.PrefetchScalarGridSpec(
            num_scalar_prefetch=1, grid=(n_piece, nb),
            in_specs=[pl.BlockSpec((1, tr, c), lambda p, i, core_ref: (p, core_ref[0] * nb + i, 0)), blk],
            out_specs=[blk, blk]),
        compiler_params=_params(2),
    )(core, g, recv)


def _chip_sum(halves, recv, chip, name):
    _, h, c = halves.shape
    tr = min(h, 512)

    def body(chip_ref, own_ref, r_ref, o_ref):
        o_ref[...] = ((own_ref[0] + r_ref[0].astype(F32)) + r_ref[1].astype(F32)) + r_ref[2].astype(F32)

    return pl.pallas_call(
        body, name=name, out_shape=jax.ShapeDtypeStruct((h, c), F32),
        grid_spec=pltpu.PrefetchScalarGridSpec(
            num_scalar_prefetch=1, grid=(h // tr,),
            in_specs=[pl.BlockSpec((1, tr, c), lambda i, chip_ref: (chip_ref[0], i, 0)),
                      pl.BlockSpec((3, tr, c), lambda i, chip_ref: (0, i, 0))],
            out_specs=pl.BlockSpec((tr, c), lambda i, chip_ref: (i, 0))),
        compiler_params=_params(1),
    )(chip, halves, recv)


def _adamw_math(w, m, v, g):
    m_new = ADAM_B1 * m + (1.0 - ADAM_B1) * g
    v_new = ADAM_B2 * v + (1.0 - ADAM_B2) * (g * g)
    m_hat = m_new / (1.0 - ADAM_B1 ** ADAM_STEP)
    v_hat = v_new / (1.0 - ADAM_B2 ** ADAM_STEP)
    return -ADAM_LR * (m_hat / (jnp.sqrt(v_hat) + ADAM_EPS) + ADAM_WD * w), m_new, v_new


def _adamw(w, m, v, g, name):
    rows, c = w.shape
    tr = min(rows, 256)

    def body(w_ref, m_ref, v_ref, g_ref, d_ref, mo_ref, vo_ref):
        d_ref[...], mo_ref[...], vo_ref[...] = _adamw_math(w_ref[...], m_ref[...], v_ref[...], g_ref[...])

    spec = pl.BlockSpec((tr, c), lambda i: (i, 0))
    shape = jax.ShapeDtypeStruct((rows, c), F32)
    return pl.pallas_call(
        body, name=name, grid=(rows // tr,),
        in_specs=[spec] * 4, out_specs=[spec] * 3, out_shape=[shape] * 3,
        compiler_params=_params(1),
    )(w, m, v, g)


def _adamw_halves(w, m, v, g_own, g_other, core, name):
    rows, c = w.shape
    half = rows // 2
    tr = min(half, 256)
    nb = half // tr

    def body(core_ref, w_ref, m_ref, v_ref, own_ref, other_ref, g_ref, d_ref, mo_ref, vo_ref):
        mine = (pl.program_id(0) // nb) == core_ref[0]
        g = jnp.where(mine, own_ref[...], other_ref[...])
        g_ref[...] = g
        d_ref[...], mo_ref[...], vo_ref[...] = _adamw_math(w_ref[...], m_ref[...], v_ref[...], g)

    spec = pl.BlockSpec((tr, c), lambda i, core_ref: (i, 0))
    own = pl.BlockSpec((tr, c), lambda i, core_ref: (jnp.clip(i - core_ref[0] * nb, 0, nb - 1), 0))
    other = pl.BlockSpec((tr, c), lambda i, core_ref: (jnp.clip(i - (1 - core_ref[0]) * nb, 0, nb - 1), 0))
    shape = jax.ShapeDtypeStruct((rows, c), F32)
    return pl.pallas_call(
        body, name=name, out_shape=[shape] * 4,
        grid_spec=pltpu.PrefetchScalarGridSpec(
            num_scalar_prefetch=1, grid=(rows // tr,),
            in_specs=[spec, spec, spec, own, other], out_specs=[spec] * 4),
        compiler_params=_params(1),
    )(core, w, m, v, g_own, g_other)


def _place():
    x, y, c = lax.axis_index("x"), lax.axis_index("y"), lax.axis_index("c")
    chips = [(1 - x, y), (x, 1 - y), (1 - x, 1 - y)]
    return x, y, c, chips


def _all_gather_chips(shards):
    n = len(shards)
    halves = [s.shape[0] // 2 for s in shards]

    def half_of(ref, k, which):
        start = which * halves[k]
        if halves[k] % 8 == 0:
            start = pl.multiple_of(start, 8)
        return ref.at[pl.ds(start, halves[k])]

    def over_ici(*refs):
        ins, outs = refs[:n], refs[n:2 * n]
        send_sems, recv_sems = refs[2 * n:]
        x, y, c, chips = _place()
        mine = 2 * x + y
        sends = []
        for k in range(n):
            sends.append(pltpu.make_async_remote_copy(
                src_ref=ins[k], dst_ref=outs[k].at[mine], send_sem=send_sems.at[k, 3], recv_sem=recv_sems.at[k, 3],
                device_id=(x, y, 1 - c), device_id_type=MESH))
            for j, (tx, ty) in enumerate(chips):
                sends.append(pltpu.make_async_remote_copy(
                    src_ref=half_of(ins[k], k, c), dst_ref=half_of(outs[k].at[mine], k, c),
                    send_sem=send_sems.at[k, j], recv_sem=recv_sems.at[k, j],
                    device_id=(tx, ty, c), device_id_type=MESH))
        for cp in sends:
            cp.start()
        for cp in sends:
            cp.wait()

    gathered = pl.pallas_call(
        over_ici, name="weights_gather_ici",
        in_specs=[ANY] * n, out_specs=[ANY] * n,
        out_shape=[jax.ShapeDtypeStruct((4,) + s.shape, s.dtype) for s in shards],
        scratch_shapes=[pltpu.SemaphoreType.DMA((n, 4)), pltpu.SemaphoreType.DMA((n, 4))],
    )(*shards)

    def over_d2d(*refs):
        ins, outs = refs[:n], refs[n:2 * n]
        send_sems, recv_sems = refs[2 * n:]
        x, y, c, chips = _place()
        sends = []
        for k in range(n):
            for j, (tx, ty) in enumerate(chips):
                piece = half_of(outs[k].at[2 * tx + ty], k, c)
                sends.append(pltpu.make_async_remote_copy(
                    src_ref=piece, dst_ref=piece, send_sem=send_sems.at[k, j], recv_sem=recv_sems.at[k, j],
                    device_id=(x, y, 1 - c), device_id_type=MESH))
        for cp in sends:
            cp.start()
        for cp in sends:
            cp.wait()

    return pl.pallas_call(
        over_d2d, name="weights_gather_pair",
        in_specs=[ANY] * n, out_specs=[ANY] * n,
        out_shape=[jax.ShapeDtypeStruct(g.shape, g.dtype) for g in gathered],
        input_output_aliases={k: k for k in range(n)},
        scratch_shapes=[pltpu.SemaphoreType.DMA((n, 3)), pltpu.SemaphoreType.DMA((n, 3))],
    )(*gathered)


def _pair_exchange(grads):
    n = len(grads)

    def body(*refs):
        ins, outs = refs[:n], refs[n:2 * n]
        send_sems, recv_sems = refs[2 * n:]
        x, y, c, _ = _place()
        copies = []
        for k in range(n):
            half = grads[k].shape[1] // 2
            other = ins[k].at[:, pl.ds(pl.multiple_of((1 - c) * half, 8), half), :]
            copies.append(pltpu.make_async_remote_copy(
                src_ref=other, dst_ref=outs[k], send_sem=send_sems.at[k], recv_sem=recv_sems.at[k],
                device_id=(x, y, 1 - c), device_id_type=MESH))
        for cp in copies:
            cp.start()
        for cp in copies:
            cp.wait()

    return pl.pallas_call(
        body, name="grads_pair_exchange",
        in_specs=[ANY] * n, out_specs=[ANY] * n,
        out_shape=[jax.ShapeDtypeStruct((4, g.shape[1] // 2, g.shape[2]), F32) for g in grads],
        scratch_shapes=[pltpu.SemaphoreType.DMA((n,)), pltpu.SemaphoreType.DMA((n,))],
    )(*grads)


def _chip_scatter(halves):
    n = len(halves)

    def body(*refs):
        ins, outs = refs[:n], refs[n:2 * n]
        send_sems, recv_sems = refs[2 * n:]
        x, y, c, chips = _place()
        sends = []
        for k in range(n):
            for j, (tx, ty) in enumerate(chips):
                sends.append(pltpu.make_async_remote_copy(
                    src_ref=ins[k].at[2 * tx + ty], dst_ref=outs[k].at[j], send_sem=send_sems.at[k, j],
                    recv_sem=recv_sems.at[k, j], device_id=(tx, ty, c), device_id_type=MESH))
        for cp in sends:
            cp.start()
        for cp in sends:
            cp.wait()

    return pl.pallas_call(
        body, name="grads_chip_scatter",
        in_specs=[ANY] * n, out_specs=[ANY] * n,
        out_shape=[jax.ShapeDtypeStruct((3,) + hv.shape[1:], hv.dtype) for hv in halves],
        scratch_shapes=[pltpu.SemaphoreType.DMA((n, 3)), pltpu.SemaphoreType.DMA((n, 3))],
    )(*halves)


def _pair_share(finals):
    n = len(finals)

    def body(*refs):
        ins, outs = refs[:n], refs[n:2 * n]
        send_sems, recv_sems = refs[2 * n:]
        x, y, c, _ = _place()
        copies = [pltpu.make_async_remote_copy(
            src_ref=ins[k], dst_ref=outs[k], send_sem=send_sems.at[k], recv_sem=recv_sems.at[k],
            device_id=(x, y, 1 - c), device_id_type=MESH) for k in range(n)]
        for cp in copies:
            cp.start()
        for cp in copies:
            cp.wait()

    return pl.pallas_call(
        body, name="grads_pair_share",
        in_specs=[ANY] * n, out_specs=[ANY] * n,
        out_shape=[jax.ShapeDtypeStruct(fv.shape, F32) for fv in finals],
        scratch_shapes=[pltpu.SemaphoreType.DMA((n,)), pltpu.SemaphoreType.DMA((n,))],
    )(*finals)


def _small_all_reduce(buf):
    rows, c_ = buf.shape

    def body(in_ref, out_ref, pair_buf, slots, send_sems, recv_sems):
        x, y, c, chips = _place()
        mine = 2 * x + y
        pair = pltpu.make_async_remote_copy(
            src_ref=in_ref, dst_ref=pair_buf, send_sem=send_sems.at[0], recv_sem=recv_sems.at[0],
            device_id=(x, y, 1 - c), device_id_type=MESH)
        pair.start()
        pair.wait()
        slots[mine] = in_ref[...] + pair_buf[...]
        sends = [pltpu.make_async_remote_copy(
            src_ref=slots.at[mine], dst_ref=slots.at[mine], send_sem=send_sems.at[1 + j], recv_sem=recv_sems.at[1 + j],
            device_id=(tx, ty, c), device_id_type=MESH) for j, (tx, ty) in enumerate(chips)]
        for cp in sends:
            cp.start()
        for j, (tx, ty) in enumerate(chips):
            pltpu.make_async_remote_copy(
                src_ref=slots.at[mine], dst_ref=slots.at[2 * tx + ty], send_sem=send_sems.at[1 + j],
                recv_sem=recv_sems.at[1 + j], device_id=(tx, ty, c), device_id_type=MESH).wait()
        out_ref[...] = ((slots[0] + slots[1]) + slots[2]) + slots[3]

    vm = pl.BlockSpec(memory_space=pltpu.VMEM)
    return pl.pallas_call(
        body, name="small_all_reduce", in_specs=[vm], out_specs=vm,
        out_shape=jax.ShapeDtypeStruct((rows, c_), F32),
        scratch_shapes=[pltpu.VMEM((rows, c_), F32), pltpu.VMEM((4, rows, c_), F32),
                        pltpu.SemaphoreType.DMA((4,)), pltpu.SemaphoreType.DMA((4,))],
        compiler_params=pltpu.CompilerParams(vmem_limit_bytes=VMEM_LIMIT_V7X),
    )(buf)


def _reduce_scatter(grads):
    core = lax.axis_index("c").astype(jnp.int32).reshape(1)
    chip = (2 * lax.axis_index("x") + lax.axis_index("y")).astype(jnp.int32).reshape(1)
    recv = _pair_exchange(grads)
    halves = [_pair_sum(g, r, core, f"pair_sum_{k}") for k, (g, r) in enumerate(zip(grads, recv))]
    recv = _chip_scatter([hb for _, hb in halves])
    finals = [_chip_sum(hv, r, chip, f"chip_sum_{k}") for k, ((hv, _), r) in enumerate(zip(halves, recv))]
    return list(zip(finals, _pair_share(finals))), core


PACK_COLS = 1024


def _pack(arrays):
    flat = jnp.concatenate([a.reshape(-1).astype(F32) for a in arrays])
    rows = -(-flat.shape[0] // PACK_COLS)
    rows = -(-rows // 8) * 8
    return jnp.pad(flat, (0, rows * PACK_COLS - flat.shape[0])).reshape(rows, PACK_COLS)


def _unpack(buf, shapes):
    flat = buf.reshape(-1)
    out, at = [], 0
    for shp in shapes:
        size = math.prod(shp)
        out.append(flat[at:at + size].reshape(shp))
        at += size
    return out


def kernel(x, mix_norm_g, ffn_norm_g, gm_w_in, gm_ln_g, gm_ln_b, gm_w_s, gm_b_s, gm_w_out, fox_w_qkvf, fox_b_f, fox_w_o, ffn_w_gate, ffn_w_up, ffn_conv_w, ffn_conv_b, ffn_w_down, final_norm_g, loss_target, m_mix_norm_g, m_ffn_norm_g, m_gm_w_in, m_gm_ln_g, m_gm_ln_b, m_gm_w_s, m_gm_b_s, m_gm_w_out, m_fox_w_qkvf, m_fox_b_f, m_fox_w_o, m_ffn_w_gate, m_ffn_w_up, m_ffn_conv_w, m_ffn_conv_b, m_ffn_w_down, m_final_norm_g, v_mix_norm_g, v_ffn_norm_g, v_gm_w_in, v_gm_ln_g, v_gm_ln_b, v_gm_w_s, v_gm_b_s, v_gm_w_out, v_fox_w_qkvf, v_fox_b_f, v_fox_w_o, v_ffn_w_gate, v_ffn_w_up, v_ffn_conv_w, v_ffn_conv_b, v_ffn_w_down, v_final_norm_g):
    _, s_len, d = x.shape
    e = gm_ln_g.shape[1]
    f = ffn_conv_b.shape[1]
    n_head = fox_b_f.shape[1]
    n_pair = n_head // 2
    gd = e // GM_GROUPS
    qkvf_cols = fox_w_qkvf.shape[2]
    assert d == n_head * HEAD_DIM and d % (2 * LANES) == 0 and s_len % 512 == 0 and gd % LANES == 0
    assert gm_w_s.shape[2] == CHUNK and 4 * qkvf_cols == 3 * d + n_head
    tm = 256
    h0 = x[0]
    target = loss_target[0]

    gathered = _all_gather_chips([
        gm_w_in[0].astype(BF16), gm_w_out[0].astype(BF16), fox_w_qkvf[0].astype(BF16), fox_w_o[0].astype(BF16),
        ffn_w_gate.astype(BF16), ffn_w_up.astype(BF16), ffn_w_down.astype(BF16), ffn_conv_w])
    w_in, w_out4, qkvf4, wo4, wg_all, wu_all, wd_all, cw4 = gathered
    w_out = w_out4.reshape(e, d)
    qkvf = jnp.transpose(qkvf4, (1, 0, 2)).reshape(d, 4 * qkvf_cols)
    wq, wk, wv = qkvf[:, :d], qkvf[:, d:2 * d], qkvf[:, 2 * d:3 * d]
    wf = jnp.pad(qkvf[:, 3 * d:], ((0, 0), (0, LANES - n_head)))
    wo = wo4.reshape(d, d)
    conv_w_full = jnp.transpose(cw4, (1, 2, 0, 3)).reshape(2, 3, f)
    conv_w8 = jnp.pad(conv_w_full, ((0, 0), (0, 5), (0, 0)))
    bf_pad = jnp.pad(fox_b_f, ((0, 0), (0, LANES - n_head)))

    tril = jnp.tril(jnp.ones((CHUNK, CHUNK), bool))
    wc = jnp.where(tril[None], gm_w_s[0], 0.0).astype(BF16)
    wct = jnp.transpose(wc, (0, 2, 1))
    bias = jnp.repeat(gm_b_s[0].T, gd, axis=1)
    seg_groups = (jnp.arange(e)[:, None] // gd == jnp.arange(LANES)[None, :]).astype(BF16)
    seg_heads = (jnp.arange(d)[:, None] // HEAD_DIM == jnp.arange(LANES)[None, :]).astype(BF16)
    sel_q = _spare_selectors(d, key_side=False)
    sel_k = _spare_selectors(d, key_side=True)

    h1, a0, hn0, gated0 = _gmlp_fwd(h0, mix_norm_g[0:1], w_in, gm_ln_g, gm_ln_b, wc, bias, w_out, tm)
    h2, fa0, fup0, fhn0, fhid0 = _ffn_fwd(h1, ffn_norm_g[0:1], wg_all, wu_all, wd_all, 0, conv_w8[0], ffn_conv_b[0:1], tm)
    (hn1, qa, qb, kat, kbt, va, vb, vat, vbt, z_f) = _fox_proj_fwd(
        h2, mix_norm_g[1:2], wq, wk, wv, wf, bf_pad, sel_q, sel_k, tm)
    o, qa2, qb2, qat2, qbt2 = _flash_fwd(qa, qb, kat, kbt, va, vb)
    h3 = _oproj_fwd(h2, o, wo, tm)
    h4, fa1, fup1, fhn1, fhid1 = _ffn_fwd(h3, ffn_norm_g[1:2], wg_all, wu_all, wd_all, 1, conv_w8[1], ffn_conv_b[1:2], tm)

    dh4, loss_part, g_final = _loss_head(h4, target, final_norm_g.reshape(1, d), tm)
    dh3, da1, dup1, gcw1, gcb1, gfn1 = _ffn_bwd(h3, dh4, fa1, fup1, ffn_norm_g[1:2], wg_all, wu_all, wd_all, 1,
                                                conv_w8[1], ffn_conv_b[1:2], tm)
    g_gate1 = _wgrad(fhn1, da1, 4, "wgrad_gate_1")
    g_up1 = _wgrad(fhn1, dup1, 4, "wgrad_up_1")
    g_down1 = _wgrad(fhid1, dh4, 1, "wgrad_down_1").reshape(4, f // 4, d)

    doa, dob, doat, dobt = _oproj_bwd(dh3, o, wo, seg_heads, sel_q, tm)
    g_wo = _wgrad(o, dh3, 1, "wgrad_wo").reshape(4, d // 4, d)
    dqt, dkt, dvt, row_sums, col_sums = _flash_bwd(qa2, qb2, qat2, qbt2, kat, kbt, vat, vbt, doa, dob, doat, dobt)
    sums = row_sums[:, 0::8, :] - col_sums[:, N_SPARE::8, :]
    dcum = jnp.pad(sums.reshape(n_head, s_len).T, ((0, 0), (0, LANES - n_head)))
    dfl, g_bf = _forget_bwd(dcum, z_f, tm)
    dh2, dflb, gmn1 = _fox_proj_bwd(h2, dh3, dqt, dkt, dvt, dfl, mix_norm_g[1:2], wq, wk, wv, wf, tm)
    g_q = _wgrad_t(dqt, hn1, "wgrad_q").T
    g_k = _wgrad_t(dkt, hn1, "wgrad_k").T
    g_v = _wgrad_t(dvt, hn1, "wgrad_v").T
    g_f = _wgrad(hn1, dflb, 1, "wgrad_f")[0][:, :n_head]
    g_qkvf = jnp.concatenate([g_q, g_k, g_v, g_f], axis=1).reshape(d, 4, qkvf_cols).transpose(1, 0, 2)

    dh1, da0f, dup0, gcw0, gcb0, gfn0 = _ffn_bwd(h1, dh2, fa0, fup0, ffn_norm_g[0:1], wg_all, wu_all, wd_all, 0,
                                                 conv_w8[0], ffn_conv_b[0:1], tm)
    g_gate0 = _wgrad(fhn0, da0f, 4, "wgrad_gate_0")
    g_up0 = _wgrad(fhn0, dup0, 4, "wgrad_up_0")
    g_down0 = _wgrad(fhid0, dh2, 1, "wgrad_down_0").reshape(4, f // 4, d)

    dh0, da0, g_ws, g_bs_t, g_lng, g_lnb, gmn0 = _gmlp_bwd(
        h0, dh1, a0, mix_norm_g[0:1], w_in, gm_ln_g, gm_ln_b, wc, wct, bias, w_out, seg_groups, tm)
    g_win = _wgrad(hn0, da0, 4, "wgrad_gm_in")
    g_wout = _wgrad(gated0, dh1, 1, "wgrad_gm_out").reshape(4, e // 4, d)

    big, core = _reduce_scatter([g_win, g_wout, g_qkvf, g_wo, g_gate0, g_gate1, g_up0, g_up1, g_down0, g_down1])
    r_win, r_wout, r_qkvf, r_wo, r_gate0, r_gate1, r_up0, r_up1, r_down0, r_down1 = big

    small = [jnp.concatenate([gmn0, gmn1]), jnp.concatenate([gfn0, gfn1]), g_lng, g_lnb, g_ws[None],
             g_bs_t[:, :GM_GROUPS].T[None], g_bf[:, :n_head], jnp.stack([gcw0[:3], gcw1[:3]]),
             jnp.concatenate([gcb0, gcb1]), g_final.reshape(d), loss_part[0, :1]]
    small_shapes = [a.shape for a in small]
    reduced = _unpack(_small_all_reduce(_pack(small)), small_shapes)
    (r_mix, r_ffn, r_lng, r_lnb, r_ws, r_bs, r_bf, r_cw_full, r_cb, r_final, r_loss) = reduced
    chip = 2 * lax.axis_index("x") + lax.axis_index("y")
    r_cw = lax.dynamic_slice_in_dim(r_cw_full, chip * (f // 4), f // 4, axis=2)

    def update_big(name, w, m, v, per_layer):
        parts = [_adamw_halves(w[l], m[l], v[l], own, other, core, f"adamw_{name}_{l}")
                 for l, (own, other) in enumerate(per_layer)]
        return tuple(jnp.stack([p[i] for p in parts]) for i in range(4))

    res = {}
    res["gm_w_in"] = update_big("gm_w_in", gm_w_in, m_gm_w_in, v_gm_w_in, [r_win])
    res["gm_w_out"] = update_big("gm_w_out", gm_w_out, m_gm_w_out, v_gm_w_out, [r_wout])
    res["fox_w_qkvf"] = update_big("fox_w_qkvf", fox_w_qkvf, m_fox_w_qkvf, v_fox_w_qkvf, [r_qkvf])
    res["fox_w_o"] = update_big("fox_w_o", fox_w_o, m_fox_w_o, v_fox_w_o, [r_wo])
    res["ffn_w_gate"] = update_big("ffn_w_gate", ffn_w_gate, m_ffn_w_gate, v_ffn_w_gate, [r_gate0, r_gate1])
    res["ffn_w_up"] = update_big("ffn_w_up", ffn_w_up, m_ffn_w_up, v_ffn_w_up, [r_up0, r_up1])
    res["ffn_w_down"] = update_big("ffn_w_down", ffn_w_down, m_ffn_w_down, v_ffn_w_down, [r_down0, r_down1])

    small_names = ["mix_norm_g", "ffn_norm_g", "gm_ln_g", "gm_ln_b", "gm_w_s", "gm_b_s", "fox_b_f", "ffn_conv_w",
                   "ffn_conv_b", "final_norm_g"]
    small_w = [mix_norm_g, ffn_norm_g, gm_ln_g, gm_ln_b, gm_w_s, gm_b_s, fox_b_f, ffn_conv_w, ffn_conv_b, final_norm_g]
    small_m = [m_mix_norm_g, m_ffn_norm_g, m_gm_ln_g, m_gm_ln_b, m_gm_w_s, m_gm_b_s, m_fox_b_f, m_ffn_conv_w,
               m_ffn_conv_b, m_final_norm_g]
    small_v = [v_mix_norm_g, v_ffn_norm_g, v_gm_ln_g, v_gm_ln_b, v_gm_w_s, v_gm_b_s, v_fox_b_f, v_ffn_conv_w,
               v_ffn_conv_b, v_final_norm_g]
    small_g = [r_mix, r_ffn, r_lng, r_lnb, r_ws, r_bs, r_bf, r_cw, r_cb, r_final]
    shapes = [w.shape for w in small_w]
    small_g = [g.reshape(s) for g, s in zip(small_g, shapes)]
    dlt, mn, vn = _adamw(_pack(small_w), _pack(small_m), _pack(small_v), _pack(small_g), "adamw_small")
    for name, g, dl_, m_, v_ in zip(small_names, small_g, _unpack(dlt, shapes), _unpack(mn, shapes), _unpack(vn, shapes)):
        res[name] = (g, dl_, m_, v_)

    order = ["mix_norm_g", "ffn_norm_g", "gm_w_in", "gm_ln_g", "gm_ln_b", "gm_w_s", "gm_b_s", "gm_w_out", "fox_w_qkvf",
             "fox_b_f", "fox_w_o", "ffn_w_gate", "ffn_w_up", "ffn_conv_w", "ffn_conv_b", "ffn_w_down", "final_norm_g"]
    outs = [r_loss.reshape(()), dh0[None]]
    for part in range(4):
        outs += [res[name][part] for name in order]
    return tuple(outs)
```

```python
import functools
import math

import jax
import jax.numpy as jnp
from jax import lax
from jax.experimental import pallas as pl
from jax.experimental.pallas import tpu as pltpu

F32 = jnp.float32
BF16 = jnp.bfloat16

RMS_EPS = 1e-6
LN_EPS = 1e-5
CHUNK = 128
GM_GROUPS = 8
HEAD_DIM = 64
LANES = 128
ATT_BLOCK = 256
ATT_CHUNK = 1024
VMEM_LIMIT_V7X = 56 * 1024 * 1024

ADAM_LR = 0.001
ADAM_B1 = 0.9
ADAM_B2 = 0.999
ADAM_EPS = 1e-08
ADAM_WD = 0.01
ADAM_STEP = 10

MESH = pl.DeviceIdType.MESH
ANY = pl.BlockSpec(memory_space=pl.ANY)
NEG_BIG = -1e30


def _params(n_grid):
    return pltpu.CompilerParams(dimension_semantics=("arbitrary",) * n_grid, vmem_limit_bytes=VMEM_LIMIT_V7X)


def _dot(a, b):
    return jnp.dot(a, b, preferred_element_type=F32)


def _dot_nt(a, b):
    return lax.dot_general(a, b, (((1,), (1,)), ((), ())), preferred_element_type=F32)


def _dot_tn(a, b):
    return lax.dot_general(a, b, (((0,), (0,)), ((), ())), preferred_element_type=F32)


def _split3(x):
    hi = x.astype(BF16)
    r = x - hi.astype(F32)
    mid = r.astype(BF16)
    lo = (r - mid.astype(F32)).astype(BF16)
    return hi, mid, lo


def _dot3_lhs(x, m):
    hi, mid, lo = _split3(x)
    return _dot(hi, m) + _dot(mid, m) + _dot(lo, m)


def _dot3_rhs(m, x):
    hi, mid, lo = _split3(x)
    return _dot(m, hi) + _dot(m, mid) + _dot(m, lo)


def _load_once(pairs, sem):
    @pl.when(pl.program_id(0) == 0)
    def _():
        copies = [pltpu.make_async_copy(src, dst, sem.at[k]) for k, (src, dst) in enumerate(pairs)]
        for cp in copies:
            cp.start()
        for cp in copies:
            cp.wait()


def _rms_fwd(x, g):
    r = lax.rsqrt(jnp.mean(x * x, axis=-1, keepdims=True) + RMS_EPS)
    xhat = x * r
    return xhat, r, xhat * g


def _rms_bwd(dy, xhat, r, g):
    w = dy * g
    dx = r * (w - xhat * jnp.mean(w * xhat, axis=-1, keepdims=True))
    return dx, dy * xhat


def _gelu_parts(a):
    c = math.sqrt(2.0 / math.pi)
    a2 = a * a
    t = jnp.tanh(c * (a + 0.044715 * a * a2))
    z = 0.5 * a * (1.0 + t)
    dz = 0.5 * (1.0 + t) + 0.5 * a * (1.0 - t * t) * (c * (1.0 + 3.0 * 0.044715 * a2))
    return z, dz


def _sigmoid(x):
    return 1.0 / (1.0 + jnp.exp(-x))


def _gmlp_core(a, lng, lnb, wc_ref, bias, n_chunk, gd):
    e = a.shape[1] // 2
    z, dz = _gelu_parts(a)
    u = z[:, :e]
    v = z[:, e:]
    mu = jnp.mean(v, axis=-1, keepdims=True)
    vc = v - mu
    rstd = lax.rsqrt(jnp.mean(vc * vc, axis=-1, keepdims=True) + LN_EPS)
    vhat = vc * rstd
    vln = vhat * lng + lnb
    vlb = vln.astype(BF16)
    rows = []
    for ci in range(n_chunk):
        cols = []
        for g in range(GM_GROUPS):
            blk = vlb[ci * CHUNK:(ci + 1) * CHUNK, g * gd:(g + 1) * gd]
            cols.append(_dot(wc_ref[g], blk))
        rows.append(jnp.concatenate(cols, axis=1) + bias)
    s = rows[0] if n_chunk == 1 else jnp.concatenate(rows, axis=0)
    return dz, u, vhat, rstd, vlb, s


def _gmlp_fwd(h, g_mix, w_in, lng, lnb, wc, bias, w_out, tm):
    s_len, d = h.shape
    n_p, _, w = w_in.shape
    e = w_out.shape[0]
    gd = e // GM_GROUPS
    n_chunk = tm // CHUNK

    def body(h_ref, g_ref, win_hbm, lng_ref, lnb_ref, wc_ref, bias_ref, wout_hbm,
             hout_ref, a_ref, hn_ref, gated_ref, win_v, wout_v, sem):
        _load_once([(win_hbm, win_v), (wout_hbm, wout_v)], sem)
        x = h_ref[...]
        _, _, y = _rms_fwd(x, g_ref[...])
        hn = y.astype(BF16)
        hn_ref[...] = hn
        for p in range(n_p):
            a_ref[:, p * w:(p + 1) * w] = _dot(hn, win_v[p])
        _, u, _, _, _, s = _gmlp_core(a_ref[...], lng_ref[...], lnb_ref[...], wc_ref, bias_ref[...], n_chunk, gd)
        gated = (u * s).astype(BF16)
        gated_ref[...] = gated
        hout_ref[...] = x + _dot(gated, wout_v[...])

    row = lambda i: (i, 0)
    const2 = lambda i: (0, 0)
    return pl.pallas_call(
        body, name="gmlp_fwd", grid=(s_len // tm,),
        in_specs=[pl.BlockSpec((tm, d), row), pl.BlockSpec((1, d), const2), ANY,
                  pl.BlockSpec((1, e), const2), pl.BlockSpec((1, e), const2),
                  pl.BlockSpec(wc.shape, lambda i: (0, 0, 0)), pl.BlockSpec((CHUNK, e), const2), ANY],
        out_specs=[pl.BlockSpec((tm, d), row), pl.BlockSpec((tm, 2 * e), row),
                   pl.BlockSpec((tm, d), row), pl.BlockSpec((tm, e), row)],
        out_shape=[jax.ShapeDtypeStruct((s_len, d), F32), jax.ShapeDtypeStruct((s_len, 2 * e), F32),
                   jax.ShapeDtypeStruct((s_len, d), BF16), jax.ShapeDtypeStruct((s_len, e), BF16)],
        scratch_shapes=[pltpu.VMEM(w_in.shape, BF16), pltpu.VMEM(w_out.shape, BF16), pltpu.SemaphoreType.DMA((2,))],
        compiler_params=_params(1),
    )(h, g_mix, w_in, lng, lnb, wc, bias, w_out)


def _gmlp_bwd(h, dh, a, g_mix, w_in, lng, lnb, wc, wct, bias, w_out, seg, tm):
    s_len, d = h.shape
    n_p, _, w = w_in.shape
    e = w_out.shape[0]
    gd = e // GM_GROUPS
    n_chunk = tm // CHUNK
    n_blk = s_len // tm

    def body(h_ref, dh_ref, a_ref, g_ref, win_hbm, lng_ref, lnb_ref, wc_ref, wct_ref, bias_ref, wout_hbm, seg_ref,
             dhin_ref, da_ref, gws_ref, gbs_ref, glng_ref, glnb_ref, gmix_ref, win_v, wout_v, dsum, sem):
        i = pl.program_id(0)
        _load_once([(win_hbm, win_v), (wout_hbm, wout_v)], sem)

        @pl.when(i == 0)
        def _():
            gws_ref[...] = jnp.zeros_like(gws_ref)
            glng_ref[...] = jnp.zeros_like(glng_ref)
            glnb_ref[...] = jnp.zeros_like(glnb_ref)
            gmix_ref[...] = jnp.zeros_like(gmix_ref)
            dsum[...] = jnp.zeros_like(dsum)

        x = h_ref[...]
        dh_v = dh_ref[...]
        g = g_ref[...]
        lng_v = lng_ref[...]
        xhat, r, _ = _rms_fwd(x, g)
        dz_da, u, vhat, rstd, vlb, s = _gmlp_core(a_ref[...], lng_v, lnb_ref[...], wc_ref, bias_ref[...], n_chunk, gd)
        dg = _dot_nt(dh_v.astype(BF16), wout_v[...])
        du = dg * s
        ds = dg * u
        dsb = ds.astype(BF16)
        rows = []
        ds_acc = None
        for ci in range(n_chunk):
            lo, hi = ci * CHUNK, (ci + 1) * CHUNK
            cols = []
            for gi in range(GM_GROUPS):
                d_blk = dsb[lo:hi, gi * gd:(gi + 1) * gd]
                gws_ref[gi] += _dot_nt(d_blk, vlb[lo:hi, gi * gd:(gi + 1) * gd])
                cols.append(_dot(wct_ref[gi], d_blk))
            rows.append(jnp.concatenate(cols, axis=1))
            ds_acc = ds[lo:hi] if ds_acc is None else ds_acc + ds[lo:hi]
        dsum[...] += ds_acc
        dvln = rows[0] if n_chunk == 1 else jnp.concatenate(rows, axis=0)
        glng_ref[...] += jnp.sum(dvln * vhat, axis=0, keepdims=True)
        glnb_ref[...] += jnp.sum(dvln, axis=0, keepdims=True)
        dvhat = dvln * lng_v
        dv = rstd * (dvhat - jnp.mean(dvhat, axis=-1, keepdims=True)
                     - vhat * jnp.mean(dvhat * vhat, axis=-1, keepdims=True))
        da = jnp.concatenate([du, dv], axis=1) * dz_da
        dab = da.astype(BF16)
        da_ref[...] = dab
        dhn = _dot_nt(dab[:, :w], win_v[0])
        for p in range(1, n_p):
            dhn += _dot_nt(dab[:, p * w:(p + 1) * w], win_v[p])
        dx, gg = _rms_bwd(dhn, xhat, r, g)
        gmix_ref[...] += jnp.sum(gg, axis=0, keepdims=True)
        dhin_ref[...] = dh_v + dx

        @pl.when(i == n_blk - 1)
        def _():
            tril = lax.broadcasted_iota(jnp.int32, (CHUNK, CHUNK), 0) >= lax.broadcasted_iota(jnp.int32, (CHUNK, CHUNK), 1)
            for gi in range(GM_GROUPS):
                gws_ref[gi] = jnp.where(tril, gws_ref[gi], 0.0)
            gbs_ref[...] = _dot3_lhs(dsum[...], seg_ref[...])

    row = lambda i: (i, 0)
    const2 = lambda i: (0, 0)
    const3 = lambda i: (0, 0, 0)
    return pl.pallas_call(
        body, name="gmlp_bwd", grid=(n_blk,),
        in_specs=[pl.BlockSpec((tm, d), row), pl.BlockSpec((tm, d), row), pl.BlockSpec((tm, 2 * e), row),
                  pl.BlockSpec((1, d), const2), ANY, pl.BlockSpec((1, e), const2), pl.BlockSpec((1, e), const2),
                  pl.BlockSpec(wc.shape, const3), pl.BlockSpec(wct.shape, const3), pl.BlockSpec((CHUNK, e), const2),
                  ANY, pl.BlockSpec((e, LANES), const2)],
        out_specs=[pl.BlockSpec((tm, d), row), pl.BlockSpec((tm, 2 * e), row), pl.BlockSpec(wc.shape, const3),
                   pl.BlockSpec((CHUNK, LANES), const2), pl.BlockSpec((1, e), const2), pl.BlockSpec((1, e), const2),
                   pl.BlockSpec((1, d), const2)],
        out_shape=[jax.ShapeDtypeStruct((s_len, d), F32), jax.ShapeDtypeStruct((s_len, 2 * e), BF16),
                   jax.ShapeDtypeStruct(wc.shape, F32), jax.ShapeDtypeStruct((CHUNK, LANES), F32),
                   jax.ShapeDtypeStruct((1, e), F32), jax.ShapeDtypeStruct((1, e), F32), jax.ShapeDtypeStruct((1, d), F32)],
        scratch_shapes=[pltpu.VMEM(w_in.shape, BF16), pltpu.VMEM(w_out.shape, BF16), pltpu.VMEM((CHUNK, e), F32),
                        pltpu.SemaphoreType.DMA((2,))],
        compiler_params=_params(1),
    )(h, dh, a, g_mix, w_in, lng, lnb, wc, wct, bias, w_out, seg)


def _shift_down(a, k, fill):
    tm = a.shape[0]
    out = pltpu.roll(a, k, 0)
    rid = lax.broadcasted_iota(jnp.int32, a.shape, 0)
    for j in range(k):
        out = jnp.where(rid == j, fill[8 - k + j:8 - k + j + 1, :], out)
    return out


def _shift_up(a, k, fill):
    tm = a.shape[0]
    out = pltpu.roll(a, tm - k, 0)
    rid = lax.broadcasted_iota(jnp.int32, a.shape, 0)
    for j in range(k):
        out = jnp.where(rid == tm - k + j, fill[j:j + 1, :], out)
    return out


def _ffn_fwd(h, g_norm, wg_all, wu_all, wd_all, layer, conv_w, conv_b, tm):
    s_len, d = h.shape
    n_p = wg_all.shape[0]
    fq = wg_all.shape[3]
    f = n_p * fq

    def body(h_ref, g_ref, wg_hbm, wu_hbm, wd_hbm, cw_ref, cb_ref,
             hout_ref, a_ref, up_ref, hn_ref, hid_ref, wg_v, wu_v, wd_v, carry, sem):
        i = pl.program_id(0)
        _load_once([(wg_hbm.at[:, layer], wg_v), (wu_hbm.at[:, layer], wu_v), (wd_hbm.at[:, layer], wd_v)], sem)

        @pl.when(i == 0)
        def _():
            carry[...] = jnp.zeros_like(carry)

        x = h_ref[...]
        _, _, y = _rms_fwd(x, g_ref[...])
        hn = y.astype(BF16)
        hn_ref[...] = hn
        for p in range(n_p):
            a_ref[:, p * fq:(p + 1) * fq] = _dot(hn, wg_v[p])
            up_ref[:, p * fq:(p + 1) * fq] = _dot(hn, wu_v[p])
        a = a_ref[...]
        prev = carry[...]
        am1 = _shift_down(a, 1, prev)
        am2 = _shift_down(a, 2, prev)
        carry[...] = a[tm - 8:tm, :]
        cw = cw_ref[...]
        ac = cb_ref[...] + am2 * cw[0:1, :]
        ac = ac + am1 * cw[1:2, :]
        ac = ac + a * cw[2:3, :]
        hid = (ac * _sigmoid(ac) * up_ref[...]).astype(BF16)
        hid_ref[...] = hid
        y2 = _dot(hid[:, :fq], wd_v[0])
        for p in range(1, n_p):
            y2 += _dot(hid[:, p * fq:(p + 1) * fq], wd_v[p])
        hout_ref[...] = x + y2

    row = lambda i: (i, 0)
    const2 = lambda i: (0, 0)
    return pl.pallas_call(
        body, name=f"ffn_fwd_{layer}", grid=(s_len // tm,),
        in_specs=[pl.BlockSpec((tm, d), row), pl.BlockSpec((1, d), const2), ANY, ANY, ANY,
                  pl.BlockSpec((8, f), const2), pl.BlockSpec((1, f), const2)],
        out_specs=[pl.BlockSpec((tm, d), row), pl.BlockSpec((tm, f), row), pl.BlockSpec((tm, f), row),
                   pl.BlockSpec((tm, d), row), pl.BlockSpec((tm, f), row)],
        out_shape=[jax.ShapeDtypeStruct((s_len, d), F32), jax.ShapeDtypeStruct((s_len, f), F32),
                   jax.ShapeDtypeStruct((s_len, f), F32), jax.ShapeDtypeStruct((s_len, d), BF16),
                   jax.ShapeDtypeStruct((s_len, f), BF16)],
        scratch_shapes=[pltpu.VMEM((n_p, d, fq), BF16), pltpu.VMEM((n_p, d, fq), BF16), pltpu.VMEM((n_p, fq, d), BF16),
                        pltpu.VMEM((8, f), F32), pltpu.SemaphoreType.DMA((3,))],
        compiler_params=_params(1),
    )(h, g_norm, wg_all, wu_all, wd_all, conv_w, conv_b)


def _ffn_bwd(h, dh, a, up, g_norm, wg_all, wu_all, wd_all, layer, conv_w, conv_b, tm):
    s_len, d = h.shape
    n_p = wg_all.shape[0]
    fq = wg_all.shape[3]
    f = n_p * fq
    n_blk = s_len // tm
    t8 = tm // 8

    def body(h_ref, dh_ref, a_ref, ahalo_ref, up_ref, g_ref, wg_hbm, wu_hbm, wd_hbm, cw_ref, cb_ref,
             dhin_ref, da_ref, dup_ref, gcw_ref, gcb_ref, gn_ref, wg_v, wu_v, wd_v, carry, sem):
        i = pl.program_id(0)
        _load_once([(wg_hbm.at[:, layer], wg_v), (wu_hbm.at[:, layer], wu_v), (wd_hbm.at[:, layer], wd_v)], sem)

        @pl.when(i == 0)
        def _():
            carry[...] = jnp.zeros_like(carry)
            gcw_ref[...] = jnp.zeros_like(gcw_ref)
            gcb_ref[...] = jnp.zeros_like(gcb_ref)
            gn_ref[...] = jnp.zeros_like(gn_ref)

        x = h_ref[...]
        dh_v = dh_ref[...]
        g = g_ref[...]
        xhat, r, _ = _rms_fwd(x, g)
        a = a_ref[...]
        up_v = up_ref[...]
        prev = jnp.where(i == n_blk - 1, 0.0, ahalo_ref[...])
        am1 = _shift_down(a, 1, prev)
        am2 = _shift_down(a, 2, prev)
        cw = cw_ref[...]
        ac = cb_ref[...] + am2 * cw[0:1, :]
        ac = ac + am1 * cw[1:2, :]
        ac = ac + a * cw[2:3, :]
        sg = _sigmoid(ac)
        sil = ac * sg
        dhb = dh_v.astype(BF16)
        dhid = jnp.concatenate([_dot_nt(dhb, wd_v[p]) for p in range(n_p)], axis=1)
        dup = dhid * sil
        dac = dhid * up_v * (sg * (1.0 + ac * (1.0 - sg)))
        gcb_ref[...] += jnp.sum(dac, axis=0, keepdims=True)
        gcw_ref[0:1, :] += jnp.sum(dac * am2, axis=0, keepdims=True)
        gcw_ref[1:2, :] += jnp.sum(dac * am1, axis=0, keepdims=True)
        gcw_ref[2:3, :] += jnp.sum(dac * a, axis=0, keepdims=True)
        nxt = carry[...]
        dp1 = _shift_up(dac, 1, nxt)
        dp2 = _shift_up(dac, 2, nxt)
        carry[...] = dac[0:8, :]
        da = dac * cw[2:3, :] + dp1 * cw[1:2, :] + dp2 * cw[0:1, :]
        dab = da.astype(BF16)
        dupb = dup.astype(BF16)
        da_ref[...] = dab
        dup_ref[...] = dupb
        dhn = _dot_nt(dab[:, :fq], wg_v[0]) + _dot_nt(dupb[:, :fq], wu_v[0])
        for p in range(1, n_p):
            dhn += _dot_nt(dab[:, p * fq:(p + 1) * fq], wg_v[p]) + _dot_nt(dupb[:, p * fq:(p + 1) * fq], wu_v[p])
        dx, gg = _rms_bwd(dhn, xhat, r, g)
        gn_ref[...] += jnp.sum(gg, axis=0, keepdims=True)
        dhin_ref[...] = dh_v + dx

    rev = lambda i: (n_blk - 1 - i, 0)
    halo = lambda i: (jnp.maximum((n_blk - 1 - i) * t8 - 1, 0), 0)
    const2 = lambda i: (0, 0)
    return pl.pallas_call(
        body, name=f"ffn_bwd_{layer}", grid=(n_blk,),
        in_specs=[pl.BlockSpec((tm, d), rev), pl.BlockSpec((tm, d), rev), pl.BlockSpec((tm, f), rev),
                  pl.BlockSpec((8, f), halo), pl.BlockSpec((tm, f), rev), pl.BlockSpec((1, d), const2), ANY, ANY, ANY,
                  pl.BlockSpec((8, f), const2), pl.BlockSpec((1, f), const2)],
        out_specs=[pl.BlockSpec((tm, d), rev), pl.BlockSpec((tm, f), rev), pl.BlockSpec((tm, f), rev),
                   pl.BlockSpec((8, f), const2), pl.BlockSpec((1, f), const2), pl.BlockSpec((1, d), const2)],
        out_shape=[jax.ShapeDtypeStruct((s_len, d), F32), jax.ShapeDtypeStruct((s_len, f), BF16),
                   jax.ShapeDtypeStruct((s_len, f), BF16), jax.ShapeDtypeStruct((8, f), F32),
                   jax.ShapeDtypeStruct((1, f), F32), jax.ShapeDtypeStruct((1, d), F32)],
        scratch_shapes=[pltpu.VMEM((n_p, d, fq), BF16), pltpu.VMEM((n_p, d, fq), BF16), pltpu.VMEM((n_p, fq, d), BF16),
                        pltpu.VMEM((8, f), F32), pltpu.SemaphoreType.DMA((3,))],
        compiler_params=_params(1),
    )(h, dh, a, a, up, g_norm, wg_all, wu_all, wd_all, conv_w, conv_b)


def _even_head_lanes(shape, axis):
    return (lax.broadcasted_iota(jnp.int32, shape, axis) & HEAD_DIM) == 0


def _pair_select(lo, hi, shape):
    return jnp.where(lax.broadcasted_iota(jnp.int32, shape, 1) < HEAD_DIM, lo, hi)


def _causal(row0, col0, shape):
    return row0 + lax.broadcasted_iota(jnp.int32, shape, 0) >= col0 + lax.broadcasted_iota(jnp.int32, shape, 1)


N_SPARE = 3


def _spare_selectors(d, key_side):
    lane = jnp.arange(d)[None, :]
    row = jnp.arange(N_SPARE * LANES)[:, None]
    head, part = row % LANES, row // LANES
    off = N_SPARE if key_side else 0
    sel_a = ((head % 2 == 0) & (lane == LANES * (head // 2) + HEAD_DIM + off + part)).astype(F32)
    sel_b = ((head % 2 == 1) & (lane == LANES * (head // 2) + off + part)).astype(F32)
    sign = -1.0 if key_side else 1.0
    ones_off = 0 if key_side else N_SPARE
    in_pair = jnp.arange(d)[None, :] % LANES
    ones_a = ((in_pair >= HEAD_DIM + ones_off) & (in_pair < HEAD_DIM + ones_off + N_SPARE)).astype(F32)
    ones_b = ((in_pair >= ones_off) & (in_pair < ones_off + N_SPARE)).astype(F32)
    return (sign * sel_a).astype(BF16), (sign * sel_b).astype(BF16), ones_a, ones_b


def _parts(x):
    return jnp.concatenate(_split3(x), axis=1)


def _fox_proj_fwd(h, g_norm, wq, wk, wv, wf, bf, sel_q, sel_k, tm):
    s_len, d = h.shape
    sq_a, sq_b, oq_a, oq_b = sel_q
    sk_a, sk_b, ok_a, ok_b = sel_k

    def body(h_ref, g_ref, wq_hbm, wk_hbm, wv_hbm, wf_ref, bf_ref, sqa_ref, sqb_ref, oqa_ref, oqb_ref,
             ska_ref, skb_ref, oka_ref, okb_ref,
             hn_ref, qa_ref, qb_ref, kat_ref, kbt_ref, va_ref, vb_ref, vat_ref, vbt_ref, z_ref,
             wq_v, wk_v, wv_v, total, sem):
        i = pl.program_id(0)
        _load_once([(wq_hbm, wq_v), (wk_hbm, wk_v), (wv_hbm, wv_v)], sem)

        @pl.when(i == 0)
        def _():
            total[...] = jnp.zeros_like(total)

        x = h_ref[...]
        _, _, y = _rms_fwd(x, g_ref[...])
        hn = y.astype(BF16)
        hn_ref[...] = hn
        z = _dot(hn, wf_ref[...]) + bf_ref[...]
        z_ref[...] = z
        logf = jnp.minimum(z, 0.0) - jnp.log(1.0 + jnp.exp(-jnp.abs(z)))
        tri = (lax.broadcasted_iota(jnp.int32, (tm, tm), 0) >= lax.broadcasted_iota(jnp.int32, (tm, tm), 1))
        cum = _dot3_rhs(jnp.where(tri, 1.0, 0.0).astype(BF16), logf) + total[0:1, :]
        total[...] = jnp.broadcast_to(cum[tm - 1:tm, :], total.shape)
        parts = _parts(cum)

        even = _even_head_lanes((tm, d), 1)
        q = _dot(hn, wq_v[...]) * (HEAD_DIM ** -0.5)
        qa_ref[...] = jnp.where(even, q, _dot(parts, sqa_ref[...]) + oqa_ref[...]).astype(BF16)
        qb_ref[...] = jnp.where(even, _dot(parts, sqb_ref[...]) + oqb_ref[...], q).astype(BF16)
        k = _dot(hn, wk_v[...])
        ka = jnp.where(even, k, _dot(parts, ska_ref[...]) + oka_ref[...])
        kb = jnp.where(even, _dot(parts, skb_ref[...]) + okb_ref[...], k)
        kat_ref[...] = ka.T.astype(BF16)
        kbt_ref[...] = kb.T.astype(BF16)
        v = _dot(hn, wv_v[...])
        va = jnp.where(even, v, oka_ref[...])
        vb = jnp.where(even, okb_ref[...], v)
        va_ref[...] = va.astype(BF16)
        vb_ref[...] = vb.astype(BF16)
        vat_ref[...] = va.T.astype(BF16)
        vbt_ref[...] = vb.T.astype(BF16)

    row = lambda i: (i, 0)
    col = lambda i: (0, i)
    const2 = lambda i: (0, 0)
    sd = jax.ShapeDtypeStruct((s_len, d), BF16)
    ds_ = jax.ShapeDtypeStruct((d, s_len), BF16)
    rs, cs = pl.BlockSpec((tm, d), row), pl.BlockSpec((d, tm), col)
    sel = pl.BlockSpec((N_SPARE * LANES, d), const2)
    one = pl.BlockSpec((1, d), const2)
    return pl.pallas_call(
        body, name="fox_proj_fwd", grid=(s_len // tm,),
        in_specs=[rs, one, ANY, ANY, ANY, pl.BlockSpec((d, LANES), const2), pl.BlockSpec((1, LANES), const2),
                  sel, sel, one, one, sel, sel, one, one],
        out_specs=[rs, rs, rs, cs, cs, rs, rs, cs, cs, pl.BlockSpec((tm, LANES), row)],
        out_shape=[sd, sd, sd, ds_, ds_, sd, sd, ds_, ds_, jax.ShapeDtypeStruct((s_len, LANES), F32)],
        scratch_shapes=[pltpu.VMEM((d, d), BF16), pltpu.VMEM((d, d), BF16), pltpu.VMEM((d, d), BF16),
                        pltpu.VMEM((8, LANES), F32), pltpu.SemaphoreType.DMA((3,))],
        compiler_params=_params(1),
    )(h, g_norm, wq, wk, wv, wf, bf, sq_a, sq_b, oq_a, oq_b, sk_a, sk_b, ok_a, ok_b)


def _spare_cols(x, base):
    xf = x[:, base:base + N_SPARE].astype(F32)
    return xf[:, 0:1] + xf[:, 1:2] + xf[:, 2:3]


def _with_query_term(x, term, base):
    lane = lax.broadcasted_iota(jnp.int32, x.shape, 1)
    hi, mid, lo = _split3(term)
    out = jnp.where(lane == base, hi.astype(F32), x)
    out = jnp.where(lane == base + 1, mid.astype(F32), out)
    out = jnp.where(lane == base + 2, lo.astype(F32), out)
    return jnp.where((lane >= base + N_SPARE) & (lane < base + 2 * N_SPARE), 1.0, out)


def _flash_fwd(qa, qb, kat, kbt, va, vb):
    s_len, d = qa.shape
    sub = ATT_BLOCK
    n_sub = 2 if s_len % (2 * sub) == 0 else 1
    t = n_sub * sub
    w = min(ATT_CHUNK, s_len)
    n_pair = d // LANES
    n_q = s_len // t
    bases = (HEAD_DIM, 0)
    chains = [(r, hh) for r in range(n_sub) for hh in range(2)]

    def body(qa_ref, qb_ref, kat_ref, kbt_ref, va_ref, vb_ref, o_ref, qa2_ref, qb2_ref, qat2_ref, qbt2_ref):
        i = pl.program_id(1)
        q_refs = (qa_ref, qb_ref)
        qs = [q_refs[hh][r * sub:(r + 1) * sub, :] for r, hh in chains]
        kts = (kat_ref, kbt_ref)
        vs = (va_ref, vb_ref)

        def step(kb, carry, masked, width=w):
            off = pl.multiple_of(kb * w, w)
            cols = pl.ds(off, width)
            scores = [_dot(qs[c], kts[hh][:, cols]) for c, (r, hh) in enumerate(chains)]
            probs, stats = [], []
            for c, (r, hh) in enumerate(chains):
                m, _ = carry[c]
                s = scores[c]
                if masked:
                    s = jnp.where(_causal(i * t + r * sub, off, (sub, width)), s, NEG_BIG)
                m_new = jnp.maximum(m, jnp.max(s, axis=1, keepdims=True))
                probs.append(jnp.exp(s - m_new).astype(BF16))
                stats.append((m_new, jnp.exp(m - m_new)))
            return tuple((stats[c][0], carry[c][1] * stats[c][1] + _dot(probs[c], vs[hh][cols, :]))
                         for c, (r, hh) in enumerate(chains))

        init = ((jnp.full((sub, 1), NEG_BIG, F32), jnp.zeros((sub, LANES), F32)),) * len(chains)
        diag = (i * t) // w
        carry = lax.fori_loop(0, diag, lambda kb, c: step(kb, c, False), init)
        carry = step(diag, carry, True)
        for r in range(n_sub):
            outs, q2 = [], []
            for hh in range(2):
                m, acc = carry[2 * r + hh]
                l = acc[:, bases[hh]:bases[hh] + 1]
                outs.append(acc / l)
                term = _spare_cols(qs[2 * r + hh], bases[hh]) - (m + jnp.log(l))
                q2.append(_with_query_term(qs[2 * r + hh].astype(F32), term, bases[hh]))
            rows = slice(r * sub, (r + 1) * sub)
            o_ref[rows, :] = _pair_select(outs[0], outs[1], (sub, LANES))
            qa2_ref[rows, :] = q2[0].astype(BF16)
            qb2_ref[rows, :] = q2[1].astype(BF16)
            qat2_ref[:, rows] = q2[0].T.astype(BF16)
            qbt2_ref[:, rows] = q2[1].T.astype(BF16)

    qblk = pl.BlockSpec((t, LANES), lambda j, i: (i, j))
    qblk_t = pl.BlockSpec((LANES, t), lambda j, i: (j, i))
    whole_t = pl.BlockSpec((LANES, s_len), lambda j, i: (j, 0))
    whole = pl.BlockSpec((s_len, LANES), lambda j, i: (0, j))
    sd = jax.ShapeDtypeStruct((s_len, d), BF16)
    ds_ = jax.ShapeDtypeStruct((d, s_len), BF16)
    return pl.pallas_call(
        body, name="flash_fwd", grid=(n_pair, n_q),
        in_specs=[qblk, qblk, whole_t, whole_t, whole, whole],
        out_specs=[qblk, qblk, qblk, qblk_t, qblk_t],
        out_shape=[jax.ShapeDtypeStruct((s_len, d), F32), sd, sd, ds_, ds_],
        compiler_params=_params(2),
    )(qa, qb, kat, kbt, va, vb)


def _flash_bwd(qa, qb, qat, qbt, kat, kbt, vat, vbt, doa, dob, doat, dobt):
    s_len, d = qa.shape
    sub = ATT_BLOCK
    n_sub = 2 if s_len % (2 * sub) == 0 else 1
    t = n_sub * sub
    w = min(ATT_CHUNK, s_len)
    n_pair = d // LANES
    n_q = s_len // t
    hd = HEAD_DIM
    chains = [(r, hh) for r in range(n_sub) for hh in range(2)]

    def body(qa_ref, qb_ref, qat_ref, qbt_ref, kat_hbm, kbt_hbm, vat_hbm, vbt_hbm, doa_ref, dob_ref, doat_ref, dobt_ref,
             dqt_ref, dkt_ref, dvt_ref, rs_ref, cs_ref,
             kat_v, kbt_v, vat_v, vbt_v, dkt_acc, dvt_acc, cs_acc, sem):
        j = pl.program_id(0)
        i = pl.program_id(1)

        @pl.when(i == 0)
        def _():
            rows = pl.ds(pl.multiple_of(j * LANES, LANES), LANES)
            copies = [pltpu.make_async_copy(src.at[rows, :], dst, sem.at[n]) for n, (src, dst) in enumerate([
                (kat_hbm, kat_v), (kbt_hbm, kbt_v), (vat_hbm, vat_v), (vbt_hbm, vbt_v)])]
            for cp in copies:
                cp.start()
            dkt_acc[...] = jnp.zeros_like(dkt_acc)
            dvt_acc[...] = jnp.zeros_like(dvt_acc)
            cs_acc[...] = jnp.zeros_like(cs_acc)
            for cp in copies:
                cp.wait()

        q_refs, do_refs = (qa_ref, qb_ref), (doa_ref, dob_ref)
        qs = [q_refs[hh][r * sub:(r + 1) * sub, :] for r, hh in chains]
        dos = [do_refs[hh][r * sub:(r + 1) * sub, :] for r, hh in chains]
        own = (slice(0, hd), slice(hd, 2 * hd))
        spare = (slice(hd, hd + 8), slice(0, 8))
        used = (slice(0, hd + 16), slice(0, 2 * hd))
        qts = (qat_ref[used[0], :], qbt_ref[used[1], :])
        dots = (doat_ref[own[0], :], dobt_ref[own[1], :])
        kts, vts = (kat_v, kbt_v), (vat_v, vbt_v)

        def step(kb, carry, masked, width=w):
            off = pl.multiple_of(kb * w, w)
            cols = pl.ds(off, width)
            scores = [_dot(qs[c], kts[hh][:, cols]) for c, (r, hh) in enumerate(chains)]
            dps = [_dot(dos[c], vts[hh][:, cols]) for c, (r, hh) in enumerate(chains)]
            ps, dss = [], []
            for c, (r, hh) in enumerate(chains):
                s = scores[c]
                if masked:
                    s = jnp.where(_causal(i * t + r * sub, off, (sub, width)), s, NEG_BIG)
                p = jnp.exp(s)
                dss.append((p * dps[c]).astype(BF16))
                ps.append(p.astype(BF16))
            out = tuple(carry[c] + _dot_nt(kts[hh][used[hh], cols], dss[c]) for c, (r, hh) in enumerate(chains))
            for hh in range(2):
                p_all = jnp.concatenate([ps[2 * r + hh] for r in range(n_sub)], axis=0)
                ds_all = jnp.concatenate([dss[2 * r + hh] for r in range(n_sub)], axis=0)
                dvt_acc[own[hh], cols] += _dot(dots[hh], p_all)
                with_sums = _dot(qts[hh], ds_all)
                dkt_acc[own[hh], cols] += with_sums[own[hh], :]
                cs_acc[8 * hh:8 * hh + 8, cols] += with_sums[spare[hh], :]
            return out

        diag = (i * t) // w
        init = (jnp.zeros((hd + 16, sub), F32), jnp.zeros((2 * hd, sub), F32)) * n_sub
        carry = lax.fori_loop(0, diag, lambda kb, c: step(kb, c, False), init)
        carry = lax.cond((i * t) % w + t <= w // 2,
                         lambda c: step(diag, c, True, w // 2), lambda c: step(diag, c, True), carry)
        for c, (r, hh) in enumerate(chains):
            at = slice(r * sub, (r + 1) * sub)
            dqt_ref[own[hh], at] = (carry[c][own[hh], :] * (hd ** -0.5)).astype(BF16)
            rs_ref[0, 8 * hh:8 * hh + 8, at] = carry[c][spare[hh], :]

        @pl.when(i == n_q - 1)
        def _():
            dkt_ref[...] = dkt_acc[...].astype(BF16)
            dvt_ref[...] = dvt_acc[...].astype(BF16)
            cs_ref[0] = cs_acc[...]

    qblk = pl.BlockSpec((t, LANES), lambda j, i: (i, j))
    qblk_t = pl.BlockSpec((LANES, t), lambda j, i: (j, i))
    whole_t = pl.BlockSpec((LANES, s_len), lambda j, i: (j, 0))
    ds_ = jax.ShapeDtypeStruct((d, s_len), BF16)
    sums = jax.ShapeDtypeStruct((n_pair, 16, s_len), F32)
    return pl.pallas_call(
        body, name="flash_bwd", grid=(n_pair, n_q),
        in_specs=[qblk, qblk, qblk_t, qblk_t, ANY, ANY, ANY, ANY, qblk, qblk, qblk_t, qblk_t],
        out_specs=[qblk_t, whole_t, whole_t, pl.BlockSpec((1, 16, t), lambda j, i: (j, 0, i)),
                   pl.BlockSpec((1, 16, s_len), lambda j, i: (j, 0, 0))],
        out_shape=[ds_, ds_, ds_, sums, sums],
        scratch_shapes=[pltpu.VMEM((LANES, s_len), BF16), pltpu.VMEM((LANES, s_len), BF16),
                        pltpu.VMEM((LANES, s_len), BF16), pltpu.VMEM((LANES, s_len), BF16),
                        pltpu.VMEM((LANES, s_len), F32), pltpu.VMEM((LANES, s_len), F32),
                        pltpu.VMEM((16, s_len), F32), pltpu.SemaphoreType.DMA((4,))],
        compiler_params=_params(2),
    )(qa, qb, qat, qbt, kat, kbt, vat, vbt, doa, dob, doat, dobt)


def _wgrad_t(at, b, name):
    k, s_len = at.shape
    n = b.shape[1]
    tn, tk, ts = min(n, 1024), min(k, 1024), min(s_len, 1024)

    def body(a_ref, b_ref, o_ref):
        @pl.when(pl.program_id(2) == 0)
        def _():
            o_ref[...] = jnp.zeros_like(o_ref)
        o_ref[...] += _dot(a_ref[...].astype(BF16), b_ref[...].astype(BF16))

    return pl.pallas_call(
        body, name=name, grid=(k // tk, n // tn, s_len // ts),
        in_specs=[pl.BlockSpec((tk, ts), lambda a, b_, c: (a, c)), pl.BlockSpec((ts, tn), lambda a, b_, c: (c, b_))],
        out_specs=pl.BlockSpec((tk, tn), lambda a, b_, c: (a, b_)),
        out_shape=jax.ShapeDtypeStruct((k, n), F32),
        compiler_params=_params(3),
    )(at, b)


def _oproj_bwd(dh, o, wo, seg, sel_q, tm):
    s_len, d = dh.shape
    sq_a, sq_b, _, _ = sel_q

    def body(dh_ref, o_ref, wo_hbm, seg_ref, sqa_ref, sqb_ref, doa_ref, dob_ref, doat_ref, dobt_ref, wo_v, sem):
        _load_once([(wo_hbm, wo_v)], sem)
        do = _dot_nt(dh_ref[...].astype(BF16), wo_v[...])
        parts = _parts(-_dot3_lhs(do * o_ref[...], seg_ref[...]))
        even = _even_head_lanes((tm, d), 1)
        doa = jnp.where(even, do, _dot(parts, sqa_ref[...]))
        dob = jnp.where(even, _dot(parts, sqb_ref[...]), do)
        doa_ref[...] = doa.astype(BF16)
        dob_ref[...] = dob.astype(BF16)
        doat_ref[...] = doa.T.astype(BF16)
        dobt_ref[...] = dob.T.astype(BF16)

    row = lambda i: (i, 0)
    const2 = lambda i: (0, 0)
    rs, cs = pl.BlockSpec((tm, d), row), pl.BlockSpec((d, tm), lambda i: (0, i))
    sel = pl.BlockSpec((N_SPARE * LANES, d), const2)
    sd = jax.ShapeDtypeStruct((s_len, d), BF16)
    ds_ = jax.ShapeDtypeStruct((d, s_len), BF16)
    return pl.pallas_call(
        body, name="oproj_bwd", grid=(s_len // tm,),
        in_specs=[rs, rs, ANY, pl.BlockSpec((d, LANES), const2), sel, sel],
        out_specs=[rs, rs, cs, cs], out_shape=[sd, sd, ds_, ds_],
        scratch_shapes=[pltpu.VMEM((d, d), BF16), pltpu.SemaphoreType.DMA((1,))],
        compiler_params=_params(1),
    )(dh, o, wo, seg, sq_a, sq_b)


def _oproj_fwd(h, o, wo, tm):
    s_len, d = h.shape

    def body(h_ref, o_ref, wo_hbm, hout_ref, wo_v, sem):
        _load_once([(wo_hbm, wo_v)], sem)
        hout_ref[...] = h_ref[...] + _dot(o_ref[...].astype(BF16), wo_v[...])

    row = lambda i: (i, 0)
    return pl.pallas_call(
        body, name="oproj_fwd", grid=(s_len // tm,),
        in_specs=[pl.BlockSpec((tm, d), row), pl.BlockSpec((tm, d), row), ANY],
        out_specs=pl.BlockSpec((tm, d), row),
        out_shape=jax.ShapeDtypeStruct((s_len, d), F32),
        scratch_shapes=[pltpu.VMEM((d, d), BF16), pltpu.SemaphoreType.DMA((1,))],
        compiler_params=_params(1),
    )(h, o, wo)


def _forget_bwd(dcum, z, tm):
    s_len = dcum.shape[0]
    n_blk = s_len // tm

    def body(dc_ref, z_ref, dfl_ref, gb_ref, total):
        i = pl.program_id(0)

        @pl.when(i == 0)
        def _():
            total[...] = jnp.zeros_like(total)
            gb_ref[...] = jnp.zeros_like(gb_ref)

        upper = (lax.broadcasted_iota(jnp.int32, (tm, tm), 0) <= lax.broadcasted_iota(jnp.int32, (tm, tm), 1))
        suffix = _dot3_rhs(jnp.where(upper, 1.0, 0.0).astype(BF16), dc_ref[...]) + total[0:1, :]
        total[...] = jnp.broadcast_to(suffix[0:1, :], total.shape)
        dfl = suffix * _sigmoid(-z_ref[...])
        dfl_ref[...] = dfl
        gb_ref[...] += jnp.sum(dfl, axis=0, keepdims=True)

    rev = lambda i: (n_blk - 1 - i, 0)
    return pl.pallas_call(
        body, name="forget_bwd", grid=(n_blk,),
        in_specs=[pl.BlockSpec((tm, LANES), rev), pl.BlockSpec((tm, LANES), rev)],
        out_specs=[pl.BlockSpec((tm, LANES), rev), pl.BlockSpec((1, LANES), lambda i: (0, 0))],
        out_shape=[jax.ShapeDtypeStruct((s_len, LANES), F32), jax.ShapeDtypeStruct((1, LANES), F32)],
        scratch_shapes=[pltpu.VMEM((8, LANES), F32)],
        compiler_params=_params(1),
    )(dcum, z)


def _fox_proj_bwd(h, dh, dqt, dkt, dvt, dfl, g_norm, wq, wk, wv, wf, tm):
    s_len, d = h.shape

    def body(h_ref, dh_ref, dqt_ref, dkt_ref, dvt_ref, dfl_ref, g_ref, wq_hbm, wk_hbm, wv_hbm, wf_ref,
             dhin_ref, dflb_ref, gn_ref, wq_v, wk_v, wv_v, sem):
        _load_once([(wq_hbm, wq_v), (wk_hbm, wk_v), (wv_hbm, wv_v)], sem)

        @pl.when(pl.program_id(0) == 0)
        def _():
            gn_ref[...] = jnp.zeros_like(gn_ref)

        g = g_ref[...]
        xhat, r, _ = _rms_fwd(h_ref[...], g)
        dflb = dfl_ref[...].astype(BF16)
        dflb_ref[...] = dflb
        from_qkv = (_dot(wq_v[...], dqt_ref[...]) + _dot(wk_v[...], dkt_ref[...])
                    + _dot(wv_v[...], dvt_ref[...]))
        dhn = _dot_nt(dflb, wf_ref[...]) + from_qkv.T
        dx, gg = _rms_bwd(dhn, xhat, r, g)
        gn_ref[...] += jnp.sum(gg, axis=0, keepdims=True)
        dhin_ref[...] = dh_ref[...] + dx

    row = lambda i: (i, 0)
    const2 = lambda i: (0, 0)
    rs = pl.BlockSpec((tm, d), row)
    cs = pl.BlockSpec((d, tm), lambda i: (0, i))
    return pl.pallas_call(
        body, name="fox_proj_bwd", grid=(s_len // tm,),
        in_specs=[rs, rs, cs, cs, cs, pl.BlockSpec((tm, LANES), row), pl.BlockSpec((1, d), const2), ANY, ANY, ANY,
                  pl.BlockSpec((d, LANES), const2)],
        out_specs=[rs, pl.BlockSpec((tm, LANES), row), pl.BlockSpec((1, d), const2)],
        out_shape=[jax.ShapeDtypeStruct((s_len, d), F32), jax.ShapeDtypeStruct((s_len, LANES), BF16),
                   jax.ShapeDtypeStruct((1, d), F32)],
        scratch_shapes=[pltpu.VMEM((d, d), BF16), pltpu.VMEM((d, d), BF16), pltpu.VMEM((d, d), BF16),
                        pltpu.SemaphoreType.DMA((3,))],
        compiler_params=_params(1),
    )(h, dh, dqt, dkt, dvt, dfl, g_norm, wq, wk, wv, wf)


def _loss_head(h, target, g_final, tm):
    s_len, d = h.shape
    n_blk = s_len // tm

    def body(h_ref, t_ref, g_ref, dh_ref, loss_ref, gg_ref, sq):
        i = pl.program_id(0)

        @pl.when(i == 0)
        def _():
            sq[...] = jnp.zeros_like(sq)
            gg_ref[...] = jnp.zeros_like(gg_ref)

        g = g_ref[...]
        xhat, r, y = _rms_fwd(h_ref[...], g)
        err = y - t_ref[...]
        sq[...] += jnp.sum(err * err, axis=0, keepdims=True)
        dx, gg = _rms_bwd(err * (1.0 / d), xhat, r, g)
        gg_ref[...] += jnp.sum(gg, axis=0, keepdims=True)
        dh_ref[...] = dx

        @pl.when(i == n_blk - 1)
        def _():
            loss_ref[...] = jnp.broadcast_to(jnp.sum(sq[...], axis=1, keepdims=True) * (0.5 / d), loss_ref.shape)

    row = lambda i: (i, 0)
    const2 = lambda i: (0, 0)
    return pl.pallas_call(
        body, name="loss_head", grid=(n_blk,),
        in_specs=[pl.BlockSpec((tm, d), row), pl.BlockSpec((tm, d), row), pl.BlockSpec((1, d), const2)],
        out_specs=[pl.BlockSpec((tm, d), row), pl.BlockSpec((1, LANES), const2), pl.BlockSpec((1, d), const2)],
        out_shape=[jax.ShapeDtypeStruct((s_len, d), F32), jax.ShapeDtypeStruct((1, LANES), F32),
                   jax.ShapeDtypeStruct((1, d), F32)],
        scratch_shapes=[pltpu.VMEM((1, d), F32)],
        compiler_params=_params(1),
    )(h, target, g_final)


def _wgrad(x, dy, n_piece, name):
    s_len, k = x.shape
    n = dy.shape[1]
    tn = min(n // n_piece, 1024)
    tk = min(k, 1024)
    ts = min(s_len, 1024)
    per_piece = (n // n_piece) // tn

    def body(x_ref, dy_ref, o_ref):
        @pl.when(pl.program_id(2) == 0)
        def _():
            o_ref[...] = jnp.zeros_like(o_ref)
        o_ref[0] += _dot_tn(x_ref[...].astype(BF16), dy_ref[...].astype(BF16))

    return pl.pallas_call(
        body, name=name, grid=(k // tk, n // tn, s_len // ts),
        in_specs=[pl.BlockSpec((ts, tk), lambda a, b, c: (c, a)), pl.BlockSpec((ts, tn), lambda a, b, c: (c, b))],
        out_specs=pl.BlockSpec((1, tk, tn), lambda a, b, c: (b // per_piece, a, b % per_piece)),
        out_shape=jax.ShapeDtypeStruct((n_piece, k, n // n_piece), F32),
        compiler_params=_params(3),
    )(x, dy)


def _pair_sum(g, recv, core, name):
    n_piece, rows, c = g.shape
    half = rows // 2
    tr = min(half, 512)
    nb = half // tr

    def body(core_ref, g_ref, r_ref, o_ref, ob_ref):
        total = g_ref[...] + r_ref[...]
        o_ref[...] = total
        ob_ref[...] = total.astype(BF16)

    blk = pl.BlockSpec((1, tr, c), lambda p, i, core_ref: (p, i, 0))
    return pl.pallas_call(
        body, name=name,
        out_shape=[jax.ShapeDtypeStruct((n_piece, half, c), F32), jax.ShapeDtypeStruct((n_piece, half, c), BF16)],
        grid_spec=pltpu.PrefetchScalarGridSpec(
            num_scalar_prefetch=1, grid=(n_piece, nb),
            in_specs=[pl.BlockSpec((1, tr, c), lambda p, i, core_ref: (p, core_ref[0] * nb + i, 0)), blk],
            out_specs=[blk, blk]),
        compiler_params=_params(2),
    )(core, g, recv)


def _chip_sum(halves, recv, chip, name):
    _, h, c = halves.shape
    tr = min(h, 512)

    def body(chip_ref, own_ref, r_ref, o_ref):
        o_ref[...] = ((own_ref[0] + r_ref[0].astype(F32)) + r_ref[1].astype(F32)) + r_ref[2].astype(F32)

    return pl.pallas_call(
        body, name=name, out_shape=jax.ShapeDtypeStruct((h, c), F32),
        grid_spec=pltpu.PrefetchScalarGridSpec(
            num_scalar_prefetch=1, grid=(h // tr,),
            in_specs=[pl.BlockSpec((1, tr, c), lambda i, chip_ref: (chip_ref[0], i, 0)),
                      pl.BlockSpec((3, tr, c), lambda i, chip_ref: (0, i, 0))],
            out_specs=pl.BlockSpec((tr, c), lambda i, chip_ref: (i, 0))),
        compiler_params=_params(1),
    )(chip, halves, recv)


def _adamw_math(w, m, v, g):
    m_new = ADAM_B1 * m + (1.0 - ADAM_B1) * g
    v_new = ADAM_B2 * v + (1.0 - ADAM_B2) * (g * g)
    m_hat = m_new / (1.0 - ADAM_B1 ** ADAM_STEP)
    v_hat = v_new / (1.0 - ADAM_B2 ** ADAM_STEP)
    return -ADAM_LR * (m_hat / (jnp.sqrt(v_hat) + ADAM_EPS) + ADAM_WD * w), m_new, v_new


def _adamw(w, m, v, g, name):
    rows, c = w.shape
    tr = min(rows, 256)

    def body(w_ref, m_ref, v_ref, g_ref, d_ref, mo_ref, vo_ref):
        d_ref[...], mo_ref[...], vo_ref[...] = _adamw_math(w_ref[...], m_ref[...], v_ref[...], g_ref[...])

    spec = pl.BlockSpec((tr, c), lambda i: (i, 0))
    shape = jax.ShapeDtypeStruct((rows, c), F32)
    return pl.pallas_call(
        body, name=name, grid=(rows // tr,),
        in_specs=[spec] * 4, out_specs=[spec] * 3, out_shape=[shape] * 3,
        compiler_params=_params(1),
    )(w, m, v, g)


def _adamw_halves(w, m, v, g_own, g_other, core, name):
    rows, c = w.shape
    half = rows // 2
    tr = min(half, 256)
    nb = half // tr

    def body(core_ref, w_ref, m_ref, v_ref, own_ref, other_ref, g_ref, d_ref, mo_ref, vo_ref):
        mine = (pl.program_id(0) // nb) == core_ref[0]
        g = jnp.where(mine, own_ref[...], other_ref[...])
        g_ref[...] = g
        d_ref[...], mo_ref[...], vo_ref[...] = _adamw_math(w_ref[...], m_ref[...], v_ref[...], g)

    spec = pl.BlockSpec((tr, c), lambda i, core_ref: (i, 0))
    own = pl.BlockSpec((tr, c), lambda i, core_ref: (jnp.clip(i - core_ref[0] * nb, 0, nb - 1), 0))
    other = pl.BlockSpec((tr, c), lambda i, core_ref: (jnp.clip(i - (1 - core_ref[0]) * nb, 0, nb - 1), 0))
    shape = jax.ShapeDtypeStruct((rows, c), F32)
    return pl.pallas_call(
        body, name=name, out_shape=[shape] * 4,
        grid_spec=pltpu.PrefetchScalarGridSpec(
            num_scalar_prefetch=1, grid=(rows // tr,),
            in_specs=[spec, spec, spec, own, other], out_specs=[spec] * 4),
        compiler_params=_params(1),
    )(core, w, m, v, g_own, g_other)


def _place():
    x, y, c = lax.axis_index("x"), lax.axis_index("y"), lax.axis_index("c")
    chips = [(1 - x, y), (x, 1 - y), (1 - x, 1 - y)]
    return x, y, c, chips


def _all_gather_chips(shards):
    n = len(shards)
    halves = [s.shape[0] // 2 for s in shards]

    def half_of(ref, k, which):
        start = which * halves[k]
        if halves[k] % 8 == 0:
            start = pl.multiple_of(start, 8)
        return ref.at[pl.ds(start, halves[k])]

    def over_ici(*refs):
        ins, outs = refs[:n], refs[n:2 * n]
        send_sems, recv_sems = refs[2 * n:]
        x, y, c, chips = _place()
        mine = 2 * x + y
        sends = []
        for k in range(n):
            sends.append(pltpu.make_async_remote_copy(
                src_ref=ins[k], dst_ref=outs[k].at[mine], send_sem=send_sems.at[k, 3], recv_sem=recv_sems.at[k, 3],
                device_id=(x, y, 1 - c), device_id_type=MESH))
            for j, (tx, ty) in enumerate(chips):
                sends.append(pltpu.make_async_remote_copy(
                    src_ref=half_of(ins[k], k, c), dst_ref=half_of(outs[k].at[mine], k, c),
                    send_sem=send_sems.at[k, j], recv_sem=recv_sems.at[k, j],
                    device_id=(tx, ty, c), device_id_type=MESH))
        for cp in sends:
            cp.start()
        for cp in sends:
            cp.wait()

    gathered = pl.pallas_call(
        over_ici, name="weights_gather_ici",
        in_specs=[ANY] * n, out_specs=[ANY] * n,
        out_shape=[jax.ShapeDtypeStruct((4,) + s.shape, s.dtype) for s in shards],
        scratch_shapes=[pltpu.SemaphoreType.DMA((n, 4)), pltpu.SemaphoreType.DMA((n, 4))],
    )(*shards)

    def over_d2d(*refs):
        ins, outs = refs[:n], refs[n:2 * n]
        send_sems, recv_sems = refs[2 * n:]
        x, y, c, chips = _place()
        sends = []
        for k in range(n):
            for j, (tx, ty) in enumerate(chips):
                piece = half_of(outs[k].at[2 * tx + ty], k, c)
                sends.append(pltpu.make_async_remote_copy(
                    src_ref=piece, dst_ref=piece, send_sem=send_sems.at[k, j], recv_sem=recv_sems.at[k, j],
                    device_id=(x, y, 1 - c), device_id_type=MESH))
        for cp in sends:
            cp.start()
        for cp in sends:
            cp.wait()

    return pl.pallas_call(
        over_d2d, name="weights_gather_pair",
        in_specs=[ANY] * n, out_specs=[ANY] * n,
        out_shape=[jax.ShapeDtypeStruct(g.shape, g.dtype) for g in gathered],
        input_output_aliases={k: k for k in range(n)},
        scratch_shapes=[pltpu.SemaphoreType.DMA((n, 3)), pltpu.SemaphoreType.DMA((n, 3))],
    )(*gathered)


def _pair_exchange(grads):
    n = len(grads)

    def body(*refs):
        ins, outs = refs[:n], refs[n:2 * n]
        send_sems, recv_sems = refs[2 * n:]
        x, y, c, _ = _place()
        copies = []
        for k in range(n):
            half = grads[k].shape[1] // 2
            other = ins[k].at[:, pl.ds(pl.multiple_of((1 - c) * half, 8), half), :]
            copies.append(pltpu.make_async_remote_copy(
                src_ref=other, dst_ref=outs[k], send_sem=send_sems.at[k], recv_sem=recv_sems.at[k],
                device_id=(x, y, 1 - c), device_id_type=MESH))
        for cp in copies:
            cp.start()
        for cp in copies:
            cp.wait()

    return pl.pallas_call(
        body, name="grads_pair_exchange",
        in_specs=[ANY] * n, out_specs=[ANY] * n,
        out_shape=[jax.ShapeDtypeStruct((4, g.shape[1] // 2, g.shape[2]), F32) for g in grads],
        scratch_shapes=[pltpu.SemaphoreType.DMA((n,)), pltpu.SemaphoreType.DMA((n,))],
    )(*grads)


def _chip_scatter(halves):
    n = len(halves)

    def body(*refs):
        ins, outs = refs[:n], refs[n:2 * n]
        send_sems, recv_sems = refs[2 * n:]
        x, y, c, chips = _place()
        sends = []
        for k in range(n):
            for j, (tx, ty) in enumerate(chips):
                sends.append(pltpu.make_async_remote_copy(
                    src_ref=ins[k].at[2 * tx + ty], dst_ref=outs[k].at[j], send_sem=send_sems.at[k, j],
                    recv_sem=recv_sems.at[k, j], device_id=(tx, ty, c), device_id_type=MESH))
        for cp in sends:
            cp.start()
        for cp in sends:
            cp.wait()

    return pl.pallas_call(
        body, name="grads_chip_scatter",
        in_specs=[ANY] * n, out_specs=[ANY] * n,
        out_shape=[jax.ShapeDtypeStruct((3,) + hv.shape[1:], hv.dtype) for hv in halves],
        scratch_shapes=[pltpu.SemaphoreType.DMA((n, 3)), pltpu.SemaphoreType.DMA((n, 3))],
    )(*halves)


def _pair_share(finals):
    n = len(finals)

    def body(*refs):
        ins, outs = refs[:n], refs[n:2 * n]
        send_sems, recv_sems = refs[2 * n:]
        x, y, c, _ = _place()
        copies = [pltpu.make_async_remote_copy(
            src_ref=ins[k], dst_ref=outs[k], send_sem=send_sems.at[k], recv_sem=recv_sems.at[k],
            device_id=(x, y, 1 - c), device_id_type=MESH) for k in range(n)]
        for cp in copies:
            cp.start()
        for cp in copies:
            cp.wait()

    return pl.pallas_call(
        body, name="grads_pair_share",
        in_specs=[ANY] * n, out_specs=[ANY] * n,
        out_shape=[jax.ShapeDtypeStruct(fv.shape, F32) for fv in finals],
        scratch_shapes=[pltpu.SemaphoreType.DMA((n,)), pltpu.SemaphoreType.DMA((n,))],
    )(*finals)


def _small_all_reduce(buf):
    rows, c_ = buf.shape

    def body(in_ref, out_ref, pair_buf, slots, send_sems, recv_sems):
        x, y, c, chips = _place()
        mine = 2 * x + y
        pair = pltpu.make_async_remote_copy(
            src_ref=in_ref, dst_ref=pair_buf, send_sem=send_sems.at[0], recv_sem=recv_sems.at[0],
            device_id=(x, y, 1 - c), device_id_type=MESH)
        pair.start()
        pair.wait()
        slots[mine] = in_ref[...] + pair_buf[...]
        sends = [pltpu.make_async_remote_copy(
            src_ref=slots.at[mine], dst_ref=slots.at[mine], send_sem=send_sems.at[1 + j], recv_sem=recv_sems.at[1 + j],
            device_id=(tx, ty, c), device_id_type=MESH) for j, (tx, ty) in enumerate(chips)]
        for cp in sends:
            cp.start()
        for j, (tx, ty) in enumerate(chips):
            pltpu.make_async_remote_copy(
                src_ref=slots.at[mine], dst_ref=slots.at[2 * tx + ty], send_sem=send_sems.at[1 + j],
                recv_sem=recv_sems.at[1 + j], device_id=(tx, ty, c), device_id_type=MESH).wait()
        out_ref[...] = ((slots[0] + slots[1]) + slots[2]) + slots[3]

    vm = pl.BlockSpec(memory_space=pltpu.VMEM)
    return pl.pallas_call(
        body, name="small_all_reduce", in_specs=[vm], out_specs=vm,
        out_shape=jax.ShapeDtypeStruct((rows, c_), F32),
        scratch_shapes=[pltpu.VMEM((rows, c_), F32), pltpu.VMEM((4, rows, c_), F32),
                        pltpu.SemaphoreType.DMA((4,)), pltpu.SemaphoreType.DMA((4,))],
        compiler_params=pltpu.CompilerParams(vmem_limit_bytes=VMEM_LIMIT_V7X),
    )(buf)


def _reduce_scatter(grads):
    core = lax.axis_index("c").astype(jnp.int32).reshape(1)
    chip = (2 * lax.axis_index("x") + lax.axis_index("y")).astype(jnp.int32).reshape(1)
    recv = _pair_exchange(grads)
    halves = [_pair_sum(g, r, core, f"pair_sum_{k}") for k, (g, r) in enumerate(zip(grads, recv))]
    recv = _chip_scatter([hb for _, hb in halves])
    finals = [_chip_sum(hv, r, chip, f"chip_sum_{k}") for k, ((hv, _), r) in enumerate(zip(halves, recv))]
    return list(zip(finals, _pair_share(finals))), core


PACK_COLS = 1024


def _pack(arrays):
    flat = jnp.concatenate([a.reshape(-1).astype(F32) for a in arrays])
    rows = -(-flat.shape[0] // PACK_COLS)
    rows = -(-rows // 8) * 8
    return jnp.pad(flat, (0, rows * PACK_COLS - flat.shape[0])).reshape(rows, PACK_COLS)


def _unpack(buf, shapes):
    flat = buf.reshape(-1)
    out, at = [], 0
    for shp in shapes:
        size = math.prod(shp)
        out.append(flat[at:at + size].reshape(shp))
        at += size
    return out


def kernel(x, mix_norm_g, ffn_norm_g, gm_w_in, gm_ln_g, gm_ln_b, gm_w_s, gm_b_s, gm_w_out, fox_w_qkvf, fox_b_f, fox_w_o, ffn_w_gate, ffn_w_up, ffn_conv_w, ffn_conv_b, ffn_w_down, final_norm_g, loss_target, m_mix_norm_g, m_ffn_norm_g, m_gm_w_in, m_gm_ln_g, m_gm_ln_b, m_gm_w_s, m_gm_b_s, m_gm_w_out, m_fox_w_qkvf, m_fox_b_f, m_fox_w_o, m_ffn_w_gate, m_ffn_w_up, m_ffn_conv_w, m_ffn_conv_b, m_ffn_w_down, m_final_norm_g, v_mix_norm_g, v_ffn_norm_g, v_gm_w_in, v_gm_ln_g, v_gm_ln_b, v_gm_w_s, v_gm_b_s, v_gm_w_out, v_fox_w_qkvf, v_fox_b_f, v_fox_w_o, v_ffn_w_gate, v_ffn_w_up, v_ffn_conv_w, v_ffn_conv_b, v_ffn_w_down, v_final_norm_g):
    _, s_len, d = x.shape
    e = gm_ln_g.shape[1]
    f = ffn_conv_b.shape[1]
    n_head = fox_b_f.shape[1]
    n_pair = n_head // 2
    gd = e // GM_GROUPS
    qkvf_cols = fox_w_qkvf.shape[2]
    assert d == n_head * HEAD_DIM and d % (2 * LANES) == 0 and s_len % 512 == 0 and gd % LANES == 0
    assert gm_w_s.shape[2] == CHUNK and 4 * qkvf_cols == 3 * d + n_head
    tm = 256
    h0 = x[0]
    target = loss_target[0]

    gathered = _all_gather_chips([
        gm_w_in[0].astype(BF16), gm_w_out[0].astype(BF16), fox_w_qkvf[0].astype(BF16), fox_w_o[0].astype(BF16),
        ffn_w_gate.astype(BF16), ffn_w_up.astype(BF16), ffn_w_down.astype(BF16), ffn_conv_w])
    w_in, w_out4, qkvf4, wo4, wg_all, wu_all, wd_all, cw4 = gathered
    w_out = w_out4.reshape(e, d)
    qkvf = jnp.transpose(qkvf4, (1, 0, 2)).reshape(d, 4 * qkvf_cols)
    wq, wk, wv = qkvf[:, :d], qkvf[:, d:2 * d], qkvf[:, 2 * d:3 * d]
    wf = jnp.pad(qkvf[:, 3 * d:], ((0, 0), (0, LANES - n_head)))
    wo = wo4.reshape(d, d)
    conv_w_full = jnp.transpose(cw4, (1, 2, 0, 3)).reshape(2, 3, f)
    conv_w8 = jnp.pad(conv_w_full, ((0, 0), (0, 5), (0, 0)))
    bf_pad = jnp.pad(fox_b_f, ((0, 0), (0, LANES - n_head)))

    tril = jnp.tril(jnp.ones((CHUNK, CHUNK), bool))
    wc = jnp.where(tril[None], gm_w_s[0], 0.0).astype(BF16)
    wct = jnp.transpose(wc, (0, 2, 1))
    bias = jnp.repeat(gm_b_s[0].T, gd, axis=1)
    seg_groups = (jnp.arange(e)[:, None] // gd == jnp.arange(LANES)[None, :]).astype(BF16)
    seg_heads = (jnp.arange(d)[:, None] // HEAD_DIM == jnp.arange(LANES)[None, :]).astype(BF16)
    sel_q = _spare_selectors(d, key_side=False)
    sel_k = _spare_selectors(d, key_side=True)

    h1, a0, hn0, gated0 = _gmlp_fwd(h0, mix_norm_g[0:1], w_in, gm_ln_g, gm_ln_b, wc, bias, w_out, tm)
    h2, fa0, fup0, fhn0, fhid0 = _ffn_fwd(h1, ffn_norm_g[0:1], wg_all, wu_all, wd_all, 0, conv_w8[0], ffn_conv_b[0:1], tm)
    (hn1, qa, qb, kat, kbt, va, vb, vat, vbt, z_f) = _fox_proj_fwd(
        h2, mix_norm_g[1:2], wq, wk, wv, wf, bf_pad, sel_q, sel_k, tm)
    o, qa2, qb2, qat2, qbt2 = _flash_fwd(qa, qb, kat, kbt, va, vb)
    h3 = _oproj_fwd(h2, o, wo, tm)
    h4, fa1, fup1, fhn1, fhid1 = _ffn_fwd(h3, ffn_norm_g[1:2], wg_all, wu_all, wd_all, 1, conv_w8[1], ffn_conv_b[1:2], tm)

    dh4, loss_part, g_final = _loss_head(h4, target, final_norm_g.reshape(1, d), tm)
    dh3, da1, dup1, gcw1, gcb1, gfn1 = _ffn_bwd(h3, dh4, fa1, fup1, ffn_norm_g[1:2], wg_all, wu_all, wd_all, 1,
                                                conv_w8[1], ffn_conv_b[1:2], tm)
    g_gate1 = _wgrad(fhn1, da1, 4, "wgrad_gate_1")
    g_up1 = _wgrad(fhn1, dup1, 4, "wgrad_up_1")
    g_down1 = _wgrad(fhid1, dh4, 1, "wgrad_down_1").reshape(4, f // 4, d)

    doa, dob, doat, dobt = _oproj_bwd(dh3, o, wo, seg_heads, sel_q, tm)
    g_wo = _wgrad(o, dh3, 1, "wgrad_wo").reshape(4, d // 4, d)
    dqt, dkt, dvt, row_sums, col_sums = _flash_bwd(qa2, qb2, qat2, qbt2, kat, kbt, vat, vbt, doa, dob, doat, dobt)
    sums = row_sums[:, 0::8, :] - col_sums[:, N_SPARE::8, :]
    dcum = jnp.pad(sums.reshape(n_head, s_len).T, ((0, 0), (0, LANES - n_head)))
    dfl, g_bf = _forget_bwd(dcum, z_f, tm)
    dh2, dflb, gmn1 = _fox_proj_bwd(h2, dh3, dqt, dkt, dvt, dfl, mix_norm_g[1:2], wq, wk, wv, wf, tm)
    g_q = _wgrad_t(dqt, hn1, "wgrad_q").T
    g_k = _wgrad_t(dkt, hn1, "wgrad_k").T
    g_v = _wgrad_t(dvt, hn1, "wgrad_v").T
    g_f = _wgrad(hn1, dflb, 1, "wgrad_f")[0][:, :n_head]
    g_qkvf = jnp.concatenate([g_q, g_k, g_v, g_f], axis=1).reshape(d, 4, qkvf_cols).transpose(1, 0, 2)

    dh1, da0f, dup0, gcw0, gcb0, gfn0 = _ffn_bwd(h1, dh2, fa0, fup0, ffn_norm_g[0:1], wg_all, wu_all, wd_all, 0,
                                                 conv_w8[0], ffn_conv_b[0:1], tm)
    g_gate0 = _wgrad(fhn0, da0f, 4, "wgrad_gate_0")
    g_up0 = _wgrad(fhn0, dup0, 4, "wgrad_up_0")
    g_down0 = _wgrad(fhid0, dh2, 1, "wgrad_down_0").reshape(4, f // 4, d)

    dh0, da0, g_ws, g_bs_t, g_lng, g_lnb, gmn0 = _gmlp_bwd(
        h0, dh1, a0, mix_norm_g[0:1], w_in, gm_ln_g, gm_ln_b, wc, wct, bias, w_out, seg_groups, tm)
    g_win = _wgrad(hn0, da0, 4, "wgrad_gm_in")
    g_wout = _wgrad(gated0, dh1, 1, "wgrad_gm_out").reshape(4, e // 4, d)

    big, core = _reduce_scatter([g_win, g_wout, g_qkvf, g_wo, g_gate0, g_gate1, g_up0, g_up1, g_down0, g_down1])
    r_win, r_wout, r_qkvf, r_wo, r_gate0, r_gate1, r_up0, r_up1, r_down0, r_down1 = big

    small = [jnp.concatenate([gmn0, gmn1]), jnp.concatenate([gfn0, gfn1]), g_lng, g_lnb, g_ws[None],
             g_bs_t[:, :GM_GROUPS].T[None], g_bf[:, :n_head], jnp.stack([gcw0[:3], gcw1[:3]]),
             jnp.concatenate([gcb0, gcb1]), g_final.reshape(d), loss_part[0, :1]]
    small_shapes = [a.shape for a in small]
    reduced = _unpack(_small_all_reduce(_pack(small)), small_shapes)
    (r_mix, r_ffn, r_lng, r_lnb, r_ws, r_bs, r_bf, r_cw_full, r_cb, r_final, r_loss) = reduced
    chip = 2 * lax.axis_index("x") + lax.axis_index("y")
    r_cw = lax.dynamic_slice_in_dim(r_cw_full, chip * (f // 4), f // 4, axis=2)

    def update_big(name, w, m, v, per_layer):
        parts = [_adamw_halves(w[l], m[l], v[l], own, other, core, f"adamw_{name}_{l}")
                 for l, (own, other) in enumerate(per_layer)]
        return tuple(jnp.stack([p[i] for p in parts]) for i in range(4))

    res = {}
    res["gm_w_in"] = update_big("gm_w_in", gm_w_in, m_gm_w_in, v_gm_w_in, [r_win])
    res["gm_w_out"] = update_big("gm_w_out", gm_w_out, m_gm_w_out, v_gm_w_out, [r_wout])
    res["fox_w_qkvf"] = update_big("fox_w_qkvf", fox_w_qkvf, m_fox_w_qkvf, v_fox_w_qkvf, [r_qkvf])
    res["fox_w_o"] = update_big("fox_w_o", fox_w_o, m_fox_w_o, v_fox_w_o, [r_wo])
    res["ffn_w_gate"] = update_big("ffn_w_gate", ffn_w_gate, m_ffn_w_gate, v_ffn_w_gate, [r_gate0, r_gate1])
    res["ffn_w_up"] = update_big("ffn_w_up", ffn_w_up, m_ffn_w_up, v_ffn_w_up, [r_up0, r_up1])
    res["ffn_w_down"] = update_big("ffn_w_down", ffn_w_down, m_ffn_w_down, v_ffn_w_down, [r_down0, r_down1])

    small_names = ["mix_norm_g", "ffn_norm_g", "gm_ln_g", "gm_ln_b", "gm_w_s", "gm_b_s", "fox_b_f", "ffn_conv_w",
                   "ffn_conv_b", "final_norm_g"]
    small_w = [mix_norm_g, ffn_norm_g, gm_ln_g, gm_ln_b, gm_w_s, gm_b_s, fox_b_f, ffn_conv_w, ffn_conv_b, final_norm_g]
    small_m = [m_mix_norm_g, m_ffn_norm_g, m_gm_ln_g, m_gm_ln_b, m_gm_w_s, m_gm_b_s, m_fox_b_f, m_ffn_conv_w,
               m_ffn_conv_b, m_final_norm_g]
    small_v = [v_mix_norm_g, v_ffn_norm_g, v_gm_ln_g, v_gm_ln_b, v_gm_w_s, v_gm_b_s, v_fox_b_f, v_ffn_conv_w,
               v_ffn_conv_b, v_final_norm_g]
    small_g = [r_mix, r_ffn, r_lng, r_lnb, r_ws, r_bs, r_bf, r_cw, r_cb, r_final]
    shapes = [w.shape for w in small_w]
    small_g = [g.reshape(s) for g, s in zip(small_g, shapes)]
    dlt, mn, vn = _adamw(_pack(small_w), _pack(small_m), _pack(small_v), _pack(small_g), "adamw_small")
    for name, g, dl_, m_, v_ in zip(small_names, small_g, _unpack(dlt, shapes), _unpack(mn, shapes), _unpack(vn, shapes)):
        res[name] = (g, dl_, m_, v_)

    order = ["mix_norm_g", "ffn_norm_g", "gm_w_in", "gm_ln_g", "gm_ln_b", "gm_w_s", "gm_b_s", "gm_w_out", "fox_w_qkvf",
             "fox_b_f", "fox_w_o", "ffn_w_gate", "ffn_w_up", "ffn_conv_w", "ffn_conv_b", "ffn_w_down", "final_norm_g"]
    outs = [r_loss.reshape(()), dh0[None]]
    for part in range(4):
        outs += [res[name][part] for name in order]
    return tuple(outs)
```

```python
import functools
import math

import jax
import jax.numpy as jnp
from jax import lax
from jax.experimental import pallas as pl
from jax.experimental.pallas import tpu as pltpu

F32 = jnp.float32
BF16 = jnp.bfloat16

RMS_EPS = 1e-6
LN_EPS = 1e-5
CHUNK = 128
GM_GROUPS = 8
HEAD_DIM = 64
LANES = 128
ATT_BLOCK = 256
ATT_CHUNK = 1024
VMEM_LIMIT_V7X = 56 * 1024 * 1024

ADAM_LR = 0.001
ADAM_B1 = 0.9
ADAM_B2 = 0.999
ADAM_EPS = 1e-08
ADAM_WD = 0.01
ADAM_STEP = 10

MESH = pl.DeviceIdType.MESH
ANY = pl.BlockSpec(memory_space=pl.ANY)
NEG_BIG = -1e30


def _params(n_grid):
    return pltpu.CompilerParams(dimension_semantics=("arbitrary",) * n_grid, vmem_limit_bytes=VMEM_LIMIT_V7X)


def _dot(a, b):
    return jnp.dot(a, b, preferred_element_type=F32)


def _dot_nt(a, b):
    return lax.dot_general(a, b, (((1,), (1,)), ((), ())), preferred_element_type=F32)


def _dot_tn(a, b):
    return lax.dot_general(a, b, (((0,), (0,)), ((), ())), preferred_element_type=F32)


def _split3(x):
    hi = x.astype(BF16)
    r = x - hi.astype(F32)
    mid = r.astype(BF16)
    lo = (r - mid.astype(F32)).astype(BF16)
    return hi, mid, lo


def _dot3_lhs(x, m):
    hi, mid, lo = _split3(x)
    return _dot(hi, m) + _dot(mid, m) + _dot(lo, m)


def _dot3_rhs(m, x):
    hi, mid, lo = _split3(x)
    return _dot(m, hi) + _dot(m, mid) + _dot(m, lo)


def _load_once(pairs, sem):
    @pl.when(pl.program_id(0) == 0)
    def _():
        copies = [pltpu.make_async_copy(src, dst, sem.at[k]) for k, (src, dst) in enumerate(pairs)]
        for cp in copies:
            cp.start()
        for cp in copies:
            cp.wait()


def _rms_fwd(x, g):
    r = lax.rsqrt(jnp.mean(x * x, axis=-1, keepdims=True) + RMS_EPS)
    xhat = x * r
    return xhat, r, xhat * g


def _rms_bwd(dy, xhat, r, g):
    w = dy * g
    dx = r * (w - xhat * jnp.mean(w * xhat, axis=-1, keepdims=True))
    return dx, dy * xhat


def _gelu_parts(a):
    c = math.sqrt(2.0 / math.pi)
    a2 = a * a
    t = jnp.tanh(c * (a + 0.044715 * a * a2))
    z = 0.5 * a * (1.0 + t)
    dz = 0.5 * (1.0 + t) + 0.5 * a * (1.0 - t * t) * (c * (1.0 + 3.0 * 0.044715 * a2))
    return z, dz


def _sigmoid(x):
    return 1.0 / (1.0 + jnp.exp(-x))


def _gmlp_core(a, lng, lnb, wc_ref, bias, n_chunk, gd):
    e = a.shape[1] // 2
    z, dz = _gelu_parts(a)
    u = z[:, :e]
    v = z[:, e:]
    mu = jnp.mean(v, axis=-1, keepdims=True)
    vc = v - mu
    rstd = lax.rsqrt(jnp.mean(vc * vc, axis=-1, keepdims=True) + LN_EPS)
    vhat = vc * rstd
    vln = vhat * lng + lnb
    vlb = vln.astype(BF16)
    rows = []
    for ci in range(n_chunk):
        cols = []
        for g in range(GM_GROUPS):
            blk = vlb[ci * CHUNK:(ci + 1) * CHUNK, g * gd:(g + 1) * gd]
            cols.append(_dot(wc_ref[g], blk))
        rows.append(jnp.concatenate(cols, axis=1) + bias)
    s = rows[0] if n_chunk == 1 else jnp.concatenate(rows, axis=0)
    return dz, u, vhat, rstd, vlb, s


def _gmlp_fwd(h, g_mix, w_in, lng, lnb, wc, bias, w_out, tm):
    s_len, d = h.shape
    n_p, _, w = w_in.shape
    e = w_out.shape[0]
    gd = e // GM_GROUPS
    n_chunk = tm // CHUNK

    def body(h_ref, g_ref, win_hbm, lng_ref, lnb_ref, wc_ref, bias_ref, wout_hbm,
             hout_ref, a_ref, hn_ref, gated_ref, win_v, wout_v, sem):
        _load_once([(win_hbm, win_v), (wout_hbm, wout_v)], sem)
        x = h_ref[...]
        _, _, y = _rms_fwd(x, g_ref[...])
        hn = y.astype(BF16)
        hn_ref[...] = hn
        for p in range(n_p):
            a_ref[:, p * w:(p + 1) * w] = _dot(hn, win_v[p])
        _, u, _, _, _, s = _gmlp_core(a_ref[...], lng_ref[...], lnb_ref[...], wc_ref, bias_ref[...], n_chunk, gd)
        gated = (u * s).astype(BF16)
        gated_ref[...] = gated
        hout_ref[...] = x + _dot(gated, wout_v[...])

    row = lambda i: (i, 0)
    const2 = lambda i: (0, 0)
    return pl.pallas_call(
        body, name="gmlp_fwd", grid=(s_len // tm,),
        in_specs=[pl.BlockSpec((tm, d), row), pl.BlockSpec((1, d), const2), ANY,
                  pl.BlockSpec((1, e), const2), pl.BlockSpec((1, e), const2),
                  pl.BlockSpec(wc.shape, lambda i: (0, 0, 0)), pl.BlockSpec((CHUNK, e), const2), ANY],
        out_specs=[pl.BlockSpec((tm, d), row), pl.BlockSpec((tm, 2 * e), row),
                   pl.BlockSpec((tm, d), row), pl.BlockSpec((tm, e), row)],
        out_shape=[jax.ShapeDtypeStruct((s_len, d), F32), jax.ShapeDtypeStruct((s_len, 2 * e), F32),
                   jax.ShapeDtypeStruct((s_len, d), BF16), jax.ShapeDtypeStruct((s_len, e), BF16)],
        scratch_shapes=[pltpu.VMEM(w_in.shape, BF16), pltpu.VMEM(w_out.shape, BF16), pltpu.SemaphoreType.DMA((2,))],
        compiler_params=_params(1),
    )(h, g_mix, w_in, lng, lnb, wc, bias, w_out)


def _gmlp_bwd(h, dh, a, g_mix, w_in, lng, lnb, wc, wct, bias, w_out, seg, tm):
    s_len, d = h.shape
    n_p, _, w = w_in.shape
    e = w_out.shape[0]
    gd = e // GM_GROUPS
    n_chunk = tm // CHUNK
    n_blk = s_len // tm

    def body(h_ref, dh_ref, a_ref, g_ref, win_hbm, lng_ref, lnb_ref, wc_ref, wct_ref, bias_ref, wout_hbm, seg_ref,
             dhin_ref, da_ref, gws_ref, gbs_ref, glng_ref, glnb_ref, gmix_ref, win_v, wout_v, dsum, sem):
        i = pl.program_id(0)
        _load_once([(win_hbm, win_v), (wout_hbm, wout_v)], sem)

        @pl.when(i == 0)
        def _():
            gws_ref[...] = jnp.zeros_like(gws_ref)
            glng_ref[...] = jnp.zeros_like(glng_ref)
            glnb_ref[...] = jnp.zeros_like(glnb_ref)
            gmix_ref[...] = jnp.zeros_like(gmix_ref)
            dsum[...] = jnp.zeros_like(dsum)

        x = h_ref[...]
        dh_v = dh_ref[...]
        g = g_ref[...]
        lng_v = lng_ref[...]
        xhat, r, _ = _rms_fwd(x, g)
        dz_da, u, vhat, rstd, vlb, s = _gmlp_core(a_ref[...], lng_v, lnb_ref[...], wc_ref, bias_ref[...], n_chunk, gd)
        dg = _dot_nt(dh_v.astype(BF16), wout_v[...])
        du = dg * s
        ds = dg * u
        dsb = ds.astype(BF16)
        rows = []
        ds_acc = None
        for ci in range(n_chunk):
            lo, hi = ci * CHUNK, (ci + 1) * CHUNK
            cols = []
            for gi in range(GM_GROUPS):
                d_blk = dsb[lo:hi, gi * gd:(gi + 1) * gd]
                gws_ref[gi] += _dot_nt(d_blk, vlb[lo:hi, gi * gd:(gi + 1) * gd])
                cols.append(_dot(wct_ref[gi], d_blk))
            rows.append(jnp.concatenate(cols, axis=1))
            ds_acc = ds[lo:hi] if ds_acc is None else ds_acc + ds[lo:hi]
        dsum[...] += ds_acc
        dvln = rows[0] if n_chunk == 1 else jnp.concatenate(rows, axis=0)
        glng_ref[...] += jnp.sum(dvln * vhat, axis=0, keepdims=True)
        glnb_ref[...] += jnp.sum(dvln, axis=0, keepdims=True)
        dvhat = dvln * lng_v
        dv = rstd * (dvhat - jnp.mean(dvhat, axis=-1, keepdims=True)
                     - vhat * jnp.mean(dvhat * vhat, axis=-1, keepdims=True))
        da = jnp.concatenate([du, dv], axis=1) * dz_da
        dab = da.astype(BF16)
        da_ref[...] = dab
        dhn = _dot_nt(dab[:, :w], win_v[0])
        for p in range(1, n_p):
            dhn += _dot_nt(dab[:, p * w:(p + 1) * w], win_v[p])
        dx, gg = _rms_bwd(dhn, xhat, r, g)
        gmix_ref[...] += jnp.sum(gg, axis=0, keepdims=True)
        dhin_ref[...] = dh_v + dx

        @pl.when(i == n_blk - 1)
        def _():
            tril = lax.broadcasted_iota(jnp.int32, (CHUNK, CHUNK), 0) >= lax.broadcasted_iota(jnp.int32, (CHUNK, CHUNK), 1)
            for gi in range(GM_GROUPS):
                gws_ref[gi] = jnp.where(tril, gws_ref[gi], 0.0)
            gbs_ref[...] = _dot3_lhs(dsum[...], seg_ref[...])

    row = lambda i: (i, 0)
    const2 = lambda i: (0, 0)
    const3 = lambda i: (0, 0, 0)
    return pl.pallas_call(
        body, name="gmlp_bwd", grid=(n_blk,),
        in_specs=[pl.BlockSpec((tm, d), row), pl.BlockSpec((tm, d), row), pl.BlockSpec((tm, 2 * e), row),
                  pl.BlockSpec((1, d), const2), ANY, pl.BlockSpec((1, e), const2), pl.BlockSpec((1, e), const2),
                  pl.BlockSpec(wc.shape, const3), pl.BlockSpec(wct.shape, const3), pl.BlockSpec((CHUNK, e), const2),
                  ANY, pl.BlockSpec((e, LANES), const2)],
        out_specs=[pl.BlockSpec((tm, d), row), pl.BlockSpec((tm, 2 * e), row), pl.BlockSpec(wc.shape, const3),
                   pl.BlockSpec((CHUNK, LANES), const2), pl.BlockSpec((1, e), const2), pl.BlockSpec((1, e), const2),
                   pl.BlockSpec((1, d), const2)],
        out_shape=[jax.ShapeDtypeStruct((s_len, d), F32), jax.ShapeDtypeStruct((s_len, 2 * e), BF16),
                   jax.ShapeDtypeStruct(wc.shape, F32), jax.ShapeDtypeStruct((CHUNK, LANES), F32),
                   jax.ShapeDtypeStruct((1, e), F32), jax.ShapeDtypeStruct((1, e), F32), jax.ShapeDtypeStruct((1, d), F32)],
        scratch_shapes=[pltpu.VMEM(w_in.shape, BF16), pltpu.VMEM(w_out.shape, BF16), pltpu.VMEM((CHUNK, e), F32),
                        pltpu.SemaphoreType.DMA((2,))],
        compiler_params=_params(1),
    )(h, dh, a, g_mix, w_in, lng, lnb, wc, wct, bias, w_out, seg)


def _shift_down(a, k, fill):
    tm = a.shape[0]
    out = pltpu.roll(a, k, 0)
    rid = lax.broadcasted_iota(jnp.int32, a.shape, 0)
    for j in range(k):
        out = jnp.where(rid == j, fill[8 - k + j:8 - k + j + 1, :], out)
    return out


def _shift_up(a, k, fill):
    tm = a.shape[0]
    out = pltpu.roll(a, tm - k, 0)
    rid = lax.broadcasted_iota(jnp.int32, a.shape, 0)
    for j in range(k):
        out = jnp.where(rid == tm - k + j, fill[j:j + 1, :], out)
    return out


def _ffn_fwd(h, g_norm, wg_all, wu_all, wd_all, layer, conv_w, conv_b, tm):
    s_len, d = h.shape
    n_p = wg_all.shape[0]
    fq = wg_all.shape[3]
    f = n_p * fq

    def body(h_ref, g_ref, wg_hbm, wu_hbm, wd_hbm, cw_ref, cb_ref,
             hout_ref, a_ref, up_ref, hn_ref, hid_ref, wg_v, wu_v, wd_v, carry, sem):
        i = pl.program_id(0)
        _load_once([(wg_hbm.at[:, layer], wg_v), (wu_hbm.at[:, layer], wu_v), (wd_hbm.at[:, layer], wd_v)], sem)

        @pl.when(i == 0)
        def _():
            carry[...] = jnp.zeros_like(carry)

        x = h_ref[...]
        _, _, y = _rms_fwd(x, g_ref[...])
        hn = y.astype(BF16)
        hn_ref[...] = hn
        for p in range(n_p):
            a_ref[:, p * fq:(p + 1) * fq] = _dot(hn, wg_v[p])
            up_ref[:, p * fq:(p + 1) * fq] = _dot(hn, wu_v[p])
        a = a_ref[...]
        prev = carry[...]
        am1 = _shift_down(a, 1, prev)
        am2 = _shift_down(a, 2, prev)
        carry[...] = a[tm - 8:tm, :]
        cw = cw_ref[...]
        ac = cb_ref[...] + am2 * cw[0:1, :]
        ac = ac + am1 * cw[1:2, :]
        ac = ac + a * cw[2:3, :]
        hid = (ac * _sigmoid(ac) * up_ref[...]).astype(BF16)
        hid_ref[...] = hid
        y2 = _dot(hid[:, :fq], wd_v[0])
        for p in range(1, n_p):
            y2 += _dot(hid[:, p * fq:(p + 1) * fq], wd_v[p])
        hout_ref[...] = x + y2

    row = lambda i: (i, 0)
    const2 = lambda i: (0, 0)
    return pl.pallas_call(
        body, name=f"ffn_fwd_{layer}", grid=(s_len // tm,),
        in_specs=[pl.BlockSpec((tm, d), row), pl.BlockSpec((1, d), const2), ANY, ANY, ANY,
                  pl.BlockSpec((8, f), const2), pl.BlockSpec((1, f), const2)],
        out_specs=[pl.BlockSpec((tm, d), row), pl.BlockSpec((tm, f), row), pl.BlockSpec((tm, f), row),
                   pl.BlockSpec((tm, d), row), pl.BlockSpec((tm, f), row)],
        out_shape=[jax.ShapeDtypeStruct((s_len, d), F32), jax.ShapeDtypeStruct((s_len, f), F32),
                   jax.ShapeDtypeStruct((s_len, f), F32), jax.ShapeDtypeStruct((s_len, d), BF16),
                   jax.ShapeDtypeStruct((s_len, f), BF16)],
        scratch_shapes=[pltpu.VMEM((n_p, d, fq), BF16), pltpu.VMEM((n_p, d, fq), BF16), pltpu.VMEM((n_p, fq, d), BF16),
                        pltpu.VMEM((8, f), F32), pltpu.SemaphoreType.DMA((3,))],
        compiler_params=_params(1),
    )(h, g_norm, wg_all, wu_all, wd_all, conv_w, conv_b)


def _ffn_bwd(h, dh, a, up, g_norm, wg_all, wu_all, wd_all, layer, conv_w, conv_b, tm):
    s_len, d = h.shape
    n_p = wg_all.shape[0]
    fq = wg_all.shape[3]
    f = n_p * fq
    n_blk = s_len // tm
    t8 = tm // 8

    def body(h_ref, dh_ref, a_ref, ahalo_ref, up_ref, g_ref, wg_hbm, wu_hbm, wd_hbm, cw_ref, cb_ref,
             dhin_ref, da_ref, dup_ref, gcw_ref, gcb_ref, gn_ref, wg_v, wu_v, wd_v, carry, sem):
        i = pl.program_id(0)
        _load_once([(wg_hbm.at[:, layer], wg_v), (wu_hbm.at[:, layer], wu_v), (wd_hbm.at[:, layer], wd_v)], sem)

        @pl.when(i == 0)
        def _():
            carry[...] = jnp.zeros_like(carry)
            gcw_ref[...] = jnp.zeros_like(gcw_ref)
            gcb_ref[...] = jnp.zeros_like(gcb_ref)
            gn_ref[...] = jnp.zeros_like(gn_ref)

        x = h_ref[...]
        dh_v = dh_ref[...]
        g = g_ref[...]
        xhat, r, _ = _rms_fwd(x, g)
        a = a_ref[...]
        up_v = up_ref[...]
        prev = jnp.where(i == n_blk - 1, 0.0, ahalo_ref[...])
        am1 = _shift_down(a, 1, prev)
        am2 = _shift_down(a, 2, prev)
        cw = cw_ref[...]
        ac = cb_ref[...] + am2 * cw[0:1, :]
        ac = ac + am1 * cw[1:2, :]
        ac = ac + a * cw[2:3, :]
        sg = _sigmoid(ac)
        sil = ac * sg
        dhb = dh_v.astype(BF16)
        dhid = jnp.concatenate([_dot_nt(dhb, wd_v[p]) for p in range(n_p)], axis=1)
        dup = dhid * sil
        dac = dhid * up_v * (sg * (1.0 + ac * (1.0 - sg)))
        gcb_ref[...] += jnp.sum(dac, axis=0, keepdims=True)
        gcw_ref[0:1, :] += jnp.sum(dac * am2, axis=0, keepdims=True)
        gcw_ref[1:2, :] += jnp.sum(dac * am1, axis=0, keepdims=True)
        gcw_ref[2:3, :] += jnp.sum(dac * a, axis=0, keepdims=True)
        nxt = carry[...]
        dp1 = _shift_up(dac, 1, nxt)
        dp2 = _shift_up(dac, 2, nxt)
        carry[...] = dac[0:8, :]
        da = dac * cw[2:3, :] + dp1 * cw[1:2, :] + dp2 * cw[0:1, :]
        dab = da.astype(BF16)
        dupb = dup.astype(BF16)
        da_ref[...] = dab
        dup_ref[...] = dupb
        dhn = _dot_nt(dab[:, :fq], wg_v[0]) + _dot_nt(dupb[:, :fq], wu_v[0])
        for p in range(1, n_p):
            dhn += _dot_nt(dab[:, p * fq:(p + 1) * fq], wg_v[p]) + _dot_nt(dupb[:, p * fq:(p + 1) * fq], wu_v[p])
        dx, gg = _rms_bwd(dhn, xhat, r, g)
        gn_ref[...] += jnp.sum(gg, axis=0, keepdims=True)
        dhin_ref[...] = dh_v + dx

    rev = lambda i: (n_blk - 1 - i, 0)
    halo = lambda i: (jnp.maximum((n_blk - 1 - i) * t8 - 1, 0), 0)
    const2 = lambda i: (0, 0)
    return pl.pallas_call(
        body, name=f"ffn_bwd_{layer}", grid=(n_blk,),
        in_specs=[pl.BlockSpec((tm, d), rev), pl.BlockSpec((tm, d), rev), pl.BlockSpec((tm, f), rev),
                  pl.BlockSpec((8, f), halo), pl.BlockSpec((tm, f), rev), pl.BlockSpec((1, d), const2), ANY, ANY, ANY,
                  pl.BlockSpec((8, f), const2), pl.BlockSpec((1, f), const2)],
        out_specs=[pl.BlockSpec((tm, d), rev), pl.BlockSpec((tm, f), rev), pl.BlockSpec((tm, f), rev),
                   pl.BlockSpec((8, f), const2), pl.BlockSpec((1, f), const2), pl.BlockSpec((1, d), const2)],
        out_shape=[jax.ShapeDtypeStruct((s_len, d), F32), jax.ShapeDtypeStruct((s_len, f), BF16),
                   jax.ShapeDtypeStruct((s_len, f), BF16), jax.ShapeDtypeStruct((8, f), F32),
                   jax.ShapeDtypeStruct((1, f), F32), jax.ShapeDtypeStruct((1, d), F32)],
        scratch_shapes=[pltpu.VMEM((n_p, d, fq), BF16), pltpu.VMEM((n_p, d, fq), BF16), pltpu.VMEM((n_p, fq, d), BF16),
                        pltpu.VMEM((8, f), F32), pltpu.SemaphoreType.DMA((3,))],
        compiler_params=_params(1),
    )(h, dh, a, a, up, g_norm, wg_all, wu_all, wd_all, conv_w, conv_b)


def _even_head_lanes(shape, axis):
    return (lax.broadcasted_iota(jnp.int32, shape, axis) & HEAD_DIM) == 0


def _pair_select(lo, hi, shape):
    return jnp.where(lax.broadcasted_iota(jnp.int32, shape, 1) < HEAD_DIM, lo, hi)


def _causal(row0, col0, shape):
    return row0 + lax.broadcasted_iota(jnp.int32, shape, 0) >= col0 + lax.broadcasted_iota(jnp.int32, shape, 1)


N_SPARE = 3


def _spare_selectors(d, key_side):
    lane = jnp.arange(d)[None, :]
    row = jnp.arange(N_SPARE * LANES)[:, None]
    head, part = row % LANES, row // LANES
    off = N_SPARE if key_side else 0
    sel_a = ((head % 2 == 0) & (lane == LANES * (head // 2) + HEAD_DIM + off + part)).astype(F32)
    sel_b = ((head % 2 == 1) & (lane == LANES * (head // 2) + off + part)).astype(F32)
    sign = -1.0 if key_side else 1.0
    ones_off = 0 if key_side else N_SPARE
    in_pair = jnp.arange(d)[None, :] % LANES
    ones_a = ((in_pair >= HEAD_DIM + ones_off) & (in_pair < HEAD_DIM + ones_off + N_SPARE)).astype(F32)
    ones_b = ((in_pair >= ones_off) & (in_pair < ones_off + N_SPARE)).astype(F32)
    return (sign * sel_a).astype(BF16), (sign * sel_b).astype(BF16), ones_a, ones_b


def _parts(x):
    return jnp.concatenate(_split3(x), axis=1)


def _fox_proj_fwd(h, g_norm, wq, wk, wv, wf, bf, sel_q, sel_k, tm):
    s_len, d = h.shape
    sq_a, sq_b, oq_a, oq_b = sel_q
    sk_a, sk_b, ok_a, ok_b = sel_k

    def body(h_ref, g_ref, wq_hbm, wk_hbm, wv_hbm, wf_ref, bf_ref, sqa_ref, sqb_ref, oqa_ref, oqb_ref,
             ska_ref, skb_ref, oka_ref, okb_ref,
             hn_ref, qa_ref, qb_ref, kat_ref, kbt_ref, va_ref, vb_ref, vat_ref, vbt_ref, z_ref,
             wq_v, wk_v, wv_v, total, sem):
        i = pl.program_id(0)
        _load_once([(wq_hbm, wq_v), (wk_hbm, wk_v), (wv_hbm, wv_v)], sem)

        @pl.when(i == 0)
        def _():
            total[...] = jnp.zeros_like(total)

        x = h_ref[...]
        _, _, y = _rms_fwd(x, g_ref[...])
        hn = y.astype(BF16)
        hn_ref[...] = hn
        z = _dot(hn, wf_ref[...]) + bf_ref[...]
        z_ref[...] = z
        logf = jnp.minimum(z, 0.0) - jnp.log(1.0 + jnp.exp(-jnp.abs(z)))
        tri = (lax.broadcasted_iota(jnp.int32, (tm, tm), 0) >= lax.broadcasted_iota(jnp.int32, (tm, tm), 1))
        cum = _dot3_rhs(jnp.where(tri, 1.0, 0.0).astype(BF16), logf) + total[0:1, :]
        total[...] = jnp.broadcast_to(cum[tm - 1:tm, :], total.shape)
        parts = _parts(cum)

        even = _even_head_lanes((tm, d), 1)
        q = _dot(hn, wq_v[...]) * (HEAD_DIM ** -0.5)
        qa_ref[...] = jnp.where(even, q, _dot(parts, sqa_ref[...]) + oqa_ref[...]).astype(BF16)
        qb_ref[...] = jnp.where(even, _dot(parts, sqb_ref[...]) + oqb_ref[...], q).astype(BF16)
        k = _dot(hn, wk_v[...])
        ka = jnp.where(even, k, _dot(parts, ska_ref[...]) + oka_ref[...])
        kb = jnp.where(even, _dot(parts, skb_ref[...]) + okb_ref[...], k)
        kat_ref[...] = ka.T.astype(BF16)
        kbt_ref[...] = kb.T.astype(BF16)
        v = _dot(hn, wv_v[...])
        va = jnp.where(even, v, oka_ref[...])
        vb = jnp.where(even, okb_ref[...], v)
        va_ref[...] = va.astype(BF16)
        vb_ref[...] = vb.astype(BF16)
        vat_ref[...] = va.T.astype(BF16)
        vbt_ref[...] = vb.T.astype(BF16)

    row = lambda i: (i, 0)
    col = lambda i: (0, i)
    const2 = lambda i: (0, 0)
    sd = jax.ShapeDtypeStruct((s_len, d), BF16)
    ds_ = jax.ShapeDtypeStruct((d, s_len), BF16)
    rs, cs = pl.BlockSpec((tm, d), row), pl.BlockSpec((d, tm), col)
    sel = pl.BlockSpec((N_SPARE * LANES, d), const2)
    one = pl.BlockSpec((1, d), const2)
    return pl.pallas_call(
        body, name="fox_proj_fwd", grid=(s_len // tm,),
        in_specs=[rs, one, ANY, ANY, ANY, pl.BlockSpec((d, LANES), const2), pl.BlockSpec((1, LANES), const2),
                  sel, sel, one, one, sel, sel, one, one],
        out_specs=[rs, rs, rs, cs, cs, rs, rs, cs, cs, pl.BlockSpec((tm, LANES), row)],
        out_shape=[sd, sd, sd, ds_, ds_, sd, sd, ds_, ds_, jax.ShapeDtypeStruct((s_len, LANES), F32)],
        scratch_shapes=[pltpu.VMEM((d, d), BF16), pltpu.VMEM((d, d), BF16), pltpu.VMEM((d, d), BF16),
                        pltpu.VMEM((8, LANES), F32), pltpu.SemaphoreType.DMA((3,))],
        compiler_params=_params(1),
    )(h, g_norm, wq, wk, wv, wf, bf, sq_a, sq_b, oq_a, oq_b, sk_a, sk_b, ok_a, ok_b)


def _spare_cols(x, base):
    xf = x[:, base:base + N_SPARE].astype(F32)
    return xf[:, 0:1] + xf[:, 1:2] + xf[:, 2:3]


def _with_query_term(x, term, base):
    lane = lax.broadcasted_iota(jnp.int32, x.shape, 1)
    hi, mid, lo = _split3(term)
    out = jnp.where(lane == base, hi.astype(F32), x)
    out = jnp.where(lane == base + 1, mid.astype(F32), out)
    out = jnp.where(lane == base + 2, lo.astype(F32), out)
    return jnp.where((lane >= base + N_SPARE) & (lane < base + 2 * N_SPARE), 1.0, out)


def _flash_fwd(qa, qb, kat, kbt, va, vb):
    s_len, d = qa.shape
    sub = ATT_BLOCK
    n_sub = 2 if s_len % (2 * sub) == 0 else 1
    t = n_sub * sub
    w = min(ATT_CHUNK, s_len)
    n_pair = d // LANES
    n_q = s_len // t
    bases = (HEAD_DIM, 0)
    chains = [(r, hh) for r in range(n_sub) for hh in range(2)]

    def body(qa_ref, qb_ref, kat_ref, kbt_ref, va_ref, vb_ref, o_ref, qa2_ref, qb2_ref, qat2_ref, qbt2_ref):
        i = pl.program_id(1)
        q_refs = (qa_ref, qb_ref)
        qs = [q_refs[hh][r * sub:(r + 1) * sub, :] for r, hh in chains]
        kts = (kat_ref, kbt_ref)
        vs = (va_ref, vb_ref)

        def step(kb, carry, masked, width=w):
            off = pl.multiple_of(kb * w, w)
            cols = pl.ds(off, width)
            scores = [_dot(qs[c], kts[hh][:, cols]) for c, (r, hh) in enumerate(chains)]
            probs, stats = [], []
            for c, (r, hh) in enumerate(chains):
                m, _ = carry[c]
                s = scores[c]
                if masked:
                    s = jnp.where(_causal(i * t + r * sub, off, (sub, width)), s, NEG_BIG)
                m_new = jnp.maximum(m, jnp.max(s, axis=1, keepdims=True))
                probs.append(jnp.exp(s - m_new).astype(BF16))
                stats.append((m_new, jnp.exp(m - m_new)))
            return tuple((stats[c][0], carry[c][1] * stats[c][1] + _dot(probs[c], vs[hh][cols, :]))
                         for c, (r, hh) in enumerate(chains))

        init = ((jnp.full((sub, 1), NEG_BIG, F32), jnp.zeros((sub, LANES), F32)),) * len(chains)
        diag = (i * t) // w
        carry = lax.fori_loop(0, diag, lambda kb, c: step(kb, c, False), init)
        carry = step(diag, carry, True)
        for r in range(n_sub):
            outs, q2 = [], []
            for hh in range(2):
                m, acc = carry[2 * r + hh]
                l = acc[:, bases[hh]:bases[hh] + 1]
                outs.append(acc / l)
                term = _spare_cols(qs[2 * r + hh], bases[hh]) - (m + jnp.log(l))
                q2.append(_with_query_term(qs[2 * r + hh].astype(F32), term, bases[hh]))
            rows = slice(r * sub, (r + 1) * sub)
            o_ref[rows, :] = _pair_select(outs[0], outs[1], (sub, LANES))
            qa2_ref[rows, :] = q2[0].astype(BF16)
            qb2_ref[rows, :] = q2[1].astype(BF16)
            qat2_ref[:, rows] = q2[0].T.astype(BF16)
            qbt2_ref[:, rows] = q2[1].T.astype(BF16)

    qblk = pl.BlockSpec((t, LANES), lambda j, i: (i, j))
    qblk_t = pl.BlockSpec((LANES, t), lambda j, i: (j, i))
    whole_t = pl.BlockSpec((LANES, s_len), lambda j, i: (j, 0))
    whole = pl.BlockSpec((s_len, LANES), lambda j, i: (0, j))
    sd = jax.ShapeDtypeStruct((s_len, d), BF16)
    ds_ = jax.ShapeDtypeStruct((d, s_len), BF16)
    return pl.pallas_call(
        body, name="flash_fwd", grid=(n_pair, n_q),
        in_specs=[qblk, qblk, whole_t, whole_t, whole, whole],
        out_specs=[qblk, qblk, qblk, qblk_t, qblk_t],
        out_shape=[jax.ShapeDtypeStruct((s_len, d), F32), sd, sd, ds_, ds_],
        compiler_params=_params(2),
    )(qa, qb, kat, kbt, va, vb)


def _flash_bwd(qa, qb, qat, qbt, kat, kbt, vat, vbt, doa, dob, doat, dobt):
    s_len, d = qa.shape
    sub = ATT_BLOCK
    n_sub = 2 if s_len % (2 * sub) == 0 else 1
    t = n_sub * sub
    w = min(ATT_CHUNK, s_len)
    n_pair = d // LANES
    n_q = s_len // t
    hd = HEAD_DIM
    chains = [(r, hh) for r in range(n_sub) for hh in range(2)]

    def body(qa_ref, qb_ref, qat_ref, qbt_ref, kat_hbm, kbt_hbm, vat_hbm, vbt_hbm, doa_ref, dob_ref, doat_ref, dobt_ref,
             dqt_ref, dkt_ref, dvt_ref, rs_ref, cs_ref,
             kat_v, kbt_v, vat_v, vbt_v, dkt_acc, dvt_acc, cs_acc, sem):
        j = pl.program_id(0)
        i = pl.program_id(1)

        @pl.when(i == 0)
        def _():
            rows = pl.ds(pl.multiple_of(j * LANES, LANES), LANES)
            copies = [pltpu.make_async_copy(src.at[rows, :], dst, sem.at[n]) for n, (src, dst) in enumerate([
                (kat_hbm, kat_v), (kbt_hbm, kbt_v), (vat_hbm, vat_v), (vbt_hbm, vbt_v)])]
            for cp in copies:
                cp.start()
            dkt_acc[...] = jnp.zeros_like(dkt_acc)
            dvt_acc[...] = jnp.zeros_like(dvt_acc)
            cs_acc[...] = jnp.zeros_like(cs_acc)
            for cp in copies:
                cp.wait()

        q_refs, do_refs = (qa_ref, qb_ref), (doa_ref, dob_ref)
        qs = [q_refs[hh][r * sub:(r + 1) * sub, :] for r, hh in chains]
        dos = [do_refs[hh][r * sub:(r + 1) * sub, :] for r, hh in chains]
        own = (slice(0, hd), slice(hd, 2 * hd))
        spare = (slice(hd, hd + 8), slice(0, 8))
        used = (slice(0, hd + 16), slice(0, 2 * hd))
        qts = (qat_ref[used[0], :], qbt_ref[used[1], :])
        dots = (doat_ref[own[0], :], dobt_ref[own[1], :])
        kts, vts = (kat_v, kbt_v), (vat_v, vbt_v)

        def step(kb, carry, masked, width=w):
            off = pl.multiple_of(kb * w, w)
            cols = pl.ds(off, width)
            scores = [_dot(qs[c], kts[hh][:, cols]) for c, (r, hh) in enumerate(chains)]
            dps = [_dot(dos[c], vts[hh][:, cols]) for c, (r, hh) in enumerate(chains)]
            ps, dss = [], []
            for c, (r, hh) in enumerate(chains):
                s = scores[c]
                if masked:
                    s = jnp.where(_causal(i * t + r * sub, off, (sub, width)), s, NEG_BIG)
                p = jnp.exp(s)
                dss.append((p * dps[c]).astype(BF16))
                ps.append(p.astype(BF16))
            out = tuple(carry[c] + _dot_nt(kts[hh][used[hh], cols], dss[c]) for c, (r, hh) in enumerate(chains))
            for hh in range(2):
                p_all = jnp.concatenate([ps[2 * r + hh] for r in range(n_sub)], axis=0)
                ds_all = jnp.concatenate([dss[2 * r + hh] for r in range(n_sub)], axis=0)
                dvt_acc[own[hh], cols] += _dot(dots[hh], p_all)
                with_sums = _dot(qts[hh], ds_all)
                dkt_acc[own[hh], cols] += with_sums[own[hh], :]
                cs_acc[8 * hh:8 * hh + 8, cols] += with_sums[spare[hh], :]
            return out

        diag = (i * t) // w
        init = (jnp.zeros((hd + 16, sub), F32), jnp.zeros((2 * hd, sub), F32)) * n_sub
        carry = lax.fori_loop(0, diag, lambda kb, c: step(kb, c, False), init)
        carry = lax.cond((i * t) % w + t <= w // 2,
                         lambda c: step(diag, c, True, w // 2), lambda c: step(diag, c, True), carry)
        for c, (r, hh) in enumerate(chains):
            at = slice(r * sub, (r + 1) * sub)
            dqt_ref[own[hh], at] = (carry[c][own[hh], :] * (hd ** -0.5)).astype(BF16)
            rs_ref[0, 8 * hh:8 * hh + 8, at] = carry[c][spare[hh], :]

        @pl.when(i == n_q - 1)
        def _():
            dkt_ref[...] = dkt_acc[...].astype(BF16)
            dvt_ref[...] = dvt_acc[...].astype(BF16)
            cs_ref[0] = cs_acc[...]

    qblk = pl.BlockSpec((t, LANES), lambda j, i: (i, j))
    qblk_t = pl.BlockSpec((LANES, t), lambda j, i: (j, i))
    whole_t = pl.BlockSpec((LANES, s_len), lambda j, i: (j, 0))
    ds_ = jax.ShapeDtypeStruct((d, s_len), BF16)
    sums = jax.ShapeDtypeStruct((n_pair, 16, s_len), F32)
    return pl.pallas_call(
        body, name="flash_bwd", grid=(n_pair, n_q),
        in_specs=[qblk, qblk, qblk_t, qblk_t, ANY, ANY, ANY, ANY, qblk, qblk, qblk_t, qblk_t],
        out_specs=[qblk_t, whole_t, whole_t, pl.BlockSpec((1, 16, t), lambda j, i: (j, 0, i)),
                   pl.BlockSpec((1, 16, s_len), lambda j, i: (j, 0, 0))],
        out_shape=[ds_, ds_, ds_, sums, sums],
        scratch_shapes=[pltpu.VMEM((LANES, s_len), BF16), pltpu.VMEM((LANES, s_len), BF16),
                        pltpu.VMEM((LANES, s_len), BF16), pltpu.VMEM((LANES, s_len), BF16),
                        pltpu.VMEM((LANES, s_len), F32), pltpu.VMEM((LANES, s_len), F32),
                        pltpu.VMEM((16, s_len), F32), pltpu.SemaphoreType.DMA((4,))],
        compiler_params=_params(2),
    )(qa, qb, qat, qbt, kat, kbt, vat, vbt, doa, dob, doat, dobt)


def _wgrad_t(at, b, name):
    k, s_len = at.shape
    n = b.shape[1]
    tn, tk, ts = min(n, 1024), min(k, 1024), min(s_len, 1024)

    def body(a_ref, b_ref, o_ref):
        @pl.when(pl.program_id(2) == 0)
        def _():
            o_ref[...] = jnp.zeros_like(o_ref)
        o_ref[...] += _dot(a_ref[...].astype(BF16), b_ref[...].astype(BF16))

    return pl.pallas_call(
        body, name=name, grid=(k // tk, n // tn, s_len // ts),
        in_specs=[pl.BlockSpec((tk, ts), lambda a, b_, c: (a, c)), pl.BlockSpec((ts, tn), lambda a, b_, c: (c, b_))],
        out_specs=pl.BlockSpec((tk, tn), lambda a, b_, c: (a, b_)),
        out_shape=jax.ShapeDtypeStruct((k, n), F32),
        compiler_params=_params(3),
    )(at, b)


def _oproj_bwd(dh, o, wo, seg, sel_q, tm):
    s_len, d = dh.shape
    sq_a, sq_b, _, _ = sel_q

    def body(dh_ref, o_ref, wo_hbm, seg_ref, sqa_ref, sqb_ref, doa_ref, dob_ref, doat_ref, dobt_ref, wo_v, sem):
        _load_once([(wo_hbm, wo_v)], sem)
        do = _dot_nt(dh_ref[...].astype(BF16), wo_v[...])
        parts = _parts(-_dot3_lhs(do * o_ref[...], seg_ref[...]))
        even = _even_head_lanes((tm, d), 1)
        doa = jnp.where(even, do, _dot(parts, sqa_ref[...]))
        dob = jnp.where(even, _dot(parts, sqb_ref[...]), do)
        doa_ref[...] = doa.astype(BF16)
        dob_ref[...] = dob.astype(BF16)
        doat_ref[...] = doa.T.astype(BF16)
        dobt_ref[...] = dob.T.astype(BF16)

    row = lambda i: (i, 0)
    const2 = lambda i: (0, 0)
    rs, cs = pl.BlockSpec((tm, d), row), pl.BlockSpec((d, tm), lambda i: (0, i))
    sel = pl.BlockSpec((N_SPARE * LANES, d), const2)
    sd = jax.ShapeDtypeStruct((s_len, d), BF16)
    ds_ = jax.ShapeDtypeStruct((d, s_len), BF16)
    return pl.pallas_call(
        body, name="oproj_bwd", grid=(s_len // tm,),
        in_specs=[rs, rs, ANY, pl.BlockSpec((d, LANES), const2), sel, sel],
        out_specs=[rs, rs, cs, cs], out_shape=[sd, sd, ds_, ds_],
        scratch_shapes=[pltpu.VMEM((d, d), BF16), pltpu.SemaphoreType.DMA((1,))],
        compiler_params=_params(1),
    )(dh, o, wo, seg, sq_a, sq_b)


def _oproj_fwd(h, o, wo, tm):
    s_len, d = h.shape

    def body(h_ref, o_ref, wo_hbm, hout_ref, wo_v, sem):
        _load_once([(wo_hbm, wo_v)], sem)
        hout_ref[...] = h_ref[...] + _dot(o_ref[...].astype(BF16), wo_v[...])

    row = lambda i: (i, 0)
    return pl.pallas_call(
        body, name="oproj_fwd", grid=(s_len // tm,),
        in_specs=[pl.BlockSpec((tm, d), row), pl.BlockSpec((tm, d), row), ANY],
        out_specs=pl.BlockSpec((tm, d), row),
        out_shape=jax.ShapeDtypeStruct((s_len, d), F32),
        scratch_shapes=[pltpu.VMEM((d, d), BF16), pltpu.SemaphoreType.DMA((1,))],
        compiler_params=_params(1),
    )(h, o, wo)


def _forget_bwd(dcum, z, tm):
    s_len = dcum.shape[0]
    n_blk = s_len // tm

    def body(dc_ref, z_ref, dfl_ref, gb_ref, total):
        i = pl.program_id(0)

        @pl.when(i == 0)
        def _():
            total[...] = jnp.zeros_like(total)
            gb_ref[...] = jnp.zeros_like(gb_ref)

        upper = (lax.broadcasted_iota(jnp.int32, (tm, tm), 0) <= lax.broadcasted_iota(jnp.int32, (tm, tm), 1))
        suffix = _dot3_rhs(jnp.where(upper, 1.0, 0.0).astype(BF16), dc_ref[...]) + total[0:1, :]
        total[...] = jnp.broadcast_to(suffix[0:1, :], total.shape)
        dfl = suffix * _sigmoid(-z_ref[...])
        dfl_ref[...] = dfl
        gb_ref[...] += jnp.sum(dfl, axis=0, keepdims=True)

    rev = lambda i: (n_blk - 1 - i, 0)
    return pl.pallas_call(
        body, name="forget_bwd", grid=(n_blk,),
        in_specs=[pl.BlockSpec((tm, LANES), rev), pl.BlockSpec((tm, LANES), rev)],
        out_specs=[pl.BlockSpec((tm, LANES), rev), pl.BlockSpec((1, LANES), lambda i: (0, 0))],
        out_shape=[jax.ShapeDtypeStruct((s_len, LANES), F32), jax.ShapeDtypeStruct((1, LANES), F32)],
        scratch_shapes=[pltpu.VMEM((8, LANES), F32)],
        compiler_params=_params(1),
    )(dcum, z)


def _fox_proj_bwd(h, dh, dqt, dkt, dvt, dfl, g_norm, wq, wk, wv, wf, tm):
    s_len, d = h.shape

    def body(h_ref, dh_ref, dqt_ref, dkt_ref, dvt_ref, dfl_ref, g_ref, wq_hbm, wk_hbm, wv_hbm, wf_ref,
             dhin_ref, dflb_ref, gn_ref, wq_v, wk_v, wv_v, sem):
        _load_once([(wq_hbm, wq_v), (wk_hbm, wk_v), (wv_hbm, wv_v)], sem)

        @pl.when(pl.program_id(0) == 0)
        def _():
            gn_ref[...] = jnp.zeros_like(gn_ref)

        g = g_ref[...]
        xhat, r, _ = _rms_fwd(h_ref[...], g)
        dflb = dfl_ref[...].astype(BF16)
        dflb_ref[...] = dflb
        from_qkv = (_dot(wq_v[...], dqt_ref[...]) + _dot(wk_v[...], dkt_ref[...])
                    + _dot(wv_v[...], dvt_ref[...]))
        dhn = _dot_nt(dflb, wf_ref[...]) + from_qkv.T
        dx, gg = _rms_bwd(dhn, xhat, r, g)
        gn_ref[...] += jnp.sum(gg, axis=0, keepdims=True)
        dhin_ref[...] = dh_ref[...] + dx

    row = lambda i: (i, 0)
    const2 = lambda i: (0, 0)
    rs = pl.BlockSpec((tm, d), row)
    cs = pl.BlockSpec((d, tm), lambda i: (0, i))
    return pl.pallas_call(
        body, name="fox_proj_bwd", grid=(s_len // tm,),
        in_specs=[rs, rs, cs, cs, cs, pl.BlockSpec((tm, LANES), row), pl.BlockSpec((1, d), const2), ANY, ANY, ANY,
                  pl.BlockSpec((d, LANES), const2)],
        out_specs=[rs, pl.BlockSpec((tm, LANES), row), pl.BlockSpec((1, d), const2)],
        out_shape=[jax.ShapeDtypeStruct((s_len, d), F32), jax.ShapeDtypeStruct((s_len, LANES), BF16),
                   jax.ShapeDtypeStruct((1, d), F32)],
        scratch_shapes=[pltpu.VMEM((d, d), BF16), pltpu.VMEM((d, d), BF16), pltpu.VMEM((d, d), BF16),
                        pltpu.SemaphoreType.DMA((3,))],
        compiler_params=_params(1),
    )(h, dh, dqt, dkt, dvt, dfl, g_norm, wq, wk, wv, wf)


def _loss_head(h, target, g_final, tm):
    s_len, d = h.shape
    n_blk = s_len // tm

    def body(h_ref, t_ref, g_ref, dh_ref, loss_ref, gg_ref, sq):
        i = pl.program_id(0)

        @pl.when(i == 0)
        def _():
            sq[...] = jnp.zeros_like(sq)
            gg_ref[...] = jnp.zeros_like(gg_ref)

        g = g_ref[...]
        xhat, r, y = _rms_fwd(h_ref[...], g)
        err = y - t_ref[...]
        sq[...] += jnp.sum(err * err, axis=0, keepdims=True)
        dx, gg = _rms_bwd(err * (1.0 / d), xhat, r, g)
        gg_ref[...] += jnp.sum(gg, axis=0, keepdims=True)
        dh_ref[...] = dx

        @pl.when(i == n_blk - 1)
        def _():
            loss_ref[...] = jnp.broadcast_to(jnp.sum(sq[...], axis=1, keepdims=True) * (0.5 / d), loss_ref.shape)

    row = lambda i: (i, 0)
    const2 = lambda i: (0, 0)
    return pl.pallas_call(
        body, name="loss_head", grid=(n_blk,),
        in_specs=[pl.BlockSpec((tm, d), row), pl.BlockSpec((tm, d), row), pl.BlockSpec((1, d), const2)],
        out_specs=[pl.BlockSpec((tm, d), row), pl.BlockSpec((1, LANES), const2), pl.BlockSpec((1, d), const2)],
        out_shape=[jax.ShapeDtypeStruct((s_len, d), F32), jax.ShapeDtypeStruct((1, LANES), F32),
                   jax.ShapeDtypeStruct((1, d), F32)],
        scratch_shapes=[pltpu.VMEM((1, d), F32)],
        compiler_params=_params(1),
    )(h, target, g_final)


def _wgrad(x, dy, n_piece, name):
    s_len, k = x.shape
    n = dy.shape[1]
    tn = min(n // n_piece, 1024)
    tk = min(k, 1024)
    ts = min(s_len, 1024)
    per_piece = (n // n_piece) // tn

    def body(x_ref, dy_ref, o_ref):
        @pl.when(pl.program_id(2) == 0)
        def _():
            o_ref[...] = jnp.zeros_like(o_ref)
        o_ref[0] += _dot_tn(x_ref[...].astype(BF16), dy_ref[...].astype(BF16))

    return pl.pallas_call(
        body, name=name, grid=(k // tk, n // tn, s_len // ts),
        in_specs=[pl.BlockSpec((ts, tk), lambda a, b, c: (c, a)), pl.BlockSpec((ts, tn), lambda a, b, c: (c, b))],
        out_specs=pl.BlockSpec((1, tk, tn), lambda a, b, c: (b // per_piece, a, b % per_piece)),
        out_shape=jax.ShapeDtypeStruct((n_piece, k, n // n_piece), F32),
        compiler_params=_params(3),
    )(x, dy)


def _pair_sum(g, recv, core, name):
    n_piece, rows, c = g.shape
    half = rows // 2
    tr = min(half, 512)
    nb = half // tr

    def body(core_ref, g_ref, r_ref, o_ref, ob_ref):
        total = g_ref[...] + r_ref[...]
        o_ref[...] = total
        ob_ref[...] = total.astype(BF16)

    blk = pl.BlockSpec((1, tr, c), lambda p, i, core_ref: (p, i, 0))
    return pl.pallas_call(
        body, name=name,
        out_shape=[jax.ShapeDtypeStruct((n_piece, half, c), F32), jax.ShapeDtypeStruct((n_piece, half, c), BF16)],
        grid_spec=pltpu.PrefetchScalarGridSpec(
            num_scalar_prefetch=1, grid=(n_piece, nb),
            in_specs=[pl.BlockSpec((1, tr, c), lambda p, i, core_ref: (p, core_ref[0] * nb + i, 0)), blk],
            out_specs=[blk, blk]),
        compiler_params=_params(2),
    )(core, g, recv)


def _chip_sum(halves, recv, chip, name):
    _, h, c = halves.shape
    tr = min(h, 512)

    def body(chip_ref, own_ref, r_ref, o_ref):
        o_ref[...] = ((own_ref[0] + r_ref[0].astype(F32)) + r_ref[1].astype(F32)) + r_ref[2].astype(F32)

    return pl.pallas_call(
        body, name=name, out_shape=jax.ShapeDtypeStruct((h, c), F32),
        grid_spec=pltpu.PrefetchScalarGridSpec(
            num_scalar_prefetch=1, grid=(h // tr,),
            in_specs=[pl.BlockSpec((1, tr, c), lambda i, chip_ref: (chip_ref[0], i, 0)),
                      pl.BlockSpec((3, tr, c), lambda i, chip_ref: (0, i, 0))],
            out_specs=pl.BlockSpec((tr, c), lambda i, chip_ref: (i, 0))),
        compiler_params=_params(1),
    )(chip, halves, recv)


def _adamw_math(w, m, v, g):
    m_new = ADAM_B1 * m + (1.0 - ADAM_B1) * g
    v_new = ADAM_B2 * v + (1.0 - ADAM_B2) * (g * g)
    m_hat = m_new / (1.0 - ADAM_B1 ** ADAM_STEP)
    v_hat = v_new / (1.0 - ADAM_B2 ** ADAM_STEP)
    return -ADAM_LR * (m_hat / (jnp.sqrt(v_hat) + ADAM_EPS) + ADAM_WD * w), m_new, v_new


def _adamw(w, m, v, g, name):
    rows, c = w.shape
    tr = min(rows, 256)

    def body(w_ref, m_ref, v_ref, g_ref, d_ref, mo_ref, vo_ref):
        d_ref[...], mo_ref[...], vo_ref[...] = _adamw_math(w_ref[...], m_ref[...], v_ref[...], g_ref[...])

    spec = pl.BlockSpec((tr, c), lambda i: (i, 0))
    shape = jax.ShapeDtypeStruct((rows, c), F32)
    return pl.pallas_call(
        body, name=name, grid=(rows // tr,),
        in_specs=[spec] * 4, out_specs=[spec] * 3, out_shape=[shape] * 3,
        compiler_params=_params(1),
    )(w, m, v, g)


def _adamw_halves(w, m, v, g_own, g_other, core, name):
    rows, c = w.shape
    half = rows // 2
    tr = min(half, 256)
    nb = half // tr

    def body(core_ref, w_ref, m_ref, v_ref, own_ref, other_ref, g_ref, d_ref, mo_ref, vo_ref):
        mine = (pl.program_id(0) // nb) == core_ref[0]
        g = jnp.where(mine, own_ref[...], other_ref[...])
        g_ref[...] = g
        d_ref[...], mo_ref[...], vo_ref[...] = _adamw_math(w_ref[...], m_ref[...], v_ref[...], g)

    spec = pl.BlockSpec((tr, c), lambda i, core_ref: (i, 0))
    own = pl.BlockSpec((tr, c), lambda i, core_ref: (jnp.clip(i - core_ref[0] * nb, 0, nb - 1), 0))
    other = pl.BlockSpec((tr, c), lambda i, core_ref: (jnp.clip(i - (1 - core_ref[0]) * nb, 0, nb - 1), 0))
    shape = jax.ShapeDtypeStruct((rows, c), F32)
    return pl.pallas_call(
        body, name=name, out_shape=[shape] * 4,
        grid_spec=pltpu.PrefetchScalarGridSpec(
            num_scalar_prefetch=1, grid=(rows // tr,),
            in_specs=[spec, spec, spec, own, other], out_specs=[spec] * 4),
        compiler_params=_params(1),
    )(core, w, m, v, g_own, g_other)


def _place():
    x, y, c = lax.axis_index("x"), lax.axis_index("y"), lax.axis_index("c")
    chips = [(1 - x, y), (x, 1 - y), (1 - x, 1 - y)]
    return x, y, c, chips


def _half_of(ref, half, which):
    start = which * half
    if half % 8 == 0:
        start = pl.multiple_of(start, 8)
    return ref.at[pl.ds(start, half)]


def _gather_copies(ins, outs, send_sems, recv_sems):
    x, y, c, chips = _place()
    mine = 2 * x + y
    copies = []
    for k, (src, dst) in enumerate(zip(ins, outs)):
        half = src.shape[0] // 2
        copies.append(pltpu.make_async_remote_copy(
            src_ref=src, dst_ref=dst.at[mine], send_sem=send_sems.at[4 * k + 3], recv_sem=recv_sems.at[4 * k + 3],
            device_id=(x, y, 1 - c), device_id_type=MESH))
        for j, (tx, ty) in enumerate(chips):
            copies.append(pltpu.make_async_remote_copy(
                src_ref=_half_of(src, half, c), dst_ref=_half_of(dst.at[mine], half, c),
                send_sem=send_sems.at[4 * k + j], recv_sem=recv_sems.at[4 * k + j],
                device_id=(tx, ty, c), device_id_type=MESH))
    return copies


def _gather_ici(shards):
    n = len(shards)

    def body(*refs):
        copies = _gather_copies(refs[:n], refs[n:2 * n], refs[2 * n], refs[2 * n + 1])
        for cp in copies:
            cp.start()
        for cp in copies:
            cp.wait()

    return pl.pallas_call(
        body, name="weights_gather_ici",
        in_specs=[ANY] * n, out_specs=[ANY] * n,
        out_shape=[jax.ShapeDtypeStruct((4,) + s.shape, s.dtype) for s in shards],
        scratch_shapes=[pltpu.SemaphoreType.DMA((4 * n,)), pltpu.SemaphoreType.DMA((4 * n,))],
    )(*shards)


HBM_SPEC = pl.BlockSpec(memory_space=pltpu.HBM)
SEM_SPEC = pl.BlockSpec(memory_space=pltpu.SEMAPHORE)
IN_FLIGHT = pltpu.SideEffectType.DATAFLOW_SIDE_EFFECTING


def _gather_ici_start(shards):
    n = len(shards)

    def body(*refs):
        for cp in _gather_copies(refs[:n], refs[n:2 * n], refs[2 * n], refs[2 * n + 1]):
            cp.start()
        token = refs[-1]
        token[...] = jnp.zeros_like(token)

    lands = [lax.empty((4,) + s.shape, s.dtype) for s in shards]
    out = pl.pallas_call(
        body, name="weights_gather_start",
        out_shape=(pltpu.SemaphoreType.DMA((4 * n,)), pltpu.SemaphoreType.DMA((4 * n,)),
                   *[pltpu.HBM(s.shape, s.dtype) for s in shards], *[pltpu.HBM(l.shape, l.dtype) for l in lands],
                   jax.ShapeDtypeStruct((8, LANES), F32)),
        in_specs=[HBM_SPEC] * (2 * n),
        out_specs=(SEM_SPEC, SEM_SPEC, *[HBM_SPEC] * (2 * n), pl.BlockSpec(memory_space=pltpu.VMEM)),
        input_output_aliases={k: 2 + k for k in range(2 * n)},
        compiler_params=pltpu.CompilerParams(has_side_effects=IN_FLIGHT),
    )(*[pltpu.with_memory_space_constraint(a, pltpu.HBM) for a in list(shards) + lands])
    return out[0], out[1], out[2:2 + n], out[2 + n:2 + 2 * n], out[-1]


def _gather_ici_wait(send_sems, recv_sems, sources, lands, after):
    n = len(sources)

    def body(*refs):
        for cp in _gather_copies(refs[:n], refs[n:2 * n], refs[2 * n], refs[2 * n + 1]):
            cp.wait_send()
            cp.wait_recv()

    out = pl.pallas_call(
        body, name="weights_gather_wait",
        out_shape=[pltpu.HBM(a.shape, a.dtype) for a in list(sources) + list(lands)],
        in_specs=[HBM_SPEC] * (2 * n) + [SEM_SPEC, SEM_SPEC, ANY], out_specs=[HBM_SPEC] * (2 * n),
        input_output_aliases={k: k for k in range(2 * n)},
        compiler_params=pltpu.CompilerParams(has_side_effects=IN_FLIGHT),
    )(*sources, *lands, send_sems, recv_sems, after)
    return out[n:]


def _gather_pair(gathered):
    n = len(gathered)

    def body(*refs):
        outs = refs[n:2 * n]
        send_sems, recv_sems = refs[2 * n:]
        x, y, c, chips = _place()
        sends = []
        for k in range(n):
            half = gathered[k].shape[1] // 2
            for j, (tx, ty) in enumerate(chips):
                piece = _half_of(outs[k].at[2 * tx + ty], half, c)
                sends.append(pltpu.make_async_remote_copy(
                    src_ref=piece, dst_ref=piece, send_sem=send_sems.at[k, j], recv_sem=recv_sems.at[k, j],
                    device_id=(x, y, 1 - c), device_id_type=MESH))
        for cp in sends:
            cp.start()
        for cp in sends:
            cp.wait()

    return pl.pallas_call(
        body, name="weights_gather_pair",
        in_specs=[ANY] * n, out_specs=[ANY] * n,
        out_shape=[jax.ShapeDtypeStruct(g.shape, g.dtype) for g in gathered],
        input_output_aliases={k: k for k in range(n)},
        scratch_shapes=[pltpu.SemaphoreType.DMA((n, 3)), pltpu.SemaphoreType.DMA((n, 3))],
    )(*gathered)


def _pair_exchange(grads):
    n = len(grads)

    def body(*refs):
        ins, outs = refs[:n], refs[n:2 * n]
        send_sems, recv_sems = refs[2 * n:]
        x, y, c, _ = _place()
        copies = []
        for k in range(n):
            half = grads[k].shape[1] // 2
            other = ins[k].at[:, pl.ds(pl.multiple_of((1 - c) * half, 8), half), :]
            copies.append(pltpu.make_async_remote_copy(
                src_ref=other, dst_ref=outs[k], send_sem=send_sems.at[k], recv_sem=recv_sems.at[k],
                device_id=(x, y, 1 - c), device_id_type=MESH))
        for cp in copies:
            cp.start()
        for cp in copies:
            cp.wait()

    return pl.pallas_call(
        body, name="grads_pair_exchange",
        in_specs=[ANY] * n, out_specs=[ANY] * n,
        out_shape=[jax.ShapeDtypeStruct((4, g.shape[1] // 2, g.shape[2]), F32) for g in grads],
        scratch_shapes=[pltpu.SemaphoreType.DMA((n,)), pltpu.SemaphoreType.DMA((n,))],
    )(*grads)


def _chip_scatter(halves):
    n = len(halves)

    def body(*refs):
        ins, outs = refs[:n], refs[n:2 * n]
        send_sems, recv_sems = refs[2 * n:]
        x, y, c, chips = _place()
        sends = []
        for k in range(n):
            for j, (tx, ty) in enumerate(chips):
                sends.append(pltpu.make_async_remote_copy(
                    src_ref=ins[k].at[2 * tx + ty], dst_ref=outs[k].at[j], send_sem=send_sems.at[k, j],
                    recv_sem=recv_sems.at[k, j], device_id=(tx, ty, c), device_id_type=MESH))
        for cp in sends:
            cp.start()
        for cp in sends:
            cp.wait()

    return pl.pallas_call(
        body, name="grads_chip_scatter",
        in_specs=[ANY] * n, out_specs=[ANY] * n,
        out_shape=[jax.ShapeDtypeStruct((3,) + hv.shape[1:], hv.dtype) for hv in halves],
        scratch_shapes=[pltpu.SemaphoreType.DMA((n, 3)), pltpu.SemaphoreType.DMA((n, 3))],
    )(*halves)


def _pair_share(finals):
    n = len(finals)

    def body(*refs):
        ins, outs = refs[:n], refs[n:2 * n]
        send_sems, recv_sems = refs[2 * n:]
        x, y, c, _ = _place()
        copies = [pltpu.make_async_remote_copy(
            src_ref=ins[k], dst_ref=outs[k], send_sem=send_sems.at[k], recv_sem=recv_sems.at[k],
            device_id=(x, y, 1 - c), device_id_type=MESH) for k in range(n)]
        for cp in copies:
            cp.start()
        for cp in copies:
            cp.wait()

    return pl.pallas_call(
        body, name="grads_pair_share",
        in_specs=[ANY] * n, out_specs=[ANY] * n,
        out_shape=[jax.ShapeDtypeStruct(fv.shape, F32) for fv in finals],
        scratch_shapes=[pltpu.SemaphoreType.DMA((n,)), pltpu.SemaphoreType.DMA((n,))],
    )(*finals)


def _small_all_reduce(buf):
    rows, c_ = buf.shape

    def body(in_ref, out_ref, pair_buf, slots, send_sems, recv_sems):
        x, y, c, chips = _place()
        mine = 2 * x + y
        pair = pltpu.make_async_remote_copy(
            src_ref=in_ref, dst_ref=pair_buf, send_sem=send_sems.at[0], recv_sem=recv_sems.at[0],
            device_id=(x, y, 1 - c), device_id_type=MESH)
        pair.start()
        pair.wait()
        slots[mine] = in_ref[...] + pair_buf[...]
        sends = [pltpu.make_async_remote_copy(
            src_ref=slots.at[mine], dst_ref=slots.at[mine], send_sem=send_sems.at[1 + j], recv_sem=recv_sems.at[1 + j],
            device_id=(tx, ty, c), device_id_type=MESH) for j, (tx, ty) in enumerate(chips)]
        for cp in sends:
            cp.start()
        for j, (tx, ty) in enumerate(chips):
            pltpu.make_async_remote_copy(
                src_ref=slots.at[mine], dst_ref=slots.at[2 * tx + ty], send_sem=send_sems.at[1 + j],
                recv_sem=recv_sems.at[1 + j], device_id=(tx, ty, c), device_id_type=MESH).wait()
        out_ref[...] = ((slots[0] + slots[1]) + slots[2]) + slots[3]

    vm = pl.BlockSpec(memory_space=pltpu.VMEM)
    return pl.pallas_call(
        body, name="small_all_reduce", in_specs=[vm], out_specs=vm,
        out_shape=jax.ShapeDtypeStruct((rows, c_), F32),
        scratch_shapes=[pltpu.VMEM((rows, c_), F32), pltpu.VMEM((4, rows, c_), F32),
                        pltpu.SemaphoreType.DMA((4,)), pltpu.SemaphoreType.DMA((4,))],
        compiler_params=pltpu.CompilerParams(vmem_limit_bytes=VMEM_LIMIT_V7X),
    )(buf)


def _reduce_scatter(grads):
    core = lax.axis_index("c").astype(jnp.int32).reshape(1)
    chip = (2 * lax.axis_index("x") + lax.axis_index("y")).astype(jnp.int32).reshape(1)
    recv = _pair_exchange(grads)
    halves = [_pair_sum(g, r, core, f"pair_sum_{k}") for k, (g, r) in enumerate(zip(grads, recv))]
    recv = _chip_scatter([hb for _, hb in halves])
    finals = [_chip_sum(hv, r, chip, f"chip_sum_{k}") for k, ((hv, _), r) in enumerate(zip(halves, recv))]
    return list(zip(finals, _pair_share(finals))), core


PACK_COLS = 1024


def _pack(arrays):
    flat = jnp.concatenate([a.reshape(-1).astype(F32) for a in arrays])
    rows = -(-flat.shape[0] // PACK_COLS)
    rows = -(-rows // 8) * 8
    return jnp.pad(flat, (0, rows * PACK_COLS - flat.shape[0])).reshape(rows, PACK_COLS)


def _unpack(buf, shapes):
    flat = buf.reshape(-1)
    out, at = [], 0
    for shp in shapes:
        size = math.prod(shp)
        out.append(flat[at:at + size].reshape(shp))
        at += size
    return out


def kernel(x, mix_norm_g, ffn_norm_g, gm_w_in, gm_ln_g, gm_ln_b, gm_w_s, gm_b_s, gm_w_out, fox_w_qkvf, fox_b_f, fox_w_o, ffn_w_gate, ffn_w_up, ffn_conv_w, ffn_conv_b, ffn_w_down, final_norm_g, loss_target, m_mix_norm_g, m_ffn_norm_g, m_gm_w_in, m_gm_ln_g, m_gm_ln_b, m_gm_w_s, m_gm_b_s, m_gm_w_out, m_fox_w_qkvf, m_fox_b_f, m_fox_w_o, m_ffn_w_gate, m_ffn_w_up, m_ffn_conv_w, m_ffn_conv_b, m_ffn_w_down, m_final_norm_g, v_mix_norm_g, v_ffn_norm_g, v_gm_w_in, v_gm_ln_g, v_gm_ln_b, v_gm_w_s, v_gm_b_s, v_gm_w_out, v_fox_w_qkvf, v_fox_b_f, v_fox_w_o, v_ffn_w_gate, v_ffn_w_up, v_ffn_conv_w, v_ffn_conv_b, v_ffn_w_down, v_final_norm_g):
    _, s_len, d = x.shape
    e = gm_ln_g.shape[1]
    f = ffn_conv_b.shape[1]
    n_head = fox_b_f.shape[1]
    n_pair = n_head // 2
    gd = e // GM_GROUPS
    qkvf_cols = fox_w_qkvf.shape[2]
    assert d == n_head * HEAD_DIM and d % (2 * LANES) == 0 and s_len % 512 == 0 and gd % LANES == 0
    assert gm_w_s.shape[2] == CHUNK and 4 * qkvf_cols == 3 * d + n_head
    tm = 256
    h0 = x[0]
    target = loss_target[0]

    w_in, w_out4 = _gather_pair(_gather_ici([gm_w_in[0].astype(BF16), gm_w_out[0].astype(BF16)]))
    send_sems, recv_sems, sources, lands, token = _gather_ici_start([
        fox_w_qkvf[0].astype(BF16), fox_w_o[0].astype(BF16), ffn_w_gate.astype(BF16), ffn_w_up.astype(BF16),
        ffn_w_down.astype(BF16), ffn_conv_w])
    w_out = w_out4.reshape(e, d)
    bf_pad = jnp.pad(fox_b_f, ((0, 0), (0, LANES - n_head)))

    tril = jnp.tril(jnp.ones((CHUNK, CHUNK), bool))
    wc = jnp.where(tril[None], gm_w_s[0], 0.0).astype(BF16)
    wct = jnp.transpose(wc, (0, 2, 1))
    bias = jnp.repeat(gm_b_s[0].T, gd, axis=1)
    seg_groups = (jnp.arange(e)[:, None] // gd == jnp.arange(LANES)[None, :]).astype(BF16)
    seg_heads = (jnp.arange(d)[:, None] // HEAD_DIM == jnp.arange(LANES)[None, :]).astype(BF16)
    sel_q = _spare_selectors(d, key_side=False)
    sel_k = _spare_selectors(d, key_side=True)

    h1, a0, hn0, gated0 = _gmlp_fwd(h0, mix_norm_g[0:1] + token[0:1, 0:1], w_in, gm_ln_g, gm_ln_b, wc, bias, w_out, tm)
    qkvf4, wo4, wg_all, wu_all, wd_all, cw4 = _gather_pair(_gather_ici_wait(send_sems, recv_sems, sources, lands, h1))
    qkvf = jnp.transpose(qkvf4, (1, 0, 2)).reshape(d, 4 * qkvf_cols)
    wq, wk, wv = qkvf[:, :d], qkvf[:, d:2 * d], qkvf[:, 2 * d:3 * d]
    wf = jnp.pad(qkvf[:, 3 * d:], ((0, 0), (0, LANES - n_head)))
    wo = wo4.reshape(d, d)
    conv_w_full = jnp.transpose(cw4, (1, 2, 0, 3)).reshape(2, 3, f)
    conv_w8 = jnp.pad(conv_w_full, ((0, 0), (0, 5), (0, 0)))
    h2, fa0, fup0, fhn0, fhid0 = _ffn_fwd(h1, ffn_norm_g[0:1], wg_all, wu_all, wd_all, 0, conv_w8[0], ffn_conv_b[0:1], tm)
    (hn1, qa, qb, kat, kbt, va, vb, vat, vbt, z_f) = _fox_proj_fwd(
        h2, mix_norm_g[1:2], wq, wk, wv, wf, bf_pad, sel_q, sel_k, tm)
    o, qa2, qb2, qat2, qbt2 = _flash_fwd(qa, qb, kat, kbt, va, vb)
    h3 = _oproj_fwd(h2, o, wo, tm)
    h4, fa1, fup1, fhn1, fhid1 = _ffn_fwd(h3, ffn_norm_g[1:2], wg_all, wu_all, wd_all, 1, conv_w8[1], ffn_conv_b[1:2], tm)

    dh4, loss_part, g_final = _loss_head(h4, target, final_norm_g.reshape(1, d), tm)
    dh3, da1, dup1, gcw1, gcb1, gfn1 = _ffn_bwd(h3, dh4, fa1, fup1, ffn_norm_g[1:2], wg_all, wu_all, wd_all, 1,
                                                conv_w8[1], ffn_conv_b[1:2], tm)
    g_gate1 = _wgrad(fhn1, da1, 4, "wgrad_gate_1")
    g_up1 = _wgrad(fhn1, dup1, 4, "wgrad_up_1")
    g_down1 = _wgrad(fhid1, dh4, 1, "wgrad_down_1").reshape(4, f // 4, d)

    doa, dob, doat, dobt = _oproj_bwd(dh3, o, wo, seg_heads, sel_q, tm)
    g_wo = _wgrad(o, dh3, 1, "wgrad_wo").reshape(4, d // 4, d)
    dqt, dkt, dvt, row_sums, col_sums = _flash_bwd(qa2, qb2, qat2, qbt2, kat, kbt, vat, vbt, doa, dob, doat, dobt)
    sums = row_sums[:, 0::8, :] - col_sums[:, N_SPARE::8, :]
    dcum = jnp.pad(sums.reshape(n_head, s_len).T, ((0, 0), (0, LANES - n_head)))
    dfl, g_bf = _forget_bwd(dcum, z_f, tm)
    dh2, dflb, gmn1 = _fox_proj_bwd(h2, dh3, dqt, dkt, dvt, dfl, mix_norm_g[1:2], wq, wk, wv, wf, tm)
    g_q = _wgrad_t(dqt, hn1, "wgrad_q").T
    g_k = _wgrad_t(dkt, hn1, "wgrad_k").T
    g_v = _wgrad_t(dvt, hn1, "wgrad_v").T
    g_f = _wgrad(hn1, dflb, 1, "wgrad_f")[0][:, :n_head]
    g_qkvf = jnp.concatenate([g_q, g_k, g_v, g_f], axis=1).reshape(d, 4, qkvf_cols).transpose(1, 0, 2)

    dh1, da0f, dup0, gcw0, gcb0, gfn0 = _ffn_bwd(h1, dh2, fa0, fup0, ffn_norm_g[0:1], wg_all, wu_all, wd_all, 0,
                                                 conv_w8[0], ffn_conv_b[0:1], tm)
    g_gate0 = _wgrad(fhn0, da0f, 4, "wgrad_gate_0")
    g_up0 = _wgrad(fhn0, dup0, 4, "wgrad_up_0")
    g_down0 = _wgrad(fhid0, dh2, 1, "wgrad_down_0").reshape(4, f // 4, d)

    dh0, da0, g_ws, g_bs_t, g_lng, g_lnb, gmn0 = _gmlp_bwd(
        h0, dh1, a0, mix_norm_g[0:1], w_in, gm_ln_g, gm_ln_b, wc, wct, bias, w_out, seg_groups, tm)
    g_win = _wgrad(hn0, da0, 4, "wgrad_gm_in")
    g_wout = _wgrad(gated0, dh1, 1, "wgrad_gm_out").reshape(4, e // 4, d)

    big, core = _reduce_scatter([g_win, g_wout, g_qkvf, g_wo, g_gate0, g_gate1, g_up0, g_up1, g_down0, g_down1])
    r_win, r_wout, r_qkvf, r_wo, r_gate0, r_gate1, r_up0, r_up1, r_down0, r_down1 = big

    small = [jnp.concatenate([gmn0, gmn1]), jnp.concatenate([gfn0, gfn1]), g_lng, g_lnb, g_ws[None],
             g_bs_t[:, :GM_GROUPS].T[None], g_bf[:, :n_head], jnp.stack([gcw0[:3], gcw1[:3]]),
             jnp.concatenate([gcb0, gcb1]), g_final.reshape(d), loss_part[0, :1]]
    small_shapes = [a.shape for a in small]
    reduced = _unpack(_small_all_reduce(_pack(small)), small_shapes)
    (r_mix, r_ffn, r_lng, r_lnb, r_ws, r_bs, r_bf, r_cw_full, r_cb, r_final, r_loss) = reduced
    chip = 2 * lax.axis_index("x") + lax.axis_index("y")
    r_cw = lax.dynamic_slice_in_dim(r_cw_full, chip * (f // 4), f // 4, axis=2)

    def update_big(name, w, m, v, per_layer):
        parts = [_adamw_halves(w[l], m[l], v[l], own, other, core, f"adamw_{name}_{l}")
                 for l, (own, other) in enumerate(per_layer)]
        return tuple(jnp.stack([p[i] for p in parts]) for i in range(4))

    res = {}
    res["gm_w_in"] = update_big("gm_w_in", gm_w_in, m_gm_w_in, v_gm_w_in, [r_win])
    res["gm_w_out"] = update_big("gm_w_out", gm_w_out, m_gm_w_out, v_gm_w_out, [r_wout])
    res["fox_w_qkvf"] = update_big("fox_w_qkvf", fox_w_qkvf, m_fox_w_qkvf, v_fox_w_qkvf, [r_qkvf])
    res["fox_w_o"] = update_big("fox_w_o", fox_w_o, m_fox_w_o, v_fox_w_o, [r_wo])
    res["ffn_w_gate"] = update_big("ffn_w_gate", ffn_w_gate, m_ffn_w_gate, v_ffn_w_gate, [r_gate0, r_gate1])
    res["ffn_w_up"] = update_big("ffn_w_up", ffn_w_up, m_ffn_w_up, v_ffn_w_up, [r_up0, r_up1])
    res["ffn_w_down"] = update_big("ffn_w_down", ffn_w_down, m_ffn_w_down, v_ffn_w_down, [r_down0, r_down1])

    small_names = ["mix_norm_g", "ffn_norm_g", "gm_ln_g", "gm_ln_b", "gm_w_s", "gm_b_s", "fox_b_f", "ffn_conv_w",
                   "ffn_conv_b", "final_norm_g"]
    small_w = [mix_norm_g, ffn_norm_g, gm_ln_g, gm_ln_b, gm_w_s, gm_b_s, fox_b_f, ffn_conv_w, ffn_conv_b, final_norm_g]
    small_m = [m_mix_norm_g, m_ffn_norm_g, m_gm_ln_g, m_gm_ln_b, m_gm_w_s, m_gm_b_s, m_fox_b_f, m_ffn_conv_w,
               m_ffn_conv_b, m_final_norm_g]
    small_v = [v_mix_norm_g, v_ffn_norm_g, v_gm_ln_g, v_gm_ln_b, v_gm_w_s, v_gm_b_s, v_fox_b_f, v_ffn_conv_w,
               v_ffn_conv_b, v_final_norm_g]
    small_g = [r_mix, r_ffn, r_lng, r_lnb, r_ws, r_bs, r_bf, r_cw, r_cb, r_final]
    shapes = [w.shape for w in small_w]
    small_g = [g.reshape(s) for g, s in zip(small_g, shapes)]
    dlt, mn, vn = _adamw(_pack(small_w), _pack(small_m), _pack(small_v), _pack(small_g), "adamw_small")
    for name, g, dl_, m_, v_ in zip(small_names, small_g, _unpack(dlt, shapes), _unpack(mn, shapes), _unpack(vn, shapes)):
        res[name] = (g, dl_, m_, v_)

    order = ["mix_norm_g", "ffn_norm_g", "gm_w_in", "gm_ln_g", "gm_ln_b", "gm_w_s", "gm_b_s", "gm_w_out", "fox_w_qkvf",
             "fox_b_f", "fox_w_o", "ffn_w_gate", "ffn_w_up", "ffn_conv_w", "ffn_conv_b", "ffn_w_down", "final_norm_g"]
    outs = [r_loss.reshape(()), dh0[None]]
    for part in range(4):
        outs += [res[name][part] for name in order]
    return tuple(outs)
```

```python
import functools
import math

import jax
import jax.numpy as jnp
from jax import lax
from jax.experimental import pallas as pl
from jax.experimental.pallas import tpu as pltpu

F32 = jnp.float32
BF16 = jnp.bfloat16

RMS_EPS = 1e-6
LN_EPS = 1e-5
CHUNK = 128
GM_GROUPS = 8
HEAD_DIM = 64
LANES = 128
ATT_BLOCK = 256
ATT_CHUNK = 1024
VMEM_LIMIT_V7X = 56 * 1024 * 1024

ADAM_LR = 0.001
ADAM_B1 = 0.9
ADAM_B2 = 0.999
ADAM_EPS = 1e-08
ADAM_WD = 0.01
ADAM_STEP = 10

MESH = pl.DeviceIdType.MESH
ANY = pl.BlockSpec(memory_space=pl.ANY)
NEG_BIG = -1e30


def _params(n_grid):
    return pltpu.CompilerParams(dimension_semantics=("arbitrary",) * n_grid, vmem_limit_bytes=VMEM_LIMIT_V7X)


def _dot(a, b):
    return jnp.dot(a, b, preferred_element_type=F32)


def _dot_nt(a, b):
    return lax.dot_general(a, b, (((1,), (1,)), ((), ())), preferred_element_type=F32)


def _dot_tn(a, b):
    return lax.dot_general(a, b, (((0,), (0,)), ((), ())), preferred_element_type=F32)


def _split3(x):
    hi = x.astype(BF16)
    r = x - hi.astype(F32)
    mid = r.astype(BF16)
    lo = (r - mid.astype(F32)).astype(BF16)
    return hi, mid, lo


def _dot3_lhs(x, m):
    hi, mid, lo = _split3(x)
    return _dot(hi, m) + _dot(mid, m) + _dot(lo, m)


def _dot3_rhs(m, x):
    hi, mid, lo = _split3(x)
    return _dot(m, hi) + _dot(m, mid) + _dot(m, lo)


def _load_once(pairs, sem):
    @pl.when(pl.program_id(0) == 0)
    def _():
        copies = [pltpu.make_async_copy(src, dst, sem.at[k]) for k, (src, dst) in enumerate(pairs)]
        for cp in copies:
            cp.start()
        for cp in copies:
            cp.wait()


def _rms_fwd(x, g):
    r = lax.rsqrt(jnp.mean(x * x, axis=-1, keepdims=True) + RMS_EPS)
    xhat = x * r
    return xhat, r, xhat * g


def _rms_bwd(dy, xhat, r, g):
    w = dy * g
    dx = r * (w - xhat * jnp.mean(w * xhat, axis=-1, keepdims=True))
    return dx, dy * xhat


def _gelu_parts(a):
    c = math.sqrt(2.0 / math.pi)
    a2 = a * a
    t = jnp.tanh(c * (a + 0.044715 * a * a2))
    z = 0.5 * a * (1.0 + t)
    dz = 0.5 * (1.0 + t) + 0.5 * a * (1.0 - t * t) * (c * (1.0 + 3.0 * 0.044715 * a2))
    return z, dz


def _sigmoid(x):
    return 1.0 / (1.0 + jnp.exp(-x))


def _gmlp_core(a, lng, lnb, wc_ref, bias, n_chunk, gd):
    e = a.shape[1] // 2
    z, dz = _gelu_parts(a)
    u = z[:, :e]
    v = z[:, e:]
    mu = jnp.mean(v, axis=-1, keepdims=True)
    vc = v - mu
    rstd = lax.rsqrt(jnp.mean(vc * vc, axis=-1, keepdims=True) + LN_EPS)
    vhat = vc * rstd
    vln = vhat * lng + lnb
    vlb = vln.astype(BF16)
    rows = []
    for ci in range(n_chunk):
        cols = []
        for g in range(GM_GROUPS):
            blk = vlb[ci * CHUNK:(ci + 1) * CHUNK, g * gd:(g + 1) * gd]
            cols.append(_dot(wc_ref[g], blk))
        rows.append(jnp.concatenate(cols, axis=1) + bias)
    s = rows[0] if n_chunk == 1 else jnp.concatenate(rows, axis=0)
    return dz, u, vhat, rstd, vlb, s


def _gmlp_fwd(h, g_mix, w_in, lng, lnb, wc, bias, w_out, tm):
    s_len, d = h.shape
    n_p, _, w = w_in.shape
    e = w_out.shape[0]
    gd = e // GM_GROUPS
    n_chunk = tm // CHUNK

    def body(h_ref, g_ref, win_hbm, lng_ref, lnb_ref, wc_ref, bias_ref, wout_hbm,
             hout_ref, a_ref, hn_ref, gated_ref, win_v, wout_v, sem):
        _load_once([(win_hbm, win_v), (wout_hbm, wout_v)], sem)
        x = h_ref[...]
        _, _, y = _rms_fwd(x, g_ref[...])
        hn = y.astype(BF16)
        hn_ref[...] = hn
        for p in range(n_p):
            a_ref[:, p * w:(p + 1) * w] = _dot(hn, win_v[p])
        _, u, _, _, _, s = _gmlp_core(a_ref[...], lng_ref[...], lnb_ref[...], wc_ref, bias_ref[...], n_chunk, gd)
        gated = (u * s).astype(BF16)
        gated_ref[...] = gated
        hout_ref[...] = x + _dot(gated, wout_v[...])

    row = lambda i: (i, 0)
    const2 = lambda i: (0, 0)
    return pl.pallas_call(
        body, name="gmlp_fwd", grid=(s_len // tm,),
        in_specs=[pl.BlockSpec((tm, d), row), pl.BlockSpec((1, d), const2), ANY,
                  pl.BlockSpec((1, e), const2), pl.BlockSpec((1, e), const2),
                  pl.BlockSpec(wc.shape, lambda i: (0, 0, 0)), pl.BlockSpec((CHUNK, e), const2), ANY],
        out_specs=[pl.BlockSpec((tm, d), row), pl.BlockSpec((tm, 2 * e), row),
                   pl.BlockSpec((tm, d), row), pl.BlockSpec((tm, e), row)],
        out_shape=[jax.ShapeDtypeStruct((s_len, d), F32), jax.ShapeDtypeStruct((s_len, 2 * e), F32),
                   jax.ShapeDtypeStruct((s_len, d), BF16), jax.ShapeDtypeStruct((s_len, e), BF16)],
        scratch_shapes=[pltpu.VMEM(w_in.shape, BF16), pltpu.VMEM(w_out.shape, BF16), pltpu.SemaphoreType.DMA((2,))],
        compiler_params=_params(1),
    )(h, g_mix, w_in, lng, lnb, wc, bias, w_out)


def _gmlp_bwd(h, dh, a, g_mix, w_in, lng, lnb, wc, wct, bias, w_out, seg, tm):
    s_len, d = h.shape
    n_p, _, w = w_in.shape
    e = w_out.shape[0]
    gd = e // GM_GROUPS
    n_chunk = tm // CHUNK
    n_blk = s_len // tm

    def body(h_ref, dh_ref, a_ref, g_ref, win_hbm, lng_ref, lnb_ref, wc_ref, wct_ref, bias_ref, wout_hbm, seg_ref,
             dhin_ref, da_ref, gws_ref, gbs_ref, glng_ref, glnb_ref, gmix_ref, win_v, wout_v, dsum, sem):
        i = pl.program_id(0)
        _load_once([(win_hbm, win_v), (wout_hbm, wout_v)], sem)

        @pl.when(i == 0)
        def _():
            gws_ref[...] = jnp.zeros_like(gws_ref)
            glng_ref[...] = jnp.zeros_like(glng_ref)
            glnb_ref[...] = jnp.zeros_like(glnb_ref)
            gmix_ref[...] = jnp.zeros_like(gmix_ref)
            dsum[...] = jnp.zeros_like(dsum)

        x = h_ref[...]
        dh_v = dh_ref[...]
        g = g_ref[...]
        lng_v = lng_ref[...]
        xhat, r, _ = _rms_fwd(x, g)
        dz_da, u, vhat, rstd, vlb, s = _gmlp_core(a_ref[...], lng_v, lnb_ref[...], wc_ref, bias_ref[...], n_chunk, gd)
        dg = _dot_nt(dh_v.astype(BF16), wout_v[...])
        du = dg * s
        ds = dg * u
        dsb = ds.astype(BF16)
        rows = []
        ds_acc = None
        for ci in range(n_chunk):
            lo, hi = ci * CHUNK, (ci + 1) * CHUNK
            cols = []
            for gi in range(GM_GROUPS):
                d_blk = dsb[lo:hi, gi * gd:(gi + 1) * gd]
                gws_ref[gi] += _dot_nt(d_blk, vlb[lo:hi, gi * gd:(gi + 1) * gd])
                cols.append(_dot(wct_ref[gi], d_blk))
            rows.append(jnp.concatenate(cols, axis=1))
            ds_acc = ds[lo:hi] if ds_acc is None else ds_acc + ds[lo:hi]
        dsum[...] += ds_acc
        dvln = rows[0] if n_chunk == 1 else jnp.concatenate(rows, axis=0)
        glng_ref[...] += jnp.sum(dvln * vhat, axis=0, keepdims=True)
        glnb_ref[...] += jnp.sum(dvln, axis=0, keepdims=True)
        dvhat = dvln * lng_v
        dv = rstd * (dvhat - jnp.mean(dvhat, axis=-1, keepdims=True)
                     - vhat * jnp.mean(dvhat * vhat, axis=-1, keepdims=True))
        da = jnp.concatenate([du, dv], axis=1) * dz_da
        dab = da.astype(BF16)
        da_ref[...] = dab
        dhn = _dot_nt(dab[:, :w], win_v[0])
        for p in range(1, n_p):
            dhn += _dot_nt(dab[:, p * w:(p + 1) * w], win_v[p])
        dx, gg = _rms_bwd(dhn, xhat, r, g)
        gmix_ref[...] += jnp.sum(gg, axis=0, keepdims=True)
        dhin_ref[...] = dh_v + dx

        @pl.when(i == n_blk - 1)
        def _():
            tril = lax.broadcasted_iota(jnp.int32, (CHUNK, CHUNK), 0) >= lax.broadcasted_iota(jnp.int32, (CHUNK, CHUNK), 1)
            for gi in range(GM_GROUPS):
                gws_ref[gi] = jnp.where(tril, gws_ref[gi], 0.0)
            gbs_ref[...] = _dot3_lhs(dsum[...], seg_ref[...])

    row = lambda i: (i, 0)
    const2 = lambda i: (0, 0)
    const3 = lambda i: (0, 0, 0)
    return pl.pallas_call(
        body, name="gmlp_bwd", grid=(n_blk,),
        in_specs=[pl.BlockSpec((tm, d), row), pl.BlockSpec((tm, d), row), pl.BlockSpec((tm, 2 * e), row),
                  pl.BlockSpec((1, d), const2), ANY, pl.BlockSpec((1, e), const2), pl.BlockSpec((1, e), const2),
                  pl.BlockSpec(wc.shape, const3), pl.BlockSpec(wct.shape, const3), pl.BlockSpec((CHUNK, e), const2),
                  ANY, pl.BlockSpec((e, LANES), const2)],
        out_specs=[pl.BlockSpec((tm, d), row), pl.BlockSpec((tm, 2 * e), row), pl.BlockSpec(wc.shape, const3),
                   pl.BlockSpec((CHUNK, LANES), const2), pl.BlockSpec((1, e), const2), pl.BlockSpec((1, e), const2),
                   pl.BlockSpec((1, d), const2)],
        out_shape=[jax.ShapeDtypeStruct((s_len, d), F32), jax.ShapeDtypeStruct((s_len, 2 * e), BF16),
                   jax.ShapeDtypeStruct(wc.shape, F32), jax.ShapeDtypeStruct((CHUNK, LANES), F32),
                   jax.ShapeDtypeStruct((1, e), F32), jax.ShapeDtypeStruct((1, e), F32), jax.ShapeDtypeStruct((1, d), F32)],
        scratch_shapes=[pltpu.VMEM(w_in.shape, BF16), pltpu.VMEM(w_out.shape, BF16), pltpu.VMEM((CHUNK, e), F32),
                        pltpu.SemaphoreType.DMA((2,))],
        compiler_params=_params(1),
    )(h, dh, a, g_mix, w_in, lng, lnb, wc, wct, bias, w_out, seg)


def _shift_down(a, k, fill):
    tm = a.shape[0]
    out = pltpu.roll(a, k, 0)
    rid = lax.broadcasted_iota(jnp.int32, a.shape, 0)
    for j in range(k):
        out = jnp.where(rid == j, fill[8 - k + j:8 - k + j + 1, :], out)
    return out


def _shift_up(a, k, fill):
    tm = a.shape[0]
    out = pltpu.roll(a, tm - k, 0)
    rid = lax.broadcasted_iota(jnp.int32, a.shape, 0)
    for j in range(k):
        out = jnp.where(rid == tm - k + j, fill[j:j + 1, :], out)
    return out


def _ffn_fwd(h, g_norm, wg_all, wu_all, wd_all, layer, conv_w, conv_b, tm):
    s_len, d = h.shape
    n_p = wg_all.shape[0]
    fq = wg_all.shape[3]
    f = n_p * fq

    def body(h_ref, g_ref, wg_hbm, wu_hbm, wd_hbm, cw_ref, cb_ref,
             hout_ref, a_ref, up_ref, hn_ref, hid_ref, wg_v, wu_v, wd_v, carry, sem):
        i = pl.program_id(0)
        _load_once([(wg_hbm.at[:, layer], wg_v), (wu_hbm.at[:, layer], wu_v), (wd_hbm.at[:, layer], wd_v)], sem)

        @pl.when(i == 0)
        def _():
            carry[...] = jnp.zeros_like(carry)

        x = h_ref[...]
        _, _, y = _rms_fwd(x, g_ref[...])
        hn = y.astype(BF16)
        hn_ref[...] = hn
        for p in range(n_p):
            a_ref[:, p * fq:(p + 1) * fq] = _dot(hn, wg_v[p])
            up_ref[:, p * fq:(p + 1) * fq] = _dot(hn, wu_v[p])
        a = a_ref[...]
        prev = carry[...]
        am1 = _shift_down(a, 1, prev)
        am2 = _shift_down(a, 2, prev)
        carry[...] = a[tm - 8:tm, :]
        cw = cw_ref[...]
        ac = cb_ref[...] + am2 * cw[0:1, :]
        ac = ac + am1 * cw[1:2, :]
        ac = ac + a * cw[2:3, :]
        hid = (ac * _sigmoid(ac) * up_ref[...]).astype(BF16)
        hid_ref[...] = hid
        y2 = _dot(hid[:, :fq], wd_v[0])
        for p in range(1, n_p):
            y2 += _dot(hid[:, p * fq:(p + 1) * fq], wd_v[p])
        hout_ref[...] = x + y2

    row = lambda i: (i, 0)
    const2 = lambda i: (0, 0)
    return pl.pallas_call(
        body, name=f"ffn_fwd_{layer}", grid=(s_len // tm,),
        in_specs=[pl.BlockSpec((tm, d), row), pl.BlockSpec((1, d), const2), ANY, ANY, ANY,
                  pl.BlockSpec((8, f), const2), pl.BlockSpec((1, f), const2)],
        out_specs=[pl.BlockSpec((tm, d), row), pl.BlockSpec((tm, f), row), pl.BlockSpec((tm, f), row),
                   pl.BlockSpec((tm, d), row), pl.BlockSpec((tm, f), row)],
        out_shape=[jax.ShapeDtypeStruct((s_len, d), F32), jax.ShapeDtypeStruct((s_len, f), F32),
                   jax.ShapeDtypeStruct((s_len, f), F32), jax.ShapeDtypeStruct((s_len, d), BF16),
                   jax.ShapeDtypeStruct((s_len, f), BF16)],
        scratch_shapes=[pltpu.VMEM((n_p, d, fq), BF16), pltpu.VMEM((n_p, d, fq), BF16), pltpu.VMEM((n_p, fq, d), BF16),
                        pltpu.VMEM((8, f), F32), pltpu.SemaphoreType.DMA((3,))],
        compiler_params=_params(1),
    )(h, g_norm, wg_all, wu_all, wd_all, conv_w, conv_b)


def _ffn_bwd(h, dh, a, up, g_norm, wg_all, wu_all, wd_all, layer, conv_w, conv_b, tm):
    s_len, d = h.shape
    n_p = wg_all.shape[0]
    fq = wg_all.shape[3]
    f = n_p * fq
    n_blk = s_len // tm
    t8 = tm // 8

    def body(h_ref, dh_ref, a_ref, ahalo_ref, up_ref, g_ref, wg_hbm, wu_hbm, wd_hbm, cw_ref, cb_ref,
             dhin_ref, da_ref, dup_ref, gcw_ref, gcb_ref, gn_ref, wg_v, wu_v, wd_v, carry, sem):
        i = pl.program_id(0)
        _load_once([(wg_hbm.at[:, layer], wg_v), (wu_hbm.at[:, layer], wu_v), (wd_hbm.at[:, layer], wd_v)], sem)

        @pl.when(i == 0)
        def _():
            carry[...] = jnp.zeros_like(carry)
            gcw_ref[...] = jnp.zeros_like(gcw_ref)
            gcb_ref[...] = jnp.zeros_like(gcb_ref)
            gn_ref[...] = jnp.zeros_like(gn_ref)

        x = h_ref[...]
        dh_v = dh_ref[...]
        g = g_ref[...]
        xhat, r, _ = _rms_fwd(x, g)
        a = a_ref[...]
        up_v = up_ref[...]
        prev = jnp.where(i == n_blk - 1, 0.0, ahalo_ref[...])
        am1 = _shift_down(a, 1, prev)
        am2 = _shift_down(a, 2, prev)
        cw = cw_ref[...]
        ac = cb_ref[...] + am2 * cw[0:1, :]
        ac = ac + am1 * cw[1:2, :]
        ac = ac + a * cw[2:3, :]
        sg = _sigmoid(ac)
        sil = ac * sg
        dhb = dh_v.astype(BF16)
        dhid = jnp.concatenate([_dot_nt(dhb, wd_v[p]) for p in range(n_p)], axis=1)
        dup = dhid * sil
        dac = dhid * up_v * (sg * (1.0 + ac * (1.0 - sg)))
        gcb_ref[...] += jnp.sum(dac, axis=0, keepdims=True)
        gcw_ref[0:1, :] += jnp.sum(dac * am2, axis=0, keepdims=True)
        gcw_ref[1:2, :] += jnp.sum(dac * am1, axis=0, keepdims=True)
        gcw_ref[2:3, :] += jnp.sum(dac * a, axis=0, keepdims=True)
        nxt = carry[...]
        dp1 = _shift_up(dac, 1, nxt)
        dp2 = _shift_up(dac, 2, nxt)
        carry[...] = dac[0:8, :]
        da = dac * cw[2:3, :] + dp1 * cw[1:2, :] + dp2 * cw[0:1, :]
        dab = da.astype(BF16)
        dupb = dup.astype(BF16)
        da_ref[...] = dab
        dup_ref[...] = dupb
        dhn = _dot_nt(dab[:, :fq], wg_v[0]) + _dot_nt(dupb[:, :fq], wu_v[0])
        for p in range(1, n_p):
            dhn += _dot_nt(dab[:, p * fq:(p + 1) * fq], wg_v[p]) + _dot_nt(dupb[:, p * fq:(p + 1) * fq], wu_v[p])
        dx, gg = _rms_bwd(dhn, xhat, r, g)
        gn_ref[...] += jnp.sum(gg, axis=0, keepdims=True)
        dhin_ref[...] = dh_v + dx

    rev = lambda i: (n_blk - 1 - i, 0)
    halo = lambda i: (jnp.maximum((n_blk - 1 - i) * t8 - 1, 0), 0)
    const2 = lambda i: (0, 0)
    return pl.pallas_call(
        body, name=f"ffn_bwd_{layer}", grid=(n_blk,),
        in_specs=[pl.BlockSpec((tm, d), rev), pl.BlockSpec((tm, d), rev), pl.BlockSpec((tm, f), rev),
                  pl.BlockSpec((8, f), halo), pl.BlockSpec((tm, f), rev), pl.BlockSpec((1, d), const2), ANY, ANY, ANY,
                  pl.BlockSpec((8, f), const2), pl.BlockSpec((1, f), const2)],
        out_specs=[pl.BlockSpec((tm, d), rev), pl.BlockSpec((tm, f), rev), pl.BlockSpec((tm, f), rev),
                   pl.BlockSpec((8, f), const2), pl.BlockSpec((1, f), const2), pl.BlockSpec((1, d), const2)],
        out_shape=[jax.ShapeDtypeStruct((s_len, d), F32), jax.ShapeDtypeStruct((s_len, f), BF16),
                   jax.ShapeDtypeStruct((s_len, f), BF16), jax.ShapeDtypeStruct((8, f), F32),
                   jax.ShapeDtypeStruct((1, f), F32), jax.ShapeDtypeStruct((1, d), F32)],
        scratch_shapes=[pltpu.VMEM((n_p, d, fq), BF16), pltpu.VMEM((n_p, d, fq), BF16), pltpu.VMEM((n_p, fq, d), BF16),
                        pltpu.VMEM((8, f), F32), pltpu.SemaphoreType.DMA((3,))],
        compiler_params=_params(1),
    )(h, dh, a, a, up, g_norm, wg_all, wu_all, wd_all, conv_w, conv_b)


def _even_head_lanes(shape, axis):
    return (lax.broadcasted_iota(jnp.int32, shape, axis) & HEAD_DIM) == 0


def _pair_select(lo, hi, shape):
    return jnp.where(lax.broadcasted_iota(jnp.int32, shape, 1) < HEAD_DIM, lo, hi)


def _causal(row0, col0, shape):
    return row0 + lax.broadcasted_iota(jnp.int32, shape, 0) >= col0 + lax.broadcasted_iota(jnp.int32, shape, 1)


N_SPARE = 3


def _spare_selectors(d, key_side):
    lane = jnp.arange(d)[None, :]
    row = jnp.arange(N_SPARE * LANES)[:, None]
    head, part = row % LANES, row // LANES
    off = N_SPARE if key_side else 0
    sel_a = ((head % 2 == 0) & (lane == LANES * (head // 2) + HEAD_DIM + off + part)).astype(F32)
    sel_b = ((head % 2 == 1) & (lane == LANES * (head // 2) + off + part)).astype(F32)
    sign = -1.0 if key_side else 1.0
    ones_off = 0 if key_side else N_SPARE
    in_pair = jnp.arange(d)[None, :] % LANES
    ones_a = ((in_pair >= HEAD_DIM + ones_off) & (in_pair < HEAD_DIM + ones_off + N_SPARE)).astype(F32)
    ones_b = ((in_pair >= ones_off) & (in_pair < ones_off + N_SPARE)).astype(F32)
    return (sign * sel_a).astype(BF16), (sign * sel_b).astype(BF16), ones_a, ones_b


def _parts(x):
    return jnp.concatenate(_split3(x), axis=1)


def _fox_proj_fwd(h, g_norm, wq, wk, wv, wf, bf, sel_q, sel_k, tm):
    s_len, d = h.shape
    sq_a, sq_b, oq_a, oq_b = sel_q
    sk_a, sk_b, ok_a, ok_b = sel_k

    def body(h_ref, g_ref, wq_hbm, wk_hbm, wv_hbm, wf_ref, bf_ref, sqa_ref, sqb_ref, oqa_ref, oqb_ref,
             ska_ref, skb_ref, oka_ref, okb_ref,
             hn_ref, qa_ref, qb_ref, kat_ref, kbt_ref, va_ref, vb_ref, vat_ref, vbt_ref, z_ref,
             wq_v, wk_v, wv_v, total, sem):
        i = pl.program_id(0)
        _load_once([(wq_hbm, wq_v), (wk_hbm, wk_v), (wv_hbm, wv_v)], sem)

        @pl.when(i == 0)
        def _():
            total[...] = jnp.zeros_like(total)

        x = h_ref[...]
        _, _, y = _rms_fwd(x, g_ref[...])
        hn = y.astype(BF16)
        hn_ref[...] = hn
        z = _dot(hn, wf_ref[...]) + bf_ref[...]
        z_ref[...] = z
        logf = jnp.minimum(z, 0.0) - jnp.log(1.0 + jnp.exp(-jnp.abs(z)))
        tri = (lax.broadcasted_iota(jnp.int32, (tm, tm), 0) >= lax.broadcasted_iota(jnp.int32, (tm, tm), 1))
        cum = _dot3_rhs(jnp.where(tri, 1.0, 0.0).astype(BF16), logf) + total[0:1, :]
        total[...] = jnp.broadcast_to(cum[tm - 1:tm, :], total.shape)
        parts = _parts(cum)

        even = _even_head_lanes((tm, d), 1)
        q = _dot(hn, wq_v[...]) * (HEAD_DIM ** -0.5)
        qa_ref[...] = jnp.where(even, q, _dot(parts, sqa_ref[...]) + oqa_ref[...]).astype(BF16)
        qb_ref[...] = jnp.where(even, _dot(parts, sqb_ref[...]) + oqb_ref[...], q).astype(BF16)
        k = _dot(hn, wk_v[...])
        ka = jnp.where(even, k, _dot(parts, ska_ref[...]) + oka_ref[...])
        kb = jnp.where(even, _dot(parts, skb_ref[...]) + okb_ref[...], k)
        kat_ref[...] = ka.T.astype(BF16)
        kbt_ref[...] = kb.T.astype(BF16)
        v = _dot(hn, wv_v[...])
        va = jnp.where(even, v, oka_ref[...])
        vb = jnp.where(even, okb_ref[...], v)
        va_ref[...] = va.astype(BF16)
        vb_ref[...] = vb.astype(BF16)
        vat_ref[...] = va.T.astype(BF16)
        vbt_ref[...] = vb.T.astype(BF16)

    row = lambda i: (i, 0)
    col = lambda i: (0, i)
    const2 = lambda i: (0, 0)
    sd = jax.ShapeDtypeStruct((s_len, d), BF16)
    ds_ = jax.ShapeDtypeStruct((d, s_len), BF16)
    rs, cs = pl.BlockSpec((tm, d), row), pl.BlockSpec((d, tm), col)
    sel = pl.BlockSpec((N_SPARE * LANES, d), const2)
    one = pl.BlockSpec((1, d), const2)
    return pl.pallas_call(
        body, name="fox_proj_fwd", grid=(s_len // tm,),
        in_specs=[rs, one, ANY, ANY, ANY, pl.BlockSpec((d, LANES), const2), pl.BlockSpec((1, LANES), const2),
                  sel, sel, one, one, sel, sel, one, one],
        out_specs=[rs, rs, rs, cs, cs, rs, rs, cs, cs, pl.BlockSpec((tm, LANES), row)],
        out_shape=[sd, sd, sd, ds_, ds_, sd, sd, ds_, ds_, jax.ShapeDtypeStruct((s_len, LANES), F32)],
        scratch_shapes=[pltpu.VMEM((d, d), BF16), pltpu.VMEM((d, d), BF16), pltpu.VMEM((d, d), BF16),
                        pltpu.VMEM((8, LANES), F32), pltpu.SemaphoreType.DMA((3,))],
        compiler_params=_params(1),
    )(h, g_norm, wq, wk, wv, wf, bf, sq_a, sq_b, oq_a, oq_b, sk_a, sk_b, ok_a, ok_b)


def _spare_cols(x, base):
    xf = x[:, base:base + N_SPARE].astype(F32)
    return xf[:, 0:1] + xf[:, 1:2] + xf[:, 2:3]


def _with_query_term(x, term, base):
    lane = lax.broadcasted_iota(jnp.int32, x.shape, 1)
    hi, mid, lo = _split3(term)
    out = jnp.where(lane == base, hi.astype(F32), x)
    out = jnp.where(lane == base + 1, mid.astype(F32), out)
    out = jnp.where(lane == base + 2, lo.astype(F32), out)
    return jnp.where((lane >= base + N_SPARE) & (lane < base + 2 * N_SPARE), 1.0, out)


def _flash_fwd(qa, qb, kat, kbt, va, vb):
    s_len, d = qa.shape
    sub = ATT_BLOCK
    n_sub = 2 if s_len % (2 * sub) == 0 else 1
    t = n_sub * sub
    w = min(ATT_CHUNK, s_len)
    n_pair = d // LANES
    n_q = s_len // t
    bases = (HEAD_DIM, 0)
    chains = [(r, hh) for r in range(n_sub) for hh in range(2)]

    def body(qa_ref, qb_ref, kat_ref, kbt_ref, va_ref, vb_ref, o_ref, qa2_ref, qb2_ref, qat2_ref, qbt2_ref):
        i = pl.program_id(1)
        q_refs = (qa_ref, qb_ref)
        qs = [q_refs[hh][r * sub:(r + 1) * sub, :] for r, hh in chains]
        kts = (kat_ref, kbt_ref)
        vs = (va_ref, vb_ref)

        def step(kb, carry, masked, width=w):
            off = pl.multiple_of(kb * w, w)
            cols = pl.ds(off, width)
            scores = [_dot(qs[c], kts[hh][:, cols]) for c, (r, hh) in enumerate(chains)]
            probs, stats = [], []
            for c, (r, hh) in enumerate(chains):
                m, _ = carry[c]
                s = scores[c]
                if masked:
                    s = jnp.where(_causal(i * t + r * sub, off, (sub, width)), s, NEG_BIG)
                m_new = jnp.maximum(m, jnp.max(s, axis=1, keepdims=True))
                probs.append(jnp.exp(s - m_new).astype(BF16))
                stats.append((m_new, jnp.exp(m - m_new)))
            return tuple((stats[c][0], carry[c][1] * stats[c][1] + _dot(probs[c], vs[hh][cols, :]))
                         for c, (r, hh) in enumerate(chains))

        init = ((jnp.full((sub, 1), NEG_BIG, F32), jnp.zeros((sub, LANES), F32)),) * len(chains)
        diag = (i * t) // w
        carry = lax.fori_loop(0, diag, lambda kb, c: step(kb, c, False), init)
        carry = step(diag, carry, True)
        for r in range(n_sub):
            outs, q2 = [], []
            for hh in range(2):
                m, acc = carry[2 * r + hh]
                l = acc[:, bases[hh]:bases[hh] + 1]
                outs.append(acc / l)
                term = _spare_cols(qs[2 * r + hh], bases[hh]) - (m + jnp.log(l))
                q2.append(_with_query_term(qs[2 * r + hh].astype(F32), term, bases[hh]))
            rows = slice(r * sub, (r + 1) * sub)
            o_ref[rows, :] = _pair_select(outs[0], outs[1], (sub, LANES))
            qa2_ref[rows, :] = q2[0].astype(BF16)
            qb2_ref[rows, :] = q2[1].astype(BF16)
            qat2_ref[:, rows] = q2[0].T.astype(BF16)
            qbt2_ref[:, rows] = q2[1].T.astype(BF16)

    qblk = pl.BlockSpec((t, LANES), lambda j, i: (i, j))
    qblk_t = pl.BlockSpec((LANES, t), lambda j, i: (j, i))
    whole_t = pl.BlockSpec((LANES, s_len), lambda j, i: (j, 0))
    whole = pl.BlockSpec((s_len, LANES), lambda j, i: (0, j))
    sd = jax.ShapeDtypeStruct((s_len, d), BF16)
    ds_ = jax.ShapeDtypeStruct((d, s_len), BF16)
    return pl.pallas_call(
        body, name="flash_fwd", grid=(n_pair, n_q),
        in_specs=[qblk, qblk, whole_t, whole_t, whole, whole],
        out_specs=[qblk, qblk, qblk, qblk_t, qblk_t],
        out_shape=[jax.ShapeDtypeStruct((s_len, d), F32), sd, sd, ds_, ds_],
        compiler_params=_params(2),
    )(qa, qb, kat, kbt, va, vb)


def _flash_bwd(qa, qb, qat, qbt, kat, kbt, vat, vbt, doa, dob, doat, dobt):
    s_len, d = qa.shape
    sub = ATT_BLOCK
    n_sub = 2 if s_len % (2 * sub) == 0 else 1
    t = n_sub * sub
    w = min(ATT_CHUNK, s_len)
    n_pair = d // LANES
    n_q = s_len // t
    hd = HEAD_DIM
    chains = [(r, hh) for r in range(n_sub) for hh in range(2)]

    def body(qa_ref, qb_ref, qat_ref, qbt_ref, kat_hbm, kbt_hbm, vat_hbm, vbt_hbm, doa_ref, dob_ref, doat_ref, dobt_ref,
             dqt_ref, dkt_ref, dvt_ref, rs_ref, cs_ref,
             kat_v, kbt_v, vat_v, vbt_v, dkt_acc, dvt_acc, cs_acc, sem):
        j = pl.program_id(0)
        i = pl.program_id(1)

        @pl.when(i == 0)
        def _():
            rows = pl.ds(pl.multiple_of(j * LANES, LANES), LANES)
            copies = [pltpu.make_async_copy(src.at[rows, :], dst, sem.at[n]) for n, (src, dst) in enumerate([
                (kat_hbm, kat_v), (kbt_hbm, kbt_v), (vat_hbm, vat_v), (vbt_hbm, vbt_v)])]
            for cp in copies:
                cp.start()
            dkt_acc[...] = jnp.zeros_like(dkt_acc)
            dvt_acc[...] = jnp.zeros_like(dvt_acc)
            cs_acc[...] = jnp.zeros_like(cs_acc)
            for cp in copies:
                cp.wait()

        q_refs, do_refs = (qa_ref, qb_ref), (doa_ref, dob_ref)
        qs = [q_refs[hh][r * sub:(r + 1) * sub, :] for r, hh in chains]
        dos = [do_refs[hh][r * sub:(r + 1) * sub, :] for r, hh in chains]
        own = (slice(0, hd), slice(hd, 2 * hd))
        spare = (slice(hd, hd + 8), slice(0, 8))
        used = (slice(0, hd + 16), slice(0, 2 * hd))
        qts = (qat_ref[used[0], :], qbt_ref[used[1], :])
        dots = (doat_ref[own[0], :], dobt_ref[own[1], :])
        kts, vts = (kat_v, kbt_v), (vat_v, vbt_v)

        def step(kb, carry, masked, width=w):
            off = pl.multiple_of(kb * w, w)
            cols = pl.ds(off, width)
            scores = [_dot(qs[c], kts[hh][:, cols]) for c, (r, hh) in enumerate(chains)]
            dps = [_dot(dos[c], vts[hh][:, cols]) for c, (r, hh) in enumerate(chains)]
            ps, dss = [], []
            for c, (r, hh) in enumerate(chains):
                s = scores[c]
                if masked:
                    s = jnp.where(_causal(i * t + r * sub, off, (sub, width)), s, NEG_BIG)
                p = jnp.exp(s)
                dss.append((p * dps[c]).astype(BF16))
                ps.append(p.astype(BF16))
            out = tuple(carry[c] + _dot_nt(kts[hh][used[hh], cols], dss[c]) for c, (r, hh) in enumerate(chains))
            for hh in range(2):
                p_all = jnp.concatenate([ps[2 * r + hh] for r in range(n_sub)], axis=0)
                ds_all = jnp.concatenate([dss[2 * r + hh] for r in range(n_sub)], axis=0)
                dvt_acc[own[hh], cols] += _dot(dots[hh], p_all)
                with_sums = _dot(qts[hh], ds_all)
                dkt_acc[own[hh], cols] += with_sums[own[hh], :]
                cs_acc[8 * hh:8 * hh + 8, cols] += with_sums[spare[hh], :]
            return out

        diag = (i * t) // w
        init = (jnp.zeros((hd + 16, sub), F32), jnp.zeros((2 * hd, sub), F32)) * n_sub
        carry = lax.fori_loop(0, diag, lambda kb, c: step(kb, c, False), init)
        carry = lax.cond((i * t) % w + t <= w // 2,
                         lambda c: step(diag, c, True, w // 2), lambda c: step(diag, c, True), carry)
        for c, (r, hh) in enumerate(chains):
            at = slice(r * sub, (r + 1) * sub)
            dqt_ref[own[hh], at] = (carry[c][own[hh], :] * (hd ** -0.5)).astype(BF16)
            rs_ref[0, 8 * hh:8 * hh + 8, at] = carry[c][spare[hh], :]

        @pl.when(i == n_q - 1)
        def _():
            dkt_ref[...] = dkt_acc[...].astype(BF16)
            dvt_ref[...] = dvt_acc[...].astype(BF16)
            cs_ref[0] = cs_acc[...]

    qblk = pl.BlockSpec((t, LANES), lambda j, i: (i, j))
    qblk_t = pl.BlockSpec((LANES, t), lambda j, i: (j, i))
    whole_t = pl.BlockSpec((LANES, s_len), lambda j, i: (j, 0))
    ds_ = jax.ShapeDtypeStruct((d, s_len), BF16)
    sums = jax.ShapeDtypeStruct((n_pair, 16, s_len), F32)
    return pl.pallas_call(
        body, name="flash_bwd", grid=(n_pair, n_q),
        in_specs=[qblk, qblk, qblk_t, qblk_t, ANY, ANY, ANY, ANY, qblk, qblk, qblk_t, qblk_t],
        out_specs=[qblk_t, whole_t, whole_t, pl.BlockSpec((1, 16, t), lambda j, i: (j, 0, i)),
                   pl.BlockSpec((1, 16, s_len), lambda j, i: (j, 0, 0))],
        out_shape=[ds_, ds_, ds_, sums, sums],
        scratch_shapes=[pltpu.VMEM((LANES, s_len), BF16), pltpu.VMEM((LANES, s_len), BF16),
                        pltpu.VMEM((LANES, s_len), BF16), pltpu.VMEM((LANES, s_len), BF16),
                        pltpu.VMEM((LANES, s_len), F32), pltpu.VMEM((LANES, s_len), F32),
                        pltpu.VMEM((16, s_len), F32), pltpu.SemaphoreType.DMA((4,))],
        compiler_params=_params(2),
    )(qa, qb, qat, qbt, kat, kbt, vat, vbt, doa, dob, doat, dobt)


def _wgrad_t(at, b, name):
    k, s_len = at.shape
    n = b.shape[1]
    tn, tk, ts = min(n, 1024), min(k, 1024), min(s_len, 1024)

    def body(a_ref, b_ref, o_ref):
        @pl.when(pl.program_id(2) == 0)
        def _():
            o_ref[...] = jnp.zeros_like(o_ref)
        o_ref[...] += _dot(a_ref[...].astype(BF16), b_ref[...].astype(BF16))

    return pl.pallas_call(
        body, name=name, grid=(k // tk, n // tn, s_len // ts),
        in_specs=[pl.BlockSpec((tk, ts), lambda a, b_, c: (a, c)), pl.BlockSpec((ts, tn), lambda a, b_, c: (c, b_))],
        out_specs=pl.BlockSpec((tk, tn), lambda a, b_, c: (a, b_)),
        out_shape=jax.ShapeDtypeStruct((k, n), F32),
        compiler_params=_params(3),
    )(at, b)


def _oproj_bwd(dh, o, wo, seg, sel_q, tm):
    s_len, d = dh.shape
    sq_a, sq_b, _, _ = sel_q

    def body(dh_ref, o_ref, wo_hbm, seg_ref, sqa_ref, sqb_ref, doa_ref, dob_ref, doat_ref, dobt_ref, wo_v, sem):
        _load_once([(wo_hbm, wo_v)], sem)
        do = _dot_nt(dh_ref[...].astype(BF16), wo_v[...])
        parts = _parts(-_dot3_lhs(do * o_ref[...], seg_ref[...]))
        even = _even_head_lanes((tm, d), 1)
        doa = jnp.where(even, do, _dot(parts, sqa_ref[...]))
        dob = jnp.where(even, _dot(parts, sqb_ref[...]), do)
        doa_ref[...] = doa.astype(BF16)
        dob_ref[...] = dob.astype(BF16)
        doat_ref[...] = doa.T.astype(BF16)
        dobt_ref[...] = dob.T.astype(BF16)

    row = lambda i: (i, 0)
    const2 = lambda i: (0, 0)
    rs, cs = pl.BlockSpec((tm, d), row), pl.BlockSpec((d, tm), lambda i: (0, i))
    sel = pl.BlockSpec((N_SPARE * LANES, d), const2)
    sd = jax.ShapeDtypeStruct((s_len, d), BF16)
    ds_ = jax.ShapeDtypeStruct((d, s_len), BF16)
    return pl.pallas_call(
        body, name="oproj_bwd", grid=(s_len // tm,),
        in_specs=[rs, rs, ANY, pl.BlockSpec((d, LANES), const2), sel, sel],
        out_specs=[rs, rs, cs, cs], out_shape=[sd, sd, ds_, ds_],
        scratch_shapes=[pltpu.VMEM((d, d), BF16), pltpu.SemaphoreType.DMA((1,))],
        compiler_params=_params(1),
    )(dh, o, wo, seg, sq_a, sq_b)


def _oproj_fwd(h, o, wo, tm):
    s_len, d = h.shape

    def body(h_ref, o_ref, wo_hbm, hout_ref, wo_v, sem):
        _load_once([(wo_hbm, wo_v)], sem)
        hout_ref[...] = h_ref[...] + _dot(o_ref[...].astype(BF16), wo_v[...])

    row = lambda i: (i, 0)
    return pl.pallas_call(
        body, name="oproj_fwd", grid=(s_len // tm,),
        in_specs=[pl.BlockSpec((tm, d), row), pl.BlockSpec((tm, d), row), ANY],
        out_specs=pl.BlockSpec((tm, d), row),
        out_shape=jax.ShapeDtypeStruct((s_len, d), F32),
        scratch_shapes=[pltpu.VMEM((d, d), BF16), pltpu.SemaphoreType.DMA((1,))],
        compiler_params=_params(1),
    )(h, o, wo)


def _forget_bwd(dcum, z, tm):
    s_len = dcum.shape[0]
    n_blk = s_len // tm

    def body(dc_ref, z_ref, dfl_ref, gb_ref, total):
        i = pl.program_id(0)

        @pl.when(i == 0)
        def _():
            total[...] = jnp.zeros_like(total)
            gb_ref[...] = jnp.zeros_like(gb_ref)

        upper = (lax.broadcasted_iota(jnp.int32, (tm, tm), 0) <= lax.broadcasted_iota(jnp.int32, (tm, tm), 1))
        suffix = _dot3_rhs(jnp.where(upper, 1.0, 0.0).astype(BF16), dc_ref[...]) + total[0:1, :]
        total[...] = jnp.broadcast_to(suffix[0:1, :], total.shape)
        dfl = suffix * _sigmoid(-z_ref[...])
        dfl_ref[...] = dfl
        gb_ref[...] += jnp.sum(dfl, axis=0, keepdims=True)

    rev = lambda i: (n_blk - 1 - i, 0)
    return pl.pallas_call(
        body, name="forget_bwd", grid=(n_blk,),
        in_specs=[pl.BlockSpec((tm, LANES), rev), pl.BlockSpec((tm, LANES), rev)],
        out_specs=[pl.BlockSpec((tm, LANES), rev), pl.BlockSpec((1, LANES), lambda i: (0, 0))],
        out_shape=[jax.ShapeDtypeStruct((s_len, LANES), F32), jax.ShapeDtypeStruct((1, LANES), F32)],
        scratch_shapes=[pltpu.VMEM((8, LANES), F32)],
        compiler_params=_params(1),
    )(dcum, z)


def _fox_proj_bwd(h, dh, dqt, dkt, dvt, dfl, g_norm, wq, wk, wv, wf, tm):
    s_len, d = h.shape

    def body(h_ref, dh_ref, dqt_ref, dkt_ref, dvt_ref, dfl_ref, g_ref, wq_hbm, wk_hbm, wv_hbm, wf_ref,
             dhin_ref, dflb_ref, gn_ref, wq_v, wk_v, wv_v, sem):
        _load_once([(wq_hbm, wq_v), (wk_hbm, wk_v), (wv_hbm, wv_v)], sem)

        @pl.when(pl.program_id(0) == 0)
        def _():
            gn_ref[...] = jnp.zeros_like(gn_ref)

        g = g_ref[...]
        xhat, r, _ = _rms_fwd(h_ref[...], g)
        dflb = dfl_ref[...].astype(BF16)
        dflb_ref[...] = dflb
        from_qkv = (_dot(wq_v[...], dqt_ref[...]) + _dot(wk_v[...], dkt_ref[...])
                    + _dot(wv_v[...], dvt_ref[...]))
        dhn = _dot_nt(dflb, wf_ref[...]) + from_qkv.T
        dx, gg = _rms_bwd(dhn, xhat, r, g)
        gn_ref[...] += jnp.sum(gg, axis=0, keepdims=True)
        dhin_ref[...] = dh_ref[...] + dx

    row = lambda i: (i, 0)
    const2 = lambda i: (0, 0)
    rs = pl.BlockSpec((tm, d), row)
    cs = pl.BlockSpec((d, tm), lambda i: (0, i))
    return pl.pallas_call(
        body, name="fox_proj_bwd", grid=(s_len // tm,),
        in_specs=[rs, rs, cs, cs, cs, pl.BlockSpec((tm, LANES), row), pl.BlockSpec((1, d), const2), ANY, ANY, ANY,
                  pl.BlockSpec((d, LANES), const2)],
        out_specs=[rs, pl.BlockSpec((tm, LANES), row), pl.BlockSpec((1, d), const2)],
        out_shape=[jax.ShapeDtypeStruct((s_len, d), F32), jax.ShapeDtypeStruct((s_len, LANES), BF16),
                   jax.ShapeDtypeStruct((1, d), F32)],
        scratch_shapes=[pltpu.VMEM((d, d), BF16), pltpu.VMEM((d, d), BF16), pltpu.VMEM((d, d), BF16),
                        pltpu.SemaphoreType.DMA((3,))],
        compiler_params=_params(1),
    )(h, dh, dqt, dkt, dvt, dfl, g_norm, wq, wk, wv, wf)


def _loss_head(h, target, g_final, tm):
    s_len, d = h.shape
    n_blk = s_len // tm

    def body(h_ref, t_ref, g_ref, dh_ref, loss_ref, gg_ref, sq):
        i = pl.program_id(0)

        @pl.when(i == 0)
        def _():
            sq[...] = jnp.zeros_like(sq)
            gg_ref[...] = jnp.zeros_like(gg_ref)

        g = g_ref[...]
        xhat, r, y = _rms_fwd(h_ref[...], g)
        err = y - t_ref[...]
        sq[...] += jnp.sum(err * err, axis=0, keepdims=True)
        dx, gg = _rms_bwd(err * (1.0 / d), xhat, r, g)
        gg_ref[...] += jnp.sum(gg, axis=0, keepdims=True)
        dh_ref[...] = dx

        @pl.when(i == n_blk - 1)
        def _():
            loss_ref[...] = jnp.broadcast_to(jnp.sum(sq[...], axis=1, keepdims=True) * (0.5 / d), loss_ref.shape)

    row = lambda i: (i, 0)
    const2 = lambda i: (0, 0)
    return pl.pallas_call(
        body, name="loss_head", grid=(n_blk,),
        in_specs=[pl.BlockSpec((tm, d), row), pl.BlockSpec((tm, d), row), pl.BlockSpec((1, d), const2)],
        out_specs=[pl.BlockSpec((tm, d), row), pl.BlockSpec((1, LANES), const2), pl.BlockSpec((1, d), const2)],
        out_shape=[jax.ShapeDtypeStruct((s_len, d), F32), jax.ShapeDtypeStruct((1, LANES), F32),
                   jax.ShapeDtypeStruct((1, d), F32)],
        scratch_shapes=[pltpu.VMEM((1, d), F32)],
        compiler_params=_params(1),
    )(h, target, g_final)


def _wgrad(x, dy, n_piece, name):
    s_len, k = x.shape
    n = dy.shape[1]
    tn = min(n // n_piece, 1024)
    tk = min(k, 1024)
    ts = min(s_len, 1024)
    per_piece = (n // n_piece) // tn

    def body(x_ref, dy_ref, o_ref):
        @pl.when(pl.program_id(2) == 0)
        def _():
            o_ref[...] = jnp.zeros_like(o_ref)
        o_ref[0] += _dot_tn(x_ref[...].astype(BF16), dy_ref[...].astype(BF16))

    return pl.pallas_call(
        body, name=name, grid=(k // tk, n // tn, s_len // ts),
        in_specs=[pl.BlockSpec((ts, tk), lambda a, b, c: (c, a)), pl.BlockSpec((ts, tn), lambda a, b, c: (c, b))],
        out_specs=pl.BlockSpec((1, tk, tn), lambda a, b, c: (b // per_piece, a, b % per_piece)),
        out_shape=jax.ShapeDtypeStruct((n_piece, k, n // n_piece), F32),
        compiler_params=_params(3),
    )(x, dy)


def _pair_sum(g, recv, core, name):
    n_piece, rows, c = g.shape
    half = rows // 2
    tr = min(half, 512)
    nb = half // tr

    def body(core_ref, g_ref, r_ref, o_ref, ob_ref):
        total = g_ref[...] + r_ref[...]
        o_ref[...] = total
        ob_ref[...] = total.astype(BF16)

    blk = pl.BlockSpec((1, tr, c), lambda p, i, core_ref: (p, i, 0))
    return pl.pallas_call(
        body, name=name,
        out_shape=[jax.ShapeDtypeStruct((n_piece, half, c), F32), jax.ShapeDtypeStruct((n_piece, half, c), BF16)],
        grid_spec=pltpu.PrefetchScalarGridSpec(
            num_scalar_prefetch=1, grid=(n_piece, nb),
            in_specs=[pl.BlockSpec((1, tr, c), lambda p, i, core_ref: (p, core_ref[0] * nb + i, 0)), blk],
            out_specs=[blk, blk]),
        compiler_params=_params(2),
    )(core, g, recv)


def _chip_sum(halves, recv, chip, name):
    _, h, c = halves.shape
    tr = min(h, 512)

    def body(chip_ref, own_ref, r_ref, o_ref):
        o_ref[...] = ((own_ref[0] + r_ref[0].astype(F32)) + r_ref[1].astype(F32)) + r_ref[2].astype(F32)

    return pl.pallas_call(
        body, name=name, out_shape=jax.ShapeDtypeStruct((h, c), F32),
        grid_spec=pltpu.PrefetchScalarGridSpec(
            num_scalar_prefetch=1, grid=(h // tr,),
            in_specs=[pl.BlockSpec((1, tr, c), lambda i, chip_ref: (chip_ref[0], i, 0)),
                      pl.BlockSpec((3, tr, c), lambda i, chip_ref: (0, i, 0))],
            out_specs=pl.BlockSpec((tr, c), lambda i, chip_ref: (i, 0))),
        compiler_params=_params(1),
    )(chip, halves, recv)


def _adamw_math(w, m, v, g):
    m_new = ADAM_B1 * m + (1.0 - ADAM_B1) * g
    v_new = ADAM_B2 * v + (1.0 - ADAM_B2) * (g * g)
    m_hat = m_new / (1.0 - ADAM_B1 ** ADAM_STEP)
    v_hat = v_new / (1.0 - ADAM_B2 ** ADAM_STEP)
    return -ADAM_LR * (m_hat / (jnp.sqrt(v_hat) + ADAM_EPS) + ADAM_WD * w), m_new, v_new


def _adamw(w, m, v, g, name):
    rows, c = w.shape
    tr = min(rows, 256)

    def body(w_ref, m_ref, v_ref, g_ref, d_ref, mo_ref, vo_ref):
        d_ref[...], mo_ref[...], vo_ref[...] = _adamw_math(w_ref[...], m_ref[...], v_ref[...], g_ref[...])

    spec = pl.BlockSpec((tr, c), lambda i: (i, 0))
    shape = jax.ShapeDtypeStruct((rows, c), F32)
    return pl.pallas_call(
        body, name=name, grid=(rows // tr,),
        in_specs=[spec] * 4, out_specs=[spec] * 3, out_shape=[shape] * 3,
        compiler_params=_params(1),
    )(w, m, v, g)


def _adamw_halves(w, m, v, g_own, g_other, core, name):
    rows, c = w.shape
    half = rows // 2
    tr = min(half, 256)
    nb = half // tr

    def body(core_ref, w_ref, m_ref, v_ref, own_ref, other_ref, g_ref, d_ref, mo_ref, vo_ref):
        mine = (pl.program_id(0) // nb) == core_ref[0]
        g = jnp.where(mine, own_ref[...], other_ref[...])
        g_ref[...] = g
        d_ref[...], mo_ref[...], vo_ref[...] = _adamw_math(w_ref[...], m_ref[...], v_ref[...], g)

    spec = pl.BlockSpec((tr, c), lambda i, core_ref: (i, 0))
    own = pl.BlockSpec((tr, c), lambda i, core_ref: (jnp.clip(i - core_ref[0] * nb, 0, nb - 1), 0))
    other = pl.BlockSpec((tr, c), lambda i, core_ref: (jnp.clip(i - (1 - core_ref[0]) * nb, 0, nb - 1), 0))
    shape = jax.ShapeDtypeStruct((rows, c), F32)
    return pl.pallas_call(
        body, name=name, out_shape=[shape] * 4,
        grid_spec=pltpu.PrefetchScalarGridSpec(
            num_scalar_prefetch=1, grid=(rows // tr,),
            in_specs=[spec, spec, spec, own, other], out_specs=[spec] * 4),
        compiler_params=_params(1),
    )(core, w, m, v, g_own, g_other)


def _place():
    x, y, c = lax.axis_index("x"), lax.axis_index("y"), lax.axis_index("c")
    chips = [(1 - x, y), (x, 1 - y), (1 - x, 1 - y)]
    return x, y, c, chips


def _half_of(ref, half, which):
    start = which * half
    if half % 8 == 0:
        start = pl.multiple_of(start, 8)
    return ref.at[pl.ds(start, half)]


def _gather_copies(ins, outs, send_sems, recv_sems):
    x, y, c, chips = _place()
    mine = 2 * x + y
    copies = []
    for k, (src, dst) in enumerate(zip(ins, outs)):
        half = src.shape[0] // 2
        copies.append(pltpu.make_async_remote_copy(
            src_ref=src, dst_ref=dst.at[mine], send_sem=send_sems.at[4 * k + 3], recv_sem=recv_sems.at[4 * k + 3],
            device_id=(x, y, 1 - c), device_id_type=MESH))
        for j, (tx, ty) in enumerate(chips):
            copies.append(pltpu.make_async_remote_copy(
                src_ref=_half_of(src, half, c), dst_ref=_half_of(dst.at[mine], half, c),
                send_sem=send_sems.at[4 * k + j], recv_sem=recv_sems.at[4 * k + j],
                device_id=(tx, ty, c), device_id_type=MESH))
    return copies


def _gather_ici(shards):
    n = len(shards)

    def body(*refs):
        copies = _gather_copies(refs[:n], refs[n:2 * n], refs[2 * n], refs[2 * n + 1])
        for cp in copies:
            cp.start()
        for cp in copies:
            cp.wait()

    return pl.pallas_call(
        body, name="weights_gather_ici",
        in_specs=[ANY] * n, out_specs=[ANY] * n,
        out_shape=[jax.ShapeDtypeStruct((4,) + s.shape, s.dtype) for s in shards],
        scratch_shapes=[pltpu.SemaphoreType.DMA((4 * n,)), pltpu.SemaphoreType.DMA((4 * n,))],
    )(*shards)


HBM_SPEC = pl.BlockSpec(memory_space=pltpu.HBM)
SEM_SPEC = pl.BlockSpec(memory_space=pltpu.SEMAPHORE)
IN_FLIGHT = pltpu.SideEffectType.DATAFLOW_SIDE_EFFECTING


def _gather_ici_start(shards, after):
    n = len(shards)

    def body(*refs):
        for cp in _gather_copies(refs[:n], refs[n:2 * n], refs[2 * n + 1], refs[2 * n + 2]):
            cp.start()
        token = refs[-1]
        token[...] = jnp.zeros_like(token)

    lands = [lax.empty((4,) + s.shape, s.dtype) for s in shards]
    out = pl.pallas_call(
        body, name="weights_gather_start",
        out_shape=(pltpu.SemaphoreType.DMA((4 * n,)), pltpu.SemaphoreType.DMA((4 * n,)),
                   *[pltpu.HBM(s.shape, s.dtype) for s in shards], *[pltpu.HBM(l.shape, l.dtype) for l in lands],
                   jax.ShapeDtypeStruct((8, LANES), F32)),
        in_specs=[HBM_SPEC] * (2 * n) + [ANY],
        out_specs=(SEM_SPEC, SEM_SPEC, *[HBM_SPEC] * (2 * n), pl.BlockSpec(memory_space=pltpu.VMEM)),
        input_output_aliases={k: 2 + k for k in range(2 * n)},
        compiler_params=pltpu.CompilerParams(has_side_effects=IN_FLIGHT),
    )(*[pltpu.with_memory_space_constraint(a, pltpu.HBM) for a in list(shards) + lands], after)
    return out[0], out[1], out[2:2 + n], out[2 + n:2 + 2 * n], out[-1]


def _gather_ici_wait(send_sems, recv_sems, sources, lands, after):
    n = len(sources)

    def body(*refs):
        for cp in _gather_copies(refs[:n], refs[n:2 * n], refs[2 * n], refs[2 * n + 1]):
            cp.wait_send()
            cp.wait_recv()

    out = pl.pallas_call(
        body, name="weights_gather_wait",
        out_shape=[pltpu.HBM(a.shape, a.dtype) for a in list(sources) + list(lands)],
        in_specs=[HBM_SPEC] * (2 * n) + [SEM_SPEC, SEM_SPEC, ANY], out_specs=[HBM_SPEC] * (2 * n),
        input_output_aliases={k: k for k in range(2 * n)},
        compiler_params=pltpu.CompilerParams(has_side_effects=IN_FLIGHT),
    )(*sources, *lands, send_sems, recv_sems, after)
    return out[n:]


def _gather_pair(gathered):
    n = len(gathered)

    def body(*refs):
        outs = refs[n:2 * n]
        send_sems, recv_sems = refs[2 * n:]
        x, y, c, chips = _place()
        sends = []
        for k in range(n):
            half = gathered[k].shape[1] // 2
            for j, (tx, ty) in enumerate(chips):
                piece = _half_of(outs[k].at[2 * tx + ty], half, c)
                sends.append(pltpu.make_async_remote_copy(
                    src_ref=piece, dst_ref=piece, send_sem=send_sems.at[k, j], recv_sem=recv_sems.at[k, j],
                    device_id=(x, y, 1 - c), device_id_type=MESH))
        for cp in sends:
            cp.start()
        for cp in sends:
            cp.wait()

    return pl.pallas_call(
        body, name="weights_gather_pair",
        in_specs=[ANY] * n, out_specs=[ANY] * n,
        out_shape=[jax.ShapeDtypeStruct(g.shape, g.dtype) for g in gathered],
        input_output_aliases={k: k for k in range(n)},
        scratch_shapes=[pltpu.SemaphoreType.DMA((n, 3)), pltpu.SemaphoreType.DMA((n, 3))],
    )(*gathered)


def _pair_exchange(grads):
    n = len(grads)

    def body(*refs):
        ins, outs = refs[:n], refs[n:2 * n]
        send_sems, recv_sems = refs[2 * n:]
        x, y, c, _ = _place()
        copies = []
        for k in range(n):
            half = grads[k].shape[1] // 2
            other = ins[k].at[:, pl.ds(pl.multiple_of((1 - c) * half, 8), half), :]
            copies.append(pltpu.make_async_remote_copy(
                src_ref=other, dst_ref=outs[k], send_sem=send_sems.at[k], recv_sem=recv_sems.at[k],
                device_id=(x, y, 1 - c), device_id_type=MESH))
        for cp in copies:
            cp.start()
        for cp in copies:
            cp.wait()

    return pl.pallas_call(
        body, name="grads_pair_exchange",
        in_specs=[ANY] * n, out_specs=[ANY] * n,
        out_shape=[jax.ShapeDtypeStruct((4, g.shape[1] // 2, g.shape[2]), F32) for g in grads],
        scratch_shapes=[pltpu.SemaphoreType.DMA((n,)), pltpu.SemaphoreType.DMA((n,))],
    )(*grads)


def _chip_scatter(halves):
    n = len(halves)

    def body(*refs):
        ins, outs = refs[:n], refs[n:2 * n]
        send_sems, recv_sems = refs[2 * n:]
        x, y, c, chips = _place()
        sends = []
        for k in range(n):
            for j, (tx, ty) in enumerate(chips):
                sends.append(pltpu.make_async_remote_copy(
                    src_ref=ins[k].at[2 * tx + ty], dst_ref=outs[k].at[j], send_sem=send_sems.at[k, j],
                    recv_sem=recv_sems.at[k, j], device_id=(tx, ty, c), device_id_type=MESH))
        for cp in sends:
            cp.start()
        for cp in sends:
            cp.wait()

    return pl.pallas_call(
        body, name="grads_chip_scatter",
        in_specs=[ANY] * n, out_specs=[ANY] * n,
        out_shape=[jax.ShapeDtypeStruct((3,) + hv.shape[1:], hv.dtype) for hv in halves],
        scratch_shapes=[pltpu.SemaphoreType.DMA((n, 3)), pltpu.SemaphoreType.DMA((n, 3))],
    )(*halves)


def _pair_share(finals):
    n = len(finals)

    def body(*refs):
        ins, outs = refs[:n], refs[n:2 * n]
        send_sems, recv_sems = refs[2 * n:]
        x, y, c, _ = _place()
        copies = [pltpu.make_async_remote_copy(
            src_ref=ins[k], dst_ref=outs[k], send_sem=send_sems.at[k], recv_sem=recv_sems.at[k],
            device_id=(x, y, 1 - c), device_id_type=MESH) for k in range(n)]
        for cp in copies:
            cp.start()
        for cp in copies:
            cp.wait()

    return pl.pallas_call(
        body, name="grads_pair_share",
        in_specs=[ANY] * n, out_specs=[ANY] * n,
        out_shape=[jax.ShapeDtypeStruct(fv.shape, F32) for fv in finals],
        scratch_shapes=[pltpu.SemaphoreType.DMA((n,)), pltpu.SemaphoreType.DMA((n,))],
    )(*finals)


def _small_all_reduce(buf):
    rows, c_ = buf.shape

    def body(in_ref, out_ref, pair_buf, slots, send_sems, recv_sems):
        x, y, c, chips = _place()
        mine = 2 * x + y
        pair = pltpu.make_async_remote_copy(
            src_ref=in_ref, dst_ref=pair_buf, send_sem=send_sems.at[0], recv_sem=recv_sems.at[0],
            device_id=(x, y, 1 - c), device_id_type=MESH)
        pair.start()
        pair.wait()
        slots[mine] = in_ref[...] + pair_buf[...]
        sends = [pltpu.make_async_remote_copy(
            src_ref=slots.at[mine], dst_ref=slots.at[mine], send_sem=send_sems.at[1 + j], recv_sem=recv_sems.at[1 + j],
            device_id=(tx, ty, c), device_id_type=MESH) for j, (tx, ty) in enumerate(chips)]
        for cp in sends:
            cp.start()
        for j, (tx, ty) in enumerate(chips):
            pltpu.make_async_remote_copy(
                src_ref=slots.at[mine], dst_ref=slots.at[2 * tx + ty], send_sem=send_sems.at[1 + j],
                recv_sem=recv_sems.at[1 + j], device_id=(tx, ty, c), device_id_type=MESH).wait()
        out_ref[...] = ((slots[0] + slots[1]) + slots[2]) + slots[3]

    vm = pl.BlockSpec(memory_space=pltpu.VMEM)
    return pl.pallas_call(
        body, name="small_all_reduce", in_specs=[vm], out_specs=vm,
        out_shape=jax.ShapeDtypeStruct((rows, c_), F32),
        scratch_shapes=[pltpu.VMEM((rows, c_), F32), pltpu.VMEM((4, rows, c_), F32),
                        pltpu.SemaphoreType.DMA((4,)), pltpu.SemaphoreType.DMA((4,))],
        compiler_params=pltpu.CompilerParams(vmem_limit_bytes=VMEM_LIMIT_V7X),
    )(buf)


def _reduce_scatter(grads):
    core = lax.axis_index("c").astype(jnp.int32).reshape(1)
    chip = (2 * lax.axis_index("x") + lax.axis_index("y")).astype(jnp.int32).reshape(1)
    recv = _pair_exchange(grads)
    halves = [_pair_sum(g, r, core, f"pair_sum_{k}") for k, (g, r) in enumerate(zip(grads, recv))]
    recv = _chip_scatter([hb for _, hb in halves])
    finals = [_chip_sum(hv, r, chip, f"chip_sum_{k}") for k, ((hv, _), r) in enumerate(zip(halves, recv))]
    return list(zip(finals, _pair_share(finals))), core


PACK_COLS = 1024


def _pack(arrays):
    flat = jnp.concatenate([a.reshape(-1).astype(F32) for a in arrays])
    rows = -(-flat.shape[0] // PACK_COLS)
    rows = -(-rows // 8) * 8
    return jnp.pad(flat, (0, rows * PACK_COLS - flat.shape[0])).reshape(rows, PACK_COLS)


def _unpack(buf, shapes):
    flat = buf.reshape(-1)
    out, at = [], 0
    for shp in shapes:
        size = math.prod(shp)
        out.append(flat[at:at + size].reshape(shp))
        at += size
    return out


def kernel(x, mix_norm_g, ffn_norm_g, gm_w_in, gm_ln_g, gm_ln_b, gm_w_s, gm_b_s, gm_w_out, fox_w_qkvf, fox_b_f, fox_w_o, ffn_w_gate, ffn_w_up, ffn_conv_w, ffn_conv_b, ffn_w_down, final_norm_g, loss_target, m_mix_norm_g, m_ffn_norm_g, m_gm_w_in, m_gm_ln_g, m_gm_ln_b, m_gm_w_s, m_gm_b_s, m_gm_w_out, m_fox_w_qkvf, m_fox_b_f, m_fox_w_o, m_ffn_w_gate, m_ffn_w_up, m_ffn_conv_w, m_ffn_conv_b, m_ffn_w_down, m_final_norm_g, v_mix_norm_g, v_ffn_norm_g, v_gm_w_in, v_gm_ln_g, v_gm_ln_b, v_gm_w_s, v_gm_b_s, v_gm_w_out, v_fox_w_qkvf, v_fox_b_f, v_fox_w_o, v_ffn_w_gate, v_ffn_w_up, v_ffn_conv_w, v_ffn_conv_b, v_ffn_w_down, v_final_norm_g):
    _, s_len, d = x.shape
    e = gm_ln_g.shape[1]
    f = ffn_conv_b.shape[1]
    n_head = fox_b_f.shape[1]
    n_pair = n_head // 2
    gd = e // GM_GROUPS
    qkvf_cols = fox_w_qkvf.shape[2]
    assert d == n_head * HEAD_DIM and d % (2 * LANES) == 0 and s_len % 512 == 0 and gd % LANES == 0
    assert gm_w_s.shape[2] == CHUNK and 4 * qkvf_cols == 3 * d + n_head
    tm = 256
    h0 = x[0]
    target = loss_target[0]

    w_in, w_out4 = _gather_pair(_gather_ici([gm_w_in[0].astype(BF16), gm_w_out[0].astype(BF16)]))
    send_sems, recv_sems, sources, lands, token = _gather_ici_start([
        fox_w_qkvf[0].astype(BF16), fox_w_o[0].astype(BF16), ffn_w_gate.astype(BF16), ffn_w_up.astype(BF16),
        ffn_w_down.astype(BF16), ffn_conv_w], after=w_in)
    w_out = w_out4.reshape(e, d)
    bf_pad = jnp.pad(fox_b_f, ((0, 0), (0, LANES - n_head)))

    tril = jnp.tril(jnp.ones((CHUNK, CHUNK), bool))
    wc = jnp.where(tril[None], gm_w_s[0], 0.0).astype(BF16)
    wct = jnp.transpose(wc, (0, 2, 1))
    bias = jnp.repeat(gm_b_s[0].T, gd, axis=1)
    seg_groups = (jnp.arange(e)[:, None] // gd == jnp.arange(LANES)[None, :]).astype(BF16)
    seg_heads = (jnp.arange(d)[:, None] // HEAD_DIM == jnp.arange(LANES)[None, :]).astype(BF16)
    sel_q = _spare_selectors(d, key_side=False)
    sel_k = _spare_selectors(d, key_side=True)

    h1, a0, hn0, gated0 = _gmlp_fwd(h0, mix_norm_g[0:1] + token[0:1, 0:1], w_in, gm_ln_g, gm_ln_b, wc, bias, w_out, tm)
    qkvf4, wo4, wg_all, wu_all, wd_all, cw4 = _gather_pair(_gather_ici_wait(send_sems, recv_sems, sources, lands, h1))
    qkvf = jnp.transpose(qkvf4, (1, 0, 2)).reshape(d, 4 * qkvf_cols)
    wq, wk, wv = qkvf[:, :d], qkvf[:, d:2 * d], qkvf[:, 2 * d:3 * d]
    wf = jnp.pad(qkvf[:, 3 * d:], ((0, 0), (0, LANES - n_head)))
    wo = wo4.reshape(d, d)
    conv_w_full = jnp.transpose(cw4, (1, 2, 0, 3)).reshape(2, 3, f)
    conv_w8 = jnp.pad(conv_w_full, ((0, 0), (0, 5), (0, 0)))
    h2, fa0, fup0, fhn0, fhid0 = _ffn_fwd(h1, ffn_norm_g[0:1], wg_all, wu_all, wd_all, 0, conv_w8[0], ffn_conv_b[0:1], tm)
    (hn1, qa, qb, kat, kbt, va, vb, vat, vbt, z_f) = _fox_proj_fwd(
        h2, mix_norm_g[1:2], wq, wk, wv, wf, bf_pad, sel_q, sel_k, tm)
    o, qa2, qb2, qat2, qbt2 = _flash_fwd(qa, qb, kat, kbt, va, vb)
    h3 = _oproj_fwd(h2, o, wo, tm)
    h4, fa1, fup1, fhn1, fhid1 = _ffn_fwd(h3, ffn_norm_g[1:2], wg_all, wu_all, wd_all, 1, conv_w8[1], ffn_conv_b[1:2], tm)

    dh4, loss_part, g_final = _loss_head(h4, target, final_norm_g.reshape(1, d), tm)
    dh3, da1, dup1, gcw1, gcb1, gfn1 = _ffn_bwd(h3, dh4, fa1, fup1, ffn_norm_g[1:2], wg_all, wu_all, wd_all, 1,
                                                conv_w8[1], ffn_conv_b[1:2], tm)
    g_gate1 = _wgrad(fhn1, da1, 4, "wgrad_gate_1")
    g_up1 = _wgrad(fhn1, dup1, 4, "wgrad_up_1")
    g_down1 = _wgrad(fhid1, dh4, 1, "wgrad_down_1").reshape(4, f // 4, d)

    doa, dob, doat, dobt = _oproj_bwd(dh3, o, wo, seg_heads, sel_q, tm)
    g_wo = _wgrad(o, dh3, 1, "wgrad_wo").reshape(4, d // 4, d)
    dqt, dkt, dvt, row_sums, col_sums = _flash_bwd(qa2, qb2, qat2, qbt2, kat, kbt, vat, vbt, doa, dob, doat, dobt)
    sums = row_sums[:, 0::8, :] - col_sums[:, N_SPARE::8, :]
    dcum = jnp.pad(sums.reshape(n_head, s_len).T, ((0, 0), (0, LANES - n_head)))
    dfl, g_bf = _forget_bwd(dcum, z_f, tm)
    dh2, dflb, gmn1 = _fox_proj_bwd(h2, dh3, dqt, dkt, dvt, dfl, mix_norm_g[1:2], wq, wk, wv, wf, tm)
    g_q = _wgrad_t(dqt, hn1, "wgrad_q").T
    g_k = _wgrad_t(dkt, hn1, "wgrad_k").T
    g_v = _wgrad_t(dvt, hn1, "wgrad_v").T
    g_f = _wgrad(hn1, dflb, 1, "wgrad_f")[0][:, :n_head]
    g_qkvf = jnp.concatenate([g_q, g_k, g_v, g_f], axis=1).reshape(d, 4, qkvf_cols).transpose(1, 0, 2)

    dh1, da0f, dup0, gcw0, gcb0, gfn0 = _ffn_bwd(h1, dh2, fa0, fup0, ffn_norm_g[0:1], wg_all, wu_all, wd_all, 0,
                                                 conv_w8[0], ffn_conv_b[0:1], tm)
    g_gate0 = _wgrad(fhn0, da0f, 4, "wgrad_gate_0")
    g_up0 = _wgrad(fhn0, dup0, 4, "wgrad_up_0")
    g_down0 = _wgrad(fhid0, dh2, 1, "wgrad_down_0").reshape(4, f // 4, d)

    dh0, da0, g_ws, g_bs_t, g_lng, g_lnb, gmn0 = _gmlp_bwd(
        h0, dh1, a0, mix_norm_g[0:1], w_in, gm_ln_g, gm_ln_b, wc, wct, bias, w_out, seg_groups, tm)
    g_win = _wgrad(hn0, da0, 4, "wgrad_gm_in")
    g_wout = _wgrad(gated0, dh1, 1, "wgrad_gm_out").reshape(4, e // 4, d)

    big, core = _reduce_scatter([g_win, g_wout, g_qkvf, g_wo, g_gate0, g_gate1, g_up0, g_up1, g_down0, g_down1])
    r_win, r_wout, r_qkvf, r_wo, r_gate0, r_gate1, r_up0, r_up1, r_down0, r_down1 = big

    small = [jnp.concatenate([gmn0, gmn1]), jnp.concatenate([gfn0, gfn1]), g_lng, g_lnb, g_ws[None],
             g_bs_t[:, :GM_GROUPS].T[None], g_bf[:, :n_head], jnp.stack([gcw0[:3], gcw1[:3]]),
             jnp.concatenate([gcb0, gcb1]), g_final.reshape(d), loss_part[0, :1]]
    small_shapes = [a.shape for a in small]
    reduced = _unpack(_small_all_reduce(_pack(small)), small_shapes)
    (r_mix, r_ffn, r_lng, r_lnb, r_ws, r_bs, r_bf, r_cw_full, r_cb, r_final, r_loss) = reduced
    chip = 2 * lax.axis_index("x") + lax.axis_index("y")
    r_cw = lax.dynamic_slice_in_dim(r_cw_full, chip * (f // 4), f // 4, axis=2)

    def update_big(name, w, m, v, per_layer):
        parts = [_adamw_halves(w[l], m[l], v[l], own, other, core, f"adamw_{name}_{l}")
                 for l, (own, other) in enumerate(per_layer)]
        return tuple(jnp.stack([p[i] for p in parts]) for i in range(4))

    res = {}
    res["gm_w_in"] = update_big("gm_w_in", gm_w_in, m_gm_w_in, v_gm_w_in, [r_win])
    res["gm_w_out"] = update_big("gm_w_out", gm_w_out, m_gm_w_out, v_gm_w_out, [r_wout])
    res["fox_w_qkvf"] = update_big("fox_w_qkvf", fox_w_qkvf, m_fox_w_qkvf, v_fox_w_qkvf, [r_qkvf])
    res["fox_w_o"] = update_big("fox_w_o", fox_w_o, m_fox_w_o, v_fox_w_o, [r_wo])
    res["ffn_w_gate"] = update_big("ffn_w_gate", ffn_w_gate, m_ffn_w_gate, v_ffn_w_gate, [r_gate0, r_gate1])
    res["ffn_w_up"] = update_big("ffn_w_up", ffn_w_up, m_ffn_w_up, v_ffn_w_up, [r_up0, r_up1])
    res["ffn_w_down"] = update_big("ffn_w_down", ffn_w_down, m_ffn_w_down, v_ffn_w_down, [r_down0, r_down1])

    small_names = ["mix_norm_g", "ffn_norm_g", "gm_ln_g", "gm_ln_b", "gm_w_s", "gm_b_s", "fox_b_f", "ffn_conv_w",
                   "ffn_conv_b", "final_norm_g"]
    small_w = [mix_norm_g, ffn_norm_g, gm_ln_g, gm_ln_b, gm_w_s, gm_b_s, fox_b_f, ffn_conv_w, ffn_conv_b, final_norm_g]
    small_m = [m_mix_norm_g, m_ffn_norm_g, m_gm_ln_g, m_gm_ln_b, m_gm_w_s, m_gm_b_s, m_fox_b_f, m_ffn_conv_w,
               m_ffn_conv_b, m_final_norm_g]
    small_v = [v_mix_norm_g, v_ffn_norm_g, v_gm_ln_g, v_gm_ln_b, v_gm_w_s, v_gm_b_s, v_fox_b_f, v_ffn_conv_w,
               v_ffn_conv_b, v_final_norm_g]
    small_g = [r_mix, r_ffn, r_lng, r_lnb, r_ws, r_bs, r_bf, r_cw, r_cb, r_final]
    shapes = [w.shape for w in small_w]
    small_g = [g.reshape(s) for g, s in zip(small_g, shapes)]
    dlt, mn, vn = _adamw(_pack(small_w), _pack(small_m), _pack(small_v), _pack(small_g), "adamw_small")
    for name, g, dl_, m_, v_ in zip(small_names, small_g, _unpack(dlt, shapes), _unpack(mn, shapes), _unpack(vn, shapes)):
        res[name] = (g, dl_, m_, v_)

    order = ["mix_norm_g", "ffn_norm_g", "gm_w_in", "gm_ln_g", "gm_ln_b", "gm_w_s", "gm_b_s", "gm_w_out", "fox_w_qkvf",
             "fox_b_f", "fox_w_o", "ffn_w_gate", "ffn_w_up", "ffn_conv_w", "ffn_conv_b", "ffn_w_down", "final_norm_g"]
    outs = [r_loss.reshape(()), dh0[None]]
    for part in range(4):
        outs += [res[name][part] for name in order]
    return tuple(outs)
```

```python
import functools
import math

import jax
import jax.numpy as jnp
from jax import lax
from jax.experimental import pallas as pl
from jax.experimental.pallas import tpu as pltpu

F32 = jnp.float32
BF16 = jnp.bfloat16

RMS_EPS = 1e-6
LN_EPS = 1e-5
CHUNK = 128
GM_GROUPS = 8
HEAD_DIM = 64
LANES = 128
ATT_BLOCK = 256
ATT_CHUNK = 1024
VMEM_LIMIT_V7X = 56 * 1024 * 1024

ADAM_LR = 0.001
ADAM_B1 = 0.9
ADAM_B2 = 0.999
ADAM_EPS = 1e-08
ADAM_WD = 0.01
ADAM_STEP = 10

MESH = pl.DeviceIdType.MESH
ANY = pl.BlockSpec(memory_space=pl.ANY)
NEG_BIG = -1e30


def _params(n_grid):
    return pltpu.CompilerParams(dimension_semantics=("arbitrary",) * n_grid, vmem_limit_bytes=VMEM_LIMIT_V7X)


def _dot(a, b):
    return jnp.dot(a, b, preferred_element_type=F32)


def _dot_nt(a, b):
    return lax.dot_general(a, b, (((1,), (1,)), ((), ())), preferred_element_type=F32)


def _dot_tn(a, b):
    return lax.dot_general(a, b, (((0,), (0,)), ((), ())), preferred_element_type=F32)


def _split3(x):
    hi = x.astype(BF16)
    r = x - hi.astype(F32)
    mid = r.astype(BF16)
    lo = (r - mid.astype(F32)).astype(BF16)
    return hi, mid, lo


def _dot3_lhs(x, m):
    hi, mid, lo = _split3(x)
    return _dot(hi, m) + _dot(mid, m) + _dot(lo, m)


def _dot3_rhs(m, x):
    hi, mid, lo = _split3(x)
    return _dot(m, hi) + _dot(m, mid) + _dot(m, lo)


def _load_once(pairs, sem):
    @pl.when(pl.program_id(0) == 0)
    def _():
        copies = [pltpu.make_async_copy(src, dst, sem.at[k]) for k, (src, dst) in enumerate(pairs)]
        for cp in copies:
            cp.start()
        for cp in copies:
            cp.wait()


def _rms_fwd(x, g):
    r = lax.rsqrt(jnp.mean(x * x, axis=-1, keepdims=True) + RMS_EPS)
    xhat = x * r
    return xhat, r, xhat * g


def _rms_bwd(dy, xhat, r, g):
    w = dy * g
    dx = r * (w - xhat * jnp.mean(w * xhat, axis=-1, keepdims=True))
    return dx, dy * xhat


def _gelu_parts(a):
    c = math.sqrt(2.0 / math.pi)
    a2 = a * a
    t = jnp.tanh(c * (a + 0.044715 * a * a2))
    z = 0.5 * a * (1.0 + t)
    dz = 0.5 * (1.0 + t) + 0.5 * a * (1.0 - t * t) * (c * (1.0 + 3.0 * 0.044715 * a2))
    return z, dz


def _sigmoid(x):
    return 1.0 / (1.0 + jnp.exp(-x))


def _gmlp_core(a, lng, lnb, wc_ref, bias, n_chunk, gd):
    e = a.shape[1] // 2
    z, dz = _gelu_parts(a)
    u = z[:, :e]
    v = z[:, e:]
    mu = jnp.mean(v, axis=-1, keepdims=True)
    vc = v - mu
    rstd = lax.rsqrt(jnp.mean(vc * vc, axis=-1, keepdims=True) + LN_EPS)
    vhat = vc * rstd
    vln = vhat * lng + lnb
    vlb = vln.astype(BF16)
    rows = []
    for ci in range(n_chunk):
        cols = []
        for g in range(GM_GROUPS):
            blk = vlb[ci * CHUNK:(ci + 1) * CHUNK, g * gd:(g + 1) * gd]
            cols.append(_dot(wc_ref[g], blk))
        rows.append(jnp.concatenate(cols, axis=1) + bias)
    s = rows[0] if n_chunk == 1 else jnp.concatenate(rows, axis=0)
    return dz, u, vhat, rstd, vlb, s


def _gmlp_fwd(h, g_mix, w_in, lng, lnb, wc, bias, w_out, tm):
    s_len, d = h.shape
    n_p, _, w = w_in.shape
    e = w_out.shape[0]
    gd = e // GM_GROUPS
    n_chunk = tm // CHUNK

    def body(h_ref, g_ref, win_hbm, lng_ref, lnb_ref, wc_ref, bias_ref, wout_hbm,
             hout_ref, a_ref, hn_ref, gated_ref, win_v, wout_v, sem):
        _load_once([(win_hbm, win_v), (wout_hbm, wout_v)], sem)
        x = h_ref[...]
        _, _, y = _rms_fwd(x, g_ref[...])
        hn = y.astype(BF16)
        hn_ref[...] = hn
        for p in range(n_p):
            a_ref[:, p * w:(p + 1) * w] = _dot(hn, win_v[p])
        _, u, _, _, _, s = _gmlp_core(a_ref[...], lng_ref[...], lnb_ref[...], wc_ref, bias_ref[...], n_chunk, gd)
        gated = (u * s).astype(BF16)
        gated_ref[...] = gated
        hout_ref[...] = x + _dot(gated, wout_v[...])

    row = lambda i: (i, 0)
    const2 = lambda i: (0, 0)
    return pl.pallas_call(
        body, name="gmlp_fwd", grid=(s_len // tm,),
        in_specs=[pl.BlockSpec((tm, d), row), pl.BlockSpec((1, d), const2), ANY,
                  pl.BlockSpec((1, e), const2), pl.BlockSpec((1, e), const2),
                  pl.BlockSpec(wc.shape, lambda i: (0, 0, 0)), pl.BlockSpec((CHUNK, e), const2), ANY],
        out_specs=[pl.BlockSpec((tm, d), row), pl.BlockSpec((tm, 2 * e), row),
                   pl.BlockSpec((tm, d), row), pl.BlockSpec((tm, e), row)],
        out_shape=[jax.ShapeDtypeStruct((s_len, d), F32), jax.ShapeDtypeStruct((s_len, 2 * e), F32),
                   jax.ShapeDtypeStruct((s_len, d), BF16), jax.ShapeDtypeStruct((s_len, e), BF16)],
        scratch_shapes=[pltpu.VMEM(w_in.shape, BF16), pltpu.VMEM(w_out.shape, BF16), pltpu.SemaphoreType.DMA((2,))],
        compiler_params=_params(1),
    )(h, g_mix, w_in, lng, lnb, wc, bias, w_out)


def _gmlp_bwd(h, dh, a, g_mix, w_in, lng, lnb, wc, wct, bias, w_out, seg, tm):
    s_len, d = h.shape
    n_p, _, w = w_in.shape
    e = w_out.shape[0]
    gd = e // GM_GROUPS
    n_chunk = tm // CHUNK
    n_blk = s_len // tm

    def body(h_ref, dh_ref, a_ref, g_ref, win_hbm, lng_ref, lnb_ref, wc_ref, wct_ref, bias_ref, wout_hbm, seg_ref,
             dhin_ref, da_ref, gws_ref, gbs_ref, glng_ref, glnb_ref, gmix_ref, win_v, wout_v, dsum, sem):
        i = pl.program_id(0)
        _load_once([(win_hbm, win_v), (wout_hbm, wout_v)], sem)

        @pl.when(i == 0)
        def _():
            gws_ref[...] = jnp.zeros_like(gws_ref)
            glng_ref[...] = jnp.zeros_like(glng_ref)
            glnb_ref[...] = jnp.zeros_like(glnb_ref)
            gmix_ref[...] = jnp.zeros_like(gmix_ref)
            dsum[...] = jnp.zeros_like(dsum)

        x = h_ref[...]
        dh_v = dh_ref[...]
        g = g_ref[...]
        lng_v = lng_ref[...]
        xhat, r, _ = _rms_fwd(x, g)
        dz_da, u, vhat, rstd, vlb, s = _gmlp_core(a_ref[...], lng_v, lnb_ref[...], wc_ref, bias_ref[...], n_chunk, gd)
        dg = _dot_nt(dh_v.astype(BF16), wout_v[...])
        du = dg * s
        ds = dg * u
        dsb = ds.astype(BF16)
        rows = []
        ds_acc = None
        for ci in range(n_chunk):
            lo, hi = ci * CHUNK, (ci + 1) * CHUNK
            cols = []
            for gi in range(GM_GROUPS):
                d_blk = dsb[lo:hi, gi * gd:(gi + 1) * gd]
                gws_ref[gi] += _dot_nt(d_blk, vlb[lo:hi, gi * gd:(gi + 1) * gd])
                cols.append(_dot(wct_ref[gi], d_blk))
            rows.append(jnp.concatenate(cols, axis=1))
            ds_acc = ds[lo:hi] if ds_acc is None else ds_acc + ds[lo:hi]
        dsum[...] += ds_acc
        dvln = rows[0] if n_chunk == 1 else jnp.concatenate(rows, axis=0)
        glng_ref[...] += jnp.sum(dvln * vhat, axis=0, keepdims=True)
        glnb_ref[...] += jnp.sum(dvln, axis=0, keepdims=True)
        dvhat = dvln * lng_v
        dv = rstd * (dvhat - jnp.mean(dvhat, axis=-1, keepdims=True)
                     - vhat * jnp.mean(dvhat * vhat, axis=-1, keepdims=True))
        da = jnp.concatenate([du, dv], axis=1) * dz_da
        dab = da.astype(BF16)
        da_ref[...] = dab
        dhn = _dot_nt(dab[:, :w], win_v[0])
        for p in range(1, n_p):
            dhn += _dot_nt(dab[:, p * w:(p + 1) * w], win_v[p])
        dx, gg = _rms_bwd(dhn, xhat, r, g)
        gmix_ref[...] += jnp.sum(gg, axis=0, keepdims=True)
        dhin_ref[...] = dh_v + dx

        @pl.when(i == n_blk - 1)
        def _():
            tril = lax.broadcasted_iota(jnp.int32, (CHUNK, CHUNK), 0) >= lax.broadcasted_iota(jnp.int32, (CHUNK, CHUNK), 1)
            for gi in range(GM_GROUPS):
                gws_ref[gi] = jnp.where(tril, gws_ref[gi], 0.0)
            gbs_ref[...] = _dot3_lhs(dsum[...], seg_ref[...])

    row = lambda i: (i, 0)
    const2 = lambda i: (0, 0)
    const3 = lambda i: (0, 0, 0)
    return pl.pallas_call(
        body, name="gmlp_bwd", grid=(n_blk,),
        in_specs=[pl.BlockSpec((tm, d), row), pl.BlockSpec((tm, d), row), pl.BlockSpec((tm, 2 * e), row),
                  pl.BlockSpec((1, d), const2), ANY, pl.BlockSpec((1, e), const2), pl.BlockSpec((1, e), const2),
                  pl.BlockSpec(wc.shape, const3), pl.BlockSpec(wct.shape, const3), pl.BlockSpec((CHUNK, e), const2),
                  ANY, pl.BlockSpec((e, LANES), const2)],
        out_specs=[pl.BlockSpec((tm, d), row), pl.BlockSpec((tm, 2 * e), row), pl.BlockSpec(wc.shape, const3),
                   pl.BlockSpec((CHUNK, LANES), const2), pl.BlockSpec((1, e), const2), pl.BlockSpec((1, e), const2),
                   pl.BlockSpec((1, d), const2)],
        out_shape=[jax.ShapeDtypeStruct((s_len, d), F32), jax.ShapeDtypeStruct((s_len, 2 * e), BF16),
                   jax.ShapeDtypeStruct(wc.shape, F32), jax.ShapeDtypeStruct((CHUNK, LANES), F32),
                   jax.ShapeDtypeStruct((1, e), F32), jax.ShapeDtypeStruct((1, e), F32), jax.ShapeDtypeStruct((1, d), F32)],
        scratch_shapes=[pltpu.VMEM(w_in.shape, BF16), pltpu.VMEM(w_out.shape, BF16), pltpu.VMEM((CHUNK, e), F32),
                        pltpu.SemaphoreType.DMA((2,))],
        compiler_params=_params(1),
    )(h, dh, a, g_mix, w_in, lng, lnb, wc, wct, bias, w_out, seg)


def _shift_down(a, k, fill):
    tm = a.shape[0]
    out = pltpu.roll(a, k, 0)
    rid = lax.broadcasted_iota(jnp.int32, a.shape, 0)
    for j in range(k):
        out = jnp.where(rid == j, fill[8 - k + j:8 - k + j + 1, :], out)
    return out


def _shift_up(a, k, fill):
    tm = a.shape[0]
    out = pltpu.roll(a, tm - k, 0)
    rid = lax.broadcasted_iota(jnp.int32, a.shape, 0)
    for j in range(k):
        out = jnp.where(rid == tm - k + j, fill[j:j + 1, :], out)
    return out


def _ffn_fwd(h, g_norm, wg_all, wu_all, wd_all, layer, conv_w, conv_b, tm):
    s_len, d = h.shape
    n_p = wg_all.shape[0]
    fq = wg_all.shape[3]
    f = n_p * fq

    def body(h_ref, g_ref, wg_hbm, wu_hbm, wd_hbm, cw_ref, cb_ref,
             hout_ref, a_ref, up_ref, hn_ref, hid_ref, wg_v, wu_v, wd_v, carry, sem):
        i = pl.program_id(0)
        _load_once([(wg_hbm.at[:, layer], wg_v), (wu_hbm.at[:, layer], wu_v), (wd_hbm.at[:, layer], wd_v)], sem)

        @pl.when(i == 0)
        def _():
            carry[...] = jnp.zeros_like(carry)

        x = h_ref[...]
        _, _, y = _rms_fwd(x, g_ref[...])
        hn = y.astype(BF16)
        hn_ref[...] = hn
        for p in range(n_p):
            a_ref[:, p * fq:(p + 1) * fq] = _dot(hn, wg_v[p])
            up_ref[:, p * fq:(p + 1) * fq] = _dot(hn, wu_v[p])
        a = a_ref[...]
        prev = carry[...]
        am1 = _shift_down(a, 1, prev)
        am2 = _shift_down(a, 2, prev)
        carry[...] = a[tm - 8:tm, :]
        cw = cw_ref[...]
        ac = cb_ref[...] + am2 * cw[0:1, :]
        ac = ac + am1 * cw[1:2, :]
        ac = ac + a * cw[2:3, :]
        hid = (ac * _sigmoid(ac) * up_ref[...]).astype(BF16)
        hid_ref[...] = hid
        y2 = _dot(hid[:, :fq], wd_v[0])
        for p in range(1, n_p):
            y2 += _dot(hid[:, p * fq:(p + 1) * fq], wd_v[p])
        hout_ref[...] = x + y2

    row = lambda i: (i, 0)
    const2 = lambda i: (0, 0)
    return pl.pallas_call(
        body, name=f"ffn_fwd_{layer}", grid=(s_len // tm,),
        in_specs=[pl.BlockSpec((tm, d), row), pl.BlockSpec((1, d), const2), ANY, ANY, ANY,
                  pl.BlockSpec((8, f), const2), pl.BlockSpec((1, f), const2)],
        out_specs=[pl.BlockSpec((tm, d), row), pl.BlockSpec((tm, f), row), pl.BlockSpec((tm, f), row),
                   pl.BlockSpec((tm, d), row), pl.BlockSpec((tm, f), row)],
        out_shape=[jax.ShapeDtypeStruct((s_len, d), F32), jax.ShapeDtypeStruct((s_len, f), F32),
                   jax.ShapeDtypeStruct((s_len, f), F32), jax.ShapeDtypeStruct((s_len, d), BF16),
                   jax.ShapeDtypeStruct((s_len, f), BF16)],
        scratch_shapes=[pltpu.VMEM((n_p, d, fq), BF16), pltpu.VMEM((n_p, d, fq), BF16), pltpu.VMEM((n_p, fq, d), BF16),
                        pltpu.VMEM((8, f), F32), pltpu.SemaphoreType.DMA((3,))],
        compiler_params=_params(1),
    )(h, g_norm, wg_all, wu_all, wd_all, conv_w, conv_b)


def _ffn_bwd(h, dh, a, up, g_norm, wg_all, wu_all, wd_all, layer, conv_w, conv_b, tm):
    s_len, d = h.shape
    n_p = wg_all.shape[0]
    fq = wg_all.shape[3]
    f = n_p * fq
    n_blk = s_len // tm
    t8 = tm // 8

    def body(h_ref, dh_ref, a_ref, ahalo_ref, up_ref, g_ref, wg_hbm, wu_hbm, wd_hbm, cw_ref, cb_ref,
             dhin_ref, da_ref, dup_ref, gcw_ref, gcb_ref, gn_ref, wg_v, wu_v, wd_v, carry, sem):
        i = pl.program_id(0)
        _load_once([(wg_hbm.at[:, layer], wg_v), (wu_hbm.at[:, layer], wu_v), (wd_hbm.at[:, layer], wd_v)], sem)

        @pl.when(i == 0)
        def _():
            carry[...] = jnp.zeros_like(carry)
            gcw_ref[...] = jnp.zeros_like(gcw_ref)
            gcb_ref[...] = jnp.zeros_like(gcb_ref)
            gn_ref[...] = jnp.zeros_like(gn_ref)

        x = h_ref[...]
        dh_v = dh_ref[...]
        g = g_ref[...]
        xhat, r, _ = _rms_fwd(x, g)
        a = a_ref[...]
        up_v = up_ref[...]
        prev = jnp.where(i == n_blk - 1, 0.0, ahalo_ref[...])
        am1 = _shift_down(a, 1, prev)
        am2 = _shift_down(a, 2, prev)
        cw = cw_ref[...]
        ac = cb_ref[...] + am2 * cw[0:1, :]
        ac = ac + am1 * cw[1:2, :]
        ac = ac + a * cw[2:3, :]
        sg = _sigmoid(ac)
        sil = ac * sg
        dhb = dh_v.astype(BF16)
        dhid = jnp.concatenate([_dot_nt(dhb, wd_v[p]) for p in range(n_p)], axis=1)
        dup = dhid * sil
        dac = dhid * up_v * (sg * (1.0 + ac * (1.0 - sg)))
        gcb_ref[...] += jnp.sum(dac, axis=0, keepdims=True)
        gcw_ref[0:1, :] += jnp.sum(dac * am2, axis=0, keepdims=True)
        gcw_ref[1:2, :] += jnp.sum(dac * am1, axis=0, keepdims=True)
        gcw_ref[2:3, :] += jnp.sum(dac * a, axis=0, keepdims=True)
        nxt = carry[...]
        dp1 = _shift_up(dac, 1, nxt)
        dp2 = _shift_up(dac, 2, nxt)
        carry[...] = dac[0:8, :]
        da = dac * cw[2:3, :] + dp1 * cw[1:2, :] + dp2 * cw[0:1, :]
        dab = da.astype(BF16)
        dupb = dup.astype(BF16)
        da_ref[...] = dab
        dup_ref[...] = dupb
        dhn = _dot_nt(dab[:, :fq], wg_v[0]) + _dot_nt(dupb[:, :fq], wu_v[0])
        for p in range(1, n_p):
            dhn += _dot_nt(dab[:, p * fq:(p + 1) * fq], wg_v[p]) + _dot_nt(dupb[:, p * fq:(p + 1) * fq], wu_v[p])
        dx, gg = _rms_bwd(dhn, xhat, r, g)
        gn_ref[...] += jnp.sum(gg, axis=0, keepdims=True)
        dhin_ref[...] = dh_v + dx

    rev = lambda i: (n_blk - 1 - i, 0)
    halo = lambda i: (jnp.maximum((n_blk - 1 - i) * t8 - 1, 0), 0)
    const2 = lambda i: (0, 0)
    return pl.pallas_call(
        body, name=f"ffn_bwd_{layer}", grid=(n_blk,),
        in_specs=[pl.BlockSpec((tm, d), rev), pl.BlockSpec((tm, d), rev), pl.BlockSpec((tm, f), rev),
                  pl.BlockSpec((8, f), halo), pl.BlockSpec((tm, f), rev), pl.BlockSpec((1, d), const2), ANY, ANY, ANY,
                  pl.BlockSpec((8, f), const2), pl.BlockSpec((1, f), const2)],
        out_specs=[pl.BlockSpec((tm, d), rev), pl.BlockSpec((tm, f), rev), pl.BlockSpec((tm, f), rev),
                   pl.BlockSpec((8, f), const2), pl.BlockSpec((1, f), const2), pl.BlockSpec((1, d), const2)],
        out_shape=[jax.ShapeDtypeStruct((s_len, d), F32), jax.ShapeDtypeStruct((s_len, f), BF16),
                   jax.ShapeDtypeStruct((s_len, f), BF16), jax.ShapeDtypeStruct((8, f), F32),
                   jax.ShapeDtypeStruct((1, f), F32), jax.ShapeDtypeStruct((1, d), F32)],
        scratch_shapes=[pltpu.VMEM((n_p, d, fq), BF16), pltpu.VMEM((n_p, d, fq), BF16), pltpu.VMEM((n_p, fq, d), BF16),
                        pltpu.VMEM((8, f), F32), pltpu.SemaphoreType.DMA((3,))],
        compiler_params=_params(1),
    )(h, dh, a, a, up, g_norm, wg_all, wu_all, wd_all, conv_w, conv_b)


def _even_head_lanes(shape, axis):
    return (lax.broadcasted_iota(jnp.int32, shape, axis) & HEAD_DIM) == 0


def _pair_select(lo, hi, shape):
    return jnp.where(lax.broadcasted_iota(jnp.int32, shape, 1) < HEAD_DIM, lo, hi)


def _causal(row0, col0, shape):
    return row0 + lax.broadcasted_iota(jnp.int32, shape, 0) >= col0 + lax.broadcasted_iota(jnp.int32, shape, 1)


N_SPARE = 3


def _spare_selectors(d, key_side):
    lane = jnp.arange(d)[None, :]
    row = jnp.arange(N_SPARE * LANES)[:, None]
    head, part = row % LANES, row // LANES
    off = N_SPARE if key_side else 0
    sel_a = ((head % 2 == 0) & (lane == LANES * (head // 2) + HEAD_DIM + off + part)).astype(F32)
    sel_b = ((head % 2 == 1) & (lane == LANES * (head // 2) + off + part)).astype(F32)
    sign = -1.0 if key_side else 1.0
    ones_off = 0 if key_side else N_SPARE
    in_pair = jnp.arange(d)[None, :] % LANES
    ones_a = ((in_pair >= HEAD_DIM + ones_off) & (in_pair < HEAD_DIM + ones_off + N_SPARE)).astype(F32)
    ones_b = ((in_pair >= ones_off) & (in_pair < ones_off + N_SPARE)).astype(F32)
    return (sign * sel_a).astype(BF16), (sign * sel_b).astype(BF16), ones_a, ones_b


def _parts(x):
    return jnp.concatenate(_split3(x), axis=1)


def _fox_proj_fwd(h, g_norm, wq, wk, wv, wf, bf, sel_q, sel_k, tm):
    s_len, d = h.shape
    sq_a, sq_b, oq_a, oq_b = sel_q
    sk_a, sk_b, ok_a, ok_b = sel_k

    def body(h_ref, g_ref, wq_hbm, wk_hbm, wv_hbm, wf_ref, bf_ref, sqa_ref, sqb_ref, oqa_ref, oqb_ref,
             ska_ref, skb_ref, oka_ref, okb_ref,
             hn_ref, qa_ref, qb_ref, kat_ref, kbt_ref, va_ref, vb_ref, vat_ref, vbt_ref, z_ref,
             wq_v, wk_v, wv_v, total, sem):
        i = pl.program_id(0)
        _load_once([(wq_hbm, wq_v), (wk_hbm, wk_v), (wv_hbm, wv_v)], sem)

        @pl.when(i == 0)
        def _():
            total[...] = jnp.zeros_like(total)

        x = h_ref[...]
        _, _, y = _rms_fwd(x, g_ref[...])
        hn = y.astype(BF16)
        hn_ref[...] = hn
        z = _dot(hn, wf_ref[...]) + bf_ref[...]
        z_ref[...] = z
        logf = jnp.minimum(z, 0.0) - jnp.log(1.0 + jnp.exp(-jnp.abs(z)))
        tri = (lax.broadcasted_iota(jnp.int32, (tm, tm), 0) >= lax.broadcasted_iota(jnp.int32, (tm, tm), 1))
        cum = _dot3_rhs(jnp.where(tri, 1.0, 0.0).astype(BF16), logf) + total[0:1, :]
        total[...] = jnp.broadcast_to(cum[tm - 1:tm, :], total.shape)
        parts = _parts(cum)

        even = _even_head_lanes((tm, d), 1)
        q = _dot(hn, wq_v[...]) * (HEAD_DIM ** -0.5)
        qa_ref[...] = jnp.where(even, q, _dot(parts, sqa_ref[...]) + oqa_ref[...]).astype(BF16)
        qb_ref[...] = jnp.where(even, _dot(parts, sqb_ref[...]) + oqb_ref[...], q).astype(BF16)
        k = _dot(hn, wk_v[...])
        ka = jnp.where(even, k, _dot(parts, ska_ref[...]) + oka_ref[...])
        kb = jnp.where(even, _dot(parts, skb_ref[...]) + okb_ref[...], k)
        kat_ref[...] = ka.T.astype(BF16)
        kbt_ref[...] = kb.T.astype(BF16)
        v = _dot(hn, wv_v[...])
        va = jnp.where(even, v, oka_ref[...])
        vb = jnp.where(even, okb_ref[...], v)
        va_ref[...] = va.astype(BF16)
        vb_ref[...] = vb.astype(BF16)
        vat_ref[...] = va.T.astype(BF16)
        vbt_ref[...] = vb.T.astype(BF16)

    row = lambda i: (i, 0)
    col = lambda i: (0, i)
    const2 = lambda i: (0, 0)
    sd = jax.ShapeDtypeStruct((s_len, d), BF16)
    ds_ = jax.ShapeDtypeStruct((d, s_len), BF16)
    rs, cs = pl.BlockSpec((tm, d), row), pl.BlockSpec((d, tm), col)
    sel = pl.BlockSpec((N_SPARE * LANES, d), const2)
    one = pl.BlockSpec((1, d), const2)
    return pl.pallas_call(
        body, name="fox_proj_fwd", grid=(s_len // tm,),
        in_specs=[rs, one, ANY, ANY, ANY, pl.BlockSpec((d, LANES), const2), pl.BlockSpec((1, LANES), const2),
                  sel, sel, one, one, sel, sel, one, one],
        out_specs=[rs, rs, rs, cs, cs, rs, rs, cs, cs, pl.BlockSpec((tm, LANES), row)],
        out_shape=[sd, sd, sd, ds_, ds_, sd, sd, ds_, ds_, jax.ShapeDtypeStruct((s_len, LANES), F32)],
        scratch_shapes=[pltpu.VMEM((d, d), BF16), pltpu.VMEM((d, d), BF16), pltpu.VMEM((d, d), BF16),
                        pltpu.VMEM((8, LANES), F32), pltpu.SemaphoreType.DMA((3,))],
        compiler_params=_params(1),
    )(h, g_norm, wq, wk, wv, wf, bf, sq_a, sq_b, oq_a, oq_b, sk_a, sk_b, ok_a, ok_b)


def _spare_cols(x, base):
    xf = x[:, base:base + N_SPARE].astype(F32)
    return xf[:, 0:1] + xf[:, 1:2] + xf[:, 2:3]


def _with_query_term(x, term, base):
    lane = lax.broadcasted_iota(jnp.int32, x.shape, 1)
    hi, mid, lo = _split3(term)
    out = jnp.where(lane == base, hi.astype(F32), x)
    out = jnp.where(lane == base + 1, mid.astype(F32), out)
    out = jnp.where(lane == base + 2, lo.astype(F32), out)
    return jnp.where((lane >= base + N_SPARE) & (lane < base + 2 * N_SPARE), 1.0, out)


def _flash_fwd(qa, qb, kat, kbt, va, vb):
    s_len, d = qa.shape
    sub = ATT_BLOCK
    n_sub = 2 if s_len % (2 * sub) == 0 else 1
    t = n_sub * sub
    w = min(ATT_CHUNK, s_len)
    n_pair = d // LANES
    n_q = s_len // t
    bases = (HEAD_DIM, 0)
    chains = [(r, hh) for r in range(n_sub) for hh in range(2)]

    def body(qa_ref, qb_ref, kat_ref, kbt_ref, va_ref, vb_ref, o_ref, qa2_ref, qb2_ref, qat2_ref, qbt2_ref):
        i = pl.program_id(1)
        q_refs = (qa_ref, qb_ref)
        qs = [q_refs[hh][r * sub:(r + 1) * sub, :] for r, hh in chains]
        kts = (kat_ref, kbt_ref)
        vs = (va_ref, vb_ref)

        def step(kb, carry, masked, width=w):
            off = pl.multiple_of(kb * w, w)
            cols = pl.ds(off, width)
            scores = [_dot(qs[c], kts[hh][:, cols]) for c, (r, hh) in enumerate(chains)]
            probs, stats = [], []
            for c, (r, hh) in enumerate(chains):
                m, _ = carry[c]
                s = scores[c]
                if masked:
                    s = jnp.where(_causal(i * t + r * sub, off, (sub, width)), s, NEG_BIG)
                m_new = jnp.maximum(m, jnp.max(s, axis=1, keepdims=True))
                probs.append(jnp.exp(s - m_new).astype(BF16))
                stats.append((m_new, jnp.exp(m - m_new)))
            return tuple((stats[c][0], carry[c][1] * stats[c][1] + _dot(probs[c], vs[hh][cols, :]))
                         for c, (r, hh) in enumerate(chains))

        init = ((jnp.full((sub, 1), NEG_BIG, F32), jnp.zeros((sub, LANES), F32)),) * len(chains)
        diag = (i * t) // w
        carry = lax.fori_loop(0, diag, lambda kb, c: step(kb, c, False), init)
        carry = step(diag, carry, True)
        for r in range(n_sub):
            outs, q2 = [], []
            for hh in range(2):
                m, acc = carry[2 * r + hh]
                l = acc[:, bases[hh]:bases[hh] + 1]
                outs.append(acc / l)
                term = _spare_cols(qs[2 * r + hh], bases[hh]) - (m + jnp.log(l))
                q2.append(_with_query_term(qs[2 * r + hh].astype(F32), term, bases[hh]))
            rows = slice(r * sub, (r + 1) * sub)
            o_ref[rows, :] = _pair_select(outs[0], outs[1], (sub, LANES))
            qa2_ref[rows, :] = q2[0].astype(BF16)
            qb2_ref[rows, :] = q2[1].astype(BF16)
            qat2_ref[:, rows] = q2[0].T.astype(BF16)
            qbt2_ref[:, rows] = q2[1].T.astype(BF16)

    qblk = pl.BlockSpec((t, LANES), lambda j, i: (i, j))
    qblk_t = pl.BlockSpec((LANES, t), lambda j, i: (j, i))
    whole_t = pl.BlockSpec((LANES, s_len), lambda j, i: (j, 0))
    whole = pl.BlockSpec((s_len, LANES), lambda j, i: (0, j))
    sd = jax.ShapeDtypeStruct((s_len, d), BF16)
    ds_ = jax.ShapeDtypeStruct((d, s_len), BF16)
    return pl.pallas_call(
        body, name="flash_fwd", grid=(n_pair, n_q),
        in_specs=[qblk, qblk, whole_t, whole_t, whole, whole],
        out_specs=[qblk, qblk, qblk, qblk_t, qblk_t],
        out_shape=[jax.ShapeDtypeStruct((s_len, d), F32), sd, sd, ds_, ds_],
        compiler_params=_params(2),
    )(qa, qb, kat, kbt, va, vb)


def _flash_bwd(qa, qb, qat, qbt, kat, kbt, vat, vbt, doa, dob, doat, dobt):
    s_len, d = qa.shape
    sub = ATT_BLOCK
    n_sub = 2 if s_len % (2 * sub) == 0 else 1
    t = n_sub * sub
    w = min(ATT_CHUNK, s_len)
    n_pair = d // LANES
    n_q = s_len // t
    hd = HEAD_DIM
    chains = [(r, hh) for r in range(n_sub) for hh in range(2)]

    def body(qa_ref, qb_ref, qat_ref, qbt_ref, kat_hbm, kbt_hbm, vat_hbm, vbt_hbm, doa_ref, dob_ref, doat_ref, dobt_ref,
             dqt_ref, dkt_ref, dvt_ref, rs_ref, cs_ref,
             kat_v, kbt_v, vat_v, vbt_v, dkt_acc, dvt_acc, cs_acc, sem):
        j = pl.program_id(0)
        i = pl.program_id(1)

        @pl.when(i == 0)
        def _():
            rows = pl.ds(pl.multiple_of(j * LANES, LANES), LANES)
            copies = [pltpu.make_async_copy(src.at[rows, :], dst, sem.at[n]) for n, (src, dst) in enumerate([
                (kat_hbm, kat_v), (kbt_hbm, kbt_v), (vat_hbm, vat_v), (vbt_hbm, vbt_v)])]
            for cp in copies:
                cp.start()
            dkt_acc[...] = jnp.zeros_like(dkt_acc)
            dvt_acc[...] = jnp.zeros_like(dvt_acc)
            cs_acc[...] = jnp.zeros_like(cs_acc)
            for cp in copies:
                cp.wait()

        q_refs, do_refs = (qa_ref, qb_ref), (doa_ref, dob_ref)
        qs = [q_refs[hh][r * sub:(r + 1) * sub, :] for r, hh in chains]
        dos = [do_refs[hh][r * sub:(r + 1) * sub, :] for r, hh in chains]
        own = (slice(0, hd), slice(hd, 2 * hd))
        spare = (slice(hd, hd + 8), slice(0, 8))
        used = (slice(0, hd + 16), slice(0, 2 * hd))
        qts = (qat_ref[used[0], :], qbt_ref[used[1], :])
        dots = (doat_ref[own[0], :], dobt_ref[own[1], :])
        kts, vts = (kat_v, kbt_v), (vat_v, vbt_v)

        def step(kb, carry, masked, width=w):
            off = pl.multiple_of(kb * w, w)
            cols = pl.ds(off, width)
            scores = [_dot(qs[c], kts[hh][:, cols]) for c, (r, hh) in enumerate(chains)]
            dps = [_dot(dos[c], vts[hh][:, cols]) for c, (r, hh) in enumerate(chains)]
            ps, dss = [], []
            for c, (r, hh) in enumerate(chains):
                s = scores[c]
                if masked:
                    s = jnp.where(_causal(i * t + r * sub, off, (sub, width)), s, NEG_BIG)
                p = jnp.exp(s)
                dss.append((p * dps[c]).astype(BF16))
                ps.append(p.astype(BF16))
            out = tuple(carry[c] + _dot_nt(kts[hh][used[hh], cols], dss[c]) for c, (r, hh) in enumerate(chains))
            for hh in range(2):
                p_all = jnp.concatenate([ps[2 * r + hh] for r in range(n_sub)], axis=0)
                ds_all = jnp.concatenate([dss[2 * r + hh] for r in range(n_sub)], axis=0)
                dvt_acc[own[hh], cols] += _dot(dots[hh], p_all)
                with_sums = _dot(qts[hh], ds_all)
                dkt_acc[own[hh], cols] += with_sums[own[hh], :]
                cs_acc[8 * hh:8 * hh + 8, cols] += with_sums[spare[hh], :]
            return out

        diag = (i * t) // w
        init = (jnp.zeros((hd + 16, sub), F32), jnp.zeros((2 * hd, sub), F32)) * n_sub
        carry = lax.fori_loop(0, diag, lambda kb, c: step(kb, c, False), init)
        carry = lax.cond((i * t) % w + t <= w // 2,
                         lambda c: step(diag, c, True, w // 2), lambda c: step(diag, c, True), carry)
        for c, (r, hh) in enumerate(chains):
            at = slice(r * sub, (r + 1) * sub)
            dqt_ref[own[hh], at] = (carry[c][own[hh], :] * (hd ** -0.5)).astype(BF16)
            rs_ref[0, 8 * hh:8 * hh + 8, at] = carry[c][spare[hh], :]

        @pl.when(i == n_q - 1)
        def _():
            dkt_ref[...] = dkt_acc[...].astype(BF16)
            dvt_ref[...] = dvt_acc[...].astype(BF16)
            cs_ref[0] = cs_acc[...]

    qblk = pl.BlockSpec((t, LANES), lambda j, i: (i, j))
    qblk_t = pl.BlockSpec((LANES, t), lambda j, i: (j, i))
    whole_t = pl.BlockSpec((LANES, s_len), lambda j, i: (j, 0))
    ds_ = jax.ShapeDtypeStruct((d, s_len), BF16)
    sums = jax.ShapeDtypeStruct((n_pair, 16, s_len), F32)
    return pl.pallas_call(
        body, name="flash_bwd", grid=(n_pair, n_q),
        in_specs=[qblk, qblk, qblk_t, qblk_t, ANY, ANY, ANY, ANY, qblk, qblk, qblk_t, qblk_t],
        out_specs=[qblk_t, whole_t, whole_t, pl.BlockSpec((1, 16, t), lambda j, i: (j, 0, i)),
                   pl.BlockSpec((1, 16, s_len), lambda j, i: (j, 0, 0))],
        out_shape=[ds_, ds_, ds_, sums, sums],
        scratch_shapes=[pltpu.VMEM((LANES, s_len), BF16), pltpu.VMEM((LANES, s_len), BF16),
                        pltpu.VMEM((LANES, s_len), BF16), pltpu.VMEM((LANES, s_len), BF16),
                        pltpu.VMEM((LANES, s_len), F32), pltpu.VMEM((LANES, s_len), F32),
                        pltpu.VMEM((16, s_len), F32), pltpu.SemaphoreType.DMA((4,))],
        compiler_params=_params(2),
    )(qa, qb, qat, qbt, kat, kbt, vat, vbt, doa, dob, doat, dobt)


def _wgrad_t(at, b, name):
    k, s_len = at.shape
    n = b.shape[1]
    tn, tk, ts = min(n, 1024), min(k, 1024), min(s_len, 1024)

    def body(a_ref, b_ref, o_ref):
        @pl.when(pl.program_id(2) == 0)
        def _():
            o_ref[...] = jnp.zeros_like(o_ref)
        o_ref[...] += _dot(a_ref[...].astype(BF16), b_ref[...].astype(BF16))

    return pl.pallas_call(
        body, name=name, grid=(k // tk, n // tn, s_len // ts),
        in_specs=[pl.BlockSpec((tk, ts), lambda a, b_, c: (a, c)), pl.BlockSpec((ts, tn), lambda a, b_, c: (c, b_))],
        out_specs=pl.BlockSpec((tk, tn), lambda a, b_, c: (a, b_)),
        out_shape=jax.ShapeDtypeStruct((k, n), F32),
        compiler_params=_params(3),
    )(at, b)


def _oproj_bwd(dh, o, wo, seg, sel_q, tm):
    s_len, d = dh.shape
    sq_a, sq_b, _, _ = sel_q

    def body(dh_ref, o_ref, wo_hbm, seg_ref, sqa_ref, sqb_ref, doa_ref, dob_ref, doat_ref, dobt_ref, wo_v, sem):
        _load_once([(wo_hbm, wo_v)], sem)
        do = _dot_nt(dh_ref[...].astype(BF16), wo_v[...])
        parts = _parts(-_dot3_lhs(do * o_ref[...], seg_ref[...]))
        even = _even_head_lanes((tm, d), 1)
        doa = jnp.where(even, do, _dot(parts, sqa_ref[...]))
        dob = jnp.where(even, _dot(parts, sqb_ref[...]), do)
        doa_ref[...] = doa.astype(BF16)
        dob_ref[...] = dob.astype(BF16)
        doat_ref[...] = doa.T.astype(BF16)
        dobt_ref[...] = dob.T.astype(BF16)

    row = lambda i: (i, 0)
    const2 = lambda i: (0, 0)
    rs, cs = pl.BlockSpec((tm, d), row), pl.BlockSpec((d, tm), lambda i: (0, i))
    sel = pl.BlockSpec((N_SPARE * LANES, d), const2)
    sd = jax.ShapeDtypeStruct((s_len, d), BF16)
    ds_ = jax.ShapeDtypeStruct((d, s_len), BF16)
    return pl.pallas_call(
        body, name="oproj_bwd", grid=(s_len // tm,),
        in_specs=[rs, rs, ANY, pl.BlockSpec((d, LANES), const2), sel, sel],
        out_specs=[rs, rs, cs, cs], out_shape=[sd, sd, ds_, ds_],
        scratch_shapes=[pltpu.VMEM((d, d), BF16), pltpu.SemaphoreType.DMA((1,))],
        compiler_params=_params(1),
    )(dh, o, wo, seg, sq_a, sq_b)


def _oproj_fwd(h, o, wo, tm):
    s_len, d = h.shape

    def body(h_ref, o_ref, wo_hbm, hout_ref, wo_v, sem):
        _load_once([(wo_hbm, wo_v)], sem)
        hout_ref[...] = h_ref[...] + _dot(o_ref[...].astype(BF16), wo_v[...])

    row = lambda i: (i, 0)
    return pl.pallas_call(
        body, name="oproj_fwd", grid=(s_len // tm,),
        in_specs=[pl.BlockSpec((tm, d), row), pl.BlockSpec((tm, d), row), ANY],
        out_specs=pl.BlockSpec((tm, d), row),
        out_shape=jax.ShapeDtypeStruct((s_len, d), F32),
        scratch_shapes=[pltpu.VMEM((d, d), BF16), pltpu.SemaphoreType.DMA((1,))],
        compiler_params=_params(1),
    )(h, o, wo)


def _forget_bwd(dcum, z, tm):
    s_len = dcum.shape[0]
    n_blk = s_len // tm

    def body(dc_ref, z_ref, dfl_ref, gb_ref, total):
        i = pl.program_id(0)

        @pl.when(i == 0)
        def _():
            total[...] = jnp.zeros_like(total)
            gb_ref[...] = jnp.zeros_like(gb_ref)

        upper = (lax.broadcasted_iota(jnp.int32, (tm, tm), 0) <= lax.broadcasted_iota(jnp.int32, (tm, tm), 1))
        suffix = _dot3_rhs(jnp.where(upper, 1.0, 0.0).astype(BF16), dc_ref[...]) + total[0:1, :]
        total[...] = jnp.broadcast_to(suffix[0:1, :], total.shape)
        dfl = suffix * _sigmoid(-z_ref[...])
        dfl_ref[...] = dfl
        gb_ref[...] += jnp.sum(dfl, axis=0, keepdims=True)

    rev = lambda i: (n_blk - 1 - i, 0)
    return pl.pallas_call(
        body, name="forget_bwd", grid=(n_blk,),
        in_specs=[pl.BlockSpec((tm, LANES), rev), pl.BlockSpec((tm, LANES), rev)],
        out_specs=[pl.BlockSpec((tm, LANES), rev), pl.BlockSpec((1, LANES), lambda i: (0, 0))],
        out_shape=[jax.ShapeDtypeStruct((s_len, LANES), F32), jax.ShapeDtypeStruct((1, LANES), F32)],
        scratch_shapes=[pltpu.VMEM((8, LANES), F32)],
        compiler_params=_params(1),
    )(dcum, z)


def _fox_proj_bwd(h, dh, dqt, dkt, dvt, dfl, g_norm, wq, wk, wv, wf, tm):
    s_len, d = h.shape

    def body(h_ref, dh_ref, dqt_ref, dkt_ref, dvt_ref, dfl_ref, g_ref, wq_hbm, wk_hbm, wv_hbm, wf_ref,
             dhin_ref, dflb_ref, gn_ref, wq_v, wk_v, wv_v, sem):
        _load_once([(wq_hbm, wq_v), (wk_hbm, wk_v), (wv_hbm, wv_v)], sem)

        @pl.when(pl.program_id(0) == 0)
        def _():
            gn_ref[...] = jnp.zeros_like(gn_ref)

        g = g_ref[...]
        xhat, r, _ = _rms_fwd(h_ref[...], g)
        dflb = dfl_ref[...].astype(BF16)
        dflb_ref[...] = dflb
        from_qkv = (_dot(wq_v[...], dqt_ref[...]) + _dot(wk_v[...], dkt_ref[...])
                    + _dot(wv_v[...], dvt_ref[...]))
        dhn = _dot_nt(dflb, wf_ref[...]) + from_qkv.T
        dx, gg = _rms_bwd(dhn, xhat, r, g)
        gn_ref[...] += jnp.sum(gg, axis=0, keepdims=True)
        dhin_ref[...] = dh_ref[...] + dx

    row = lambda i: (i, 0)
    const2 = lambda i: (0, 0)
    rs = pl.BlockSpec((tm, d), row)
    cs = pl.BlockSpec((d, tm), lambda i: (0, i))
    return pl.pallas_call(
        body, name="fox_proj_bwd", grid=(s_len // tm,),
        in_specs=[rs, rs, cs, cs, cs, pl.BlockSpec((tm, LANES), row), pl.BlockSpec((1, d), const2), ANY, ANY, ANY,
                  pl.BlockSpec((d, LANES), const2)],
        out_specs=[rs, pl.BlockSpec((tm, LANES), row), pl.BlockSpec((1, d), const2)],
        out_shape=[jax.ShapeDtypeStruct((s_len, d), F32), jax.ShapeDtypeStruct((s_len, LANES), BF16),
                   jax.ShapeDtypeStruct((1, d), F32)],
        scratch_shapes=[pltpu.VMEM((d, d), BF16), pltpu.VMEM((d, d), BF16), pltpu.VMEM((d, d), BF16),
                        pltpu.SemaphoreType.DMA((3,))],
        compiler_params=_params(1),
    )(h, dh, dqt, dkt, dvt, dfl, g_norm, wq, wk, wv, wf)


def _loss_head(h, target, g_final, tm):
    s_len, d = h.shape
    n_blk = s_len // tm

    def body(h_ref, t_ref, g_ref, dh_ref, loss_ref, gg_ref, sq):
        i = pl.program_id(0)

        @pl.when(i == 0)
        def _():
            sq[...] = jnp.zeros_like(sq)
            gg_ref[...] = jnp.zeros_like(gg_ref)

        g = g_ref[...]
        xhat, r, y = _rms_fwd(h_ref[...], g)
        err = y - t_ref[...]
        sq[...] += jnp.sum(err * err, axis=0, keepdims=True)
        dx, gg = _rms_bwd(err * (1.0 / d), xhat, r, g)
        gg_ref[...] += jnp.sum(gg, axis=0, keepdims=True)
        dh_ref[...] = dx

        @pl.when(i == n_blk - 1)
        def _():
            loss_ref[...] = jnp.broadcast_to(jnp.sum(sq[...], axis=1, keepdims=True) * (0.5 / d), loss_ref.shape)

    row = lambda i: (i, 0)
    const2 = lambda i: (0, 0)
    return pl.pallas_call(
        body, name="loss_head", grid=(n_blk,),
        in_specs=[pl.BlockSpec((tm, d), row), pl.BlockSpec((tm, d), row), pl.BlockSpec((1, d), const2)],
        out_specs=[pl.BlockSpec((tm, d), row), pl.BlockSpec((1, LANES), const2), pl.BlockSpec((1, d), const2)],
        out_shape=[jax.ShapeDtypeStruct((s_len, d), F32), jax.ShapeDtypeStruct((1, LANES), F32),
                   jax.ShapeDtypeStruct((1, d), F32)],
        scratch_shapes=[pltpu.VMEM((1, d), F32)],
        compiler_params=_params(1),
    )(h, target, g_final)


def _wgrad(x, dy, n_piece, name):
    s_len, k = x.shape
    n = dy.shape[1]
    tn = min(n // n_piece, 1024)
    tk = min(k, 1024)
    ts = min(s_len, 1024)
    per_piece = (n // n_piece) // tn

    def body(x_ref, dy_ref, o_ref):
        @pl.when(pl.program_id(2) == 0)
        def _():
            o_ref[...] = jnp.zeros_like(o_ref)
        o_ref[0] += _dot_tn(x_ref[...].astype(BF16), dy_ref[...].astype(BF16))

    return pl.pallas_call(
        body, name=name, grid=(k // tk, n // tn, s_len // ts),
        in_specs=[pl.BlockSpec((ts, tk), lambda a, b, c: (c, a)), pl.BlockSpec((ts, tn), lambda a, b, c: (c, b))],
        out_specs=pl.BlockSpec((1, tk, tn), lambda a, b, c: (b // per_piece, a, b % per_piece)),
        out_shape=jax.ShapeDtypeStruct((n_piece, k, n // n_piece), F32),
        compiler_params=_params(3),
    )(x, dy)


def _pair_sum(g, recv, core, name):
    n_piece, rows, c = g.shape
    half = rows // 2
    tr = min(half, 512)
    nb = half // tr

    def body(core_ref, g_ref, r_ref, o_ref, ob_ref):
        total = g_ref[...] + r_ref[...]
        o_ref[...] = total
        ob_ref[...] = total.astype(BF16)

    blk = pl.BlockSpec((1, tr, c), lambda p, i, core_ref: (p, i, 0))
    return pl.pallas_call(
        body, name=name,
        out_shape=[jax.ShapeDtypeStruct((n_piece, half, c), F32), jax.ShapeDtypeStruct((n_piece, half, c), BF16)],
        grid_spec=pltpu.PrefetchScalarGridSpec(
            num_scalar_prefetch=1, grid=(n_piece, nb),
            in_specs=[pl.BlockSpec((1, tr, c), lambda p, i, core_ref: (p, core_ref[0] * nb + i, 0)), blk],
            out_specs=[blk, blk]),
        compiler_params=_params(2),
    )(core, g, recv)


def _chip_sum(halves, recv, chip, name):
    _, h, c = halves.shape
    tr = min(h, 512)

    def body(chip_ref, own_ref, r_ref, o_ref):
        o_ref[...] = ((own_ref[0] + r_ref[0].astype(F32)) + r_ref[1].astype(F32)) + r_ref[2].astype(F32)

    return pl.pallas_call(
        body, name=name, out_shape=jax.ShapeDtypeStruct((h, c), F32),
        grid_spec=pltpu.PrefetchScalarGridSpec(
            num_scalar_prefetch=1, grid=(h // tr,),
            in_specs=[pl.BlockSpec((1, tr, c), lambda i, chip_ref: (chip_ref[0], i, 0)),
                      pl.BlockSpec((3, tr, c), lambda i, chip_ref: (0, i, 0))],
            out_specs=pl.BlockSpec((tr, c), lambda i, chip_ref: (i, 0))),
        compiler_params=_params(1),
    )(chip, halves, recv)


def _adamw_math(w, m, v, g):
    m_new = ADAM_B1 * m + (1.0 - ADAM_B1) * g
    v_new = ADAM_B2 * v + (1.0 - ADAM_B2) * (g * g)
    m_hat = m_new / (1.0 - ADAM_B1 ** ADAM_STEP)
    v_hat = v_new / (1.0 - ADAM_B2 ** ADAM_STEP)
    return -ADAM_LR * (m_hat / (jnp.sqrt(v_hat) + ADAM_EPS) + ADAM_WD * w), m_new, v_new


def _adamw(w, m, v, g, name):
    rows, c = w.shape
    tr = min(rows, 256)

    def body(w_ref, m_ref, v_ref, g_ref, d_ref, mo_ref, vo_ref):
        d_ref[...], mo_ref[...], vo_ref[...] = _adamw_math(w_ref[...], m_ref[...], v_ref[...], g_ref[...])

    spec = pl.BlockSpec((tr, c), lambda i: (i, 0))
    shape = jax.ShapeDtypeStruct((rows, c), F32)
    return pl.pallas_call(
        body, name=name, grid=(rows // tr,),
        in_specs=[spec] * 4, out_specs=[spec] * 3, out_shape=[shape] * 3,
        compiler_params=_params(1),
    )(w, m, v, g)


def _adamw_halves(w, m, v, g_own, g_other, core, name):
    rows, c = w.shape
    half = rows // 2
    tr = min(half, 256)
    nb = half // tr

    def body(core_ref, w_ref, m_ref, v_ref, own_ref, other_ref, g_ref, d_ref, mo_ref, vo_ref):
        mine = (pl.program_id(0) // nb) == core_ref[0]
        g = jnp.where(mine, own_ref[...], other_ref[...])
        g_ref[...] = g
        d_ref[...], mo_ref[...], vo_ref[...] = _adamw_math(w_ref[...], m_ref[...], v_ref[...], g)

    spec = pl.BlockSpec((tr, c), lambda i, core_ref: (i, 0))
    own = pl.BlockSpec((tr, c), lambda i, core_ref: (jnp.clip(i - core_ref[0] * nb, 0, nb - 1), 0))
    other = pl.BlockSpec((tr, c), lambda i, core_ref: (jnp.clip(i - (1 - core_ref[0]) * nb, 0, nb - 1), 0))
    shape = jax.ShapeDtypeStruct((rows, c), F32)
    return pl.pallas_call(
        body, name=name, out_shape=[shape] * 4,
        grid_spec=pltpu.PrefetchScalarGridSpec(
            num_scalar_prefetch=1, grid=(rows // tr,),
            in_specs=[spec, spec, spec, own, other], out_specs=[spec] * 4),
        compiler_params=_params(1),
    )(core, w, m, v, g_own, g_other)


def _place():
    x, y, c = lax.axis_index("x"), lax.axis_index("y"), lax.axis_index("c")
    chips = [(1 - x, y), (x, 1 - y), (1 - x, 1 - y)]
    return x, y, c, chips


def _half_of(ref, half, which):
    start = which * half
    if half % 8 == 0:
        start = pl.multiple_of(start, 8)
    return ref.at[pl.ds(start, half)]


def _gather_copies(ins, outs, send_sems, recv_sems):
    x, y, c, chips = _place()
    mine = 2 * x + y
    copies = []
    for k, (src, dst) in enumerate(zip(ins, outs)):
        half = src.shape[0] // 2
        copies.append(pltpu.make_async_remote_copy(
            src_ref=src, dst_ref=dst.at[mine], send_sem=send_sems.at[4 * k + 3], recv_sem=recv_sems.at[4 * k + 3],
            device_id=(x, y, 1 - c), device_id_type=MESH))
        for j, (tx, ty) in enumerate(chips):
            copies.append(pltpu.make_async_remote_copy(
                src_ref=_half_of(src, half, c), dst_ref=_half_of(dst.at[mine], half, c),
                send_sem=send_sems.at[4 * k + j], recv_sem=recv_sems.at[4 * k + j],
                device_id=(tx, ty, c), device_id_type=MESH))
    return copies


def _gather_ici(shards):
    n = len(shards)

    def body(*refs):
        copies = _gather_copies(refs[:n], refs[n:2 * n], refs[2 * n], refs[2 * n + 1])
        for cp in copies:
            cp.start()
        for cp in copies:
            cp.wait()

    return pl.pallas_call(
        body, name="weights_gather_ici",
        in_specs=[ANY] * n, out_specs=[ANY] * n,
        out_shape=[jax.ShapeDtypeStruct((4,) + s.shape, s.dtype) for s in shards],
        scratch_shapes=[pltpu.SemaphoreType.DMA((4 * n,)), pltpu.SemaphoreType.DMA((4 * n,))],
    )(*shards)


HBM_SPEC = pl.BlockSpec(memory_space=pltpu.HBM)
SEM_SPEC = pl.BlockSpec(memory_space=pltpu.SEMAPHORE)
IN_FLIGHT = pltpu.SideEffectType.DATAFLOW_SIDE_EFFECTING


def _gather_ici_start(shards, after):
    n = len(shards)

    def body(*refs):
        for cp in _gather_copies(refs[:n], refs[n:2 * n], refs[2 * n + 1], refs[2 * n + 2]):
            cp.start()
        token = refs[-1]
        token[...] = jnp.zeros_like(token)

    lands = [lax.empty((4,) + s.shape, s.dtype) for s in shards]
    out = pl.pallas_call(
        body, name="weights_gather_start",
        out_shape=(pltpu.SemaphoreType.DMA((4 * n,)), pltpu.SemaphoreType.DMA((4 * n,)),
                   *[pltpu.HBM(s.shape, s.dtype) for s in shards], *[pltpu.HBM(l.shape, l.dtype) for l in lands],
                   jax.ShapeDtypeStruct((8, LANES), F32)),
        in_specs=[HBM_SPEC] * (2 * n) + [ANY],
        out_specs=(SEM_SPEC, SEM_SPEC, *[HBM_SPEC] * (2 * n), pl.BlockSpec(memory_space=pltpu.VMEM)),
        input_output_aliases={k: 2 + k for k in range(2 * n)},
        compiler_params=pltpu.CompilerParams(has_side_effects=IN_FLIGHT),
    )(*[pltpu.with_memory_space_constraint(a, pltpu.HBM) for a in list(shards) + lands], after)
    return out[0], out[1], out[2:2 + n], out[2 + n:2 + 2 * n], out[-1]


def _gather_ici_wait(send_sems, recv_sems, sources, lands, after):
    n = len(sources)

    def body(*refs):
        for cp in _gather_copies(refs[:n], refs[n:2 * n], refs[2 * n], refs[2 * n + 1]):
            cp.wait_send()
            cp.wait_recv()

    out = pl.pallas_call(
        body, name="weights_gather_wait",
        out_shape=[pltpu.HBM(a.shape, a.dtype) for a in list(sources) + list(lands)],
        in_specs=[HBM_SPEC] * (2 * n) + [SEM_SPEC, SEM_SPEC, ANY], out_specs=[HBM_SPEC] * (2 * n),
        input_output_aliases={k: k for k in range(2 * n)},
        compiler_params=pltpu.CompilerParams(has_side_effects=IN_FLIGHT),
    )(*sources, *lands, send_sems, recv_sems, after)
    return out[n:]


def _gather_pair(gathered):
    n = len(gathered)

    def body(*refs):
        outs = refs[n:2 * n]
        send_sems, recv_sems = refs[2 * n:]
        x, y, c, chips = _place()
        sends = []
        for k in range(n):
            half = gathered[k].shape[1] // 2
            for j, (tx, ty) in enumerate(chips):
                piece = _half_of(outs[k].at[2 * tx + ty], half, c)
                sends.append(pltpu.make_async_remote_copy(
                    src_ref=piece, dst_ref=piece, send_sem=send_sems.at[k, j], recv_sem=recv_sems.at[k, j],
                    device_id=(x, y, 1 - c), device_id_type=MESH))
        for cp in sends:
            cp.start()
        for cp in sends:
            cp.wait()

    return pl.pallas_call(
        body, name="weights_gather_pair",
        in_specs=[ANY] * n, out_specs=[ANY] * n,
        out_shape=[jax.ShapeDtypeStruct(g.shape, g.dtype) for g in gathered],
        input_output_aliases={k: k for k in range(n)},
        scratch_shapes=[pltpu.SemaphoreType.DMA((n, 3)), pltpu.SemaphoreType.DMA((n, 3))],
    )(*gathered)


def _pair_exchange(grads, tag):
    n = len(grads)

    def body(*refs):
        ins, outs = refs[:n], refs[n:2 * n]
        send_sems, recv_sems = refs[2 * n:]
        x, y, c, _ = _place()
        copies = []
        for k in range(n):
            half = grads[k].shape[1] // 2
            other = ins[k].at[:, pl.ds(pl.multiple_of((1 - c) * half, 8), half), :]
            copies.append(pltpu.make_async_remote_copy(
                src_ref=other, dst_ref=outs[k], send_sem=send_sems.at[k], recv_sem=recv_sems.at[k],
                device_id=(x, y, 1 - c), device_id_type=MESH))
        for cp in copies:
            cp.start()
        for cp in copies:
            cp.wait()

    return pl.pallas_call(
        body, name=f"grads_pair_exchange_{tag}",
        in_specs=[ANY] * n, out_specs=[ANY] * n,
        out_shape=[jax.ShapeDtypeStruct((4, g.shape[1] // 2, g.shape[2]), F32) for g in grads],
        scratch_shapes=[pltpu.SemaphoreType.DMA((n,)), pltpu.SemaphoreType.DMA((n,))],
    )(*grads)


def _scatter_copies(ins, outs, send_sems, recv_sems):
    x, y, c, chips = _place()
    return [pltpu.make_async_remote_copy(
        src_ref=ins[k].at[2 * tx + ty], dst_ref=outs[k].at[j], send_sem=send_sems.at[3 * k + j],
        recv_sem=recv_sems.at[3 * k + j], device_id=(tx, ty, c), device_id_type=MESH)
        for k in range(len(ins)) for j, (tx, ty) in enumerate(chips)]


def _chip_scatter(halves):
    n = len(halves)

    def body(*refs):
        copies = _scatter_copies(refs[:n], refs[n:2 * n], refs[2 * n], refs[2 * n + 1])
        for cp in copies:
            cp.start()
        for cp in copies:
            cp.wait()

    return pl.pallas_call(
        body, name="grads_chip_scatter",
        in_specs=[ANY] * n, out_specs=[ANY] * n,
        out_shape=[jax.ShapeDtypeStruct((3,) + hv.shape[1:], hv.dtype) for hv in halves],
        scratch_shapes=[pltpu.SemaphoreType.DMA((3 * n,)), pltpu.SemaphoreType.DMA((3 * n,))],
    )(*halves)


def _chip_scatter_start(halves):
    n = len(halves)

    def body(*refs):
        for cp in _scatter_copies(refs[:n], refs[n:2 * n], refs[2 * n], refs[2 * n + 1]):
            cp.start()
        token = refs[-1]
        token[...] = jnp.zeros_like(token)

    lands = [lax.empty((3,) + hv.shape[1:], hv.dtype) for hv in halves]
    out = pl.pallas_call(
        body, name="grads_chip_scatter_start",
        out_shape=(pltpu.SemaphoreType.DMA((3 * n,)), pltpu.SemaphoreType.DMA((3 * n,)),
                   *[pltpu.HBM(a.shape, a.dtype) for a in list(halves) + lands], jax.ShapeDtypeStruct((8, LANES), F32)),
        in_specs=[HBM_SPEC] * (2 * n),
        out_specs=(SEM_SPEC, SEM_SPEC, *[HBM_SPEC] * (2 * n), pl.BlockSpec(memory_space=pltpu.VMEM)),
        input_output_aliases={k: 2 + k for k in range(2 * n)},
        compiler_params=pltpu.CompilerParams(has_side_effects=IN_FLIGHT),
    )(*[pltpu.with_memory_space_constraint(a, pltpu.HBM) for a in list(halves) + lands])
    return out[0], out[1], out[2:2 + n], out[2 + n:2 + 2 * n], out[-1]


def _chip_scatter_wait(send_sems, recv_sems, sources, lands, after):
    n = len(sources)

    def body(*refs):
        for cp in _scatter_copies(refs[:n], refs[n:2 * n], refs[2 * n], refs[2 * n + 1]):
            cp.wait_send()
            cp.wait_recv()

    out = pl.pallas_call(
        body, name="grads_chip_scatter_wait",
        out_shape=[pltpu.HBM(a.shape, a.dtype) for a in list(sources) + list(lands)],
        in_specs=[HBM_SPEC] * (2 * n) + [SEM_SPEC, SEM_SPEC, ANY], out_specs=[HBM_SPEC] * (2 * n),
        input_output_aliases={k: k for k in range(2 * n)},
        compiler_params=pltpu.CompilerParams(has_side_effects=IN_FLIGHT),
    )(*sources, *lands, send_sems, recv_sems, after)
    return out[n:]


def _pair_share(finals):
    n = len(finals)

    def body(*refs):
        ins, outs = refs[:n], refs[n:2 * n]
        send_sems, recv_sems = refs[2 * n:]
        x, y, c, _ = _place()
        copies = [pltpu.make_async_remote_copy(
            src_ref=ins[k], dst_ref=outs[k], send_sem=send_sems.at[k], recv_sem=recv_sems.at[k],
            device_id=(x, y, 1 - c), device_id_type=MESH) for k in range(n)]
        for cp in copies:
            cp.start()
        for cp in copies:
            cp.wait()

    return pl.pallas_call(
        body, name="grads_pair_share",
        in_specs=[ANY] * n, out_specs=[ANY] * n,
        out_shape=[jax.ShapeDtypeStruct(fv.shape, F32) for fv in finals],
        scratch_shapes=[pltpu.SemaphoreType.DMA((n,)), pltpu.SemaphoreType.DMA((n,))],
    )(*finals)


def _small_all_reduce(buf):
    rows, c_ = buf.shape

    def body(in_ref, out_ref, pair_buf, slots, send_sems, recv_sems):
        x, y, c, chips = _place()
        mine = 2 * x + y
        pair = pltpu.make_async_remote_copy(
            src_ref=in_ref, dst_ref=pair_buf, send_sem=send_sems.at[0], recv_sem=recv_sems.at[0],
            device_id=(x, y, 1 - c), device_id_type=MESH)
        pair.start()
        pair.wait()
        slots[mine] = in_ref[...] + pair_buf[...]
        sends = [pltpu.make_async_remote_copy(
            src_ref=slots.at[mine], dst_ref=slots.at[mine], send_sem=send_sems.at[1 + j], recv_sem=recv_sems.at[1 + j],
            device_id=(tx, ty, c), device_id_type=MESH) for j, (tx, ty) in enumerate(chips)]
        for cp in sends:
            cp.start()
        for j, (tx, ty) in enumerate(chips):
            pltpu.make_async_remote_copy(
                src_ref=slots.at[mine], dst_ref=slots.at[2 * tx + ty], send_sem=send_sems.at[1 + j],
                recv_sem=recv_sems.at[1 + j], device_id=(tx, ty, c), device_id_type=MESH).wait()
        out_ref[...] = ((slots[0] + slots[1]) + slots[2]) + slots[3]

    vm = pl.BlockSpec(memory_space=pltpu.VMEM)
    return pl.pallas_call(
        body, name="small_all_reduce", in_specs=[vm], out_specs=vm,
        out_shape=jax.ShapeDtypeStruct((rows, c_), F32),
        scratch_shapes=[pltpu.VMEM((rows, c_), F32), pltpu.VMEM((4, rows, c_), F32),
                        pltpu.SemaphoreType.DMA((4,)), pltpu.SemaphoreType.DMA((4,))],
        compiler_params=pltpu.CompilerParams(vmem_limit_bytes=VMEM_LIMIT_V7X),
    )(buf)


def _pair_reduce(grads, core, tag):
    recv = _pair_exchange(grads, tag)
    return [_pair_sum(g, r, core, f"pair_sum_{tag}_{k}") for k, (g, r) in enumerate(zip(grads, recv))]


def _chip_reduce(halves, recv, chip, tag):
    return [_chip_sum(hv, r, chip, f"chip_sum_{tag}_{k}") for k, ((hv, _), r) in enumerate(zip(halves, recv))]


PACK_COLS = 1024


def _pack(arrays):
    flat = jnp.concatenate([a.reshape(-1).astype(F32) for a in arrays])
    rows = -(-flat.shape[0] // PACK_COLS)
    rows = -(-rows // 8) * 8
    return jnp.pad(flat, (0, rows * PACK_COLS - flat.shape[0])).reshape(rows, PACK_COLS)


def _unpack(buf, shapes):
    flat = buf.reshape(-1)
    out, at = [], 0
    for shp in shapes:
        size = math.prod(shp)
        out.append(flat[at:at + size].reshape(shp))
        at += size
    return out


def kernel(x, mix_norm_g, ffn_norm_g, gm_w_in, gm_ln_g, gm_ln_b, gm_w_s, gm_b_s, gm_w_out, fox_w_qkvf, fox_b_f, fox_w_o, ffn_w_gate, ffn_w_up, ffn_conv_w, ffn_conv_b, ffn_w_down, final_norm_g, loss_target, m_mix_norm_g, m_ffn_norm_g, m_gm_w_in, m_gm_ln_g, m_gm_ln_b, m_gm_w_s, m_gm_b_s, m_gm_w_out, m_fox_w_qkvf, m_fox_b_f, m_fox_w_o, m_ffn_w_gate, m_ffn_w_up, m_ffn_conv_w, m_ffn_conv_b, m_ffn_w_down, m_final_norm_g, v_mix_norm_g, v_ffn_norm_g, v_gm_w_in, v_gm_ln_g, v_gm_ln_b, v_gm_w_s, v_gm_b_s, v_gm_w_out, v_fox_w_qkvf, v_fox_b_f, v_fox_w_o, v_ffn_w_gate, v_ffn_w_up, v_ffn_conv_w, v_ffn_conv_b, v_ffn_w_down, v_final_norm_g):
    _, s_len, d = x.shape
    e = gm_ln_g.shape[1]
    f = ffn_conv_b.shape[1]
    n_head = fox_b_f.shape[1]
    n_pair = n_head // 2
    gd = e // GM_GROUPS
    qkvf_cols = fox_w_qkvf.shape[2]
    assert d == n_head * HEAD_DIM and d % (2 * LANES) == 0 and s_len % 512 == 0 and gd % LANES == 0
    assert gm_w_s.shape[2] == CHUNK and 4 * qkvf_cols == 3 * d + n_head
    tm = 256
    h0 = x[0]
    target = loss_target[0]

    w_in, w_out4 = _gather_pair(_gather_ici([gm_w_in[0].astype(BF16), gm_w_out[0].astype(BF16)]))
    send_sems, recv_sems, sources, lands, token = _gather_ici_start([
        fox_w_qkvf[0].astype(BF16), fox_w_o[0].astype(BF16), ffn_w_gate.astype(BF16), ffn_w_up.astype(BF16),
        ffn_w_down.astype(BF16), ffn_conv_w], after=w_in)
    w_out = w_out4.reshape(e, d)
    bf_pad = jnp.pad(fox_b_f, ((0, 0), (0, LANES - n_head)))

    tril = jnp.tril(jnp.ones((CHUNK, CHUNK), bool))
    wc = jnp.where(tril[None], gm_w_s[0], 0.0).astype(BF16)
    wct = jnp.transpose(wc, (0, 2, 1))
    bias = jnp.repeat(gm_b_s[0].T, gd, axis=1)
    seg_groups = (jnp.arange(e)[:, None] // gd == jnp.arange(LANES)[None, :]).astype(BF16)
    seg_heads = (jnp.arange(d)[:, None] // HEAD_DIM == jnp.arange(LANES)[None, :]).astype(BF16)
    sel_q = _spare_selectors(d, key_side=False)
    sel_k = _spare_selectors(d, key_side=True)

    h1, a0, hn0, gated0 = _gmlp_fwd(h0, mix_norm_g[0:1] + token[0:1, 0:1], w_in, gm_ln_g, gm_ln_b, wc, bias, w_out, tm)
    qkvf4, wo4, wg_all, wu_all, wd_all, cw4 = _gather_pair(_gather_ici_wait(send_sems, recv_sems, sources, lands, h1))
    qkvf = jnp.transpose(qkvf4, (1, 0, 2)).reshape(d, 4 * qkvf_cols)
    wq, wk, wv = qkvf[:, :d], qkvf[:, d:2 * d], qkvf[:, 2 * d:3 * d]
    wf = jnp.pad(qkvf[:, 3 * d:], ((0, 0), (0, LANES - n_head)))
    wo = wo4.reshape(d, d)
    conv_w_full = jnp.transpose(cw4, (1, 2, 0, 3)).reshape(2, 3, f)
    conv_w8 = jnp.pad(conv_w_full, ((0, 0), (0, 5), (0, 0)))
    h2, fa0, fup0, fhn0, fhid0 = _ffn_fwd(h1, ffn_norm_g[0:1], wg_all, wu_all, wd_all, 0, conv_w8[0], ffn_conv_b[0:1], tm)
    (hn1, qa, qb, kat, kbt, va, vb, vat, vbt, z_f) = _fox_proj_fwd(
        h2, mix_norm_g[1:2], wq, wk, wv, wf, bf_pad, sel_q, sel_k, tm)
    o, qa2, qb2, qat2, qbt2 = _flash_fwd(qa, qb, kat, kbt, va, vb)
    h3 = _oproj_fwd(h2, o, wo, tm)
    h4, fa1, fup1, fhn1, fhid1 = _ffn_fwd(h3, ffn_norm_g[1:2], wg_all, wu_all, wd_all, 1, conv_w8[1], ffn_conv_b[1:2], tm)

    dh4, loss_part, g_final = _loss_head(h4, target, final_norm_g.reshape(1, d), tm)
    dh3, da1, dup1, gcw1, gcb1, gfn1 = _ffn_bwd(h3, dh4, fa1, fup1, ffn_norm_g[1:2], wg_all, wu_all, wd_all, 1,
                                                conv_w8[1], ffn_conv_b[1:2], tm)
    g_gate1 = _wgrad(fhn1, da1, 4, "wgrad_gate_1")
    g_up1 = _wgrad(fhn1, dup1, 4, "wgrad_up_1")
    g_down1 = _wgrad(fhid1, dh4, 1, "wgrad_down_1").reshape(4, f // 4, d)

    doa, dob, doat, dobt = _oproj_bwd(dh3, o, wo, seg_heads, sel_q, tm)
    g_wo = _wgrad(o, dh3, 1, "wgrad_wo").reshape(4, d // 4, d)
    dqt, dkt, dvt, row_sums, col_sums = _flash_bwd(qa2, qb2, qat2, qbt2, kat, kbt, vat, vbt, doa, dob, doat, dobt)
    sums = row_sums[:, 0::8, :] - col_sums[:, N_SPARE::8, :]
    dcum = jnp.pad(sums.reshape(n_head, s_len).T, ((0, 0), (0, LANES - n_head)))
    dfl, g_bf = _forget_bwd(dcum, z_f, tm)
    dh2, dflb, gmn1 = _fox_proj_bwd(h2, dh3, dqt, dkt, dvt, dfl, mix_norm_g[1:2], wq, wk, wv, wf, tm)
    g_q = _wgrad_t(dqt, hn1, "wgrad_q").T
    g_k = _wgrad_t(dkt, hn1, "wgrad_k").T
    g_v = _wgrad_t(dvt, hn1, "wgrad_v").T
    g_f = _wgrad(hn1, dflb, 1, "wgrad_f")[0][:, :n_head]
    g_qkvf = jnp.concatenate([g_q, g_k, g_v, g_f], axis=1).reshape(d, 4, qkvf_cols).transpose(1, 0, 2)

    core = lax.axis_index("c").astype(jnp.int32).reshape(1)
    chip = (2 * lax.axis_index("x") + lax.axis_index("y")).astype(jnp.int32).reshape(1)
    halves_early = _pair_reduce([g_qkvf, g_wo, g_gate1, g_up1, g_down1], core, "early")
    sc_send, sc_recv, sc_src, sc_land, sc_token = _chip_scatter_start([hb for _, hb in halves_early])

    dh1, da0f, dup0, gcw0, gcb0, gfn0 = _ffn_bwd(h1, dh2, fa0, fup0, ffn_norm_g[0:1] + sc_token[0:1, 0:1],
                                                 wg_all, wu_all, wd_all, 0, conv_w8[0], ffn_conv_b[0:1], tm)
    g_gate0 = _wgrad(fhn0, da0f, 4, "wgrad_gate_0")
    g_up0 = _wgrad(fhn0, dup0, 4, "wgrad_up_0")
    g_down0 = _wgrad(fhid0, dh2, 1, "wgrad_down_0").reshape(4, f // 4, d)

    dh0, da0, g_ws, g_bs_t, g_lng, g_lnb, gmn0 = _gmlp_bwd(
        h0, dh1, a0, mix_norm_g[0:1], w_in, gm_ln_g, gm_ln_b, wc, wct, bias, w_out, seg_groups, tm)
    g_win = _wgrad(hn0, da0, 4, "wgrad_gm_in")
    g_wout = _wgrad(gated0, dh1, 1, "wgrad_gm_out").reshape(4, e // 4, d)

    halves_late = _pair_reduce([g_win, g_wout, g_gate0, g_up0, g_down0], core, "late")
    finals_late = _chip_reduce(halves_late, _chip_scatter([hb for _, hb in halves_late]), chip, "late")
    recv_early = _chip_scatter_wait(sc_send, sc_recv, sc_src, sc_land, after=finals_late[0])
    finals = _chip_reduce(halves_early, recv_early, chip, "early") + finals_late
    big = list(zip(finals, _pair_share(finals)))
    r_qkvf, r_wo, r_gate1, r_up1, r_down1, r_win, r_wout, r_gate0, r_up0, r_down0 = big

    small = [jnp.concatenate([gmn0, gmn1]), jnp.concatenate([gfn0, gfn1]), g_lng, g_lnb, g_ws[None],
             g_bs_t[:, :GM_GROUPS].T[None], g_bf[:, :n_head], jnp.stack([gcw0[:3], gcw1[:3]]),
             jnp.concatenate([gcb0, gcb1]), g_final.reshape(d), loss_part[0, :1]]
    small_shapes = [a.shape for a in small]
    reduced = _unpack(_small_all_reduce(_pack(small)), small_shapes)
    (r_mix, r_ffn, r_lng, r_lnb, r_ws, r_bs, r_bf, r_cw_full, r_cb, r_final, r_loss) = reduced
    r_cw = lax.dynamic_slice_in_dim(r_cw_full, chip[0] * (f // 4), f // 4, axis=2)

    def update_big(name, w, m, v, per_layer):
        parts = [_adamw_halves(w[l], m[l], v[l], own, other, core, f"adamw_{name}_{l}")
                 for l, (own, other) in enumerate(per_layer)]
        return tuple(jnp.stack([p[i] for p in parts]) for i in range(4))

    res = {}
    res["gm_w_in"] = update_big("gm_w_in", gm_w_in, m_gm_w_in, v_gm_w_in, [r_win])
    res["gm_w_out"] = update_big("gm_w_out", gm_w_out, m_gm_w_out, v_gm_w_out, [r_wout])
    res["fox_w_qkvf"] = update_big("fox_w_qkvf", fox_w_qkvf, m_fox_w_qkvf, v_fox_w_qkvf, [r_qkvf])
    res["fox_w_o"] = update_big("fox_w_o", fox_w_o, m_fox_w_o, v_fox_w_o, [r_wo])
    res["ffn_w_gate"] = update_big("ffn_w_gate", ffn_w_gate, m_ffn_w_gate, v_ffn_w_gate, [r_gate0, r_gate1])
    res["ffn_w_up"] = update_big("ffn_w_up", ffn_w_up, m_ffn_w_up, v_ffn_w_up, [r_up0, r_up1])
    res["ffn_w_down"] = update_big("ffn_w_down", ffn_w_down, m_ffn_w_down, v_ffn_w_down, [r_down0, r_down1])

    small_names = ["mix_norm_g", "ffn_norm_g", "gm_ln_g", "gm_ln_b", "gm_w_s", "gm_b_s", "fox_b_f", "ffn_conv_w",
                   "ffn_conv_b", "final_norm_g"]
    small_w = [mix_norm_g, ffn_norm_g, gm_ln_g, gm_ln_b, gm_w_s, gm_b_s, fox_b_f, ffn_conv_w, ffn_conv_b, final_norm_g]
    small_m = [m_mix_norm_g, m_ffn_norm_g, m_gm_ln_g, m_gm_ln_b, m_gm_w_s, m_gm_b_s, m_fox_b_f, m_ffn_conv_w,
               m_ffn_conv_b, m_final_norm_g]
    small_v = [v_mix_norm_g, v_ffn_norm_g, v_gm_ln_g, v_gm_ln_b, v_gm_w_s, v_gm_b_s, v_fox_b_f, v_ffn_conv_w,
               v_ffn_conv_b, v_final_norm_g]
    small_g = [r_mix, r_ffn, r_lng, r_lnb, r_ws, r_bs, r_bf, r_cw, r_cb, r_final]
    shapes = [w.shape for w in small_w]
    small_g = [g.reshape(s) for g, s in zip(small_g, shapes)]
    dlt, mn, vn = _adamw(_pack(small_w), _pack(small_m), _pack(small_v), _pack(small_g), "adamw_small")
    for name, g, dl_, m_, v_ in zip(small_names, small_g, _unpack(dlt, shapes), _unpack(mn, shapes), _unpack(vn, shapes)):
        res[name] = (g, dl_, m_, v_)

    order = ["mix_norm_g", "ffn_norm_g", "gm_w_in", "gm_ln_g", "gm_ln_b", "gm_w_s", "gm_b_s", "gm_w_out", "fox_w_qkvf",
             "fox_b_f", "fox_w_o", "ffn_w_gate", "ffn_w_up", "ffn_conv_w", "ffn_conv_b", "ffn_w_down", "final_norm_g"]
    outs = [r_loss.reshape(()), dh0[None]]
    for part in range(4):
        outs += [res[name][part] for name in order]
    return tuple(outs)
```

```python
import functools
import math

import jax
import jax.numpy as jnp
from jax import lax
from jax.experimental import pallas as pl
from jax.experimental.pallas import tpu as pltpu

F32 = jnp.float32
BF16 = jnp.bfloat16

RMS_EPS = 1e-6
LN_EPS = 1e-5
CHUNK = 128
GM_GROUPS = 8
HEAD_DIM = 64
LANES = 128
ATT_BLOCK = 256
ATT_CHUNK = 1024
VMEM_LIMIT_V7X = 56 * 1024 * 1024

ADAM_LR = 0.001
ADAM_B1 = 0.9
ADAM_B2 = 0.999
ADAM_EPS = 1e-08
ADAM_WD = 0.01
ADAM_STEP = 10

MESH = pl.DeviceIdType.MESH
ANY = pl.BlockSpec(memory_space=pl.ANY)
NEG_BIG = -1e30


def _params(n_grid):
    return pltpu.CompilerParams(dimension_semantics=("arbitrary",) * n_grid, vmem_limit_bytes=VMEM_LIMIT_V7X)


def _dot(a, b):
    return jnp.dot(a, b, preferred_element_type=F32)


def _dot_nt(a, b):
    return lax.dot_general(a, b, (((1,), (1,)), ((), ())), preferred_element_type=F32)


def _dot_tn(a, b):
    return lax.dot_general(a, b, (((0,), (0,)), ((), ())), preferred_element_type=F32)


def _split3(x):
    hi = x.astype(BF16)
    r = x - hi.astype(F32)
    mid = r.astype(BF16)
    lo = (r - mid.astype(F32)).astype(BF16)
    return hi, mid, lo


def _dot3_lhs(x, m):
    hi, mid, lo = _split3(x)
    return _dot(hi, m) + _dot(mid, m) + _dot(lo, m)


def _dot3_rhs(m, x):
    hi, mid, lo = _split3(x)
    return _dot(m, hi) + _dot(m, mid) + _dot(m, lo)


def _load_once(pairs, sem):
    @pl.when(pl.program_id(0) == 0)
    def _():
        copies = [pltpu.make_async_copy(src, dst, sem.at[k]) for k, (src, dst) in enumerate(pairs)]
        for cp in copies:
            cp.start()
        for cp in copies:
            cp.wait()


def _rms_fwd(x, g):
    r = lax.rsqrt(jnp.mean(x * x, axis=-1, keepdims=True) + RMS_EPS)
    xhat = x * r
    return xhat, r, xhat * g


def _rms_bwd(dy, xhat, r, g):
    w = dy * g
    dx = r * (w - xhat * jnp.mean(w * xhat, axis=-1, keepdims=True))
    return dx, dy * xhat


def _gelu_parts(a):
    c = math.sqrt(2.0 / math.pi)
    a2 = a * a
    t = jnp.tanh(c * (a + 0.044715 * a * a2))
    z = 0.5 * a * (1.0 + t)
    dz = 0.5 * (1.0 + t) + 0.5 * a * (1.0 - t * t) * (c * (1.0 + 3.0 * 0.044715 * a2))
    return z, dz


def _sigmoid(x):
    return 1.0 / (1.0 + jnp.exp(-x))


def _gmlp_core(a, lng, lnb, wc_ref, bias, n_chunk, gd):
    e = a.shape[1] // 2
    z, dz = _gelu_parts(a)
    u = z[:, :e]
    v = z[:, e:]
    mu = jnp.mean(v, axis=-1, keepdims=True)
    vc = v - mu
    rstd = lax.rsqrt(jnp.mean(vc * vc, axis=-1, keepdims=True) + LN_EPS)
    vhat = vc * rstd
    vln = vhat * lng + lnb
    vlb = vln.astype(BF16)
    rows = []
    for ci in range(n_chunk):
        cols = []
        for g in range(GM_GROUPS):
            blk = vlb[ci * CHUNK:(ci + 1) * CHUNK, g * gd:(g + 1) * gd]
            cols.append(_dot(wc_ref[g], blk))
        rows.append(jnp.concatenate(cols, axis=1) + bias)
    s = rows[0] if n_chunk == 1 else jnp.concatenate(rows, axis=0)
    return dz, u, vhat, rstd, vlb, s


def _gmlp_fwd(h, g_mix, w_in, lng, lnb, wc, bias, w_out, tm):
    s_len, d = h.shape
    n_p, _, w = w_in.shape
    e = w_out.shape[0]
    gd = e // GM_GROUPS
    n_chunk = tm // CHUNK

    def body(h_ref, g_ref, win_hbm, lng_ref, lnb_ref, wc_ref, bias_ref, wout_hbm,
             hout_ref, a_ref, hn_ref, gated_ref, win_v, wout_v, sem):
        _load_once([(win_hbm, win_v), (wout_hbm, wout_v)], sem)
        x = h_ref[...]
        _, _, y = _rms_fwd(x, g_ref[...])
        hn = y.astype(BF16)
        hn_ref[...] = hn
        for p in range(n_p):
            a_ref[:, p * w:(p + 1) * w] = _dot(hn, win_v[p])
        _, u, _, _, _, s = _gmlp_core(a_ref[...], lng_ref[...], lnb_ref[...], wc_ref, bias_ref[...], n_chunk, gd)
        gated = (u * s).astype(BF16)
        gated_ref[...] = gated
        hout_ref[...] = x + _dot(gated, wout_v[...])

    row = lambda i: (i, 0)
    const2 = lambda i: (0, 0)
    return pl.pallas_call(
        body, name="gmlp_fwd", grid=(s_len // tm,),
        in_specs=[pl.BlockSpec((tm, d), row), pl.BlockSpec((1, d), const2), ANY,
                  pl.BlockSpec((1, e), const2), pl.BlockSpec((1, e), const2),
                  pl.BlockSpec(wc.shape, lambda i: (0, 0, 0)), pl.BlockSpec((CHUNK, e), const2), ANY],
        out_specs=[pl.BlockSpec((tm, d), row), pl.BlockSpec((tm, 2 * e), row),
                   pl.BlockSpec((tm, d), row), pl.BlockSpec((tm, e), row)],
        out_shape=[jax.ShapeDtypeStruct((s_len, d), F32), jax.ShapeDtypeStruct((s_len, 2 * e), F32),
                   jax.ShapeDtypeStruct((s_len, d), BF16), jax.ShapeDtypeStruct((s_len, e), BF16)],
        scratch_shapes=[pltpu.VMEM(w_in.shape, BF16), pltpu.VMEM(w_out.shape, BF16), pltpu.SemaphoreType.DMA((2,))],
        compiler_params=_params(1),
    )(h, g_mix, w_in, lng, lnb, wc, bias, w_out)


def _gmlp_bwd(h, dh, a, g_mix, w_in, lng, lnb, wc, wct, bias, w_out, seg, tm):
    s_len, d = h.shape
    n_p, _, w = w_in.shape
    e = w_out.shape[0]
    gd = e // GM_GROUPS
    n_chunk = tm // CHUNK
    n_blk = s_len // tm

    def body(h_ref, dh_ref, a_ref, g_ref, win_hbm, lng_ref, lnb_ref, wc_ref, wct_ref, bias_ref, wout_hbm, seg_ref,
             dhin_ref, da_ref, gws_ref, gbs_ref, glng_ref, glnb_ref, gmix_ref, win_v, wout_v, dsum, sem):
        i = pl.program_id(0)
        _load_once([(win_hbm, win_v), (wout_hbm, wout_v)], sem)

        @pl.when(i == 0)
        def _():
            gws_ref[...] = jnp.zeros_like(gws_ref)
            glng_ref[...] = jnp.zeros_like(glng_ref)
            glnb_ref[...] = jnp.zeros_like(glnb_ref)
            gmix_ref[...] = jnp.zeros_like(gmix_ref)
            dsum[...] = jnp.zeros_like(dsum)

        x = h_ref[...]
        dh_v = dh_ref[...]
        g = g_ref[...]
        lng_v = lng_ref[...]
        xhat, r, _ = _rms_fwd(x, g)
        dz_da, u, vhat, rstd, vlb, s = _gmlp_core(a_ref[...], lng_v, lnb_ref[...], wc_ref, bias_ref[...], n_chunk, gd)
        dg = _dot_nt(dh_v.astype(BF16), wout_v[...])
        du = dg * s
        ds = dg * u
        dsb = ds.astype(BF16)
        rows = []
        ds_acc = None
        for ci in range(n_chunk):
            lo, hi = ci * CHUNK, (ci + 1) * CHUNK
            cols = []
            for gi in range(GM_GROUPS):
                d_blk = dsb[lo:hi, gi * gd:(gi + 1) * gd]
                gws_ref[gi] += _dot_nt(d_blk, vlb[lo:hi, gi * gd:(gi + 1) * gd])
                cols.append(_dot(wct_ref[gi], d_blk))
            rows.append(jnp.concatenate(cols, axis=1))
            ds_acc = ds[lo:hi] if ds_acc is None else ds_acc + ds[lo:hi]
        dsum[...] += ds_acc
        dvln = rows[0] if n_chunk == 1 else jnp.concatenate(rows, axis=0)
        glng_ref[...] += jnp.sum(dvln * vhat, axis=0, keepdims=True)
        glnb_ref[...] += jnp.sum(dvln, axis=0, keepdims=True)
        dvhat = dvln * lng_v
        dv = rstd * (dvhat - jnp.mean(dvhat, axis=-1, keepdims=True)
                     - vhat * jnp.mean(dvhat * vhat, axis=-1, keepdims=True))
        da = jnp.concatenate([du, dv], axis=1) * dz_da
        dab = da.astype(BF16)
        da_ref[...] = dab
        dhn = _dot_nt(dab[:, :w], win_v[0])
        for p in range(1, n_p):
            dhn += _dot_nt(dab[:, p * w:(p + 1) * w], win_v[p])
        dx, gg = _rms_bwd(dhn, xhat, r, g)
        gmix_ref[...] += jnp.sum(gg, axis=0, keepdims=True)
        dhin_ref[...] = dh_v + dx

        @pl.when(i == n_blk - 1)
        def _():
            tril = lax.broadcasted_iota(jnp.int32, (CHUNK, CHUNK), 0) >= lax.broadcasted_iota(jnp.int32, (CHUNK, CHUNK), 1)
            for gi in range(GM_GROUPS):
                gws_ref[gi] = jnp.where(tril, gws_ref[gi], 0.0)
            gbs_ref[...] = _dot3_lhs(dsum[...], seg_ref[...])

    row = lambda i: (i, 0)
    const2 = lambda i: (0, 0)
    const3 = lambda i: (0, 0, 0)
    return pl.pallas_call(
        body, name="gmlp_bwd", grid=(n_blk,),
        in_specs=[pl.BlockSpec((tm, d), row), pl.BlockSpec((tm, d), row), pl.BlockSpec((tm, 2 * e), row),
                  pl.BlockSpec((1, d), const2), ANY, pl.BlockSpec((1, e), const2), pl.BlockSpec((1, e), const2),
                  pl.BlockSpec(wc.shape, const3), pl.BlockSpec(wct.shape, const3), pl.BlockSpec((CHUNK, e), const2),
                  ANY, pl.BlockSpec((e, LANES), const2)],
        out_specs=[pl.BlockSpec((tm, d), row), pl.BlockSpec((tm, 2 * e), row), pl.BlockSpec(wc.shape, const3),
                   pl.BlockSpec((CHUNK, LANES), const2), pl.BlockSpec((1, e), const2), pl.BlockSpec((1, e), const2),
                   pl.BlockSpec((1, d), const2)],
        out_shape=[jax.ShapeDtypeStruct((s_len, d), F32), jax.ShapeDtypeStruct((s_len, 2 * e), BF16),
                   jax.ShapeDtypeStruct(wc.shape, F32), jax.ShapeDtypeStruct((CHUNK, LANES), F32),
                   jax.ShapeDtypeStruct((1, e), F32), jax.ShapeDtypeStruct((1, e), F32), jax.ShapeDtypeStruct((1, d), F32)],
        scratch_shapes=[pltpu.VMEM(w_in.shape, BF16), pltpu.VMEM(w_out.shape, BF16), pltpu.VMEM((CHUNK, e), F32),
                        pltpu.SemaphoreType.DMA((2,))],
        compiler_params=_params(1),
    )(h, dh, a, g_mix, w_in, lng, lnb, wc, wct, bias, w_out, seg)


def _shift_down(a, k, fill):
    tm = a.shape[0]
    out = pltpu.roll(a, k, 0)
    rid = lax.broadcasted_iota(jnp.int32, a.shape, 0)
    for j in range(k):
        out = jnp.where(rid == j, fill[8 - k + j:8 - k + j + 1, :], out)
    return out


def _shift_up(a, k, fill):
    tm = a.shape[0]
    out = pltpu.roll(a, tm - k, 0)
    rid = lax.broadcasted_iota(jnp.int32, a.shape, 0)
    for j in range(k):
        out = jnp.where(rid == tm - k + j, fill[j:j + 1, :], out)
    return out


def _ffn_fwd(h, g_norm, wg_all, wu_all, wd_all, layer, conv_w, conv_b, tm):
    s_len, d = h.shape
    n_p = wg_all.shape[0]
    fq = wg_all.shape[3]
    f = n_p * fq

    def body(h_ref, g_ref, wg_hbm, wu_hbm, wd_hbm, cw_ref, cb_ref,
             hout_ref, a_ref, up_ref, hn_ref, hid_ref, wg_v, wu_v, wd_v, carry, sem):
        i = pl.program_id(0)
        _load_once([(wg_hbm.at[:, layer], wg_v), (wu_hbm.at[:, layer], wu_v), (wd_hbm.at[:, layer], wd_v)], sem)

        @pl.when(i == 0)
        def _():
            carry[...] = jnp.zeros_like(carry)

        x = h_ref[...]
        _, _, y = _rms_fwd(x, g_ref[...])
        hn = y.astype(BF16)
        hn_ref[...] = hn
        for p in range(n_p):
            a_ref[:, p * fq:(p + 1) * fq] = _dot(hn, wg_v[p])
            up_ref[:, p * fq:(p + 1) * fq] = _dot(hn, wu_v[p])
        a = a_ref[...]
        prev = carry[...]
        am1 = _shift_down(a, 1, prev)
        am2 = _shift_down(a, 2, prev)
        carry[...] = a[tm - 8:tm, :]
        cw = cw_ref[...]
        ac = cb_ref[...] + am2 * cw[0:1, :]
        ac = ac + am1 * cw[1:2, :]
        ac = ac + a * cw[2:3, :]
        hid = (ac * _sigmoid(ac) * up_ref[...]).astype(BF16)
        hid_ref[...] = hid
        y2 = _dot(hid[:, :fq], wd_v[0])
        for p in range(1, n_p):
            y2 += _dot(hid[:, p * fq:(p + 1) * fq], wd_v[p])
        hout_ref[...] = x + y2

    row = lambda i: (i, 0)
    const2 = lambda i: (0, 0)
    return pl.pallas_call(
        body, name=f"ffn_fwd_{layer}", grid=(s_len // tm,),
        in_specs=[pl.BlockSpec((tm, d), row), pl.BlockSpec((1, d), const2), ANY, ANY, ANY,
                  pl.BlockSpec((8, f), const2), pl.BlockSpec((1, f), const2)],
        out_specs=[pl.BlockSpec((tm, d), row), pl.BlockSpec((tm, f), row), pl.BlockSpec((tm, f), row),
                   pl.BlockSpec((tm, d), row), pl.BlockSpec((tm, f), row)],
        out_shape=[jax.ShapeDtypeStruct((s_len, d), F32), jax.ShapeDtypeStruct((s_len, f), F32),
                   jax.ShapeDtypeStruct((s_len, f), F32), jax.ShapeDtypeStruct((s_len, d), BF16),
                   jax.ShapeDtypeStruct((s_len, f), BF16)],
        scratch_shapes=[pltpu.VMEM((n_p, d, fq), BF16), pltpu.VMEM((n_p, d, fq), BF16), pltpu.VMEM((n_p, fq, d), BF16),
                        pltpu.VMEM((8, f), F32), pltpu.SemaphoreType.DMA((3,))],
        compiler_params=_params(1),
    )(h, g_norm, wg_all, wu_all, wd_all, conv_w, conv_b)


def _ffn_bwd(h, dh, a, up, g_norm, wg_all, wu_all, wd_all, layer, conv_w, conv_b, tm):
    s_len, d = h.shape
    n_p = wg_all.shape[0]
    fq = wg_all.shape[3]
    f = n_p * fq
    n_blk = s_len // tm
    t8 = tm // 8

    def body(h_ref, dh_ref, a_ref, ahalo_ref, up_ref, g_ref, wg_hbm, wu_hbm, wd_hbm, cw_ref, cb_ref,
             dhin_ref, da_ref, dup_ref, gcw_ref, gcb_ref, gn_ref, wg_v, wu_v, wd_v, carry, sem):
        i = pl.program_id(0)
        _load_once([(wg_hbm.at[:, layer], wg_v), (wu_hbm.at[:, layer], wu_v), (wd_hbm.at[:, layer], wd_v)], sem)

        @pl.when(i == 0)
        def _():
            carry[...] = jnp.zeros_like(carry)
            gcw_ref[...] = jnp.zeros_like(gcw_ref)
            gcb_ref[...] = jnp.zeros_like(gcb_ref)
            gn_ref[...] = jnp.zeros_like(gn_ref)

        x = h_ref[...]
        dh_v = dh_ref[...]
        g = g_ref[...]
        xhat, r, _ = _rms_fwd(x, g)
        a = a_ref[...]
        up_v = up_ref[...]
        prev = jnp.where(i == n_blk - 1, 0.0, ahalo_ref[...])
        am1 = _shift_down(a, 1, prev)
        am2 = _shift_down(a, 2, prev)
        cw = cw_ref[...]
        ac = cb_ref[...] + am2 * cw[0:1, :]
        ac = ac + am1 * cw[1:2, :]
        ac = ac + a * cw[2:3, :]
        sg = _sigmoid(ac)
        sil = ac * sg
        dhb = dh_v.astype(BF16)
        dhid = jnp.concatenate([_dot_nt(dhb, wd_v[p]) for p in range(n_p)], axis=1)
        dup = dhid * sil
        dac = dhid * up_v * (sg * (1.0 + ac * (1.0 - sg)))
        gcb_ref[...] += jnp.sum(dac, axis=0, keepdims=True)
        gcw_ref[0:1, :] += jnp.sum(dac * am2, axis=0, keepdims=True)
        gcw_ref[1:2, :] += jnp.sum(dac * am1, axis=0, keepdims=True)
        gcw_ref[2:3, :] += jnp.sum(dac * a, axis=0, keepdims=True)
        nxt = carry[...]
        dp1 = _shift_up(dac, 1, nxt)
        dp2 = _shift_up(dac, 2, nxt)
        carry[...] = dac[0:8, :]
        da = dac * cw[2:3, :] + dp1 * cw[1:2, :] + dp2 * cw[0:1, :]
        dab = da.astype(BF16)
        dupb = dup.astype(BF16)
        da_ref[...] = dab
        dup_ref[...] = dupb
        dhn = _dot_nt(dab[:, :fq], wg_v[0]) + _dot_nt(dupb[:, :fq], wu_v[0])
        for p in range(1, n_p):
            dhn += _dot_nt(dab[:, p * fq:(p + 1) * fq], wg_v[p]) + _dot_nt(dupb[:, p * fq:(p + 1) * fq], wu_v[p])
        dx, gg = _rms_bwd(dhn, xhat, r, g)
        gn_ref[...] += jnp.sum(gg, axis=0, keepdims=True)
        dhin_ref[...] = dh_v + dx

    rev = lambda i: (n_blk - 1 - i, 0)
    halo = lambda i: (jnp.maximum((n_blk - 1 - i) * t8 - 1, 0), 0)
    const2 = lambda i: (0, 0)
    return pl.pallas_call(
        body, name=f"ffn_bwd_{layer}", grid=(n_blk,),
        in_specs=[pl.BlockSpec((tm, d), rev), pl.BlockSpec((tm, d), rev), pl.BlockSpec((tm, f), rev),
                  pl.BlockSpec((8, f), halo), pl.BlockSpec((tm, f), rev), pl.BlockSpec((1, d), const2), ANY, ANY, ANY,
                  pl.BlockSpec((8, f), const2), pl.BlockSpec((1, f), const2)],
        out_specs=[pl.BlockSpec((tm, d), rev), pl.BlockSpec((tm, f), rev), pl.BlockSpec((tm, f), rev),
                   pl.BlockSpec((8, f), const2), pl.BlockSpec((1, f), const2), pl.BlockSpec((1, d), const2)],
        out_shape=[jax.ShapeDtypeStruct((s_len, d), F32), jax.ShapeDtypeStruct((s_len, f), BF16),
                   jax.ShapeDtypeStruct((s_len, f), BF16), jax.ShapeDtypeStruct((8, f), F32),
                   jax.ShapeDtypeStruct((1, f), F32), jax.ShapeDtypeStruct((1, d), F32)],
        scratch_shapes=[pltpu.VMEM((n_p, d, fq), BF16), pltpu.VMEM((n_p, d, fq), BF16), pltpu.VMEM((n_p, fq, d), BF16),
                        pltpu.VMEM((8, f), F32), pltpu.SemaphoreType.DMA((3,))],
        compiler_params=_params(1),
    )(h, dh, a, a, up, g_norm, wg_all, wu_all, wd_all, conv_w, conv_b)


def _even_head_lanes(shape, axis):
    return (lax.broadcasted_iota(jnp.int32, shape, axis) & HEAD_DIM) == 0


def _pair_select(lo, hi, shape):
    return jnp.where(lax.broadcasted_iota(jnp.int32, shape, 1) < HEAD_DIM, lo, hi)


def _causal(row0, col0, shape):
    return row0 + lax.broadcasted_iota(jnp.int32, shape, 0) >= col0 + lax.broadcasted_iota(jnp.int32, shape, 1)


N_SPARE = 3


def _spare_selectors(d, key_side):
    lane = jnp.arange(d)[None, :]
    row = jnp.arange(N_SPARE * LANES)[:, None]
    head, part = row % LANES, row // LANES
    off = N_SPARE if key_side else 0
    sel_a = ((head % 2 == 0) & (lane == LANES * (head // 2) + HEAD_DIM + off + part)).astype(F32)
    sel_b = ((head % 2 == 1) & (lane == LANES * (head // 2) + off + part)).astype(F32)
    sign = -1.0 if key_side else 1.0
    ones_off = 0 if key_side else N_SPARE
    in_pair = jnp.arange(d)[None, :] % LANES
    ones_a = ((in_pair >= HEAD_DIM + ones_off) & (in_pair < HEAD_DIM + ones_off + N_SPARE)).astype(F32)
    ones_b = ((in_pair >= ones_off) & (in_pair < ones_off + N_SPARE)).astype(F32)
    return (sign * sel_a).astype(BF16), (sign * sel_b).astype(BF16), ones_a, ones_b


def _parts(x):
    return jnp.concatenate(_split3(x), axis=1)


def _fox_proj_fwd(h, g_norm, wq, wk, wv, wf, bf, sel_q, sel_k, tm):
    s_len, d = h.shape
    sq_a, sq_b, oq_a, oq_b = sel_q
    sk_a, sk_b, ok_a, ok_b = sel_k

    def body(h_ref, g_ref, wq_hbm, wk_hbm, wv_hbm, wf_ref, bf_ref, sqa_ref, sqb_ref, oqa_ref, oqb_ref,
             ska_ref, skb_ref, oka_ref, okb_ref,
             hn_ref, qa_ref, qb_ref, kat_ref, kbt_ref, va_ref, vb_ref, vat_ref, vbt_ref, z_ref,
             wq_v, wk_v, wv_v, total, sem):
        i = pl.program_id(0)
        _load_once([(wq_hbm, wq_v), (wk_hbm, wk_v), (wv_hbm, wv_v)], sem)

        @pl.when(i == 0)
        def _():
            total[...] = jnp.zeros_like(total)

        x = h_ref[...]
        _, _, y = _rms_fwd(x, g_ref[...])
        hn = y.astype(BF16)
        hn_ref[...] = hn
        z = _dot(hn, wf_ref[...]) + bf_ref[...]
        z_ref[...] = z
        logf = jnp.minimum(z, 0.0) - jnp.log(1.0 + jnp.exp(-jnp.abs(z)))
        tri = (lax.broadcasted_iota(jnp.int32, (tm, tm), 0) >= lax.broadcasted_iota(jnp.int32, (tm, tm), 1))
        cum = _dot3_rhs(jnp.where(tri, 1.0, 0.0).astype(BF16), logf) + total[0:1, :]
        total[...] = jnp.broadcast_to(cum[tm - 1:tm, :], total.shape)
        parts = _parts(cum)

        even = _even_head_lanes((tm, d), 1)
        q = _dot(hn, wq_v[...]) * (HEAD_DIM ** -0.5)
        qa_ref[...] = jnp.where(even, q, _dot(parts, sqa_ref[...]) + oqa_ref[...]).astype(BF16)
        qb_ref[...] = jnp.where(even, _dot(parts, sqb_ref[...]) + oqb_ref[...], q).astype(BF16)
        k = _dot(hn, wk_v[...])
        ka = jnp.where(even, k, _dot(parts, ska_ref[...]) + oka_ref[...])
        kb = jnp.where(even, _dot(parts, skb_ref[...]) + okb_ref[...], k)
        kat_ref[...] = ka.T.astype(BF16)
        kbt_ref[...] = kb.T.astype(BF16)
        v = _dot(hn, wv_v[...])
        va = jnp.where(even, v, oka_ref[...])
        vb = jnp.where(even, okb_ref[...], v)
        va_ref[...] = va.astype(BF16)
        vb_ref[...] = vb.astype(BF16)
        vat_ref[...] = va.T.astype(BF16)
        vbt_ref[...] = vb.T.astype(BF16)

    row = lambda i: (i, 0)
    col = lambda i: (0, i)
    const2 = lambda i: (0, 0)
    sd = jax.ShapeDtypeStruct((s_len, d), BF16)
    ds_ = jax.ShapeDtypeStruct((d, s_len), BF16)
    rs, cs = pl.BlockSpec((tm, d), row), pl.BlockSpec((d, tm), col)
    sel = pl.BlockSpec((N_SPARE * LANES, d), const2)
    one = pl.BlockSpec((1, d), const2)
    return pl.pallas_call(
        body, name="fox_proj_fwd", grid=(s_len // tm,),
        in_specs=[rs, one, ANY, ANY, ANY, pl.BlockSpec((d, LANES), const2), pl.BlockSpec((1, LANES), const2),
                  sel, sel, one, one, sel, sel, one, one],
        out_specs=[rs, rs, rs, cs, cs, rs, rs, cs, cs, pl.BlockSpec((tm, LANES), row)],
        out_shape=[sd, sd, sd, ds_, ds_, sd, sd, ds_, ds_, jax.ShapeDtypeStruct((s_len, LANES), F32)],
        scratch_shapes=[pltpu.VMEM((d, d), BF16), pltpu.VMEM((d, d), BF16), pltpu.VMEM((d, d), BF16),
                        pltpu.VMEM((8, LANES), F32), pltpu.SemaphoreType.DMA((3,))],
        compiler_params=_params(1),
    )(h, g_norm, wq, wk, wv, wf, bf, sq_a, sq_b, oq_a, oq_b, sk_a, sk_b, ok_a, ok_b)


def _spare_cols(x, base):
    xf = x[:, base:base + N_SPARE].astype(F32)
    return xf[:, 0:1] + xf[:, 1:2] + xf[:, 2:3]


def _with_query_term(x, term, base):
    lane = lax.broadcasted_iota(jnp.int32, x.shape, 1)
    hi, mid, lo = _split3(term)
    out = jnp.where(lane == base, hi.astype(F32), x)
    out = jnp.where(lane == base + 1, mid.astype(F32), out)
    out = jnp.where(lane == base + 2, lo.astype(F32), out)
    return jnp.where((lane >= base + N_SPARE) & (lane < base + 2 * N_SPARE), 1.0, out)


def _flash_fwd(qa, qb, kat, kbt, va, vb):
    s_len, d = qa.shape
    sub = ATT_BLOCK
    n_sub = 2 if s_len % (2 * sub) == 0 else 1
    t = n_sub * sub
    w = min(ATT_CHUNK, s_len)
    n_pair = d // LANES
    n_q = s_len // t
    bases = (HEAD_DIM, 0)
    chains = [(r, hh) for r in range(n_sub) for hh in range(2)]

    def body(qa_ref, qb_ref, kat_ref, kbt_ref, va_ref, vb_ref, o_ref, qa2_ref, qb2_ref, qat2_ref, qbt2_ref):
        i = pl.program_id(1)
        q_refs = (qa_ref, qb_ref)
        qs = [q_refs[hh][r * sub:(r + 1) * sub, :] for r, hh in chains]
        kts = (kat_ref, kbt_ref)
        vs = (va_ref, vb_ref)

        def step(kb, carry, masked, width=w):
            off = pl.multiple_of(kb * w, w)
            cols = pl.ds(off, width)
            scores = [_dot(qs[c], kts[hh][:, cols]) for c, (r, hh) in enumerate(chains)]
            probs, stats = [], []
            for c, (r, hh) in enumerate(chains):
                m, _ = carry[c]
                s = scores[c]
                if masked:
                    s = jnp.where(_causal(i * t + r * sub, off, (sub, width)), s, NEG_BIG)
                m_new = jnp.maximum(m, jnp.max(s, axis=1, keepdims=True))
                probs.append(jnp.exp(s - m_new).astype(BF16))
                stats.append((m_new, jnp.exp(m - m_new)))
            return tuple((stats[c][0], carry[c][1] * stats[c][1] + _dot(probs[c], vs[hh][cols, :]))
                         for c, (r, hh) in enumerate(chains))

        init = ((jnp.full((sub, 1), NEG_BIG, F32), jnp.zeros((sub, LANES), F32)),) * len(chains)
        diag = (i * t) // w
        carry = lax.fori_loop(0, diag, lambda kb, c: step(kb, c, False), init)
        carry = step(diag, carry, True)
        for r in range(n_sub):
            outs, q2 = [], []
            for hh in range(2):
                m, acc = carry[2 * r + hh]
                l = acc[:, bases[hh]:bases[hh] + 1]
                outs.append(acc / l)
                term = _spare_cols(qs[2 * r + hh], bases[hh]) - (m + jnp.log(l))
                q2.append(_with_query_term(qs[2 * r + hh].astype(F32), term, bases[hh]))
            rows = slice(r * sub, (r + 1) * sub)
            o_ref[rows, :] = _pair_select(outs[0], outs[1], (sub, LANES))
            qa2_ref[rows, :] = q2[0].astype(BF16)
            qb2_ref[rows, :] = q2[1].astype(BF16)
            qat2_ref[:, rows] = q2[0].T.astype(BF16)
            qbt2_ref[:, rows] = q2[1].T.astype(BF16)

    qblk = pl.BlockSpec((t, LANES), lambda j, i: (i, j))
    qblk_t = pl.BlockSpec((LANES, t), lambda j, i: (j, i))
    whole_t = pl.BlockSpec((LANES, s_len), lambda j, i: (j, 0))
    whole = pl.BlockSpec((s_len, LANES), lambda j, i: (0, j))
    sd = jax.ShapeDtypeStruct((s_len, d), BF16)
    ds_ = jax.ShapeDtypeStruct((d, s_len), BF16)
    return pl.pallas_call(
        body, name="flash_fwd", grid=(n_pair, n_q),
        in_specs=[qblk, qblk, whole_t, whole_t, whole, whole],
        out_specs=[qblk, qblk, qblk, qblk_t, qblk_t],
        out_shape=[jax.ShapeDtypeStruct((s_len, d), F32), sd, sd, ds_, ds_],
        compiler_params=_params(2),
    )(qa, qb, kat, kbt, va, vb)


def _flash_bwd(qa, qb, qat, qbt, kat, kbt, vat, vbt, doa, dob, doat, dobt):
    s_len, d = qa.shape
    sub = ATT_BLOCK
    n_sub = 2 if s_len % (2 * sub) == 0 else 1
    t = n_sub * sub
    w = min(ATT_CHUNK, s_len)
    n_pair = d // LANES
    n_q = s_len // t
    hd = HEAD_DIM
    chains = [(r, hh) for r in range(n_sub) for hh in range(2)]

    def body(qa_ref, qb_ref, qat_ref, qbt_ref, kat_hbm, kbt_hbm, vat_hbm, vbt_hbm, doa_ref, dob_ref, doat_ref, dobt_ref,
             dqt_ref, dkt_ref, dvt_ref, rs_ref, cs_ref,
             kat_v, kbt_v, vat_v, vbt_v, dkt_acc, dvt_acc, cs_acc, sem):
        j = pl.program_id(0)
        i = pl.program_id(1)

        @pl.when(i == 0)
        def _():
            rows = pl.ds(pl.multiple_of(j * LANES, LANES), LANES)
            copies = [pltpu.make_async_copy(src.at[rows, :], dst, sem.at[n]) for n, (src, dst) in enumerate([
                (kat_hbm, kat_v), (kbt_hbm, kbt_v), (vat_hbm, vat_v), (vbt_hbm, vbt_v)])]
            for cp in copies:
                cp.start()
            dkt_acc[...] = jnp.zeros_like(dkt_acc)
            dvt_acc[...] = jnp.zeros_like(dvt_acc)
            cs_acc[...] = jnp.zeros_like(cs_acc)
            for cp in copies:
                cp.wait()

        q_refs, do_refs = (qa_ref, qb_ref), (doa_ref, dob_ref)
        qs = [q_refs[hh][r * sub:(r + 1) * sub, :] for r, hh in chains]
        dos = [do_refs[hh][r * sub:(r + 1) * sub, :] for r, hh in chains]
        own = (slice(0, hd), slice(hd, 2 * hd))
        spare = (slice(hd, hd + 8), slice(0, 8))
        used = (slice(0, hd + 16), slice(0, 2 * hd))
        qts = (qat_ref[used[0], :], qbt_ref[used[1], :])
        dots = (doat_ref[own[0], :], dobt_ref[own[1], :])
        kts, vts = (kat_v, kbt_v), (vat_v, vbt_v)

        def step(kb, carry, masked, width=w):
            off = pl.multiple_of(kb * w, w)
            cols = pl.ds(off, width)
            scores = [_dot(qs[c], kts[hh][:, cols]) for c, (r, hh) in enumerate(chains)]
            dps = [_dot(dos[c], vts[hh][:, cols]) for c, (r, hh) in enumerate(chains)]
            ps, dss = [], []
            for c, (r, hh) in enumerate(chains):
                s = scores[c]
                if masked:
                    s = jnp.where(_causal(i * t + r * sub, off, (sub, width)), s, NEG_BIG)
                p = jnp.exp(s)
                dss.append((p * dps[c]).astype(BF16))
                ps.append(p.astype(BF16))
            out = tuple(carry[c] + _dot_nt(kts[hh][used[hh], cols], dss[c]) for c, (r, hh) in enumerate(chains))
            for hh in range(2):
                p_all = jnp.concatenate([ps[2 * r + hh] for r in range(n_sub)], axis=0)
                ds_all = jnp.concatenate([dss[2 * r + hh] for r in range(n_sub)], axis=0)
                dvt_acc[own[hh], cols] += _dot(dots[hh], p_all)
                with_sums = _dot(qts[hh], ds_all)
                dkt_acc[own[hh], cols] += with_sums[own[hh], :]
                cs_acc[8 * hh:8 * hh + 8, cols] += with_sums[spare[hh], :]
            return out

        diag = (i * t) // w
        init = (jnp.zeros((hd + 16, sub), F32), jnp.zeros((2 * hd, sub), F32)) * n_sub
        carry = lax.fori_loop(0, diag, lambda kb, c: step(kb, c, False), init)
        carry = lax.cond((i * t) % w + t <= w // 2,
                         lambda c: step(diag, c, True, w // 2), lambda c: step(diag, c, True), carry)
        for c, (r, hh) in enumerate(chains):
            at = slice(r * sub, (r + 1) * sub)
            dqt_ref[own[hh], at] = (carry[c][own[hh], :] * (hd ** -0.5)).astype(BF16)
            rs_ref[0, 8 * hh:8 * hh + 8, at] = carry[c][spare[hh], :]

        @pl.when(i == n_q - 1)
        def _():
            dkt_ref[...] = dkt_acc[...].astype(BF16)
            dvt_ref[...] = dvt_acc[...].astype(BF16)
            cs_ref[0] = cs_acc[...]

    qblk = pl.BlockSpec((t, LANES), lambda j, i: (i, j))
    qblk_t = pl.BlockSpec((LANES, t), lambda j, i: (j, i))
    whole_t = pl.BlockSpec((LANES, s_len), lambda j, i: (j, 0))
    ds_ = jax.ShapeDtypeStruct((d, s_len), BF16)
    sums = jax.ShapeDtypeStruct((n_pair, 16, s_len), F32)
    return pl.pallas_call(
        body, name="flash_bwd", grid=(n_pair, n_q),
        in_specs=[qblk, qblk, qblk_t, qblk_t, ANY, ANY, ANY, ANY, qblk, qblk, qblk_t, qblk_t],
        out_specs=[qblk_t, whole_t, whole_t, pl.BlockSpec((1, 16, t), lambda j, i: (j, 0, i)),
                   pl.BlockSpec((1, 16, s_len), lambda j, i: (j, 0, 0))],
        out_shape=[ds_, ds_, ds_, sums, sums],
        scratch_shapes=[pltpu.VMEM((LANES, s_len), BF16), pltpu.VMEM((LANES, s_len), BF16),
                        pltpu.VMEM((LANES, s_len), BF16), pltpu.VMEM((LANES, s_len), BF16),
                        pltpu.VMEM((LANES, s_len), F32), pltpu.VMEM((LANES, s_len), F32),
                        pltpu.VMEM((16, s_len), F32), pltpu.SemaphoreType.DMA((4,))],
        compiler_params=_params(2),
    )(qa, qb, qat, qbt, kat, kbt, vat, vbt, doa, dob, doat, dobt)


def _wgrad_t(at, b, name):
    k, s_len = at.shape
    n = b.shape[1]
    tn, tk, ts = min(n, 1024), min(k, 1024), min(s_len, 1024)

    def body(a_ref, b_ref, o_ref):
        @pl.when(pl.program_id(2) == 0)
        def _():
            o_ref[...] = jnp.zeros_like(o_ref)
        o_ref[...] += _dot(a_ref[...].astype(BF16), b_ref[...].astype(BF16))

    return pl.pallas_call(
        body, name=name, grid=(k // tk, n // tn, s_len // ts),
        in_specs=[pl.BlockSpec((tk, ts), lambda a, b_, c: (a, c)), pl.BlockSpec((ts, tn), lambda a, b_, c: (c, b_))],
        out_specs=pl.BlockSpec((tk, tn), lambda a, b_, c: (a, b_)),
        out_shape=jax.ShapeDtypeStruct((k, n), F32),
        compiler_params=_params(3),
    )(at, b)


def _oproj_bwd(dh, o, wo, seg, sel_q, tm):
    s_len, d = dh.shape
    sq_a, sq_b, _, _ = sel_q

    def body(dh_ref, o_ref, wo_hbm, seg_ref, sqa_ref, sqb_ref, doa_ref, dob_ref, doat_ref, dobt_ref, wo_v, sem):
        _load_once([(wo_hbm, wo_v)], sem)
        do = _dot_nt(dh_ref[...].astype(BF16), wo_v[...])
        parts = _parts(-_dot3_lhs(do * o_ref[...], seg_ref[...]))
        even = _even_head_lanes((tm, d), 1)
        doa = jnp.where(even, do, _dot(parts, sqa_ref[...]))
        dob = jnp.where(even, _dot(parts, sqb_ref[...]), do)
        doa_ref[...] = doa.astype(BF16)
        dob_ref[...] = dob.astype(BF16)
        doat_ref[...] = doa.T.astype(BF16)
        dobt_ref[...] = dob.T.astype(BF16)

    row = lambda i: (i, 0)
    const2 = lambda i: (0, 0)
    rs, cs = pl.BlockSpec((tm, d), row), pl.BlockSpec((d, tm), lambda i: (0, i))
    sel = pl.BlockSpec((N_SPARE * LANES, d), const2)
    sd = jax.ShapeDtypeStruct((s_len, d), BF16)
    ds_ = jax.ShapeDtypeStruct((d, s_len), BF16)
    return pl.pallas_call(
        body, name="oproj_bwd", grid=(s_len // tm,),
        in_specs=[rs, rs, ANY, pl.BlockSpec((d, LANES), const2), sel, sel],
        out_specs=[rs, rs, cs, cs], out_shape=[sd, sd, ds_, ds_],
        scratch_shapes=[pltpu.VMEM((d, d), BF16), pltpu.SemaphoreType.DMA((1,))],
        compiler_params=_params(1),
    )(dh, o, wo, seg, sq_a, sq_b)


def _oproj_fwd(h, o, wo, tm):
    s_len, d = h.shape

    def body(h_ref, o_ref, wo_hbm, hout_ref, wo_v, sem):
        _load_once([(wo_hbm, wo_v)], sem)
        hout_ref[...] = h_ref[...] + _dot(o_ref[...].astype(BF16), wo_v[...])

    row = lambda i: (i, 0)
    return pl.pallas_call(
        body, name="oproj_fwd", grid=(s_len // tm,),
        in_specs=[pl.BlockSpec((tm, d), row), pl.BlockSpec((tm, d), row), ANY],
        out_specs=pl.BlockSpec((tm, d), row),
        out_shape=jax.ShapeDtypeStruct((s_len, d), F32),
        scratch_shapes=[pltpu.VMEM((d, d), BF16), pltpu.SemaphoreType.DMA((1,))],
        compiler_params=_params(1),
    )(h, o, wo)


def _forget_bwd(dcum, z, tm):
    s_len = dcum.shape[0]
    n_blk = s_len // tm

    def body(dc_ref, z_ref, dfl_ref, gb_ref, total):
        i = pl.program_id(0)

        @pl.when(i == 0)
        def _():
            total[...] = jnp.zeros_like(total)
            gb_ref[...] = jnp.zeros_like(gb_ref)

        upper = (lax.broadcasted_iota(jnp.int32, (tm, tm), 0) <= lax.broadcasted_iota(jnp.int32, (tm, tm), 1))
        suffix = _dot3_rhs(jnp.where(upper, 1.0, 0.0).astype(BF16), dc_ref[...]) + total[0:1, :]
        total[...] = jnp.broadcast_to(suffix[0:1, :], total.shape)
        dfl = suffix * _sigmoid(-z_ref[...])
        dfl_ref[...] = dfl
        gb_ref[...] += jnp.sum(dfl, axis=0, keepdims=True)

    rev = lambda i: (n_blk - 1 - i, 0)
    return pl.pallas_call(
        body, name="forget_bwd", grid=(n_blk,),
        in_specs=[pl.BlockSpec((tm, LANES), rev), pl.BlockSpec((tm, LANES), rev)],
        out_specs=[pl.BlockSpec((tm, LANES), rev), pl.BlockSpec((1, LANES), lambda i: (0, 0))],
        out_shape=[jax.ShapeDtypeStruct((s_len, LANES), F32), jax.ShapeDtypeStruct((1, LANES), F32)],
        scratch_shapes=[pltpu.VMEM((8, LANES), F32)],
        compiler_params=_params(1),
    )(dcum, z)


def _fox_proj_bwd(h, dh, dqt, dkt, dvt, dfl, g_norm, wq, wk, wv, wf, tm):
    s_len, d = h.shape

    def body(h_ref, dh_ref, dqt_ref, dkt_ref, dvt_ref, dfl_ref, g_ref, wq_hbm, wk_hbm, wv_hbm, wf_ref,
             dhin_ref, dflb_ref, gn_ref, wq_v, wk_v, wv_v, sem):
        _load_once([(wq_hbm, wq_v), (wk_hbm, wk_v), (wv_hbm, wv_v)], sem)

        @pl.when(pl.program_id(0) == 0)
        def _():
            gn_ref[...] = jnp.zeros_like(gn_ref)

        g = g_ref[...]
        xhat, r, _ = _rms_fwd(h_ref[...], g)
        dflb = dfl_ref[...].astype(BF16)
        dflb_ref[...] = dflb
        from_qkv = (_dot(wq_v[...], dqt_ref[...]) + _dot(wk_v[...], dkt_ref[...])
                    + _dot(wv_v[...], dvt_ref[...]))
        dhn = _dot_nt(dflb, wf_ref[...]) + from_qkv.T
        dx, gg = _rms_bwd(dhn, xhat, r, g)
        gn_ref[...] += jnp.sum(gg, axis=0, keepdims=True)
        dhin_ref[...] = dh_ref[...] + dx

    row = lambda i: (i, 0)
    const2 = lambda i: (0, 0)
    rs = pl.BlockSpec((tm, d), row)
    cs = pl.BlockSpec((d, tm), lambda i: (0, i))
    return pl.pallas_call(
        body, name="fox_proj_bwd", grid=(s_len // tm,),
        in_specs=[rs, rs, cs, cs, cs, pl.BlockSpec((tm, LANES), row), pl.BlockSpec((1, d), const2), ANY, ANY, ANY,
                  pl.BlockSpec((d, LANES), const2)],
        out_specs=[rs, pl.BlockSpec((tm, LANES), row), pl.BlockSpec((1, d), const2)],
        out_shape=[jax.ShapeDtypeStruct((s_len, d), F32), jax.ShapeDtypeStruct((s_len, LANES), BF16),
                   jax.ShapeDtypeStruct((1, d), F32)],
        scratch_shapes=[pltpu.VMEM((d, d), BF16), pltpu.VMEM((d, d), BF16), pltpu.VMEM((d, d), BF16),
                        pltpu.SemaphoreType.DMA((3,))],
        compiler_params=_params(1),
    )(h, dh, dqt, dkt, dvt, dfl, g_norm, wq, wk, wv, wf)


def _loss_head(h, target, g_final, tm):
    s_len, d = h.shape
    n_blk = s_len // tm

    def body(h_ref, t_ref, g_ref, dh_ref, loss_ref, gg_ref, sq):
        i = pl.program_id(0)

        @pl.when(i == 0)
        def _():
            sq[...] = jnp.zeros_like(sq)
            gg_ref[...] = jnp.zeros_like(gg_ref)

        g = g_ref[...]
        xhat, r, y = _rms_fwd(h_ref[...], g)
        err = y - t_ref[...]
        sq[...] += jnp.sum(err * err, axis=0, keepdims=True)
        dx, gg = _rms_bwd(err * (1.0 / d), xhat, r, g)
        gg_ref[...] += jnp.sum(gg, axis=0, keepdims=True)
        dh_ref[...] = dx

        @pl.when(i == n_blk - 1)
        def _():
            loss_ref[...] = jnp.broadcast_to(jnp.sum(sq[...], axis=1, keepdims=True) * (0.5 / d), loss_ref.shape)

    row = lambda i: (i, 0)
    const2 = lambda i: (0, 0)
    return pl.pallas_call(
        body, name="loss_head", grid=(n_blk,),
        in_specs=[pl.BlockSpec((tm, d), row), pl.BlockSpec((tm, d), row), pl.BlockSpec((1, d), const2)],
        out_specs=[pl.BlockSpec((tm, d), row), pl.BlockSpec((1, LANES), const2), pl.BlockSpec((1, d), const2)],
        out_shape=[jax.ShapeDtypeStruct((s_len, d), F32), jax.ShapeDtypeStruct((1, LANES), F32),
                   jax.ShapeDtypeStruct((1, d), F32)],
        scratch_shapes=[pltpu.VMEM((1, d), F32)],
        compiler_params=_params(1),
    )(h, target, g_final)


def _wgrad(x, dy, n_piece, name):
    s_len, k = x.shape
    n = dy.shape[1]
    tn = min(n // n_piece, 1024)
    tk = min(k, 1024)
    ts = min(s_len, 1024)
    per_piece = (n // n_piece) // tn

    def body(x_ref, dy_ref, o_ref):
        @pl.when(pl.program_id(2) == 0)
        def _():
            o_ref[...] = jnp.zeros_like(o_ref)
        o_ref[0] += _dot_tn(x_ref[...].astype(BF16), dy_ref[...].astype(BF16))

    return pl.pallas_call(
        body, name=name, grid=(k // tk, n // tn, s_len // ts),
        in_specs=[pl.BlockSpec((ts, tk), lambda a, b, c: (c, a)), pl.BlockSpec((ts, tn), lambda a, b, c: (c, b))],
        out_specs=pl.BlockSpec((1, tk, tn), lambda a, b, c: (b // per_piece, a, b % per_piece)),
        out_shape=jax.ShapeDtypeStruct((n_piece, k, n // n_piece), F32),
        compiler_params=_params(3),
    )(x, dy)


def _pair_sum(g, recv, core, name):
    n_piece, rows, c = g.shape
    half = rows // 2
    tr = min(half, 512)
    nb = half // tr

    def body(core_ref, g_ref, r_ref, o_ref, ob_ref):
        total = g_ref[...] + r_ref[...]
        o_ref[...] = total
        ob_ref[...] = total.astype(BF16)

    blk = pl.BlockSpec((1, tr, c), lambda p, i, core_ref: (p, i, 0))
    return pl.pallas_call(
        body, name=name,
        out_shape=[jax.ShapeDtypeStruct((n_piece, half, c), F32), jax.ShapeDtypeStruct((n_piece, half, c), BF16)],
        grid_spec=pltpu.PrefetchScalarGridSpec(
            num_scalar_prefetch=1, grid=(n_piece, nb),
            in_specs=[pl.BlockSpec((1, tr, c), lambda p, i, core_ref: (p, core_ref[0] * nb + i, 0)), blk],
            out_specs=[blk, blk]),
        compiler_params=_params(2),
    )(core, g, recv)


def _chip_sum(halves, recv, chip, name):
    _, h, c = halves.shape
    tr = min(h, 512)

    def body(chip_ref, own_ref, r_ref, o_ref):
        o_ref[...] = ((own_ref[0] + r_ref[0].astype(F32)) + r_ref[1].astype(F32)) + r_ref[2].astype(F32)

    return pl.pallas_call(
        body, name=name, out_shape=jax.ShapeDtypeStruct((h, c), F32),
        grid_spec=pltpu.PrefetchScalarGridSpec(
            num_scalar_prefetch=1, grid=(h // tr,),
            in_specs=[pl.BlockSpec((1, tr, c), lambda i, chip_ref: (chip_ref[0], i, 0)),
                      pl.BlockSpec((3, tr, c), lambda i, chip_ref: (0, i, 0))],
            out_specs=pl.BlockSpec((tr, c), lambda i, chip_ref: (i, 0))),
        compiler_params=_params(1),
    )(chip, halves, recv)


def _adamw_math(w, m, v, g):
    m_new = ADAM_B1 * m + (1.0 - ADAM_B1) * g
    v_new = ADAM_B2 * v + (1.0 - ADAM_B2) * (g * g)
    m_hat = m_new / (1.0 - ADAM_B1 ** ADAM_STEP)
    v_hat = v_new / (1.0 - ADAM_B2 ** ADAM_STEP)
    return -ADAM_LR * (m_hat / (jnp.sqrt(v_hat) + ADAM_EPS) + ADAM_WD * w), m_new, v_new


def _adamw(w, m, v, g, name):
    rows, c = w.shape
    tr = min(rows, 256)

    def body(w_ref, m_ref, v_ref, g_ref, d_ref, mo_ref, vo_ref):
        d_ref[...], mo_ref[...], vo_ref[...] = _adamw_math(w_ref[...], m_ref[...], v_ref[...], g_ref[...])

    spec = pl.BlockSpec((tr, c), lambda i: (i, 0))
    shape = jax.ShapeDtypeStruct((rows, c), F32)
    return pl.pallas_call(
        body, name=name, grid=(rows // tr,),
        in_specs=[spec] * 4, out_specs=[spec] * 3, out_shape=[shape] * 3,
        compiler_params=_params(1),
    )(w, m, v, g)


def _adamw_halves(w, m, v, g_own, g_other, core, name):
    rows, c = w.shape
    half = rows // 2
    tr = min(half, 256)
    nb = half // tr

    def body(core_ref, w_ref, m_ref, v_ref, own_ref, other_ref, g_ref, d_ref, mo_ref, vo_ref):
        mine = (pl.program_id(0) // nb) == core_ref[0]
        g = jnp.where(mine, own_ref[...], other_ref[...])
        g_ref[...] = g
        d_ref[...], mo_ref[...], vo_ref[...] = _adamw_math(w_ref[...], m_ref[...], v_ref[...], g)

    spec = pl.BlockSpec((tr, c), lambda i, core_ref: (i, 0))
    own = pl.BlockSpec((tr, c), lambda i, core_ref: (jnp.clip(i - core_ref[0] * nb, 0, nb - 1), 0))
    other = pl.BlockSpec((tr, c), lambda i, core_ref: (jnp.clip(i - (1 - core_ref[0]) * nb, 0, nb - 1), 0))
    shape = jax.ShapeDtypeStruct((rows, c), F32)
    return pl.pallas_call(
        body, name=name, out_shape=[shape] * 4,
        grid_spec=pltpu.PrefetchScalarGridSpec(
            num_scalar_prefetch=1, grid=(rows // tr,),
            in_specs=[spec, spec, spec, own, other], out_specs=[spec] * 4),
        compiler_params=_params(1),
    )(core, w, m, v, g_own, g_other)


def _place():
    x, y, c = lax.axis_index("x"), lax.axis_index("y"), lax.axis_index("c")
    chips = [(1 - x, y), (x, 1 - y), (1 - x, 1 - y)]
    return x, y, c, chips


def _half_of(ref, half, which):
    start = which * half
    if half % 8 == 0:
        start = pl.multiple_of(start, 8)
    return ref.at[pl.ds(start, half)]


def _gather_copies(ins, outs, send_sems, recv_sems):
    x, y, c, chips = _place()
    mine = 2 * x + y
    copies = []
    for k, (src, dst) in enumerate(zip(ins, outs)):
        half = src.shape[0] // 2
        copies.append(pltpu.make_async_remote_copy(
            src_ref=src, dst_ref=dst.at[mine], send_sem=send_sems.at[4 * k + 3], recv_sem=recv_sems.at[4 * k + 3],
            device_id=(x, y, 1 - c), device_id_type=MESH))
        for j, (tx, ty) in enumerate(chips):
            copies.append(pltpu.make_async_remote_copy(
                src_ref=_half_of(src, half, c), dst_ref=_half_of(dst.at[mine], half, c),
                send_sem=send_sems.at[4 * k + j], recv_sem=recv_sems.at[4 * k + j],
                device_id=(tx, ty, c), device_id_type=MESH))
    return copies


def _gather_ici(shards):
    n = len(shards)

    def body(*refs):
        copies = _gather_copies(refs[:n], refs[n:2 * n], refs[2 * n], refs[2 * n + 1])
        for cp in copies:
            cp.start()
        for cp in copies:
            cp.wait()

    return pl.pallas_call(
        body, name="weights_gather_ici",
        in_specs=[ANY] * n, out_specs=[ANY] * n,
        out_shape=[jax.ShapeDtypeStruct((4,) + s.shape, s.dtype) for s in shards],
        scratch_shapes=[pltpu.SemaphoreType.DMA((4 * n,)), pltpu.SemaphoreType.DMA((4 * n,))],
    )(*shards)


HBM_SPEC = pl.BlockSpec(memory_space=pltpu.HBM)
SEM_SPEC = pl.BlockSpec(memory_space=pltpu.SEMAPHORE)
IN_FLIGHT = pltpu.SideEffectType.DATAFLOW_SIDE_EFFECTING


def _gather_ici_start(shards, after):
    n = len(shards)

    def body(*refs):
        for cp in _gather_copies(refs[:n], refs[n:2 * n], refs[2 * n + 1], refs[2 * n + 2]):
            cp.start()
        token = refs[-1]
        token[...] = jnp.zeros_like(token)

    lands = [lax.empty((4,) + s.shape, s.dtype) for s in shards]
    out = pl.pallas_call(
        body, name="weights_gather_start",
        out_shape=(pltpu.SemaphoreType.DMA((4 * n,)), pltpu.SemaphoreType.DMA((4 * n,)),
                   *[pltpu.HBM(s.shape, s.dtype) for s in shards], *[pltpu.HBM(l.shape, l.dtype) for l in lands],
                   jax.ShapeDtypeStruct((8, LANES), F32)),
        in_specs=[HBM_SPEC] * (2 * n) + [ANY],
        out_specs=(SEM_SPEC, SEM_SPEC, *[HBM_SPEC] * (2 * n), pl.BlockSpec(memory_space=pltpu.VMEM)),
        input_output_aliases={k: 2 + k for k in range(2 * n)},
        compiler_params=pltpu.CompilerParams(has_side_effects=IN_FLIGHT),
    )(*[pltpu.with_memory_space_constraint(a, pltpu.HBM) for a in list(shards) + lands], after)
    return out[0], out[1], out[2:2 + n], out[2 + n:2 + 2 * n], out[-1]


def _gather_ici_wait(send_sems, recv_sems, sources, lands, after):
    n = len(sources)

    def body(*refs):
        for cp in _gather_copies(refs[:n], refs[n:2 * n], refs[2 * n], refs[2 * n + 1]):
            cp.wait_send()
            cp.wait_recv()

    out = pl.pallas_call(
        body, name="weights_gather_wait",
        out_shape=[pltpu.HBM(a.shape, a.dtype) for a in list(sources) + list(lands)],
        in_specs=[HBM_SPEC] * (2 * n) + [SEM_SPEC, SEM_SPEC, ANY], out_specs=[HBM_SPEC] * (2 * n),
        input_output_aliases={k: k for k in range(2 * n)},
        compiler_params=pltpu.CompilerParams(has_side_effects=IN_FLIGHT),
    )(*sources, *lands, send_sems, recv_sems, after)
    return out[n:]


def _gather_pair(gathered):
    n = len(gathered)

    def body(*refs):
        outs = refs[n:2 * n]
        send_sems, recv_sems = refs[2 * n:]
        x, y, c, chips = _place()
        sends = []
        for k in range(n):
            half = gathered[k].shape[1] // 2
            for j, (tx, ty) in enumerate(chips):
                piece = _half_of(outs[k].at[2 * tx + ty], half, c)
                sends.append(pltpu.make_async_remote_copy(
                    src_ref=piece, dst_ref=piece, send_sem=send_sems.at[k, j], recv_sem=recv_sems.at[k, j],
                    device_id=(x, y, 1 - c), device_id_type=MESH))
        for cp in sends:
            cp.start()
        for cp in sends:
            cp.wait()

    return pl.pallas_call(
        body, name="weights_gather_pair",
        in_specs=[ANY] * n, out_specs=[ANY] * n,
        out_shape=[jax.ShapeDtypeStruct(g.shape, g.dtype) for g in gathered],
        input_output_aliases={k: k for k in range(n)},
        scratch_shapes=[pltpu.SemaphoreType.DMA((n, 3)), pltpu.SemaphoreType.DMA((n, 3))],
    )(*gathered)


def _pair_exchange(grads, tag):
    n = len(grads)

    def body(*refs):
        ins, outs = refs[:n], refs[n:2 * n]
        send_sems, recv_sems = refs[2 * n:]
        x, y, c, _ = _place()
        copies = []
        for k in range(n):
            half = grads[k].shape[1] // 2
            other = ins[k].at[:, pl.ds(pl.multiple_of((1 - c) * half, 8), half), :]
            copies.append(pltpu.make_async_remote_copy(
                src_ref=other, dst_ref=outs[k], send_sem=send_sems.at[k], recv_sem=recv_sems.at[k],
                device_id=(x, y, 1 - c), device_id_type=MESH))
        for cp in copies:
            cp.start()
        for cp in copies:
            cp.wait()

    return pl.pallas_call(
        body, name=f"grads_pair_exchange_{tag}",
        in_specs=[ANY] * n, out_specs=[ANY] * n,
        out_shape=[jax.ShapeDtypeStruct((4, g.shape[1] // 2, g.shape[2]), F32) for g in grads],
        scratch_shapes=[pltpu.SemaphoreType.DMA((n,)), pltpu.SemaphoreType.DMA((n,))],
    )(*grads)


def _scatter_copies(ins, outs, send_sems, recv_sems):
    x, y, c, chips = _place()
    return [pltpu.make_async_remote_copy(
        src_ref=ins[k].at[2 * tx + ty], dst_ref=outs[k].at[j], send_sem=send_sems.at[3 * k + j],
        recv_sem=recv_sems.at[3 * k + j], device_id=(tx, ty, c), device_id_type=MESH)
        for k in range(len(ins)) for j, (tx, ty) in enumerate(chips)]


def _chip_scatter(halves):
    n = len(halves)

    def body(*refs):
        copies = _scatter_copies(refs[:n], refs[n:2 * n], refs[2 * n], refs[2 * n + 1])
        for cp in copies:
            cp.start()
        for cp in copies:
            cp.wait()

    return pl.pallas_call(
        body, name="grads_chip_scatter",
        in_specs=[ANY] * n, out_specs=[ANY] * n,
        out_shape=[jax.ShapeDtypeStruct((3,) + hv.shape[1:], hv.dtype) for hv in halves],
        scratch_shapes=[pltpu.SemaphoreType.DMA((3 * n,)), pltpu.SemaphoreType.DMA((3 * n,))],
    )(*halves)


def _chip_scatter_start(halves, tag):
    n = len(halves)

    def body(*refs):
        for cp in _scatter_copies(refs[:n], refs[n:2 * n], refs[2 * n], refs[2 * n + 1]):
            cp.start()
        token = refs[-1]
        token[...] = jnp.zeros_like(token)

    lands = [lax.empty((3,) + hv.shape[1:], hv.dtype) for hv in halves]
    out = pl.pallas_call(
        body, name=f"grads_chip_scatter_start_{tag}",
        out_shape=(pltpu.SemaphoreType.DMA((3 * n,)), pltpu.SemaphoreType.DMA((3 * n,)),
                   *[pltpu.HBM(a.shape, a.dtype) for a in list(halves) + lands], jax.ShapeDtypeStruct((8, LANES), F32)),
        in_specs=[HBM_SPEC] * (2 * n),
        out_specs=(SEM_SPEC, SEM_SPEC, *[HBM_SPEC] * (2 * n), pl.BlockSpec(memory_space=pltpu.VMEM)),
        input_output_aliases={k: 2 + k for k in range(2 * n)},
        compiler_params=pltpu.CompilerParams(has_side_effects=IN_FLIGHT),
    )(*[pltpu.with_memory_space_constraint(a, pltpu.HBM) for a in list(halves) + lands])
    return out[0], out[1], out[2:2 + n], out[2 + n:2 + 2 * n], out[-1]


def _chip_scatter_wait(send_sems, recv_sems, sources, lands, after, tag):
    n = len(sources)

    def body(*refs):
        for cp in _scatter_copies(refs[:n], refs[n:2 * n], refs[2 * n], refs[2 * n + 1]):
            cp.wait_send()
            cp.wait_recv()

    out = pl.pallas_call(
        body, name=f"grads_chip_scatter_wait_{tag}",
        out_shape=[pltpu.HBM(a.shape, a.dtype) for a in list(sources) + list(lands)],
        in_specs=[HBM_SPEC] * (2 * n) + [SEM_SPEC, SEM_SPEC] + [ANY] * len(after), out_specs=[HBM_SPEC] * (2 * n),
        input_output_aliases={k: k for k in range(2 * n)},
        compiler_params=pltpu.CompilerParams(has_side_effects=IN_FLIGHT),
    )(*sources, *lands, send_sems, recv_sems, *after)
    return out[n:]


def _pair_share(finals, tag, after):
    n = len(finals)

    def body(*refs):
        ins, outs = refs[:n], refs[n + 1:2 * n + 1]
        send_sems, recv_sems = refs[2 * n + 1:]
        x, y, c, _ = _place()
        copies = [pltpu.make_async_remote_copy(
            src_ref=ins[k], dst_ref=outs[k], send_sem=send_sems.at[k], recv_sem=recv_sems.at[k],
            device_id=(x, y, 1 - c), device_id_type=MESH) for k in range(n)]
        for cp in copies:
            cp.start()
        for cp in copies:
            cp.wait()

    return pl.pallas_call(
        body, name=f"grads_pair_share_{tag}",
        in_specs=[ANY] * (n + 1), out_specs=[ANY] * n,
        out_shape=[jax.ShapeDtypeStruct(fv.shape, F32) for fv in finals],
        scratch_shapes=[pltpu.SemaphoreType.DMA((n,)), pltpu.SemaphoreType.DMA((n,))],
    )(*finals, after)


def _small_all_reduce(buf):
    rows, c_ = buf.shape

    def body(in_ref, out_ref, pair_buf, slots, send_sems, recv_sems):
        x, y, c, chips = _place()
        mine = 2 * x + y
        pair = pltpu.make_async_remote_copy(
            src_ref=in_ref, dst_ref=pair_buf, send_sem=send_sems.at[0], recv_sem=recv_sems.at[0],
            device_id=(x, y, 1 - c), device_id_type=MESH)
        pair.start()
        pair.wait()
        slots[mine] = in_ref[...] + pair_buf[...]
        sends = [pltpu.make_async_remote_copy(
            src_ref=slots.at[mine], dst_ref=slots.at[mine], send_sem=send_sems.at[1 + j], recv_sem=recv_sems.at[1 + j],
            device_id=(tx, ty, c), device_id_type=MESH) for j, (tx, ty) in enumerate(chips)]
        for cp in sends:
            cp.start()
        for j, (tx, ty) in enumerate(chips):
            pltpu.make_async_remote_copy(
                src_ref=slots.at[mine], dst_ref=slots.at[2 * tx + ty], send_sem=send_sems.at[1 + j],
                recv_sem=recv_sems.at[1 + j], device_id=(tx, ty, c), device_id_type=MESH).wait()
        out_ref[...] = ((slots[0] + slots[1]) + slots[2]) + slots[3]

    vm = pl.BlockSpec(memory_space=pltpu.VMEM)
    return pl.pallas_call(
        body, name="small_all_reduce", in_specs=[vm], out_specs=vm,
        out_shape=jax.ShapeDtypeStruct((rows, c_), F32),
        scratch_shapes=[pltpu.VMEM((rows, c_), F32), pltpu.VMEM((4, rows, c_), F32),
                        pltpu.SemaphoreType.DMA((4,)), pltpu.SemaphoreType.DMA((4,))],
        compiler_params=pltpu.CompilerParams(vmem_limit_bytes=VMEM_LIMIT_V7X),
    )(buf)


def _pair_reduce(grads, core, tag):
    recv = _pair_exchange(grads, tag)
    return [_pair_sum(g, r, core, f"pair_sum_{tag}_{k}") for k, (g, r) in enumerate(zip(grads, recv))]


def _chip_reduce(halves, recv, chip, tag):
    return [_chip_sum(hv, r, chip, f"chip_sum_{tag}_{k}") for k, ((hv, _), r) in enumerate(zip(halves, recv))]


PACK_COLS = 1024


def _pack(arrays):
    flat = jnp.concatenate([a.reshape(-1).astype(F32) for a in arrays])
    rows = -(-flat.shape[0] // PACK_COLS)
    rows = -(-rows // 8) * 8
    return jnp.pad(flat, (0, rows * PACK_COLS - flat.shape[0])).reshape(rows, PACK_COLS)


def _unpack(buf, shapes):
    flat = buf.reshape(-1)
    out, at = [], 0
    for shp in shapes:
        size = math.prod(shp)
        out.append(flat[at:at + size].reshape(shp))
        at += size
    return out


def kernel(x, mix_norm_g, ffn_norm_g, gm_w_in, gm_ln_g, gm_ln_b, gm_w_s, gm_b_s, gm_w_out, fox_w_qkvf, fox_b_f, fox_w_o, ffn_w_gate, ffn_w_up, ffn_conv_w, ffn_conv_b, ffn_w_down, final_norm_g, loss_target, m_mix_norm_g, m_ffn_norm_g, m_gm_w_in, m_gm_ln_g, m_gm_ln_b, m_gm_w_s, m_gm_b_s, m_gm_w_out, m_fox_w_qkvf, m_fox_b_f, m_fox_w_o, m_ffn_w_gate, m_ffn_w_up, m_ffn_conv_w, m_ffn_conv_b, m_ffn_w_down, m_final_norm_g, v_mix_norm_g, v_ffn_norm_g, v_gm_w_in, v_gm_ln_g, v_gm_ln_b, v_gm_w_s, v_gm_b_s, v_gm_w_out, v_fox_w_qkvf, v_fox_b_f, v_fox_w_o, v_ffn_w_gate, v_ffn_w_up, v_ffn_conv_w, v_ffn_conv_b, v_ffn_w_down, v_final_norm_g):
    _, s_len, d = x.shape
    e = gm_ln_g.shape[1]
    f = ffn_conv_b.shape[1]
    n_head = fox_b_f.shape[1]
    n_pair = n_head // 2
    gd = e // GM_GROUPS
    qkvf_cols = fox_w_qkvf.shape[2]
    assert d == n_head * HEAD_DIM and d % (2 * LANES) == 0 and s_len % 512 == 0 and gd % LANES == 0
    assert gm_w_s.shape[2] == CHUNK and 4 * qkvf_cols == 3 * d + n_head
    tm = 256
    h0 = x[0]
    target = loss_target[0]

    w_in, w_out4 = _gather_pair(_gather_ici([gm_w_in[0].astype(BF16), gm_w_out[0].astype(BF16)]))
    send_sems, recv_sems, sources, lands, token = _gather_ici_start([
        fox_w_qkvf[0].astype(BF16), fox_w_o[0].astype(BF16), ffn_w_gate.astype(BF16), ffn_w_up.astype(BF16),
        ffn_w_down.astype(BF16), ffn_conv_w], after=w_in)
    w_out = w_out4.reshape(e, d)
    bf_pad = jnp.pad(fox_b_f, ((0, 0), (0, LANES - n_head)))

    tril = jnp.tril(jnp.ones((CHUNK, CHUNK), bool))
    wc = jnp.where(tril[None], gm_w_s[0], 0.0).astype(BF16)
    wct = jnp.transpose(wc, (0, 2, 1))
    bias = jnp.repeat(gm_b_s[0].T, gd, axis=1)
    seg_groups = (jnp.arange(e)[:, None] // gd == jnp.arange(LANES)[None, :]).astype(BF16)
    seg_heads = (jnp.arange(d)[:, None] // HEAD_DIM == jnp.arange(LANES)[None, :]).astype(BF16)
    sel_q = _spare_selectors(d, key_side=False)
    sel_k = _spare_selectors(d, key_side=True)

    h1, a0, hn0, gated0 = _gmlp_fwd(h0, mix_norm_g[0:1] + token[0:1, 0:1], w_in, gm_ln_g, gm_ln_b, wc, bias, w_out, tm)
    qkvf4, wo4, wg_all, wu_all, wd_all, cw4 = _gather_pair(_gather_ici_wait(send_sems, recv_sems, sources, lands, h1))
    qkvf = jnp.transpose(qkvf4, (1, 0, 2)).reshape(d, 4 * qkvf_cols)
    wq, wk, wv = qkvf[:, :d], qkvf[:, d:2 * d], qkvf[:, 2 * d:3 * d]
    wf = jnp.pad(qkvf[:, 3 * d:], ((0, 0), (0, LANES - n_head)))
    wo = wo4.reshape(d, d)
    conv_w_full = jnp.transpose(cw4, (1, 2, 0, 3)).reshape(2, 3, f)
    conv_w8 = jnp.pad(conv_w_full, ((0, 0), (0, 5), (0, 0)))
    h2, fa0, fup0, fhn0, fhid0 = _ffn_fwd(h1, ffn_norm_g[0:1], wg_all, wu_all, wd_all, 0, conv_w8[0], ffn_conv_b[0:1], tm)
    (hn1, qa, qb, kat, kbt, va, vb, vat, vbt, z_f) = _fox_proj_fwd(
        h2, mix_norm_g[1:2], wq, wk, wv, wf, bf_pad, sel_q, sel_k, tm)
    o, qa2, qb2, qat2, qbt2 = _flash_fwd(qa, qb, kat, kbt, va, vb)
    h3 = _oproj_fwd(h2, o, wo, tm)
    h4, fa1, fup1, fhn1, fhid1 = _ffn_fwd(h3, ffn_norm_g[1:2], wg_all, wu_all, wd_all, 1, conv_w8[1], ffn_conv_b[1:2], tm)

    dh4, loss_part, g_final = _loss_head(h4, target, final_norm_g.reshape(1, d), tm)
    dh3, da1, dup1, gcw1, gcb1, gfn1 = _ffn_bwd(h3, dh4, fa1, fup1, ffn_norm_g[1:2], wg_all, wu_all, wd_all, 1,
                                                conv_w8[1], ffn_conv_b[1:2], tm)
    g_gate1 = _wgrad(fhn1, da1, 4, "wgrad_gate_1")
    g_up1 = _wgrad(fhn1, dup1, 4, "wgrad_up_1")
    g_down1 = _wgrad(fhid1, dh4, 1, "wgrad_down_1").reshape(4, f // 4, d)

    doa, dob, doat, dobt = _oproj_bwd(dh3, o, wo, seg_heads, sel_q, tm)
    g_wo = _wgrad(o, dh3, 1, "wgrad_wo").reshape(4, d // 4, d)
    dqt, dkt, dvt, row_sums, col_sums = _flash_bwd(qa2, qb2, qat2, qbt2, kat, kbt, vat, vbt, doa, dob, doat, dobt)
    sums = row_sums[:, 0::8, :] - col_sums[:, N_SPARE::8, :]
    dcum = jnp.pad(sums.reshape(n_head, s_len).T, ((0, 0), (0, LANES - n_head)))
    dfl, g_bf = _forget_bwd(dcum, z_f, tm)
    dh2, dflb, gmn1 = _fox_proj_bwd(h2, dh3, dqt, dkt, dvt, dfl, mix_norm_g[1:2], wq, wk, wv, wf, tm)
    g_q = _wgrad_t(dqt, hn1, "wgrad_q").T
    g_k = _wgrad_t(dkt, hn1, "wgrad_k").T
    g_v = _wgrad_t(dvt, hn1, "wgrad_v").T
    g_f = _wgrad(hn1, dflb, 1, "wgrad_f")[0][:, :n_head]
    g_qkvf = jnp.concatenate([g_q, g_k, g_v, g_f], axis=1).reshape(d, 4, qkvf_cols).transpose(1, 0, 2)

    core = lax.axis_index("c").astype(jnp.int32).reshape(1)
    chip = (2 * lax.axis_index("x") + lax.axis_index("y")).astype(jnp.int32).reshape(1)
    halves_early = _pair_reduce([g_qkvf, g_wo, g_gate1, g_up1, g_down1], core, "early")
    sc_send, sc_recv, sc_src, sc_land, sc_token = _chip_scatter_start([hb for _, hb in halves_early], "early")

    dh1, da0f, dup0, gcw0, gcb0, gfn0 = _ffn_bwd(h1, dh2, fa0, fup0, ffn_norm_g[0:1] + sc_token[0:1, 0:1],
                                                 wg_all, wu_all, wd_all, 0, conv_w8[0], ffn_conv_b[0:1], tm)
    g_gate0 = _wgrad(fhn0, da0f, 4, "wgrad_gate_0")
    g_up0 = _wgrad(fhn0, dup0, 4, "wgrad_up_0")
    g_down0 = _wgrad(fhid0, dh2, 1, "wgrad_down_0").reshape(4, f // 4, d)

    dh0, da0, g_ws, g_bs_t, g_lng, g_lnb, gmn0 = _gmlp_bwd(
        h0, dh1, a0, mix_norm_g[0:1], w_in, gm_ln_g, gm_ln_b, wc, wct, bias, w_out, seg_groups, tm)
    g_win = _wgrad(hn0, da0, 4, "wgrad_gm_in")
    g_wout = _wgrad(gated0, dh1, 1, "wgrad_gm_out").reshape(4, e // 4, d)

    halves_late = _pair_reduce([g_win, g_wout, g_gate0, g_up0, g_down0], core, "late")
    lt_send, lt_recv, lt_src, lt_land, lt_token = _chip_scatter_start([hb for _, hb in halves_late], "late")
    recv_early = _chip_scatter_wait(sc_send, sc_recv, sc_src, sc_land, [halves_late[0][0]], "early")
    finals_early = _chip_reduce(halves_early, recv_early, chip, "early")
    r_qkvf, r_wo, r_gate1, r_up1, r_down1 = zip(finals_early, _pair_share(finals_early, "early", lt_token))

    def update(name, layer, w, m, v, grad, core_):
        return _adamw_halves(w[layer], m[layer], v[layer], grad[0], grad[1], core_, f"adamw_{name}_{layer}")

    u_qkvf = update("fox_w_qkvf", 0, fox_w_qkvf, m_fox_w_qkvf, v_fox_w_qkvf, r_qkvf, core)
    u_wo = update("fox_w_o", 0, fox_w_o, m_fox_w_o, v_fox_w_o, r_wo, core)
    u_gate1 = update("ffn_w_gate", 1, ffn_w_gate, m_ffn_w_gate, v_ffn_w_gate, r_gate1, core)
    u_up1 = update("ffn_w_up", 1, ffn_w_up, m_ffn_w_up, v_ffn_w_up, r_up1, core)
    u_down1 = update("ffn_w_down", 1, ffn_w_down, m_ffn_w_down, v_ffn_w_down, r_down1, core)

    small = [jnp.concatenate([gmn0, gmn1]), jnp.concatenate([gfn0, gfn1]), g_lng, g_lnb, g_ws[None],
             g_bs_t[:, :GM_GROUPS].T[None], g_bf[:, :n_head], jnp.stack([gcw0[:3], gcw1[:3]]),
             jnp.concatenate([gcb0, gcb1]), g_final.reshape(d), loss_part[0, :1] + lt_token[0, :1]]
    small_shapes = [a.shape for a in small]
    reduced = _unpack(_small_all_reduce(_pack(small)), small_shapes)
    (r_mix, r_ffn, r_lng, r_lnb, r_ws, r_bs, r_bf, r_cw_full, r_cb, r_final, r_loss) = reduced
    r_cw = lax.dynamic_slice_in_dim(r_cw_full, chip[0] * (f // 4), f // 4, axis=2)

    res = {}
    small_names = ["mix_norm_g", "ffn_norm_g", "gm_ln_g", "gm_ln_b", "gm_w_s", "gm_b_s", "fox_b_f", "ffn_conv_w",
                   "ffn_conv_b", "final_norm_g"]
    small_w = [mix_norm_g, ffn_norm_g, gm_ln_g, gm_ln_b, gm_w_s, gm_b_s, fox_b_f, ffn_conv_w, ffn_conv_b, final_norm_g]
    small_m = [m_mix_norm_g, m_ffn_norm_g, m_gm_ln_g, m_gm_ln_b, m_gm_w_s, m_gm_b_s, m_fox_b_f, m_ffn_conv_w,
               m_ffn_conv_b, m_final_norm_g]
    small_v = [v_mix_norm_g, v_ffn_norm_g, v_gm_ln_g, v_gm_ln_b, v_gm_w_s, v_gm_b_s, v_fox_b_f, v_ffn_conv_w,
               v_ffn_conv_b, v_final_norm_g]
    small_g = [r_mix, r_ffn, r_lng, r_lnb, r_ws, r_bs, r_bf, r_cw, r_cb, r_final]
    shapes = [w.shape for w in small_w]
    small_g = [g.reshape(s) for g, s in zip(small_g, shapes)]
    dlt, mn, vn = _adamw(_pack(small_w), _pack(small_m), _pack(small_v), _pack(small_g), "adamw_small")
    for name, g, dl_, m_, v_ in zip(small_names, small_g, _unpack(dlt, shapes), _unpack(mn, shapes), _unpack(vn, shapes)):
        res[name] = (g, dl_, m_, v_)

    recv_late = _chip_scatter_wait(lt_send, lt_recv, lt_src, lt_land, [u_down1[1], dlt], "late")
    finals_late = _chip_reduce(halves_late, recv_late, chip, "late")
    r_win, r_wout, r_gate0, r_up0, r_down0 = zip(finals_late, _pair_share(finals_late, "late", lt_token))
    u_gate0 = update("ffn_w_gate", 0, ffn_w_gate, m_ffn_w_gate, v_ffn_w_gate, r_gate0, core)
    u_up0 = update("ffn_w_up", 0, ffn_w_up, m_ffn_w_up, v_ffn_w_up, r_up0, core)
    u_down0 = update("ffn_w_down", 0, ffn_w_down, m_ffn_w_down, v_ffn_w_down, r_down0, core)
    layers = {"gm_w_in": [update("gm_w_in", 0, gm_w_in, m_gm_w_in, v_gm_w_in, r_win, core)],
              "gm_w_out": [update("gm_w_out", 0, gm_w_out, m_gm_w_out, v_gm_w_out, r_wout, core)],
              "fox_w_qkvf": [u_qkvf], "fox_w_o": [u_wo], "ffn_w_gate": [u_gate0, u_gate1],
              "ffn_w_up": [u_up0, u_up1], "ffn_w_down": [u_down0, u_down1]}
    for name, parts in layers.items():
        res[name] = tuple(jnp.stack([p[i] for p in parts]) for i in range(4))

    order = ["mix_norm_g", "ffn_norm_g", "gm_w_in", "gm_ln_g", "gm_ln_b", "gm_w_s", "gm_b_s", "gm_w_out", "fox_w_qkvf",
             "fox_b_f", "fox_w_o", "ffn_w_gate", "ffn_w_up", "ffn_conv_w", "ffn_conv_b", "ffn_w_down", "final_norm_g"]
    outs = [r_loss.reshape(()), dh0[None]]
    for part in range(4):
        outs += [res[name][part] for name in order]
    return tuple(outs)
```

```python
import functools
import math

import jax
import jax.numpy as jnp
from jax import lax
from jax.experimental import pallas as pl
from jax.experimental.pallas import tpu as pltpu

F32 = jnp.float32
BF16 = jnp.bfloat16

RMS_EPS = 1e-6
LN_EPS = 1e-5
CHUNK = 128
GM_GROUPS = 8
HEAD_DIM = 64
LANES = 128
ATT_BLOCK = 256
ATT_CHUNK = 1024
VMEM_LIMIT_V7X = 56 * 1024 * 1024

ADAM_LR = 0.001
ADAM_B1 = 0.9
ADAM_B2 = 0.999
ADAM_EPS = 1e-08
ADAM_WD = 0.01
ADAM_STEP = 10

MESH = pl.DeviceIdType.MESH
ANY = pl.BlockSpec(memory_space=pl.ANY)
NEG_BIG = -1e30


def _params(n_grid):
    return pltpu.CompilerParams(dimension_semantics=("arbitrary",) * n_grid, vmem_limit_bytes=VMEM_LIMIT_V7X)


def _dot(a, b):
    return jnp.dot(a, b, preferred_element_type=F32)


def _dot_nt(a, b):
    return lax.dot_general(a, b, (((1,), (1,)), ((), ())), preferred_element_type=F32)


def _dot_tn(a, b):
    return lax.dot_general(a, b, (((0,), (0,)), ((), ())), preferred_element_type=F32)


def _split3(x):
    hi = x.astype(BF16)
    r = x - hi.astype(F32)
    mid = r.astype(BF16)
    lo = (r - mid.astype(F32)).astype(BF16)
    return hi, mid, lo


def _dot3_lhs(x, m):
    hi, mid, lo = _split3(x)
    return _dot(hi, m) + _dot(mid, m) + _dot(lo, m)


def _dot3_rhs(m, x):
    hi, mid, lo = _split3(x)
    return _dot(m, hi) + _dot(m, mid) + _dot(m, lo)


def _load_once(pairs, sem):
    @pl.when(pl.program_id(0) == 0)
    def _():
        copies = [pltpu.make_async_copy(src, dst, sem.at[k]) for k, (src, dst) in enumerate(pairs)]
        for cp in copies:
            cp.start()
        for cp in copies:
            cp.wait()


def _rms_fwd(x, g):
    r = lax.rsqrt(jnp.mean(x * x, axis=-1, keepdims=True) + RMS_EPS)
    xhat = x * r
    return xhat, r, xhat * g


def _rms_bwd(dy, xhat, r, g):
    w = dy * g
    dx = r * (w - xhat * jnp.mean(w * xhat, axis=-1, keepdims=True))
    return dx, dy * xhat


def _gelu_parts(a):
    c = math.sqrt(2.0 / math.pi)
    a2 = a * a
    t = jnp.tanh(c * (a + 0.044715 * a * a2))
    z = 0.5 * a * (1.0 + t)
    dz = 0.5 * (1.0 + t) + 0.5 * a * (1.0 - t * t) * (c * (1.0 + 3.0 * 0.044715 * a2))
    return z, dz


def _sigmoid(x):
    return 1.0 / (1.0 + jnp.exp(-x))


def _gmlp_core(a, lng, lnb, wc_ref, bias, n_chunk, gd):
    e = a.shape[1] // 2
    z, dz = _gelu_parts(a)
    u = z[:, :e]
    v = z[:, e:]
    mu = jnp.mean(v, axis=-1, keepdims=True)
    vc = v - mu
    rstd = lax.rsqrt(jnp.mean(vc * vc, axis=-1, keepdims=True) + LN_EPS)
    vhat = vc * rstd
    vln = vhat * lng + lnb
    vlb = vln.astype(BF16)
    rows = []
    for ci in range(n_chunk):
        cols = []
        for g in range(GM_GROUPS):
            blk = vlb[ci * CHUNK:(ci + 1) * CHUNK, g * gd:(g + 1) * gd]
            cols.append(_dot(wc_ref[g], blk))
        rows.append(jnp.concatenate(cols, axis=1) + bias)
    s = rows[0] if n_chunk == 1 else jnp.concatenate(rows, axis=0)
    return dz, u, vhat, rstd, vlb, s


def _gmlp_fwd(h, g_mix, w_in, lng, lnb, wc, bias, w_out, tm):
    s_len, d = h.shape
    n_p, _, w = w_in.shape
    e = w_out.shape[0]
    gd = e // GM_GROUPS
    n_chunk = tm // CHUNK

    def body(h_ref, g_ref, win_hbm, lng_ref, lnb_ref, wc_ref, bias_ref, wout_hbm,
             hout_ref, a_ref, hn_ref, gated_ref, win_v, wout_v, sem):
        _load_once([(win_hbm, win_v), (wout_hbm, wout_v)], sem)
        x = h_ref[...]
        _, _, y = _rms_fwd(x, g_ref[...])
        hn = y.astype(BF16)
        hn_ref[...] = hn
        for p in range(n_p):
            a_ref[:, p * w:(p + 1) * w] = _dot(hn, win_v[p])
        _, u, _, _, _, s = _gmlp_core(a_ref[...], lng_ref[...], lnb_ref[...], wc_ref, bias_ref[...], n_chunk, gd)
        gated = (u * s).astype(BF16)
        gated_ref[...] = gated
        hout_ref[...] = x + _dot(gated, wout_v[...])

    row = lambda i: (i, 0)
    const2 = lambda i: (0, 0)
    return pl.pallas_call(
        body, name="gmlp_fwd", grid=(s_len // tm,),
        in_specs=[pl.BlockSpec((tm, d), row), pl.BlockSpec((1, d), const2), ANY,
                  pl.BlockSpec((1, e), const2), pl.BlockSpec((1, e), const2),
                  pl.BlockSpec(wc.shape, lambda i: (0, 0, 0)), pl.BlockSpec((CHUNK, e), const2), ANY],
        out_specs=[pl.BlockSpec((tm, d), row), pl.BlockSpec((tm, 2 * e), row),
                   pl.BlockSpec((tm, d), row), pl.BlockSpec((tm, e), row)],
        out_shape=[jax.ShapeDtypeStruct((s_len, d), F32), jax.ShapeDtypeStruct((s_len, 2 * e), F32),
                   jax.ShapeDtypeStruct((s_len, d), BF16), jax.ShapeDtypeStruct((s_len, e), BF16)],
        scratch_shapes=[pltpu.VMEM(w_in.shape, BF16), pltpu.VMEM(w_out.shape, BF16), pltpu.SemaphoreType.DMA((2,))],
        compiler_params=_params(1),
    )(h, g_mix, w_in, lng, lnb, wc, bias, w_out)


def _gmlp_bwd(h, dh, a, g_mix, w_in, lng, lnb, wc, wct, bias, w_out, seg, tm):
    s_len, d = h.shape
    n_p, _, w = w_in.shape
    e = w_out.shape[0]
    gd = e // GM_GROUPS
    n_chunk = tm // CHUNK
    n_blk = s_len // tm

    def body(h_ref, dh_ref, a_ref, g_ref, win_hbm, lng_ref, lnb_ref, wc_ref, wct_ref, bias_ref, wout_hbm, seg_ref,
             dhin_ref, da_ref, gws_ref, gbs_ref, glng_ref, glnb_ref, gmix_ref, win_v, wout_v, dsum, sem):
        i = pl.program_id(0)
        _load_once([(win_hbm, win_v), (wout_hbm, wout_v)], sem)

        @pl.when(i == 0)
        def _():
            gws_ref[...] = jnp.zeros_like(gws_ref)
            glng_ref[...] = jnp.zeros_like(glng_ref)
            glnb_ref[...] = jnp.zeros_like(glnb_ref)
            gmix_ref[...] = jnp.zeros_like(gmix_ref)
            dsum[...] = jnp.zeros_like(dsum)

        x = h_ref[...]
        dh_v = dh_ref[...]
        g = g_ref[...]
        lng_v = lng_ref[...]
        xhat, r, _ = _rms_fwd(x, g)
        dz_da, u, vhat, rstd, vlb, s = _gmlp_core(a_ref[...], lng_v, lnb_ref[...], wc_ref, bias_ref[...], n_chunk, gd)
        dg = _dot_nt(dh_v.astype(BF16), wout_v[...])
        du = dg * s
        ds = dg * u
        dsb = ds.astype(BF16)
        rows = []
        ds_acc = None
        for ci in range(n_chunk):
            lo, hi = ci * CHUNK, (ci + 1) * CHUNK
            cols = []
            for gi in range(GM_GROUPS):
                d_blk = dsb[lo:hi, gi * gd:(gi + 1) * gd]
                gws_ref[gi] += _dot_nt(d_blk, vlb[lo:hi, gi * gd:(gi + 1) * gd])
                cols.append(_dot(wct_ref[gi], d_blk))
            rows.append(jnp.concatenate(cols, axis=1))
            ds_acc = ds[lo:hi] if ds_acc is None else ds_acc + ds[lo:hi]
        dsum[...] += ds_acc
        dvln = rows[0] if n_chunk == 1 else jnp.concatenate(rows, axis=0)
        glng_ref[...] += jnp.sum(dvln * vhat, axis=0, keepdims=True)
        glnb_ref[...] += jnp.sum(dvln, axis=0, keepdims=True)
        dvhat = dvln * lng_v
        dv = rstd * (dvhat - jnp.mean(dvhat, axis=-1, keepdims=True)
                     - vhat * jnp.mean(dvhat * vhat, axis=-1, keepdims=True))
        da = jnp.concatenate([du, dv], axis=1) * dz_da
        dab = da.astype(BF16)
        da_ref[...] = dab
        dhn = _dot_nt(dab[:, :w], win_v[0])
        for p in range(1, n_p):
            dhn += _dot_nt(dab[:, p * w:(p + 1) * w], win_v[p])
        dx, gg = _rms_bwd(dhn, xhat, r, g)
        gmix_ref[...] += jnp.sum(gg, axis=0, keepdims=True)
        dhin_ref[...] = dh_v + dx

        @pl.when(i == n_blk - 1)
        def _():
            tril = lax.broadcasted_iota(jnp.int32, (CHUNK, CHUNK), 0) >= lax.broadcasted_iota(jnp.int32, (CHUNK, CHUNK), 1)
            for gi in range(GM_GROUPS):
                gws_ref[gi] = jnp.where(tril, gws_ref[gi], 0.0)
            gbs_ref[...] = _dot3_lhs(dsum[...], seg_ref[...])

    row = lambda i: (i, 0)
    const2 = lambda i: (0, 0)
    const3 = lambda i: (0, 0, 0)
    return pl.pallas_call(
        body, name="gmlp_bwd", grid=(n_blk,),
        in_specs=[pl.BlockSpec((tm, d), row), pl.BlockSpec((tm, d), row), pl.BlockSpec((tm, 2 * e), row),
                  pl.BlockSpec((1, d), const2), ANY, pl.BlockSpec((1, e), const2), pl.BlockSpec((1, e), const2),
                  pl.BlockSpec(wc.shape, const3), pl.BlockSpec(wct.shape, const3), pl.BlockSpec((CHUNK, e), const2),
                  ANY, pl.BlockSpec((e, LANES), const2)],
        out_specs=[pl.BlockSpec((tm, d), row), pl.BlockSpec((tm, 2 * e), row), pl.BlockSpec(wc.shape, const3),
                   pl.BlockSpec((CHUNK, LANES), const2), pl.BlockSpec((1, e), const2), pl.BlockSpec((1, e), const2),
                   pl.BlockSpec((1, d), const2)],
        out_shape=[jax.ShapeDtypeStruct((s_len, d), F32), jax.ShapeDtypeStruct((s_len, 2 * e), BF16),
                   jax.ShapeDtypeStruct(wc.shape, F32), jax.ShapeDtypeStruct((CHUNK, LANES), F32),
                   jax.ShapeDtypeStruct((1, e), F32), jax.ShapeDtypeStruct((1, e), F32), jax.ShapeDtypeStruct((1, d), F32)],
        scratch_shapes=[pltpu.VMEM(w_in.shape, BF16), pltpu.VMEM(w_out.shape, BF16), pltpu.VMEM((CHUNK, e), F32),
                        pltpu.SemaphoreType.DMA((2,))],
        compiler_params=_params(1),
    )(h, dh, a, g_mix, w_in, lng, lnb, wc, wct, bias, w_out, seg)


def _shift_down(a, k, fill):
    tm = a.shape[0]
    out = pltpu.roll(a, k, 0)
    rid = lax.broadcasted_iota(jnp.int32, a.shape, 0)
    for j in range(k):
        out = jnp.where(rid == j, fill[8 - k + j:8 - k + j + 1, :], out)
    return out


def _shift_up(a, k, fill):
    tm = a.shape[0]
    out = pltpu.roll(a, tm - k, 0)
    rid = lax.broadcasted_iota(jnp.int32, a.shape, 0)
    for j in range(k):
        out = jnp.where(rid == tm - k + j, fill[j:j + 1, :], out)
    return out


def _ffn_fwd(h, g_norm, wg_all, wu_all, wd_all, layer, conv_w, conv_b, tm):
    s_len, d = h.shape
    n_p = wg_all.shape[0]
    fq = wg_all.shape[3]
    f = n_p * fq

    def body(h_ref, g_ref, wg_hbm, wu_hbm, wd_hbm, cw_ref, cb_ref,
             hout_ref, a_ref, up_ref, hn_ref, hid_ref, wg_v, wu_v, wd_v, carry, sem):
        i = pl.program_id(0)
        _load_once([(wg_hbm.at[:, layer], wg_v), (wu_hbm.at[:, layer], wu_v), (wd_hbm.at[:, layer], wd_v)], sem)

        @pl.when(i == 0)
        def _():
            carry[...] = jnp.zeros_like(carry)

        x = h_ref[...]
        _, _, y = _rms_fwd(x, g_ref[...])
        hn = y.astype(BF16)
        hn_ref[...] = hn
        for p in range(n_p):
            a_ref[:, p * fq:(p + 1) * fq] = _dot(hn, wg_v[p])
            up_ref[:, p * fq:(p + 1) * fq] = _dot(hn, wu_v[p])
        a = a_ref[...]
        prev = carry[...]
        am1 = _shift_down(a, 1, prev)
        am2 = _shift_down(a, 2, prev)
        carry[...] = a[tm - 8:tm, :]
        cw = cw_ref[...]
        ac = cb_ref[...] + am2 * cw[0:1, :]
        ac = ac + am1 * cw[1:2, :]
        ac = ac + a * cw[2:3, :]
        hid = (ac * _sigmoid(ac) * up_ref[...]).astype(BF16)
        hid_ref[...] = hid
        y2 = _dot(hid[:, :fq], wd_v[0])
        for p in range(1, n_p):
            y2 += _dot(hid[:, p * fq:(p + 1) * fq], wd_v[p])
        hout_ref[...] = x + y2

    row = lambda i: (i, 0)
    const2 = lambda i: (0, 0)
    return pl.pallas_call(
        body, name=f"ffn_fwd_{layer}", grid=(s_len // tm,),
        in_specs=[pl.BlockSpec((tm, d), row), pl.BlockSpec((1, d), const2), ANY, ANY, ANY,
                  pl.BlockSpec((8, f), const2), pl.BlockSpec((1, f), const2)],
        out_specs=[pl.BlockSpec((tm, d), row), pl.BlockSpec((tm, f), row), pl.BlockSpec((tm, f), row),
                   pl.BlockSpec((tm, d), row), pl.BlockSpec((tm, f), row)],
        out_shape=[jax.ShapeDtypeStruct((s_len, d), F32), jax.ShapeDtypeStruct((s_len, f), F32),
                   jax.ShapeDtypeStruct((s_len, f), F32), jax.ShapeDtypeStruct((s_len, d), BF16),
                   jax.ShapeDtypeStruct((s_len, f), BF16)],
        scratch_shapes=[pltpu.VMEM((n_p, d, fq), BF16), pltpu.VMEM((n_p, d, fq), BF16), pltpu.VMEM((n_p, fq, d), BF16),
                        pltpu.VMEM((8, f), F32), pltpu.SemaphoreType.DMA((3,))],
        compiler_params=_params(1),
    )(h, g_norm, wg_all, wu_all, wd_all, conv_w, conv_b)


def _ffn_bwd(h, dh, a, up, g_norm, wg_all, wu_all, wd_all, layer, conv_w, conv_b, tm):
    s_len, d = h.shape
    n_p = wg_all.shape[0]
    fq = wg_all.shape[3]
    f = n_p * fq
    n_blk = s_len // tm
    t8 = tm // 8

    def body(h_ref, dh_ref, a_ref, ahalo_ref, up_ref, g_ref, wg_hbm, wu_hbm, wd_hbm, cw_ref, cb_ref,
             dhin_ref, da_ref, dup_ref, gcw_ref, gcb_ref, gn_ref, wg_v, wu_v, wd_v, carry, sem):
        i = pl.program_id(0)
        _load_once([(wg_hbm.at[:, layer], wg_v), (wu_hbm.at[:, layer], wu_v), (wd_hbm.at[:, layer], wd_v)], sem)

        @pl.when(i == 0)
        def _():
            carry[...] = jnp.zeros_like(carry)
            gcw_ref[...] = jnp.zeros_like(gcw_ref)
            gcb_ref[...] = jnp.zeros_like(gcb_ref)
            gn_ref[...] = jnp.zeros_like(gn_ref)

        x = h_ref[...]
        dh_v = dh_ref[...]
        g = g_ref[...]
        xhat, r, _ = _rms_fwd(x, g)
        a = a_ref[...]
        up_v = up_ref[...]
        prev = jnp.where(i == n_blk - 1, 0.0, ahalo_ref[...])
        am1 = _shift_down(a, 1, prev)
        am2 = _shift_down(a, 2, prev)
        cw = cw_ref[...]
        ac = cb_ref[...] + am2 * cw[0:1, :]
        ac = ac + am1 * cw[1:2, :]
        ac = ac + a * cw[2:3, :]
        sg = _sigmoid(ac)
        sil = ac * sg
        dhb = dh_v.astype(BF16)
        dhid = jnp.concatenate([_dot_nt(dhb, wd_v[p]) for p in range(n_p)], axis=1)
        dup = dhid * sil
        dac = dhid * up_v * (sg * (1.0 + ac * (1.0 - sg)))
        gcb_ref[...] += jnp.sum(dac, axis=0, keepdims=True)
        gcw_ref[0:1, :] += jnp.sum(dac * am2, axis=0, keepdims=True)
        gcw_ref[1:2, :] += jnp.sum(dac * am1, axis=0, keepdims=True)
        gcw_ref[2:3, :] += jnp.sum(dac * a, axis=0, keepdims=True)
        nxt = carry[...]
        dp1 = _shift_up(dac, 1, nxt)
        dp2 = _shift_up(dac, 2, nxt)
        carry[...] = dac[0:8, :]
        da = dac * cw[2:3, :] + dp1 * cw[1:2, :] + dp2 * cw[0:1, :]
        dab = da.astype(BF16)
        dupb = dup.astype(BF16)
        da_ref[...] = dab
        dup_ref[...] = dupb
        dhn = _dot_nt(dab[:, :fq], wg_v[0]) + _dot_nt(dupb[:, :fq], wu_v[0])
        for p in range(1, n_p):
            dhn += _dot_nt(dab[:, p * fq:(p + 1) * fq], wg_v[p]) + _dot_nt(dupb[:, p * fq:(p + 1) * fq], wu_v[p])
        dx, gg = _rms_bwd(dhn, xhat, r, g)
        gn_ref[...] += jnp.sum(gg, axis=0, keepdims=True)
        dhin_ref[...] = dh_v + dx

    rev = lambda i: (n_blk - 1 - i, 0)
    halo = lambda i: (jnp.maximum((n_blk - 1 - i) * t8 - 1, 0), 0)
    const2 = lambda i: (0, 0)
    return pl.pallas_call(
        body, name=f"ffn_bwd_{layer}", grid=(n_blk,),
        in_specs=[pl.BlockSpec((tm, d), rev), pl.BlockSpec((tm, d), rev), pl.BlockSpec((tm, f), rev),
                  pl.BlockSpec((8, f), halo), pl.BlockSpec((tm, f), rev), pl.BlockSpec((1, d), const2), ANY, ANY, ANY,
                  pl.BlockSpec((8, f), const2), pl.BlockSpec((1, f), const2)],
        out_specs=[pl.BlockSpec((tm, d), rev), pl.BlockSpec((tm, f), rev), pl.BlockSpec((tm, f), rev),
                   pl.BlockSpec((8, f), const2), pl.BlockSpec((1, f), const2), pl.BlockSpec((1, d), const2)],
        out_shape=[jax.ShapeDtypeStruct((s_len, d), F32), jax.ShapeDtypeStruct((s_len, f), BF16),
                   jax.ShapeDtypeStruct((s_len, f), BF16), jax.ShapeDtypeStruct((8, f), F32),
                   jax.ShapeDtypeStruct((1, f), F32), jax.ShapeDtypeStruct((1, d), F32)],
        scratch_shapes=[pltpu.VMEM((n_p, d, fq), BF16), pltpu.VMEM((n_p, d, fq), BF16), pltpu.VMEM((n_p, fq, d), BF16),
                        pltpu.VMEM((8, f), F32), pltpu.SemaphoreType.DMA((3,))],
        compiler_params=_params(1),
    )(h, dh, a, a, up, g_norm, wg_all, wu_all, wd_all, conv_w, conv_b)


def _even_head_lanes(shape, axis):
    return (lax.broadcasted_iota(jnp.int32, shape, axis) & HEAD_DIM) == 0


def _pair_select(lo, hi, shape):
    return jnp.where(lax.broadcasted_iota(jnp.int32, shape, 1) < HEAD_DIM, lo, hi)


def _causal(row0, col0, shape):
    return row0 + lax.broadcasted_iota(jnp.int32, shape, 0) >= col0 + lax.broadcasted_iota(jnp.int32, shape, 1)


N_SPARE = 3


def _spare_selectors(d, key_side):
    lane = jnp.arange(d)[None, :]
    row = jnp.arange(N_SPARE * LANES)[:, None]
    head, part = row % LANES, row // LANES
    off = N_SPARE if key_side else 0
    sel_a = ((head % 2 == 0) & (lane == LANES * (head // 2) + HEAD_DIM + off + part)).astype(F32)
    sel_b = ((head % 2 == 1) & (lane == LANES * (head // 2) + off + part)).astype(F32)
    sign = -1.0 if key_side else 1.0
    ones_off = 0 if key_side else N_SPARE
    in_pair = jnp.arange(d)[None, :] % LANES
    ones_a = ((in_pair >= HEAD_DIM + ones_off) & (in_pair < HEAD_DIM + ones_off + N_SPARE)).astype(F32)
    ones_b = ((in_pair >= ones_off) & (in_pair < ones_off + N_SPARE)).astype(F32)
    return (sign * sel_a).astype(BF16), (sign * sel_b).astype(BF16), ones_a, ones_b


def _parts(x):
    return jnp.concatenate(_split3(x), axis=1)


def _fox_proj_fwd(h, g_norm, wq, wk, wv, wf, bf, sel_q, sel_k, tm):
    s_len, d = h.shape
    sq_a, sq_b, oq_a, oq_b = sel_q
    sk_a, sk_b, ok_a, ok_b = sel_k

    def body(h_ref, g_ref, wq_hbm, wk_hbm, wv_hbm, wf_ref, bf_ref, sqa_ref, sqb_ref, oqa_ref, oqb_ref,
             ska_ref, skb_ref, oka_ref, okb_ref,
             hn_ref, qa_ref, qb_ref, kat_ref, kbt_ref, va_ref, vb_ref, vat_ref, vbt_ref, z_ref,
             wq_v, wk_v, wv_v, total, sem):
        i = pl.program_id(0)
        _load_once([(wq_hbm, wq_v), (wk_hbm, wk_v), (wv_hbm, wv_v)], sem)

        @pl.when(i == 0)
        def _():
            total[...] = jnp.zeros_like(total)

        x = h_ref[...]
        _, _, y = _rms_fwd(x, g_ref[...])
        hn = y.astype(BF16)
        hn_ref[...] = hn
        z = _dot(hn, wf_ref[...]) + bf_ref[...]
        z_ref[...] = z
        logf = jnp.minimum(z, 0.0) - jnp.log(1.0 + jnp.exp(-jnp.abs(z)))
        tri = (lax.broadcasted_iota(jnp.int32, (tm, tm), 0) >= lax.broadcasted_iota(jnp.int32, (tm, tm), 1))
        cum = _dot3_rhs(jnp.where(tri, 1.0, 0.0).astype(BF16), logf) + total[0:1, :]
        total[...] = jnp.broadcast_to(cum[tm - 1:tm, :], total.shape)
        parts = _parts(cum)

        even = _even_head_lanes((tm, d), 1)
        q = _dot(hn, wq_v[...]) * (HEAD_DIM ** -0.5)
        qa_ref[...] = jnp.where(even, q, _dot(parts, sqa_ref[...]) + oqa_ref[...]).astype(BF16)
        qb_ref[...] = jnp.where(even, _dot(parts, sqb_ref[...]) + oqb_ref[...], q).astype(BF16)
        k = _dot(hn, wk_v[...])
        ka = jnp.where(even, k, _dot(parts, ska_ref[...]) + oka_ref[...])
        kb = jnp.where(even, _dot(parts, skb_ref[...]) + okb_ref[...], k)
        kat_ref[...] = ka.T.astype(BF16)
        kbt_ref[...] = kb.T.astype(BF16)
        v = _dot(hn, wv_v[...])
        va = jnp.where(even, v, oka_ref[...])
        vb = jnp.where(even, okb_ref[...], v)
        va_ref[...] = va.astype(BF16)
        vb_ref[...] = vb.astype(BF16)
        vat_ref[...] = va.T.astype(BF16)
        vbt_ref[...] = vb.T.astype(BF16)

    row = lambda i: (i, 0)
    col = lambda i: (0, i)
    const2 = lambda i: (0, 0)
    sd = jax.ShapeDtypeStruct((s_len, d), BF16)
    ds_ = jax.ShapeDtypeStruct((d, s_len), BF16)
    rs, cs = pl.BlockSpec((tm, d), row), pl.BlockSpec((d, tm), col)
    sel = pl.BlockSpec((N_SPARE * LANES, d), const2)
    one = pl.BlockSpec((1, d), const2)
    return pl.pallas_call(
        body, name="fox_proj_fwd", grid=(s_len // tm,),
        in_specs=[rs, one, ANY, ANY, ANY, pl.BlockSpec((d, LANES), const2), pl.BlockSpec((1, LANES), const2),
                  sel, sel, one, one, sel, sel, one, one],
        out_specs=[rs, rs, rs, cs, cs, rs, rs, cs, cs, pl.BlockSpec((tm, LANES), row)],
        out_shape=[sd, sd, sd, ds_, ds_, sd, sd, ds_, ds_, jax.ShapeDtypeStruct((s_len, LANES), F32)],
        scratch_shapes=[pltpu.VMEM((d, d), BF16), pltpu.VMEM((d, d), BF16), pltpu.VMEM((d, d), BF16),
                        pltpu.VMEM((8, LANES), F32), pltpu.SemaphoreType.DMA((3,))],
        compiler_params=_params(1),
    )(h, g_norm, wq, wk, wv, wf, bf, sq_a, sq_b, oq_a, oq_b, sk_a, sk_b, ok_a, ok_b)


def _spare_cols(x, base):
    xf = x[:, base:base + N_SPARE].astype(F32)
    return xf[:, 0:1] + xf[:, 1:2] + xf[:, 2:3]


def _with_query_term(x, term, base):
    lane = lax.broadcasted_iota(jnp.int32, x.shape, 1)
    hi, mid, lo = _split3(term)
    out = jnp.where(lane == base, hi.astype(F32), x)
    out = jnp.where(lane == base + 1, mid.astype(F32), out)
    out = jnp.where(lane == base + 2, lo.astype(F32), out)
    return jnp.where((lane >= base + N_SPARE) & (lane < base + 2 * N_SPARE), 1.0, out)


def _flash_fwd(qa, qb, kat, kbt, va, vb):
    s_len, d = qa.shape
    sub = ATT_BLOCK
    n_sub = 2 if s_len % (2 * sub) == 0 else 1
    t = n_sub * sub
    w = min(ATT_CHUNK, s_len)
    n_pair = d // LANES
    n_q = s_len // t
    bases = (HEAD_DIM, 0)
    chains = [(r, hh) for r in range(n_sub) for hh in range(2)]

    def body(qa_ref, qb_ref, kat_ref, kbt_ref, va_ref, vb_ref, o_ref, qa2_ref, qb2_ref, qat2_ref, qbt2_ref):
        i = pl.program_id(1)
        q_refs = (qa_ref, qb_ref)
        qs = [q_refs[hh][r * sub:(r + 1) * sub, :] for r, hh in chains]
        kts = (kat_ref, kbt_ref)
        vs = (va_ref, vb_ref)

        def step(kb, carry, masked, width=w):
            off = pl.multiple_of(kb * w, w)
            cols = pl.ds(off, width)
            scores = [_dot(qs[c], kts[hh][:, cols]) for c, (r, hh) in enumerate(chains)]
            probs, stats = [], []
            for c, (r, hh) in enumerate(chains):
                m, _ = carry[c]
                s = scores[c]
                if masked:
                    s = jnp.where(_causal(i * t + r * sub, off, (sub, width)), s, NEG_BIG)
                m_new = jnp.maximum(m, jnp.max(s, axis=1, keepdims=True))
                probs.append(jnp.exp(s - m_new).astype(BF16))
                stats.append((m_new, jnp.exp(m - m_new)))
            return tuple((stats[c][0], carry[c][1] * stats[c][1] + _dot(probs[c], vs[hh][cols, :]))
                         for c, (r, hh) in enumerate(chains))

        init = ((jnp.full((sub, 1), NEG_BIG, F32), jnp.zeros((sub, LANES), F32)),) * len(chains)
        diag = (i * t) // w
        carry = lax.fori_loop(0, diag, lambda kb, c: step(kb, c, False), init)
        carry = step(diag, carry, True)
        for r in range(n_sub):
            outs, q2 = [], []
            for hh in range(2):
                m, acc = carry[2 * r + hh]
                l = acc[:, bases[hh]:bases[hh] + 1]
                outs.append(acc / l)
                term = _spare_cols(qs[2 * r + hh], bases[hh]) - (m + jnp.log(l))
                q2.append(_with_query_term(qs[2 * r + hh].astype(F32), term, bases[hh]))
            rows = slice(r * sub, (r + 1) * sub)
            o_ref[rows, :] = _pair_select(outs[0], outs[1], (sub, LANES))
            qa2_ref[rows, :] = q2[0].astype(BF16)
            qb2_ref[rows, :] = q2[1].astype(BF16)
            qat2_ref[:, rows] = q2[0].T.astype(BF16)
            qbt2_ref[:, rows] = q2[1].T.astype(BF16)

    qblk = pl.BlockSpec((t, LANES), lambda j, i: (i, j))
    qblk_t = pl.BlockSpec((LANES, t), lambda j, i: (j, i))
    whole_t = pl.BlockSpec((LANES, s_len), lambda j, i: (j, 0))
    whole = pl.BlockSpec((s_len, LANES), lambda j, i: (0, j))
    sd = jax.ShapeDtypeStruct((s_len, d), BF16)
    ds_ = jax.ShapeDtypeStruct((d, s_len), BF16)
    return pl.pallas_call(
        body, name="flash_fwd", grid=(n_pair, n_q),
        in_specs=[qblk, qblk, whole_t, whole_t, whole, whole],
        out_specs=[qblk, qblk, qblk, qblk_t, qblk_t],
        out_shape=[jax.ShapeDtypeStruct((s_len, d), F32), sd, sd, ds_, ds_],
        compiler_params=_params(2),
    )(qa, qb, kat, kbt, va, vb)


def _flash_bwd(qa, qb, qat, qbt, kat, kbt, vat, vbt, doa, dob, doat, dobt):
    s_len, d = qa.shape
    sub = ATT_BLOCK
    n_sub = 2 if s_len % (2 * sub) == 0 else 1
    t = n_sub * sub
    w = min(ATT_CHUNK, s_len)
    n_pair = d // LANES
    n_q = s_len // t
    hd = HEAD_DIM
    chains = [(r, hh) for r in range(n_sub) for hh in range(2)]

    def body(qa_ref, qb_ref, qat_ref, qbt_ref, kat_hbm, kbt_hbm, vat_hbm, vbt_hbm, doa_ref, dob_ref, doat_ref, dobt_ref,
             dqt_ref, dkt_ref, dvt_ref, rs_ref, cs_ref,
             kat_v, kbt_v, vat_v, vbt_v, dkt_acc, dvt_acc, cs_acc, sem):
        j = pl.program_id(0)
        i = pl.program_id(1)

        @pl.when(i == 0)
        def _():
            rows = pl.ds(pl.multiple_of(j * LANES, LANES), LANES)
            copies = [pltpu.make_async_copy(src.at[rows, :], dst, sem.at[n]) for n, (src, dst) in enumerate([
                (kat_hbm, kat_v), (kbt_hbm, kbt_v), (vat_hbm, vat_v), (vbt_hbm, vbt_v)])]
            for cp in copies:
                cp.start()
            dkt_acc[...] = jnp.zeros_like(dkt_acc)
            dvt_acc[...] = jnp.zeros_like(dvt_acc)
            cs_acc[...] = jnp.zeros_like(cs_acc)
            for cp in copies:
                cp.wait()

        q_refs, do_refs = (qa_ref, qb_ref), (doa_ref, dob_ref)
        qs = [q_refs[hh][r * sub:(r + 1) * sub, :] for r, hh in chains]
        dos = [do_refs[hh][r * sub:(r + 1) * sub, :] for r, hh in chains]
        own = (slice(0, hd), slice(hd, 2 * hd))
        spare = (slice(hd, hd + 8), slice(0, 8))
        used = (slice(0, hd + 16), slice(0, 2 * hd))
        qts = (qat_ref[used[0], :], qbt_ref[used[1], :])
        dots = (doat_ref[own[0], :], dobt_ref[own[1], :])
        kts, vts = (kat_v, kbt_v), (vat_v, vbt_v)

        def step(kb, carry, masked, width=w):
            off = pl.multiple_of(kb * w, w)
            cols = pl.ds(off, width)
            scores = [_dot(qs[c], kts[hh][:, cols]) for c, (r, hh) in enumerate(chains)]
            dps = [_dot(dos[c], vts[hh][:, cols]) for c, (r, hh) in enumerate(chains)]
            ps, dss = [], []
            for c, (r, hh) in enumerate(chains):
                s = scores[c]
                if masked:
                    s = jnp.where(_causal(i * t + r * sub, off, (sub, width)), s, NEG_BIG)
                p = jnp.exp(s)
                dss.append((p * dps[c]).astype(BF16))
                ps.append(p.astype(BF16))
            out = tuple(carry[c] + _dot_nt(kts[hh][used[hh], cols], dss[c]) for c, (r, hh) in enumerate(chains))
            for hh in range(2):
                p_all = jnp.concatenate([ps[2 * r + hh] for r in range(n_sub)], axis=0)
                ds_all = jnp.concatenate([dss[2 * r + hh] for r in range(n_sub)], axis=0)
                dvt_acc[own[hh], cols] += _dot(dots[hh], p_all)
                with_sums = _dot(qts[hh], ds_all)
                dkt_acc[own[hh], cols] += with_sums[own[hh], :]
                cs_acc[8 * hh:8 * hh + 8, cols] += with_sums[spare[hh], :]
            return out

        diag = (i * t) // w
        init = (jnp.zeros((hd + 16, sub), F32), jnp.zeros((2 * hd, sub), F32)) * n_sub
        carry = lax.fori_loop(0, diag, lambda kb, c: step(kb, c, False), init)
        carry = lax.cond((i * t) % w + t <= w // 2,
                         lambda c: step(diag, c, True, w // 2), lambda c: step(diag, c, True), carry)
        for c, (r, hh) in enumerate(chains):
            at = slice(r * sub, (r + 1) * sub)
            dqt_ref[own[hh], at] = (carry[c][own[hh], :] * (hd ** -0.5)).astype(BF16)
            rs_ref[0, 8 * hh:8 * hh + 8, at] = carry[c][spare[hh], :]

        @pl.when(i == n_q - 1)
        def _():
            dkt_ref[...] = dkt_acc[...].astype(BF16)
            dvt_ref[...] = dvt_acc[...].astype(BF16)
            cs_ref[0] = cs_acc[...]

    qblk = pl.BlockSpec((t, LANES), lambda j, i: (i, j))
    qblk_t = pl.BlockSpec((LANES, t), lambda j, i: (j, i))
    whole_t = pl.BlockSpec((LANES, s_len), lambda j, i: (j, 0))
    ds_ = jax.ShapeDtypeStruct((d, s_len), BF16)
    sums = jax.ShapeDtypeStruct((n_pair, 16, s_len), F32)
    return pl.pallas_call(
        body, name="flash_bwd", grid=(n_pair, n_q),
        in_specs=[qblk, qblk, qblk_t, qblk_t, ANY, ANY, ANY, ANY, qblk, qblk, qblk_t, qblk_t],
        out_specs=[qblk_t, whole_t, whole_t, pl.BlockSpec((1, 16, t), lambda j, i: (j, 0, i)),
                   pl.BlockSpec((1, 16, s_len), lambda j, i: (j, 0, 0))],
        out_shape=[ds_, ds_, ds_, sums, sums],
        scratch_shapes=[pltpu.VMEM((LANES, s_len), BF16), pltpu.VMEM((LANES, s_len), BF16),
                        pltpu.VMEM((LANES, s_len), BF16), pltpu.VMEM((LANES, s_len), BF16),
                        pltpu.VMEM((LANES, s_len), F32), pltpu.VMEM((LANES, s_len), F32),
                        pltpu.VMEM((16, s_len), F32), pltpu.SemaphoreType.DMA((4,))],
        compiler_params=_params(2),
    )(qa, qb, qat, qbt, kat, kbt, vat, vbt, doa, dob, doat, dobt)


def _wgrad_t(at, b, name):
    k, s_len = at.shape
    n = b.shape[1]
    tn, tk, ts = min(n, 1024), min(k, 1024), min(s_len, 1024)

    def body(a_ref, b_ref, o_ref):
        @pl.when(pl.program_id(2) == 0)
        def _():
            o_ref[...] = jnp.zeros_like(o_ref)
        o_ref[...] += _dot(a_ref[...].astype(BF16), b_ref[...].astype(BF16))

    return pl.pallas_call(
        body, name=name, grid=(k // tk, n // tn, s_len // ts),
        in_specs=[pl.BlockSpec((tk, ts), lambda a, b_, c: (a, c)), pl.BlockSpec((ts, tn), lambda a, b_, c: (c, b_))],
        out_specs=pl.BlockSpec((tk, tn), lambda a, b_, c: (a, b_)),
        out_shape=jax.ShapeDtypeStruct((k, n), F32),
        compiler_params=_params(3),
    )(at, b)


def _oproj_bwd(dh, o, wo, seg, sel_q, tm):
    s_len, d = dh.shape
    sq_a, sq_b, _, _ = sel_q

    def body(dh_ref, o_ref, wo_hbm, seg_ref, sqa_ref, sqb_ref, doa_ref, dob_ref, doat_ref, dobt_ref, wo_v, sem):
        _load_once([(wo_hbm, wo_v)], sem)
        do = _dot_nt(dh_ref[...].astype(BF16), wo_v[...])
        parts = _parts(-_dot3_lhs(do * o_ref[...], seg_ref[...]))
        even = _even_head_lanes((tm, d), 1)
        doa = jnp.where(even, do, _dot(parts, sqa_ref[...]))
        dob = jnp.where(even, _dot(parts, sqb_ref[...]), do)
        doa_ref[...] = doa.astype(BF16)
        dob_ref[...] = dob.astype(BF16)
        doat_ref[...] = doa.T.astype(BF16)
        dobt_ref[...] = dob.T.astype(BF16)

    row = lambda i: (i, 0)
    const2 = lambda i: (0, 0)
    rs, cs = pl.BlockSpec((tm, d), row), pl.BlockSpec((d, tm), lambda i: (0, i))
    sel = pl.BlockSpec((N_SPARE * LANES, d), const2)
    sd = jax.ShapeDtypeStruct((s_len, d), BF16)
    ds_ = jax.ShapeDtypeStruct((d, s_len), BF16)
    return pl.pallas_call(
        body, name="oproj_bwd", grid=(s_len // tm,),
        in_specs=[rs, rs, ANY, pl.BlockSpec((d, LANES), const2), sel, sel],
        out_specs=[rs, rs, cs, cs], out_shape=[sd, sd, ds_, ds_],
        scratch_shapes=[pltpu.VMEM((d, d), BF16), pltpu.SemaphoreType.DMA((1,))],
        compiler_params=_params(1),
    )(dh, o, wo, seg, sq_a, sq_b)


def _oproj_fwd(h, o, wo, tm):
    s_len, d = h.shape

    def body(h_ref, o_ref, wo_hbm, hout_ref, wo_v, sem):
        _load_once([(wo_hbm, wo_v)], sem)
        hout_ref[...] = h_ref[...] + _dot(o_ref[...].astype(BF16), wo_v[...])

    row = lambda i: (i, 0)
    return pl.pallas_call(
        body, name="oproj_fwd", grid=(s_len // tm,),
        in_specs=[pl.BlockSpec((tm, d), row), pl.BlockSpec((tm, d), row), ANY],
        out_specs=pl.BlockSpec((tm, d), row),
        out_shape=jax.ShapeDtypeStruct((s_len, d), F32),
        scratch_shapes=[pltpu.VMEM((d, d), BF16), pltpu.SemaphoreType.DMA((1,))],
        compiler_params=_params(1),
    )(h, o, wo)


def _forget_bwd(dcum, z, tm):
    s_len = dcum.shape[0]
    n_blk = s_len // tm

    def body(dc_ref, z_ref, dfl_ref, gb_ref, total):
        i = pl.program_id(0)

        @pl.when(i == 0)
        def _():
            total[...] = jnp.zeros_like(total)
            gb_ref[...] = jnp.zeros_like(gb_ref)

        upper = (lax.broadcasted_iota(jnp.int32, (tm, tm), 0) <= lax.broadcasted_iota(jnp.int32, (tm, tm), 1))
        suffix = _dot3_rhs(jnp.where(upper, 1.0, 0.0).astype(BF16), dc_ref[...]) + total[0:1, :]
        total[...] = jnp.broadcast_to(suffix[0:1, :], total.shape)
        dfl = suffix * _sigmoid(-z_ref[...])
        dfl_ref[...] = dfl
        gb_ref[...] += jnp.sum(dfl, axis=0, keepdims=True)

    rev = lambda i: (n_blk - 1 - i, 0)
    return pl.pallas_call(
        body, name="forget_bwd", grid=(n_blk,),
        in_specs=[pl.BlockSpec((tm, LANES), rev), pl.BlockSpec((tm, LANES), rev)],
        out_specs=[pl.BlockSpec((tm, LANES), rev), pl.BlockSpec((1, LANES), lambda i: (0, 0))],
        out_shape=[jax.ShapeDtypeStruct((s_len, LANES), F32), jax.ShapeDtypeStruct((1, LANES), F32)],
        scratch_shapes=[pltpu.VMEM((8, LANES), F32)],
        compiler_params=_params(1),
    )(dcum, z)


def _fox_proj_bwd(h, dh, dqt, dkt, dvt, dfl, g_norm, wq, wk, wv, wf, tm):
    s_len, d = h.shape

    def body(h_ref, dh_ref, dqt_ref, dkt_ref, dvt_ref, dfl_ref, g_ref, wq_hbm, wk_hbm, wv_hbm, wf_ref,
             dhin_ref, dflb_ref, gn_ref, wq_v, wk_v, wv_v, sem):
        _load_once([(wq_hbm, wq_v), (wk_hbm, wk_v), (wv_hbm, wv_v)], sem)

        @pl.when(pl.program_id(0) == 0)
        def _():
            gn_ref[...] = jnp.zeros_like(gn_ref)

        g = g_ref[...]
        xhat, r, _ = _rms_fwd(h_ref[...], g)
        dflb = dfl_ref[...].astype(BF16)
        dflb_ref[...] = dflb
        from_qkv = (_dot(wq_v[...], dqt_ref[...]) + _dot(wk_v[...], dkt_ref[...])
                    + _dot(wv_v[...], dvt_ref[...]))
        dhn = _dot_nt(dflb, wf_ref[...]) + from_qkv.T
        dx, gg = _rms_bwd(dhn, xhat, r, g)
        gn_ref[...] += jnp.sum(gg, axis=0, keepdims=True)
        dhin_ref[...] = dh_ref[...] + dx

    row = lambda i: (i, 0)
    const2 = lambda i: (0, 0)
    rs = pl.BlockSpec((tm, d), row)
    cs = pl.BlockSpec((d, tm), lambda i: (0, i))
    return pl.pallas_call(
        body, name="fox_proj_bwd", grid=(s_len // tm,),
        in_specs=[rs, rs, cs, cs, cs, pl.BlockSpec((tm, LANES), row), pl.BlockSpec((1, d), const2), ANY, ANY, ANY,
                  pl.BlockSpec((d, LANES), const2)],
        out_specs=[rs, pl.BlockSpec((tm, LANES), row), pl.BlockSpec((1, d), const2)],
        out_shape=[jax.ShapeDtypeStruct((s_len, d), F32), jax.ShapeDtypeStruct((s_len, LANES), BF16),
                   jax.ShapeDtypeStruct((1, d), F32)],
        scratch_shapes=[pltpu.VMEM((d, d), BF16), pltpu.VMEM((d, d), BF16), pltpu.VMEM((d, d), BF16),
                        pltpu.SemaphoreType.DMA((3,))],
        compiler_params=_params(1),
    )(h, dh, dqt, dkt, dvt, dfl, g_norm, wq, wk, wv, wf)


def _loss_head(h, target, g_final, tm):
    s_len, d = h.shape
    n_blk = s_len // tm

    def body(h_ref, t_ref, g_ref, dh_ref, loss_ref, gg_ref, sq):
        i = pl.program_id(0)

        @pl.when(i == 0)
        def _():
            sq[...] = jnp.zeros_like(sq)
            gg_ref[...] = jnp.zeros_like(gg_ref)

        g = g_ref[...]
        xhat, r, y = _rms_fwd(h_ref[...], g)
        err = y - t_ref[...]
        sq[...] += jnp.sum(err * err, axis=0, keepdims=True)
        dx, gg = _rms_bwd(err * (1.0 / d), xhat, r, g)
        gg_ref[...] += jnp.sum(gg, axis=0, keepdims=True)
        dh_ref[...] = dx

        @pl.when(i == n_blk - 1)
        def _():
            loss_ref[...] = jnp.broadcast_to(jnp.sum(sq[...], axis=1, keepdims=True) * (0.5 / d), loss_ref.shape)

    row = lambda i: (i, 0)
    const2 = lambda i: (0, 0)
    return pl.pallas_call(
        body, name="loss_head", grid=(n_blk,),
        in_specs=[pl.BlockSpec((tm, d), row), pl.BlockSpec((tm, d), row), pl.BlockSpec((1, d), const2)],
        out_specs=[pl.BlockSpec((tm, d), row), pl.BlockSpec((1, LANES), const2), pl.BlockSpec((1, d), const2)],
        out_shape=[jax.ShapeDtypeStruct((s_len, d), F32), jax.ShapeDtypeStruct((1, LANES), F32),
                   jax.ShapeDtypeStruct((1, d), F32)],
        scratch_shapes=[pltpu.VMEM((1, d), F32)],
        compiler_params=_params(1),
    )(h, target, g_final)


def _wgrad(x, dy, n_piece, name):
    s_len, k = x.shape
    n = dy.shape[1]
    tn = min(n // n_piece, 1024)
    tk = min(k, 1024)
    ts = min(s_len, 1024)
    per_piece = (n // n_piece) // tn

    def body(x_ref, dy_ref, o_ref):
        @pl.when(pl.program_id(2) == 0)
        def _():
            o_ref[...] = jnp.zeros_like(o_ref)
        o_ref[0] += _dot_tn(x_ref[...].astype(BF16), dy_ref[...].astype(BF16))

    return pl.pallas_call(
        body, name=name, grid=(k // tk, n // tn, s_len // ts),
        in_specs=[pl.BlockSpec((ts, tk), lambda a, b, c: (c, a)), pl.BlockSpec((ts, tn), lambda a, b, c: (c, b))],
        out_specs=pl.BlockSpec((1, tk, tn), lambda a, b, c: (b // per_piece, a, b % per_piece)),
        out_shape=jax.ShapeDtypeStruct((n_piece, k, n // n_piece), F32),
        compiler_params=_params(3),
    )(x, dy)


def _pair_sum(g, recv, core, name):
    n_piece, rows, c = g.shape
    half = rows // 2
    tr = min(half, 512)
    nb = half // tr

    def body(core_ref, g_ref, r_ref, o_ref, ob_ref):
        total = g_ref[...] + r_ref[...]
        o_ref[...] = total
        ob_ref[...] = total.astype(BF16)

    blk = pl.BlockSpec((1, tr, c), lambda p, i, core_ref: (p, i, 0))
    return pl.pallas_call(
        body, name=name,
        out_shape=[jax.ShapeDtypeStruct((n_piece, half, c), F32), jax.ShapeDtypeStruct((n_piece, half, c), BF16)],
        grid_spec=pltpu.PrefetchScalarGridSpec(
            num_scalar_prefetch=1, grid=(n_piece, nb),
            in_specs=[pl.BlockSpec((1, tr, c), lambda p, i, core_ref: (p, core_ref[0] * nb + i, 0)), blk],
            out_specs=[blk, blk]),
        compiler_params=_params(2),
    )(core, g, recv)


def _chip_sum(halves, recv, chip, name):
    _, h, c = halves.shape
    tr = min(h, 512)

    def body(chip_ref, own_ref, r_ref, o_ref):
        o_ref[...] = ((own_ref[0] + r_ref[0].astype(F32)) + r_ref[1].astype(F32)) + r_ref[2].astype(F32)

    return pl.pallas_call(
        body, name=name, out_shape=jax.ShapeDtypeStruct((h, c), F32),
        grid_spec=pltpu.PrefetchScalarGridSpec(
            num_scalar_prefetch=1, grid=(h // tr,),
            in_specs=[pl.BlockSpec((1, tr, c), lambda i, chip_ref: (chip_ref[0], i, 0)),
                      pl.BlockSpec((3, tr, c), lambda i, chip_ref: (0, i, 0))],
            out_specs=pl.BlockSpec((tr, c), lambda i, chip_ref: (i, 0))),
        compiler_params=_params(1),
    )(chip, halves, recv)


def _adamw_math(w, m, v, g):
    m_new = ADAM_B1 * m + (1.0 - ADAM_B1) * g
    v_new = ADAM_B2 * v + (1.0 - ADAM_B2) * (g * g)
    m_hat = m_new / (1.0 - ADAM_B1 ** ADAM_STEP)
    v_hat = v_new / (1.0 - ADAM_B2 ** ADAM_STEP)
    return -ADAM_LR * (m_hat / (jnp.sqrt(v_hat) + ADAM_EPS) + ADAM_WD * w), m_new, v_new


def _adamw(w, m, v, g, name):
    rows, c = w.shape
    tr = min(rows, 256)

    def body(w_ref, m_ref, v_ref, g_ref, d_ref, mo_ref, vo_ref):
        d_ref[...], mo_ref[...], vo_ref[...] = _adamw_math(w_ref[...], m_ref[...], v_ref[...], g_ref[...])

    spec = pl.BlockSpec((tr, c), lambda i: (i, 0))
    shape = jax.ShapeDtypeStruct((rows, c), F32)
    return pl.pallas_call(
        body, name=name, grid=(rows // tr,),
        in_specs=[spec] * 4, out_specs=[spec] * 3, out_shape=[shape] * 3,
        compiler_params=_params(1),
    )(w, m, v, g)


def _adamw_halves(w, m, v, g_own, g_other, core, name):
    rows, c = w.shape
    half = rows // 2
    tr = min(half, 256)
    nb = half // tr

    def body(core_ref, w_ref, m_ref, v_ref, own_ref, other_ref, g_ref, d_ref, mo_ref, vo_ref):
        mine = (pl.program_id(0) // nb) == core_ref[0]
        g = jnp.where(mine, own_ref[...], other_ref[...])
        g_ref[...] = g
        d_ref[...], mo_ref[...], vo_ref[...] = _adamw_math(w_ref[...], m_ref[...], v_ref[...], g)

    spec = pl.BlockSpec((tr, c), lambda i, core_ref: (i, 0))
    own = pl.BlockSpec((tr, c), lambda i, core_ref: (jnp.clip(i - core_ref[0] * nb, 0, nb - 1), 0))
    other = pl.BlockSpec((tr, c), lambda i, core_ref: (jnp.clip(i - (1 - core_ref[0]) * nb, 0, nb - 1), 0))
    shape = jax.ShapeDtypeStruct((rows, c), F32)
    return pl.pallas_call(
        body, name=name, out_shape=[shape] * 4,
        grid_spec=pltpu.PrefetchScalarGridSpec(
            num_scalar_prefetch=1, grid=(rows // tr,),
            in_specs=[spec, spec, spec, own, other], out_specs=[spec] * 4),
        compiler_params=_params(1),
    )(core, w, m, v, g_own, g_other)


def _place():
    x, y, c = lax.axis_index("x"), lax.axis_index("y"), lax.axis_index("c")
    chips = [(1 - x, y), (x, 1 - y), (1 - x, 1 - y)]
    return x, y, c, chips


def _half_of(ref, half, which):
    start = which * half
    if half % 8 == 0:
        start = pl.multiple_of(start, 8)
    return ref.at[pl.ds(start, half)]


def _gather_copies(ins, outs, send_sems, recv_sems):
    x, y, c, chips = _place()
    mine = 2 * x + y
    copies = []
    for k, (src, dst) in enumerate(zip(ins, outs)):
        half = src.shape[0] // 2
        copies.append(pltpu.make_async_remote_copy(
            src_ref=src, dst_ref=dst.at[mine], send_sem=send_sems.at[4 * k + 3], recv_sem=recv_sems.at[4 * k + 3],
            device_id=(x, y, 1 - c), device_id_type=MESH))
        for j, (tx, ty) in enumerate(chips):
            copies.append(pltpu.make_async_remote_copy(
                src_ref=_half_of(src, half, c), dst_ref=_half_of(dst.at[mine], half, c),
                send_sem=send_sems.at[4 * k + j], recv_sem=recv_sems.at[4 * k + j],
                device_id=(tx, ty, c), device_id_type=MESH))
    return copies


def _gather_ici(shards):
    n = len(shards)

    def body(*refs):
        copies = _gather_copies(refs[:n], refs[n:2 * n], refs[2 * n], refs[2 * n + 1])
        for cp in copies:
            cp.start()
        for cp in copies:
            cp.wait()

    return pl.pallas_call(
        body, name="weights_gather_ici",
        in_specs=[ANY] * n, out_specs=[ANY] * n,
        out_shape=[jax.ShapeDtypeStruct((4,) + s.shape, s.dtype) for s in shards],
        scratch_shapes=[pltpu.SemaphoreType.DMA((4 * n,)), pltpu.SemaphoreType.DMA((4 * n,))],
    )(*shards)


HBM_SPEC = pl.BlockSpec(memory_space=pltpu.HBM)
SEM_SPEC = pl.BlockSpec(memory_space=pltpu.SEMAPHORE)
IN_FLIGHT = pltpu.SideEffectType.DATAFLOW_SIDE_EFFECTING


def _gather_ici_start(shards, after):
    n = len(shards)

    def body(*refs):
        for cp in _gather_copies(refs[:n], refs[n:2 * n], refs[2 * n + 1], refs[2 * n + 2]):
            cp.start()
        token = refs[-1]
        token[...] = jnp.zeros_like(token)

    lands = [lax.empty((4,) + s.shape, s.dtype) for s in shards]
    out = pl.pallas_call(
        body, name="weights_gather_start",
        out_shape=(pltpu.SemaphoreType.DMA((4 * n,)), pltpu.SemaphoreType.DMA((4 * n,)),
                   *[pltpu.HBM(s.shape, s.dtype) for s in shards], *[pltpu.HBM(l.shape, l.dtype) for l in lands],
                   jax.ShapeDtypeStruct((8, LANES), F32)),
        in_specs=[HBM_SPEC] * (2 * n) + [ANY],
        out_specs=(SEM_SPEC, SEM_SPEC, *[HBM_SPEC] * (2 * n), pl.BlockSpec(memory_space=pltpu.VMEM)),
        input_output_aliases={k: 2 + k for k in range(2 * n)},
        compiler_params=pltpu.CompilerParams(has_side_effects=IN_FLIGHT),
    )(*[pltpu.with_memory_space_constraint(a, pltpu.HBM) for a in list(shards) + lands], after)
    return out[0], out[1], out[2:2 + n], out[2 + n:2 + 2 * n], out[-1]


def _gather_ici_wait(send_sems, recv_sems, sources, lands, after):
    n = len(sources)

    def body(*refs):
        for cp in _gather_copies(refs[:n], refs[n:2 * n], refs[2 * n], refs[2 * n + 1]):
            cp.wait_send()
            cp.wait_recv()

    out = pl.pallas_call(
        body, name="weights_gather_wait",
        out_shape=[pltpu.HBM(a.shape, a.dtype) for a in list(sources) + list(lands)],
        in_specs=[HBM_SPEC] * (2 * n) + [SEM_SPEC, SEM_SPEC, ANY], out_specs=[HBM_SPEC] * (2 * n),
        input_output_aliases={k: k for k in range(2 * n)},
        compiler_params=pltpu.CompilerParams(has_side_effects=IN_FLIGHT),
    )(*sources, *lands, send_sems, recv_sems, after)
    return out[n:]


def _gather_pair(gathered):
    n = len(gathered)

    def body(*refs):
        outs = refs[n:2 * n]
        send_sems, recv_sems = refs[2 * n:]
        x, y, c, chips = _place()
        sends = []
        for k in range(n):
            half = gathered[k].shape[1] // 2
            for j, (tx, ty) in enumerate(chips):
                piece = _half_of(outs[k].at[2 * tx + ty], half, c)
                sends.append(pltpu.make_async_remote_copy(
                    src_ref=piece, dst_ref=piece, send_sem=send_sems.at[k, j], recv_sem=recv_sems.at[k, j],
                    device_id=(x, y, 1 - c), device_id_type=MESH))
        for cp in sends:
            cp.start()
        for cp in sends:
            cp.wait()

    return pl.pallas_call(
        body, name="weights_gather_pair",
        in_specs=[ANY] * n, out_specs=[ANY] * n,
        out_shape=[jax.ShapeDtypeStruct(g.shape, g.dtype) for g in gathered],
        input_output_aliases={k: k for k in range(n)},
        scratch_shapes=[pltpu.SemaphoreType.DMA((n, 3)), pltpu.SemaphoreType.DMA((n, 3))],
    )(*gathered)


def _pair_exchange(grads, tag):
    n = len(grads)

    def body(*refs):
        ins, outs = refs[:n], refs[n:2 * n]
        send_sems, recv_sems = refs[2 * n:]
        x, y, c, _ = _place()
        copies = []
        for k in range(n):
            half = grads[k].shape[1] // 2
            other = ins[k].at[:, pl.ds(pl.multiple_of((1 - c) * half, 8), half), :]
            copies.append(pltpu.make_async_remote_copy(
                src_ref=other, dst_ref=outs[k], send_sem=send_sems.at[k], recv_sem=recv_sems.at[k],
                device_id=(x, y, 1 - c), device_id_type=MESH))
        for cp in copies:
            cp.start()
        for cp in copies:
            cp.wait()

    return pl.pallas_call(
        body, name=f"grads_pair_exchange_{tag}",
        in_specs=[ANY] * n, out_specs=[ANY] * n,
        out_shape=[jax.ShapeDtypeStruct((4, g.shape[1] // 2, g.shape[2]), F32) for g in grads],
        scratch_shapes=[pltpu.SemaphoreType.DMA((n,)), pltpu.SemaphoreType.DMA((n,))],
    )(*grads)


def _scatter_copies(ins, outs, send_sems, recv_sems):
    x, y, c, chips = _place()
    return [pltpu.make_async_remote_copy(
        src_ref=ins[k].at[2 * tx + ty], dst_ref=outs[k].at[j], send_sem=send_sems.at[3 * k + j],
        recv_sem=recv_sems.at[3 * k + j], device_id=(tx, ty, c), device_id_type=MESH)
        for k in range(len(ins)) for j, (tx, ty) in enumerate(chips)]


def _chip_scatter(halves):
    n = len(halves)

    def body(*refs):
        copies = _scatter_copies(refs[:n], refs[n:2 * n], refs[2 * n], refs[2 * n + 1])
        for cp in copies:
            cp.start()
        for cp in copies:
            cp.wait()

    return pl.pallas_call(
        body, name="grads_chip_scatter",
        in_specs=[ANY] * n, out_specs=[ANY] * n,
        out_shape=[jax.ShapeDtypeStruct((3,) + hv.shape[1:], hv.dtype) for hv in halves],
        scratch_shapes=[pltpu.SemaphoreType.DMA((3 * n,)), pltpu.SemaphoreType.DMA((3 * n,))],
    )(*halves)


def _chip_scatter_start(halves, tag, after=()):
    n = len(halves)
    n_in = 2 * n + len(after)

    def body(*refs):
        for cp in _scatter_copies(refs[:n], refs[n:2 * n], refs[n_in], refs[n_in + 1]):
            cp.start()
        token = refs[-1]
        token[...] = jnp.zeros_like(token)

    lands = [lax.empty((3,) + hv.shape[1:], hv.dtype) for hv in halves]
    out = pl.pallas_call(
        body, name=f"grads_chip_scatter_start_{tag}",
        out_shape=(pltpu.SemaphoreType.DMA((3 * n,)), pltpu.SemaphoreType.DMA((3 * n,)),
                   *[pltpu.HBM(a.shape, a.dtype) for a in list(halves) + lands], jax.ShapeDtypeStruct((8, LANES), F32)),
        in_specs=[HBM_SPEC] * (2 * n) + [ANY] * len(after),
        out_specs=(SEM_SPEC, SEM_SPEC, *[HBM_SPEC] * (2 * n), pl.BlockSpec(memory_space=pltpu.VMEM)),
        input_output_aliases={k: 2 + k for k in range(2 * n)},
        compiler_params=pltpu.CompilerParams(has_side_effects=IN_FLIGHT),
    )(*[pltpu.with_memory_space_constraint(a, pltpu.HBM) for a in list(halves) + lands], *after)
    return out[0], out[1], out[2:2 + n], out[2 + n:2 + 2 * n], out[-1]


def _chip_scatter_wait(send_sems, recv_sems, sources, lands, after, tag):
    n = len(sources)

    def body(*refs):
        for cp in _scatter_copies(refs[:n], refs[n:2 * n], refs[2 * n], refs[2 * n + 1]):
            cp.wait_send()
            cp.wait_recv()

    out = pl.pallas_call(
        body, name=f"grads_chip_scatter_wait_{tag}",
        out_shape=[pltpu.HBM(a.shape, a.dtype) for a in list(sources) + list(lands)],
        in_specs=[HBM_SPEC] * (2 * n) + [SEM_SPEC, SEM_SPEC] + [ANY] * len(after), out_specs=[HBM_SPEC] * (2 * n),
        input_output_aliases={k: k for k in range(2 * n)},
        compiler_params=pltpu.CompilerParams(has_side_effects=IN_FLIGHT),
    )(*sources, *lands, send_sems, recv_sems, *after)
    return out[n:]


def _pair_share(finals, tag, after):
    n = len(finals)

    def body(*refs):
        ins, outs = refs[:n], refs[n + 1:2 * n + 1]
        send_sems, recv_sems = refs[2 * n + 1:]
        x, y, c, _ = _place()
        copies = [pltpu.make_async_remote_copy(
            src_ref=ins[k], dst_ref=outs[k], send_sem=send_sems.at[k], recv_sem=recv_sems.at[k],
            device_id=(x, y, 1 - c), device_id_type=MESH) for k in range(n)]
        for cp in copies:
            cp.start()
        for cp in copies:
            cp.wait()

    return pl.pallas_call(
        body, name=f"grads_pair_share_{tag}",
        in_specs=[ANY] * (n + 1), out_specs=[ANY] * n,
        out_shape=[jax.ShapeDtypeStruct(fv.shape, F32) for fv in finals],
        scratch_shapes=[pltpu.SemaphoreType.DMA((n,)), pltpu.SemaphoreType.DMA((n,))],
    )(*finals, after)


def _small_all_reduce(buf):
    rows, c_ = buf.shape

    def body(in_ref, out_ref, pair_buf, slots, send_sems, recv_sems):
        x, y, c, chips = _place()
        mine = 2 * x + y
        pair = pltpu.make_async_remote_copy(
            src_ref=in_ref, dst_ref=pair_buf, send_sem=send_sems.at[0], recv_sem=recv_sems.at[0],
            device_id=(x, y, 1 - c), device_id_type=MESH)
        pair.start()
        pair.wait()
        slots[mine] = in_ref[...] + pair_buf[...]
        sends = [pltpu.make_async_remote_copy(
            src_ref=slots.at[mine], dst_ref=slots.at[mine], send_sem=send_sems.at[1 + j], recv_sem=recv_sems.at[1 + j],
            device_id=(tx, ty, c), device_id_type=MESH) for j, (tx, ty) in enumerate(chips)]
        for cp in sends:
            cp.start()
        for j, (tx, ty) in enumerate(chips):
            pltpu.make_async_remote_copy(
                src_ref=slots.at[mine], dst_ref=slots.at[2 * tx + ty], send_sem=send_sems.at[1 + j],
                recv_sem=recv_sems.at[1 + j], device_id=(tx, ty, c), device_id_type=MESH).wait()
        out_ref[...] = ((slots[0] + slots[1]) + slots[2]) + slots[3]

    vm = pl.BlockSpec(memory_space=pltpu.VMEM)
    return pl.pallas_call(
        body, name="small_all_reduce", in_specs=[vm], out_specs=vm,
        out_shape=jax.ShapeDtypeStruct((rows, c_), F32),
        scratch_shapes=[pltpu.VMEM((rows, c_), F32), pltpu.VMEM((4, rows, c_), F32),
                        pltpu.SemaphoreType.DMA((4,)), pltpu.SemaphoreType.DMA((4,))],
        compiler_params=pltpu.CompilerParams(vmem_limit_bytes=VMEM_LIMIT_V7X),
    )(buf)


def _pair_reduce(grads, core, tag):
    recv = _pair_exchange(grads, tag)
    return [_pair_sum(g, r, core, f"pair_sum_{tag}_{k}") for k, (g, r) in enumerate(zip(grads, recv))]


def _chip_reduce(halves, recv, chip, tag):
    return [_chip_sum(hv, r, chip, f"chip_sum_{tag}_{k}") for k, ((hv, _), r) in enumerate(zip(halves, recv))]


PACK_COLS = 1024


def _pack(arrays):
    flat = jnp.concatenate([a.reshape(-1).astype(F32) for a in arrays])
    rows = -(-flat.shape[0] // PACK_COLS)
    rows = -(-rows // 8) * 8
    return jnp.pad(flat, (0, rows * PACK_COLS - flat.shape[0])).reshape(rows, PACK_COLS)


def _unpack(buf, shapes):
    flat = buf.reshape(-1)
    out, at = [], 0
    for shp in shapes:
        size = math.prod(shp)
        out.append(flat[at:at + size].reshape(shp))
        at += size
    return out


def kernel(x, mix_norm_g, ffn_norm_g, gm_w_in, gm_ln_g, gm_ln_b, gm_w_s, gm_b_s, gm_w_out, fox_w_qkvf, fox_b_f, fox_w_o, ffn_w_gate, ffn_w_up, ffn_conv_w, ffn_conv_b, ffn_w_down, final_norm_g, loss_target, m_mix_norm_g, m_ffn_norm_g, m_gm_w_in, m_gm_ln_g, m_gm_ln_b, m_gm_w_s, m_gm_b_s, m_gm_w_out, m_fox_w_qkvf, m_fox_b_f, m_fox_w_o, m_ffn_w_gate, m_ffn_w_up, m_ffn_conv_w, m_ffn_conv_b, m_ffn_w_down, m_final_norm_g, v_mix_norm_g, v_ffn_norm_g, v_gm_w_in, v_gm_ln_g, v_gm_ln_b, v_gm_w_s, v_gm_b_s, v_gm_w_out, v_fox_w_qkvf, v_fox_b_f, v_fox_w_o, v_ffn_w_gate, v_ffn_w_up, v_ffn_conv_w, v_ffn_conv_b, v_ffn_w_down, v_final_norm_g):
    _, s_len, d = x.shape
    e = gm_ln_g.shape[1]
    f = ffn_conv_b.shape[1]
    n_head = fox_b_f.shape[1]
    n_pair = n_head // 2
    gd = e // GM_GROUPS
    qkvf_cols = fox_w_qkvf.shape[2]
    assert d == n_head * HEAD_DIM and d % (2 * LANES) == 0 and s_len % 512 == 0 and gd % LANES == 0
    assert gm_w_s.shape[2] == CHUNK and 4 * qkvf_cols == 3 * d + n_head
    tm = 256
    h0 = x[0]
    target = loss_target[0]

    w_in, w_out4 = _gather_pair(_gather_ici([gm_w_in[0].astype(BF16), gm_w_out[0].astype(BF16)]))
    send_sems, recv_sems, sources, lands, token = _gather_ici_start([
        fox_w_qkvf[0].astype(BF16), fox_w_o[0].astype(BF16), ffn_w_gate.astype(BF16), ffn_w_up.astype(BF16),
        ffn_w_down.astype(BF16), ffn_conv_w], after=w_in)
    w_out = w_out4.reshape(e, d)
    bf_pad = jnp.pad(fox_b_f, ((0, 0), (0, LANES - n_head)))

    tril = jnp.tril(jnp.ones((CHUNK, CHUNK), bool))
    wc = jnp.where(tril[None], gm_w_s[0], 0.0).astype(BF16)
    wct = jnp.transpose(wc, (0, 2, 1))
    bias = jnp.repeat(gm_b_s[0].T, gd, axis=1)
    seg_groups = (jnp.arange(e)[:, None] // gd == jnp.arange(LANES)[None, :]).astype(BF16)
    seg_heads = (jnp.arange(d)[:, None] // HEAD_DIM == jnp.arange(LANES)[None, :]).astype(BF16)
    sel_q = _spare_selectors(d, key_side=False)
    sel_k = _spare_selectors(d, key_side=True)

    h1, a0, hn0, gated0 = _gmlp_fwd(h0, mix_norm_g[0:1] + token[0:1, 0:1], w_in, gm_ln_g, gm_ln_b, wc, bias, w_out, tm)
    qkvf4, wo4, wg_all, wu_all, wd_all, cw4 = _gather_pair(_gather_ici_wait(send_sems, recv_sems, sources, lands, h1))
    qkvf = jnp.transpose(qkvf4, (1, 0, 2)).reshape(d, 4 * qkvf_cols)
    wq, wk, wv = qkvf[:, :d], qkvf[:, d:2 * d], qkvf[:, 2 * d:3 * d]
    wf = jnp.pad(qkvf[:, 3 * d:], ((0, 0), (0, LANES - n_head)))
    wo = wo4.reshape(d, d)
    conv_w_full = jnp.transpose(cw4, (1, 2, 0, 3)).reshape(2, 3, f)
    conv_w8 = jnp.pad(conv_w_full, ((0, 0), (0, 5), (0, 0)))
    h2, fa0, fup0, fhn0, fhid0 = _ffn_fwd(h1, ffn_norm_g[0:1], wg_all, wu_all, wd_all, 0, conv_w8[0], ffn_conv_b[0:1], tm)
    (hn1, qa, qb, kat, kbt, va, vb, vat, vbt, z_f) = _fox_proj_fwd(
        h2, mix_norm_g[1:2], wq, wk, wv, wf, bf_pad, sel_q, sel_k, tm)
    o, qa2, qb2, qat2, qbt2 = _flash_fwd(qa, qb, kat, kbt, va, vb)
    h3 = _oproj_fwd(h2, o, wo, tm)
    h4, fa1, fup1, fhn1, fhid1 = _ffn_fwd(h3, ffn_norm_g[1:2], wg_all, wu_all, wd_all, 1, conv_w8[1], ffn_conv_b[1:2], tm)

    dh4, loss_part, g_final = _loss_head(h4, target, final_norm_g.reshape(1, d), tm)
    dh3, da1, dup1, gcw1, gcb1, gfn1 = _ffn_bwd(h3, dh4, fa1, fup1, ffn_norm_g[1:2], wg_all, wu_all, wd_all, 1,
                                                conv_w8[1], ffn_conv_b[1:2], tm)
    g_gate1 = _wgrad(fhn1, da1, 4, "wgrad_gate_1")
    g_up1 = _wgrad(fhn1, dup1, 4, "wgrad_up_1")
    g_down1 = _wgrad(fhid1, dh4, 1, "wgrad_down_1").reshape(4, f // 4, d)

    doa, dob, doat, dobt = _oproj_bwd(dh3, o, wo, seg_heads, sel_q, tm)
    g_wo = _wgrad(o, dh3, 1, "wgrad_wo").reshape(4, d // 4, d)
    dqt, dkt, dvt, row_sums, col_sums = _flash_bwd(qa2, qb2, qat2, qbt2, kat, kbt, vat, vbt, doa, dob, doat, dobt)
    sums = row_sums[:, 0::8, :] - col_sums[:, N_SPARE::8, :]
    dcum = jnp.pad(sums.reshape(n_head, s_len).T, ((0, 0), (0, LANES - n_head)))
    dfl, g_bf = _forget_bwd(dcum, z_f, tm)
    dh2, dflb, gmn1 = _fox_proj_bwd(h2, dh3, dqt, dkt, dvt, dfl, mix_norm_g[1:2], wq, wk, wv, wf, tm)
    g_q = _wgrad_t(dqt, hn1, "wgrad_q").T
    g_k = _wgrad_t(dkt, hn1, "wgrad_k").T
    g_v = _wgrad_t(dvt, hn1, "wgrad_v").T
    g_f = _wgrad(hn1, dflb, 1, "wgrad_f")[0][:, :n_head]
    g_qkvf = jnp.concatenate([g_q, g_k, g_v, g_f], axis=1).reshape(d, 4, qkvf_cols).transpose(1, 0, 2)

    core = lax.axis_index("c").astype(jnp.int32).reshape(1)
    chip = (2 * lax.axis_index("x") + lax.axis_index("y")).astype(jnp.int32).reshape(1)
    halves_early = _pair_reduce([g_qkvf, g_wo, g_gate1, g_up1, g_down1], core, "early")
    sc_send, sc_recv, sc_src, sc_land, sc_token = _chip_scatter_start([hb for _, hb in halves_early], "early")

    dh1, da0f, dup0, gcw0, gcb0, gfn0 = _ffn_bwd(h1, dh2, fa0, fup0, ffn_norm_g[0:1] + sc_token[0:1, 0:1],
                                                 wg_all, wu_all, wd_all, 0, conv_w8[0], ffn_conv_b[0:1], tm)
    g_gate0 = _wgrad(fhn0, da0f, 4, "wgrad_gate_0")
    g_up0 = _wgrad(fhn0, dup0, 4, "wgrad_up_0")
    g_down0 = _wgrad(fhid0, dh2, 1, "wgrad_down_0").reshape(4, f // 4, d)

    dh0, da0, g_ws, g_bs_t, g_lng, g_lnb, gmn0 = _gmlp_bwd(
        h0, dh1, a0, mix_norm_g[0:1], w_in, gm_ln_g, gm_ln_b, wc, wct, bias, w_out, seg_groups, tm)
    g_win = _wgrad(hn0, da0, 4, "wgrad_gm_in")
    g_wout = _wgrad(gated0, dh1, 1, "wgrad_gm_out").reshape(4, e // 4, d)

    small = [jnp.concatenate([gmn0, gmn1]), jnp.concatenate([gfn0, gfn1]), g_lng, g_lnb, g_ws[None],
             g_bs_t[:, :GM_GROUPS].T[None], g_bf[:, :n_head], jnp.stack([gcw0[:3], gcw1[:3]]),
             jnp.concatenate([gcb0, gcb1]), g_final.reshape(d), loss_part[0, :1]]
    small_shapes = [a.shape for a in small]
    small_sum = _small_all_reduce(_pack(small))
    reduced = _unpack(small_sum, small_shapes)
    (r_mix, r_ffn, r_lng, r_lnb, r_ws, r_bs, r_bf, r_cw_full, r_cb, r_final, r_loss) = reduced
    r_cw = lax.dynamic_slice_in_dim(r_cw_full, chip[0] * (f // 4), f // 4, axis=2)

    halves_late = _pair_reduce([g_win, g_wout, g_gate0, g_up0, g_down0], core, "late")
    lt_send, lt_recv, lt_src, lt_land, lt_token = _chip_scatter_start([hb for _, hb in halves_late], "late", [small_sum])
    recv_early = _chip_scatter_wait(sc_send, sc_recv, sc_src, sc_land, [halves_late[0][0]], "early")
    finals_early = _chip_reduce(halves_early, recv_early, chip, "early")
    r_qkvf, r_wo, r_gate1, r_up1, r_down1 = zip(finals_early, _pair_share(finals_early, "early", lt_token))

    def update(name, layer, w, m, v, grad, core_):
        return _adamw_halves(w[layer], m[layer], v[layer], grad[0], grad[1], core_, f"adamw_{name}_{layer}")

    u_qkvf = update("fox_w_qkvf", 0, fox_w_qkvf, m_fox_w_qkvf, v_fox_w_qkvf, r_qkvf, core)
    u_wo = update("fox_w_o", 0, fox_w_o, m_fox_w_o, v_fox_w_o, r_wo, core)
    u_gate1 = update("ffn_w_gate", 1, ffn_w_gate, m_ffn_w_gate, v_ffn_w_gate, r_gate1, core)
    u_up1 = update("ffn_w_up", 1, ffn_w_up, m_ffn_w_up, v_ffn_w_up, r_up1, core)
    u_down1 = update("ffn_w_down", 1, ffn_w_down, m_ffn_w_down, v_ffn_w_down, r_down1, core)

    res = {}
    small_names = ["mix_norm_g", "ffn_norm_g", "gm_ln_g", "gm_ln_b", "gm_w_s", "gm_b_s", "fox_b_f", "ffn_conv_w",
                   "ffn_conv_b", "final_norm_g"]
    small_w = [mix_norm_g, ffn_norm_g, gm_ln_g, gm_ln_b, gm_w_s, gm_b_s, fox_b_f, ffn_conv_w, ffn_conv_b, final_norm_g]
    small_m = [m_mix_norm_g, m_ffn_norm_g, m_gm_ln_g, m_gm_ln_b, m_gm_w_s, m_gm_b_s, m_fox_b_f, m_ffn_conv_w,
               m_ffn_conv_b, m_final_norm_g]
    small_v = [v_mix_norm_g, v_ffn_norm_g, v_gm_ln_g, v_gm_ln_b, v_gm_w_s, v_gm_b_s, v_fox_b_f, v_ffn_conv_w,
               v_ffn_conv_b, v_final_norm_g]
    small_g = [r_mix, r_ffn, r_lng, r_lnb, r_ws, r_bs, r_bf, r_cw, r_cb, r_final]
    shapes = [w.shape for w in small_w]
    small_g = [g.reshape(s) for g, s in zip(small_g, shapes)]
    dlt, mn, vn = _adamw(_pack(small_w), _pack(small_m), _pack(small_v), _pack(small_g), "adamw_small")
    for name, g, dl_, m_, v_ in zip(small_names, small_g, _unpack(dlt, shapes), _unpack(mn, shapes), _unpack(vn, shapes)):
        res[name] = (g, dl_, m_, v_)

    recv_late = _chip_scatter_wait(lt_send, lt_recv, lt_src, lt_land, [u_down1[1], dlt], "late")
    finals_late = _chip_reduce(halves_late, recv_late, chip, "late")
    r_win, r_wout, r_gate0, r_up0, r_down0 = zip(finals_late, _pair_share(finals_late, "late", lt_token))
    u_gate0 = update("ffn_w_gate", 0, ffn_w_gate, m_ffn_w_gate, v_ffn_w_gate, r_gate0, core)
    u_up0 = update("ffn_w_up", 0, ffn_w_up, m_ffn_w_up, v_ffn_w_up, r_up0, core)
    u_down0 = update("ffn_w_down", 0, ffn_w_down, m_ffn_w_down, v_ffn_w_down, r_down0, core)
    layers = {"gm_w_in": [update("gm_w_in", 0, gm_w_in, m_gm_w_in, v_gm_w_in, r_win, core)],
              "gm_w_out": [update("gm_w_out", 0, gm_w_out, m_gm_w_out, v_gm_w_out, r_wout, core)],
              "fox_w_qkvf": [u_qkvf], "fox_w_o": [u_wo], "ffn_w_gate": [u_gate0, u_gate1],
              "ffn_w_up": [u_up0, u_up1], "ffn_w_down": [u_down0, u_down1]}
    for name, parts in layers.items():
        res[name] = tuple(jnp.stack([p[i] for p in parts]) for i in range(4))

    order = ["mix_norm_g", "ffn_norm_g", "gm_w_in", "gm_ln_g", "gm_ln_b", "gm_w_s", "gm_b_s", "gm_w_out", "fox_w_qkvf",
             "fox_b_f", "fox_w_o", "ffn_w_gate", "ffn_w_up", "ffn_conv_w", "ffn_conv_b", "ffn_w_down", "final_norm_g"]
    outs = [r_loss.reshape(()), dh0[None]]
    for part in range(4):
        outs += [res[name][part] for name in order]
    return tuple(outs)
```

```python
import math

import jax
import jax.numpy as jnp
from jax import lax
from jax.experimental import pallas as pl
from jax.experimental.pallas import tpu as pltpu

F32 = jnp.float32
BF16 = jnp.bfloat16

RMS_EPS = 1e-6
LN_EPS = 1e-5
CHUNK = 128
GM_GROUPS = 8
HEAD_DIM = 64
LANES = 128
ATT_BLOCK = 256
ATT_CHUNK = 1024
VMEM_LIMIT_V7X = 56 * 1024 * 1024

ADAM_LR = 0.001
ADAM_B1 = 0.9
ADAM_B2 = 0.999
ADAM_EPS = 1e-08
ADAM_WD = 0.01
ADAM_STEP = 10

MESH = pl.DeviceIdType.MESH
ANY = pl.BlockSpec(memory_space=pl.ANY)
NEG_BIG = -1e30


def _params(n_grid):
    return pltpu.CompilerParams(dimension_semantics=("arbitrary",) * n_grid, vmem_limit_bytes=VMEM_LIMIT_V7X)


def _dot(a, b):
    return jnp.dot(a, b, preferred_element_type=F32)


def _dot_nt(a, b):
    return lax.dot_general(a, b, (((1,), (1,)), ((), ())), preferred_element_type=F32)


def _dot_tn(a, b):
    return lax.dot_general(a, b, (((0,), (0,)), ((), ())), preferred_element_type=F32)


def _split3(x):
    hi = x.astype(BF16)
    r = x - hi.astype(F32)
    mid = r.astype(BF16)
    lo = (r - mid.astype(F32)).astype(BF16)
    return hi, mid, lo


def _dot3_lhs(x, m):
    hi, mid, lo = _split3(x)
    return _dot(hi, m) + _dot(mid, m) + _dot(lo, m)


def _dot3_rhs(m, x):
    hi, mid, lo = _split3(x)
    return _dot(m, hi) + _dot(m, mid) + _dot(m, lo)


def _load_once(pairs, sem):
    @pl.when(pl.program_id(0) == 0)
    def _():
        copies = [pltpu.make_async_copy(src, dst, sem.at[k]) for k, (src, dst) in enumerate(pairs)]
        for cp in copies:
            cp.start()
        for cp in copies:
            cp.wait()


def _rms_fwd(x, g):
    r = lax.rsqrt(jnp.mean(x * x, axis=-1, keepdims=True) + RMS_EPS)
    xhat = x * r
    return xhat, r, xhat * g


def _rms_bwd(dy, xhat, r, g):
    w = dy * g
    dx = r * (w - xhat * jnp.mean(w * xhat, axis=-1, keepdims=True))
    return dx, dy * xhat


def _gelu_parts(a):
    c = math.sqrt(2.0 / math.pi)
    a2 = a * a
    t = jnp.tanh(c * (a + 0.044715 * a * a2))
    z = 0.5 * a * (1.0 + t)
    dz = 0.5 * (1.0 + t) + 0.5 * a * (1.0 - t * t) * (c * (1.0 + 3.0 * 0.044715 * a2))
    return z, dz


def _sigmoid(x):
    return 1.0 / (1.0 + jnp.exp(-x))


def _gmlp_core(a, lng, lnb, wc_ref, bias, n_chunk, gd):
    e = a.shape[1] // 2
    z, dz = _gelu_parts(a)
    u = z[:, :e]
    v = z[:, e:]
    mu = jnp.mean(v, axis=-1, keepdims=True)
    vc = v - mu
    rstd = lax.rsqrt(jnp.mean(vc * vc, axis=-1, keepdims=True) + LN_EPS)
    vhat = vc * rstd
    vln = vhat * lng + lnb
    vlb = vln.astype(BF16)
    rows = []
    for ci in range(n_chunk):
        cols = []
        for g in range(GM_GROUPS):
            blk = vlb[ci * CHUNK:(ci + 1) * CHUNK, g * gd:(g + 1) * gd]
            cols.append(_dot(wc_ref[g], blk))
        rows.append(jnp.concatenate(cols, axis=1) + bias)
    s = rows[0] if n_chunk == 1 else jnp.concatenate(rows, axis=0)
    return dz, u, vhat, rstd, vlb, s


def _gmlp_fwd(h, g_mix, w_in, lng, lnb, wc, bias, w_out, tm):
    s_len, d = h.shape
    n_p, _, w = w_in.shape
    e = w_out.shape[0]
    gd = e // GM_GROUPS
    n_chunk = tm // CHUNK

    def body(h_ref, g_ref, win_hbm, lng_ref, lnb_ref, wc_ref, bias_ref, wout_hbm,
             hout_ref, a_ref, hn_ref, gated_ref, win_v, wout_v, sem):
        _load_once([(win_hbm, win_v), (wout_hbm, wout_v)], sem)
        x = h_ref[...]
        _, _, y = _rms_fwd(x, g_ref[...])
        hn = y.astype(BF16)
        hn_ref[...] = hn
        for p in range(n_p):
            a_ref[:, p * w:(p + 1) * w] = _dot(hn, win_v[p])
        _, u, _, _, _, s = _gmlp_core(a_ref[...], lng_ref[...], lnb_ref[...], wc_ref, bias_ref[...], n_chunk, gd)
        gated = (u * s).astype(BF16)
        gated_ref[...] = gated
        hout_ref[...] = x + _dot(gated, wout_v[...])

    row = lambda i: (i, 0)
    const2 = lambda i: (0, 0)
    return pl.pallas_call(
        body, name="gmlp_fwd", grid=(s_len // tm,),
        in_specs=[pl.BlockSpec((tm, d), row), pl.BlockSpec((1, d), const2), ANY,
                  pl.BlockSpec((1, e), const2), pl.BlockSpec((1, e), const2),
                  pl.BlockSpec(wc.shape, lambda i: (0, 0, 0)), pl.BlockSpec((CHUNK, e), const2), ANY],
        out_specs=[pl.BlockSpec((tm, d), row), pl.BlockSpec((tm, 2 * e), row),
                   pl.BlockSpec((tm, d), row), pl.BlockSpec((tm, e), row)],
        out_shape=[jax.ShapeDtypeStruct((s_len, d), F32), jax.ShapeDtypeStruct((s_len, 2 * e), F32),
                   jax.ShapeDtypeStruct((s_len, d), BF16), jax.ShapeDtypeStruct((s_len, e), BF16)],
        scratch_shapes=[pltpu.VMEM(w_in.shape, BF16), pltpu.VMEM(w_out.shape, BF16), pltpu.SemaphoreType.DMA((2,))],
        compiler_params=_params(1),
    )(h, g_mix, w_in, lng, lnb, wc, bias, w_out)


def _gmlp_bwd(h, dh, a, g_mix, w_in, lng, lnb, wc, wct, bias, w_out, seg, tm):
    s_len, d = h.shape
    n_p, _, w = w_in.shape
    e = w_out.shape[0]
    gd = e // GM_GROUPS
    n_chunk = tm // CHUNK
    n_blk = s_len // tm

    def body(h_ref, dh_ref, a_ref, g_ref, win_hbm, lng_ref, lnb_ref, wc_ref, wct_ref, bias_ref, wout_hbm, seg_ref,
             dhin_ref, da_ref, gws_ref, gbs_ref, glng_ref, glnb_ref, gmix_ref, win_v, wout_v, dsum, sem):
        i = pl.program_id(0)
        _load_once([(win_hbm, win_v), (wout_hbm, wout_v)], sem)

        @pl.when(i == 0)
        def _():
            gws_ref[...] = jnp.zeros_like(gws_ref)
            glng_ref[...] = jnp.zeros_like(glng_ref)
            glnb_ref[...] = jnp.zeros_like(glnb_ref)
            gmix_ref[...] = jnp.zeros_like(gmix_ref)
            dsum[...] = jnp.zeros_like(dsum)

        x = h_ref[...]
        dh_v = dh_ref[...]
        g = g_ref[...]
        lng_v = lng_ref[...]
        xhat, r, _ = _rms_fwd(x, g)
        dz_da, u, vhat, rstd, vlb, s = _gmlp_core(a_ref[...], lng_v, lnb_ref[...], wc_ref, bias_ref[...], n_chunk, gd)
        dg = _dot_nt(dh_v.astype(BF16), wout_v[...])
        du = dg * s
        ds = dg * u
        dsb = ds.astype(BF16)
        rows = []
        ds_acc = None
        for ci in range(n_chunk):
            lo, hi = ci * CHUNK, (ci + 1) * CHUNK
            cols = []
            for gi in range(GM_GROUPS):
                d_blk = dsb[lo:hi, gi * gd:(gi + 1) * gd]
                gws_ref[gi] += _dot_nt(d_blk, vlb[lo:hi, gi * gd:(gi + 1) * gd])
                cols.append(_dot(wct_ref[gi], d_blk))
            rows.append(jnp.concatenate(cols, axis=1))
            ds_acc = ds[lo:hi] if ds_acc is None else ds_acc + ds[lo:hi]
        dsum[...] += ds_acc
        dvln = rows[0] if n_chunk == 1 else jnp.concatenate(rows, axis=0)
        glng_ref[...] += jnp.sum(dvln * vhat, axis=0, keepdims=True)
        glnb_ref[...] += jnp.sum(dvln, axis=0, keepdims=True)
        dvhat = dvln * lng_v
        dv = rstd * (dvhat - jnp.mean(dvhat, axis=-1, keepdims=True)
                     - vhat * jnp.mean(dvhat * vhat, axis=-1, keepdims=True))
        da = jnp.concatenate([du, dv], axis=1) * dz_da
        dab = da.astype(BF16)
        da_ref[...] = dab
        dhn = _dot_nt(dab[:, :w], win_v[0])
        for p in range(1, n_p):
            dhn += _dot_nt(dab[:, p * w:(p + 1) * w], win_v[p])
        dx, gg = _rms_bwd(dhn, xhat, r, g)
        gmix_ref[...] += jnp.sum(gg, axis=0, keepdims=True)
        dhin_ref[...] = dh_v + dx

        @pl.when(i == n_blk - 1)
        def _():
            tril = lax.broadcasted_iota(jnp.int32, (CHUNK, CHUNK), 0) >= lax.broadcasted_iota(jnp.int32, (CHUNK, CHUNK), 1)
            for gi in range(GM_GROUPS):
                gws_ref[gi] = jnp.where(tril, gws_ref[gi], 0.0)
            gbs_ref[...] = _dot3_lhs(dsum[...], seg_ref[...])

    row = lambda i: (i, 0)
    const2 = lambda i: (0, 0)
    const3 = lambda i: (0, 0, 0)
    return pl.pallas_call(
        body, name="gmlp_bwd", grid=(n_blk,),
        in_specs=[pl.BlockSpec((tm, d), row), pl.BlockSpec((tm, d), row), pl.BlockSpec((tm, 2 * e), row),
                  pl.BlockSpec((1, d), const2), ANY, pl.BlockSpec((1, e), const2), pl.BlockSpec((1, e), const2),
                  pl.BlockSpec(wc.shape, const3), pl.BlockSpec(wct.shape, const3), pl.BlockSpec((CHUNK, e), const2),
                  ANY, pl.BlockSpec((e, LANES), const2)],
        out_specs=[pl.BlockSpec((tm, d), row), pl.BlockSpec((tm, 2 * e), row), pl.BlockSpec(wc.shape, const3),
                   pl.BlockSpec((CHUNK, LANES), const2), pl.BlockSpec((1, e), const2), pl.BlockSpec((1, e), const2),
                   pl.BlockSpec((1, d), const2)],
        out_shape=[jax.ShapeDtypeStruct((s_len, d), F32), jax.ShapeDtypeStruct((s_len, 2 * e), BF16),
                   jax.ShapeDtypeStruct(wc.shape, F32), jax.ShapeDtypeStruct((CHUNK, LANES), F32),
                   jax.ShapeDtypeStruct((1, e), F32), jax.ShapeDtypeStruct((1, e), F32), jax.ShapeDtypeStruct((1, d), F32)],
        scratch_shapes=[pltpu.VMEM(w_in.shape, BF16), pltpu.VMEM(w_out.shape, BF16), pltpu.VMEM((CHUNK, e), F32),
                        pltpu.SemaphoreType.DMA((2,))],
        compiler_params=_params(1),
    )(h, dh, a, g_mix, w_in, lng, lnb, wc, wct, bias, w_out, seg)


def _shift_down(a, k, fill):
    tm = a.shape[0]
    out = pltpu.roll(a, k, 0)
    rid = lax.broadcasted_iota(jnp.int32, a.shape, 0)
    for j in range(k):
        out = jnp.where(rid == j, fill[8 - k + j:8 - k + j + 1, :], out)
    return out


def _shift_up(a, k, fill):
    tm = a.shape[0]
    out = pltpu.roll(a, tm - k, 0)
    rid = lax.broadcasted_iota(jnp.int32, a.shape, 0)
    for j in range(k):
        out = jnp.where(rid == tm - k + j, fill[j:j + 1, :], out)
    return out


def _ffn_fwd(h, g_norm, wg_all, wu_all, wd_all, layer, conv_w, conv_b, tm):
    s_len, d = h.shape
    n_p = wg_all.shape[0]
    fq = wg_all.shape[3]
    f = n_p * fq

    def body(h_ref, g_ref, wg_hbm, wu_hbm, wd_hbm, cw_ref, cb_ref,
             hout_ref, a_ref, up_ref, hn_ref, hid_ref, wg_v, wu_v, wd_v, carry, sem):
        i = pl.program_id(0)
        _load_once([(wg_hbm.at[:, layer], wg_v), (wu_hbm.at[:, layer], wu_v), (wd_hbm.at[:, layer], wd_v)], sem)

        @pl.when(i == 0)
        def _():
            carry[...] = jnp.zeros_like(carry)

        x = h_ref[...]
        _, _, y = _rms_fwd(x, g_ref[...])
        hn = y.astype(BF16)
        hn_ref[...] = hn
        for p in range(n_p):
            a_ref[:, p * fq:(p + 1) * fq] = _dot(hn, wg_v[p])
            up_ref[:, p * fq:(p + 1) * fq] = _dot(hn, wu_v[p])
        a = a_ref[...]
        prev = carry[...]
        am1 = _shift_down(a, 1, prev)
        am2 = _shift_down(a, 2, prev)
        carry[...] = a[tm - 8:tm, :]
        cw = cw_ref[...]
        ac = cb_ref[...] + am2 * cw[0:1, :]
        ac = ac + am1 * cw[1:2, :]
        ac = ac + a * cw[2:3, :]
        hid = (ac * _sigmoid(ac) * up_ref[...]).astype(BF16)
        hid_ref[...] = hid
        y2 = _dot(hid[:, :fq], wd_v[0])
        for p in range(1, n_p):
            y2 += _dot(hid[:, p * fq:(p + 1) * fq], wd_v[p])
        hout_ref[...] = x + y2

    row = lambda i: (i, 0)
    const2 = lambda i: (0, 0)
    return pl.pallas_call(
        body, name=f"ffn_fwd_{layer}", grid=(s_len // tm,),
        in_specs=[pl.BlockSpec((tm, d), row), pl.BlockSpec((1, d), const2), ANY, ANY, ANY,
                  pl.BlockSpec((8, f), const2), pl.BlockSpec((1, f), const2)],
        out_specs=[pl.BlockSpec((tm, d), row), pl.BlockSpec((tm, f), row), pl.BlockSpec((tm, f), row),
                   pl.BlockSpec((tm, d), row), pl.BlockSpec((tm, f), row)],
        out_shape=[jax.ShapeDtypeStruct((s_len, d), F32), jax.ShapeDtypeStruct((s_len, f), F32),
                   jax.ShapeDtypeStruct((s_len, f), F32), jax.ShapeDtypeStruct((s_len, d), BF16),
                   jax.ShapeDtypeStruct((s_len, f), BF16)],
        scratch_shapes=[pltpu.VMEM((n_p, d, fq), BF16), pltpu.VMEM((n_p, d, fq), BF16), pltpu.VMEM((n_p, fq, d), BF16),
                        pltpu.VMEM((8, f), F32), pltpu.SemaphoreType.DMA((3,))],
        compiler_params=_params(1),
    )(h, g_norm, wg_all, wu_all, wd_all, conv_w, conv_b)


def _ffn_bwd(h, dh, a, up, g_norm, wg_all, wu_all, wd_all, layer, conv_w, conv_b, tm):
    s_len, d = h.shape
    n_p = wg_all.shape[0]
    fq = wg_all.shape[3]
    f = n_p * fq
    n_blk = s_len // tm
    t8 = tm // 8

    def body(h_ref, dh_ref, a_ref, ahalo_ref, up_ref, g_ref, wg_hbm, wu_hbm, wd_hbm, cw_ref, cb_ref,
             dhin_ref, da_ref, dup_ref, gcw_ref, gcb_ref, gn_ref, wg_v, wu_v, wd_v, carry, sem):
        i = pl.program_id(0)
        _load_once([(wg_hbm.at[:, layer], wg_v), (wu_hbm.at[:, layer], wu_v), (wd_hbm.at[:, layer], wd_v)], sem)

        @pl.when(i == 0)
        def _():
            carry[...] = jnp.zeros_like(carry)
            gcw_ref[...] = jnp.zeros_like(gcw_ref)
            gcb_ref[...] = jnp.zeros_like(gcb_ref)
            gn_ref[...] = jnp.zeros_like(gn_ref)

        x = h_ref[...]
        dh_v = dh_ref[...]
        g = g_ref[...]
        xhat, r, _ = _rms_fwd(x, g)
        a = a_ref[...]
        up_v = up_ref[...]
        prev = jnp.where(i == n_blk - 1, 0.0, ahalo_ref[...])
        am1 = _shift_down(a, 1, prev)
        am2 = _shift_down(a, 2, prev)
        cw = cw_ref[...]
        ac = cb_ref[...] + am2 * cw[0:1, :]
        ac = ac + am1 * cw[1:2, :]
        ac = ac + a * cw[2:3, :]
        sg = _sigmoid(ac)
        sil = ac * sg
        dhb = dh_v.astype(BF16)
        dhid = jnp.concatenate([_dot_nt(dhb, wd_v[p]) for p in range(n_p)], axis=1)
        dup = dhid * sil
        dac = dhid * up_v * (sg * (1.0 + ac * (1.0 - sg)))
        gcb_ref[...] += jnp.sum(dac, axis=0, keepdims=True)
        gcw_ref[0:1, :] += jnp.sum(dac * am2, axis=0, keepdims=True)
        gcw_ref[1:2, :] += jnp.sum(dac * am1, axis=0, keepdims=True)
        gcw_ref[2:3, :] += jnp.sum(dac * a, axis=0, keepdims=True)
        nxt = carry[...]
        dp1 = _shift_up(dac, 1, nxt)
        dp2 = _shift_up(dac, 2, nxt)
        carry[...] = dac[0:8, :]
        da = dac * cw[2:3, :] + dp1 * cw[1:2, :] + dp2 * cw[0:1, :]
        dab = da.astype(BF16)
        dupb = dup.astype(BF16)
        da_ref[...] = dab
        dup_ref[...] = dupb
        dhn = _dot_nt(dab[:, :fq], wg_v[0]) + _dot_nt(dupb[:, :fq], wu_v[0])
        for p in range(1, n_p):
            dhn += _dot_nt(dab[:, p * fq:(p + 1) * fq], wg_v[p]) + _dot_nt(dupb[:, p * fq:(p + 1) * fq], wu_v[p])
        dx, gg = _rms_bwd(dhn, xhat, r, g)
        gn_ref[...] += jnp.sum(gg, axis=0, keepdims=True)
        dhin_ref[...] = dh_v + dx

    rev = lambda i: (n_blk - 1 - i, 0)
    halo = lambda i: (jnp.maximum((n_blk - 1 - i) * t8 - 1, 0), 0)
    const2 = lambda i: (0, 0)
    return pl.pallas_call(
        body, name=f"ffn_bwd_{layer}", grid=(n_blk,),
        in_specs=[pl.BlockSpec((tm, d), rev), pl.BlockSpec((tm, d), rev), pl.BlockSpec((tm, f), rev),
                  pl.BlockSpec((8, f), halo), pl.BlockSpec((tm, f), rev), pl.BlockSpec((1, d), const2), ANY, ANY, ANY,
                  pl.BlockSpec((8, f), const2), pl.BlockSpec((1, f), const2)],
        out_specs=[pl.BlockSpec((tm, d), rev), pl.BlockSpec((tm, f), rev), pl.BlockSpec((tm, f), rev),
                   pl.BlockSpec((8, f), const2), pl.BlockSpec((1, f), const2), pl.BlockSpec((1, d), const2)],
        out_shape=[jax.ShapeDtypeStruct((s_len, d), F32), jax.ShapeDtypeStruct((s_len, f), BF16),
                   jax.ShapeDtypeStruct((s_len, f), BF16), jax.ShapeDtypeStruct((8, f), F32),
                   jax.ShapeDtypeStruct((1, f), F32), jax.ShapeDtypeStruct((1, d), F32)],
        scratch_shapes=[pltpu.VMEM((n_p, d, fq), BF16), pltpu.VMEM((n_p, d, fq), BF16), pltpu.VMEM((n_p, fq, d), BF16),
                        pltpu.VMEM((8, f), F32), pltpu.SemaphoreType.DMA((3,))],
        compiler_params=_params(1),
    )(h, dh, a, a, up, g_norm, wg_all, wu_all, wd_all, conv_w, conv_b)


def _even_head_lanes(shape, axis):
    return (lax.broadcasted_iota(jnp.int32, shape, axis) & HEAD_DIM) == 0


def _pair_select(lo, hi, shape):
    return jnp.where(lax.broadcasted_iota(jnp.int32, shape, 1) < HEAD_DIM, lo, hi)


def _causal(row0, col0, shape):
    return row0 + lax.broadcasted_iota(jnp.int32, shape, 0) >= col0 + lax.broadcasted_iota(jnp.int32, shape, 1)


N_SPARE = 3


def _spare_selectors(d, key_side):
    lane = jnp.arange(d)[None, :]
    row = jnp.arange(N_SPARE * LANES)[:, None]
    head, part = row % LANES, row // LANES
    off = N_SPARE if key_side else 0
    sel_a = ((head % 2 == 0) & (lane == LANES * (head // 2) + HEAD_DIM + off + part)).astype(F32)
    sel_b = ((head % 2 == 1) & (lane == LANES * (head // 2) + off + part)).astype(F32)
    sign = -1.0 if key_side else 1.0
    ones_off = 0 if key_side else N_SPARE
    in_pair = jnp.arange(d)[None, :] % LANES
    ones_a = ((in_pair >= HEAD_DIM + ones_off) & (in_pair < HEAD_DIM + ones_off + N_SPARE)).astype(F32)
    ones_b = ((in_pair >= ones_off) & (in_pair < ones_off + N_SPARE)).astype(F32)
    return (sign * sel_a).astype(BF16), (sign * sel_b).astype(BF16), ones_a, ones_b


def _parts(x):
    return jnp.concatenate(_split3(x), axis=1)


def _fox_proj_fwd(h, g_norm, wq, wk, wv, wf, bf, sel_q, sel_k, tm):
    s_len, d = h.shape
    sq_a, sq_b, oq_a, oq_b = sel_q
    sk_a, sk_b, ok_a, ok_b = sel_k

    def body(h_ref, g_ref, wq_hbm, wk_hbm, wv_hbm, wf_ref, bf_ref, sqa_ref, sqb_ref, oqa_ref, oqb_ref,
             ska_ref, skb_ref, oka_ref, okb_ref,
             hn_ref, qa_ref, qb_ref, kat_ref, kbt_ref, va_ref, vb_ref, vat_ref, vbt_ref, z_ref,
             wq_v, wk_v, wv_v, total, sem):
        i = pl.program_id(0)
        _load_once([(wq_hbm, wq_v), (wk_hbm, wk_v), (wv_hbm, wv_v)], sem)

        @pl.when(i == 0)
        def _():
            total[...] = jnp.zeros_like(total)

        x = h_ref[...]
        _, _, y = _rms_fwd(x, g_ref[...])
        hn = y.astype(BF16)
        hn_ref[...] = hn
        z = _dot(hn, wf_ref[...]) + bf_ref[...]
        z_ref[...] = z
        logf = jnp.minimum(z, 0.0) - jnp.log(1.0 + jnp.exp(-jnp.abs(z)))
        tri = (lax.broadcasted_iota(jnp.int32, (tm, tm), 0) >= lax.broadcasted_iota(jnp.int32, (tm, tm), 1))
        cum = _dot3_rhs(jnp.where(tri, 1.0, 0.0).astype(BF16), logf) + total[0:1, :]
        total[...] = jnp.broadcast_to(cum[tm - 1:tm, :], total.shape)
        parts = _parts(cum)

        even = _even_head_lanes((tm, d), 1)
        q = _dot(hn, wq_v[...]) * (HEAD_DIM ** -0.5)
        qa_ref[...] = jnp.where(even, q, _dot(parts, sqa_ref[...]) + oqa_ref[...]).astype(BF16)
        qb_ref[...] = jnp.where(even, _dot(parts, sqb_ref[...]) + oqb_ref[...], q).astype(BF16)
        k = _dot(hn, wk_v[...])
        ka = jnp.where(even, k, _dot(parts, ska_ref[...]) + oka_ref[...])
        kb = jnp.where(even, _dot(parts, skb_ref[...]) + okb_ref[...], k)
        kat_ref[...] = ka.T.astype(BF16)
        kbt_ref[...] = kb.T.astype(BF16)
        v = _dot(hn, wv_v[...])
        va = jnp.where(even, v, oka_ref[...])
        vb = jnp.where(even, okb_ref[...], v)
        va_ref[...] = va.astype(BF16)
        vb_ref[...] = vb.astype(BF16)
        vat_ref[...] = va.T.astype(BF16)
        vbt_ref[...] = vb.T.astype(BF16)

    row = lambda i: (i, 0)
    col = lambda i: (0, i)
    const2 = lambda i: (0, 0)
    sd = jax.ShapeDtypeStruct((s_len, d), BF16)
    ds_ = jax.ShapeDtypeStruct((d, s_len), BF16)
    rs, cs = pl.BlockSpec((tm, d), row), pl.BlockSpec((d, tm), col)
    sel = pl.BlockSpec((N_SPARE * LANES, d), const2)
    one = pl.BlockSpec((1, d), const2)
    return pl.pallas_call(
        body, name="fox_proj_fwd", grid=(s_len // tm,),
        in_specs=[rs, one, ANY, ANY, ANY, pl.BlockSpec((d, LANES), const2), pl.BlockSpec((1, LANES), const2),
                  sel, sel, one, one, sel, sel, one, one],
        out_specs=[rs, rs, rs, cs, cs, rs, rs, cs, cs, pl.BlockSpec((tm, LANES), row)],
        out_shape=[sd, sd, sd, ds_, ds_, sd, sd, ds_, ds_, jax.ShapeDtypeStruct((s_len, LANES), F32)],
        scratch_shapes=[pltpu.VMEM((d, d), BF16), pltpu.VMEM((d, d), BF16), pltpu.VMEM((d, d), BF16),
                        pltpu.VMEM((8, LANES), F32), pltpu.SemaphoreType.DMA((3,))],
        compiler_params=_params(1),
    )(h, g_norm, wq, wk, wv, wf, bf, sq_a, sq_b, oq_a, oq_b, sk_a, sk_b, ok_a, ok_b)


def _spare_cols(x, base):
    xf = x[:, base:base + N_SPARE].astype(F32)
    return xf[:, 0:1] + xf[:, 1:2] + xf[:, 2:3]


def _with_query_term(x, term, base):
    lane = lax.broadcasted_iota(jnp.int32, x.shape, 1)
    hi, mid, lo = _split3(term)
    out = jnp.where(lane == base, hi.astype(F32), x)
    out = jnp.where(lane == base + 1, mid.astype(F32), out)
    out = jnp.where(lane == base + 2, lo.astype(F32), out)
    return jnp.where((lane >= base + N_SPARE) & (lane < base + 2 * N_SPARE), 1.0, out)


def _flash_fwd(qa, qb, kat, kbt, va, vb):
    s_len, d = qa.shape
    sub = ATT_BLOCK
    n_sub = 2 if s_len % (2 * sub) == 0 else 1
    t = n_sub * sub
    w = min(ATT_CHUNK, s_len)
    n_pair = d // LANES
    n_q = s_len // t
    bases = (HEAD_DIM, 0)
    chains = [(r, hh) for r in range(n_sub) for hh in range(2)]

    def body(qa_ref, qb_ref, kat_ref, kbt_ref, va_ref, vb_ref, o_ref, qa2_ref, qb2_ref, qat2_ref, qbt2_ref):
        i = pl.program_id(1)
        q_refs = (qa_ref, qb_ref)
        qs = [q_refs[hh][r * sub:(r + 1) * sub, :] for r, hh in chains]
        kts = (kat_ref, kbt_ref)
        vs = (va_ref, vb_ref)

        def step(kb, carry, masked, width=w):
            off = pl.multiple_of(kb * w, w)
            cols = pl.ds(off, width)
            scores = [_dot(qs[c], kts[hh][:, cols]) for c, (r, hh) in enumerate(chains)]
            probs, stats = [], []
            for c, (r, hh) in enumerate(chains):
                m, _ = carry[c]
                s = scores[c]
                if masked:
                    s = jnp.where(_causal(i * t + r * sub, off, (sub, width)), s, NEG_BIG)
                m_new = jnp.maximum(m, jnp.max(s, axis=1, keepdims=True))
                probs.append(jnp.exp(s - m_new).astype(BF16))
                stats.append((m_new, jnp.exp(m - m_new)))
            return tuple((stats[c][0], carry[c][1] * stats[c][1] + _dot(probs[c], vs[hh][cols, :]))
                         for c, (r, hh) in enumerate(chains))

        init = ((jnp.full((sub, 1), NEG_BIG, F32), jnp.zeros((sub, LANES), F32)),) * len(chains)
        diag = (i * t) // w
        carry = lax.fori_loop(0, diag, lambda kb, c: step(kb, c, False), init)
        carry = step(diag, carry, True)
        for r in range(n_sub):
            outs, q2 = [], []
            for hh in range(2):
                m, acc = carry[2 * r + hh]
                l = acc[:, bases[hh]:bases[hh] + 1]
                outs.append(acc / l)
                term = _spare_cols(qs[2 * r + hh], bases[hh]) - (m + jnp.log(l))
                q2.append(_with_query_term(qs[2 * r + hh].astype(F32), term, bases[hh]))
            rows = slice(r * sub, (r + 1) * sub)
            o_ref[rows, :] = _pair_select(outs[0], outs[1], (sub, LANES))
            qa2_ref[rows, :] = q2[0].astype(BF16)
            qb2_ref[rows, :] = q2[1].astype(BF16)
            qat2_ref[:, rows] = q2[0].T.astype(BF16)
            qbt2_ref[:, rows] = q2[1].T.astype(BF16)

    qblk = pl.BlockSpec((t, LANES), lambda j, i: (i, j))
    qblk_t = pl.BlockSpec((LANES, t), lambda j, i: (j, i))
    whole_t = pl.BlockSpec((LANES, s_len), lambda j, i: (j, 0))
    whole = pl.BlockSpec((s_len, LANES), lambda j, i: (0, j))
    sd = jax.ShapeDtypeStruct((s_len, d), BF16)
    ds_ = jax.ShapeDtypeStruct((d, s_len), BF16)
    return pl.pallas_call(
        body, name="flash_fwd", grid=(n_pair, n_q),
        in_specs=[qblk, qblk, whole_t, whole_t, whole, whole],
        out_specs=[qblk, qblk, qblk, qblk_t, qblk_t],
        out_shape=[jax.ShapeDtypeStruct((s_len, d), F32), sd, sd, ds_, ds_],
        compiler_params=_params(2),
    )(qa, qb, kat, kbt, va, vb)


def _flash_bwd(qa, qb, qat, qbt, kat, kbt, vat, vbt, doa, dob, doat, dobt):
    s_len, d = qa.shape
    sub = ATT_BLOCK
    n_sub = 2 if s_len % (2 * sub) == 0 else 1
    t = n_sub * sub
    w = min(ATT_CHUNK, s_len)
    n_pair = d // LANES
    n_q = s_len // t
    hd = HEAD_DIM
    chains = [(r, hh) for r in range(n_sub) for hh in range(2)]

    def body(qa_ref, qb_ref, qat_ref, qbt_ref, kat_hbm, kbt_hbm, vat_hbm, vbt_hbm, doa_ref, dob_ref, doat_ref, dobt_ref,
             dqt_ref, dkt_ref, dvt_ref, rs_ref, cs_ref,
             kat_v, kbt_v, vat_v, vbt_v, dkt_acc, dvt_acc, cs_acc, sem):
        j = pl.program_id(0)
        i = pl.program_id(1)

        @pl.when(i == 0)
        def _():
            rows = pl.ds(pl.multiple_of(j * LANES, LANES), LANES)
            copies = [pltpu.make_async_copy(src.at[rows, :], dst, sem.at[n]) for n, (src, dst) in enumerate([
                (kat_hbm, kat_v), (kbt_hbm, kbt_v), (vat_hbm, vat_v), (vbt_hbm, vbt_v)])]
            for cp in copies:
                cp.start()
            dkt_acc[...] = jnp.zeros_like(dkt_acc)
            dvt_acc[...] = jnp.zeros_like(dvt_acc)
            cs_acc[...] = jnp.zeros_like(cs_acc)
            for cp in copies:
                cp.wait()

        q_refs, do_refs = (qa_ref, qb_ref), (doa_ref, dob_ref)
        qs = [q_refs[hh][r * sub:(r + 1) * sub, :] for r, hh in chains]
        dos = [do_refs[hh][r * sub:(r + 1) * sub, :] for r, hh in chains]
        own = (slice(0, hd), slice(hd, 2 * hd))
        spare = (slice(hd, hd + 8), slice(0, 8))
        used = (slice(0, hd + 16), slice(0, 2 * hd))
        qts = (qat_ref[used[0], :], qbt_ref[used[1], :])
        dots = (doat_ref[own[0], :], dobt_ref[own[1], :])
        kts, vts = (kat_v, kbt_v), (vat_v, vbt_v)

        def step(kb, carry, masked, width=w):
            off = pl.multiple_of(kb * w, w)
            cols = pl.ds(off, width)
            scores = [_dot(qs[c], kts[hh][:, cols]) for c, (r, hh) in enumerate(chains)]
            dps = [_dot(dos[c], vts[hh][:, cols]) for c, (r, hh) in enumerate(chains)]
            ps, dss = [], []
            for c, (r, hh) in enumerate(chains):
                s = scores[c]
                if masked:
                    s = jnp.where(_causal(i * t + r * sub, off, (sub, width)), s, NEG_BIG)
                p = jnp.exp(s)
                dss.append((p * dps[c]).astype(BF16))
                ps.append(p.astype(BF16))
            out = tuple(carry[c] + _dot_nt(kts[hh][used[hh], cols], dss[c]) for c, (r, hh) in enumerate(chains))
            for hh in range(2):
                p_all = jnp.concatenate([ps[2 * r + hh] for r in range(n_sub)], axis=0)
                ds_all = jnp.concatenate([dss[2 * r + hh] for r in range(n_sub)], axis=0)
                dvt_acc[own[hh], cols] += _dot(dots[hh], p_all)
                with_sums = _dot(qts[hh], ds_all)
                dkt_acc[own[hh], cols] += with_sums[own[hh], :]
                cs_acc[8 * hh:8 * hh + 8, cols] += with_sums[spare[hh], :]
            return out

        diag = (i * t) // w
        init = (jnp.zeros((hd + 16, sub), F32), jnp.zeros((2 * hd, sub), F32)) * n_sub
        carry = lax.fori_loop(0, diag, lambda kb, c: step(kb, c, False), init)
        carry = lax.cond((i * t) % w + t <= w // 2,
                         lambda c: step(diag, c, True, w // 2), lambda c: step(diag, c, True), carry)
        for c, (r, hh) in enumerate(chains):
            at = slice(r * sub, (r + 1) * sub)
            dqt_ref[own[hh], at] = (carry[c][own[hh], :] * (hd ** -0.5)).astype(BF16)
            rs_ref[0, 8 * hh:8 * hh + 8, at] = carry[c][spare[hh], :]

        @pl.when(i == n_q - 1)
        def _():
            dkt_ref[...] = dkt_acc[...].astype(BF16)
            dvt_ref[...] = dvt_acc[...].astype(BF16)
            cs_ref[0] = cs_acc[...]

    qblk = pl.BlockSpec((t, LANES), lambda j, i: (i, j))
    qblk_t = pl.BlockSpec((LANES, t), lambda j, i: (j, i))
    whole_t = pl.BlockSpec((LANES, s_len), lambda j, i: (j, 0))
    ds_ = jax.ShapeDtypeStruct((d, s_len), BF16)
    sums = jax.ShapeDtypeStruct((n_pair, 16, s_len), F32)
    return pl.pallas_call(
        body, name="flash_bwd", grid=(n_pair, n_q),
        in_specs=[qblk, qblk, qblk_t, qblk_t, ANY, ANY, ANY, ANY, qblk, qblk, qblk_t, qblk_t],
        out_specs=[qblk_t, whole_t, whole_t, pl.BlockSpec((1, 16, t), lambda j, i: (j, 0, i)),
                   pl.BlockSpec((1, 16, s_len), lambda j, i: (j, 0, 0))],
        out_shape=[ds_, ds_, ds_, sums, sums],
        scratch_shapes=[pltpu.VMEM((LANES, s_len), BF16), pltpu.VMEM((LANES, s_len), BF16),
                        pltpu.VMEM((LANES, s_len), BF16), pltpu.VMEM((LANES, s_len), BF16),
                        pltpu.VMEM((LANES, s_len), F32), pltpu.VMEM((LANES, s_len), F32),
                        pltpu.VMEM((16, s_len), F32), pltpu.SemaphoreType.DMA((4,))],
        compiler_params=_params(2),
    )(qa, qb, qat, qbt, kat, kbt, vat, vbt, doa, dob, doat, dobt)


def _wgrad_t(at, b, name):
    k, s_len = at.shape
    n = b.shape[1]
    tn, tk, ts = min(n, 1024), min(k, 1024), min(s_len, 1024)

    def body(a_ref, b_ref, o_ref):
        @pl.when(pl.program_id(2) == 0)
        def _():
            o_ref[...] = jnp.zeros_like(o_ref)
        o_ref[...] += _dot(a_ref[...].astype(BF16), b_ref[...].astype(BF16))

    return pl.pallas_call(
        body, name=name, grid=(k // tk, n // tn, s_len // ts),
        in_specs=[pl.BlockSpec((tk, ts), lambda a, b_, c: (a, c)), pl.BlockSpec((ts, tn), lambda a, b_, c: (c, b_))],
        out_specs=pl.BlockSpec((tk, tn), lambda a, b_, c: (a, b_)),
        out_shape=jax.ShapeDtypeStruct((k, n), F32),
        compiler_params=_params(3),
    )(at, b)


def _oproj_bwd(dh, o, wo, seg, sel_q, tm):
    s_len, d = dh.shape
    sq_a, sq_b, _, _ = sel_q

    def body(dh_ref, o_ref, wo_hbm, seg_ref, sqa_ref, sqb_ref, doa_ref, dob_ref, doat_ref, dobt_ref, wo_v, sem):
        _load_once([(wo_hbm, wo_v)], sem)
        do = _dot_nt(dh_ref[...].astype(BF16), wo_v[...])
        parts = _parts(-_dot3_lhs(do * o_ref[...], seg_ref[...]))
        even = _even_head_lanes((tm, d), 1)
        doa = jnp.where(even, do, _dot(parts, sqa_ref[...]))
        dob = jnp.where(even, _dot(parts, sqb_ref[...]), do)
        doa_ref[...] = doa.astype(BF16)
        dob_ref[...] = dob.astype(BF16)
        doat_ref[...] = doa.T.astype(BF16)
        dobt_ref[...] = dob.T.astype(BF16)

    row = lambda i: (i, 0)
    const2 = lambda i: (0, 0)
    rs, cs = pl.BlockSpec((tm, d), row), pl.BlockSpec((d, tm), lambda i: (0, i))
    sel = pl.BlockSpec((N_SPARE * LANES, d), const2)
    sd = jax.ShapeDtypeStruct((s_len, d), BF16)
    ds_ = jax.ShapeDtypeStruct((d, s_len), BF16)
    return pl.pallas_call(
        body, name="oproj_bwd", grid=(s_len // tm,),
        in_specs=[rs, rs, ANY, pl.BlockSpec((d, LANES), const2), sel, sel],
        out_specs=[rs, rs, cs, cs], out_shape=[sd, sd, ds_, ds_],
        scratch_shapes=[pltpu.VMEM((d, d), BF16), pltpu.SemaphoreType.DMA((1,))],
        compiler_params=_params(1),
    )(dh, o, wo, seg, sq_a, sq_b)


def _oproj_fwd(h, o, wo, tm):
    s_len, d = h.shape

    def body(h_ref, o_ref, wo_hbm, hout_ref, wo_v, sem):
        _load_once([(wo_hbm, wo_v)], sem)
        hout_ref[...] = h_ref[...] + _dot(o_ref[...].astype(BF16), wo_v[...])

    row = lambda i: (i, 0)
    return pl.pallas_call(
        body, name="oproj_fwd", grid=(s_len // tm,),
        in_specs=[pl.BlockSpec((tm, d), row), pl.BlockSpec((tm, d), row), ANY],
        out_specs=pl.BlockSpec((tm, d), row),
        out_shape=jax.ShapeDtypeStruct((s_len, d), F32),
        scratch_shapes=[pltpu.VMEM((d, d), BF16), pltpu.SemaphoreType.DMA((1,))],
        compiler_params=_params(1),
    )(h, o, wo)


def _forget_bwd(dcum, z, tm):
    s_len = dcum.shape[0]
    n_blk = s_len // tm

    def body(dc_ref, z_ref, dfl_ref, gb_ref, total):
        i = pl.program_id(0)

        @pl.when(i == 0)
        def _():
            total[...] = jnp.zeros_like(total)
            gb_ref[...] = jnp.zeros_like(gb_ref)

        upper = (lax.broadcasted_iota(jnp.int32, (tm, tm), 0) <= lax.broadcasted_iota(jnp.int32, (tm, tm), 1))
        suffix = _dot3_rhs(jnp.where(upper, 1.0, 0.0).astype(BF16), dc_ref[...]) + total[0:1, :]
        total[...] = jnp.broadcast_to(suffix[0:1, :], total.shape)
        dfl = suffix * _sigmoid(-z_ref[...])
        dfl_ref[...] = dfl
        gb_ref[...] += jnp.sum(dfl, axis=0, keepdims=True)

    rev = lambda i: (n_blk - 1 - i, 0)
    return pl.pallas_call(
        body, name="forget_bwd", grid=(n_blk,),
        in_specs=[pl.BlockSpec((tm, LANES), rev), pl.BlockSpec((tm, LANES), rev)],
        out_specs=[pl.BlockSpec((tm, LANES), rev), pl.BlockSpec((1, LANES), lambda i: (0, 0))],
        out_shape=[jax.ShapeDtypeStruct((s_len, LANES), F32), jax.ShapeDtypeStruct((1, LANES), F32)],
        scratch_shapes=[pltpu.VMEM((8, LANES), F32)],
        compiler_params=_params(1),
    )(dcum, z)


def _fox_proj_bwd(h, dh, dqt, dkt, dvt, dfl, g_norm, wq, wk, wv, wf, tm):
    s_len, d = h.shape

    def body(h_ref, dh_ref, dqt_ref, dkt_ref, dvt_ref, dfl_ref, g_ref, wq_hbm, wk_hbm, wv_hbm, wf_ref,
             dhin_ref, dflb_ref, gn_ref, wq_v, wk_v, wv_v, sem):
        _load_once([(wq_hbm, wq_v), (wk_hbm, wk_v), (wv_hbm, wv_v)], sem)

        @pl.when(pl.program_id(0) == 0)
        def _():
            gn_ref[...] = jnp.zeros_like(gn_ref)

        g = g_ref[...]
        xhat, r, _ = _rms_fwd(h_ref[...], g)
        dflb = dfl_ref[...].astype(BF16)
        dflb_ref[...] = dflb
        from_qkv = (_dot(wq_v[...], dqt_ref[...]) + _dot(wk_v[...], dkt_ref[...])
                    + _dot(wv_v[...], dvt_ref[...]))
        dhn = _dot_nt(dflb, wf_ref[...]) + from_qkv.T
        dx, gg = _rms_bwd(dhn, xhat, r, g)
        gn_ref[...] += jnp.sum(gg, axis=0, keepdims=True)
        dhin_ref[...] = dh_ref[...] + dx

    row = lambda i: (i, 0)
    const2 = lambda i: (0, 0)
    rs = pl.BlockSpec((tm, d), row)
    cs = pl.BlockSpec((d, tm), lambda i: (0, i))
    return pl.pallas_call(
        body, name="fox_proj_bwd", grid=(s_len // tm,),
        in_specs=[rs, rs, cs, cs, cs, pl.BlockSpec((tm, LANES), row), pl.BlockSpec((1, d), const2), ANY, ANY, ANY,
                  pl.BlockSpec((d, LANES), const2)],
        out_specs=[rs, pl.BlockSpec((tm, LANES), row), pl.BlockSpec((1, d), const2)],
        out_shape=[jax.ShapeDtypeStruct((s_len, d), F32), jax.ShapeDtypeStruct((s_len, LANES), BF16),
                   jax.ShapeDtypeStruct((1, d), F32)],
        scratch_shapes=[pltpu.VMEM((d, d), BF16), pltpu.VMEM((d, d), BF16), pltpu.VMEM((d, d), BF16),
                        pltpu.SemaphoreType.DMA((3,))],
        compiler_params=_params(1),
    )(h, dh, dqt, dkt, dvt, dfl, g_norm, wq, wk, wv, wf)


def _loss_head(h, target, g_final, tm):
    s_len, d = h.shape
    n_blk = s_len // tm

    def body(h_ref, t_ref, g_ref, dh_ref, loss_ref, gg_ref, sq):
        i = pl.program_id(0)

        @pl.when(i == 0)
        def _():
            sq[...] = jnp.zeros_like(sq)
            gg_ref[...] = jnp.zeros_like(gg_ref)

        g = g_ref[...]
        xhat, r, y = _rms_fwd(h_ref[...], g)
        err = y - t_ref[...]
        sq[...] += jnp.sum(err * err, axis=0, keepdims=True)
        dx, gg = _rms_bwd(err * (1.0 / d), xhat, r, g)
        gg_ref[...] += jnp.sum(gg, axis=0, keepdims=True)
        dh_ref[...] = dx

        @pl.when(i == n_blk - 1)
        def _():
            loss_ref[...] = jnp.broadcast_to(jnp.sum(sq[...], axis=1, keepdims=True) * (0.5 / d), loss_ref.shape)

    row = lambda i: (i, 0)
    const2 = lambda i: (0, 0)
    return pl.pallas_call(
        body, name="loss_head", grid=(n_blk,),
        in_specs=[pl.BlockSpec((tm, d), row), pl.BlockSpec((tm, d), row), pl.BlockSpec((1, d), const2)],
        out_specs=[pl.BlockSpec((tm, d), row), pl.BlockSpec((1, LANES), const2), pl.BlockSpec((1, d), const2)],
        out_shape=[jax.ShapeDtypeStruct((s_len, d), F32), jax.ShapeDtypeStruct((1, LANES), F32),
                   jax.ShapeDtypeStruct((1, d), F32)],
        scratch_shapes=[pltpu.VMEM((1, d), F32)],
        compiler_params=_params(1),
    )(h, target, g_final)


def _wgrad(x, dy, n_piece, name):
    s_len, k = x.shape
    n = dy.shape[1]
    tn = min(n // n_piece, 1024)
    tk = min(k, 1024)
    ts = min(s_len, 1024)
    per_piece = (n // n_piece) // tn

    def body(x_ref, dy_ref, o_ref):
        @pl.when(pl.program_id(2) == 0)
        def _():
            o_ref[...] = jnp.zeros_like(o_ref)
        o_ref[0] += _dot_tn(x_ref[...].astype(BF16), dy_ref[...].astype(BF16))

    return pl.pallas_call(
        body, name=name, grid=(k // tk, n // tn, s_len // ts),
        in_specs=[pl.BlockSpec((ts, tk), lambda a, b, c: (c, a)), pl.BlockSpec((ts, tn), lambda a, b, c: (c, b))],
        out_specs=pl.BlockSpec((1, tk, tn), lambda a, b, c: (b // per_piece, a, b % per_piece)),
        out_shape=jax.ShapeDtypeStruct((n_piece, k, n // n_piece), F32),
        compiler_params=_params(3),
    )(x, dy)


def _pair_sum(g, recv, core, name):
    n_piece, rows, c = g.shape
    half = rows // 2
    tr = min(half, 512)
    nb = half // tr

    def body(core_ref, g_ref, r_ref, o_ref, ob_ref):
        total = g_ref[...] + r_ref[...]
        o_ref[...] = total
        ob_ref[...] = total.astype(BF16)

    blk = pl.BlockSpec((1, tr, c), lambda p, i, core_ref: (p, i, 0))
    return pl.pallas_call(
        body, name=name,
        out_shape=[jax.ShapeDtypeStruct((n_piece, half, c), F32), jax.ShapeDtypeStruct((n_piece, half, c), BF16)],
        grid_spec=pltpu.PrefetchScalarGridSpec(
            num_scalar_prefetch=1, grid=(n_piece, nb),
            in_specs=[pl.BlockSpec((1, tr, c), lambda p, i, core_ref: (p, core_ref[0] * nb + i, 0)), blk],
            out_specs=[blk, blk]),
        compiler_params=_params(2),
    )(core, g, recv)


def _chip_sum(halves, recv, chip, name):
    _, h, c = halves.shape
    tr = min(h, 512)

    def body(chip_ref, own_ref, r_ref, o_ref):
        o_ref[...] = ((own_ref[0] + r_ref[0].astype(F32)) + r_ref[1].astype(F32)) + r_ref[2].astype(F32)

    return pl.pallas_call(
        body, name=name, out_shape=jax.ShapeDtypeStruct((h, c), F32),
        grid_spec=pltpu.PrefetchScalarGridSpec(
            num_scalar_prefetch=1, grid=(h // tr,),
            in_specs=[pl.BlockSpec((1, tr, c), lambda i, chip_ref: (chip_ref[0], i, 0)),
                      pl.BlockSpec((3, tr, c), lambda i, chip_ref: (0, i, 0))],
            out_specs=pl.BlockSpec((tr, c), lambda i, chip_ref: (i, 0))),
        compiler_params=_params(1),
    )(chip, halves, recv)


def _adamw_math(w, m, v, g):
    m_new = ADAM_B1 * m + (1.0 - ADAM_B1) * g
    v_new = ADAM_B2 * v + (1.0 - ADAM_B2) * (g * g)
    m_hat = m_new / (1.0 - ADAM_B1 ** ADAM_STEP)
    v_hat = v_new / (1.0 - ADAM_B2 ** ADAM_STEP)
    return -ADAM_LR * (m_hat / (jnp.sqrt(v_hat) + ADAM_EPS) + ADAM_WD * w), m_new, v_new


def _adamw(w, m, v, g, name):
    rows, c = w.shape
    tr = min(rows, 256)

    def body(w_ref, m_ref, v_ref, g_ref, d_ref, mo_ref, vo_ref):
        d_ref[...], mo_ref[...], vo_ref[...] = _adamw_math(w_ref[...], m_ref[...], v_ref[...], g_ref[...])

    spec = pl.BlockSpec((tr, c), lambda i: (i, 0))
    shape = jax.ShapeDtypeStruct((rows, c), F32)
    return pl.pallas_call(
        body, name=name, grid=(rows // tr,),
        in_specs=[spec] * 4, out_specs=[spec] * 3, out_shape=[shape] * 3,
        compiler_params=_params(1),
    )(w, m, v, g)


def _adamw_halves(w, m, v, g_own, g_other, core, name):
    rows, c = w.shape
    half = rows // 2
    tr = min(half, 256)
    nb = half // tr

    def body(core_ref, w_ref, m_ref, v_ref, own_ref, other_ref, g_ref, d_ref, mo_ref, vo_ref):
        mine = (pl.program_id(0) // nb) == core_ref[0]
        g = jnp.where(mine, own_ref[...], other_ref[...])
        g_ref[...] = g
        d_ref[...], mo_ref[...], vo_ref[...] = _adamw_math(w_ref[...], m_ref[...], v_ref[...], g)

    spec = pl.BlockSpec((tr, c), lambda i, core_ref: (i, 0))
    own = pl.BlockSpec((tr, c), lambda i, core_ref: (jnp.clip(i - core_ref[0] * nb, 0, nb - 1), 0))
    other = pl.BlockSpec((tr, c), lambda i, core_ref: (jnp.clip(i - (1 - core_ref[0]) * nb, 0, nb - 1), 0))
    shape = jax.ShapeDtypeStruct((rows, c), F32)
    return pl.pallas_call(
        body, name=name, out_shape=[shape] * 4,
        grid_spec=pltpu.PrefetchScalarGridSpec(
            num_scalar_prefetch=1, grid=(rows // tr,),
            in_specs=[spec, spec, spec, own, other], out_specs=[spec] * 4),
        compiler_params=_params(1),
    )(core, w, m, v, g_own, g_other)


def _place():
    x, y, c = lax.axis_index("x"), lax.axis_index("y"), lax.axis_index("c")
    chips = [(1 - x, y), (x, 1 - y), (1 - x, 1 - y)]
    return x, y, c, chips


def _half_of(ref, half, which):
    start = which * half
    if half % 8 == 0:
        start = pl.multiple_of(start, 8)
    return ref.at[pl.ds(start, half)]


def _gather_copies(ins, outs, send_sems, recv_sems):
    x, y, c, chips = _place()
    mine = 2 * x + y
    copies = []
    for k, (src, dst) in enumerate(zip(ins, outs)):
        half = src.shape[0] // 2
        copies.append(pltpu.make_async_remote_copy(
            src_ref=src, dst_ref=dst.at[mine], send_sem=send_sems.at[4 * k + 3], recv_sem=recv_sems.at[4 * k + 3],
            device_id=(x, y, 1 - c), device_id_type=MESH))
        for j, (tx, ty) in enumerate(chips):
            copies.append(pltpu.make_async_remote_copy(
                src_ref=_half_of(src, half, c), dst_ref=_half_of(dst.at[mine], half, c),
                send_sem=send_sems.at[4 * k + j], recv_sem=recv_sems.at[4 * k + j],
                device_id=(tx, ty, c), device_id_type=MESH))
    return copies


def _gather_ici(shards):
    n = len(shards)

    def body(*refs):
        copies = _gather_copies(refs[:n], refs[n:2 * n], refs[2 * n], refs[2 * n + 1])
        for cp in copies:
            cp.start()
        for cp in copies:
            cp.wait()

    return pl.pallas_call(
        body, name="weights_gather_ici",
        in_specs=[ANY] * n, out_specs=[ANY] * n,
        out_shape=[jax.ShapeDtypeStruct((4,) + s.shape, s.dtype) for s in shards],
        scratch_shapes=[pltpu.SemaphoreType.DMA((4 * n,)), pltpu.SemaphoreType.DMA((4 * n,))],
    )(*shards)


HBM_SPEC = pl.BlockSpec(memory_space=pltpu.HBM)
SEM_SPEC = pl.BlockSpec(memory_space=pltpu.SEMAPHORE)
IN_FLIGHT = pltpu.SideEffectType.DATAFLOW_SIDE_EFFECTING


def _gather_ici_start(shards, after):
    n = len(shards)

    def body(*refs):
        for cp in _gather_copies(refs[:n], refs[n:2 * n], refs[2 * n + 1], refs[2 * n + 2]):
            cp.start()
        token = refs[-1]
        token[...] = jnp.zeros_like(token)

    lands = [lax.empty((4,) + s.shape, s.dtype) for s in shards]
    out = pl.pallas_call(
        body, name="weights_gather_start",
        out_shape=(pltpu.SemaphoreType.DMA((4 * n,)), pltpu.SemaphoreType.DMA((4 * n,)),
                   *[pltpu.HBM(s.shape, s.dtype) for s in shards], *[pltpu.HBM(l.shape, l.dtype) for l in lands],
                   jax.ShapeDtypeStruct((8, LANES), F32)),
        in_specs=[HBM_SPEC] * (2 * n) + [ANY],
        out_specs=(SEM_SPEC, SEM_SPEC, *[HBM_SPEC] * (2 * n), pl.BlockSpec(memory_space=pltpu.VMEM)),
        input_output_aliases={k: 2 + k for k in range(2 * n)},
        compiler_params=pltpu.CompilerParams(has_side_effects=IN_FLIGHT),
    )(*[pltpu.with_memory_space_constraint(a, pltpu.HBM) for a in list(shards) + lands], after)
    return out[0], out[1], out[2:2 + n], out[2 + n:2 + 2 * n], out[-1]


def _gather_ici_wait(send_sems, recv_sems, sources, lands, after):
    n = len(sources)

    def body(*refs):
        for cp in _gather_copies(refs[:n], refs[n:2 * n], refs[2 * n], refs[2 * n + 1]):
            cp.wait_send()
            cp.wait_recv()

    out = pl.pallas_call(
        body, name="weights_gather_wait",
        out_shape=[pltpu.HBM(a.shape, a.dtype) for a in list(sources) + list(lands)],
        in_specs=[HBM_SPEC] * (2 * n) + [SEM_SPEC, SEM_SPEC, ANY], out_specs=[HBM_SPEC] * (2 * n),
        input_output_aliases={k: k for k in range(2 * n)},
        compiler_params=pltpu.CompilerParams(has_side_effects=IN_FLIGHT),
    )(*sources, *lands, send_sems, recv_sems, after)
    return out[n:]


def _gather_pair(gathered):
    n = len(gathered)

    def body(*refs):
        outs = refs[n:2 * n]
        send_sems, recv_sems = refs[2 * n:]
        x, y, c, chips = _place()
        sends = []
        for k in range(n):
            half = gathered[k].shape[1] // 2
            for j, (tx, ty) in enumerate(chips):
                piece = _half_of(outs[k].at[2 * tx + ty], half, c)
                sends.append(pltpu.make_async_remote_copy(
                    src_ref=piece, dst_ref=piece, send_sem=send_sems.at[k, j], recv_sem=recv_sems.at[k, j],
                    device_id=(x, y, 1 - c), device_id_type=MESH))
        for cp in sends:
            cp.start()
        for cp in sends:
            cp.wait()

    return pl.pallas_call(
        body, name="weights_gather_pair",
        in_specs=[ANY] * n, out_specs=[ANY] * n,
        out_shape=[jax.ShapeDtypeStruct(g.shape, g.dtype) for g in gathered],
        input_output_aliases={k: k for k in range(n)},
        scratch_shapes=[pltpu.SemaphoreType.DMA((n, 3)), pltpu.SemaphoreType.DMA((n, 3))],
    )(*gathered)


def _pair_exchange(grads, tag):
    n = len(grads)

    def body(*refs):
        ins, outs = refs[:n], refs[n:2 * n]
        send_sems, recv_sems = refs[2 * n:]
        x, y, c, _ = _place()
        copies = []
        for k in range(n):
            half = grads[k].shape[1] // 2
            other = ins[k].at[:, pl.ds(pl.multiple_of((1 - c) * half, 8), half), :]
            copies.append(pltpu.make_async_remote_copy(
                src_ref=other, dst_ref=outs[k], send_sem=send_sems.at[k], recv_sem=recv_sems.at[k],
                device_id=(x, y, 1 - c), device_id_type=MESH))
        for cp in copies:
            cp.start()
        for cp in copies:
            cp.wait()

    return pl.pallas_call(
        body, name=f"grads_pair_exchange_{tag}",
        in_specs=[ANY] * n, out_specs=[ANY] * n,
        out_shape=[jax.ShapeDtypeStruct((4, g.shape[1] // 2, g.shape[2]), F32) for g in grads],
        scratch_shapes=[pltpu.SemaphoreType.DMA((n,)), pltpu.SemaphoreType.DMA((n,))],
    )(*grads)


def _scatter_copies(ins, outs, send_sems, recv_sems):
    x, y, c, chips = _place()
    return [pltpu.make_async_remote_copy(
        src_ref=ins[k].at[2 * tx + ty], dst_ref=outs[k].at[j], send_sem=send_sems.at[3 * k + j],
        recv_sem=recv_sems.at[3 * k + j], device_id=(tx, ty, c), device_id_type=MESH)
        for k in range(len(ins)) for j, (tx, ty) in enumerate(chips)]


def _chip_scatter_start(halves, tag, after=()):
    n = len(halves)
    n_in = 2 * n + len(after)

    def body(*refs):
        for cp in _scatter_copies(refs[:n], refs[n:2 * n], refs[n_in], refs[n_in + 1]):
            cp.start()
        token = refs[-1]
        token[...] = jnp.zeros_like(token)

    lands = [lax.empty((3,) + hv.shape[1:], hv.dtype) for hv in halves]
    out = pl.pallas_call(
        body, name=f"grads_chip_scatter_start_{tag}",
        out_shape=(pltpu.SemaphoreType.DMA((3 * n,)), pltpu.SemaphoreType.DMA((3 * n,)),
                   *[pltpu.HBM(a.shape, a.dtype) for a in list(halves) + lands], jax.ShapeDtypeStruct((8, LANES), F32)),
        in_specs=[HBM_SPEC] * (2 * n) + [ANY] * len(after),
        out_specs=(SEM_SPEC, SEM_SPEC, *[HBM_SPEC] * (2 * n), pl.BlockSpec(memory_space=pltpu.VMEM)),
        input_output_aliases={k: 2 + k for k in range(2 * n)},
        compiler_params=pltpu.CompilerParams(has_side_effects=IN_FLIGHT),
    )(*[pltpu.with_memory_space_constraint(a, pltpu.HBM) for a in list(halves) + lands], *after)
    return out[0], out[1], out[2:2 + n], out[2 + n:2 + 2 * n], out[-1]


def _chip_scatter_wait(send_sems, recv_sems, sources, lands, after, tag):
    n = len(sources)

    def body(*refs):
        for cp in _scatter_copies(refs[:n], refs[n:2 * n], refs[2 * n], refs[2 * n + 1]):
            cp.wait_send()
            cp.wait_recv()

    out = pl.pallas_call(
        body, name=f"grads_chip_scatter_wait_{tag}",
        out_shape=[pltpu.HBM(a.shape, a.dtype) for a in list(sources) + list(lands)],
        in_specs=[HBM_SPEC] * (2 * n) + [SEM_SPEC, SEM_SPEC] + [ANY] * len(after), out_specs=[HBM_SPEC] * (2 * n),
        input_output_aliases={k: k for k in range(2 * n)},
        compiler_params=pltpu.CompilerParams(has_side_effects=IN_FLIGHT),
    )(*sources, *lands, send_sems, recv_sems, *after)
    return out[n:]


def _pair_share(finals, tag, after):
    n = len(finals)

    def body(*refs):
        ins, outs = refs[:n], refs[n + 1:2 * n + 1]
        send_sems, recv_sems = refs[2 * n + 1:]
        x, y, c, _ = _place()
        copies = [pltpu.make_async_remote_copy(
            src_ref=ins[k], dst_ref=outs[k], send_sem=send_sems.at[k], recv_sem=recv_sems.at[k],
            device_id=(x, y, 1 - c), device_id_type=MESH) for k in range(n)]
        for cp in copies:
            cp.start()
        for cp in copies:
            cp.wait()

    return pl.pallas_call(
        body, name=f"grads_pair_share_{tag}",
        in_specs=[ANY] * (n + 1), out_specs=[ANY] * n,
        out_shape=[jax.ShapeDtypeStruct(fv.shape, F32) for fv in finals],
        scratch_shapes=[pltpu.SemaphoreType.DMA((n,)), pltpu.SemaphoreType.DMA((n,))],
    )(*finals, after)


def _small_all_reduce(buf):
    rows, c_ = buf.shape

    def body(in_ref, out_ref, pair_buf, slots, send_sems, recv_sems):
        x, y, c, chips = _place()
        mine = 2 * x + y
        pair = pltpu.make_async_remote_copy(
            src_ref=in_ref, dst_ref=pair_buf, send_sem=send_sems.at[0], recv_sem=recv_sems.at[0],
            device_id=(x, y, 1 - c), device_id_type=MESH)
        pair.start()
        pair.wait()
        slots[mine] = in_ref[...] + pair_buf[...]
        sends = [pltpu.make_async_remote_copy(
            src_ref=slots.at[mine], dst_ref=slots.at[mine], send_sem=send_sems.at[1 + j], recv_sem=recv_sems.at[1 + j],
            device_id=(tx, ty, c), device_id_type=MESH) for j, (tx, ty) in enumerate(chips)]
        for cp in sends:
            cp.start()
        for j, (tx, ty) in enumerate(chips):
            pltpu.make_async_remote_copy(
                src_ref=slots.at[mine], dst_ref=slots.at[2 * tx + ty], send_sem=send_sems.at[1 + j],
                recv_sem=recv_sems.at[1 + j], device_id=(tx, ty, c), device_id_type=MESH).wait()
        out_ref[...] = ((slots[0] + slots[1]) + slots[2]) + slots[3]

    vm = pl.BlockSpec(memory_space=pltpu.VMEM)
    return pl.pallas_call(
        body, name="small_all_reduce", in_specs=[vm], out_specs=vm,
        out_shape=jax.ShapeDtypeStruct((rows, c_), F32),
        scratch_shapes=[pltpu.VMEM((rows, c_), F32), pltpu.VMEM((4, rows, c_), F32),
                        pltpu.SemaphoreType.DMA((4,)), pltpu.SemaphoreType.DMA((4,))],
        compiler_params=pltpu.CompilerParams(vmem_limit_bytes=VMEM_LIMIT_V7X),
    )(buf)


def _pair_reduce(grads, core, tag):
    recv = _pair_exchange(grads, tag)
    return [_pair_sum(g, r, core, f"pair_sum_{tag}_{k}") for k, (g, r) in enumerate(zip(grads, recv))]


def _chip_reduce(halves, recv, chip, tag):
    return [_chip_sum(hv, r, chip, f"chip_sum_{tag}_{k}") for k, ((hv, _), r) in enumerate(zip(halves, recv))]


PACK_COLS = 1024


def _pack(arrays):
    flat = jnp.concatenate([a.reshape(-1).astype(F32) for a in arrays])
    rows = -(-flat.shape[0] // PACK_COLS)
    rows = -(-rows // 8) * 8
    return jnp.pad(flat, (0, rows * PACK_COLS - flat.shape[0])).reshape(rows, PACK_COLS)


def _unpack(buf, shapes):
    flat = buf.reshape(-1)
    out, at = [], 0
    for shp in shapes:
        size = math.prod(shp)
        out.append(flat[at:at + size].reshape(shp))
        at += size
    return out


def kernel(x, mix_norm_g, ffn_norm_g, gm_w_in, gm_ln_g, gm_ln_b, gm_w_s, gm_b_s, gm_w_out, fox_w_qkvf, fox_b_f, fox_w_o, ffn_w_gate, ffn_w_up, ffn_conv_w, ffn_conv_b, ffn_w_down, final_norm_g, loss_target, m_mix_norm_g, m_ffn_norm_g, m_gm_w_in, m_gm_ln_g, m_gm_ln_b, m_gm_w_s, m_gm_b_s, m_gm_w_out, m_fox_w_qkvf, m_fox_b_f, m_fox_w_o, m_ffn_w_gate, m_ffn_w_up, m_ffn_conv_w, m_ffn_conv_b, m_ffn_w_down, m_final_norm_g, v_mix_norm_g, v_ffn_norm_g, v_gm_w_in, v_gm_ln_g, v_gm_ln_b, v_gm_w_s, v_gm_b_s, v_gm_w_out, v_fox_w_qkvf, v_fox_b_f, v_fox_w_o, v_ffn_w_gate, v_ffn_w_up, v_ffn_conv_w, v_ffn_conv_b, v_ffn_w_down, v_final_norm_g):
    _, s_len, d = x.shape
    e = gm_ln_g.shape[1]
    f = ffn_conv_b.shape[1]
    n_head = fox_b_f.shape[1]
    n_pair = n_head // 2
    gd = e // GM_GROUPS
    qkvf_cols = fox_w_qkvf.shape[2]
    assert d == n_head * HEAD_DIM and d % (2 * LANES) == 0 and s_len % 512 == 0 and gd % LANES == 0
    assert gm_w_s.shape[2] == CHUNK and 4 * qkvf_cols == 3 * d + n_head
    tm = 256
    h0 = x[0]
    target = loss_target[0]

    w_in, w_out4 = _gather_pair(_gather_ici([gm_w_in[0].astype(BF16), gm_w_out[0].astype(BF16)]))
    send_sems, recv_sems, sources, lands, token = _gather_ici_start([
        fox_w_qkvf[0].astype(BF16), fox_w_o[0].astype(BF16), ffn_w_gate.astype(BF16), ffn_w_up.astype(BF16),
        ffn_w_down.astype(BF16), ffn_conv_w], after=w_in)
    w_out = w_out4.reshape(e, d)
    bf_pad = jnp.pad(fox_b_f, ((0, 0), (0, LANES - n_head)))

    tril = jnp.tril(jnp.ones((CHUNK, CHUNK), bool))
    wc = jnp.where(tril[None], gm_w_s[0], 0.0).astype(BF16)
    wct = jnp.transpose(wc, (0, 2, 1))
    bias = jnp.repeat(gm_b_s[0].T, gd, axis=1)
    seg_groups = (jnp.arange(e)[:, None] // gd == jnp.arange(LANES)[None, :]).astype(BF16)
    seg_heads = (jnp.arange(d)[:, None] // HEAD_DIM == jnp.arange(LANES)[None, :]).astype(BF16)
    sel_q = _spare_selectors(d, key_side=False)
    sel_k = _spare_selectors(d, key_side=True)

    h1, a0, hn0, gated0 = _gmlp_fwd(h0, mix_norm_g[0:1] + token[0:1, 0:1], w_in, gm_ln_g, gm_ln_b, wc, bias, w_out, tm)
    qkvf4, wo4, wg_all, wu_all, wd_all, cw4 = _gather_pair(_gather_ici_wait(send_sems, recv_sems, sources, lands, h1))
    qkvf = jnp.transpose(qkvf4, (1, 0, 2)).reshape(d, 4 * qkvf_cols)
    wq, wk, wv = qkvf[:, :d], qkvf[:, d:2 * d], qkvf[:, 2 * d:3 * d]
    wf = jnp.pad(qkvf[:, 3 * d:], ((0, 0), (0, LANES - n_head)))
    wo = wo4.reshape(d, d)
    conv_w_full = jnp.transpose(cw4, (1, 2, 0, 3)).reshape(2, 3, f)
    conv_w8 = jnp.pad(conv_w_full, ((0, 0), (0, 5), (0, 0)))
    h2, fa0, fup0, fhn0, fhid0 = _ffn_fwd(h1, ffn_norm_g[0:1], wg_all, wu_all, wd_all, 0, conv_w8[0], ffn_conv_b[0:1], tm)
    (hn1, qa, qb, kat, kbt, va, vb, vat, vbt, z_f) = _fox_proj_fwd(
        h2, mix_norm_g[1:2], wq, wk, wv, wf, bf_pad, sel_q, sel_k, tm)
    o, qa2, qb2, qat2, qbt2 = _flash_fwd(qa, qb, kat, kbt, va, vb)
    h3 = _oproj_fwd(h2, o, wo, tm)
    h4, fa1, fup1, fhn1, fhid1 = _ffn_fwd(h3, ffn_norm_g[1:2], wg_all, wu_all, wd_all, 1, conv_w8[1], ffn_conv_b[1:2], tm)

    dh4, loss_part, g_final = _loss_head(h4, target, final_norm_g.reshape(1, d), tm)
    dh3, da1, dup1, gcw1, gcb1, gfn1 = _ffn_bwd(h3, dh4, fa1, fup1, ffn_norm_g[1:2], wg_all, wu_all, wd_all, 1,
                                                conv_w8[1], ffn_conv_b[1:2], tm)
    g_gate1 = _wgrad(fhn1, da1, 4, "wgrad_gate_1")
    g_up1 = _wgrad(fhn1, dup1, 4, "wgrad_up_1")
    g_down1 = _wgrad(fhid1, dh4, 1, "wgrad_down_1").reshape(4, f // 4, d)

    doa, dob, doat, dobt = _oproj_bwd(dh3, o, wo, seg_heads, sel_q, tm)
    g_wo = _wgrad(o, dh3, 1, "wgrad_wo").reshape(4, d // 4, d)
    dqt, dkt, dvt, row_sums, col_sums = _flash_bwd(qa2, qb2, qat2, qbt2, kat, kbt, vat, vbt, doa, dob, doat, dobt)
    sums = row_sums[:, 0::8, :] - col_sums[:, N_SPARE::8, :]
    dcum = jnp.pad(sums.reshape(n_head, s_len).T, ((0, 0), (0, LANES - n_head)))
    dfl, g_bf = _forget_bwd(dcum, z_f, tm)
    dh2, dflb, gmn1 = _fox_proj_bwd(h2, dh3, dqt, dkt, dvt, dfl, mix_norm_g[1:2], wq, wk, wv, wf, tm)
    g_q = _wgrad_t(dqt, hn1, "wgrad_q").T
    g_k = _wgrad_t(dkt, hn1, "wgrad_k").T
    g_v = _wgrad_t(dvt, hn1, "wgrad_v").T
    g_f = _wgrad(hn1, dflb, 1, "wgrad_f")[0][:, :n_head]
    g_qkvf = jnp.concatenate([g_q, g_k, g_v, g_f], axis=1).reshape(d, 4, qkvf_cols).transpose(1, 0, 2)

    core = lax.axis_index("c").astype(jnp.int32).reshape(1)
    chip = (2 * lax.axis_index("x") + lax.axis_index("y")).astype(jnp.int32).reshape(1)
    halves_early = _pair_reduce([g_qkvf, g_wo, g_gate1, g_up1, g_down1], core, "early")
    sc_send, sc_recv, sc_src, sc_land, sc_token = _chip_scatter_start([hb for _, hb in halves_early], "early")

    dh1, da0f, dup0, gcw0, gcb0, gfn0 = _ffn_bwd(h1, dh2, fa0, fup0, ffn_norm_g[0:1] + sc_token[0:1, 0:1],
                                                 wg_all, wu_all, wd_all, 0, conv_w8[0], ffn_conv_b[0:1], tm)
    g_gate0 = _wgrad(fhn0, da0f, 4, "wgrad_gate_0")
    g_up0 = _wgrad(fhn0, dup0, 4, "wgrad_up_0")
    g_down0 = _wgrad(fhid0, dh2, 1, "wgrad_down_0").reshape(4, f // 4, d)

    dh0, da0, g_ws, g_bs_t, g_lng, g_lnb, gmn0 = _gmlp_bwd(
        h0, dh1, a0, mix_norm_g[0:1], w_in, gm_ln_g, gm_ln_b, wc, wct, bias, w_out, seg_groups, tm)
    g_win = _wgrad(hn0, da0, 4, "wgrad_gm_in")
    g_wout = _wgrad(gated0, dh1, 1, "wgrad_gm_out").reshape(4, e // 4, d)

    small = [jnp.concatenate([gmn0, gmn1]), jnp.concatenate([gfn0, gfn1]), g_lng, g_lnb, g_ws[None],
             g_bs_t[:, :GM_GROUPS].T[None], g_bf[:, :n_head], jnp.stack([gcw0[:3], gcw1[:3]]),
             jnp.concatenate([gcb0, gcb1]), g_final.reshape(d), loss_part[0, :1]]
    small_shapes = [a.shape for a in small]
    small_sum = _small_all_reduce(_pack(small))
    reduced = _unpack(small_sum, small_shapes)
    (r_mix, r_ffn, r_lng, r_lnb, r_ws, r_bs, r_bf, r_cw_full, r_cb, r_final, r_loss) = reduced
    r_cw = lax.dynamic_slice_in_dim(r_cw_full, chip[0] * (f // 4), f // 4, axis=2)

    halves_late = _pair_reduce([g_win, g_wout, g_gate0, g_up0, g_down0], core, "late")
    lt_send, lt_recv, lt_src, lt_land, lt_token = _chip_scatter_start([hb for _, hb in halves_late], "late", [small_sum])
    recv_early = _chip_scatter_wait(sc_send, sc_recv, sc_src, sc_land, [halves_late[0][0]], "early")
    finals_early = _chip_reduce(halves_early, recv_early, chip, "early")
    r_qkvf, r_wo, r_gate1, r_up1, r_down1 = zip(finals_early, _pair_share(finals_early, "early", lt_token))

    def update(name, layer, w, m, v, grad, core_):
        return _adamw_halves(w[layer], m[layer], v[layer], grad[0], grad[1], core_, f"adamw_{name}_{layer}")

    u_qkvf = update("fox_w_qkvf", 0, fox_w_qkvf, m_fox_w_qkvf, v_fox_w_qkvf, r_qkvf, core)
    u_wo = update("fox_w_o", 0, fox_w_o, m_fox_w_o, v_fox_w_o, r_wo, core)
    u_gate1 = update("ffn_w_gate", 1, ffn_w_gate, m_ffn_w_gate, v_ffn_w_gate, r_gate1, core)
    u_up1 = update("ffn_w_up", 1, ffn_w_up, m_ffn_w_up, v_ffn_w_up, r_up1, core)
    u_down1 = update("ffn_w_down", 1, ffn_w_down, m_ffn_w_down, v_ffn_w_down, r_down1, core)

    res = {}
    small_names = ["mix_norm_g", "ffn_norm_g", "gm_ln_g", "gm_ln_b", "gm_w_s", "gm_b_s", "fox_b_f", "ffn_conv_w",
                   "ffn_conv_b", "final_norm_g"]
    small_w = [mix_norm_g, ffn_norm_g, gm_ln_g, gm_ln_b, gm_w_s, gm_b_s, fox_b_f, ffn_conv_w, ffn_conv_b, final_norm_g]
    small_m = [m_mix_norm_g, m_ffn_norm_g, m_gm_ln_g, m_gm_ln_b, m_gm_w_s, m_gm_b_s, m_fox_b_f, m_ffn_conv_w,
               m_ffn_conv_b, m_final_norm_g]
    small_v = [v_mix_norm_g, v_ffn_norm_g, v_gm_ln_g, v_gm_ln_b, v_gm_w_s, v_gm_b_s, v_fox_b_f, v_ffn_conv_w,
               v_ffn_conv_b, v_final_norm_g]
    small_g = [r_mix, r_ffn, r_lng, r_lnb, r_ws, r_bs, r_bf, r_cw, r_cb, r_final]
    shapes = [w.shape for w in small_w]
    small_g = [g.reshape(s) for g, s in zip(small_g, shapes)]
    dlt, mn, vn = _adamw(_pack(small_w), _pack(small_m), _pack(small_v), _pack(small_g), "adamw_small")
    for name, g, dl_, m_, v_ in zip(small_names, small_g, _unpack(dlt, shapes), _unpack(mn, shapes), _unpack(vn, shapes)):
        res[name] = (g, dl_, m_, v_)

    done_meanwhile = [u[1] for u in (u_qkvf, u_wo, u_gate1, u_up1, u_down1)] + [dlt]
    recv_late = _chip_scatter_wait(lt_send, lt_recv, lt_src, lt_land, done_meanwhile, "late")
    finals_late = _chip_reduce(halves_late, recv_late, chip, "late")
    r_win, r_wout, r_gate0, r_up0, r_down0 = zip(finals_late, _pair_share(finals_late, "late", lt_token))
    u_gate0 = update("ffn_w_gate", 0, ffn_w_gate, m_ffn_w_gate, v_ffn_w_gate, r_gate0, core)
    u_up0 = update("ffn_w_up", 0, ffn_w_up, m_ffn_w_up, v_ffn_w_up, r_up0, core)
    u_down0 = update("ffn_w_down", 0, ffn_w_down, m_ffn_w_down, v_ffn_w_down, r_down0, core)
    layers = {"gm_w_in": [update("gm_w_in", 0, gm_w_in, m_gm_w_in, v_gm_w_in, r_win, core)],
              "gm_w_out": [update("gm_w_out", 0, gm_w_out, m_gm_w_out, v_gm_w_out, r_wout, core)],
              "fox_w_qkvf": [u_qkvf], "fox_w_o": [u_wo], "ffn_w_gate": [u_gate0, u_gate1],
              "ffn_w_up": [u_up0, u_up1], "ffn_w_down": [u_down0, u_down1]}
    for name, parts in layers.items():
        res[name] = tuple(jnp.stack([p[i] for p in parts]) for i in range(4))

    order = ["mix_norm_g", "ffn_norm_g", "gm_w_in", "gm_ln_g", "gm_ln_b", "gm_w_s", "gm_b_s", "gm_w_out", "fox_w_qkvf",
             "fox_b_f", "fox_w_o", "ffn_w_gate", "ffn_w_up", "ffn_conv_w", "ffn_conv_b", "ffn_w_down", "final_norm_g"]
    outs = [r_loss.reshape(()), dh0[None]]
    for part in range(4):
        outs += [res[name][part] for name in order]
    return tuple(outs)
```

```python
import math

import jax
import jax.numpy as jnp
from jax import lax
from jax.experimental import pallas as pl
from jax.experimental.pallas import tpu as pltpu

F32 = jnp.float32
BF16 = jnp.bfloat16

RMS_EPS = 1e-6
LN_EPS = 1e-5
CHUNK = 128
GM_GROUPS = 8
HEAD_DIM = 64
LANES = 128
ATT_BLOCK = 256
ATT_CHUNK = 1024
VMEM_LIMIT_V7X = 56 * 1024 * 1024

ADAM_LR = 0.001
ADAM_B1 = 0.9
ADAM_B2 = 0.999
ADAM_EPS = 1e-08
ADAM_WD = 0.01
ADAM_STEP = 10

MESH = pl.DeviceIdType.MESH
ANY = pl.BlockSpec(memory_space=pl.ANY)
NEG_BIG = -1e30


def _params(n_grid):
    return pltpu.CompilerParams(dimension_semantics=("arbitrary",) * n_grid, vmem_limit_bytes=VMEM_LIMIT_V7X)


def _dot(a, b):
    return jnp.dot(a, b, preferred_element_type=F32)


def _dot_nt(a, b):
    return lax.dot_general(a, b, (((1,), (1,)), ((), ())), preferred_element_type=F32)


def _dot_tn(a, b):
    return lax.dot_general(a, b, (((0,), (0,)), ((), ())), preferred_element_type=F32)


def _split3(x):
    hi = x.astype(BF16)
    r = x - hi.astype(F32)
    mid = r.astype(BF16)
    lo = (r - mid.astype(F32)).astype(BF16)
    return hi, mid, lo


def _dot3_lhs(x, m):
    hi, mid, lo = _split3(x)
    return _dot(hi, m) + _dot(mid, m) + _dot(lo, m)


def _dot3_rhs(m, x):
    hi, mid, lo = _split3(x)
    return _dot(m, hi) + _dot(m, mid) + _dot(m, lo)


def _load_once(pairs, sem):
    @pl.when(pl.program_id(0) == 0)
    def _():
        copies = [pltpu.make_async_copy(src, dst, sem.at[k]) for k, (src, dst) in enumerate(pairs)]
        for cp in copies:
            cp.start()
        for cp in copies:
            cp.wait()


def _rms_fwd(x, g):
    r = lax.rsqrt(jnp.mean(x * x, axis=-1, keepdims=True) + RMS_EPS)
    xhat = x * r
    return xhat, r, xhat * g


def _rms_bwd(dy, xhat, r, g):
    w = dy * g
    dx = r * (w - xhat * jnp.mean(w * xhat, axis=-1, keepdims=True))
    return dx, dy * xhat


def _gelu_parts(a):
    c = math.sqrt(2.0 / math.pi)
    a2 = a * a
    t = jnp.tanh(c * (a + 0.044715 * a * a2))
    z = 0.5 * a * (1.0 + t)
    dz = 0.5 * (1.0 + t) + 0.5 * a * (1.0 - t * t) * (c * (1.0 + 3.0 * 0.044715 * a2))
    return z, dz


def _sigmoid(x):
    return 1.0 / (1.0 + jnp.exp(-x))


def _gmlp_core(a, lng, lnb, wc_ref, bias, n_chunk, gd):
    e = a.shape[1] // 2
    z, dz = _gelu_parts(a)
    u = z[:, :e]
    v = z[:, e:]
    mu = jnp.mean(v, axis=-1, keepdims=True)
    vc = v - mu
    rstd = lax.rsqrt(jnp.mean(vc * vc, axis=-1, keepdims=True) + LN_EPS)
    vhat = vc * rstd
    vln = vhat * lng + lnb
    vlb = vln.astype(BF16)
    rows = []
    for ci in range(n_chunk):
        cols = []
        for g in range(GM_GROUPS):
            blk = vlb[ci * CHUNK:(ci + 1) * CHUNK, g * gd:(g + 1) * gd]
            cols.append(_dot(wc_ref[g], blk))
        rows.append(jnp.concatenate(cols, axis=1) + bias)
    s = rows[0] if n_chunk == 1 else jnp.concatenate(rows, axis=0)
    return dz, u, vhat, rstd, vlb, s


def _gmlp_fwd(h, g_mix, w_in, lng, lnb, wc, bias, w_out, tm):
    s_len, d = h.shape
    n_p, _, w = w_in.shape
    e = w_out.shape[0]
    gd = e // GM_GROUPS
    n_chunk = tm // CHUNK

    def body(h_ref, g_ref, win_hbm, lng_ref, lnb_ref, wc_ref, bias_ref, wout_hbm,
             hout_ref, a_ref, hn_ref, gated_ref, win_v, wout_v, sem):
        _load_once([(win_hbm, win_v), (wout_hbm, wout_v)], sem)
        x = h_ref[...]
        _, _, y = _rms_fwd(x, g_ref[...])
        hn = y.astype(BF16)
        hn_ref[...] = hn
        for p in range(n_p):
            a_ref[:, p * w:(p + 1) * w] = _dot(hn, win_v[p])
        _, u, _, _, _, s = _gmlp_core(a_ref[...], lng_ref[...], lnb_ref[...], wc_ref, bias_ref[...], n_chunk, gd)
        gated = (u * s).astype(BF16)
        gated_ref[...] = gated
        hout_ref[...] = x + _dot(gated, wout_v[...])

    row = lambda i: (i, 0)
    const2 = lambda i: (0, 0)
    return pl.pallas_call(
        body, name="gmlp_fwd", grid=(s_len // tm,),
        in_specs=[pl.BlockSpec((tm, d), row), pl.BlockSpec((1, d), const2), ANY,
                  pl.BlockSpec((1, e), const2), pl.BlockSpec((1, e), const2),
                  pl.BlockSpec(wc.shape, lambda i: (0, 0, 0)), pl.BlockSpec((CHUNK, e), const2), ANY],
        out_specs=[pl.BlockSpec((tm, d), row), pl.BlockSpec((tm, 2 * e), row),
                   pl.BlockSpec((tm, d), row), pl.BlockSpec((tm, e), row)],
        out_shape=[jax.ShapeDtypeStruct((s_len, d), F32), jax.ShapeDtypeStruct((s_len, 2 * e), F32),
                   jax.ShapeDtypeStruct((s_len, d), BF16), jax.ShapeDtypeStruct((s_len, e), BF16)],
        scratch_shapes=[pltpu.VMEM(w_in.shape, BF16), pltpu.VMEM(w_out.shape, BF16), pltpu.SemaphoreType.DMA((2,))],
        compiler_params=_params(1),
    )(h, g_mix, w_in, lng, lnb, wc, bias, w_out)


def _gmlp_bwd(h, dh, a, g_mix, w_in, lng, lnb, wc, wct, bias, w_out, seg, tm):
    s_len, d = h.shape
    n_p, _, w = w_in.shape
    e = w_out.shape[0]
    gd = e // GM_GROUPS
    n_chunk = tm // CHUNK
    n_blk = s_len // tm

    def body(h_ref, dh_ref, a_ref, g_ref, win_hbm, lng_ref, lnb_ref, wc_ref, wct_ref, bias_ref, wout_hbm, seg_ref,
             dhin_ref, da_ref, gws_ref, gbs_ref, glng_ref, glnb_ref, gmix_ref, win_v, wout_v, dsum, sem):
        i = pl.program_id(0)
        _load_once([(win_hbm, win_v), (wout_hbm, wout_v)], sem)

        @pl.when(i == 0)
        def _():
            gws_ref[...] = jnp.zeros_like(gws_ref)
            glng_ref[...] = jnp.zeros_like(glng_ref)
            glnb_ref[...] = jnp.zeros_like(glnb_ref)
            gmix_ref[...] = jnp.zeros_like(gmix_ref)
            dsum[...] = jnp.zeros_like(dsum)

        x = h_ref[...]
        dh_v = dh_ref[...]
        g = g_ref[...]
        lng_v = lng_ref[...]
        xhat, r, _ = _rms_fwd(x, g)
        dz_da, u, vhat, rstd, vlb, s = _gmlp_core(a_ref[...], lng_v, lnb_ref[...], wc_ref, bias_ref[...], n_chunk, gd)
        dg = _dot_nt(dh_v.astype(BF16), wout_v[...])
        du = dg * s
        ds = dg * u
        dsb = ds.astype(BF16)
        rows = []
        ds_acc = None
        for ci in range(n_chunk):
            lo, hi = ci * CHUNK, (ci + 1) * CHUNK
            cols = []
            for gi in range(GM_GROUPS):
                d_blk = dsb[lo:hi, gi * gd:(gi + 1) * gd]
                gws_ref[gi] += _dot_nt(d_blk, vlb[lo:hi, gi * gd:(gi + 1) * gd])
                cols.append(_dot(wct_ref[gi], d_blk))
            rows.append(jnp.concatenate(cols, axis=1))
            ds_acc = ds[lo:hi] if ds_acc is None else ds_acc + ds[lo:hi]
        dsum[...] += ds_acc
        dvln = rows[0] if n_chunk == 1 else jnp.concatenate(rows, axis=0)
        glng_ref[...] += jnp.sum(dvln * vhat, axis=0, keepdims=True)
        glnb_ref[...] += jnp.sum(dvln, axis=0, keepdims=True)
        dvhat = dvln * lng_v
        dv = rstd * (dvhat - jnp.mean(dvhat, axis=-1, keepdims=True)
                     - vhat * jnp.mean(dvhat * vhat, axis=-1, keepdims=True))
        da = jnp.concatenate([du, dv], axis=1) * dz_da
        dab = da.astype(BF16)
        da_ref[...] = dab
        dhn = _dot_nt(dab[:, :w], win_v[0])
        for p in range(1, n_p):
            dhn += _dot_nt(dab[:, p * w:(p + 1) * w], win_v[p])
        dx, gg = _rms_bwd(dhn, xhat, r, g)
        gmix_ref[...] += jnp.sum(gg, axis=0, keepdims=True)
        dhin_ref[...] = dh_v + dx

        @pl.when(i == n_blk - 1)
        def _():
            tril = lax.broadcasted_iota(jnp.int32, (CHUNK, CHUNK), 0) >= lax.broadcasted_iota(jnp.int32, (CHUNK, CHUNK), 1)
            for gi in range(GM_GROUPS):
                gws_ref[gi] = jnp.where(tril, gws_ref[gi], 0.0)
            gbs_ref[...] = _dot3_lhs(dsum[...], seg_ref[...])

    row = lambda i: (i, 0)
    const2 = lambda i: (0, 0)
    const3 = lambda i: (0, 0, 0)
    return pl.pallas_call(
        body, name="gmlp_bwd", grid=(n_blk,),
        in_specs=[pl.BlockSpec((tm, d), row), pl.BlockSpec((tm, d), row), pl.BlockSpec((tm, 2 * e), row),
                  pl.BlockSpec((1, d), const2), ANY, pl.BlockSpec((1, e), const2), pl.BlockSpec((1, e), const2),
                  pl.BlockSpec(wc.shape, const3), pl.BlockSpec(wct.shape, const3), pl.BlockSpec((CHUNK, e), const2),
                  ANY, pl.BlockSpec((e, LANES), const2)],
        out_specs=[pl.BlockSpec((tm, d), row), pl.BlockSpec((tm, 2 * e), row), pl.BlockSpec(wc.shape, const3),
                   pl.BlockSpec((CHUNK, LANES), const2), pl.BlockSpec((1, e), const2), pl.BlockSpec((1, e), const2),
                   pl.BlockSpec((1, d), const2)],
        out_shape=[jax.ShapeDtypeStruct((s_len, d), F32), jax.ShapeDtypeStruct((s_len, 2 * e), BF16),
                   jax.ShapeDtypeStruct(wc.shape, F32), jax.ShapeDtypeStruct((CHUNK, LANES), F32),
                   jax.ShapeDtypeStruct((1, e), F32), jax.ShapeDtypeStruct((1, e), F32), jax.ShapeDtypeStruct((1, d), F32)],
        scratch_shapes=[pltpu.VMEM(w_in.shape, BF16), pltpu.VMEM(w_out.shape, BF16), pltpu.VMEM((CHUNK, e), F32),
                        pltpu.SemaphoreType.DMA((2,))],
        compiler_params=_params(1),
    )(h, dh, a, g_mix, w_in, lng, lnb, wc, wct, bias, w_out, seg)


def _shift_down(a, k, fill):
    tm = a.shape[0]
    out = pltpu.roll(a, k, 0)
    rid = lax.broadcasted_iota(jnp.int32, a.shape, 0)
    for j in range(k):
        out = jnp.where(rid == j, fill[8 - k + j:8 - k + j + 1, :], out)
    return out


def _shift_up(a, k, fill):
    tm = a.shape[0]
    out = pltpu.roll(a, tm - k, 0)
    rid = lax.broadcasted_iota(jnp.int32, a.shape, 0)
    for j in range(k):
        out = jnp.where(rid == tm - k + j, fill[j:j + 1, :], out)
    return out


def _ffn_fwd(h, g_norm, wg_all, wu_all, wd_all, layer, conv_w, conv_b, tm):
    s_len, d = h.shape
    n_p = wg_all.shape[0]
    fq = wg_all.shape[3]
    f = n_p * fq

    def body(h_ref, g_ref, wg_hbm, wu_hbm, wd_hbm, cw_ref, cb_ref,
             hout_ref, a_ref, up_ref, hn_ref, hid_ref, wg_v, wu_v, wd_v, carry, sem):
        i = pl.program_id(0)
        _load_once([(wg_hbm.at[:, layer], wg_v), (wu_hbm.at[:, layer], wu_v), (wd_hbm.at[:, layer], wd_v)], sem)

        @pl.when(i == 0)
        def _():
            carry[...] = jnp.zeros_like(carry)

        x = h_ref[...]
        _, _, y = _rms_fwd(x, g_ref[...])
        hn = y.astype(BF16)
        hn_ref[...] = hn
        for p in range(n_p):
            a_ref[:, p * fq:(p + 1) * fq] = _dot(hn, wg_v[p])
            up_ref[:, p * fq:(p + 1) * fq] = _dot(hn, wu_v[p])
        a = a_ref[...]
        prev = carry[...]
        am1 = _shift_down(a, 1, prev)
        am2 = _shift_down(a, 2, prev)
        carry[...] = a[tm - 8:tm, :]
        cw = cw_ref[...]
        ac = cb_ref[...] + am2 * cw[0:1, :]
        ac = ac + am1 * cw[1:2, :]
        ac = ac + a * cw[2:3, :]
        hid = (ac * _sigmoid(ac) * up_ref[...]).astype(BF16)
        hid_ref[...] = hid
        y2 = _dot(hid[:, :fq], wd_v[0])
        for p in range(1, n_p):
            y2 += _dot(hid[:, p * fq:(p + 1) * fq], wd_v[p])
        hout_ref[...] = x + y2

    row = lambda i: (i, 0)
    const2 = lambda i: (0, 0)
    return pl.pallas_call(
        body, name=f"ffn_fwd_{layer}", grid=(s_len // tm,),
        in_specs=[pl.BlockSpec((tm, d), row), pl.BlockSpec((1, d), const2), ANY, ANY, ANY,
                  pl.BlockSpec((8, f), const2), pl.BlockSpec((1, f), const2)],
        out_specs=[pl.BlockSpec((tm, d), row), pl.BlockSpec((tm, f), row), pl.BlockSpec((tm, f), row),
                   pl.BlockSpec((tm, d), row), pl.BlockSpec((tm, f), row)],
        out_shape=[jax.ShapeDtypeStruct((s_len, d), F32), jax.ShapeDtypeStruct((s_len, f), F32),
                   jax.ShapeDtypeStruct((s_len, f), F32), jax.ShapeDtypeStruct((s_len, d), BF16),
                   jax.ShapeDtypeStruct((s_len, f), BF16)],
        scratch_shapes=[pltpu.VMEM((n_p, d, fq), BF16), pltpu.VMEM((n_p, d, fq), BF16), pltpu.VMEM((n_p, fq, d), BF16),
                        pltpu.VMEM((8, f), F32), pltpu.SemaphoreType.DMA((3,))],
        compiler_params=_params(1),
    )(h, g_norm, wg_all, wu_all, wd_all, conv_w, conv_b)


def _ffn_bwd(h, dh, a, up, g_norm, wg_all, wu_all, wd_all, layer, conv_w, conv_b, tm):
    s_len, d = h.shape
    n_p = wg_all.shape[0]
    fq = wg_all.shape[3]
    f = n_p * fq
    n_blk = s_len // tm
    t8 = tm // 8

    def body(h_ref, dh_ref, a_ref, ahalo_ref, up_ref, g_ref, wg_hbm, wu_hbm, wd_hbm, cw_ref, cb_ref,
             dhin_ref, da_ref, dup_ref, gcw_ref, gcb_ref, gn_ref, wg_v, wu_v, wd_v, carry, sem):
        i = pl.program_id(0)
        _load_once([(wg_hbm.at[:, layer], wg_v), (wu_hbm.at[:, layer], wu_v), (wd_hbm.at[:, layer], wd_v)], sem)

        @pl.when(i == 0)
        def _():
            carry[...] = jnp.zeros_like(carry)
            gcw_ref[...] = jnp.zeros_like(gcw_ref)
            gcb_ref[...] = jnp.zeros_like(gcb_ref)
            gn_ref[...] = jnp.zeros_like(gn_ref)

        x = h_ref[...]
        dh_v = dh_ref[...]
        g = g_ref[...]
        xhat, r, _ = _rms_fwd(x, g)
        a = a_ref[...]
        up_v = up_ref[...]
        prev = jnp.where(i == n_blk - 1, 0.0, ahalo_ref[...])
        am1 = _shift_down(a, 1, prev)
        am2 = _shift_down(a, 2, prev)
        cw = cw_ref[...]
        ac = cb_ref[...] + am2 * cw[0:1, :]
        ac = ac + am1 * cw[1:2, :]
        ac = ac + a * cw[2:3, :]
        sg = _sigmoid(ac)
        sil = ac * sg
        dhb = dh_v.astype(BF16)
        dhid = jnp.concatenate([_dot_nt(dhb, wd_v[p]) for p in range(n_p)], axis=1)
        dup = dhid * sil
        dac = dhid * up_v * (sg * (1.0 + ac * (1.0 - sg)))
        gcb_ref[...] += jnp.sum(dac, axis=0, keepdims=True)
        gcw_ref[0:1, :] += jnp.sum(dac * am2, axis=0, keepdims=True)
        gcw_ref[1:2, :] += jnp.sum(dac * am1, axis=0, keepdims=True)
        gcw_ref[2:3, :] += jnp.sum(dac * a, axis=0, keepdims=True)
        nxt = carry[...]
        dp1 = _shift_up(dac, 1, nxt)
        dp2 = _shift_up(dac, 2, nxt)
        carry[...] = dac[0:8, :]
        da = dac * cw[2:3, :] + dp1 * cw[1:2, :] + dp2 * cw[0:1, :]
        dab = da.astype(BF16)
        dupb = dup.astype(BF16)
        da_ref[...] = dab
        dup_ref[...] = dupb
        dhn = _dot_nt(dab[:, :fq], wg_v[0]) + _dot_nt(dupb[:, :fq], wu_v[0])
        for p in range(1, n_p):
            dhn += _dot_nt(dab[:, p * fq:(p + 1) * fq], wg_v[p]) + _dot_nt(dupb[:, p * fq:(p + 1) * fq], wu_v[p])
        dx, gg = _rms_bwd(dhn, xhat, r, g)
        gn_ref[...] += jnp.sum(gg, axis=0, keepdims=True)
        dhin_ref[...] = dh_v + dx

    rev = lambda i: (n_blk - 1 - i, 0)
    halo = lambda i: (jnp.maximum((n_blk - 1 - i) * t8 - 1, 0), 0)
    const2 = lambda i: (0, 0)
    return pl.pallas_call(
        body, name=f"ffn_bwd_{layer}", grid=(n_blk,),
        in_specs=[pl.BlockSpec((tm, d), rev), pl.BlockSpec((tm, d), rev), pl.BlockSpec((tm, f), rev),
                  pl.BlockSpec((8, f), halo), pl.BlockSpec((tm, f), rev), pl.BlockSpec((1, d), const2), ANY, ANY, ANY,
                  pl.BlockSpec((8, f), const2), pl.BlockSpec((1, f), const2)],
        out_specs=[pl.BlockSpec((tm, d), rev), pl.BlockSpec((tm, f), rev), pl.BlockSpec((tm, f), rev),
                   pl.BlockSpec((8, f), const2), pl.BlockSpec((1, f), const2), pl.BlockSpec((1, d), const2)],
        out_shape=[jax.ShapeDtypeStruct((s_len, d), F32), jax.ShapeDtypeStruct((s_len, f), BF16),
                   jax.ShapeDtypeStruct((s_len, f), BF16), jax.ShapeDtypeStruct((8, f), F32),
                   jax.ShapeDtypeStruct((1, f), F32), jax.ShapeDtypeStruct((1, d), F32)],
        scratch_shapes=[pltpu.VMEM((n_p, d, fq), BF16), pltpu.VMEM((n_p, d, fq), BF16), pltpu.VMEM((n_p, fq, d), BF16),
                        pltpu.VMEM((8, f), F32), pltpu.SemaphoreType.DMA((3,))],
        compiler_params=_params(1),
    )(h, dh, a, a, up, g_norm, wg_all, wu_all, wd_all, conv_w, conv_b)


def _even_head_lanes(shape, axis):
    return (lax.broadcasted_iota(jnp.int32, shape, axis) & HEAD_DIM) == 0


def _pair_select(lo, hi, shape):
    return jnp.where(lax.broadcasted_iota(jnp.int32, shape, 1) < HEAD_DIM, lo, hi)


def _causal(row0, col0, shape):
    return row0 + lax.broadcasted_iota(jnp.int32, shape, 0) >= col0 + lax.broadcasted_iota(jnp.int32, shape, 1)


N_SPARE = 3


def _spare_selectors(d, key_side):
    lane = jnp.arange(d)[None, :]
    row = jnp.arange(N_SPARE * LANES)[:, None]
    head, part = row % LANES, row // LANES
    off = N_SPARE if key_side else 0
    sel_a = ((head % 2 == 0) & (lane == LANES * (head // 2) + HEAD_DIM + off + part)).astype(F32)
    sel_b = ((head % 2 == 1) & (lane == LANES * (head // 2) + off + part)).astype(F32)
    sign = -1.0 if key_side else 1.0
    ones_off = 0 if key_side else N_SPARE
    in_pair = jnp.arange(d)[None, :] % LANES
    ones_a = ((in_pair >= HEAD_DIM + ones_off) & (in_pair < HEAD_DIM + ones_off + N_SPARE)).astype(F32)
    ones_b = ((in_pair >= ones_off) & (in_pair < ones_off + N_SPARE)).astype(F32)
    return (sign * sel_a).astype(BF16), (sign * sel_b).astype(BF16), ones_a, ones_b


def _parts(x):
    return jnp.concatenate(_split3(x), axis=1)


def _fox_proj_fwd(h, g_norm, wq, wk, wv, wf, bf, sel_q, sel_k, tm):
    s_len, d = h.shape
    sq_a, sq_b, oq_a, oq_b = sel_q
    sk_a, sk_b, ok_a, ok_b = sel_k

    def body(h_ref, g_ref, wq_hbm, wk_hbm, wv_hbm, wf_ref, bf_ref, sqa_ref, sqb_ref, oqa_ref, oqb_ref,
             ska_ref, skb_ref, oka_ref, okb_ref,
             hn_ref, qa_ref, qb_ref, kat_ref, kbt_ref, va_ref, vb_ref, vat_ref, vbt_ref, z_ref,
             wq_v, wk_v, wv_v, total, sem):
        i = pl.program_id(0)
        _load_once([(wq_hbm, wq_v), (wk_hbm, wk_v), (wv_hbm, wv_v)], sem)

        @pl.when(i == 0)
        def _():
            total[...] = jnp.zeros_like(total)

        x = h_ref[...]
        _, _, y = _rms_fwd(x, g_ref[...])
        hn = y.astype(BF16)
        hn_ref[...] = hn
        z = _dot(hn, wf_ref[...]) + bf_ref[...]
        z_ref[...] = z
        logf = jnp.minimum(z, 0.0) - jnp.log(1.0 + jnp.exp(-jnp.abs(z)))
        tri = (lax.broadcasted_iota(jnp.int32, (tm, tm), 0) >= lax.broadcasted_iota(jnp.int32, (tm, tm), 1))
        cum = _dot3_rhs(jnp.where(tri, 1.0, 0.0).astype(BF16), logf) + total[0:1, :]
        total[...] = jnp.broadcast_to(cum[tm - 1:tm, :], total.shape)
        parts = _parts(cum)

        even = _even_head_lanes((tm, d), 1)
        q = _dot(hn, wq_v[...]) * (HEAD_DIM ** -0.5)
        qa_ref[...] = jnp.where(even, q, _dot(parts, sqa_ref[...]) + oqa_ref[...]).astype(BF16)
        qb_ref[...] = jnp.where(even, _dot(parts, sqb_ref[...]) + oqb_ref[...], q).astype(BF16)
        k = _dot(hn, wk_v[...])
        ka = jnp.where(even, k, _dot(parts, ska_ref[...]) + oka_ref[...])
        kb = jnp.where(even, _dot(parts, skb_ref[...]) + okb_ref[...], k)
        kat_ref[...] = ka.T.astype(BF16)
        kbt_ref[...] = kb.T.astype(BF16)
        v = _dot(hn, wv_v[...])
        va = jnp.where(even, v, oka_ref[...])
        vb = jnp.where(even, okb_ref[...], v)
        va_ref[...] = va.astype(BF16)
        vb_ref[...] = vb.astype(BF16)
        vat_ref[...] = va.T.astype(BF16)
        vbt_ref[...] = vb.T.astype(BF16)

    row = lambda i: (i, 0)
    col = lambda i: (0, i)
    const2 = lambda i: (0, 0)
    sd = jax.ShapeDtypeStruct((s_len, d), BF16)
    ds_ = jax.ShapeDtypeStruct((d, s_len), BF16)
    rs, cs = pl.BlockSpec((tm, d), row), pl.BlockSpec((d, tm), col)
    sel = pl.BlockSpec((N_SPARE * LANES, d), const2)
    one = pl.BlockSpec((1, d), const2)
    return pl.pallas_call(
        body, name="fox_proj_fwd", grid=(s_len // tm,),
        in_specs=[rs, one, ANY, ANY, ANY, pl.BlockSpec((d, LANES), const2), pl.BlockSpec((1, LANES), const2),
                  sel, sel, one, one, sel, sel, one, one],
        out_specs=[rs, rs, rs, cs, cs, rs, rs, cs, cs, pl.BlockSpec((tm, LANES), row)],
        out_shape=[sd, sd, sd, ds_, ds_, sd, sd, ds_, ds_, jax.ShapeDtypeStruct((s_len, LANES), F32)],
        scratch_shapes=[pltpu.VMEM((d, d), BF16), pltpu.VMEM((d, d), BF16), pltpu.VMEM((d, d), BF16),
                        pltpu.VMEM((8, LANES), F32), pltpu.SemaphoreType.DMA((3,))],
        compiler_params=_params(1),
    )(h, g_norm, wq, wk, wv, wf, bf, sq_a, sq_b, oq_a, oq_b, sk_a, sk_b, ok_a, ok_b)


def _spare_cols(x, base):
    xf = x[:, base:base + N_SPARE].astype(F32)
    return xf[:, 0:1] + xf[:, 1:2] + xf[:, 2:3]


def _with_query_term(x, term, base):
    lane = lax.broadcasted_iota(jnp.int32, x.shape, 1)
    hi, mid, lo = _split3(term)
    out = jnp.where(lane == base, hi.astype(F32), x)
    out = jnp.where(lane == base + 1, mid.astype(F32), out)
    out = jnp.where(lane == base + 2, lo.astype(F32), out)
    return jnp.where((lane >= base + N_SPARE) & (lane < base + 2 * N_SPARE), 1.0, out)


def _flash_fwd(qa, qb, kat, kbt, va, vb):
    s_len, d = qa.shape
    sub = ATT_BLOCK
    n_sub = 2 if s_len % (2 * sub) == 0 else 1
    t = n_sub * sub
    w = min(ATT_CHUNK, s_len)
    n_pair = d // LANES
    n_q = s_len // t
    bases = (HEAD_DIM, 0)
    chains = [(r, hh) for r in range(n_sub) for hh in range(2)]

    def body(qa_ref, qb_ref, kat_ref, kbt_ref, va_ref, vb_ref, o_ref, qa2_ref, qb2_ref, qat2_ref, qbt2_ref):
        i = pl.program_id(1)
        q_refs = (qa_ref, qb_ref)
        qs = [q_refs[hh][r * sub:(r + 1) * sub, :] for r, hh in chains]
        kts = (kat_ref, kbt_ref)
        vs = (va_ref, vb_ref)

        def step(kb, carry, masked, width=w):
            off = pl.multiple_of(kb * w, w)
            cols = pl.ds(off, width)
            scores = [_dot(qs[c], kts[hh][:, cols]) for c, (r, hh) in enumerate(chains)]
            probs, stats = [], []
            for c, (r, hh) in enumerate(chains):
                m, _ = carry[c]
                s = scores[c]
                if masked:
                    s = jnp.where(_causal(i * t + r * sub, off, (sub, width)), s, NEG_BIG)
                m_new = jnp.maximum(m, jnp.max(s, axis=1, keepdims=True))
                probs.append(jnp.exp(s - m_new).astype(BF16))
                stats.append((m_new, jnp.exp(m - m_new)))
            return tuple((stats[c][0], carry[c][1] * stats[c][1] + _dot(probs[c], vs[hh][cols, :]))
                         for c, (r, hh) in enumerate(chains))

        init = ((jnp.full((sub, 1), NEG_BIG, F32), jnp.zeros((sub, LANES), F32)),) * len(chains)
        diag = (i * t) // w
        carry = lax.fori_loop(0, diag, lambda kb, c: step(kb, c, False), init)
        carry = step(diag, carry, True)
        for r in range(n_sub):
            outs, q2 = [], []
            for hh in range(2):
                m, acc = carry[2 * r + hh]
                l = acc[:, bases[hh]:bases[hh] + 1]
                outs.append(acc / l)
                term = _spare_cols(qs[2 * r + hh], bases[hh]) - (m + jnp.log(l))
                q2.append(_with_query_term(qs[2 * r + hh].astype(F32), term, bases[hh]))
            rows = slice(r * sub, (r + 1) * sub)
            o_ref[rows, :] = _pair_select(outs[0], outs[1], (sub, LANES))
            qa2_ref[rows, :] = q2[0].astype(BF16)
            qb2_ref[rows, :] = q2[1].astype(BF16)
            qat2_ref[:, rows] = q2[0].T.astype(BF16)
            qbt2_ref[:, rows] = q2[1].T.astype(BF16)

    qblk = pl.BlockSpec((t, LANES), lambda j, i: (i, j))
    qblk_t = pl.BlockSpec((LANES, t), lambda j, i: (j, i))
    whole_t = pl.BlockSpec((LANES, s_len), lambda j, i: (j, 0))
    whole = pl.BlockSpec((s_len, LANES), lambda j, i: (0, j))
    sd = jax.ShapeDtypeStruct((s_len, d), BF16)
    ds_ = jax.ShapeDtypeStruct((d, s_len), BF16)
    return pl.pallas_call(
        body, name="flash_fwd", grid=(n_pair, n_q),
        in_specs=[qblk, qblk, whole_t, whole_t, whole, whole],
        out_specs=[qblk, qblk, qblk, qblk_t, qblk_t],
        out_shape=[jax.ShapeDtypeStruct((s_len, d), F32), sd, sd, ds_, ds_],
        compiler_params=_params(2),
    )(qa, qb, kat, kbt, va, vb)


def _flash_bwd(qa, qb, qat, qbt, kat, kbt, vat, vbt, doa, dob, doat, dobt):
    s_len, d = qa.shape
    sub = ATT_BLOCK
    n_sub = 2 if s_len % (2 * sub) == 0 else 1
    t = n_sub * sub
    w = min(ATT_CHUNK, s_len)
    n_pair = d // LANES
    n_q = s_len // t
    hd = HEAD_DIM
    chains = [(r, hh) for r in range(n_sub) for hh in range(2)]

    def body(qa_ref, qb_ref, qat_ref, qbt_ref, kat_hbm, kbt_hbm, vat_hbm, vbt_hbm, doa_ref, dob_ref, doat_ref, dobt_ref,
             dqt_ref, dkt_ref, dvt_ref, rs_ref, cs_ref,
             kat_v, kbt_v, vat_v, vbt_v, dkt_acc, dvt_acc, cs_acc, sem):
        j = pl.program_id(0)
        i = pl.program_id(1)

        @pl.when(i == 0)
        def _():
            rows = pl.ds(pl.multiple_of(j * LANES, LANES), LANES)
            copies = [pltpu.make_async_copy(src.at[rows, :], dst, sem.at[n]) for n, (src, dst) in enumerate([
                (kat_hbm, kat_v), (kbt_hbm, kbt_v), (vat_hbm, vat_v), (vbt_hbm, vbt_v)])]
            for cp in copies:
                cp.start()
            dkt_acc[...] = jnp.zeros_like(dkt_acc)
            dvt_acc[...] = jnp.zeros_like(dvt_acc)
            cs_acc[...] = jnp.zeros_like(cs_acc)
            for cp in copies:
                cp.wait()

        q_refs, do_refs = (qa_ref, qb_ref), (doa_ref, dob_ref)
        qs = [q_refs[hh][r * sub:(r + 1) * sub, :] for r, hh in chains]
        dos = [do_refs[hh][r * sub:(r + 1) * sub, :] for r, hh in chains]
        own = (slice(0, hd), slice(hd, 2 * hd))
        spare = (slice(hd, hd + 8), slice(0, 8))
        used = (slice(0, hd + 16), slice(0, 2 * hd))
        qts = (qat_ref[used[0], :], qbt_ref[used[1], :])
        dots = (doat_ref[own[0], :], dobt_ref[own[1], :])
        kts, vts = (kat_v, kbt_v), (vat_v, vbt_v)

        def step(kb, carry, masked, width=w):
            off = pl.multiple_of(kb * w, w)
            cols = pl.ds(off, width)
            scores = [_dot(qs[c], kts[hh][:, cols]) for c, (r, hh) in enumerate(chains)]
            dps = [_dot(dos[c], vts[hh][:, cols]) for c, (r, hh) in enumerate(chains)]
            ps, dss = [], []
            for c, (r, hh) in enumerate(chains):
                s = scores[c]
                if masked:
                    s = jnp.where(_causal(i * t + r * sub, off, (sub, width)), s, NEG_BIG)
                p = jnp.exp(s)
                dss.append((p * dps[c]).astype(BF16))
                ps.append(p.astype(BF16))
            out = tuple(carry[c] + _dot_nt(kts[hh][used[hh], cols], dss[c]) for c, (r, hh) in enumerate(chains))
            for hh in range(2):
                p_all = jnp.concatenate([ps[2 * r + hh] for r in range(n_sub)], axis=0)
                ds_all = jnp.concatenate([dss[2 * r + hh] for r in range(n_sub)], axis=0)
                dvt_acc[own[hh], cols] += _dot(dots[hh], p_all)
                with_sums = _dot(qts[hh], ds_all)
                dkt_acc[own[hh], cols] += with_sums[own[hh], :]
                cs_acc[8 * hh:8 * hh + 8, cols] += with_sums[spare[hh], :]
            return out

        diag = (i * t) // w
        init = (jnp.zeros((hd + 16, sub), F32), jnp.zeros((2 * hd, sub), F32)) * n_sub
        carry = lax.fori_loop(0, diag, lambda kb, c: step(kb, c, False), init)
        carry = lax.cond((i * t) % w + t <= w // 2,
                         lambda c: step(diag, c, True, w // 2), lambda c: step(diag, c, True), carry)
        for c, (r, hh) in enumerate(chains):
            at = slice(r * sub, (r + 1) * sub)
            dqt_ref[own[hh], at] = (carry[c][own[hh], :] * (hd ** -0.5)).astype(BF16)
            rs_ref[0, 8 * hh:8 * hh + 8, at] = carry[c][spare[hh], :]

        @pl.when(i == n_q - 1)
        def _():
            dkt_ref[...] = dkt_acc[...].astype(BF16)
            dvt_ref[...] = dvt_acc[...].astype(BF16)
            cs_ref[0] = cs_acc[...]

    qblk = pl.BlockSpec((t, LANES), lambda j, i: (i, j))
    qblk_t = pl.BlockSpec((LANES, t), lambda j, i: (j, i))
    whole_t = pl.BlockSpec((LANES, s_len), lambda j, i: (j, 0))
    ds_ = jax.ShapeDtypeStruct((d, s_len), BF16)
    sums = jax.ShapeDtypeStruct((n_pair, 16, s_len), F32)
    return pl.pallas_call(
        body, name="flash_bwd", grid=(n_pair, n_q),
        in_specs=[qblk, qblk, qblk_t, qblk_t, ANY, ANY, ANY, ANY, qblk, qblk, qblk_t, qblk_t],
        out_specs=[qblk_t, whole_t, whole_t, pl.BlockSpec((1, 16, t), lambda j, i: (j, 0, i)),
                   pl.BlockSpec((1, 16, s_len), lambda j, i: (j, 0, 0))],
        out_shape=[ds_, ds_, ds_, sums, sums],
        scratch_shapes=[pltpu.VMEM((LANES, s_len), BF16), pltpu.VMEM((LANES, s_len), BF16),
                        pltpu.VMEM((LANES, s_len), BF16), pltpu.VMEM((LANES, s_len), BF16),
                        pltpu.VMEM((LANES, s_len), F32), pltpu.VMEM((LANES, s_len), F32),
                        pltpu.VMEM((16, s_len), F32), pltpu.SemaphoreType.DMA((4,))],
        compiler_params=_params(2),
    )(qa, qb, qat, qbt, kat, kbt, vat, vbt, doa, dob, doat, dobt)


def _wgrad_t(at, b, name):
    k, s_len = at.shape
    n = b.shape[1]
    tn, tk, ts = min(n, 1024), min(k, 1024), min(s_len, 2048)

    def body(a_ref, b_ref, o_ref):
        @pl.when(pl.program_id(2) == 0)
        def _():
            o_ref[...] = jnp.zeros_like(o_ref)
        o_ref[...] += _dot(a_ref[...].astype(BF16), b_ref[...].astype(BF16))

    return pl.pallas_call(
        body, name=name, grid=(k // tk, n // tn, s_len // ts),
        in_specs=[pl.BlockSpec((tk, ts), lambda a, b_, c: (a, c)), pl.BlockSpec((ts, tn), lambda a, b_, c: (c, b_))],
        out_specs=pl.BlockSpec((tk, tn), lambda a, b_, c: (a, b_)),
        out_shape=jax.ShapeDtypeStruct((k, n), F32),
        compiler_params=_params(3),
    )(at, b)


def _oproj_bwd(dh, o, wo, seg, sel_q, tm):
    s_len, d = dh.shape
    sq_a, sq_b, _, _ = sel_q

    def body(dh_ref, o_ref, wo_hbm, seg_ref, sqa_ref, sqb_ref, doa_ref, dob_ref, doat_ref, dobt_ref, wo_v, sem):
        _load_once([(wo_hbm, wo_v)], sem)
        do = _dot_nt(dh_ref[...].astype(BF16), wo_v[...])
        parts = _parts(-_dot3_lhs(do * o_ref[...], seg_ref[...]))
        even = _even_head_lanes((tm, d), 1)
        doa = jnp.where(even, do, _dot(parts, sqa_ref[...]))
        dob = jnp.where(even, _dot(parts, sqb_ref[...]), do)
        doa_ref[...] = doa.astype(BF16)
        dob_ref[...] = dob.astype(BF16)
        doat_ref[...] = doa.T.astype(BF16)
        dobt_ref[...] = dob.T.astype(BF16)

    row = lambda i: (i, 0)
    const2 = lambda i: (0, 0)
    rs, cs = pl.BlockSpec((tm, d), row), pl.BlockSpec((d, tm), lambda i: (0, i))
    sel = pl.BlockSpec((N_SPARE * LANES, d), const2)
    sd = jax.ShapeDtypeStruct((s_len, d), BF16)
    ds_ = jax.ShapeDtypeStruct((d, s_len), BF16)
    return pl.pallas_call(
        body, name="oproj_bwd", grid=(s_len // tm,),
        in_specs=[rs, rs, ANY, pl.BlockSpec((d, LANES), const2), sel, sel],
        out_specs=[rs, rs, cs, cs], out_shape=[sd, sd, ds_, ds_],
        scratch_shapes=[pltpu.VMEM((d, d), BF16), pltpu.SemaphoreType.DMA((1,))],
        compiler_params=_params(1),
    )(dh, o, wo, seg, sq_a, sq_b)


def _oproj_fwd(h, o, wo, tm):
    s_len, d = h.shape

    def body(h_ref, o_ref, wo_hbm, hout_ref, wo_v, sem):
        _load_once([(wo_hbm, wo_v)], sem)
        hout_ref[...] = h_ref[...] + _dot(o_ref[...].astype(BF16), wo_v[...])

    row = lambda i: (i, 0)
    return pl.pallas_call(
        body, name="oproj_fwd", grid=(s_len // tm,),
        in_specs=[pl.BlockSpec((tm, d), row), pl.BlockSpec((tm, d), row), ANY],
        out_specs=pl.BlockSpec((tm, d), row),
        out_shape=jax.ShapeDtypeStruct((s_len, d), F32),
        scratch_shapes=[pltpu.VMEM((d, d), BF16), pltpu.SemaphoreType.DMA((1,))],
        compiler_params=_params(1),
    )(h, o, wo)


def _forget_bwd(dcum, z, tm):
    s_len = dcum.shape[0]
    n_blk = s_len // tm

    def body(dc_ref, z_ref, dfl_ref, gb_ref, total):
        i = pl.program_id(0)

        @pl.when(i == 0)
        def _():
            total[...] = jnp.zeros_like(total)
            gb_ref[...] = jnp.zeros_like(gb_ref)

        upper = (lax.broadcasted_iota(jnp.int32, (tm, tm), 0) <= lax.broadcasted_iota(jnp.int32, (tm, tm), 1))
        suffix = _dot3_rhs(jnp.where(upper, 1.0, 0.0).astype(BF16), dc_ref[...]) + total[0:1, :]
        total[...] = jnp.broadcast_to(suffix[0:1, :], total.shape)
        dfl = suffix * _sigmoid(-z_ref[...])
        dfl_ref[...] = dfl
        gb_ref[...] += jnp.sum(dfl, axis=0, keepdims=True)

    rev = lambda i: (n_blk - 1 - i, 0)
    return pl.pallas_call(
        body, name="forget_bwd", grid=(n_blk,),
        in_specs=[pl.BlockSpec((tm, LANES), rev), pl.BlockSpec((tm, LANES), rev)],
        out_specs=[pl.BlockSpec((tm, LANES), rev), pl.BlockSpec((1, LANES), lambda i: (0, 0))],
        out_shape=[jax.ShapeDtypeStruct((s_len, LANES), F32), jax.ShapeDtypeStruct((1, LANES), F32)],
        scratch_shapes=[pltpu.VMEM((8, LANES), F32)],
        compiler_params=_params(1),
    )(dcum, z)


def _fox_proj_bwd(h, dh, dqt, dkt, dvt, dfl, g_norm, wq, wk, wv, wf, tm):
    s_len, d = h.shape

    def body(h_ref, dh_ref, dqt_ref, dkt_ref, dvt_ref, dfl_ref, g_ref, wq_hbm, wk_hbm, wv_hbm, wf_ref,
             dhin_ref, dflb_ref, gn_ref, wq_v, wk_v, wv_v, sem):
        _load_once([(wq_hbm, wq_v), (wk_hbm, wk_v), (wv_hbm, wv_v)], sem)

        @pl.when(pl.program_id(0) == 0)
        def _():
            gn_ref[...] = jnp.zeros_like(gn_ref)

        g = g_ref[...]
        xhat, r, _ = _rms_fwd(h_ref[...], g)
        dflb = dfl_ref[...].astype(BF16)
        dflb_ref[...] = dflb
        from_qkv = (_dot(wq_v[...], dqt_ref[...]) + _dot(wk_v[...], dkt_ref[...])
                    + _dot(wv_v[...], dvt_ref[...]))
        dhn = _dot_nt(dflb, wf_ref[...]) + from_qkv.T
        dx, gg = _rms_bwd(dhn, xhat, r, g)
        gn_ref[...] += jnp.sum(gg, axis=0, keepdims=True)
        dhin_ref[...] = dh_ref[...] + dx

    row = lambda i: (i, 0)
    const2 = lambda i: (0, 0)
    rs = pl.BlockSpec((tm, d), row)
    cs = pl.BlockSpec((d, tm), lambda i: (0, i))
    return pl.pallas_call(
        body, name="fox_proj_bwd", grid=(s_len // tm,),
        in_specs=[rs, rs, cs, cs, cs, pl.BlockSpec((tm, LANES), row), pl.BlockSpec((1, d), const2), ANY, ANY, ANY,
                  pl.BlockSpec((d, LANES), const2)],
        out_specs=[rs, pl.BlockSpec((tm, LANES), row), pl.BlockSpec((1, d), const2)],
        out_shape=[jax.ShapeDtypeStruct((s_len, d), F32), jax.ShapeDtypeStruct((s_len, LANES), BF16),
                   jax.ShapeDtypeStruct((1, d), F32)],
        scratch_shapes=[pltpu.VMEM((d, d), BF16), pltpu.VMEM((d, d), BF16), pltpu.VMEM((d, d), BF16),
                        pltpu.SemaphoreType.DMA((3,))],
        compiler_params=_params(1),
    )(h, dh, dqt, dkt, dvt, dfl, g_norm, wq, wk, wv, wf)


def _loss_head(h, target, g_final, tm):
    s_len, d = h.shape
    n_blk = s_len // tm

    def body(h_ref, t_ref, g_ref, dh_ref, loss_ref, gg_ref, sq):
        i = pl.program_id(0)

        @pl.when(i == 0)
        def _():
            sq[...] = jnp.zeros_like(sq)
            gg_ref[...] = jnp.zeros_like(gg_ref)

        g = g_ref[...]
        xhat, r, y = _rms_fwd(h_ref[...], g)
        err = y - t_ref[...]
        sq[...] += jnp.sum(err * err, axis=0, keepdims=True)
        dx, gg = _rms_bwd(err * (1.0 / d), xhat, r, g)
        gg_ref[...] += jnp.sum(gg, axis=0, keepdims=True)
        dh_ref[...] = dx

        @pl.when(i == n_blk - 1)
        def _():
            loss_ref[...] = jnp.broadcast_to(jnp.sum(sq[...], axis=1, keepdims=True) * (0.5 / d), loss_ref.shape)

    row = lambda i: (i, 0)
    const2 = lambda i: (0, 0)
    return pl.pallas_call(
        body, name="loss_head", grid=(n_blk,),
        in_specs=[pl.BlockSpec((tm, d), row), pl.BlockSpec((tm, d), row), pl.BlockSpec((1, d), const2)],
        out_specs=[pl.BlockSpec((tm, d), row), pl.BlockSpec((1, LANES), const2), pl.BlockSpec((1, d), const2)],
        out_shape=[jax.ShapeDtypeStruct((s_len, d), F32), jax.ShapeDtypeStruct((1, LANES), F32),
                   jax.ShapeDtypeStruct((1, d), F32)],
        scratch_shapes=[pltpu.VMEM((1, d), F32)],
        compiler_params=_params(1),
    )(h, target, g_final)


def _wgrad(x, dy, n_piece, name):
    s_len, k = x.shape
    n = dy.shape[1]
    tn = min(n // n_piece, 1024)
    tk = min(k, 1024)
    ts = min(s_len, 2048)
    per_piece = (n // n_piece) // tn

    def body(x_ref, dy_ref, o_ref):
        @pl.when(pl.program_id(2) == 0)
        def _():
            o_ref[...] = jnp.zeros_like(o_ref)
        o_ref[0] += _dot_tn(x_ref[...].astype(BF16), dy_ref[...].astype(BF16))

    return pl.pallas_call(
        body, name=name, grid=(k // tk, n // tn, s_len // ts),
        in_specs=[pl.BlockSpec((ts, tk), lambda a, b, c: (c, a)), pl.BlockSpec((ts, tn), lambda a, b, c: (c, b))],
        out_specs=pl.BlockSpec((1, tk, tn), lambda a, b, c: (b // per_piece, a, b % per_piece)),
        out_shape=jax.ShapeDtypeStruct((n_piece, k, n // n_piece), F32),
        compiler_params=_params(3),
    )(x, dy)


def _pair_sum(g, recv, core, name):
    n_piece, rows, c = g.shape
    half = rows // 2
    tr = min(half, 512)
    nb = half // tr

    def body(core_ref, g_ref, r_ref, o_ref, ob_ref):
        total = g_ref[...] + r_ref[...]
        o_ref[...] = total
        ob_ref[...] = total.astype(BF16)

    blk = pl.BlockSpec((1, tr, c), lambda p, i, core_ref: (p, i, 0))
    return pl.pallas_call(
        body, name=name,
        out_shape=[jax.ShapeDtypeStruct((n_piece, half, c), F32), jax.ShapeDtypeStruct((n_piece, half, c), BF16)],
        grid_spec=pltpu.PrefetchScalarGridSpec(
            num_scalar_prefetch=1, grid=(n_piece, nb),
            in_specs=[pl.BlockSpec((1, tr, c), lambda p, i, core_ref: (p, core_ref[0] * nb + i, 0)), blk],
            out_specs=[blk, blk]),
        compiler_params=_params(2),
    )(core, g, recv)


def _chip_sum(halves, recv, chip, name):
    _, h, c = halves.shape
    tr = min(h, 512)

    def body(chip_ref, own_ref, r_ref, o_ref):
        o_ref[...] = ((own_ref[0] + r_ref[0].astype(F32)) + r_ref[1].astype(F32)) + r_ref[2].astype(F32)

    return pl.pallas_call(
        body, name=name, out_shape=jax.ShapeDtypeStruct((h, c), F32),
        grid_spec=pltpu.PrefetchScalarGridSpec(
            num_scalar_prefetch=1, grid=(h // tr,),
            in_specs=[pl.BlockSpec((1, tr, c), lambda i, chip_ref: (chip_ref[0], i, 0)),
                      pl.BlockSpec((3, tr, c), lambda i, chip_ref: (0, i, 0))],
            out_specs=pl.BlockSpec((tr, c), lambda i, chip_ref: (i, 0))),
        compiler_params=_params(1),
    )(chip, halves, recv)


def _adamw_math(w, m, v, g):
    m_new = ADAM_B1 * m + (1.0 - ADAM_B1) * g
    v_new = ADAM_B2 * v + (1.0 - ADAM_B2) * (g * g)
    m_hat = m_new / (1.0 - ADAM_B1 ** ADAM_STEP)
    v_hat = v_new / (1.0 - ADAM_B2 ** ADAM_STEP)
    return -ADAM_LR * (m_hat / (jnp.sqrt(v_hat) + ADAM_EPS) + ADAM_WD * w), m_new, v_new


def _adamw(w, m, v, g, name):
    rows, c = w.shape
    tr = min(rows, 256)

    def body(w_ref, m_ref, v_ref, g_ref, d_ref, mo_ref, vo_ref):
        d_ref[...], mo_ref[...], vo_ref[...] = _adamw_math(w_ref[...], m_ref[...], v_ref[...], g_ref[...])

    spec = pl.BlockSpec((tr, c), lambda i: (i, 0))
    shape = jax.ShapeDtypeStruct((rows, c), F32)
    return pl.pallas_call(
        body, name=name, grid=(rows // tr,),
        in_specs=[spec] * 4, out_specs=[spec] * 3, out_shape=[shape] * 3,
        compiler_params=_params(1),
    )(w, m, v, g)


def _adamw_halves(w, m, v, g_own, g_other, core, name):
    rows, c = w.shape
    half = rows // 2
    tr = min(half, 256)
    nb = half // tr

    def body(core_ref, w_ref, m_ref, v_ref, own_ref, other_ref, g_ref, d_ref, mo_ref, vo_ref):
        mine = (pl.program_id(0) // nb) == core_ref[0]
        g = jnp.where(mine, own_ref[...], other_ref[...])
        g_ref[...] = g
        d_ref[...], mo_ref[...], vo_ref[...] = _adamw_math(w_ref[...], m_ref[...], v_ref[...], g)

    spec = pl.BlockSpec((tr, c), lambda i, core_ref: (i, 0))
    own = pl.BlockSpec((tr, c), lambda i, core_ref: (jnp.clip(i - core_ref[0] * nb, 0, nb - 1), 0))
    other = pl.BlockSpec((tr, c), lambda i, core_ref: (jnp.clip(i - (1 - core_ref[0]) * nb, 0, nb - 1), 0))
    shape = jax.ShapeDtypeStruct((rows, c), F32)
    return pl.pallas_call(
        body, name=name, out_shape=[shape] * 4,
        grid_spec=pltpu.PrefetchScalarGridSpec(
            num_scalar_prefetch=1, grid=(rows // tr,),
            in_specs=[spec, spec, spec, own, other], out_specs=[spec] * 4),
        compiler_params=_params(1),
    )(core, w, m, v, g_own, g_other)


def _place():
    x, y, c = lax.axis_index("x"), lax.axis_index("y"), lax.axis_index("c")
    chips = [(1 - x, y), (x, 1 - y), (1 - x, 1 - y)]
    return x, y, c, chips


def _half_of(ref, half, which):
    start = which * half
    if half % 8 == 0:
        start = pl.multiple_of(start, 8)
    return ref.at[pl.ds(start, half)]


def _gather_copies(ins, outs, send_sems, recv_sems):
    x, y, c, chips = _place()
    mine = 2 * x + y
    copies = []
    for k, (src, dst) in enumerate(zip(ins, outs)):
        half = src.shape[0] // 2
        copies.append(pltpu.make_async_remote_copy(
            src_ref=src, dst_ref=dst.at[mine], send_sem=send_sems.at[4 * k + 3], recv_sem=recv_sems.at[4 * k + 3],
            device_id=(x, y, 1 - c), device_id_type=MESH))
        for j, (tx, ty) in enumerate(chips):
            copies.append(pltpu.make_async_remote_copy(
                src_ref=_half_of(src, half, c), dst_ref=_half_of(dst.at[mine], half, c),
                send_sem=send_sems.at[4 * k + j], recv_sem=recv_sems.at[4 * k + j],
                device_id=(tx, ty, c), device_id_type=MESH))
    return copies


def _gather_ici(shards):
    n = len(shards)

    def body(*refs):
        copies = _gather_copies(refs[:n], refs[n:2 * n], refs[2 * n], refs[2 * n + 1])
        for cp in copies:
            cp.start()
        for cp in copies:
            cp.wait()

    return pl.pallas_call(
        body, name="weights_gather_ici",
        in_specs=[ANY] * n, out_specs=[ANY] * n,
        out_shape=[jax.ShapeDtypeStruct((4,) + s.shape, s.dtype) for s in shards],
        scratch_shapes=[pltpu.SemaphoreType.DMA((4 * n,)), pltpu.SemaphoreType.DMA((4 * n,))],
    )(*shards)


HBM_SPEC = pl.BlockSpec(memory_space=pltpu.HBM)
SEM_SPEC = pl.BlockSpec(memory_space=pltpu.SEMAPHORE)
IN_FLIGHT = pltpu.SideEffectType.DATAFLOW_SIDE_EFFECTING


def _gather_ici_start(shards, after):
    n = len(shards)

    def body(*refs):
        for cp in _gather_copies(refs[:n], refs[n:2 * n], refs[2 * n + 1], refs[2 * n + 2]):
            cp.start()
        token = refs[-1]
        token[...] = jnp.zeros_like(token)

    lands = [lax.empty((4,) + s.shape, s.dtype) for s in shards]
    out = pl.pallas_call(
        body, name="weights_gather_start",
        out_shape=(pltpu.SemaphoreType.DMA((4 * n,)), pltpu.SemaphoreType.DMA((4 * n,)),
                   *[pltpu.HBM(s.shape, s.dtype) for s in shards], *[pltpu.HBM(l.shape, l.dtype) for l in lands],
                   jax.ShapeDtypeStruct((8, LANES), F32)),
        in_specs=[HBM_SPEC] * (2 * n) + [ANY],
        out_specs=(SEM_SPEC, SEM_SPEC, *[HBM_SPEC] * (2 * n), pl.BlockSpec(memory_space=pltpu.VMEM)),
        input_output_aliases={k: 2 + k for k in range(2 * n)},
        compiler_params=pltpu.CompilerParams(has_side_effects=IN_FLIGHT),
    )(*[pltpu.with_memory_space_constraint(a, pltpu.HBM) for a in list(shards) + lands], after)
    return out[0], out[1], out[2:2 + n], out[2 + n:2 + 2 * n], out[-1]


def _gather_ici_wait(send_sems, recv_sems, sources, lands, after):
    n = len(sources)

    def body(*refs):
        for cp in _gather_copies(refs[:n], refs[n:2 * n], refs[2 * n], refs[2 * n + 1]):
            cp.wait_send()
            cp.wait_recv()

    out = pl.pallas_call(
        body, name="weights_gather_wait",
        out_shape=[pltpu.HBM(a.shape, a.dtype) for a in list(sources) + list(lands)],
        in_specs=[HBM_SPEC] * (2 * n) + [SEM_SPEC, SEM_SPEC, ANY], out_specs=[HBM_SPEC] * (2 * n),
        input_output_aliases={k: k for k in range(2 * n)},
        compiler_params=pltpu.CompilerParams(has_side_effects=IN_FLIGHT),
    )(*sources, *lands, send_sems, recv_sems, after)
    return out[n:]


def _gather_pair(gathered):
    n = len(gathered)

    def body(*refs):
        outs = refs[n:2 * n]
        send_sems, recv_sems = refs[2 * n:]
        x, y, c, chips = _place()
        sends = []
        for k in range(n):
            half = gathered[k].shape[1] // 2
            for j, (tx, ty) in enumerate(chips):
                piece = _half_of(outs[k].at[2 * tx + ty], half, c)
                sends.append(pltpu.make_async_remote_copy(
                    src_ref=piece, dst_ref=piece, send_sem=send_sems.at[k, j], recv_sem=recv_sems.at[k, j],
                    device_id=(x, y, 1 - c), device_id_type=MESH))
        for cp in sends:
            cp.start()
        for cp in sends:
            cp.wait()

    return pl.pallas_call(
        body, name="weights_gather_pair",
        in_specs=[ANY] * n, out_specs=[ANY] * n,
        out_shape=[jax.ShapeDtypeStruct(g.shape, g.dtype) for g in gathered],
        input_output_aliases={k: k for k in range(n)},
        scratch_shapes=[pltpu.SemaphoreType.DMA((n, 3)), pltpu.SemaphoreType.DMA((n, 3))],
    )(*gathered)


def _pair_exchange(grads, tag):
    n = len(grads)

    def body(*refs):
        ins, outs = refs[:n], refs[n:2 * n]
        send_sems, recv_sems = refs[2 * n:]
        x, y, c, _ = _place()
        copies = []
        for k in range(n):
            half = grads[k].shape[1] // 2
            other = ins[k].at[:, pl.ds(pl.multiple_of((1 - c) * half, 8), half), :]
            copies.append(pltpu.make_async_remote_copy(
                src_ref=other, dst_ref=outs[k], send_sem=send_sems.at[k], recv_sem=recv_sems.at[k],
                device_id=(x, y, 1 - c), device_id_type=MESH))
        for cp in copies:
            cp.start()
        for cp in copies:
            cp.wait()

    return pl.pallas_call(
        body, name=f"grads_pair_exchange_{tag}",
        in_specs=[ANY] * n, out_specs=[ANY] * n,
        out_shape=[jax.ShapeDtypeStruct((4, g.shape[1] // 2, g.shape[2]), F32) for g in grads],
        scratch_shapes=[pltpu.SemaphoreType.DMA((n,)), pltpu.SemaphoreType.DMA((n,))],
    )(*grads)


def _scatter_copies(ins, outs, send_sems, recv_sems):
    x, y, c, chips = _place()
    return [pltpu.make_async_remote_copy(
        src_ref=ins[k].at[2 * tx + ty], dst_ref=outs[k].at[j], send_sem=send_sems.at[3 * k + j],
        recv_sem=recv_sems.at[3 * k + j], device_id=(tx, ty, c), device_id_type=MESH)
        for k in range(len(ins)) for j, (tx, ty) in enumerate(chips)]


def _chip_scatter_start(halves, tag, after=()):
    n = len(halves)
    n_in = 2 * n + len(after)

    def body(*refs):
        for cp in _scatter_copies(refs[:n], refs[n:2 * n], refs[n_in], refs[n_in + 1]):
            cp.start()
        token = refs[-1]
        token[...] = jnp.zeros_like(token)

    lands = [lax.empty((3,) + hv.shape[1:], hv.dtype) for hv in halves]
    out = pl.pallas_call(
        body, name=f"grads_chip_scatter_start_{tag}",
        out_shape=(pltpu.SemaphoreType.DMA((3 * n,)), pltpu.SemaphoreType.DMA((3 * n,)),
                   *[pltpu.HBM(a.shape, a.dtype) for a in list(halves) + lands], jax.ShapeDtypeStruct((8, LANES), F32)),
        in_specs=[HBM_SPEC] * (2 * n) + [ANY] * len(after),
        out_specs=(SEM_SPEC, SEM_SPEC, *[HBM_SPEC] * (2 * n), pl.BlockSpec(memory_space=pltpu.VMEM)),
        input_output_aliases={k: 2 + k for k in range(2 * n)},
        compiler_params=pltpu.CompilerParams(has_side_effects=IN_FLIGHT),
    )(*[pltpu.with_memory_space_constraint(a, pltpu.HBM) for a in list(halves) + lands], *after)
    return out[0], out[1], out[2:2 + n], out[2 + n:2 + 2 * n], out[-1]


def _chip_scatter_wait(send_sems, recv_sems, sources, lands, after, tag):
    n = len(sources)

    def body(*refs):
        for cp in _scatter_copies(refs[:n], refs[n:2 * n], refs[2 * n], refs[2 * n + 1]):
            cp.wait_send()
            cp.wait_recv()

    out = pl.pallas_call(
        body, name=f"grads_chip_scatter_wait_{tag}",
        out_shape=[pltpu.HBM(a.shape, a.dtype) for a in list(sources) + list(lands)],
        in_specs=[HBM_SPEC] * (2 * n) + [SEM_SPEC, SEM_SPEC] + [ANY] * len(after), out_specs=[HBM_SPEC] * (2 * n),
        input_output_aliases={k: k for k in range(2 * n)},
        compiler_params=pltpu.CompilerParams(has_side_effects=IN_FLIGHT),
    )(*sources, *lands, send_sems, recv_sems, *after)
    return out[n:]


def _pair_share(finals, tag, after):
    n = len(finals)

    def body(*refs):
        ins, outs = refs[:n], refs[n + 1:2 * n + 1]
        send_sems, recv_sems = refs[2 * n + 1:]
        x, y, c, _ = _place()
        copies = [pltpu.make_async_remote_copy(
            src_ref=ins[k], dst_ref=outs[k], send_sem=send_sems.at[k], recv_sem=recv_sems.at[k],
            device_id=(x, y, 1 - c), device_id_type=MESH) for k in range(n)]
        for cp in copies:
            cp.start()
        for cp in copies:
            cp.wait()

    return pl.pallas_call(
        body, name=f"grads_pair_share_{tag}",
        in_specs=[ANY] * (n + 1), out_specs=[ANY] * n,
        out_shape=[jax.ShapeDtypeStruct(fv.shape, F32) for fv in finals],
        scratch_shapes=[pltpu.SemaphoreType.DMA((n,)), pltpu.SemaphoreType.DMA((n,))],
    )(*finals, after)


def _small_all_reduce(buf):
    rows, c_ = buf.shape

    def body(in_ref, out_ref, pair_buf, slots, send_sems, recv_sems):
        x, y, c, chips = _place()
        mine = 2 * x + y
        pair = pltpu.make_async_remote_copy(
            src_ref=in_ref, dst_ref=pair_buf, send_sem=send_sems.at[0], recv_sem=recv_sems.at[0],
            device_id=(x, y, 1 - c), device_id_type=MESH)
        pair.start()
        pair.wait()
        slots[mine] = in_ref[...] + pair_buf[...]
        sends = [pltpu.make_async_remote_copy(
            src_ref=slots.at[mine], dst_ref=slots.at[mine], send_sem=send_sems.at[1 + j], recv_sem=recv_sems.at[1 + j],
            device_id=(tx, ty, c), device_id_type=MESH) for j, (tx, ty) in enumerate(chips)]
        for cp in sends:
            cp.start()
        for j, (tx, ty) in enumerate(chips):
            pltpu.make_async_remote_copy(
                src_ref=slots.at[mine], dst_ref=slots.at[2 * tx + ty], send_sem=send_sems.at[1 + j],
                recv_sem=recv_sems.at[1 + j], device_id=(tx, ty, c), device_id_type=MESH).wait()
        out_ref[...] = ((slots[0] + slots[1]) + slots[2]) + slots[3]

    vm = pl.BlockSpec(memory_space=pltpu.VMEM)
    return pl.pallas_call(
        body, name="small_all_reduce", in_specs=[vm], out_specs=vm,
        out_shape=jax.ShapeDtypeStruct((rows, c_), F32),
        scratch_shapes=[pltpu.VMEM((rows, c_), F32), pltpu.VMEM((4, rows, c_), F32),
                        pltpu.SemaphoreType.DMA((4,)), pltpu.SemaphoreType.DMA((4,))],
        compiler_params=pltpu.CompilerParams(vmem_limit_bytes=VMEM_LIMIT_V7X),
    )(buf)


def _pair_reduce(grads, core, tag):
    recv = _pair_exchange(grads, tag)
    return [_pair_sum(g, r, core, f"pair_sum_{tag}_{k}") for k, (g, r) in enumerate(zip(grads, recv))]


def _chip_reduce(halves, recv, chip, tag):
    return [_chip_sum(hv, r, chip, f"chip_sum_{tag}_{k}") for k, ((hv, _), r) in enumerate(zip(halves, recv))]


PACK_COLS = 1024


def _pack(arrays):
    flat = jnp.concatenate([a.reshape(-1).astype(F32) for a in arrays])
    rows = -(-flat.shape[0] // PACK_COLS)
    rows = -(-rows // 8) * 8
    return jnp.pad(flat, (0, rows * PACK_COLS - flat.shape[0])).reshape(rows, PACK_COLS)


def _unpack(buf, shapes):
    flat = buf.reshape(-1)
    out, at = [], 0
    for shp in shapes:
        size = math.prod(shp)
        out.append(flat[at:at + size].reshape(shp))
        at += size
    return out


def kernel(x, mix_norm_g, ffn_norm_g, gm_w_in, gm_ln_g, gm_ln_b, gm_w_s, gm_b_s, gm_w_out, fox_w_qkvf, fox_b_f, fox_w_o, ffn_w_gate, ffn_w_up, ffn_conv_w, ffn_conv_b, ffn_w_down, final_norm_g, loss_target, m_mix_norm_g, m_ffn_norm_g, m_gm_w_in, m_gm_ln_g, m_gm_ln_b, m_gm_w_s, m_gm_b_s, m_gm_w_out, m_fox_w_qkvf, m_fox_b_f, m_fox_w_o, m_ffn_w_gate, m_ffn_w_up, m_ffn_conv_w, m_ffn_conv_b, m_ffn_w_down, m_final_norm_g, v_mix_norm_g, v_ffn_norm_g, v_gm_w_in, v_gm_ln_g, v_gm_ln_b, v_gm_w_s, v_gm_b_s, v_gm_w_out, v_fox_w_qkvf, v_fox_b_f, v_fox_w_o, v_ffn_w_gate, v_ffn_w_up, v_ffn_conv_w, v_ffn_conv_b, v_ffn_w_down, v_final_norm_g):
    _, s_len, d = x.shape
    e = gm_ln_g.shape[1]
    f = ffn_conv_b.shape[1]
    n_head = fox_b_f.shape[1]
    n_pair = n_head // 2
    gd = e // GM_GROUPS
    qkvf_cols = fox_w_qkvf.shape[2]
    assert d == n_head * HEAD_DIM and d % (2 * LANES) == 0 and s_len % 512 == 0 and gd % LANES == 0
    assert gm_w_s.shape[2] == CHUNK and 4 * qkvf_cols == 3 * d + n_head
    tm = 256
    h0 = x[0]
    target = loss_target[0]

    w_in, w_out4 = _gather_pair(_gather_ici([gm_w_in[0].astype(BF16), gm_w_out[0].astype(BF16)]))
    send_sems, recv_sems, sources, lands, token = _gather_ici_start([
        fox_w_qkvf[0].astype(BF16), fox_w_o[0].astype(BF16), ffn_w_gate.astype(BF16), ffn_w_up.astype(BF16),
        ffn_w_down.astype(BF16), ffn_conv_w], after=w_in)
    w_out = w_out4.reshape(e, d)
    bf_pad = jnp.pad(fox_b_f, ((0, 0), (0, LANES - n_head)))

    tril = jnp.tril(jnp.ones((CHUNK, CHUNK), bool))
    wc = jnp.where(tril[None], gm_w_s[0], 0.0).astype(BF16)
    wct = jnp.transpose(wc, (0, 2, 1))
    bias = jnp.repeat(gm_b_s[0].T, gd, axis=1)
    seg_groups = (jnp.arange(e)[:, None] // gd == jnp.arange(LANES)[None, :]).astype(BF16)
    seg_heads = (jnp.arange(d)[:, None] // HEAD_DIM == jnp.arange(LANES)[None, :]).astype(BF16)
    sel_q = _spare_selectors(d, key_side=False)
    sel_k = _spare_selectors(d, key_side=True)

    h1, a0, hn0, gated0 = _gmlp_fwd(h0, mix_norm_g[0:1] + token[0:1, 0:1], w_in, gm_ln_g, gm_ln_b, wc, bias, w_out, tm)
    qkvf4, wo4, wg_all, wu_all, wd_all, cw4 = _gather_pair(_gather_ici_wait(send_sems, recv_sems, sources, lands, h1))
    qkvf = jnp.transpose(qkvf4, (1, 0, 2)).reshape(d, 4 * qkvf_cols)
    wq, wk, wv = qkvf[:, :d], qkvf[:, d:2 * d], qkvf[:, 2 * d:3 * d]
    wf = jnp.pad(qkvf[:, 3 * d:], ((0, 0), (0, LANES - n_head)))
    wo = wo4.reshape(d, d)
    conv_w_full = jnp.transpose(cw4, (1, 2, 0, 3)).reshape(2, 3, f)
    conv_w8 = jnp.pad(conv_w_full, ((0, 0), (0, 5), (0, 0)))
    h2, fa0, fup0, fhn0, fhid0 = _ffn_fwd(h1, ffn_norm_g[0:1], wg_all, wu_all, wd_all, 0, conv_w8[0], ffn_conv_b[0:1], tm)
    (hn1, qa, qb, kat, kbt, va, vb, vat, vbt, z_f) = _fox_proj_fwd(
        h2, mix_norm_g[1:2], wq, wk, wv, wf, bf_pad, sel_q, sel_k, tm)
    o, qa2, qb2, qat2, qbt2 = _flash_fwd(qa, qb, kat, kbt, va, vb)
    h3 = _oproj_fwd(h2, o, wo, tm)
    h4, fa1, fup1, fhn1, fhid1 = _ffn_fwd(h3, ffn_norm_g[1:2], wg_all, wu_all, wd_all, 1, conv_w8[1], ffn_conv_b[1:2], tm)

    dh4, loss_part, g_final = _loss_head(h4, target, final_norm_g.reshape(1, d), tm)
    dh3, da1, dup1, gcw1, gcb1, gfn1 = _ffn_bwd(h3, dh4, fa1, fup1, ffn_norm_g[1:2], wg_all, wu_all, wd_all, 1,
                                                conv_w8[1], ffn_conv_b[1:2], tm)
    g_gate1 = _wgrad(fhn1, da1, 4, "wgrad_gate_1")
    g_up1 = _wgrad(fhn1, dup1, 4, "wgrad_up_1")
    g_down1 = _wgrad(fhid1, dh4, 1, "wgrad_down_1").reshape(4, f // 4, d)

    doa, dob, doat, dobt = _oproj_bwd(dh3, o, wo, seg_heads, sel_q, tm)
    g_wo = _wgrad(o, dh3, 1, "wgrad_wo").reshape(4, d // 4, d)
    dqt, dkt, dvt, row_sums, col_sums = _flash_bwd(qa2, qb2, qat2, qbt2, kat, kbt, vat, vbt, doa, dob, doat, dobt)
    sums = row_sums[:, 0::8, :] - col_sums[:, N_SPARE::8, :]
    dcum = jnp.pad(sums.reshape(n_head, s_len).T, ((0, 0), (0, LANES - n_head)))
    dfl, g_bf = _forget_bwd(dcum, z_f, tm)
    dh2, dflb, gmn1 = _fox_proj_bwd(h2, dh3, dqt, dkt, dvt, dfl, mix_norm_g[1:2], wq, wk, wv, wf, tm)
    g_q = _wgrad_t(dqt, hn1, "wgrad_q").T
    g_k = _wgrad_t(dkt, hn1, "wgrad_k").T
    g_v = _wgrad_t(dvt, hn1, "wgrad_v").T
    g_f = _wgrad(hn1, dflb, 1, "wgrad_f")[0][:, :n_head]
    g_qkvf = jnp.concatenate([g_q, g_k, g_v, g_f], axis=1).reshape(d, 4, qkvf_cols).transpose(1, 0, 2)

    core = lax.axis_index("c").astype(jnp.int32).reshape(1)
    chip = (2 * lax.axis_index("x") + lax.axis_index("y")).astype(jnp.int32).reshape(1)
    halves_early = _pair_reduce([g_qkvf, g_wo, g_gate1, g_up1, g_down1], core, "early")
    sc_send, sc_recv, sc_src, sc_land, sc_token = _chip_scatter_start([hb for _, hb in halves_early], "early")

    dh1, da0f, dup0, gcw0, gcb0, gfn0 = _ffn_bwd(h1, dh2, fa0, fup0, ffn_norm_g[0:1] + sc_token[0:1, 0:1],
                                                 wg_all, wu_all, wd_all, 0, conv_w8[0], ffn_conv_b[0:1], tm)
    g_gate0 = _wgrad(fhn0, da0f, 4, "wgrad_gate_0")
    g_up0 = _wgrad(fhn0, dup0, 4, "wgrad_up_0")
    g_down0 = _wgrad(fhid0, dh2, 1, "wgrad_down_0").reshape(4, f // 4, d)

    dh0, da0, g_ws, g_bs_t, g_lng, g_lnb, gmn0 = _gmlp_bwd(
        h0, dh1, a0, mix_norm_g[0:1], w_in, gm_ln_g, gm_ln_b, wc, wct, bias, w_out, seg_groups, tm)
    g_win = _wgrad(hn0, da0, 4, "wgrad_gm_in")
    g_wout = _wgrad(gated0, dh1, 1, "wgrad_gm_out").reshape(4, e // 4, d)

    small = [jnp.concatenate([gmn0, gmn1]), jnp.concatenate([gfn0, gfn1]), g_lng, g_lnb, g_ws[None],
             g_bs_t[:, :GM_GROUPS].T[None], g_bf[:, :n_head], jnp.stack([gcw0[:3], gcw1[:3]]),
             jnp.concatenate([gcb0, gcb1]), g_final.reshape(d), loss_part[0, :1]]
    small_shapes = [a.shape for a in small]
    small_sum = _small_all_reduce(_pack(small))
    reduced = _unpack(small_sum, small_shapes)
    (r_mix, r_ffn, r_lng, r_lnb, r_ws, r_bs, r_bf, r_cw_full, r_cb, r_final, r_loss) = reduced
    r_cw = lax.dynamic_slice_in_dim(r_cw_full, chip[0] * (f // 4), f // 4, axis=2)

    halves_late = _pair_reduce([g_win, g_wout, g_gate0, g_up0, g_down0], core, "late")
    lt_send, lt_recv, lt_src, lt_land, lt_token = _chip_scatter_start([hb for _, hb in halves_late], "late", [small_sum])
    recv_early = _chip_scatter_wait(sc_send, sc_recv, sc_src, sc_land, [halves_late[0][0]], "early")
    finals_early = _chip_reduce(halves_early, recv_early, chip, "early")
    r_qkvf, r_wo, r_gate1, r_up1, r_down1 = zip(finals_early, _pair_share(finals_early, "early", lt_token))

    def update(name, layer, w, m, v, grad, core_):
        return _adamw_halves(w[layer], m[layer], v[layer], grad[0], grad[1], core_, f"adamw_{name}_{layer}")

    u_qkvf = update("fox_w_qkvf", 0, fox_w_qkvf, m_fox_w_qkvf, v_fox_w_qkvf, r_qkvf, core)
    u_wo = update("fox_w_o", 0, fox_w_o, m_fox_w_o, v_fox_w_o, r_wo, core)
    u_gate1 = update("ffn_w_gate", 1, ffn_w_gate, m_ffn_w_gate, v_ffn_w_gate, r_gate1, core)
    u_up1 = update("ffn_w_up", 1, ffn_w_up, m_ffn_w_up, v_ffn_w_up, r_up1, core)
    u_down1 = update("ffn_w_down", 1, ffn_w_down, m_ffn_w_down, v_ffn_w_down, r_down1, core)

    res = {}
    small_names = ["mix_norm_g", "ffn_norm_g", "gm_ln_g", "gm_ln_b", "gm_w_s", "gm_b_s", "fox_b_f", "ffn_conv_w",
                   "ffn_conv_b", "final_norm_g"]
    small_w = [mix_norm_g, ffn_norm_g, gm_ln_g, gm_ln_b, gm_w_s, gm_b_s, fox_b_f, ffn_conv_w, ffn_conv_b, final_norm_g]
    small_m = [m_mix_norm_g, m_ffn_norm_g, m_gm_ln_g, m_gm_ln_b, m_gm_w_s, m_gm_b_s, m_fox_b_f, m_ffn_conv_w,
               m_ffn_conv_b, m_final_norm_g]
    small_v = [v_mix_norm_g, v_ffn_norm_g, v_gm_ln_g, v_gm_ln_b, v_gm_w_s, v_gm_b_s, v_fox_b_f, v_ffn_conv_w,
               v_ffn_conv_b, v_final_norm_g]
    small_g = [r_mix, r_ffn, r_lng, r_lnb, r_ws, r_bs, r_bf, r_cw, r_cb, r_final]
    shapes = [w.shape for w in small_w]
    small_g = [g.reshape(s) for g, s in zip(small_g, shapes)]
    dlt, mn, vn = _adamw(_pack(small_w), _pack(small_m), _pack(small_v), _pack(small_g), "adamw_small")
    for name, g, dl_, m_, v_ in zip(small_names, small_g, _unpack(dlt, shapes), _unpack(mn, shapes), _unpack(vn, shapes)):
        res[name] = (g, dl_, m_, v_)

    done_meanwhile = [u[1] for u in (u_qkvf, u_wo, u_gate1, u_up1, u_down1)] + [dlt]
    recv_late = _chip_scatter_wait(lt_send, lt_recv, lt_src, lt_land, done_meanwhile, "late")
    finals_late = _chip_reduce(halves_late, recv_late, chip, "late")
    r_win, r_wout, r_gate0, r_up0, r_down0 = zip(finals_late, _pair_share(finals_late, "late", lt_token))
    u_gate0 = update("ffn_w_gate", 0, ffn_w_gate, m_ffn_w_gate, v_ffn_w_gate, r_gate0, core)
    u_up0 = update("ffn_w_up", 0, ffn_w_up, m_ffn_w_up, v_ffn_w_up, r_up0, core)
    u_down0 = update("ffn_w_down", 0, ffn_w_down, m_ffn_w_down, v_ffn_w_down, r_down0, core)
    layers = {"gm_w_in": [update("gm_w_in", 0, gm_w_in, m_gm_w_in, v_gm_w_in, r_win, core)],
              "gm_w_out": [update("gm_w_out", 0, gm_w_out, m_gm_w_out, v_gm_w_out, r_wout, core)],
              "fox_w_qkvf": [u_qkvf], "fox_w_o": [u_wo], "ffn_w_gate": [u_gate0, u_gate1],
              "ffn_w_up": [u_up0, u_up1], "ffn_w_down": [u_down0, u_down1]}
    for name, parts in layers.items():
        res[name] = tuple(jnp.stack([p[i] for p in parts]) for i in range(4))

    order = ["mix_norm_g", "ffn_norm_g", "gm_w_in", "gm_ln_g", "gm_ln_b", "gm_w_s", "gm_b_s", "gm_w_out", "fox_w_qkvf",
             "fox_b_f", "fox_w_o", "ffn_w_gate", "ffn_w_up", "ffn_conv_w", "ffn_conv_b", "ffn_w_down", "final_norm_g"]
    outs = [r_loss.reshape(()), dh0[None]]
    for part in range(4):
        outs += [res[name][part] for name in order]
    return tuple(outs)
```

```python
import math

import jax
import jax.numpy as jnp
from jax import lax
from jax.experimental import pallas as pl
from jax.experimental.pallas import tpu as pltpu

F32 = jnp.float32
BF16 = jnp.bfloat16

RMS_EPS = 1e-6
LN_EPS = 1e-5
CHUNK = 128
GM_GROUPS = 8
HEAD_DIM = 64
LANES = 128
ATT_BLOCK = 256
ATT_CHUNK = 1024
VMEM_LIMIT_V7X = 56 * 1024 * 1024

ADAM_LR = 0.001
ADAM_B1 = 0.9
ADAM_B2 = 0.999
ADAM_EPS = 1e-08
ADAM_WD = 0.01
ADAM_STEP = 10

MESH = pl.DeviceIdType.MESH
ANY = pl.BlockSpec(memory_space=pl.ANY)
NEG_BIG = -1e30


def _params(n_grid):
    return pltpu.CompilerParams(dimension_semantics=("arbitrary",) * n_grid, vmem_limit_bytes=VMEM_LIMIT_V7X)


def _dot(a, b):
    return jnp.dot(a, b, preferred_element_type=F32)


def _dot_nt(a, b):
    return lax.dot_general(a, b, (((1,), (1,)), ((), ())), preferred_element_type=F32)


def _dot_tn(a, b):
    return lax.dot_general(a, b, (((0,), (0,)), ((), ())), preferred_element_type=F32)


def _split3(x):
    hi = x.astype(BF16)
    r = x - hi.astype(F32)
    mid = r.astype(BF16)
    lo = (r - mid.astype(F32)).astype(BF16)
    return hi, mid, lo


def _dot3_lhs(x, m):
    hi, mid, lo = _split3(x)
    return _dot(hi, m) + _dot(mid, m) + _dot(lo, m)


def _dot3_rhs(m, x):
    hi, mid, lo = _split3(x)
    return _dot(m, hi) + _dot(m, mid) + _dot(m, lo)


def _load_once(pairs, sem):
    @pl.when(pl.program_id(0) == 0)
    def _():
        copies = [pltpu.make_async_copy(src, dst, sem.at[k]) for k, (src, dst) in enumerate(pairs)]
        for cp in copies:
            cp.start()
        for cp in copies:
            cp.wait()


def _rms_fwd(x, g):
    r = lax.rsqrt(jnp.mean(x * x, axis=-1, keepdims=True) + RMS_EPS)
    xhat = x * r
    return xhat, r, xhat * g


def _rms_bwd(dy, xhat, r, g):
    w = dy * g
    dx = r * (w - xhat * jnp.mean(w * xhat, axis=-1, keepdims=True))
    return dx, dy * xhat


def _gelu_parts(a):
    c = math.sqrt(2.0 / math.pi)
    a2 = a * a
    t = jnp.tanh(c * (a + 0.044715 * a * a2))
    z = 0.5 * a * (1.0 + t)
    dz = 0.5 * (1.0 + t) + 0.5 * a * (1.0 - t * t) * (c * (1.0 + 3.0 * 0.044715 * a2))
    return z, dz


def _sigmoid(x):
    return 1.0 / (1.0 + jnp.exp(-x))


def _gmlp_core(z, lng, lnb, wc_ref, bias, n_chunk, gd):
    e = z.shape[1] // 2
    u = z[:, :e]
    v = z[:, e:]
    mu = jnp.mean(v, axis=-1, keepdims=True)
    vc = v - mu
    rstd = lax.rsqrt(jnp.mean(vc * vc, axis=-1, keepdims=True) + LN_EPS)
    vhat = vc * rstd
    vln = vhat * lng + lnb
    vlb = vln.astype(BF16)
    rows = []
    for ci in range(n_chunk):
        cols = []
        for g in range(GM_GROUPS):
            blk = vlb[ci * CHUNK:(ci + 1) * CHUNK, g * gd:(g + 1) * gd]
            cols.append(_dot(wc_ref[g], blk))
        rows.append(jnp.concatenate(cols, axis=1) + bias)
    s = rows[0] if n_chunk == 1 else jnp.concatenate(rows, axis=0)
    return u, vhat, rstd, vlb, s


def _gmlp_fwd(h, g_mix, w_in, lng, lnb, wc, bias, w_out, tm):
    s_len, d = h.shape
    n_p, _, w = w_in.shape
    e = w_out.shape[0]
    gd = e // GM_GROUPS
    n_chunk = tm // CHUNK

    def body(h_ref, g_ref, win_hbm, lng_ref, lnb_ref, wc_ref, bias_ref, wout_hbm,
             hout_ref, z_ref, dz_ref, hn_ref, gated_ref, win_v, wout_v, sem):
        _load_once([(win_hbm, win_v), (wout_hbm, wout_v)], sem)
        x = h_ref[...]
        _, _, y = _rms_fwd(x, g_ref[...])
        hn = y.astype(BF16)
        hn_ref[...] = hn
        for p in range(n_p):
            z_p, dz_p = _gelu_parts(_dot(hn, win_v[p]))
            z_ref[:, p * w:(p + 1) * w] = z_p
            dz_ref[:, p * w:(p + 1) * w] = dz_p.astype(BF16)
        u, _, _, _, s = _gmlp_core(z_ref[...], lng_ref[...], lnb_ref[...], wc_ref, bias_ref[...], n_chunk, gd)
        gated = (u * s).astype(BF16)
        gated_ref[...] = gated
        hout_ref[...] = x + _dot(gated, wout_v[...])

    row = lambda i: (i, 0)
    const2 = lambda i: (0, 0)
    return pl.pallas_call(
        body, name="gmlp_fwd", grid=(s_len // tm,),
        in_specs=[pl.BlockSpec((tm, d), row), pl.BlockSpec((1, d), const2), ANY,
                  pl.BlockSpec((1, e), const2), pl.BlockSpec((1, e), const2),
                  pl.BlockSpec(wc.shape, lambda i: (0, 0, 0)), pl.BlockSpec((CHUNK, e), const2), ANY],
        out_specs=[pl.BlockSpec((tm, d), row), pl.BlockSpec((tm, 2 * e), row), pl.BlockSpec((tm, 2 * e), row),
                   pl.BlockSpec((tm, d), row), pl.BlockSpec((tm, e), row)],
        out_shape=[jax.ShapeDtypeStruct((s_len, d), F32), jax.ShapeDtypeStruct((s_len, 2 * e), F32),
                   jax.ShapeDtypeStruct((s_len, 2 * e), BF16),
                   jax.ShapeDtypeStruct((s_len, d), BF16), jax.ShapeDtypeStruct((s_len, e), BF16)],
        scratch_shapes=[pltpu.VMEM(w_in.shape, BF16), pltpu.VMEM(w_out.shape, BF16), pltpu.SemaphoreType.DMA((2,))],
        compiler_params=_params(1),
    )(h, g_mix, w_in, lng, lnb, wc, bias, w_out)


def _gmlp_bwd(h, dh, z, dz, g_mix, w_in, lng, lnb, wc, wct, bias, w_out, seg, tm):
    s_len, d = h.shape
    n_p, _, w = w_in.shape
    e = w_out.shape[0]
    gd = e // GM_GROUPS
    n_chunk = tm // CHUNK
    n_blk = s_len // tm

    def body(h_ref, dh_ref, z_ref, dz_ref, g_ref, win_hbm, lng_ref, lnb_ref, wc_ref, wct_ref, bias_ref, wout_hbm, seg_ref,
             dhin_ref, da_ref, gws_ref, gbs_ref, glng_ref, glnb_ref, gmix_ref, win_v, wout_v, dsum, sem):
        i = pl.program_id(0)
        _load_once([(win_hbm, win_v), (wout_hbm, wout_v)], sem)

        @pl.when(i == 0)
        def _():
            gws_ref[...] = jnp.zeros_like(gws_ref)
            glng_ref[...] = jnp.zeros_like(glng_ref)
            glnb_ref[...] = jnp.zeros_like(glnb_ref)
            gmix_ref[...] = jnp.zeros_like(gmix_ref)
            dsum[...] = jnp.zeros_like(dsum)

        x = h_ref[...]
        dh_v = dh_ref[...]
        g = g_ref[...]
        lng_v = lng_ref[...]
        xhat, r, _ = _rms_fwd(x, g)
        u, vhat, rstd, vlb, s = _gmlp_core(z_ref[...], lng_v, lnb_ref[...], wc_ref, bias_ref[...], n_chunk, gd)
        dg = _dot_nt(dh_v.astype(BF16), wout_v[...])
        du = dg * s
        ds = dg * u
        dsb = ds.astype(BF16)
        rows = []
        ds_acc = None
        for ci in range(n_chunk):
            lo, hi = ci * CHUNK, (ci + 1) * CHUNK
            cols = []
            for gi in range(GM_GROUPS):
                d_blk = dsb[lo:hi, gi * gd:(gi + 1) * gd]
                gws_ref[gi] += _dot_nt(d_blk, vlb[lo:hi, gi * gd:(gi + 1) * gd])
                cols.append(_dot(wct_ref[gi], d_blk))
            rows.append(jnp.concatenate(cols, axis=1))
            ds_acc = ds[lo:hi] if ds_acc is None else ds_acc + ds[lo:hi]
        dsum[...] += ds_acc
        dvln = rows[0] if n_chunk == 1 else jnp.concatenate(rows, axis=0)
        glng_ref[...] += jnp.sum(dvln * vhat, axis=0, keepdims=True)
        glnb_ref[...] += jnp.sum(dvln, axis=0, keepdims=True)
        dvhat = dvln * lng_v
        dv = rstd * (dvhat - jnp.mean(dvhat, axis=-1, keepdims=True)
                     - vhat * jnp.mean(dvhat * vhat, axis=-1, keepdims=True))
        da = jnp.concatenate([du, dv], axis=1) * dz_ref[...].astype(F32)
        dab = da.astype(BF16)
        da_ref[...] = dab
        dhn = _dot_nt(dab[:, :w], win_v[0])
        for p in range(1, n_p):
            dhn += _dot_nt(dab[:, p * w:(p + 1) * w], win_v[p])
        dx, gg = _rms_bwd(dhn, xhat, r, g)
        gmix_ref[...] += jnp.sum(gg, axis=0, keepdims=True)
        dhin_ref[...] = dh_v + dx

        @pl.when(i == n_blk - 1)
        def _():
            tril = lax.broadcasted_iota(jnp.int32, (CHUNK, CHUNK), 0) >= lax.broadcasted_iota(jnp.int32, (CHUNK, CHUNK), 1)
            for gi in range(GM_GROUPS):
                gws_ref[gi] = jnp.where(tril, gws_ref[gi], 0.0)
            gbs_ref[...] = _dot3_lhs(dsum[...], seg_ref[...])

    row = lambda i: (i, 0)
    const2 = lambda i: (0, 0)
    const3 = lambda i: (0, 0, 0)
    return pl.pallas_call(
        body, name="gmlp_bwd", grid=(n_blk,),
        in_specs=[pl.BlockSpec((tm, d), row), pl.BlockSpec((tm, d), row), pl.BlockSpec((tm, 2 * e), row),
                  pl.BlockSpec((tm, 2 * e), row),
                  pl.BlockSpec((1, d), const2), ANY, pl.BlockSpec((1, e), const2), pl.BlockSpec((1, e), const2),
                  pl.BlockSpec(wc.shape, const3), pl.BlockSpec(wct.shape, const3), pl.BlockSpec((CHUNK, e), const2),
                  ANY, pl.BlockSpec((e, LANES), const2)],
        out_specs=[pl.BlockSpec((tm, d), row), pl.BlockSpec((tm, 2 * e), row), pl.BlockSpec(wc.shape, const3),
                   pl.BlockSpec((CHUNK, LANES), const2), pl.BlockSpec((1, e), const2), pl.BlockSpec((1, e), const2),
                   pl.BlockSpec((1, d), const2)],
        out_shape=[jax.ShapeDtypeStruct((s_len, d), F32), jax.ShapeDtypeStruct((s_len, 2 * e), BF16),
                   jax.ShapeDtypeStruct(wc.shape, F32), jax.ShapeDtypeStruct((CHUNK, LANES), F32),
                   jax.ShapeDtypeStruct((1, e), F32), jax.ShapeDtypeStruct((1, e), F32), jax.ShapeDtypeStruct((1, d), F32)],
        scratch_shapes=[pltpu.VMEM(w_in.shape, BF16), pltpu.VMEM(w_out.shape, BF16), pltpu.VMEM((CHUNK, e), F32),
                        pltpu.SemaphoreType.DMA((2,))],
        compiler_params=_params(1),
    )(h, dh, z, dz, g_mix, w_in, lng, lnb, wc, wct, bias, w_out, seg)


def _shift_down(a, k, fill):
    tm = a.shape[0]
    out = pltpu.roll(a, k, 0)
    rid = lax.broadcasted_iota(jnp.int32, a.shape, 0)
    for j in range(k):
        out = jnp.where(rid == j, fill[8 - k + j:8 - k + j + 1, :], out)
    return out


def _shift_up(a, k, fill):
    tm = a.shape[0]
    out = pltpu.roll(a, tm - k, 0)
    rid = lax.broadcasted_iota(jnp.int32, a.shape, 0)
    for j in range(k):
        out = jnp.where(rid == tm - k + j, fill[j:j + 1, :], out)
    return out


def _ffn_fwd(h, g_norm, wg_all, wu_all, wd_all, layer, conv_w, conv_b, tm):
    s_len, d = h.shape
    n_p = wg_all.shape[0]
    fq = wg_all.shape[3]
    f = n_p * fq

    def body(h_ref, g_ref, wg_hbm, wu_hbm, wd_hbm, cw_ref, cb_ref,
             hout_ref, a_ref, up_ref, hn_ref, hid_ref, wg_v, wu_v, wd_v, carry, sem):
        i = pl.program_id(0)
        _load_once([(wg_hbm.at[:, layer], wg_v), (wu_hbm.at[:, layer], wu_v), (wd_hbm.at[:, layer], wd_v)], sem)

        @pl.when(i == 0)
        def _():
            carry[...] = jnp.zeros_like(carry)

        x = h_ref[...]
        _, _, y = _rms_fwd(x, g_ref[...])
        hn = y.astype(BF16)
        hn_ref[...] = hn
        for p in range(n_p):
            a_ref[:, p * fq:(p + 1) * fq] = _dot(hn, wg_v[p])
            up_ref[:, p * fq:(p + 1) * fq] = _dot(hn, wu_v[p])
        a = a_ref[...]
        prev = carry[...]
        am1 = _shift_down(a, 1, prev)
        am2 = _shift_down(a, 2, prev)
        carry[...] = a[tm - 8:tm, :]
        cw = cw_ref[...]
        ac = cb_ref[...] + am2 * cw[0:1, :]
        ac = ac + am1 * cw[1:2, :]
        ac = ac + a * cw[2:3, :]
        hid = (ac * _sigmoid(ac) * up_ref[...]).astype(BF16)
        hid_ref[...] = hid
        y2 = _dot(hid[:, :fq], wd_v[0])
        for p in range(1, n_p):
            y2 += _dot(hid[:, p * fq:(p + 1) * fq], wd_v[p])
        hout_ref[...] = x + y2

    row = lambda i: (i, 0)
    const2 = lambda i: (0, 0)
    return pl.pallas_call(
        body, name=f"ffn_fwd_{layer}", grid=(s_len // tm,),
        in_specs=[pl.BlockSpec((tm, d), row), pl.BlockSpec((1, d), const2), ANY, ANY, ANY,
                  pl.BlockSpec((8, f), const2), pl.BlockSpec((1, f), const2)],
        out_specs=[pl.BlockSpec((tm, d), row), pl.BlockSpec((tm, f), row), pl.BlockSpec((tm, f), row),
                   pl.BlockSpec((tm, d), row), pl.BlockSpec((tm, f), row)],
        out_shape=[jax.ShapeDtypeStruct((s_len, d), F32), jax.ShapeDtypeStruct((s_len, f), F32),
                   jax.ShapeDtypeStruct((s_len, f), F32), jax.ShapeDtypeStruct((s_len, d), BF16),
                   jax.ShapeDtypeStruct((s_len, f), BF16)],
        scratch_shapes=[pltpu.VMEM((n_p, d, fq), BF16), pltpu.VMEM((n_p, d, fq), BF16), pltpu.VMEM((n_p, fq, d), BF16),
                        pltpu.VMEM((8, f), F32), pltpu.SemaphoreType.DMA((3,))],
        compiler_params=_params(1),
    )(h, g_norm, wg_all, wu_all, wd_all, conv_w, conv_b)


def _ffn_bwd(h, dh, a, up, g_norm, wg_all, wu_all, wd_all, layer, conv_w, conv_b, tm):
    s_len, d = h.shape
    n_p = wg_all.shape[0]
    fq = wg_all.shape[3]
    f = n_p * fq
    n_blk = s_len // tm
    t8 = tm // 8

    def body(h_ref, dh_ref, a_ref, ahalo_ref, up_ref, g_ref, wg_hbm, wu_hbm, wd_hbm, cw_ref, cb_ref,
             dhin_ref, da_ref, dup_ref, gcw_ref, gcb_ref, gn_ref, wg_v, wu_v, wd_v, carry, sem):
        i = pl.program_id(0)
        _load_once([(wg_hbm.at[:, layer], wg_v), (wu_hbm.at[:, layer], wu_v), (wd_hbm.at[:, layer], wd_v)], sem)

        @pl.when(i == 0)
        def _():
            carry[...] = jnp.zeros_like(carry)
            gcw_ref[...] = jnp.zeros_like(gcw_ref)
            gcb_ref[...] = jnp.zeros_like(gcb_ref)
            gn_ref[...] = jnp.zeros_like(gn_ref)

        x = h_ref[...]
        dh_v = dh_ref[...]
        g = g_ref[...]
        xhat, r, _ = _rms_fwd(x, g)
        a = a_ref[...]
        up_v = up_ref[...]
        prev = jnp.where(i == n_blk - 1, 0.0, ahalo_ref[...])
        am1 = _shift_down(a, 1, prev)
        am2 = _shift_down(a, 2, prev)
        cw = cw_ref[...]
        ac = cb_ref[...] + am2 * cw[0:1, :]
        ac = ac + am1 * cw[1:2, :]
        ac = ac + a * cw[2:3, :]
        sg = _sigmoid(ac)
        sil = ac * sg
        dhb = dh_v.astype(BF16)
        dhid = jnp.concatenate([_dot_nt(dhb, wd_v[p]) for p in range(n_p)], axis=1)
        dup = dhid * sil
        dac = dhid * up_v * (sg * (1.0 + ac * (1.0 - sg)))
        gcb_ref[...] += jnp.sum(dac, axis=0, keepdims=True)
        gcw_ref[0:1, :] += jnp.sum(dac * am2, axis=0, keepdims=True)
        gcw_ref[1:2, :] += jnp.sum(dac * am1, axis=0, keepdims=True)
        gcw_ref[2:3, :] += jnp.sum(dac * a, axis=0, keepdims=True)
        nxt = carry[...]
        dp1 = _shift_up(dac, 1, nxt)
        dp2 = _shift_up(dac, 2, nxt)
        carry[...] = dac[0:8, :]
        da = dac * cw[2:3, :] + dp1 * cw[1:2, :] + dp2 * cw[0:1, :]
        dab = da.astype(BF16)
        dupb = dup.astype(BF16)
        da_ref[...] = dab
        dup_ref[...] = dupb
        dhn = _dot_nt(dab[:, :fq], wg_v[0]) + _dot_nt(dupb[:, :fq], wu_v[0])
        for p in range(1, n_p):
            dhn += _dot_nt(dab[:, p * fq:(p + 1) * fq], wg_v[p]) + _dot_nt(dupb[:, p * fq:(p + 1) * fq], wu_v[p])
        dx, gg = _rms_bwd(dhn, xhat, r, g)
        gn_ref[...] += jnp.sum(gg, axis=0, keepdims=True)
        dhin_ref[...] = dh_v + dx

    rev = lambda i: (n_blk - 1 - i, 0)
    halo = lambda i: (jnp.maximum((n_blk - 1 - i) * t8 - 1, 0), 0)
    const2 = lambda i: (0, 0)
    return pl.pallas_call(
        body, name=f"ffn_bwd_{layer}", grid=(n_blk,),
        in_specs=[pl.BlockSpec((tm, d), rev), pl.BlockSpec((tm, d), rev), pl.BlockSpec((tm, f), rev),
                  pl.BlockSpec((8, f), halo), pl.BlockSpec((tm, f), rev), pl.BlockSpec((1, d), const2), ANY, ANY, ANY,
                  pl.BlockSpec((8, f), const2), pl.BlockSpec((1, f), const2)],
        out_specs=[pl.BlockSpec((tm, d), rev), pl.BlockSpec((tm, f), rev), pl.BlockSpec((tm, f), rev),
                   pl.BlockSpec((8, f), const2), pl.BlockSpec((1, f), const2), pl.BlockSpec((1, d), const2)],
        out_shape=[jax.ShapeDtypeStruct((s_len, d), F32), jax.ShapeDtypeStruct((s_len, f), BF16),
                   jax.ShapeDtypeStruct((s_len, f), BF16), jax.ShapeDtypeStruct((8, f), F32),
                   jax.ShapeDtypeStruct((1, f), F32), jax.ShapeDtypeStruct((1, d), F32)],
        scratch_shapes=[pltpu.VMEM((n_p, d, fq), BF16), pltpu.VMEM((n_p, d, fq), BF16), pltpu.VMEM((n_p, fq, d), BF16),
                        pltpu.VMEM((8, f), F32), pltpu.SemaphoreType.DMA((3,))],
        compiler_params=_params(1),
    )(h, dh, a, a, up, g_norm, wg_all, wu_all, wd_all, conv_w, conv_b)


def _even_head_lanes(shape, axis):
    return (lax.broadcasted_iota(jnp.int32, shape, axis) & HEAD_DIM) == 0


def _pair_select(lo, hi, shape):
    return jnp.where(lax.broadcasted_iota(jnp.int32, shape, 1) < HEAD_DIM, lo, hi)


def _causal(row0, col0, shape):
    return row0 + lax.broadcasted_iota(jnp.int32, shape, 0) >= col0 + lax.broadcasted_iota(jnp.int32, shape, 1)


N_SPARE = 3


def _spare_selectors(d, key_side):
    lane = jnp.arange(d)[None, :]
    row = jnp.arange(N_SPARE * LANES)[:, None]
    head, part = row % LANES, row // LANES
    off = N_SPARE if key_side else 0
    sel_a = ((head % 2 == 0) & (lane == LANES * (head // 2) + HEAD_DIM + off + part)).astype(F32)
    sel_b = ((head % 2 == 1) & (lane == LANES * (head // 2) + off + part)).astype(F32)
    sign = -1.0 if key_side else 1.0
    ones_off = 0 if key_side else N_SPARE
    in_pair = jnp.arange(d)[None, :] % LANES
    ones_a = ((in_pair >= HEAD_DIM + ones_off) & (in_pair < HEAD_DIM + ones_off + N_SPARE)).astype(F32)
    ones_b = ((in_pair >= ones_off) & (in_pair < ones_off + N_SPARE)).astype(F32)
    return (sign * sel_a).astype(BF16), (sign * sel_b).astype(BF16), ones_a, ones_b


def _parts(x):
    return jnp.concatenate(_split3(x), axis=1)


def _fox_proj_fwd(h, g_norm, wq, wk, wv, wf, bf, sel_q, sel_k, tm):
    s_len, d = h.shape
    sq_a, sq_b, oq_a, oq_b = sel_q
    sk_a, sk_b, ok_a, ok_b = sel_k

    def body(h_ref, g_ref, wq_hbm, wk_hbm, wv_hbm, wf_ref, bf_ref, sqa_ref, sqb_ref, oqa_ref, oqb_ref,
             ska_ref, skb_ref, oka_ref, okb_ref,
             hn_ref, qa_ref, qb_ref, kat_ref, kbt_ref, va_ref, vb_ref, vat_ref, vbt_ref, z_ref,
             wq_v, wk_v, wv_v, total, sem):
        i = pl.program_id(0)
        _load_once([(wq_hbm, wq_v), (wk_hbm, wk_v), (wv_hbm, wv_v)], sem)

        @pl.when(i == 0)
        def _():
            total[...] = jnp.zeros_like(total)

        x = h_ref[...]
        _, _, y = _rms_fwd(x, g_ref[...])
        hn = y.astype(BF16)
        hn_ref[...] = hn
        z = _dot(hn, wf_ref[...]) + bf_ref[...]
        z_ref[...] = z
        logf = jnp.minimum(z, 0.0) - jnp.log(1.0 + jnp.exp(-jnp.abs(z)))
        tri = (lax.broadcasted_iota(jnp.int32, (tm, tm), 0) >= lax.broadcasted_iota(jnp.int32, (tm, tm), 1))
        cum = _dot3_rhs(jnp.where(tri, 1.0, 0.0).astype(BF16), logf) + total[0:1, :]
        total[...] = jnp.broadcast_to(cum[tm - 1:tm, :], total.shape)
        parts = _parts(cum)

        even = _even_head_lanes((tm, d), 1)
        q = _dot(hn, wq_v[...]) * (HEAD_DIM ** -0.5)
        qa_ref[...] = jnp.where(even, q, _dot(parts, sqa_ref[...]) + oqa_ref[...]).astype(BF16)
        qb_ref[...] = jnp.where(even, _dot(parts, sqb_ref[...]) + oqb_ref[...], q).astype(BF16)
        k = _dot(hn, wk_v[...])
        ka = jnp.where(even, k, _dot(parts, ska_ref[...]) + oka_ref[...])
        kb = jnp.where(even, _dot(parts, skb_ref[...]) + okb_ref[...], k)
        kat_ref[...] = ka.T.astype(BF16)
        kbt_ref[...] = kb.T.astype(BF16)
        v = _dot(hn, wv_v[...])
        va = jnp.where(even, v, oka_ref[...])
        vb = jnp.where(even, okb_ref[...], v)
        va_ref[...] = va.astype(BF16)
        vb_ref[...] = vb.astype(BF16)
        vat_ref[...] = va.T.astype(BF16)
        vbt_ref[...] = vb.T.astype(BF16)

    row = lambda i: (i, 0)
    col = lambda i: (0, i)
    const2 = lambda i: (0, 0)
    sd = jax.ShapeDtypeStruct((s_len, d), BF16)
    ds_ = jax.ShapeDtypeStruct((d, s_len), BF16)
    rs, cs = pl.BlockSpec((tm, d), row), pl.BlockSpec((d, tm), col)
    sel = pl.BlockSpec((N_SPARE * LANES, d), const2)
    one = pl.BlockSpec((1, d), const2)
    return pl.pallas_call(
        body, name="fox_proj_fwd", grid=(s_len // tm,),
        in_specs=[rs, one, ANY, ANY, ANY, pl.BlockSpec((d, LANES), const2), pl.BlockSpec((1, LANES), const2),
                  sel, sel, one, one, sel, sel, one, one],
        out_specs=[rs, rs, rs, cs, cs, rs, rs, cs, cs, pl.BlockSpec((tm, LANES), row)],
        out_shape=[sd, sd, sd, ds_, ds_, sd, sd, ds_, ds_, jax.ShapeDtypeStruct((s_len, LANES), F32)],
        scratch_shapes=[pltpu.VMEM((d, d), BF16), pltpu.VMEM((d, d), BF16), pltpu.VMEM((d, d), BF16),
                        pltpu.VMEM((8, LANES), F32), pltpu.SemaphoreType.DMA((3,))],
        compiler_params=_params(1),
    )(h, g_norm, wq, wk, wv, wf, bf, sq_a, sq_b, oq_a, oq_b, sk_a, sk_b, ok_a, ok_b)


def _spare_cols(x, base):
    xf = x[:, base:base + N_SPARE].astype(F32)
    return xf[:, 0:1] + xf[:, 1:2] + xf[:, 2:3]


def _with_query_term(x, term, base):
    lane = lax.broadcasted_iota(jnp.int32, x.shape, 1)
    hi, mid, lo = _split3(term)
    out = jnp.where(lane == base, hi.astype(F32), x)
    out = jnp.where(lane == base + 1, mid.astype(F32), out)
    out = jnp.where(lane == base + 2, lo.astype(F32), out)
    return jnp.where((lane >= base + N_SPARE) & (lane < base + 2 * N_SPARE), 1.0, out)


def _flash_fwd(qa, qb, kat, kbt, va, vb):
    s_len, d = qa.shape
    sub = ATT_BLOCK
    n_sub = 2 if s_len % (2 * sub) == 0 else 1
    t = n_sub * sub
    w = min(ATT_CHUNK, s_len)
    n_pair = d // LANES
    n_q = s_len // t
    bases = (HEAD_DIM, 0)
    chains = [(r, hh) for r in range(n_sub) for hh in range(2)]

    def body(qa_ref, qb_ref, kat_ref, kbt_ref, va_ref, vb_ref, o_ref, qa2_ref, qb2_ref, qat2_ref, qbt2_ref):
        i = pl.program_id(1)
        q_refs = (qa_ref, qb_ref)
        qs = [q_refs[hh][r * sub:(r + 1) * sub, :] for r, hh in chains]
        kts = (kat_ref, kbt_ref)
        vs = (va_ref, vb_ref)

        def step(kb, carry, masked, width=w):
            off = pl.multiple_of(kb * w, w)
            cols = pl.ds(off, width)
            scores = [_dot(qs[c], kts[hh][:, cols]) for c, (r, hh) in enumerate(chains)]
            probs, stats = [], []
            for c, (r, hh) in enumerate(chains):
                m, _ = carry[c]
                s = scores[c]
                if masked:
                    s = jnp.where(_causal(i * t + r * sub, off, (sub, width)), s, NEG_BIG)
                m_new = jnp.maximum(m, jnp.max(s, axis=1, keepdims=True))
                probs.append(jnp.exp(s - m_new).astype(BF16))
                stats.append((m_new, jnp.exp(m - m_new)))
            return tuple((stats[c][0], carry[c][1] * stats[c][1] + _dot(probs[c], vs[hh][cols, :]))
                         for c, (r, hh) in enumerate(chains))

        init = ((jnp.full((sub, 1), NEG_BIG, F32), jnp.zeros((sub, LANES), F32)),) * len(chains)
        diag = (i * t) // w
        carry = lax.fori_loop(0, diag, lambda kb, c: step(kb, c, False), init)
        carry = step(diag, carry, True)
        for r in range(n_sub):
            outs, q2 = [], []
            for hh in range(2):
                m, acc = carry[2 * r + hh]
                l = acc[:, bases[hh]:bases[hh] + 1]
                outs.append(acc / l)
                term = _spare_cols(qs[2 * r + hh], bases[hh]) - (m + jnp.log(l))
                q2.append(_with_query_term(qs[2 * r + hh].astype(F32), term, bases[hh]))
            rows = slice(r * sub, (r + 1) * sub)
            o_ref[rows, :] = _pair_select(outs[0], outs[1], (sub, LANES))
            qa2_ref[rows, :] = q2[0].astype(BF16)
            qb2_ref[rows, :] = q2[1].astype(BF16)
            qat2_ref[:, rows] = q2[0].T.astype(BF16)
            qbt2_ref[:, rows] = q2[1].T.astype(BF16)

    qblk = pl.BlockSpec((t, LANES), lambda j, i: (i, j))
    qblk_t = pl.BlockSpec((LANES, t), lambda j, i: (j, i))
    whole_t = pl.BlockSpec((LANES, s_len), lambda j, i: (j, 0))
    whole = pl.BlockSpec((s_len, LANES), lambda j, i: (0, j))
    sd = jax.ShapeDtypeStruct((s_len, d), BF16)
    ds_ = jax.ShapeDtypeStruct((d, s_len), BF16)
    return pl.pallas_call(
        body, name="flash_fwd", grid=(n_pair, n_q),
        in_specs=[qblk, qblk, whole_t, whole_t, whole, whole],
        out_specs=[qblk, qblk, qblk, qblk_t, qblk_t],
        out_shape=[jax.ShapeDtypeStruct((s_len, d), F32), sd, sd, ds_, ds_],
        compiler_params=_params(2),
    )(qa, qb, kat, kbt, va, vb)


def _flash_bwd(qa, qb, qat, qbt, kat, kbt, vat, vbt, doa, dob, doat, dobt):
    s_len, d = qa.shape
    sub = ATT_BLOCK
    n_sub = 2 if s_len % (2 * sub) == 0 else 1
    t = n_sub * sub
    w = min(ATT_CHUNK, s_len)
    n_pair = d // LANES
    n_q = s_len // t
    hd = HEAD_DIM
    chains = [(r, hh) for r in range(n_sub) for hh in range(2)]

    def body(qa_ref, qb_ref, qat_ref, qbt_ref, kat_hbm, kbt_hbm, vat_hbm, vbt_hbm, doa_ref, dob_ref, doat_ref, dobt_ref,
             dqt_ref, dkt_ref, dvt_ref, rs_ref, cs_ref,
             kat_v, kbt_v, vat_v, vbt_v, dkt_acc, dvt_acc, cs_acc, sem):
        j = pl.program_id(0)
        i = pl.program_id(1)

        @pl.when(i == 0)
        def _():
            rows = pl.ds(pl.multiple_of(j * LANES, LANES), LANES)
            copies = [pltpu.make_async_copy(src.at[rows, :], dst, sem.at[n]) for n, (src, dst) in enumerate([
                (kat_hbm, kat_v), (kbt_hbm, kbt_v), (vat_hbm, vat_v), (vbt_hbm, vbt_v)])]
            for cp in copies:
                cp.start()
            dkt_acc[...] = jnp.zeros_like(dkt_acc)
            dvt_acc[...] = jnp.zeros_like(dvt_acc)
            cs_acc[...] = jnp.zeros_like(cs_acc)
            for cp in copies:
                cp.wait()

        q_refs, do_refs = (qa_ref, qb_ref), (doa_ref, dob_ref)
        qs = [q_refs[hh][r * sub:(r + 1) * sub, :] for r, hh in chains]
        dos = [do_refs[hh][r * sub:(r + 1) * sub, :] for r, hh in chains]
        own = (slice(0, hd), slice(hd, 2 * hd))
        spare = (slice(hd, hd + 8), slice(0, 8))
        used = (slice(0, hd + 16), slice(0, 2 * hd))
        qts = (qat_ref[used[0], :], qbt_ref[used[1], :])
        dots = (doat_ref[own[0], :], dobt_ref[own[1], :])
        kts, vts = (kat_v, kbt_v), (vat_v, vbt_v)

        def step(kb, carry, masked, width=w):
            off = pl.multiple_of(kb * w, w)
            cols = pl.ds(off, width)
            scores = [_dot(qs[c], kts[hh][:, cols]) for c, (r, hh) in enumerate(chains)]
            dps = [_dot(dos[c], vts[hh][:, cols]) for c, (r, hh) in enumerate(chains)]
            ps, dss = [], []
            for c, (r, hh) in enumerate(chains):
                s = scores[c]
                if masked:
                    s = jnp.where(_causal(i * t + r * sub, off, (sub, width)), s, NEG_BIG)
                p = jnp.exp(s)
                dss.append((p * dps[c]).astype(BF16))
                ps.append(p.astype(BF16))
            out = tuple(carry[c] + _dot_nt(kts[hh][used[hh], cols], dss[c]) for c, (r, hh) in enumerate(chains))
            for hh in range(2):
                p_all = jnp.concatenate([ps[2 * r + hh] for r in range(n_sub)], axis=0)
                ds_all = jnp.concatenate([dss[2 * r + hh] for r in range(n_sub)], axis=0)
                dvt_acc[own[hh], cols] += _dot(dots[hh], p_all)
                with_sums = _dot(qts[hh], ds_all)
                dkt_acc[own[hh], cols] += with_sums[own[hh], :]
                cs_acc[8 * hh:8 * hh + 8, cols] += with_sums[spare[hh], :]
            return out

        diag = (i * t) // w
        init = (jnp.zeros((hd + 16, sub), F32), jnp.zeros((2 * hd, sub), F32)) * n_sub
        carry = lax.fori_loop(0, diag, lambda kb, c: step(kb, c, False), init)
        carry = lax.cond((i * t) % w + t <= w // 2,
                         lambda c: step(diag, c, True, w // 2), lambda c: step(diag, c, True), carry)
        for c, (r, hh) in enumerate(chains):
            at = slice(r * sub, (r + 1) * sub)
            dqt_ref[own[hh], at] = (carry[c][own[hh], :] * (hd ** -0.5)).astype(BF16)
            rs_ref[0, 8 * hh:8 * hh + 8, at] = carry[c][spare[hh], :]

        @pl.when(i == n_q - 1)
        def _():
            dkt_ref[...] = dkt_acc[...].astype(BF16)
            dvt_ref[...] = dvt_acc[...].astype(BF16)
            cs_ref[0] = cs_acc[...]

    qblk = pl.BlockSpec((t, LANES), lambda j, i: (i, j))
    qblk_t = pl.BlockSpec((LANES, t), lambda j, i: (j, i))
    whole_t = pl.BlockSpec((LANES, s_len), lambda j, i: (j, 0))
    ds_ = jax.ShapeDtypeStruct((d, s_len), BF16)
    sums = jax.ShapeDtypeStruct((n_pair, 16, s_len), F32)
    return pl.pallas_call(
        body, name="flash_bwd", grid=(n_pair, n_q),
        in_specs=[qblk, qblk, qblk_t, qblk_t, ANY, ANY, ANY, ANY, qblk, qblk, qblk_t, qblk_t],
        out_specs=[qblk_t, whole_t, whole_t, pl.BlockSpec((1, 16, t), lambda j, i: (j, 0, i)),
                   pl.BlockSpec((1, 16, s_len), lambda j, i: (j, 0, 0))],
        out_shape=[ds_, ds_, ds_, sums, sums],
        scratch_shapes=[pltpu.VMEM((LANES, s_len), BF16), pltpu.VMEM((LANES, s_len), BF16),
                        pltpu.VMEM((LANES, s_len), BF16), pltpu.VMEM((LANES, s_len), BF16),
                        pltpu.VMEM((LANES, s_len), F32), pltpu.VMEM((LANES, s_len), F32),
                        pltpu.VMEM((16, s_len), F32), pltpu.SemaphoreType.DMA((4,))],
        compiler_params=_params(2),
    )(qa, qb, qat, qbt, kat, kbt, vat, vbt, doa, dob, doat, dobt)


def _wgrad_t(at, b, name):
    k, s_len = at.shape
    n = b.shape[1]
    tn, tk, ts = min(n, 1024), min(k, 1024), min(s_len, 2048)

    def body(a_ref, b_ref, o_ref):
        @pl.when(pl.program_id(2) == 0)
        def _():
            o_ref[...] = jnp.zeros_like(o_ref)
        o_ref[...] += _dot(a_ref[...].astype(BF16), b_ref[...].astype(BF16))

    return pl.pallas_call(
        body, name=name, grid=(k // tk, n // tn, s_len // ts),
        in_specs=[pl.BlockSpec((tk, ts), lambda a, b_, c: (a, c)), pl.BlockSpec((ts, tn), lambda a, b_, c: (c, b_))],
        out_specs=pl.BlockSpec((tk, tn), lambda a, b_, c: (a, b_)),
        out_shape=jax.ShapeDtypeStruct((k, n), F32),
        compiler_params=_params(3),
    )(at, b)


def _oproj_bwd(dh, o, wo, seg, sel_q, tm):
    s_len, d = dh.shape
    sq_a, sq_b, _, _ = sel_q

    def body(dh_ref, o_ref, wo_hbm, seg_ref, sqa_ref, sqb_ref, doa_ref, dob_ref, doat_ref, dobt_ref, wo_v, sem):
        _load_once([(wo_hbm, wo_v)], sem)
        do = _dot_nt(dh_ref[...].astype(BF16), wo_v[...])
        parts = _parts(-_dot3_lhs(do * o_ref[...], seg_ref[...]))
        even = _even_head_lanes((tm, d), 1)
        doa = jnp.where(even, do, _dot(parts, sqa_ref[...]))
        dob = jnp.where(even, _dot(parts, sqb_ref[...]), do)
        doa_ref[...] = doa.astype(BF16)
        dob_ref[...] = dob.astype(BF16)
        doat_ref[...] = doa.T.astype(BF16)
        dobt_ref[...] = dob.T.astype(BF16)

    row = lambda i: (i, 0)
    const2 = lambda i: (0, 0)
    rs, cs = pl.BlockSpec((tm, d), row), pl.BlockSpec((d, tm), lambda i: (0, i))
    sel = pl.BlockSpec((N_SPARE * LANES, d), const2)
    sd = jax.ShapeDtypeStruct((s_len, d), BF16)
    ds_ = jax.ShapeDtypeStruct((d, s_len), BF16)
    return pl.pallas_call(
        body, name="oproj_bwd", grid=(s_len // tm,),
        in_specs=[rs, rs, ANY, pl.BlockSpec((d, LANES), const2), sel, sel],
        out_specs=[rs, rs, cs, cs], out_shape=[sd, sd, ds_, ds_],
        scratch_shapes=[pltpu.VMEM((d, d), BF16), pltpu.SemaphoreType.DMA((1,))],
        compiler_params=_params(1),
    )(dh, o, wo, seg, sq_a, sq_b)


def _oproj_fwd(h, o, wo, tm):
    s_len, d = h.shape

    def body(h_ref, o_ref, wo_hbm, hout_ref, wo_v, sem):
        _load_once([(wo_hbm, wo_v)], sem)
        hout_ref[...] = h_ref[...] + _dot(o_ref[...].astype(BF16), wo_v[...])

    row = lambda i: (i, 0)
    return pl.pallas_call(
        body, name="oproj_fwd", grid=(s_len // tm,),
        in_specs=[pl.BlockSpec((tm, d), row), pl.BlockSpec((tm, d), row), ANY],
        out_specs=pl.BlockSpec((tm, d), row),
        out_shape=jax.ShapeDtypeStruct((s_len, d), F32),
        scratch_shapes=[pltpu.VMEM((d, d), BF16), pltpu.SemaphoreType.DMA((1,))],
        compiler_params=_params(1),
    )(h, o, wo)


def _forget_bwd(dcum, z, tm):
    s_len = dcum.shape[0]
    n_blk = s_len // tm

    def body(dc_ref, z_ref, dfl_ref, gb_ref, total):
        i = pl.program_id(0)

        @pl.when(i == 0)
        def _():
            total[...] = jnp.zeros_like(total)
            gb_ref[...] = jnp.zeros_like(gb_ref)

        upper = (lax.broadcasted_iota(jnp.int32, (tm, tm), 0) <= lax.broadcasted_iota(jnp.int32, (tm, tm), 1))
        suffix = _dot3_rhs(jnp.where(upper, 1.0, 0.0).astype(BF16), dc_ref[...]) + total[0:1, :]
        total[...] = jnp.broadcast_to(suffix[0:1, :], total.shape)
        dfl = suffix * _sigmoid(-z_ref[...])
        dfl_ref[...] = dfl
        gb_ref[...] += jnp.sum(dfl, axis=0, keepdims=True)

    rev = lambda i: (n_blk - 1 - i, 0)
    return pl.pallas_call(
        body, name="forget_bwd", grid=(n_blk,),
        in_specs=[pl.BlockSpec((tm, LANES), rev), pl.BlockSpec((tm, LANES), rev)],
        out_specs=[pl.BlockSpec((tm, LANES), rev), pl.BlockSpec((1, LANES), lambda i: (0, 0))],
        out_shape=[jax.ShapeDtypeStruct((s_len, LANES), F32), jax.ShapeDtypeStruct((1, LANES), F32)],
        scratch_shapes=[pltpu.VMEM((8, LANES), F32)],
        compiler_params=_params(1),
    )(dcum, z)


def _fox_proj_bwd(h, dh, dqt, dkt, dvt, dfl, g_norm, wq, wk, wv, wf, tm):
    s_len, d = h.shape

    def body(h_ref, dh_ref, dqt_ref, dkt_ref, dvt_ref, dfl_ref, g_ref, wq_hbm, wk_hbm, wv_hbm, wf_ref,
             dhin_ref, dflb_ref, gn_ref, wq_v, wk_v, wv_v, sem):
        _load_once([(wq_hbm, wq_v), (wk_hbm, wk_v), (wv_hbm, wv_v)], sem)

        @pl.when(pl.program_id(0) == 0)
        def _():
            gn_ref[...] = jnp.zeros_like(gn_ref)

        g = g_ref[...]
        xhat, r, _ = _rms_fwd(h_ref[...], g)
        dflb = dfl_ref[...].astype(BF16)
        dflb_ref[...] = dflb
        from_qkv = (_dot(wq_v[...], dqt_ref[...]) + _dot(wk_v[...], dkt_ref[...])
                    + _dot(wv_v[...], dvt_ref[...]))
        dhn = _dot_nt(dflb, wf_ref[...]) + from_qkv.T
        dx, gg = _rms_bwd(dhn, xhat, r, g)
        gn_ref[...] += jnp.sum(gg, axis=0, keepdims=True)
        dhin_ref[...] = dh_ref[...] + dx

    row = lambda i: (i, 0)
    const2 = lambda i: (0, 0)
    rs = pl.BlockSpec((tm, d), row)
    cs = pl.BlockSpec((d, tm), lambda i: (0, i))
    return pl.pallas_call(
        body, name="fox_proj_bwd", grid=(s_len // tm,),
        in_specs=[rs, rs, cs, cs, cs, pl.BlockSpec((tm, LANES), row), pl.BlockSpec((1, d), const2), ANY, ANY, ANY,
                  pl.BlockSpec((d, LANES), const2)],
        out_specs=[rs, pl.BlockSpec((tm, LANES), row), pl.BlockSpec((1, d), const2)],
        out_shape=[jax.ShapeDtypeStruct((s_len, d), F32), jax.ShapeDtypeStruct((s_len, LANES), BF16),
                   jax.ShapeDtypeStruct((1, d), F32)],
        scratch_shapes=[pltpu.VMEM((d, d), BF16), pltpu.VMEM((d, d), BF16), pltpu.VMEM((d, d), BF16),
                        pltpu.SemaphoreType.DMA((3,))],
        compiler_params=_params(1),
    )(h, dh, dqt, dkt, dvt, dfl, g_norm, wq, wk, wv, wf)


def _loss_head(h, target, g_final, tm):
    s_len, d = h.shape
    n_blk = s_len // tm

    def body(h_ref, t_ref, g_ref, dh_ref, loss_ref, gg_ref, sq):
        i = pl.program_id(0)

        @pl.when(i == 0)
        def _():
            sq[...] = jnp.zeros_like(sq)
            gg_ref[...] = jnp.zeros_like(gg_ref)

        g = g_ref[...]
        xhat, r, y = _rms_fwd(h_ref[...], g)
        err = y - t_ref[...]
        sq[...] += jnp.sum(err * err, axis=0, keepdims=True)
        dx, gg = _rms_bwd(err * (1.0 / d), xhat, r, g)
        gg_ref[...] += jnp.sum(gg, axis=0, keepdims=True)
        dh_ref[...] = dx

        @pl.when(i == n_blk - 1)
        def _():
            loss_ref[...] = jnp.broadcast_to(jnp.sum(sq[...], axis=1, keepdims=True) * (0.5 / d), loss_ref.shape)

    row = lambda i: (i, 0)
    const2 = lambda i: (0, 0)
    return pl.pallas_call(
        body, name="loss_head", grid=(n_blk,),
        in_specs=[pl.BlockSpec((tm, d), row), pl.BlockSpec((tm, d), row), pl.BlockSpec((1, d), const2)],
        out_specs=[pl.BlockSpec((tm, d), row), pl.BlockSpec((1, LANES), const2), pl.BlockSpec((1, d), const2)],
        out_shape=[jax.ShapeDtypeStruct((s_len, d), F32), jax.ShapeDtypeStruct((1, LANES), F32),
                   jax.ShapeDtypeStruct((1, d), F32)],
        scratch_shapes=[pltpu.VMEM((1, d), F32)],
        compiler_params=_params(1),
    )(h, target, g_final)


def _wgrad(x, dy, n_piece, name):
    s_len, k = x.shape
    n = dy.shape[1]
    tn = min(n // n_piece, 1024)
    tk = min(k, 1024)
    ts = min(s_len, 2048)
    per_piece = (n // n_piece) // tn

    def body(x_ref, dy_ref, o_ref):
        @pl.when(pl.program_id(2) == 0)
        def _():
            o_ref[...] = jnp.zeros_like(o_ref)
        o_ref[0] += _dot_tn(x_ref[...].astype(BF16), dy_ref[...].astype(BF16))

    return pl.pallas_call(
        body, name=name, grid=(k // tk, n // tn, s_len // ts),
        in_specs=[pl.BlockSpec((ts, tk), lambda a, b, c: (c, a)), pl.BlockSpec((ts, tn), lambda a, b, c: (c, b))],
        out_specs=pl.BlockSpec((1, tk, tn), lambda a, b, c: (b // per_piece, a, b % per_piece)),
        out_shape=jax.ShapeDtypeStruct((n_piece, k, n // n_piece), F32),
        compiler_params=_params(3),
    )(x, dy)


def _pair_sum(g, recv, core, name):
    n_piece, rows, c = g.shape
    half = rows // 2
    tr = min(half, 512)
    nb = half // tr

    def body(core_ref, g_ref, r_ref, o_ref, ob_ref):
        total = g_ref[...] + r_ref[...]
        o_ref[...] = total
        ob_ref[...] = total.astype(BF16)

    blk = pl.BlockSpec((1, tr, c), lambda p, i, core_ref: (p, i, 0))
    return pl.pallas_call(
        body, name=name,
        out_shape=[jax.ShapeDtypeStruct((n_piece, half, c), F32), jax.ShapeDtypeStruct((n_piece, half, c), BF16)],
        grid_spec=pltpu.PrefetchScalarGridSpec(
            num_scalar_prefetch=1, grid=(n_piece, nb),
            in_specs=[pl.BlockSpec((1, tr, c), lambda p, i, core_ref: (p, core_ref[0] * nb + i, 0)), blk],
            out_specs=[blk, blk]),
        compiler_params=_params(2),
    )(core, g, recv)


def _chip_sum(halves, recv, chip, name):
    _, h, c = halves.shape
    tr = min(h, 512)

    def body(chip_ref, own_ref, r_ref, o_ref):
        o_ref[...] = ((own_ref[0] + r_ref[0].astype(F32)) + r_ref[1].astype(F32)) + r_ref[2].astype(F32)

    return pl.pallas_call(
        body, name=name, out_shape=jax.ShapeDtypeStruct((h, c), F32),
        grid_spec=pltpu.PrefetchScalarGridSpec(
            num_scalar_prefetch=1, grid=(h // tr,),
            in_specs=[pl.BlockSpec((1, tr, c), lambda i, chip_ref: (chip_ref[0], i, 0)),
                      pl.BlockSpec((3, tr, c), lambda i, chip_ref: (0, i, 0))],
            out_specs=pl.BlockSpec((tr, c), lambda i, chip_ref: (i, 0))),
        compiler_params=_params(1),
    )(chip, halves, recv)


def _adamw_math(w, m, v, g):
    m_new = ADAM_B1 * m + (1.0 - ADAM_B1) * g
    v_new = ADAM_B2 * v + (1.0 - ADAM_B2) * (g * g)
    m_hat = m_new / (1.0 - ADAM_B1 ** ADAM_STEP)
    v_hat = v_new / (1.0 - ADAM_B2 ** ADAM_STEP)
    return -ADAM_LR * (m_hat / (jnp.sqrt(v_hat) + ADAM_EPS) + ADAM_WD * w), m_new, v_new


def _adamw(w, m, v, g, name):
    rows, c = w.shape
    tr = min(rows, 256)

    def body(w_ref, m_ref, v_ref, g_ref, d_ref, mo_ref, vo_ref):
        d_ref[...], mo_ref[...], vo_ref[...] = _adamw_math(w_ref[...], m_ref[...], v_ref[...], g_ref[...])

    spec = pl.BlockSpec((tr, c), lambda i: (i, 0))
    shape = jax.ShapeDtypeStruct((rows, c), F32)
    return pl.pallas_call(
        body, name=name, grid=(rows // tr,),
        in_specs=[spec] * 4, out_specs=[spec] * 3, out_shape=[shape] * 3,
        compiler_params=_params(1),
    )(w, m, v, g)


def _adamw_halves(w, m, v, g_own, g_other, core, name):
    rows, c = w.shape
    half = rows // 2
    tr = min(half, 256)
    nb = half // tr

    def body(core_ref, w_ref, m_ref, v_ref, own_ref, other_ref, g_ref, d_ref, mo_ref, vo_ref):
        mine = (pl.program_id(0) // nb) == core_ref[0]
        g = jnp.where(mine, own_ref[...], other_ref[...])
        g_ref[...] = g
        d_ref[...], mo_ref[...], vo_ref[...] = _adamw_math(w_ref[...], m_ref[...], v_ref[...], g)

    spec = pl.BlockSpec((tr, c), lambda i, core_ref: (i, 0))
    own = pl.BlockSpec((tr, c), lambda i, core_ref: (jnp.clip(i - core_ref[0] * nb, 0, nb - 1), 0))
    other = pl.BlockSpec((tr, c), lambda i, core_ref: (jnp.clip(i - (1 - core_ref[0]) * nb, 0, nb - 1), 0))
    shape = jax.ShapeDtypeStruct((rows, c), F32)
    return pl.pallas_call(
        body, name=name, out_shape=[shape] * 4,
        grid_spec=pltpu.PrefetchScalarGridSpec(
            num_scalar_prefetch=1, grid=(rows // tr,),
            in_specs=[spec, spec, spec, own, other], out_specs=[spec] * 4),
        compiler_params=_params(1),
    )(core, w, m, v, g_own, g_other)


def _place():
    x, y, c = lax.axis_index("x"), lax.axis_index("y"), lax.axis_index("c")
    chips = [(1 - x, y), (x, 1 - y), (1 - x, 1 - y)]
    return x, y, c, chips


def _half_of(ref, half, which):
    start = which * half
    if half % 8 == 0:
        start = pl.multiple_of(start, 8)
    return ref.at[pl.ds(start, half)]


def _gather_copies(ins, outs, send_sems, recv_sems):
    x, y, c, chips = _place()
    mine = 2 * x + y
    copies = []
    for k, (src, dst) in enumerate(zip(ins, outs)):
        half = src.shape[0] // 2
        copies.append(pltpu.make_async_remote_copy(
            src_ref=src, dst_ref=dst.at[mine], send_sem=send_sems.at[4 * k + 3], recv_sem=recv_sems.at[4 * k + 3],
            device_id=(x, y, 1 - c), device_id_type=MESH))
        for j, (tx, ty) in enumerate(chips):
            copies.append(pltpu.make_async_remote_copy(
                src_ref=_half_of(src, half, c), dst_ref=_half_of(dst.at[mine], half, c),
                send_sem=send_sems.at[4 * k + j], recv_sem=recv_sems.at[4 * k + j],
                device_id=(tx, ty, c), device_id_type=MESH))
    return copies


def _gather_ici(shards):
    n = len(shards)

    def body(*refs):
        copies = _gather_copies(refs[:n], refs[n:2 * n], refs[2 * n], refs[2 * n + 1])
        for cp in copies:
            cp.start()
        for cp in copies:
            cp.wait()

    return pl.pallas_call(
        body, name="weights_gather_ici",
        in_specs=[ANY] * n, out_specs=[ANY] * n,
        out_shape=[jax.ShapeDtypeStruct((4,) + s.shape, s.dtype) for s in shards],
        scratch_shapes=[pltpu.SemaphoreType.DMA((4 * n,)), pltpu.SemaphoreType.DMA((4 * n,))],
    )(*shards)


HBM_SPEC = pl.BlockSpec(memory_space=pltpu.HBM)
SEM_SPEC = pl.BlockSpec(memory_space=pltpu.SEMAPHORE)
IN_FLIGHT = pltpu.SideEffectType.DATAFLOW_SIDE_EFFECTING


def _gather_ici_start(shards, after):
    n = len(shards)

    def body(*refs):
        for cp in _gather_copies(refs[:n], refs[n:2 * n], refs[2 * n + 1], refs[2 * n + 2]):
            cp.start()
        token = refs[-1]
        token[...] = jnp.zeros_like(token)

    lands = [lax.empty((4,) + s.shape, s.dtype) for s in shards]
    out = pl.pallas_call(
        body, name="weights_gather_start",
        out_shape=(pltpu.SemaphoreType.DMA((4 * n,)), pltpu.SemaphoreType.DMA((4 * n,)),
                   *[pltpu.HBM(s.shape, s.dtype) for s in shards], *[pltpu.HBM(l.shape, l.dtype) for l in lands],
                   jax.ShapeDtypeStruct((8, LANES), F32)),
        in_specs=[HBM_SPEC] * (2 * n) + [ANY],
        out_specs=(SEM_SPEC, SEM_SPEC, *[HBM_SPEC] * (2 * n), pl.BlockSpec(memory_space=pltpu.VMEM)),
        input_output_aliases={k: 2 + k for k in range(2 * n)},
        compiler_params=pltpu.CompilerParams(has_side_effects=IN_FLIGHT),
    )(*[pltpu.with_memory_space_constraint(a, pltpu.HBM) for a in list(shards) + lands], after)
    return out[0], out[1], out[2:2 + n], out[2 + n:2 + 2 * n], out[-1]


def _gather_ici_wait(send_sems, recv_sems, sources, lands, after):
    n = len(sources)

    def body(*refs):
        for cp in _gather_copies(refs[:n], refs[n:2 * n], refs[2 * n], refs[2 * n + 1]):
            cp.wait_send()
            cp.wait_recv()

    out = pl.pallas_call(
        body, name="weights_gather_wait",
        out_shape=[pltpu.HBM(a.shape, a.dtype) for a in list(sources) + list(lands)],
        in_specs=[HBM_SPEC] * (2 * n) + [SEM_SPEC, SEM_SPEC, ANY], out_specs=[HBM_SPEC] * (2 * n),
        input_output_aliases={k: k for k in range(2 * n)},
        compiler_params=pltpu.CompilerParams(has_side_effects=IN_FLIGHT),
    )(*sources, *lands, send_sems, recv_sems, after)
    return out[n:]


def _gather_pair(gathered):
    n = len(gathered)

    def body(*refs):
        outs = refs[n:2 * n]
        send_sems, recv_sems = refs[2 * n:]
        x, y, c, chips = _place()
        sends = []
        for k in range(n):
            half = gathered[k].shape[1] // 2
            for j, (tx, ty) in enumerate(chips):
                piece = _half_of(outs[k].at[2 * tx + ty], half, c)
                sends.append(pltpu.make_async_remote_copy(
                    src_ref=piece, dst_ref=piece, send_sem=send_sems.at[k, j], recv_sem=recv_sems.at[k, j],
                    device_id=(x, y, 1 - c), device_id_type=MESH))
        for cp in sends:
            cp.start()
        for cp in sends:
            cp.wait()

    return pl.pallas_call(
        body, name="weights_gather_pair",
        in_specs=[ANY] * n, out_specs=[ANY] * n,
        out_shape=[jax.ShapeDtypeStruct(g.shape, g.dtype) for g in gathered],
        input_output_aliases={k: k for k in range(n)},
        scratch_shapes=[pltpu.SemaphoreType.DMA((n, 3)), pltpu.SemaphoreType.DMA((n, 3))],
    )(*gathered)


def _pair_exchange(grads, tag):
    n = len(grads)

    def body(*refs):
        ins, outs = refs[:n], refs[n:2 * n]
        send_sems, recv_sems = refs[2 * n:]
        x, y, c, _ = _place()
        copies = []
        for k in range(n):
            half = grads[k].shape[1] // 2
            other = ins[k].at[:, pl.ds(pl.multiple_of((1 - c) * half, 8), half), :]
            copies.append(pltpu.make_async_remote_copy(
                src_ref=other, dst_ref=outs[k], send_sem=send_sems.at[k], recv_sem=recv_sems.at[k],
                device_id=(x, y, 1 - c), device_id_type=MESH))
        for cp in copies:
            cp.start()
        for cp in copies:
            cp.wait()

    return pl.pallas_call(
        body, name=f"grads_pair_exchange_{tag}",
        in_specs=[ANY] * n, out_specs=[ANY] * n,
        out_shape=[jax.ShapeDtypeStruct((4, g.shape[1] // 2, g.shape[2]), F32) for g in grads],
        scratch_shapes=[pltpu.SemaphoreType.DMA((n,)), pltpu.SemaphoreType.DMA((n,))],
    )(*grads)


def _scatter_copies(ins, outs, send_sems, recv_sems):
    x, y, c, chips = _place()
    return [pltpu.make_async_remote_copy(
        src_ref=ins[k].at[2 * tx + ty], dst_ref=outs[k].at[j], send_sem=send_sems.at[3 * k + j],
        recv_sem=recv_sems.at[3 * k + j], device_id=(tx, ty, c), device_id_type=MESH)
        for k in range(len(ins)) for j, (tx, ty) in enumerate(chips)]


def _chip_scatter_start(halves, tag, after=()):
    n = len(halves)
    n_in = 2 * n + len(after)

    def body(*refs):
        for cp in _scatter_copies(refs[:n], refs[n:2 * n], refs[n_in], refs[n_in + 1]):
            cp.start()
        token = refs[-1]
        token[...] = jnp.zeros_like(token)

    lands = [lax.empty((3,) + hv.shape[1:], hv.dtype) for hv in halves]
    out = pl.pallas_call(
        body, name=f"grads_chip_scatter_start_{tag}",
        out_shape=(pltpu.SemaphoreType.DMA((3 * n,)), pltpu.SemaphoreType.DMA((3 * n,)),
                   *[pltpu.HBM(a.shape, a.dtype) for a in list(halves) + lands], jax.ShapeDtypeStruct((8, LANES), F32)),
        in_specs=[HBM_SPEC] * (2 * n) + [ANY] * len(after),
        out_specs=(SEM_SPEC, SEM_SPEC, *[HBM_SPEC] * (2 * n), pl.BlockSpec(memory_space=pltpu.VMEM)),
        input_output_aliases={k: 2 + k for k in range(2 * n)},
        compiler_params=pltpu.CompilerParams(has_side_effects=IN_FLIGHT),
    )(*[pltpu.with_memory_space_constraint(a, pltpu.HBM) for a in list(halves) + lands], *after)
    return out[0], out[1], out[2:2 + n], out[2 + n:2 + 2 * n], out[-1]


def _chip_scatter_wait(send_sems, recv_sems, sources, lands, after, tag):
    n = len(sources)

    def body(*refs):
        for cp in _scatter_copies(refs[:n], refs[n:2 * n], refs[2 * n], refs[2 * n + 1]):
            cp.wait_send()
            cp.wait_recv()

    out = pl.pallas_call(
        body, name=f"grads_chip_scatter_wait_{tag}",
        out_shape=[pltpu.HBM(a.shape, a.dtype) for a in list(sources) + list(lands)],
        in_specs=[HBM_SPEC] * (2 * n) + [SEM_SPEC, SEM_SPEC] + [ANY] * len(after), out_specs=[HBM_SPEC] * (2 * n),
        input_output_aliases={k: k for k in range(2 * n)},
        compiler_params=pltpu.CompilerParams(has_side_effects=IN_FLIGHT),
    )(*sources, *lands, send_sems, recv_sems, *after)
    return out[n:]


def _pair_share(finals, tag, after):
    n = len(finals)

    def body(*refs):
        ins, outs = refs[:n], refs[n + 1:2 * n + 1]
        send_sems, recv_sems = refs[2 * n + 1:]
        x, y, c, _ = _place()
        copies = [pltpu.make_async_remote_copy(
            src_ref=ins[k], dst_ref=outs[k], send_sem=send_sems.at[k], recv_sem=recv_sems.at[k],
            device_id=(x, y, 1 - c), device_id_type=MESH) for k in range(n)]
        for cp in copies:
            cp.start()
        for cp in copies:
            cp.wait()

    return pl.pallas_call(
        body, name=f"grads_pair_share_{tag}",
        in_specs=[ANY] * (n + 1), out_specs=[ANY] * n,
        out_shape=[jax.ShapeDtypeStruct(fv.shape, F32) for fv in finals],
        scratch_shapes=[pltpu.SemaphoreType.DMA((n,)), pltpu.SemaphoreType.DMA((n,))],
    )(*finals, after)


def _small_all_reduce(buf):
    rows, c_ = buf.shape

    def body(in_ref, out_ref, pair_buf, slots, send_sems, recv_sems):
        x, y, c, chips = _place()
        mine = 2 * x + y
        pair = pltpu.make_async_remote_copy(
            src_ref=in_ref, dst_ref=pair_buf, send_sem=send_sems.at[0], recv_sem=recv_sems.at[0],
            device_id=(x, y, 1 - c), device_id_type=MESH)
        pair.start()
        pair.wait()
        slots[mine] = in_ref[...] + pair_buf[...]
        sends = [pltpu.make_async_remote_copy(
            src_ref=slots.at[mine], dst_ref=slots.at[mine], send_sem=send_sems.at[1 + j], recv_sem=recv_sems.at[1 + j],
            device_id=(tx, ty, c), device_id_type=MESH) for j, (tx, ty) in enumerate(chips)]
        for cp in sends:
            cp.start()
        for j, (tx, ty) in enumerate(chips):
            pltpu.make_async_remote_copy(
                src_ref=slots.at[mine], dst_ref=slots.at[2 * tx + ty], send_sem=send_sems.at[1 + j],
                recv_sem=recv_sems.at[1 + j], device_id=(tx, ty, c), device_id_type=MESH).wait()
        out_ref[...] = ((slots[0] + slots[1]) + slots[2]) + slots[3]

    vm = pl.BlockSpec(memory_space=pltpu.VMEM)
    return pl.pallas_call(
        body, name="small_all_reduce", in_specs=[vm], out_specs=vm,
        out_shape=jax.ShapeDtypeStruct((rows, c_), F32),
        scratch_shapes=[pltpu.VMEM((rows, c_), F32), pltpu.VMEM((4, rows, c_), F32),
                        pltpu.SemaphoreType.DMA((4,)), pltpu.SemaphoreType.DMA((4,))],
        compiler_params=pltpu.CompilerParams(vmem_limit_bytes=VMEM_LIMIT_V7X),
    )(buf)


def _pair_reduce(grads, core, tag):
    recv = _pair_exchange(grads, tag)
    return [_pair_sum(g, r, core, f"pair_sum_{tag}_{k}") for k, (g, r) in enumerate(zip(grads, recv))]


def _chip_reduce(halves, recv, chip, tag):
    return [_chip_sum(hv, r, chip, f"chip_sum_{tag}_{k}") for k, ((hv, _), r) in enumerate(zip(halves, recv))]


PACK_COLS = 1024


def _pack(arrays):
    flat = jnp.concatenate([a.reshape(-1).astype(F32) for a in arrays])
    rows = -(-flat.shape[0] // PACK_COLS)
    rows = -(-rows // 8) * 8
    return jnp.pad(flat, (0, rows * PACK_COLS - flat.shape[0])).reshape(rows, PACK_COLS)


def _unpack(buf, shapes):
    flat = buf.reshape(-1)
    out, at = [], 0
    for shp in shapes:
        size = math.prod(shp)
        out.append(flat[at:at + size].reshape(shp))
        at += size
    return out


def kernel(x, mix_norm_g, ffn_norm_g, gm_w_in, gm_ln_g, gm_ln_b, gm_w_s, gm_b_s, gm_w_out, fox_w_qkvf, fox_b_f, fox_w_o, ffn_w_gate, ffn_w_up, ffn_conv_w, ffn_conv_b, ffn_w_down, final_norm_g, loss_target, m_mix_norm_g, m_ffn_norm_g, m_gm_w_in, m_gm_ln_g, m_gm_ln_b, m_gm_w_s, m_gm_b_s, m_gm_w_out, m_fox_w_qkvf, m_fox_b_f, m_fox_w_o, m_ffn_w_gate, m_ffn_w_up, m_ffn_conv_w, m_ffn_conv_b, m_ffn_w_down, m_final_norm_g, v_mix_norm_g, v_ffn_norm_g, v_gm_w_in, v_gm_ln_g, v_gm_ln_b, v_gm_w_s, v_gm_b_s, v_gm_w_out, v_fox_w_qkvf, v_fox_b_f, v_fox_w_o, v_ffn_w_gate, v_ffn_w_up, v_ffn_conv_w, v_ffn_conv_b, v_ffn_w_down, v_final_norm_g):
    _, s_len, d = x.shape
    e = gm_ln_g.shape[1]
    f = ffn_conv_b.shape[1]
    n_head = fox_b_f.shape[1]
    n_pair = n_head // 2
    gd = e // GM_GROUPS
    qkvf_cols = fox_w_qkvf.shape[2]
    assert d == n_head * HEAD_DIM and d % (2 * LANES) == 0 and s_len % 512 == 0 and gd % LANES == 0
    assert gm_w_s.shape[2] == CHUNK and 4 * qkvf_cols == 3 * d + n_head
    tm = 256
    h0 = x[0]
    target = loss_target[0]

    w_in, w_out4 = _gather_pair(_gather_ici([gm_w_in[0].astype(BF16), gm_w_out[0].astype(BF16)]))
    send_sems, recv_sems, sources, lands, token = _gather_ici_start([
        fox_w_qkvf[0].astype(BF16), fox_w_o[0].astype(BF16), ffn_w_gate.astype(BF16), ffn_w_up.astype(BF16),
        ffn_w_down.astype(BF16), ffn_conv_w], after=w_in)
    w_out = w_out4.reshape(e, d)
    bf_pad = jnp.pad(fox_b_f, ((0, 0), (0, LANES - n_head)))

    tril = jnp.tril(jnp.ones((CHUNK, CHUNK), bool))
    wc = jnp.where(tril[None], gm_w_s[0], 0.0).astype(BF16)
    wct = jnp.transpose(wc, (0, 2, 1))
    bias = jnp.repeat(gm_b_s[0].T, gd, axis=1)
    seg_groups = (jnp.arange(e)[:, None] // gd == jnp.arange(LANES)[None, :]).astype(BF16)
    seg_heads = (jnp.arange(d)[:, None] // HEAD_DIM == jnp.arange(LANES)[None, :]).astype(BF16)
    sel_q = _spare_selectors(d, key_side=False)
    sel_k = _spare_selectors(d, key_side=True)

    h1, z0, dz0, hn0, gated0 = _gmlp_fwd(h0, mix_norm_g[0:1] + token[0:1, 0:1], w_in, gm_ln_g, gm_ln_b, wc, bias, w_out, tm)
    qkvf4, wo4, wg_all, wu_all, wd_all, cw4 = _gather_pair(_gather_ici_wait(send_sems, recv_sems, sources, lands, h1))
    qkvf = jnp.transpose(qkvf4, (1, 0, 2)).reshape(d, 4 * qkvf_cols)
    wq, wk, wv = qkvf[:, :d], qkvf[:, d:2 * d], qkvf[:, 2 * d:3 * d]
    wf = jnp.pad(qkvf[:, 3 * d:], ((0, 0), (0, LANES - n_head)))
    wo = wo4.reshape(d, d)
    conv_w_full = jnp.transpose(cw4, (1, 2, 0, 3)).reshape(2, 3, f)
    conv_w8 = jnp.pad(conv_w_full, ((0, 0), (0, 5), (0, 0)))
    h2, fa0, fup0, fhn0, fhid0 = _ffn_fwd(h1, ffn_norm_g[0:1], wg_all, wu_all, wd_all, 0, conv_w8[0], ffn_conv_b[0:1], tm)
    (hn1, qa, qb, kat, kbt, va, vb, vat, vbt, z_f) = _fox_proj_fwd(
        h2, mix_norm_g[1:2], wq, wk, wv, wf, bf_pad, sel_q, sel_k, tm)
    o, qa2, qb2, qat2, qbt2 = _flash_fwd(qa, qb, kat, kbt, va, vb)
    h3 = _oproj_fwd(h2, o, wo, tm)
    h4, fa1, fup1, fhn1, fhid1 = _ffn_fwd(h3, ffn_norm_g[1:2], wg_all, wu_all, wd_all, 1, conv_w8[1], ffn_conv_b[1:2], tm)

    dh4, loss_part, g_final = _loss_head(h4, target, final_norm_g.reshape(1, d), tm)
    dh3, da1, dup1, gcw1, gcb1, gfn1 = _ffn_bwd(h3, dh4, fa1, fup1, ffn_norm_g[1:2], wg_all, wu_all, wd_all, 1,
                                                conv_w8[1], ffn_conv_b[1:2], tm)
    g_gate1 = _wgrad(fhn1, da1, 4, "wgrad_gate_1")
    g_up1 = _wgrad(fhn1, dup1, 4, "wgrad_up_1")
    g_down1 = _wgrad(fhid1, dh4, 1, "wgrad_down_1").reshape(4, f // 4, d)

    doa, dob, doat, dobt = _oproj_bwd(dh3, o, wo, seg_heads, sel_q, tm)
    g_wo = _wgrad(o, dh3, 1, "wgrad_wo").reshape(4, d // 4, d)
    dqt, dkt, dvt, row_sums, col_sums = _flash_bwd(qa2, qb2, qat2, qbt2, kat, kbt, vat, vbt, doa, dob, doat, dobt)
    sums = row_sums[:, 0::8, :] - col_sums[:, N_SPARE::8, :]
    dcum = jnp.pad(sums.reshape(n_head, s_len).T, ((0, 0), (0, LANES - n_head)))
    dfl, g_bf = _forget_bwd(dcum, z_f, tm)
    dh2, dflb, gmn1 = _fox_proj_bwd(h2, dh3, dqt, dkt, dvt, dfl, mix_norm_g[1:2], wq, wk, wv, wf, tm)
    g_q = _wgrad_t(dqt, hn1, "wgrad_q").T
    g_k = _wgrad_t(dkt, hn1, "wgrad_k").T
    g_v = _wgrad_t(dvt, hn1, "wgrad_v").T
    g_f = _wgrad(hn1, dflb, 1, "wgrad_f")[0][:, :n_head]
    g_qkvf = jnp.concatenate([g_q, g_k, g_v, g_f], axis=1).reshape(d, 4, qkvf_cols).transpose(1, 0, 2)

    core = lax.axis_index("c").astype(jnp.int32).reshape(1)
    chip = (2 * lax.axis_index("x") + lax.axis_index("y")).astype(jnp.int32).reshape(1)
    halves_early = _pair_reduce([g_qkvf, g_wo, g_gate1, g_up1, g_down1], core, "early")
    sc_send, sc_recv, sc_src, sc_land, sc_token = _chip_scatter_start([hb for _, hb in halves_early], "early")

    dh1, da0f, dup0, gcw0, gcb0, gfn0 = _ffn_bwd(h1, dh2, fa0, fup0, ffn_norm_g[0:1] + sc_token[0:1, 0:1],
                                                 wg_all, wu_all, wd_all, 0, conv_w8[0], ffn_conv_b[0:1], tm)
    g_gate0 = _wgrad(fhn0, da0f, 4, "wgrad_gate_0")
    g_up0 = _wgrad(fhn0, dup0, 4, "wgrad_up_0")
    g_down0 = _wgrad(fhid0, dh2, 1, "wgrad_down_0").reshape(4, f // 4, d)

    dh0, da0, g_ws, g_bs_t, g_lng, g_lnb, gmn0 = _gmlp_bwd(
        h0, dh1, z0, dz0, mix_norm_g[0:1], w_in, gm_ln_g, gm_ln_b, wc, wct, bias, w_out, seg_groups, tm)
    g_win = _wgrad(hn0, da0, 4, "wgrad_gm_in")
    g_wout = _wgrad(gated0, dh1, 1, "wgrad_gm_out").reshape(4, e // 4, d)

    small = [jnp.concatenate([gmn0, gmn1]), jnp.concatenate([gfn0, gfn1]), g_lng, g_lnb, g_ws[None],
             g_bs_t[:, :GM_GROUPS].T[None], g_bf[:, :n_head], jnp.stack([gcw0[:3], gcw1[:3]]),
             jnp.concatenate([gcb0, gcb1]), g_final.reshape(d), loss_part[0, :1]]
    small_shapes = [a.shape for a in small]
    small_sum = _small_all_reduce(_pack(small))
    reduced = _unpack(small_sum, small_shapes)
    (r_mix, r_ffn, r_lng, r_lnb, r_ws, r_bs, r_bf, r_cw_full, r_cb, r_final, r_loss) = reduced
    r_cw = lax.dynamic_slice_in_dim(r_cw_full, chip[0] * (f // 4), f // 4, axis=2)

    halves_late = _pair_reduce([g_win, g_wout, g_gate0, g_up0, g_down0], core, "late")
    lt_send, lt_recv, lt_src, lt_land, lt_token = _chip_scatter_start([hb for _, hb in halves_late], "late", [small_sum])
    recv_early = _chip_scatter_wait(sc_send, sc_recv, sc_src, sc_land, [halves_late[0][0]], "early")
    finals_early = _chip_reduce(halves_early, recv_early, chip, "early")
    r_qkvf, r_wo, r_gate1, r_up1, r_down1 = zip(finals_early, _pair_share(finals_early, "early", lt_token))

    def update(name, layer, w, m, v, grad, core_):
        return _adamw_halves(w[layer], m[layer], v[layer], grad[0], grad[1], core_, f"adamw_{name}_{layer}")

    u_qkvf = update("fox_w_qkvf", 0, fox_w_qkvf, m_fox_w_qkvf, v_fox_w_qkvf, r_qkvf, core)
    u_wo = update("fox_w_o", 0, fox_w_o, m_fox_w_o, v_fox_w_o, r_wo, core)
    u_gate1 = update("ffn_w_gate", 1, ffn_w_gate, m_ffn_w_gate, v_ffn_w_gate, r_gate1, core)
    u_up1 = update("ffn_w_up", 1, ffn_w_up, m_ffn_w_up, v_ffn_w_up, r_up1, core)
    u_down1 = update("ffn_w_down", 1, ffn_w_down, m_ffn_w_down, v_ffn_w_down, r_down1, core)

    res = {}
    small_names = ["mix_norm_g", "ffn_norm_g", "gm_ln_g", "gm_ln_b", "gm_w_s", "gm_b_s", "fox_b_f", "ffn_conv_w",
                   "ffn_conv_b", "final_norm_g"]
    small_w = [mix_norm_g, ffn_norm_g, gm_ln_g, gm_ln_b, gm_w_s, gm_b_s, fox_b_f, ffn_conv_w, ffn_conv_b, final_norm_g]
    small_m = [m_mix_norm_g, m_ffn_norm_g, m_gm_ln_g, m_gm_ln_b, m_gm_w_s, m_gm_b_s, m_fox_b_f, m_ffn_conv_w,
               m_ffn_conv_b, m_final_norm_g]
    small_v = [v_mix_norm_g, v_ffn_norm_g, v_gm_ln_g, v_gm_ln_b, v_gm_w_s, v_gm_b_s, v_fox_b_f, v_ffn_conv_w,
               v_ffn_conv_b, v_final_norm_g]
    small_g = [r_mix, r_ffn, r_lng, r_lnb, r_ws, r_bs, r_bf, r_cw, r_cb, r_final]
    shapes = [w.shape for w in small_w]
    small_g = [g.reshape(s) for g, s in zip(small_g, shapes)]
    dlt, mn, vn = _adamw(_pack(small_w), _pack(small_m), _pack(small_v), _pack(small_g), "adamw_small")
    for name, g, dl_, m_, v_ in zip(small_names, small_g, _unpack(dlt, shapes), _unpack(mn, shapes), _unpack(vn, shapes)):
        res[name] = (g, dl_, m_, v_)

    done_meanwhile = [u[1] for u in (u_qkvf, u_wo, u_gate1, u_up1, u_down1)] + [dlt]
    recv_late = _chip_scatter_wait(lt_send, lt_recv, lt_src, lt_land, done_meanwhile, "late")
    finals_late = _chip_reduce(halves_late, recv_late, chip, "late")
    r_win, r_wout, r_gate0, r_up0, r_down0 = zip(finals_late, _pair_share(finals_late, "late", lt_token))
    u_gate0 = update("ffn_w_gate", 0, ffn_w_gate, m_ffn_w_gate, v_ffn_w_gate, r_gate0, core)
    u_up0 = update("ffn_w_up", 0, ffn_w_up, m_ffn_w_up, v_ffn_w_up, r_up0, core)
    u_down0 = update("ffn_w_down", 0, ffn_w_down, m_ffn_w_down, v_ffn_w_down, r_down0, core)
    layers = {"gm_w_in": [update("gm_w_in", 0, gm_w_in, m_gm_w_in, v_gm_w_in, r_win, core)],
              "gm_w_out": [update("gm_w_out", 0, gm_w_out, m_gm_w_out, v_gm_w_out, r_wout, core)],
              "fox_w_qkvf": [u_qkvf], "fox_w_o": [u_wo], "ffn_w_gate": [u_gate0, u_gate1],
              "ffn_w_up": [u_up0, u_up1], "ffn_w_down": [u_down0, u_down1]}
    for name, parts in layers.items():
        res[name] = tuple(jnp.stack([p[i] for p in parts]) for i in range(4))

    order = ["mix_norm_g", "ffn_norm_g", "gm_w_in", "gm_ln_g", "gm_ln_b", "gm_w_s", "gm_b_s", "gm_w_out", "fox_w_qkvf",
             "fox_b_f", "fox_w_o", "ffn_w_gate", "ffn_w_up", "ffn_conv_w", "ffn_conv_b", "ffn_w_down", "final_norm_g"]
    outs = [r_loss.reshape(()), dh0[None]]
    for part in range(4):
        outs += [res[name][part] for name in order]
    return tuple(outs)
```
